```python
import jax, jax.numpy as jnp
from jax import lax
import numpy as np

D_MODEL = 1024
BATCH = 8
SEQ = 4096
DEPTH = 2

GRID_W = 64
CTX_LEN = 256
N_MIXERS = 2
N_RET = (DEPTH + 1) // 2
N_MLA = DEPTH // 2
EPS = 1e-6
ROPE_BASE = 10000.0
RET_HEADS = 4
RET_DK = D_MODEL // RET_HEADS
RET_DV = 2 * RET_DK
RET_VW = RET_HEADS * RET_DV
RET_CHUNK = 128
RET_IN = 2 * D_MODEL + 3 * RET_VW
MLA_HEADS = 8
MLA_NOPE = 128
MLA_ROPE = 64
MLA_QK = MLA_NOPE + MLA_ROPE
MLA_V = 128
MLA_Q_RANK = 384
MLA_KV_RANK = 256
MLA_IN = MLA_Q_RANK + MLA_KV_RANK + MLA_ROPE
ATTN_BLOCK = 128
N_EXPERTS = 64
TOP_K = 8
N_GROUPS = 8
TOPK_GROUPS = 4
EXPERT_FF = 256
SHARED_FF = 256
ROUTED_SCALE = 2.5

kernel_name = 'hybrid_retention_mla_moe_dit'


def rms_scale(x):
    xf = x.astype(jnp.float32)
    return (xf * lax.rsqrt(jnp.mean(xf * xf, axis=-1, keepdims=True) + EPS)).astype(x.dtype)


def rmsnorm(x, g):
    return rms_scale(x) * g


def modulate(h, shift, scale):
    return h * (1 + scale) + shift


def axial_angles(rows_n, rot_dim):
    axis_dim = rot_dim // 2
    inv = ROPE_BASE ** (-jnp.arange(0, axis_dim, 2, dtype=jnp.float32) / axis_dim)
    row = jnp.repeat(jnp.arange(rows_n, dtype=jnp.float32), GRID_W)
    col = jnp.tile(jnp.arange(GRID_W, dtype=jnp.float32), rows_n)
    return row[:, None] * inv, col[:, None] * inv


def rotate_half(x, ang):
    x1, x2 = jnp.split(x, 2, axis=-1)
    cos = jnp.cos(ang)[None, :, None, :].astype(x.dtype)
    sin = jnp.sin(ang)[None, :, None, :].astype(x.dtype)
    return jnp.concatenate([x1 * cos - x2 * sin, x2 * cos + x1 * sin], axis=-1)


def axial_rope(x, ang_r, ang_c):
    xr, xc = jnp.split(x, 2, axis=-1)
    return jnp.concatenate([rotate_half(xr, ang_r), rotate_half(xc, ang_c)], axis=-1)


def retention_scan(q, k, v, log_gamma, s0):
    b, L, H, _ = q.shape
    dv = v.shape[-1]
    nc = L // RET_CHUNK

    def chunks(t):
        return t.reshape(b, nc, RET_CHUNK, H, t.shape[-1]).transpose(1, 0, 2, 3, 4)

    idx = jnp.arange(RET_CHUNK, dtype=jnp.float32)
    rel = idx[:, None] - idx[None, :]
    lg = log_gamma.astype(jnp.float32)
    intra = jnp.where(rel[None] >= 0, jnp.exp(lg[:, None, None] * jnp.maximum(rel, 0.0)[None]), 0.0).astype(q.dtype)
    dec_q = jnp.exp(lg[:, None] * (idx + 1)[None]).T[None, :, :, None].astype(q.dtype)
    dec_k = jnp.exp(lg[:, None] * (RET_CHUNK - 1 - idx)[None]).T[None, :, :, None].astype(q.dtype)
    dec_chunk = jnp.exp(lg * RET_CHUNK)[None, :, None, None].astype(q.dtype)

    def step(S, qkv):
        qc, kc, vc = qkv
        scores = jnp.einsum('bihd,bjhd->bhij', qc, kc) * intra
        y = jnp.einsum('bhij,bjhv->bihv', scores, vc) + jnp.einsum('bihd,bhdv->bihv', qc, S) * dec_q
        S = S * dec_chunk + jnp.einsum('bjhd,bjhv->bhdv', kc * dec_k, vc)
        return S, y

    S, ys = lax.scan(step, s0, (chunks(q), chunks(k), chunks(v)))
    return ys.transpose(1, 0, 2, 3, 4).reshape(b, L, H, dv), S


def bidir_retention(q, k, v, lg_f, lg_b, s0_f, s0_b):
    y_f, s_f = retention_scan(q, k, v, lg_f, s0_f)
    y_b, s_b = retention_scan(q[:, ::-1], k[:, ::-1], v[:, ::-1], lg_b, s0_b)
    return y_f, y_b[:, ::-1], s_f, s_b


def retention_mixer(h_lat, h_ctx, w_in, decay_f, decay_b, w_o, ang_r, ang_c, need_ctx_out):
    lg_f = -jnp.exp(decay_f.astype(jnp.float32))
    lg_b = -jnp.exp(decay_b.astype(jnp.float32))

    def project(h, rotate):
        b, L, _ = h.shape
        q, k, v, g_f, g_b = jnp.split(h @ w_in, [D_MODEL, 2 * D_MODEL, 2 * D_MODEL + RET_VW, 2 * D_MODEL + 2 * RET_VW], axis=-1)
        q = q.reshape(b, L, RET_HEADS, RET_DK)
        k = k.reshape(b, L, RET_HEADS, RET_DK)
        v = v.reshape(b, L, RET_HEADS, RET_DV)
        if rotate:
            q = axial_rope(q, ang_r, ang_c)
            k = axial_rope(k, ang_r, ang_c)
        return q, k * RET_DK ** -0.5, v, g_f, g_b

    def readout(y_f, y_b, g_f, g_b):
        b, L = g_f.shape[:2]
        o = jax.nn.silu(g_f) * rms_scale(y_f).reshape(b, L, RET_VW) + jax.nn.silu(g_b) * rms_scale(y_b).reshape(b, L, RET_VW)
        return o @ w_o

    qc, kc, vc, gfc, gbc = project(h_ctx, False)
    s0 = jnp.zeros((h_ctx.shape[0], RET_HEADS, RET_DK, RET_DV), h_ctx.dtype)
    yc_f, yc_b, sc_f, sc_b = bidir_retention(qc, kc, vc, lg_f, lg_b, s0, s0)
    ql, kl, vl, gfl, gbl = project(h_lat, True)
    yl_f, yl_b, _, _ = bidir_retention(ql, kl, vl, lg_f, lg_b, sc_f, sc_b)
    out_lat = readout(yl_f, yl_b, gfl, gbl)
    out_ctx = readout(yc_f, yc_b, gfc, gbc) if need_ctx_out else None
    return out_lat, out_ctx


def attend(q, k, v):
    s = jnp.einsum('bqhd,bkhd->bhqk', q, k).astype(jnp.float32)
    p = jax.nn.softmax(s, axis=-1).astype(v.dtype)
    return jnp.einsum('bhqk,bkhd->bqhd', p, v)


def mla_mixer(h_lat, h_ctx, w_in, q_a_norm, w_q_b, kv_a_norm, w_kv_b, q_norm, k_norm, w_o, ang_r, ang_c, need_ctx_out):
    def queries(a, rotate):
        b, L, _ = a.shape
        q = (rmsnorm(a[..., :MLA_Q_RANK], q_a_norm) @ w_q_b).reshape(b, L, MLA_HEADS, MLA_QK)
        q_nope = rmsnorm(q[..., :MLA_NOPE], q_norm[:MLA_NOPE])
        q_rope = rmsnorm(q[..., MLA_NOPE:], q_norm[MLA_NOPE:])
        if rotate:
            q_rope = axial_rope(q_rope, ang_r, ang_c)
        return jnp.concatenate([q_nope, q_rope], axis=-1) * MLA_QK ** -0.5

    def keys_values(a, rotate):
        b, L, _ = a.shape
        kv_lat = a[..., MLA_Q_RANK:MLA_Q_RANK + MLA_KV_RANK]
        k_rope = a[..., MLA_Q_RANK + MLA_KV_RANK:][:, :, None, :]
        kv = (rmsnorm(kv_lat, kv_a_norm) @ w_kv_b).reshape(b, L, MLA_HEADS, MLA_NOPE + MLA_V)
        k_nope = rmsnorm(kv[..., :MLA_NOPE], k_norm[:MLA_NOPE])
        k_rope = rmsnorm(k_rope, k_norm[MLA_NOPE:])
        if rotate:
            k_rope = axial_rope(k_rope, ang_r, ang_c)
        k = jnp.concatenate([k_nope, jnp.broadcast_to(k_rope, (b, L, MLA_HEADS, MLA_ROPE))], axis=-1)
        return k, kv[..., MLA_NOPE:]

    a_c = h_ctx @ w_in
    a_l = h_lat @ w_in
    kc, vc = keys_values(a_c, False)
    kl, vl = keys_values(a_l, True)
    ql = queries(a_l, True)
    k_all = jnp.concatenate([kl, kc], axis=1)
    v_all = jnp.concatenate([vl, vc], axis=1)
    b, S = h_lat.shape[:2]
    nb = S // ATTN_BLOCK
    q_blocks = ql.reshape(b, nb, ATTN_BLOCK, MLA_HEADS, MLA_QK).transpose(1, 0, 2, 3, 4)
    o = lax.map(lambda qb: attend(qb, k_all, v_all), q_blocks)
    o = o.transpose(1, 0, 2, 3, 4).reshape(b, S, MLA_HEADS * MLA_V)
    out_lat = o @ w_o
    if need_ctx_out:
        qc = queries(a_c, False)
        out_ctx = attend(qc, kc, vc).reshape(h_ctx.shape[0], h_ctx.shape[1], MLA_HEADS * MLA_V) @ w_o
    else:
        out_ctx = None
    return out_lat, out_ctx


def moe_ffn(h, router_w, router_bias, w_gu, w_down, sh_gu, sh_down):
    n = h.shape[0]
    s = jax.nn.sigmoid((h @ router_w).astype(jnp.float32))
    sel = s + router_bias.astype(jnp.float32)
    grp = sel.reshape(n, N_GROUPS, N_EXPERTS // N_GROUPS)
    gscore = lax.top_k(grp, 2)[0].sum(-1)
    _, gidx = lax.top_k(gscore, TOPK_GROUPS)
    gmask = jax.nn.one_hot(gidx, N_GROUPS, dtype=jnp.float32).sum(1) > 0
    emask = jnp.repeat(gmask, N_EXPERTS // N_GROUPS, axis=1)
    _, eidx = lax.top_k(jnp.where(emask, sel, -jnp.inf), TOP_K)
    w = jnp.take_along_axis(s, eidx, axis=1)
    w = w / jnp.sum(w, axis=-1, keepdims=True) * ROUTED_SCALE
    flat = eidx.reshape(-1)
    order = jnp.argsort(flat)
    tok = order // TOP_K
    sizes = jnp.bincount(flat, length=N_EXPERTS).astype(jnp.int32)
    gu = lax.ragged_dot(h[tok], w_gu, sizes)
    g, u = jnp.split(gu, 2, axis=-1)
    y = lax.ragged_dot(jax.nn.silu(g) * u, w_down, sizes)
    y = y[jnp.argsort(order)].reshape(n, TOP_K, h.shape[-1])
    routed = jnp.einsum('nkd,nk->nd', y, w.astype(y.dtype))
    gs, us = jnp.split(h @ sh_gu, 2, axis=-1)
    return routed + (jax.nn.silu(gs) * us) @ sh_down


def setup_inputs(seed: int = 0) -> dict:
    key = jax.random.key(seed)
    ks = jax.random.split(key, 32)
    f32 = jnp.float32

    def nrm(k, shape, scale):
        return jax.random.normal(k, shape, f32) * scale

    def gain(k, shape):
        return 1.0 + 0.05 * jax.random.normal(k, shape, f32)

    hh = jnp.arange(RET_HEADS, dtype=f32)
    base_decay = jnp.log(-jnp.log1p(-(2.0 ** (-5.0 - hh))))
    return {
        'x': nrm(ks[0], (BATCH, SEQ, D_MODEL), 1.0),
        'c': nrm(ks[1], (BATCH, D_MODEL), 1.0),
        'ctx': nrm(ks[2], (BATCH, CTX_LEN, D_MODEL), 1.0),
        'c_ctx': nrm(ks[3], (D_MODEL,), 1.0),
        'ada_w': nrm(ks[4], (DEPTH, D_MODEL, 6 * D_MODEL), 0.5 * D_MODEL ** -0.5),
        'ada_b': nrm(ks[5], (DEPTH, 6 * D_MODEL), 0.02),
        'norm_mix': gain(ks[6], (DEPTH, D_MODEL)),
        'norm_ffn': gain(ks[7], (DEPTH, D_MODEL)),
        'ret_w_in': nrm(ks[8], (N_RET, D_MODEL, RET_IN), D_MODEL ** -0.5),
        'ret_decay_f': base_decay[None] + 0.05 * jax.random.normal(ks[9], (N_RET, RET_HEADS), f32),
        'ret_decay_b': base_decay[None] + 0.05 * jax.random.normal(ks[10], (N_RET, RET_HEADS), f32),
        'ret_w_o': nrm(ks[11], (N_RET, RET_VW, D_MODEL), RET_VW ** -0.5),
        'mla_w_in': nrm(ks[12], (N_MLA, D_MODEL, MLA_IN), D_MODEL ** -0.5),
        'mla_q_a_norm': gain(ks[13], (N_MLA, MLA_Q_RANK)),
        'mla_w_q_b': nrm(ks[14], (N_MLA, MLA_Q_RANK, MLA_HEADS * MLA_QK), MLA_Q_RANK ** -0.5),
        'mla_kv_a_norm': gain(ks[15], (N_MLA, MLA_KV_RANK)),
        'mla_w_kv_b': nrm(ks[16], (N_MLA, MLA_KV_RANK, MLA_HEADS * (MLA_NOPE + MLA_V)), MLA_KV_RANK ** -0.5),
        'mla_q_norm': gain(ks[17], (N_MLA, MLA_QK)),
        'mla_k_norm': gain(ks[18], (N_MLA, MLA_QK)),
        'mla_w_o': nrm(ks[19], (N_MLA, MLA_HEADS * MLA_V, D_MODEL), (MLA_HEADS * MLA_V) ** -0.5),
        'router_w': nrm(ks[20], (DEPTH, D_MODEL, N_EXPERTS), D_MODEL ** -0.5),
        'router_bias': nrm(ks[21], (DEPTH, N_EXPERTS), 0.01),
        'exp_w_gu': nrm(ks[22], (DEPTH, N_EXPERTS, D_MODEL, 2 * EXPERT_FF), D_MODEL ** -0.5),
        'exp_w_down': nrm(ks[23], (DEPTH, N_EXPERTS, EXPERT_FF, D_MODEL), EXPERT_FF ** -0.5),
        'sh_w_gu': nrm(ks[24], (DEPTH, D_MODEL, 2 * SHARED_FF), D_MODEL ** -0.5),
        'sh_w_down': nrm(ks[25], (DEPTH, SHARED_FF, D_MODEL), SHARED_FF ** -0.5),
    }


def reference(x, c, ctx, c_ctx, ada_w, ada_b, norm_mix, norm_ffn, ret_w_in, ret_decay_f, ret_decay_b, ret_w_o,
              mla_w_in, mla_q_a_norm, mla_w_q_b, mla_kv_a_norm, mla_w_kv_b, mla_q_norm, mla_k_norm, mla_w_o,
              router_w, router_bias, exp_w_gu, exp_w_down, sh_w_gu, sh_w_down):
    ROWS = x.shape[1] // GRID_W
    ret_r, ret_c = axial_angles(ROWS, RET_DK)
    mla_r, mla_c = axial_angles(ROWS, MLA_ROPE)
    b = x.shape[0]
    n_ctx = ctx.shape[1]
    for i in range(DEPTH):
        need_ctx = i < DEPTH - 1
        j = i // N_MIXERS
        mod_l = (jax.nn.silu(c) @ ada_w[i] + ada_b[i])[:, None, :]
        mod_c = (jax.nn.silu(c_ctx) @ ada_w[i] + ada_b[i])[None, None, :]
        sh_ml, sc_ml, g_ml, sh_fl, sc_fl, g_fl = jnp.split(mod_l, 6, axis=-1)
        sh_mc, sc_mc, g_mc, sh_fc, sc_fc, g_fc = jnp.split(mod_c, 6, axis=-1)
        h_l = modulate(rmsnorm(x, norm_mix[i]), sh_ml, sc_ml)
        h_c = modulate(rmsnorm(ctx, norm_mix[i]), sh_mc, sc_mc)
        if i % N_MIXERS == 0:
            out_l, out_c = retention_mixer(h_l, h_c, ret_w_in[j], ret_decay_f[j], ret_decay_b[j], ret_w_o[j],
                                           ret_r, ret_c, need_ctx)
        else:
            out_l, out_c = mla_mixer(h_l, h_c, mla_w_in[j], mla_q_a_norm[j], mla_w_q_b[j], mla_kv_a_norm[j],
                                     mla_w_kv_b[j], mla_q_norm[j], mla_k_norm[j], mla_w_o[j], mla_r, mla_c, need_ctx)
        x = x + g_ml * out_l
        moe = lambda t: moe_ffn(t, router_w[i], router_bias[i], exp_w_gu[i], exp_w_down[i], sh_w_gu[i], sh_w_down[i])
        f_l_in = modulate(rmsnorm(x, norm_ffn[i]), sh_fl, sc_fl)
        if need_ctx:
            ctx = ctx + g_mc * out_c
            f_c_in = modulate(rmsnorm(ctx, norm_ffn[i]), sh_fc, sc_fc)
            f = lax.map(moe, jnp.concatenate([f_c_in, f_l_in], axis=1))
            ctx = ctx + g_fc * f[:, :n_ctx]
            x = x + g_fl * f[:, n_ctx:]
        else:
            x = x + g_fl * lax.map(moe, f_l_in)
    return x
```

```python
import functools

import jax
import jax.numpy as jnp
import numpy as np
from jax import lax
from jax.experimental import pallas as pl
from jax.experimental.pallas import tpu as pltpu

f32 = jnp.float32
bf16 = jnp.bfloat16
i32 = jnp.int32

D = 1024
GRID_W = 64
EPS = 1e-6
ROPE_BASE = 10000.0
RET_H = 4
RET_DK = 256
RET_DV = 512
RET_VW = RET_H * RET_DV
RET_CHUNK = 128
MLA_H = 8
MLA_NOPE = 128
MLA_ROPE = 64
MLA_QK = MLA_NOPE + MLA_ROPE
MLA_V = 128
MLA_QR = 384
MLA_KVR = 256
MLA_IN_PAD = 768
N_EXP = 64
TOP_K = 8
N_GRP = 8
TOPK_GRP = 4
EXP_FF = 256
SH_FF = 256
ROUTED_SCALE = 2.5

LANES = 128
SUBLANES = 8
ROW_TILE = 256
MOE_TILE = 256
VMEM_LIMIT = 56 * 1024 * 1024


def _cparams(sem, vmem=VMEM_LIMIT):
    return pltpu.CompilerParams(dimension_semantics=sem, vmem_limit_bytes=vmem)


def _sigmoid(x):
    return 1.0 / (1.0 + jnp.exp(-x))


def _silu(x):
    return x * _sigmoid(x)


def _rms(x, n=None):
    n = x.shape[-1] if n is None else n
    return x * lax.rsqrt(jnp.sum(x * x, axis=-1, keepdims=True) * (1.0 / n) + EPS)


def _dot(a, b):
    return jnp.dot(a, b, preferred_element_type=f32)


def _dot_nt(a, b, precision=None):
    return lax.dot_general(a, b, (((1,), (1,)), ((), ())), preferred_element_type=f32, precision=precision)


def _ada_kernel(c_ref, w_ref, b_ref, o_ref):
    s = _silu(c_ref[...]).astype(bf16)
    o_ref[...] = _dot(s, w_ref[...].astype(bf16)) + b_ref[...]


def _ada(cc, ada_w, ada_b):
    depth = ada_w.shape[0]
    rows = cc.shape[0]
    tn = 1536
    return pl.pallas_call(
        _ada_kernel,
        grid=(depth, 6 * D // tn),
        in_specs=[pl.BlockSpec((rows, D), lambda i, j: (0, 0)),
                  pl.BlockSpec((None, D, tn), lambda i, j: (i, 0, j)),
                  pl.BlockSpec((None, 1, tn), lambda i, j: (i, 0, j))],
        out_specs=pl.BlockSpec((None, rows, tn), lambda i, j: (i, 0, j)),
        out_shape=jax.ShapeDtypeStruct((depth, rows, 6 * D), f32),
        compiler_params=_cparams(("arbitrary", "arbitrary")),
        name="ada",
    )(cc, ada_w, ada_b.reshape(depth, 1, 6 * D))


def _ret_inproj_kernel(x_ref, mod_ref, g_ref, w_ref, cos_ref, sin_ref, q_ref, k_ref, v_ref, gf_ref, gb_ref):
    x = x_ref[...]
    mod = mod_ref[...]
    h = (_rms(x) * g_ref[...]) * (1.0 + mod[:, D:2 * D]) + mod[:, 0:D]
    hb = h.astype(bf16)
    cos = cos_ref[...]
    sin = sin_ref[...]

    def rope(a):
        outs = []
        for half in range(2):
            sl = slice(half * LANES, (half + 1) * LANES)
            ah = a[:, sl]
            outs.append(ah * cos[:, sl] + pltpu.roll(ah, LANES // 2, axis=1) * sin[:, sl])
        return jnp.concatenate(outs, axis=1)

    for hd in range(RET_H):
        sl = slice(hd * RET_DK, (hd + 1) * RET_DK)
        q_ref[:, sl] = rope(_dot(hb, w_ref[:, sl])).astype(bf16)
    for hd in range(RET_H):
        sl = slice(hd * RET_DK, (hd + 1) * RET_DK)
        wsl = slice(D + hd * RET_DK, D + (hd + 1) * RET_DK)
        k_ref[:, sl] = (rope(_dot(hb, w_ref[:, wsl])) * (RET_DK ** -0.5)).astype(bf16)
    cw = 512
    for c in range(RET_VW // cw):
        sl = slice(c * cw, (c + 1) * cw)
        v_ref[:, sl] = _dot(hb, w_ref[:, 2 * D + c * cw:2 * D + (c + 1) * cw]).astype(bf16)
        gf_ref[:, sl] = _silu(_dot(hb, w_ref[:, 2 * D + RET_VW + c * cw:2 * D + RET_VW + (c + 1) * cw])).astype(bf16)
        gb_ref[:, sl] = _silu(_dot(hb, w_ref[:, 2 * D + 2 * RET_VW + c * cw:2 * D + 2 * RET_VW + (c + 1) * cw])).astype(bf16)


def _ret_inproj(xc, modtab, gain, w_in, cos_t, sin_t, n_ctx_tiles):
    B, T, _ = xc.shape
    tm = ROW_TILE
    n_in = w_in.shape[1]
    tok = lambda w: pl.BlockSpec((None, tm, w), lambda b, i: (b, i, 0))
    return pl.pallas_call(
        _ret_inproj_kernel,
        grid=(B, T // tm),
        in_specs=[tok(D),
                  pl.BlockSpec((None, None, 1, 6 * D), lambda b, i: (b, jnp.where(i < n_ctx_tiles, 0, 1), 0, 0)),
                  pl.BlockSpec((1, D), lambda b, i: (0, 0)),
                  pl.BlockSpec((D, n_in), lambda b, i: (0, 0), pipeline_mode=pl.Buffered(1)),
                  pl.BlockSpec((tm, RET_DK), lambda b, i: (i, 0)),
                  pl.BlockSpec((tm, RET_DK), lambda b, i: (i, 0))],
        out_specs=[tok(D), tok(D), tok(RET_VW), tok(RET_VW), tok(RET_VW)],
        out_shape=[jax.ShapeDtypeStruct((B, T, D), bf16), jax.ShapeDtypeStruct((B, T, D), bf16),
                   jax.ShapeDtypeStruct((B, T, RET_VW), bf16), jax.ShapeDtypeStruct((B, T, RET_VW), bf16),
                   jax.ShapeDtypeStruct((B, T, RET_VW), bf16)],
        compiler_params=_cparams(("arbitrary", "arbitrary")),
        name="ret_inproj",
    )(xc, modtab, gain, w_in, cos_t, sin_t)


def _ret_chunk_index(t, nc, ncc):
    u = t - nc
    back = jnp.where(u < ncc, ncc - 1 - u, nc - 1 - u + ncc)
    return jnp.where(t < nc, t, back)


def _ret_scan_kernel(dt_ref, q_ref, k_ref, v_ref, gf_ref, gb_ref, o_ref,
                     s_ref, of_ref, mask_ref, dq_ref, dk_ref, dc_ref, *, nc, ncc):
    t = pl.program_id(1)
    C = RET_CHUNK

    def init(direction):
        s_ref[...] = jnp.zeros_like(s_ref)
        ii = lax.broadcasted_iota(i32, (C, C), 0)
        jj = lax.broadcasted_iota(i32, (C, C), 1)
        rel = (ii - jj if direction == 0 else jj - ii).astype(f32)
        pos = lax.broadcasted_iota(i32, (C, 1), 0).astype(f32)
        for hd in range(RET_H):
            r = direction * RET_H + hd
            lg = -jnp.exp(dt_ref[r:r + 1, :])
            lg1 = lg[:, 0:1]
            mask_ref[hd] = jnp.where(rel >= 0, jnp.exp(lg * jnp.maximum(rel, 0.0)), 0.0)
            if direction == 0:
                dq_ref[hd] = jnp.exp(lg1 * (pos + 1.0))
                dk_ref[hd] = jnp.exp(lg1 * (C - 1.0 - pos))
            else:
                dq_ref[hd] = jnp.exp(lg1 * (C - pos))
                dk_ref[hd] = jnp.exp(lg1 * pos)
            dc_ref[hd] = jnp.exp(lg * float(C))

    pl.when(t == 0)(functools.partial(init, 0))
    pl.when(t == nc)(functools.partial(init, 1))

    fwd = t < nc
    row0 = pl.multiple_of(_ret_chunk_index(t, nc, ncc) * C, C)

    for hd in range(RET_H):
        ks = slice(hd * RET_DK, (hd + 1) * RET_DK)
        vs = slice(hd * RET_DV, (hd + 1) * RET_DV)
        qh = q_ref[:, ks]
        kh = k_ref[:, ks]
        vh = v_ref[:, vs]
        p = (_dot_nt(qh, kh) * mask_ref[hd]).astype(bf16)
        y = _dot(p, vh) + _dot(qh, s_ref[hd].astype(bf16)) * dq_ref[hd]
        kd = (kh.astype(f32) * dk_ref[hd]).astype(bf16)
        upd = lax.dot_general(kd, vh, (((0,), (0,)), ((), ())), preferred_element_type=f32)
        s_ref[hd] = s_ref[hd] * dc_ref[hd][0:1, 0:1] + upd
        yn = _rms(y)

        @pl.when(fwd)
        def _():
            of_ref[pl.ds(row0, C), vs] = (gf_ref[:, vs].astype(f32) * yn).astype(bf16)

        @pl.when(jnp.logical_not(fwd))
        def _():
            o_ref[:, vs] = (of_ref[pl.ds(row0, C), vs].astype(f32) + gb_ref[:, vs].astype(f32) * yn).astype(bf16)


def _ret_scan(dtab, q, k, v, gf, gb, n_ctx):
    B, T, _ = q.shape
    C = RET_CHUNK
    nc = T // C
    ncc = n_ctx // C
    cidx = functools.partial(_ret_chunk_index, nc=nc, ncc=ncc)
    first_back = ncc - 1
    return pl.pallas_call(
        functools.partial(_ret_scan_kernel, nc=nc, ncc=ncc),
        grid=(B, 2 * nc),
        in_specs=[pl.BlockSpec((2 * RET_H, LANES), lambda b, t: (0, 0)),
                  pl.BlockSpec((None, C, D), lambda b, t: (b, cidx(t), 0)),
                  pl.BlockSpec((None, C, D), lambda b, t: (b, cidx(t), 0)),
                  pl.BlockSpec((None, C, RET_VW), lambda b, t: (b, cidx(t), 0)),
                  pl.BlockSpec((None, C, RET_VW), lambda b, t: (b, jnp.where(t < nc, t, nc - 1), 0)),
                  pl.BlockSpec((None, C, RET_VW), lambda b, t: (b, jnp.where(t < nc, first_back, cidx(t)), 0))],
        out_specs=pl.BlockSpec((None, C, RET_VW), lambda b, t: (b, jnp.where(t < nc, first_back, cidx(t)), 0)),
        out_shape=jax.ShapeDtypeStruct((B, T, RET_VW), bf16),
        scratch_shapes=[pltpu.VMEM((RET_H, RET_DK, RET_DV), f32),
                        pltpu.VMEM((T, RET_VW), bf16),
                        pltpu.VMEM((RET_H, C, C), f32),
                        pltpu.VMEM((RET_H, C, 1), f32),
                        pltpu.VMEM((RET_H, C, 1), f32),
                        pltpu.VMEM((RET_H, 1, LANES), f32)],
        compiler_params=_cparams(("arbitrary", "arbitrary")),
        name="ret_scan",
    )(dtab, q, k, v, gf, gb)


def _route(f, rwt_ref, rb_ref, cnt_ref, e_ref, w_ref, r_ref):
    tm = f.shape[0]
    G = N_EXP // N_GRP
    logits = _dot_nt(rwt_ref[...], f, precision=lax.Precision.HIGHEST)
    s = _sigmoid(logits)
    sel = s + rb_ref[...]
    mi = lax.broadcasted_iota(i32, (G, tm), 0)
    neg = -jnp.inf
    s_g = [s[g * G:(g + 1) * G, :] for g in range(N_GRP)]
    sel_g = [sel[g * G:(g + 1) * G, :] for g in range(N_GRP)]

    def first_max(a, ids, big):
        mx = jnp.max(a, axis=0, keepdims=True)
        ix = jnp.min(jnp.where(a == mx, ids, big), axis=0, keepdims=True)
        return mx, ix

    gscore = jnp.zeros((N_GRP, tm), f32)
    gi = lax.broadcasted_iota(i32, (N_GRP, tm), 0)
    for g in range(N_GRP):
        t1, i1 = first_max(sel_g[g], mi, G)
        t2 = jnp.max(jnp.where(mi == i1, neg, sel_g[g]), axis=0, keepdims=True)
        gscore = jnp.where(gi == g, t1 + t2, gscore)
    gmask = jnp.zeros((N_GRP, tm), i32)
    cur = gscore
    for _ in range(TOPK_GRP):
        _, ix = first_max(cur, gi, N_GRP)
        hit = gi == ix
        gmask = jnp.where(hit, 1, gmask)
        cur = jnp.where(hit, neg, cur)
    cand = [jnp.where(gmask[g:g + 1, :] > 0, sel_g[g], neg) for g in range(N_GRP)]
    ids = [mi + g * G for g in range(N_GRP)]

    e_rows, w_rows = [], []
    for _ in range(TOP_K):
        mx = cand[0].max(axis=0, keepdims=True)
        for g in range(1, N_GRP):
            mx = jnp.maximum(mx, cand[g].max(axis=0, keepdims=True))
        ix = jnp.min(jnp.where(cand[0] == mx, ids[0], N_EXP), axis=0, keepdims=True)
        for g in range(1, N_GRP):
            ix = jnp.minimum(ix, jnp.min(jnp.where(cand[g] == mx, ids[g], N_EXP), axis=0, keepdims=True))
        wv = jnp.zeros((1, tm), f32)
        for g in range(N_GRP):
            hit = ids[g] == ix
            cand[g] = jnp.where(hit, neg, cand[g])
            wv = wv + jnp.sum(jnp.where(hit, s_g[g], 0.0), axis=0, keepdims=True)
        e_rows.append(ix)
        w_rows.append(wv)
    wsum = w_rows[0]
    for r in range(1, TOP_K):
        wsum = wsum + w_rows[r]

    selm = [jnp.zeros((G, tm), f32) for _ in range(N_GRP)]
    for r in range(TOP_K):
        for g in range(N_GRP):
            selm[g] = jnp.where(ids[g] == e_rows[r], 1.0, selm[g])
    m_all = jnp.concatenate(selm, axis=0)
    ri = lax.broadcasted_iota(i32, (tm, tm), 0)
    ci = lax.broadcasted_iota(i32, (tm, tm), 1)
    upper = jnp.where(ri <= ci, 1.0, 0.0).astype(bf16)
    incl = _dot(m_all.astype(bf16), upper)
    carry = cnt_ref[:, 0:1]
    rank_all = carry + incl - m_all
    cnt_ref[...] = cnt_ref[...] + incl[:, tm - 1:tm]
    for r in range(TOP_K):
        rk = jnp.zeros((1, tm), f32)
        for g in range(N_GRP):
            rk = rk + jnp.sum(jnp.where(ids[g] == e_rows[r], rank_all[g * G:(g + 1) * G, :], 0.0), axis=0, keepdims=True)
        e_ref[r:r + 1, :] = e_rows[r]
        w_ref[r:r + 1, :] = w_rows[r] / wsum * ROUTED_SCALE
        r_ref[r:r + 1, :] = rk.astype(i32)


def _post_mix_kernel(o_ref, wo_ref, x_ref, mod_ref, g_ref, rwt_ref, rb_ref,
                     x1_ref, fin_ref, hlin_ref, e_ref, w_ref, r_ref, cnt_ref):
    i = pl.program_id(1)

    @pl.when(i == 0)
    def _():
        cnt_ref[...] = jnp.zeros_like(cnt_ref)

    mod = mod_ref[...]
    x1 = x_ref[...] + mod[:, 2 * D:3 * D] * _dot(o_ref[...], wo_ref[...])
    x1_ref[...] = x1
    f = (_rms(x1) * g_ref[...]) * (1.0 + mod[:, 4 * D:5 * D]) + mod[:, 3 * D:4 * D]
    fin_ref[...] = f.astype(bf16)
    tm = f.shape[0]
    for j in range(D // LANES):
        hlin_ref[pl.ds(j, tm, stride=D // LANES), :] = f[:, j * LANES:(j + 1) * LANES]
    _route(f, rwt_ref, rb_ref, cnt_ref, e_ref, w_ref, r_ref)


def _post_mix(o, w_o, xs, x_tile_off, modtab, n_ctx_tiles, gain, rwt, rb):
    B, N, KO = o.shape
    tm = ROW_TILE
    nt = N // tm
    tok = lambda w: pl.BlockSpec((None, tm, w), lambda b, i: (b, i, 0))
    sel = lambda: pl.BlockSpec((None, TOP_K, tm), lambda b, i: (b, 0, i))
    return pl.pallas_call(
        _post_mix_kernel,
        grid=(B, nt),
        in_specs=[tok(KO),
                  pl.BlockSpec((KO, D), lambda b, i: (0, 0)),
                  pl.BlockSpec((None, tm, D), lambda b, i: (b, i + x_tile_off, 0)),
                  pl.BlockSpec((None, None, 1, 6 * D), lambda b, i: (b, jnp.where(i < n_ctx_tiles, 0, 1), 0, 0)),
                  pl.BlockSpec((1, D), lambda b, i: (0, 0)),
                  pl.BlockSpec((N_EXP, D), lambda b, i: (0, 0)),
                  pl.BlockSpec((N_EXP, 1), lambda b, i: (0, 0))],
        out_specs=[tok(D), tok(D),
                   pl.BlockSpec((None, tm * (D // LANES), LANES), lambda b, i: (b, i, 0)),
                   sel(), sel(), sel(),
                   pl.BlockSpec((None, N_EXP, LANES), lambda b, i: (b, 0, 0))],
        out_shape=[jax.ShapeDtypeStruct((B, N, D), f32), jax.ShapeDtypeStruct((B, N, D), bf16),
                   jax.ShapeDtypeStruct((B, N * (D // LANES), LANES), f32),
                   jax.ShapeDtypeStruct((B, TOP_K, N), i32), jax.ShapeDtypeStruct((B, TOP_K, N), f32),
                   jax.ShapeDtypeStruct((B, TOP_K, N), i32),
                   jax.ShapeDtypeStruct((B, N_EXP, LANES), f32)],
        compiler_params=_cparams(("arbitrary", "arbitrary")),
        name="post_mix",
    )(o, w_o, xs, modtab, gain, rwt, rb)


def _moe_kernel(te_ref, nt_ref, src_ref, hlin_ref, wtab_ref, wgu_ref, wd_ref, out_ref,
                acc_ref, xs_ref, wt_ref, ylin_ref, *, nt_max, n_tok):
    b = pl.program_id(0)
    t = pl.program_id(1)
    TM = MOE_TILE
    NCH = D // LANES
    TMP = TM + SUBLANES
    U = 8

    @pl.when(t == 0)
    def _():
        acc_ref[...] = jnp.zeros_like(acc_ref)

    @pl.when(t < nt_ref[b])
    def _():
        e = te_ref[b * nt_max + t]

        def gather(c, carry):
            for u in range(U):
                m = c * U + u
                tok = src_ref[m]
                g = jnp.minimum(tok, n_tok - 1)
                xs_ref[pl.ds(m, NCH, stride=TMP), :] = hlin_ref[pl.ds(pl.multiple_of(g * NCH, NCH), NCH), :]
                wt_ref[pl.ds(m, 1), :] = wtab_ref[pl.ds(tok, 1), :]
            return carry

        lax.fori_loop(0, TM // U, gather, 0)
        x = jnp.concatenate([xs_ref[pl.ds(j * TMP, TM), :] for j in range(NCH)], axis=1).astype(bf16)
        gu = _dot(x, wgu_ref[...])
        a = _silu(gu[:, :EXP_FF]) * gu[:, EXP_FF:]
        wt = wt_ref[...]
        match = jnp.where(wt == -(e + 1).astype(f32), 1.0, 0.0)
        wcol = jnp.sum(wt * pltpu.roll(match, LANES - TOP_K, axis=1), axis=1, keepdims=True)
        y = _dot((a * wcol).astype(bf16), wd_ref[...])
        for j in range(NCH):
            ylin_ref[pl.ds(j, TM, stride=NCH), :] = y[:, j * LANES:(j + 1) * LANES]

        def scatter(c, carry):
            toks = [src_ref[c * U + u] for u in range(U)]
            news = []
            for u in range(U):
                m = c * U + u
                row = pl.multiple_of((toks[u] & (ROW_TILE - 1)) * NCH, NCH)
                news.append(acc_ref[toks[u] >> 8, pl.ds(row, NCH), :]
                            + ylin_ref[pl.ds(pl.multiple_of(m * NCH, NCH), NCH), :])
            for u in range(U):
                row = pl.multiple_of((toks[u] & (ROW_TILE - 1)) * NCH, NCH)
                acc_ref[toks[u] >> 8, pl.ds(row, NCH), :] = news[u]
            return carry

        lax.fori_loop(0, TM // U, scatter, 0)

    @pl.when(t >= nt_max)
    def _():
        fidx = t - nt_max
        for j in range(NCH):
            out_ref[:, j * LANES:(j + 1) * LANES] = acc_ref[fidx, pl.ds(j, ROW_TILE, stride=NCH), :]


def _moe(te, ntiles, src, hlin, wtab, w_gu, w_d, nt_max):
    B = hlin.shape[0]
    NCH = D // LANES
    n_tok = hlin.shape[1] // NCH
    assert ROW_TILE == 256 and n_tok % ROW_TILE == 0
    nf = n_tok // ROW_TILE
    TM = MOE_TILE

    def tile(b, t, te_ref, nt_ref):
        return jnp.minimum(t, nt_ref[b] - 1)

    grid_spec = pltpu.PrefetchScalarGridSpec(
        num_scalar_prefetch=2,
        grid=(B, nt_max + nf),
        in_specs=[pl.BlockSpec((TM,), lambda b, t, te_ref, nt_ref: (b * nt_max + tile(b, t, te_ref, nt_ref),),
                               memory_space=pltpu.SMEM),
                  pl.BlockSpec((None, n_tok * NCH, LANES), lambda b, t, te_ref, nt_ref: (b, 0, 0),
                               pipeline_mode=pl.Buffered(1)),
                  pl.BlockSpec((None, n_tok + SUBLANES, LANES), lambda b, t, te_ref, nt_ref: (b, 0, 0)),
                  pl.BlockSpec((None, D, 2 * EXP_FF),
                               lambda b, t, te_ref, nt_ref: (te_ref[b * nt_max + tile(b, t, te_ref, nt_ref)], 0, 0)),
                  pl.BlockSpec((None, EXP_FF, D),
                               lambda b, t, te_ref, nt_ref: (te_ref[b * nt_max + tile(b, t, te_ref, nt_ref)], 0, 0))],
        out_specs=pl.BlockSpec((None, ROW_TILE, D), lambda b, t, te_ref, nt_ref: (b, jnp.maximum(t - nt_max, 0), 0)),
        scratch_shapes=[pltpu.VMEM((nf + 1, ROW_TILE * NCH, LANES), f32),
                        pltpu.VMEM((NCH * (TM + SUBLANES), LANES), f32),
                        pltpu.VMEM((TM, LANES), f32),
                        pltpu.VMEM((TM * NCH, LANES), f32)],
    )
    return pl.pallas_call(
        functools.partial(_moe_kernel, nt_max=nt_max, n_tok=n_tok),
        grid_spec=grid_spec,
        out_shape=jax.ShapeDtypeStruct((B, n_tok, D), f32),
        compiler_params=_cparams(("arbitrary", "arbitrary")),
        name="moe",
    )(te, ntiles, src, hlin, wtab, w_gu, w_d)


def _moe_plan(e_t, w_t, r_t, cnt):
    B, K, N = e_t.shape
    TM = MOE_TILE
    nt_max = (N * K + N_EXP * (TM - 1)) // TM + 1
    counts = cnt[:, :, 0].astype(i32)
    ntile = (counts + TM - 1) // TM
    tend = jnp.cumsum(ntile, axis=1)
    tstart = tend - ntile
    ntiles = jnp.maximum(tend[:, -1], 1).astype(i32)
    pos = jnp.take_along_axis(tstart * TM, e_t.reshape(B, K * N), axis=1).reshape(B, K, N) + r_t
    tok = jnp.broadcast_to(jnp.arange(N, dtype=i32)[None, None, :], (B, K, N))
    bidx = jnp.broadcast_to(jnp.arange(B, dtype=i32)[:, None, None], (B, K, N))
    src = jnp.full((B, nt_max * TM), N, i32).at[bidx, pos].set(tok)
    tt = jnp.minimum(jnp.arange(nt_max, dtype=i32)[None, :], ntiles[:, None] - 1)
    te = jnp.sum((tt[:, :, None] >= tend[:, None, :]).astype(i32), axis=2)
    te = jnp.minimum(te, N_EXP - 1).astype(i32)
    lanes = jnp.concatenate([w_t, -(e_t.astype(f32) + 1.0)], axis=1)
    wtab = jnp.zeros((B, N + SUBLANES, LANES), f32).at[:, :N, :2 * K].set(jnp.swapaxes(lanes, 1, 2))
    return te.reshape(-1), ntiles, src.reshape(-1), wtab, nt_max


def _shared_ffn(fin, shgu_ref, shd_ref):
    gu = _dot(fin, shgu_ref[...])
    return _dot((_silu(gu[:, :SH_FF]) * gu[:, SH_FF:]).astype(bf16), shd_ref[...])


def _post_ffn_mla_kernel(x1_ref, routed_ref, fin_ref, shgu_ref, shd_ref, mod0_ref, mod1_ref, g_ref, win_ref,
                         x2_ref, a_ref):
    x2 = x1_ref[...] + mod0_ref[...][:, 5 * D:6 * D] * (routed_ref[...] + _shared_ffn(fin_ref[...], shgu_ref, shd_ref))
    x2_ref[...] = x2
    mod1 = mod1_ref[...]
    h = (_rms(x2) * g_ref[...]) * (1.0 + mod1[:, D:2 * D]) + mod1[:, 0:D]
    a_ref[...] = _dot(h.astype(bf16), win_ref[...])


def _post_ffn_mla(x1, routed, fin, sh_gu, sh_d, modtab0, modtab1, gain, w_in, n_ctx_tiles):
    B, T, _ = x1.shape
    tm = ROW_TILE
    tok = lambda w: pl.BlockSpec((None, tm, w), lambda b, i: (b, i, 0))
    modspec = lambda: pl.BlockSpec((None, None, 1, 6 * D), lambda b, i: (b, jnp.where(i < n_ctx_tiles, 0, 1), 0, 0))
    full = lambda r, c: pl.BlockSpec((r, c), lambda b, i: (0, 0))
    return pl.pallas_call(
        _post_ffn_mla_kernel,
        grid=(B, T // tm),
        in_specs=[tok(D), tok(D), tok(D), full(D, 2 * SH_FF), full(SH_FF, D), modspec(), modspec(),
                  full(1, D), full(D, MLA_IN_PAD)],
        out_specs=[tok(D), tok(MLA_IN_PAD)],
        out_shape=[jax.ShapeDtypeStruct((B, T, D), f32), jax.ShapeDtypeStruct((B, T, MLA_IN_PAD), f32)],
        compiler_params=_cparams(("arbitrary", "arbitrary")),
        name="post_ffn_mla",
    )(x1, routed, fin, sh_gu, sh_d, modtab0, modtab1, gain, w_in)


def _post_ffn_final_kernel(x1_ref, routed_ref, fin_ref, shgu_ref, shd_ref, mod_ref, out_ref):
    out_ref[...] = x1_ref[...] + mod_ref[...][:, 5 * D:6 * D] * (
        routed_ref[...] + _shared_ffn(fin_ref[...], shgu_ref, shd_ref))


def _post_ffn_final(x1, routed, fin, sh_gu, sh_d, modtab):
    B, N, _ = x1.shape
    tm = ROW_TILE
    tok = lambda w: pl.BlockSpec((None, tm, w), lambda b, i: (b, i, 0))
    full = lambda r, c: pl.BlockSpec((r, c), lambda b, i: (0, 0))
    return pl.pallas_call(
        _post_ffn_final_kernel,
        grid=(B, N // tm),
        in_specs=[tok(D), tok(D), tok(D), full(D, 2 * SH_FF), full(SH_FF, D),
                  pl.BlockSpec((None, None, 1, 6 * D), lambda b, i: (b, 1, 0, 0))],
        out_specs=tok(D),
        out_shape=jax.ShapeDtypeStruct((B, N, D), f32),
        compiler_params=_cparams(("arbitrary", "arbitrary")),
        name="post_ffn_final",
    )(x1, routed, fin, sh_gu, sh_d, modtab)


def _mla_prep_kernel(a_ref, qan_ref, wqn_ref, wqr_ref, kvan_ref, wk_ref, wv_ref, qnn_ref, qnr_ref, knn_ref, knr_ref,
                     cos_ref, sin_ref, q_ref, k_ref, v_ref):
    a = a_ref[...]
    tm = a.shape[0]
    scale = MLA_QK ** -0.5
    cos = cos_ref[...]
    sin = sin_ref[...]
    lane = lax.broadcasted_iota(i32, (tm, LANES), 1)
    first = (lane // (MLA_ROPE // 4)) % 2 == 0

    def rope(xb):
        sw = jnp.where(first, pltpu.roll(xb, LANES - MLA_ROPE // 4, axis=1), pltpu.roll(xb, MLA_ROPE // 4, axis=1))
        return xb * cos + sw * sin

    qa = (_rms(a[:, :MLA_QR]) * qan_ref[...]).astype(bf16)
    qn = _dot(qa, wqn_ref[...])
    qr = _dot(qa, wqr_ref[...])
    ri = lax.broadcasted_iota(i32, (MLA_H * MLA_ROPE, MLA_H * MLA_ROPE), 0) // MLA_ROPE
    ci = lax.broadcasted_iota(i32, (MLA_H * MLA_ROPE, MLA_H * MLA_ROPE), 1) // MLA_ROPE
    seg = jnp.where(ri == ci, 1.0, 0.0)
    ssq = jnp.dot(qr * qr, seg, preferred_element_type=f32, precision=lax.Precision.HIGHEST)
    qr = qr * lax.rsqrt(ssq * (1.0 / MLA_ROPE) + EPS) * qnr_ref[...]
    qr_blocks = [rope(qr[:, p * LANES:(p + 1) * LANES]) * scale for p in range(MLA_H // 2)]

    kv = (_rms(a[:, MLA_QR:MLA_QR + MLA_KVR]) * kvan_ref[...]).astype(bf16)
    kn = _dot(kv, wk_ref[...])
    v_ref[...] = _dot(kv, wv_ref[...]).astype(bf16)
    kr = a[:, MLA_QR + MLA_KVR:MLA_IN_PAD]
    kr = rope(_rms(kr, MLA_ROPE) * knr_ref[...])
    kr_odd = pltpu.roll(kr, MLA_ROPE, axis=1)
    for hd in range(MLA_H):
        sl = slice(hd * MLA_NOPE, (hd + 1) * MLA_NOPE)
        q_ref[:, 2 * hd * LANES:(2 * hd + 1) * LANES] = (_rms(qn[:, sl]) * qnn_ref[...] * scale).astype(bf16)
        q_ref[:, (2 * hd + 1) * LANES:(2 * hd + 2) * LANES] = qr_blocks[hd // 2].astype(bf16)
        k_ref[:, 2 * hd * LANES:(2 * hd + 1) * LANES] = (_rms(kn[:, sl]) * knn_ref[...]).astype(bf16)
        k_ref[:, (2 * hd + 1) * LANES:(2 * hd + 2) * LANES] = (kr if hd % 2 == 0 else kr_odd).astype(bf16)


def _mla_prep(a, qan, wqn, wqr, kvan, wk, wv, qnn, qnr, knn, knr, cos_t, sin_t):
    B, T, _ = a.shape
    tm = ROW_TILE
    tok = lambda w: pl.BlockSpec((None, tm, w), lambda b, i: (b, i, 0))
    full = lambda r, c: pl.BlockSpec((r, c), lambda b, i: (0, 0))
    hw = 2 * LANES * MLA_H
    return pl.pallas_call(
        _mla_prep_kernel,
        grid=(B, T // tm),
        in_specs=[tok(MLA_IN_PAD), full(1, MLA_QR), full(MLA_QR, MLA_H * MLA_NOPE), full(MLA_QR, MLA_H * MLA_ROPE),
                  full(1, MLA_KVR), full(MLA_KVR, MLA_H * MLA_NOPE), full(MLA_KVR, MLA_H * MLA_V),
                  full(1, MLA_NOPE), full(1, MLA_H * MLA_ROPE), full(1, MLA_NOPE), full(1, LANES),
                  pl.BlockSpec((tm, LANES), lambda b, i: (i, 0)), pl.BlockSpec((tm, LANES), lambda b, i: (i, 0))],
        out_specs=[tok(hw), tok(hw), tok(MLA_H * MLA_V)],
        out_shape=[jax.ShapeDtypeStruct((B, T, hw), bf16), jax.ShapeDtypeStruct((B, T, hw), bf16),
                   jax.ShapeDtypeStruct((B, T, MLA_H * MLA_V), bf16)],
        compiler_params=_cparams(("arbitrary", "arbitrary")),
        name="mla_prep",
    )(a, qan, wqn, wqr, kvan, wk, wv, qnn, qnr, knn, knr, cos_t, sin_t)


def _mla_attn_kernel(q_ref, k_ref, v_ref, o_ref):
    s = _dot_nt(q_ref[...], k_ref[...])
    m = jnp.max(s, axis=-1, keepdims=True)
    p = jnp.exp(s - m)
    l = jnp.sum(p, axis=-1, keepdims=True)
    o_ref[...] = (_dot(p.astype(bf16), v_ref[...]) / l).astype(bf16)


def _mla_attn(q, k, v, n_ctx_tiles):
    B, T, _ = q.shape
    tq = ROW_TILE
    S = T - n_ctx_tiles * tq
    return pl.pallas_call(
        _mla_attn_kernel,
        grid=(B, MLA_H, S // tq),
        in_specs=[pl.BlockSpec((None, tq, 2 * LANES), lambda b, h, i: (b, i + n_ctx_tiles, h)),
                  pl.BlockSpec((None, T, 2 * LANES), lambda b, h, i: (b, 0, h)),
                  pl.BlockSpec((None, T, MLA_V), lambda b, h, i: (b, 0, h))],
        out_specs=pl.BlockSpec((None, tq, MLA_V), lambda b, h, i: (b, i, h)),
        out_shape=jax.ShapeDtypeStruct((B, S, MLA_H * MLA_V), bf16),
        compiler_params=_cparams(("arbitrary", "arbitrary", "arbitrary")),
        name="mla_attn",
    )(q, k, v)


def _axial_angles(rows_n, rot_dim):
    axis_dim = rot_dim // 2
    inv = ROPE_BASE ** (-jnp.arange(0, axis_dim, 2, dtype=f32) / axis_dim)
    row = jnp.repeat(jnp.arange(rows_n, dtype=f32), GRID_W)
    col = jnp.tile(jnp.arange(GRID_W, dtype=f32), rows_n)
    return row[:, None] * inv, col[:, None] * inv


def _rope_tables(seq, n_ctx, rot_dim, reps):
    ang_r, ang_c = _axial_angles(seq // GRID_W, rot_dim)
    cos = jnp.concatenate([jnp.cos(ang_r)] * 2 + [jnp.cos(ang_c)] * 2, axis=1)
    sin = jnp.concatenate([-jnp.sin(ang_r), jnp.sin(ang_r), -jnp.sin(ang_c), jnp.sin(ang_c)], axis=1)
    cos = jnp.concatenate([jnp.ones((n_ctx, rot_dim), f32), cos], axis=0)
    sin = jnp.concatenate([jnp.zeros((n_ctx, rot_dim), f32), sin], axis=0)
    return jnp.tile(cos, (1, reps)), jnp.tile(sin, (1, reps))


def kernel(x, c, ctx, c_ctx, ada_w, ada_b, norm_mix, norm_ffn, ret_w_in, ret_decay_f, ret_decay_b, ret_w_o,
           mla_w_in, mla_q_a_norm, mla_w_q_b, mla_kv_a_norm, mla_w_kv_b, mla_q_norm, mla_k_norm, mla_w_o,
           router_w, router_bias, exp_w_gu, exp_w_down, sh_w_gu, sh_w_down):
    B, S, _ = x.shape
    n_ctx = ctx.shape[1]
    assert n_ctx % ROW_TILE == 0 and S % ROW_TILE == 0 and S % GRID_W == 0
    n_ctx_tiles = n_ctx // ROW_TILE

    rows = -(-(B + 1) // SUBLANES) * SUBLANES
    cc = jnp.zeros((rows, D), f32).at[:B].set(c).at[B].set(c_ctx)
    mod = _ada(cc, ada_w, ada_b)

    def modtab(i):
        ctx_row = jnp.broadcast_to(mod[i, B][None, :], (B, 6 * D))
        return jnp.stack([ctx_row, mod[i, :B]], axis=1)[:, :, None, :]

    mod0, mod1 = modtab(0), modtab(1)
    xc = jnp.concatenate([ctx, x], axis=1)

    cos_r, sin_r = _rope_tables(S, n_ctx, RET_DK, 1)
    q, k, v, gf, gb = _ret_inproj(xc, mod0, norm_mix[0][None, :], ret_w_in[0].astype(bf16), cos_r, sin_r, n_ctx_tiles)
    dtab = jnp.broadcast_to(jnp.concatenate([ret_decay_f[0], ret_decay_b[0]])[:, None], (2 * RET_H, LANES))
    o = _ret_scan(dtab, q, k, v, gf, gb, n_ctx)
    x1, fin, hlin, e_t, w_t, r_t, cnt = _post_mix(
        o, ret_w_o[0].astype(bf16), xc, 0, mod0, n_ctx_tiles, norm_ffn[0][None, :],
        router_w[0].T, router_bias[0][:, None])
    te, ntiles, src, wtab, nt_max = _moe_plan(e_t, w_t, r_t, cnt)
    routed = _moe(te, ntiles, src, hlin, wtab, exp_w_gu[0].astype(bf16), exp_w_down[0].astype(bf16), nt_max)
    w_in1 = jnp.zeros((D, MLA_IN_PAD), f32).at[:, :mla_w_in.shape[2]].set(mla_w_in[0]).astype(bf16)
    x2, a = _post_ffn_mla(x1, routed, fin, sh_w_gu[0].astype(bf16), sh_w_down[0].astype(bf16), mod0, mod1,
                          norm_mix[1][None, :], w_in1, n_ctx_tiles)

    wq = mla_w_q_b[0].reshape(MLA_QR, MLA_H, MLA_QK)
    wqn = wq[:, :, :MLA_NOPE].reshape(MLA_QR, MLA_H * MLA_NOPE).astype(bf16)
    wqr = wq[:, :, MLA_NOPE:].reshape(MLA_QR, MLA_H * MLA_ROPE).astype(bf16)
    wkv = mla_w_kv_b[0].reshape(MLA_KVR, MLA_H, MLA_NOPE + MLA_V)
    wk = wkv[:, :, :MLA_NOPE].reshape(MLA_KVR, MLA_H * MLA_NOPE).astype(bf16)
    wv = wkv[:, :, MLA_NOPE:].reshape(MLA_KVR, MLA_H * MLA_V).astype(bf16)
    qnn = mla_q_norm[0][None, :MLA_NOPE]
    qnr = jnp.tile(mla_q_norm[0][None, MLA_NOPE:], (1, MLA_H))
    knn = mla_k_norm[0][None, :MLA_NOPE]
    knr = jnp.concatenate([mla_k_norm[0][MLA_NOPE:], jnp.zeros((LANES - MLA_ROPE,), f32)])[None, :]
    cos_m, sin_m = _rope_tables(S, n_ctx, MLA_ROPE, LANES // MLA_ROPE)
    qf, kf, vf = _mla_prep(a, mla_q_a_norm[0][None, :], wqn, wqr, mla_kv_a_norm[0][None, :], wk, wv,
                           qnn, qnr, knn, knr, cos_m, sin_m)
    o1 = _mla_attn(qf, kf, vf, n_ctx_tiles)
    x3, fin1, hlin1, e1, w1, r1, cnt1 = _post_mix(
        o1, mla_w_o[0].astype(bf16), x2, n_ctx_tiles, mod1, 0, norm_ffn[1][None, :],
        router_w[1].T, router_bias[1][:, None])
    te1, ntiles1, src1, wtab1, nt_max1 = _moe_plan(e1, w1, r1, cnt1)
    routed1 = _moe(te1, ntiles1, src1, hlin1, wtab1, exp_w_gu[1].astype(bf16), exp_w_down[1].astype(bf16), nt_max1)
    return _post_ffn_final(x3, routed1, fin1, sh_w_gu[1].astype(bf16), sh_w_down[1].astype(bf16), mod1)
```

```python
import functools

import jax
import jax.numpy as jnp
import numpy as np
from jax import lax
from jax.experimental import pallas as pl
from jax.experimental.pallas import tpu as pltpu

f32 = jnp.float32
bf16 = jnp.bfloat16
i32 = jnp.int32

D = 1024
GRID_W = 64
EPS = 1e-6
ROPE_BASE = 10000.0
RET_H = 4
RET_DK = 256
RET_DV = 512
RET_VW = RET_H * RET_DV
RET_CHUNK = 128
MLA_H = 8
MLA_NOPE = 128
MLA_ROPE = 64
MLA_QK = MLA_NOPE + MLA_ROPE
MLA_V = 128
MLA_QR = 384
MLA_KVR = 256
MLA_IN_PAD = 768
N_EXP = 64
TOP_K = 8
N_GRP = 8
TOPK_GRP = 4
EXP_FF = 256
SH_FF = 256
ROUTED_SCALE = 2.5

LANES = 128
SUBLANES = 8
ROW_TILE = 256
MOE_TILE = 256
VMEM_LIMIT = 56 * 1024 * 1024


def _cparams(sem, vmem=VMEM_LIMIT):
    return pltpu.CompilerParams(dimension_semantics=sem, vmem_limit_bytes=vmem)


def _sigmoid(x):
    return 1.0 / (1.0 + jnp.exp(-x))


def _silu(x):
    return x * _sigmoid(x)


def _rms(x, n=None):
    n = x.shape[-1] if n is None else n
    return x * lax.rsqrt(jnp.sum(x * x, axis=-1, keepdims=True) * (1.0 / n) + EPS)


def _dot(a, b):
    return jnp.dot(a, b, preferred_element_type=f32)


def _dot_nt(a, b, precision=None):
    return lax.dot_general(a, b, (((1,), (1,)), ((), ())), preferred_element_type=f32, precision=precision)


def _ada_kernel(c_ref, w_ref, b_ref, o_ref):
    s = _silu(c_ref[...]).astype(bf16)
    o_ref[...] = _dot(s, w_ref[...].astype(bf16)) + b_ref[...]


def _ada(cc, ada_w, ada_b):
    depth = ada_w.shape[0]
    rows = cc.shape[0]
    tn = 1536
    return pl.pallas_call(
        _ada_kernel,
        grid=(depth, 6 * D // tn),
        in_specs=[pl.BlockSpec((rows, D), lambda i, j: (0, 0)),
                  pl.BlockSpec((None, D, tn), lambda i, j: (i, 0, j)),
                  pl.BlockSpec((None, 1, tn), lambda i, j: (i, 0, j))],
        out_specs=pl.BlockSpec((None, rows, tn), lambda i, j: (i, 0, j)),
        out_shape=jax.ShapeDtypeStruct((depth, rows, 6 * D), f32),
        compiler_params=_cparams(("arbitrary", "arbitrary")),
        name="ada",
    )(cc, ada_w, ada_b.reshape(depth, 1, 6 * D))


def _ret_inproj_kernel(x_ref, mod_ref, g_ref, w_ref, cos_ref, sin_ref, q_ref, k_ref, v_ref, gf_ref, gb_ref):
    x = x_ref[...]
    mod = mod_ref[...]
    h = (_rms(x) * g_ref[...]) * (1.0 + mod[:, D:2 * D]) + mod[:, 0:D]
    hb = h.astype(bf16)
    cos = cos_ref[...]
    sin = sin_ref[...]

    def rope(a):
        outs = []
        for half in range(2):
            sl = slice(half * LANES, (half + 1) * LANES)
            ah = a[:, sl]
            outs.append(ah * cos[:, sl] + pltpu.roll(ah, LANES // 2, axis=1) * sin[:, sl])
        return jnp.concatenate(outs, axis=1)

    for hd in range(RET_H):
        sl = slice(hd * RET_DK, (hd + 1) * RET_DK)
        q_ref[:, sl] = rope(_dot(hb, w_ref[:, sl])).astype(bf16)
    for hd in range(RET_H):
        sl = slice(hd * RET_DK, (hd + 1) * RET_DK)
        wsl = slice(D + hd * RET_DK, D + (hd + 1) * RET_DK)
        k_ref[:, sl] = (rope(_dot(hb, w_ref[:, wsl])) * (RET_DK ** -0.5)).astype(bf16)
    cw = 512
    for c in range(RET_VW // cw):
        sl = slice(c * cw, (c + 1) * cw)
        v_ref[:, sl] = _dot(hb, w_ref[:, 2 * D + c * cw:2 * D + (c + 1) * cw]).astype(bf16)
        gf_ref[:, sl] = _silu(_dot(hb, w_ref[:, 2 * D + RET_VW + c * cw:2 * D + RET_VW + (c + 1) * cw])).astype(bf16)
        gb_ref[:, sl] = _silu(_dot(hb, w_ref[:, 2 * D + 2 * RET_VW + c * cw:2 * D + 2 * RET_VW + (c + 1) * cw])).astype(bf16)


def _ret_inproj(xc, modtab, gain, w_in, cos_t, sin_t, n_ctx_tiles):
    B, T, _ = xc.shape
    tm = ROW_TILE
    n_in = w_in.shape[1]
    tok = lambda w: pl.BlockSpec((None, tm, w), lambda b, i: (b, i, 0))
    return pl.pallas_call(
        _ret_inproj_kernel,
        grid=(B, T // tm),
        in_specs=[tok(D),
                  pl.BlockSpec((None, None, 1, 6 * D), lambda b, i: (b, jnp.where(i < n_ctx_tiles, 0, 1), 0, 0)),
                  pl.BlockSpec((1, D), lambda b, i: (0, 0)),
                  pl.BlockSpec((D, n_in), lambda b, i: (0, 0), pipeline_mode=pl.Buffered(1)),
                  pl.BlockSpec((tm, RET_DK), lambda b, i: (i, 0)),
                  pl.BlockSpec((tm, RET_DK), lambda b, i: (i, 0))],
        out_specs=[tok(D), tok(D), tok(RET_VW), tok(RET_VW), tok(RET_VW)],
        out_shape=[jax.ShapeDtypeStruct((B, T, D), bf16), jax.ShapeDtypeStruct((B, T, D), bf16),
                   jax.ShapeDtypeStruct((B, T, RET_VW), bf16), jax.ShapeDtypeStruct((B, T, RET_VW), bf16),
                   jax.ShapeDtypeStruct((B, T, RET_VW), bf16)],
        compiler_params=_cparams(("arbitrary", "arbitrary")),
        name="ret_inproj",
    )(xc, modtab, gain, w_in, cos_t, sin_t)


def _ret_chunk_index(t, nc, ncc):
    u = t - nc
    back = jnp.where(u < ncc, ncc - 1 - u, nc - 1 - u + ncc)
    return jnp.where(t < nc, t, back)


def _ret_scan_kernel(dt_ref, q_ref, k_ref, v_ref, gf_ref, gb_ref, o_ref,
                     s_ref, of_ref, mask_ref, dq_ref, dk_ref, dc_ref, *, nc, ncc):
    t = pl.program_id(1)
    C = RET_CHUNK

    def init(direction):
        s_ref[...] = jnp.zeros_like(s_ref)
        ii = lax.broadcasted_iota(i32, (C, C), 0)
        jj = lax.broadcasted_iota(i32, (C, C), 1)
        rel = (ii - jj if direction == 0 else jj - ii).astype(f32)
        pos = lax.broadcasted_iota(i32, (C, 1), 0).astype(f32)
        for hd in range(RET_H):
            r = direction * RET_H + hd
            lg = -jnp.exp(dt_ref[r:r + 1, :])
            lg1 = lg[:, 0:1]
            mask_ref[hd] = jnp.where(rel >= 0, jnp.exp(lg * jnp.maximum(rel, 0.0)), 0.0)
            if direction == 0:
                dq_ref[hd] = jnp.exp(lg1 * (pos + 1.0))
                dk_ref[hd] = jnp.exp(lg1 * (C - 1.0 - pos))
            else:
                dq_ref[hd] = jnp.exp(lg1 * (C - pos))
                dk_ref[hd] = jnp.exp(lg1 * pos)
            dc_ref[hd] = jnp.exp(lg * float(C))

    pl.when(t == 0)(functools.partial(init, 0))
    pl.when(t == nc)(functools.partial(init, 1))

    fwd = t < nc
    row0 = pl.multiple_of(_ret_chunk_index(t, nc, ncc) * C, C)

    for hd in range(RET_H):
        ks = slice(hd * RET_DK, (hd + 1) * RET_DK)
        vs = slice(hd * RET_DV, (hd + 1) * RET_DV)
        qh = q_ref[:, ks]
        kh = k_ref[:, ks]
        vh = v_ref[:, vs]
        p = (_dot_nt(qh, kh) * mask_ref[hd]).astype(bf16)
        y = _dot(p, vh) + _dot(qh, s_ref[hd].astype(bf16)) * dq_ref[hd]
        kd = (kh.astype(f32) * dk_ref[hd]).astype(bf16)
        upd = lax.dot_general(kd, vh, (((0,), (0,)), ((), ())), preferred_element_type=f32)
        s_ref[hd] = s_ref[hd] * dc_ref[hd][0:1, 0:1] + upd
        yn = _rms(y)

        @pl.when(fwd)
        def _():
            of_ref[pl.ds(row0, C), vs] = (gf_ref[:, vs].astype(f32) * yn).astype(bf16)

        @pl.when(jnp.logical_not(fwd))
        def _():
            o_ref[:, vs] = (of_ref[pl.ds(row0, C), vs].astype(f32) + gb_ref[:, vs].astype(f32) * yn).astype(bf16)


def _ret_scan(dtab, q, k, v, gf, gb, n_ctx):
    B, T, _ = q.shape
    C = RET_CHUNK
    nc = T // C
    ncc = n_ctx // C
    cidx = functools.partial(_ret_chunk_index, nc=nc, ncc=ncc)
    first_back = ncc - 1
    return pl.pallas_call(
        functools.partial(_ret_scan_kernel, nc=nc, ncc=ncc),
        grid=(B, 2 * nc),
        in_specs=[pl.BlockSpec((2 * RET_H, LANES), lambda b, t: (0, 0)),
                  pl.BlockSpec((None, C, D), lambda b, t: (b, cidx(t), 0)),
                  pl.BlockSpec((None, C, D), lambda b, t: (b, cidx(t), 0)),
                  pl.BlockSpec((None, C, RET_VW), lambda b, t: (b, cidx(t), 0)),
                  pl.BlockSpec((None, C, RET_VW), lambda b, t: (b, jnp.where(t < nc, t, nc - 1), 0)),
                  pl.BlockSpec((None, C, RET_VW), lambda b, t: (b, jnp.where(t < nc, first_back, cidx(t)), 0))],
        out_specs=pl.BlockSpec((None, C, RET_VW), lambda b, t: (b, jnp.where(t < nc, first_back, cidx(t)), 0)),
        out_shape=jax.ShapeDtypeStruct((B, T, RET_VW), bf16),
        scratch_shapes=[pltpu.VMEM((RET_H, RET_DK, RET_DV), f32),
                        pltpu.VMEM((T, RET_VW), bf16),
                        pltpu.VMEM((RET_H, C, C), f32),
                        pltpu.VMEM((RET_H, C, 1), f32),
                        pltpu.VMEM((RET_H, C, 1), f32),
                        pltpu.VMEM((RET_H, 1, LANES), f32)],
        compiler_params=_cparams(("arbitrary", "arbitrary")),
        name="ret_scan",
    )(dtab, q, k, v, gf, gb)


def _route(f, rwt_ref, rb_ref, cnt_ref, e_ref, w_ref, r_ref):
    tm = f.shape[0]
    G = N_EXP // N_GRP
    logits = _dot_nt(rwt_ref[...], f, precision=lax.Precision.HIGHEST)
    s = _sigmoid(logits)
    sel = s + rb_ref[...]
    mi = lax.broadcasted_iota(i32, (G, tm), 0)
    neg = -jnp.inf
    s_g = [s[g * G:(g + 1) * G, :] for g in range(N_GRP)]
    sel_g = [sel[g * G:(g + 1) * G, :] for g in range(N_GRP)]

    def first_max(a, ids, big):
        mx = jnp.max(a, axis=0, keepdims=True)
        ix = jnp.min(jnp.where(a == mx, ids, big), axis=0, keepdims=True)
        return mx, ix

    gscore = jnp.zeros((N_GRP, tm), f32)
    gi = lax.broadcasted_iota(i32, (N_GRP, tm), 0)
    for g in range(N_GRP):
        t1, i1 = first_max(sel_g[g], mi, G)
        t2 = jnp.max(jnp.where(mi == i1, neg, sel_g[g]), axis=0, keepdims=True)
        gscore = jnp.where(gi == g, t1 + t2, gscore)
    gmask = jnp.zeros((N_GRP, tm), i32)
    cur = gscore
    for _ in range(TOPK_GRP):
        _, ix = first_max(cur, gi, N_GRP)
        hit = gi == ix
        gmask = jnp.where(hit, 1, gmask)
        cur = jnp.where(hit, neg, cur)
    cand = [jnp.where(gmask[g:g + 1, :] > 0, sel_g[g], neg) for g in range(N_GRP)]
    ids = [mi + g * G for g in range(N_GRP)]

    e_rows, w_rows = [], []
    for _ in range(TOP_K):
        mx = cand[0].max(axis=0, keepdims=True)
        for g in range(1, N_GRP):
            mx = jnp.maximum(mx, cand[g].max(axis=0, keepdims=True))
        ix = jnp.min(jnp.where(cand[0] == mx, ids[0], N_EXP), axis=0, keepdims=True)
        for g in range(1, N_GRP):
            ix = jnp.minimum(ix, jnp.min(jnp.where(cand[g] == mx, ids[g], N_EXP), axis=0, keepdims=True))
        wv = jnp.zeros((1, tm), f32)
        for g in range(N_GRP):
            hit = ids[g] == ix
            cand[g] = jnp.where(hit, neg, cand[g])
            wv = wv + jnp.sum(jnp.where(hit, s_g[g], 0.0), axis=0, keepdims=True)
        e_rows.append(ix)
        w_rows.append(wv)
    wsum = w_rows[0]
    for r in range(1, TOP_K):
        wsum = wsum + w_rows[r]

    selm = [jnp.zeros((G, tm), f32) for _ in range(N_GRP)]
    for r in range(TOP_K):
        for g in range(N_GRP):
            selm[g] = jnp.where(ids[g] == e_rows[r], 1.0, selm[g])
    m_all = jnp.concatenate(selm, axis=0)
    ri = lax.broadcasted_iota(i32, (tm, tm), 0)
    ci = lax.broadcasted_iota(i32, (tm, tm), 1)
    upper = jnp.where(ri <= ci, 1.0, 0.0).astype(bf16)
    incl = _dot(m_all.astype(bf16), upper)
    carry = cnt_ref[:, 0:1]
    rank_all = carry + incl - m_all
    cnt_ref[...] = cnt_ref[...] + incl[:, tm - 1:tm]
    for r in range(TOP_K):
        rk = jnp.zeros((1, tm), f32)
        for g in range(N_GRP):
            rk = rk + jnp.sum(jnp.where(ids[g] == e_rows[r], rank_all[g * G:(g + 1) * G, :], 0.0), axis=0, keepdims=True)
        e_ref[r:r + 1, :] = e_rows[r]
        w_ref[r:r + 1, :] = w_rows[r] / wsum * ROUTED_SCALE
        r_ref[r:r + 1, :] = rk.astype(i32)


def _post_mix_kernel(o_ref, wo_ref, x_ref, mod_ref, g_ref, rwt_ref, rb_ref,
                     x1_ref, fin_ref, hlin_ref, e_ref, w_ref, r_ref, cnt_ref):
    i = pl.program_id(1)

    @pl.when(i == 0)
    def _():
        cnt_ref[...] = jnp.zeros_like(cnt_ref)

    mod = mod_ref[...]
    x1 = x_ref[...] + mod[:, 2 * D:3 * D] * _dot(o_ref[...], wo_ref[...])
    x1_ref[...] = x1
    f = (_rms(x1) * g_ref[...]) * (1.0 + mod[:, 4 * D:5 * D]) + mod[:, 3 * D:4 * D]
    fin_ref[...] = f.astype(bf16)
    tm = f.shape[0]
    for j in range(D // LANES):
        hlin_ref[pl.ds(j, tm, stride=D // LANES), :] = f[:, j * LANES:(j + 1) * LANES]
    _route(f, rwt_ref, rb_ref, cnt_ref, e_ref, w_ref, r_ref)


def _post_mix(o, w_o, xs, x_tile_off, modtab, n_ctx_tiles, gain, rwt, rb):
    B, N, KO = o.shape
    tm = ROW_TILE
    nt = N // tm
    tok = lambda w: pl.BlockSpec((None, tm, w), lambda b, i: (b, i, 0))
    sel = lambda: pl.BlockSpec((None, TOP_K, tm), lambda b, i: (b, 0, i))
    return pl.pallas_call(
        _post_mix_kernel,
        grid=(B, nt),
        in_specs=[tok(KO),
                  pl.BlockSpec((KO, D), lambda b, i: (0, 0)),
                  pl.BlockSpec((None, tm, D), lambda b, i: (b, i + x_tile_off, 0)),
                  pl.BlockSpec((None, None, 1, 6 * D), lambda b, i: (b, jnp.where(i < n_ctx_tiles, 0, 1), 0, 0)),
                  pl.BlockSpec((1, D), lambda b, i: (0, 0)),
                  pl.BlockSpec((N_EXP, D), lambda b, i: (0, 0)),
                  pl.BlockSpec((N_EXP, 1), lambda b, i: (0, 0))],
        out_specs=[tok(D), tok(D),
                   pl.BlockSpec((None, tm * (D // LANES), LANES), lambda b, i: (b, i, 0)),
                   sel(), sel(), sel(),
                   pl.BlockSpec((None, N_EXP, LANES), lambda b, i: (b, 0, 0))],
        out_shape=[jax.ShapeDtypeStruct((B, N, D), f32), jax.ShapeDtypeStruct((B, N, D), bf16),
                   jax.ShapeDtypeStruct((B, N * (D // LANES), LANES), f32),
                   jax.ShapeDtypeStruct((B, TOP_K, N), i32), jax.ShapeDtypeStruct((B, TOP_K, N), f32),
                   jax.ShapeDtypeStruct((B, TOP_K, N), i32),
                   jax.ShapeDtypeStruct((B, N_EXP, LANES), f32)],
        compiler_params=_cparams(("arbitrary", "arbitrary")),
        name="post_mix",
    )(o, w_o, xs, modtab, gain, rwt, rb)


META_W = 256
PLAN_ALIGN = 1024


def _round_up(n, m):
    return -(-n // m) * m


def _moe_sizes(n_tok):
    tm = MOE_TILE
    nt_max = (n_tok * TOP_K + N_EXP * (tm - 1)) // tm + 1
    ntp = _round_up(nt_max + 1, PLAN_ALIGN // tm)
    assert ntp <= META_W
    return ntp, _round_up(n_tok, PLAN_ALIGN), _round_up(n_tok * TOP_K + SUBLANES, PLAN_ALIGN)


def _plan_kernel(e_ref, r_ref, cnt_ref, pos_ref, meta_ref, *, n_tok):
    tm = MOE_TILE
    ntile = jnp.floor((cnt_ref[...] + (tm - 1.0)) * (1.0 / tm))
    ntb = ntile.astype(bf16)
    ei = lax.broadcasted_iota(i32, (N_EXP, LANES), 0)
    ej = lax.broadcasted_iota(i32, (N_EXP, LANES), 1)
    lower = jnp.where(ej <= ei, 1.0, 0.0)[:, :N_EXP].astype(bf16)
    tend = _dot(lower, ntb)
    tstart = tend - ntile
    upper = jnp.where(ei <= ej, 1.0, 0.0).astype(bf16)
    tend_row = lax.dot_general(ntb, upper, (((0,), (0,)), ((), ())), preferred_element_type=f32)[0:1, :]
    tt = lax.broadcasted_iota(i32, (N_EXP, META_W), 1).astype(f32)
    te = jnp.sum(jnp.where(tt >= tend[:, 0:1], 1.0, 0.0), axis=0, keepdims=True)
    meta_ref[...] = jnp.zeros_like(meta_ref)
    meta_ref[0:1, :] = jnp.minimum(te, N_EXP - 1.0).astype(i32)
    meta_ref[1:2, :] = jnp.broadcast_to(tend[N_EXP - 1:N_EXP, 0:1], (1, META_W)).astype(i32)
    meta_ref[2:3, 0:LANES] = tend_row.astype(i32)
    e = e_ref[...]
    base = jnp.zeros(e.shape, f32)
    for ex in range(N_EXP):
        base = jnp.where(e == ex, tstart[ex:ex + 1, 0:1] * float(tm), base)
    pos_ref[...] = jnp.zeros_like(pos_ref)
    pos_ref[:, 0:n_tok] = base.astype(i32) + r_ref[...]


def _plan(e_t, r_t, cnt):
    B, K, N = e_t.shape
    _, npad, _ = _moe_sizes(N)
    return pl.pallas_call(
        functools.partial(_plan_kernel, n_tok=N),
        grid=(B,),
        in_specs=[pl.BlockSpec((None, K, N), lambda b: (b, 0, 0)),
                  pl.BlockSpec((None, K, N), lambda b: (b, 0, 0)),
                  pl.BlockSpec((None, N_EXP, LANES), lambda b: (b, 0, 0))],
        out_specs=[pl.BlockSpec((None, K, npad), lambda b: (b, 0, 0)),
                   pl.BlockSpec((None, SUBLANES, META_W), lambda b: (b, 0, 0))],
        out_shape=[jax.ShapeDtypeStruct((B, K, npad), i32), jax.ShapeDtypeStruct((B, SUBLANES, META_W), i32)],
        compiler_params=_cparams(("arbitrary",)),
        name="moe_plan",
    )(e_t, r_t, cnt)


def _plan_invert_kernel(tend_ref, nt_ref, pos_ref, sk_ref, *, n_tok, ntp):
    b = pl.program_id(0)
    k = pl.program_id(1)
    tm = MOE_TILE
    U = 8
    pad = n_tok * TOP_K

    def fill_tile(tile):
        def body(c, carry):
            for u in range(U):
                sk_ref[tile * tm + c * U + u] = pad
            return carry
        lax.fori_loop(0, tm // U, body, 0)

    @pl.when(k == 0)
    def _():
        def per_expert(ex, carry):
            fill_tile(jnp.maximum(tend_ref[b * N_EXP + ex] - 1, 0))
            return carry
        lax.fori_loop(0, N_EXP, per_expert, 0)

        def tail(tile, carry):
            fill_tile(tile)
            return carry
        lax.fori_loop(nt_ref[b], ntp, tail, 0)

    def body(c, carry):
        for u in range(U):
            n = c * U + u
            sk_ref[pos_ref[n]] = n * TOP_K + k
        return carry
    lax.fori_loop(0, n_tok // U, body, 0)


def _plan_invert(tend, nt, pos, n_tok):
    B, K, npad = pos.shape
    ntp, _, _ = _moe_sizes(n_tok)
    plen = ntp * MOE_TILE
    grid_spec = pltpu.PrefetchScalarGridSpec(
        num_scalar_prefetch=2,
        grid=(B, K),
        in_specs=[pl.BlockSpec((npad,), lambda b, k, *_: (b * K + k,), memory_space=pltpu.SMEM)],
        out_specs=pl.BlockSpec((plen,), lambda b, k, *_: (b,), memory_space=pltpu.SMEM),
    )
    return pl.pallas_call(
        functools.partial(_plan_invert_kernel, n_tok=n_tok, ntp=ntp),
        grid_spec=grid_spec,
        out_shape=jax.ShapeDtypeStruct((B * plen,), i32),
        compiler_params=_cparams(("arbitrary", "arbitrary")),
        name="moe_invert",
    )(tend, nt, pos.reshape(-1))


def _moe_kernel(te_ref, nt_ref, skg_ref, sks_ref, w_ref, hlin_ref, wgu_ref, wd_ref, out_ref,
                acc_ref, xs_ref, ylin_ref, *, ntp, n_tok):
    b = pl.program_id(0)
    t = pl.program_id(1)
    TM = MOE_TILE
    NCH = D // LANES
    TMP = TM + SUBLANES
    U = 8

    @pl.when(t == 0)
    def _():
        acc_ref[...] = jnp.zeros_like(acc_ref)
        xs_ref[...] = jnp.zeros_like(xs_ref)
        ylin_ref[...] = jnp.zeros_like(ylin_ref)

    def stage(p):
        for m in range(TM):
            sk = skg_ref[m]
            off = pl.multiple_of(jnp.minimum(sk & -NCH, (n_tok - 1) * NCH), NCH)
            xs_ref[p, pl.ds(m, NCH, stride=TMP), :] = hlin_ref[pl.ds(off, NCH), :]
        x = jnp.concatenate([xs_ref[1 - p, pl.ds(j * TMP, TM), :] for j in range(NCH)], axis=1).astype(bf16)
        gu = _dot(x, wgu_ref[...])
        a = (_silu(gu[:, :EXP_FF]) * gu[:, EXP_FF:]).astype(bf16)
        y = _dot(a, wd_ref[...])
        for j in range(NCH):
            ylin_ref[1 - p, pl.ds(j, TM, stride=NCH), :] = y[:, j * LANES:(j + 1) * LANES]
        for c in range(TM // U):
            sks = [sks_ref[c * U + u] for u in range(U)]
            offs = [pl.multiple_of(s & -NCH, NCH) for s in sks]
            news = [acc_ref[pl.ds(offs[u], NCH), :] + w_ref[sks[u]] * ylin_ref[p, pl.ds((c * U + u) * NCH, NCH), :]
                    for u in range(U)]
            for u in range(U):
                acc_ref[pl.ds(offs[u], NCH), :] = news[u]

    live = t < nt_ref[b] + 2
    pl.when(jnp.logical_and(live, t % 2 == 0))(functools.partial(stage, 0))
    pl.when(jnp.logical_and(live, t % 2 == 1))(functools.partial(stage, 1))

    @pl.when(t >= ntp + 2)
    def _():
        row0 = (t - (ntp + 2)) * (ROW_TILE * NCH)
        for j in range(NCH):
            out_ref[:, j * LANES:(j + 1) * LANES] = acc_ref[pl.ds(row0 + j, ROW_TILE, stride=NCH), :]


def _moe(te, nt, sk, w_flat, hlin, w_gu, w_d):
    B = hlin.shape[0]
    NCH = D // LANES
    n_tok = hlin.shape[1] // NCH
    assert n_tok % ROW_TILE == 0
    nf = n_tok // ROW_TILE
    TM = MOE_TILE
    ntp, _, wlen = _moe_sizes(n_tok)
    pad_tile = ntp - 1

    def gather_tile(b, t, te_ref, nt_ref):
        return b * ntp + jnp.where(t < nt_ref[b], t, pad_tile)

    def scatter_tile(b, t, te_ref, nt_ref):
        return b * ntp + jnp.where(jnp.logical_and(t >= 2, t < nt_ref[b] + 2), t - 2, pad_tile)

    def expert(b, t, te_ref, nt_ref):
        return te_ref[b * META_W + jnp.clip(t - 1, 0, nt_ref[b] - 1)]

    grid_spec = pltpu.PrefetchScalarGridSpec(
        num_scalar_prefetch=2,
        grid=(B, ntp + 2 + nf),
        in_specs=[pl.BlockSpec((TM,), lambda *a: (gather_tile(*a),), memory_space=pltpu.SMEM),
                  pl.BlockSpec((TM,), lambda *a: (scatter_tile(*a),), memory_space=pltpu.SMEM),
                  pl.BlockSpec((wlen,), lambda b, t, *_: (b,), memory_space=pltpu.SMEM),
                  pl.BlockSpec((None, n_tok * NCH, LANES), lambda b, t, *_: (b, 0, 0), pipeline_mode=pl.Buffered(1)),
                  pl.BlockSpec((None, D, 2 * EXP_FF), lambda *a: (expert(*a), 0, 0)),
                  pl.BlockSpec((None, EXP_FF, D), lambda *a: (expert(*a), 0, 0))],
        out_specs=pl.BlockSpec((None, ROW_TILE, D), lambda b, t, *_: (b, jnp.maximum(t - (ntp + 2), 0), 0)),
        scratch_shapes=[pltpu.VMEM(((n_tok + SUBLANES) * NCH, LANES), f32),
                        pltpu.VMEM((2, NCH * (TM + SUBLANES), LANES), f32),
                        pltpu.VMEM((2, TM * NCH, LANES), f32)],
    )
    return pl.pallas_call(
        functools.partial(_moe_kernel, ntp=ntp, n_tok=n_tok),
        grid_spec=grid_spec,
        out_shape=jax.ShapeDtypeStruct((B, n_tok, D), f32),
        compiler_params=_cparams(("arbitrary", "arbitrary")),
        name="moe",
    )(te, nt, sk, sk, w_flat, hlin, w_gu, w_d)


def _routed_experts(e_t, w_t, r_t, cnt, hlin, w_gu, w_d):
    B, K, N = e_t.shape
    _, _, wlen = _moe_sizes(N)
    pos, meta = _plan(e_t, r_t, cnt)
    te = meta[:, 0, :].reshape(-1)
    nt = meta[:, 1, 0]
    tend = meta[:, 2, :N_EXP].reshape(-1)
    sk = _plan_invert(tend, nt, pos, N)
    w_flat = jnp.zeros((B, wlen), f32).at[:, :N * K].set(jnp.swapaxes(w_t, 1, 2).reshape(B, N * K)).reshape(-1)
    return _moe(te, nt, sk, w_flat, hlin, w_gu, w_d)


def _shared_ffn(fin, shgu_ref, shd_ref):
    gu = _dot(fin, shgu_ref[...])
    return _dot((_silu(gu[:, :SH_FF]) * gu[:, SH_FF:]).astype(bf16), shd_ref[...])


def _post_ffn_mla_kernel(x1_ref, routed_ref, fin_ref, shgu_ref, shd_ref, mod0_ref, mod1_ref, g_ref, win_ref,
                         x2_ref, a_ref):
    x2 = x1_ref[...] + mod0_ref[...][:, 5 * D:6 * D] * (routed_ref[...] + _shared_ffn(fin_ref[...], shgu_ref, shd_ref))
    x2_ref[...] = x2
    mod1 = mod1_ref[...]
    h = (_rms(x2) * g_ref[...]) * (1.0 + mod1[:, D:2 * D]) + mod1[:, 0:D]
    a_ref[...] = _dot(h.astype(bf16), win_ref[...])


def _post_ffn_mla(x1, routed, fin, sh_gu, sh_d, modtab0, modtab1, gain, w_in, n_ctx_tiles):
    B, T, _ = x1.shape
    tm = ROW_TILE
    tok = lambda w: pl.BlockSpec((None, tm, w), lambda b, i: (b, i, 0))
    modspec = lambda: pl.BlockSpec((None, None, 1, 6 * D), lambda b, i: (b, jnp.where(i < n_ctx_tiles, 0, 1), 0, 0))
    full = lambda r, c: pl.BlockSpec((r, c), lambda b, i: (0, 0))
    return pl.pallas_call(
        _post_ffn_mla_kernel,
        grid=(B, T // tm),
        in_specs=[tok(D), tok(D), tok(D), full(D, 2 * SH_FF), full(SH_FF, D), modspec(), modspec(),
                  full(1, D), full(D, MLA_IN_PAD)],
        out_specs=[tok(D), tok(MLA_IN_PAD)],
        out_shape=[jax.ShapeDtypeStruct((B, T, D), f32), jax.ShapeDtypeStruct((B, T, MLA_IN_PAD), f32)],
        compiler_params=_cparams(("arbitrary", "arbitrary")),
        name="post_ffn_mla",
    )(x1, routed, fin, sh_gu, sh_d, modtab0, modtab1, gain, w_in)


def _post_ffn_final_kernel(x1_ref, routed_ref, fin_ref, shgu_ref, shd_ref, mod_ref, out_ref):
    out_ref[...] = x1_ref[...] + mod_ref[...][:, 5 * D:6 * D] * (
        routed_ref[...] + _shared_ffn(fin_ref[...], shgu_ref, shd_ref))


def _post_ffn_final(x1, routed, fin, sh_gu, sh_d, modtab):
    B, N, _ = x1.shape
    tm = ROW_TILE
    tok = lambda w: pl.BlockSpec((None, tm, w), lambda b, i: (b, i, 0))
    full = lambda r, c: pl.BlockSpec((r, c), lambda b, i: (0, 0))
    return pl.pallas_call(
        _post_ffn_final_kernel,
        grid=(B, N // tm),
        in_specs=[tok(D), tok(D), tok(D), full(D, 2 * SH_FF), full(SH_FF, D),
                  pl.BlockSpec((None, None, 1, 6 * D), lambda b, i: (b, 1, 0, 0))],
        out_specs=tok(D),
        out_shape=jax.ShapeDtypeStruct((B, N, D), f32),
        compiler_params=_cparams(("arbitrary", "arbitrary")),
        name="post_ffn_final",
    )(x1, routed, fin, sh_gu, sh_d, modtab)


def _mla_prep_kernel(a_ref, qan_ref, wqn_ref, wqr_ref, kvan_ref, wk_ref, wv_ref, qnn_ref, qnr_ref, knn_ref, knr_ref,
                     cos_ref, sin_ref, q_ref, k_ref, v_ref):
    a = a_ref[...]
    tm = a.shape[0]
    scale = MLA_QK ** -0.5
    cos = cos_ref[...]
    sin = sin_ref[...]
    lane = lax.broadcasted_iota(i32, (tm, LANES), 1)
    first = (lane // (MLA_ROPE // 4)) % 2 == 0

    def rope(xb):
        sw = jnp.where(first, pltpu.roll(xb, LANES - MLA_ROPE // 4, axis=1), pltpu.roll(xb, MLA_ROPE // 4, axis=1))
        return xb * cos + sw * sin

    qa = (_rms(a[:, :MLA_QR]) * qan_ref[...]).astype(bf16)
    qn = _dot(qa, wqn_ref[...])
    qr = _dot(qa, wqr_ref[...])
    ri = lax.broadcasted_iota(i32, (MLA_H * MLA_ROPE, MLA_H * MLA_ROPE), 0) // MLA_ROPE
    ci = lax.broadcasted_iota(i32, (MLA_H * MLA_ROPE, MLA_H * MLA_ROPE), 1) // MLA_ROPE
    seg = jnp.where(ri == ci, 1.0, 0.0)
    ssq = jnp.dot(qr * qr, seg, preferred_element_type=f32, precision=lax.Precision.HIGHEST)
    qr = qr * lax.rsqrt(ssq * (1.0 / MLA_ROPE) + EPS) * qnr_ref[...]
    qr_blocks = [rope(qr[:, p * LANES:(p + 1) * LANES]) * scale for p in range(MLA_H // 2)]

    kv = (_rms(a[:, MLA_QR:MLA_QR + MLA_KVR]) * kvan_ref[...]).astype(bf16)
    kn = _dot(kv, wk_ref[...])
    v_ref[...] = _dot(kv, wv_ref[...]).astype(bf16)
    kr = a[:, MLA_QR + MLA_KVR:MLA_IN_PAD]
    kr = rope(_rms(kr, MLA_ROPE) * knr_ref[...])
    kr_odd = pltpu.roll(kr, MLA_ROPE, axis=1)
    for hd in range(MLA_H):
        sl = slice(hd * MLA_NOPE, (hd + 1) * MLA_NOPE)
        q_ref[:, 2 * hd * LANES:(2 * hd + 1) * LANES] = (_rms(qn[:, sl]) * qnn_ref[...] * scale).astype(bf16)
        q_ref[:, (2 * hd + 1) * LANES:(2 * hd + 2) * LANES] = qr_blocks[hd // 2].astype(bf16)
        k_ref[:, 2 * hd * LANES:(2 * hd + 1) * LANES] = (_rms(kn[:, sl]) * knn_ref[...]).astype(bf16)
        k_ref[:, (2 * hd + 1) * LANES:(2 * hd + 2) * LANES] = (kr if hd % 2 == 0 else kr_odd).astype(bf16)


def _mla_prep(a, qan, wqn, wqr, kvan, wk, wv, qnn, qnr, knn, knr, cos_t, sin_t):
    B, T, _ = a.shape
    tm = ROW_TILE
    tok = lambda w: pl.BlockSpec((None, tm, w), lambda b, i: (b, i, 0))
    full = lambda r, c: pl.BlockSpec((r, c), lambda b, i: (0, 0))
    hw = 2 * LANES * MLA_H
    return pl.pallas_call(
        _mla_prep_kernel,
        grid=(B, T // tm),
        in_specs=[tok(MLA_IN_PAD), full(1, MLA_QR), full(MLA_QR, MLA_H * MLA_NOPE), full(MLA_QR, MLA_H * MLA_ROPE),
                  full(1, MLA_KVR), full(MLA_KVR, MLA_H * MLA_NOPE), full(MLA_KVR, MLA_H * MLA_V),
                  full(1, MLA_NOPE), full(1, MLA_H * MLA_ROPE), full(1, MLA_NOPE), full(1, LANES),
                  pl.BlockSpec((tm, LANES), lambda b, i: (i, 0)), pl.BlockSpec((tm, LANES), lambda b, i: (i, 0))],
        out_specs=[tok(hw), tok(hw), tok(MLA_H * MLA_V)],
        out_shape=[jax.ShapeDtypeStruct((B, T, hw), bf16), jax.ShapeDtypeStruct((B, T, hw), bf16),
                   jax.ShapeDtypeStruct((B, T, MLA_H * MLA_V), bf16)],
        compiler_params=_cparams(("arbitrary", "arbitrary")),
        name="mla_prep",
    )(a, qan, wqn, wqr, kvan, wk, wv, qnn, qnr, knn, knr, cos_t, sin_t)


def _mla_attn_kernel(q_ref, k_ref, v_ref, o_ref):
    s = _dot_nt(q_ref[...], k_ref[...])
    m = jnp.max(s, axis=-1, keepdims=True)
    p = jnp.exp(s - m)
    l = jnp.sum(p, axis=-1, keepdims=True)
    o_ref[...] = (_dot(p.astype(bf16), v_ref[...]) / l).astype(bf16)


def _mla_attn(q, k, v, n_ctx_tiles):
    B, T, _ = q.shape
    tq = ROW_TILE
    S = T - n_ctx_tiles * tq
    return pl.pallas_call(
        _mla_attn_kernel,
        grid=(B, MLA_H, S // tq),
        in_specs=[pl.BlockSpec((None, tq, 2 * LANES), lambda b, h, i: (b, i + n_ctx_tiles, h)),
                  pl.BlockSpec((None, T, 2 * LANES), lambda b, h, i: (b, 0, h)),
                  pl.BlockSpec((None, T, MLA_V), lambda b, h, i: (b, 0, h))],
        out_specs=pl.BlockSpec((None, tq, MLA_V), lambda b, h, i: (b, i, h)),
        out_shape=jax.ShapeDtypeStruct((B, S, MLA_H * MLA_V), bf16),
        compiler_params=_cparams(("arbitrary", "arbitrary", "arbitrary")),
        name="mla_attn",
    )(q, k, v)


def _axial_angles(rows_n, rot_dim):
    axis_dim = rot_dim // 2
    inv = ROPE_BASE ** (-jnp.arange(0, axis_dim, 2, dtype=f32) / axis_dim)
    row = jnp.repeat(jnp.arange(rows_n, dtype=f32), GRID_W)
    col = jnp.tile(jnp.arange(GRID_W, dtype=f32), rows_n)
    return row[:, None] * inv, col[:, None] * inv


def _rope_tables(seq, n_ctx, rot_dim, reps):
    ang_r, ang_c = _axial_angles(seq // GRID_W, rot_dim)
    cos = jnp.concatenate([jnp.cos(ang_r)] * 2 + [jnp.cos(ang_c)] * 2, axis=1)
    sin = jnp.concatenate([-jnp.sin(ang_r), jnp.sin(ang_r), -jnp.sin(ang_c), jnp.sin(ang_c)], axis=1)
    cos = jnp.concatenate([jnp.ones((n_ctx, rot_dim), f32), cos], axis=0)
    sin = jnp.concatenate([jnp.zeros((n_ctx, rot_dim), f32), sin], axis=0)
    return jnp.tile(cos, (1, reps)), jnp.tile(sin, (1, reps))


def kernel(x, c, ctx, c_ctx, ada_w, ada_b, norm_mix, norm_ffn, ret_w_in, ret_decay_f, ret_decay_b, ret_w_o,
           mla_w_in, mla_q_a_norm, mla_w_q_b, mla_kv_a_norm, mla_w_kv_b, mla_q_norm, mla_k_norm, mla_w_o,
           router_w, router_bias, exp_w_gu, exp_w_down, sh_w_gu, sh_w_down):
    B, S, _ = x.shape
    n_ctx = ctx.shape[1]
    assert n_ctx % ROW_TILE == 0 and S % ROW_TILE == 0 and S % GRID_W == 0
    n_ctx_tiles = n_ctx // ROW_TILE

    rows = -(-(B + 1) // SUBLANES) * SUBLANES
    cc = jnp.zeros((rows, D), f32).at[:B].set(c).at[B].set(c_ctx)
    mod = _ada(cc, ada_w, ada_b)

    def modtab(i):
        ctx_row = jnp.broadcast_to(mod[i, B][None, :], (B, 6 * D))
        return jnp.stack([ctx_row, mod[i, :B]], axis=1)[:, :, None, :]

    mod0, mod1 = modtab(0), modtab(1)
    xc = jnp.concatenate([ctx, x], axis=1)

    cos_r, sin_r = _rope_tables(S, n_ctx, RET_DK, 1)
    q, k, v, gf, gb = _ret_inproj(xc, mod0, norm_mix[0][None, :], ret_w_in[0].astype(bf16), cos_r, sin_r, n_ctx_tiles)
    dtab = jnp.broadcast_to(jnp.concatenate([ret_decay_f[0], ret_decay_b[0]])[:, None], (2 * RET_H, LANES))
    o = _ret_scan(dtab, q, k, v, gf, gb, n_ctx)
    x1, fin, hlin, e_t, w_t, r_t, cnt = _post_mix(
        o, ret_w_o[0].astype(bf16), xc, 0, mod0, n_ctx_tiles, norm_ffn[0][None, :],
        router_w[0].T, router_bias[0][:, None])
    routed = _routed_experts(e_t, w_t, r_t, cnt, hlin, exp_w_gu[0].astype(bf16), exp_w_down[0].astype(bf16))
    w_in1 = jnp.zeros((D, MLA_IN_PAD), f32).at[:, :mla_w_in.shape[2]].set(mla_w_in[0]).astype(bf16)
    x2, a = _post_ffn_mla(x1, routed, fin, sh_w_gu[0].astype(bf16), sh_w_down[0].astype(bf16), mod0, mod1,
                          norm_mix[1][None, :], w_in1, n_ctx_tiles)

    wq = mla_w_q_b[0].reshape(MLA_QR, MLA_H, MLA_QK)
    wqn = wq[:, :, :MLA_NOPE].reshape(MLA_QR, MLA_H * MLA_NOPE).astype(bf16)
    wqr = wq[:, :, MLA_NOPE:].reshape(MLA_QR, MLA_H * MLA_ROPE).astype(bf16)
    wkv = mla_w_kv_b[0].reshape(MLA_KVR, MLA_H, MLA_NOPE + MLA_V)
    wk = wkv[:, :, :MLA_NOPE].reshape(MLA_KVR, MLA_H * MLA_NOPE).astype(bf16)
    wv = wkv[:, :, MLA_NOPE:].reshape(MLA_KVR, MLA_H * MLA_V).astype(bf16)
    qnn = mla_q_norm[0][None, :MLA_NOPE]
    qnr = jnp.tile(mla_q_norm[0][None, MLA_NOPE:], (1, MLA_H))
    knn = mla_k_norm[0][None, :MLA_NOPE]
    knr = jnp.concatenate([mla_k_norm[0][MLA_NOPE:], jnp.zeros((LANES - MLA_ROPE,), f32)])[None, :]
    cos_m, sin_m = _rope_tables(S, n_ctx, MLA_ROPE, LANES // MLA_ROPE)
    qf, kf, vf = _mla_prep(a, mla_q_a_norm[0][None, :], wqn, wqr, mla_kv_a_norm[0][None, :], wk, wv,
                           qnn, qnr, knn, knr, cos_m, sin_m)
    o1 = _mla_attn(qf, kf, vf, n_ctx_tiles)
    x3, fin1, hlin1, e1, w1, r1, cnt1 = _post_mix(
        o1, mla_w_o[0].astype(bf16), x2, n_ctx_tiles, mod1, 0, norm_ffn[1][None, :],
        router_w[1].T, router_bias[1][:, None])
    routed1 = _routed_experts(e1, w1, r1, cnt1, hlin1, exp_w_gu[1].astype(bf16), exp_w_down[1].astype(bf16))
    return _post_ffn_final(x3, routed1, fin1, sh_w_gu[1].astype(bf16), sh_w_down[1].astype(bf16), mod1)
```

```python
import functools

import jax
import jax.numpy as jnp
import numpy as np
from jax import lax
from jax.experimental import pallas as pl
from jax.experimental.pallas import tpu as pltpu

f32 = jnp.float32
bf16 = jnp.bfloat16
i32 = jnp.int32

D = 1024
GRID_W = 64
EPS = 1e-6
ROPE_BASE = 10000.0
RET_H = 4
RET_DK = 256
RET_DV = 512
RET_VW = RET_H * RET_DV
RET_CHUNK = 256
MLA_H = 8
MLA_NOPE = 128
MLA_ROPE = 64
MLA_QK = MLA_NOPE + MLA_ROPE
MLA_V = 128
MLA_QR = 384
MLA_KVR = 256
MLA_IN_PAD = 768
N_EXP = 64
TOP_K = 8
N_GRP = 8
TOPK_GRP = 4
EXP_FF = 256
SH_FF = 256
ROUTED_SCALE = 2.5

LANES = 128
SUBLANES = 8
ROW_TILE = 256
MOE_TILE = 256
VMEM_LIMIT = 56 * 1024 * 1024


def _cparams(sem, vmem=VMEM_LIMIT):
    return pltpu.CompilerParams(dimension_semantics=sem, vmem_limit_bytes=vmem)


def _sigmoid(x):
    return 1.0 / (1.0 + jnp.exp(-x))


def _silu(x):
    return x * _sigmoid(x)


def _rms(x, n=None):
    n = x.shape[-1] if n is None else n
    return x * lax.rsqrt(jnp.sum(x * x, axis=-1, keepdims=True) * (1.0 / n) + EPS)


def _dot(a, b):
    return jnp.dot(a, b, preferred_element_type=f32)


def _dot_nt(a, b, precision=None):
    return lax.dot_general(a, b, (((1,), (1,)), ((), ())), preferred_element_type=f32, precision=precision)


def _ada_kernel(c_ref, w_ref, b_ref, o_ref):
    s = _silu(c_ref[...]).astype(bf16)
    o_ref[...] = _dot(s, w_ref[...].astype(bf16)) + b_ref[...]


def _ada(cc, ada_w, ada_b):
    depth = ada_w.shape[0]
    rows = cc.shape[0]
    tn = 1536
    return pl.pallas_call(
        _ada_kernel,
        grid=(depth, 6 * D // tn),
        in_specs=[pl.BlockSpec((rows, D), lambda i, j: (0, 0)),
                  pl.BlockSpec((None, D, tn), lambda i, j: (i, 0, j)),
                  pl.BlockSpec((None, 1, tn), lambda i, j: (i, 0, j))],
        out_specs=pl.BlockSpec((None, rows, tn), lambda i, j: (i, 0, j)),
        out_shape=jax.ShapeDtypeStruct((depth, rows, 6 * D), f32),
        compiler_params=_cparams(("arbitrary", "arbitrary")),
        name="ada",
    )(cc, ada_w, ada_b.reshape(depth, 1, 6 * D))


def _ret_inproj_kernel(x_ref, mod_ref, g_ref, w_ref, cos_ref, sin_ref, q_ref, k_ref, v_ref, gf_ref, gb_ref):
    x = x_ref[...]
    mod = mod_ref[...]
    h = (_rms(x) * g_ref[...]) * (1.0 + mod[:, D:2 * D]) + mod[:, 0:D]
    hb = h.astype(bf16)
    cos = cos_ref[...]
    sin = sin_ref[...]

    def rope(a):
        outs = []
        for half in range(2):
            sl = slice(half * LANES, (half + 1) * LANES)
            ah = a[:, sl]
            outs.append(ah * cos[:, sl] + pltpu.roll(ah, LANES // 2, axis=1) * sin[:, sl])
        return jnp.concatenate(outs, axis=1)

    for hd in range(RET_H):
        sl = slice(hd * RET_DK, (hd + 1) * RET_DK)
        q_ref[:, sl] = rope(_dot(hb, w_ref[:, sl])).astype(bf16)
    for hd in range(RET_H):
        sl = slice(hd * RET_DK, (hd + 1) * RET_DK)
        wsl = slice(D + hd * RET_DK, D + (hd + 1) * RET_DK)
        k_ref[:, sl] = (rope(_dot(hb, w_ref[:, wsl])) * (RET_DK ** -0.5)).astype(bf16)
    cw = 512
    for c in range(RET_VW // cw):
        sl = slice(c * cw, (c + 1) * cw)
        v_ref[:, sl] = _dot(hb, w_ref[:, 2 * D + c * cw:2 * D + (c + 1) * cw]).astype(bf16)
        gf_ref[:, sl] = _silu(_dot(hb, w_ref[:, 2 * D + RET_VW + c * cw:2 * D + RET_VW + (c + 1) * cw])).astype(bf16)
        gb_ref[:, sl] = _silu(_dot(hb, w_ref[:, 2 * D + 2 * RET_VW + c * cw:2 * D + 2 * RET_VW + (c + 1) * cw])).astype(bf16)


def _ret_inproj(xc, modtab, gain, w_in, cos_t, sin_t, n_ctx_tiles):
    B, T, _ = xc.shape
    tm = ROW_TILE
    n_in = w_in.shape[1]
    tok = lambda w: pl.BlockSpec((None, tm, w), lambda b, i: (b, i, 0))
    return pl.pallas_call(
        _ret_inproj_kernel,
        grid=(B, T // tm),
        in_specs=[tok(D),
                  pl.BlockSpec((None, None, 1, 6 * D), lambda b, i: (b, jnp.where(i < n_ctx_tiles, 0, 1), 0, 0)),
                  pl.BlockSpec((1, D), lambda b, i: (0, 0)),
                  pl.BlockSpec((D, n_in), lambda b, i: (0, 0), pipeline_mode=pl.Buffered(1)),
                  pl.BlockSpec((tm, RET_DK), lambda b, i: (i, 0)),
                  pl.BlockSpec((tm, RET_DK), lambda b, i: (i, 0))],
        out_specs=[tok(D), tok(D), tok(RET_VW), tok(RET_VW), tok(RET_VW)],
        out_shape=[jax.ShapeDtypeStruct((B, T, D), bf16), jax.ShapeDtypeStruct((B, T, D), bf16),
                   jax.ShapeDtypeStruct((B, T, RET_VW), bf16), jax.ShapeDtypeStruct((B, T, RET_VW), bf16),
                   jax.ShapeDtypeStruct((B, T, RET_VW), bf16)],
        compiler_params=_cparams(("arbitrary", "arbitrary")),
        name="ret_inproj",
    )(xc, modtab, gain, w_in, cos_t, sin_t)


def _ret_chunk_index(t, nc, ncc):
    u = t - nc
    back = jnp.where(u < ncc, ncc - 1 - u, nc - 1 - u + ncc)
    return jnp.where(t < nc, t, back)


def _ret_scan_kernel(dt_ref, q_ref, k_ref, v_ref, gf_ref, gb_ref, o_ref,
                     s_ref, of_ref, mask_ref, dq_ref, dk_ref, dc_ref, *, nc, ncc):
    t = pl.program_id(1)
    C = RET_CHUNK

    def init(direction):
        s_ref[...] = jnp.zeros_like(s_ref)
        ii = lax.broadcasted_iota(i32, (C, C), 0)
        jj = lax.broadcasted_iota(i32, (C, C), 1)
        rel = (ii - jj if direction == 0 else jj - ii).astype(f32)
        pos = lax.broadcasted_iota(i32, (C, 1), 0).astype(f32)
        for hd in range(RET_H):
            r = direction * RET_H + hd
            lg = -jnp.exp(dt_ref[r:r + 1, :])
            lg1 = lg[:, 0:1]
            mask_ref[hd] = jnp.where(rel >= 0, jnp.exp(lg1 * jnp.maximum(rel, 0.0)), 0.0)
            if direction == 0:
                dq_ref[hd] = jnp.exp(lg1 * (pos + 1.0))
                dk_ref[hd] = jnp.exp(lg1 * (C - 1.0 - pos))
            else:
                dq_ref[hd] = jnp.exp(lg1 * (C - pos))
                dk_ref[hd] = jnp.exp(lg1 * pos)
            dc_ref[hd] = jnp.exp(lg * float(C))

    pl.when(t == 0)(functools.partial(init, 0))
    pl.when(t == nc)(functools.partial(init, 1))

    fwd = t < nc
    row0 = pl.multiple_of(_ret_chunk_index(t, nc, ncc) * C, C)

    for hd in range(RET_H):
        ks = slice(hd * RET_DK, (hd + 1) * RET_DK)
        vs = slice(hd * RET_DV, (hd + 1) * RET_DV)
        qh = q_ref[:, ks]
        kh = k_ref[:, ks]
        vh = v_ref[:, vs]
        p = (_dot_nt(qh, kh) * mask_ref[hd]).astype(bf16)
        y = _dot(p, vh) + _dot(qh, s_ref[hd].astype(bf16)) * dq_ref[hd]
        kd = (kh.astype(f32) * dk_ref[hd]).astype(bf16)
        upd = lax.dot_general(kd, vh, (((0,), (0,)), ((), ())), preferred_element_type=f32)
        s_ref[hd] = s_ref[hd] * dc_ref[hd][0:1, 0:1] + upd
        yn = _rms(y)

        @pl.when(fwd)
        def _():
            of_ref[pl.ds(row0, C), vs] = (gf_ref[:, vs].astype(f32) * yn).astype(bf16)

        @pl.when(jnp.logical_not(fwd))
        def _():
            o_ref[:, vs] = (of_ref[pl.ds(row0, C), vs].astype(f32) + gb_ref[:, vs].astype(f32) * yn).astype(bf16)


def _ret_scan(dtab, q, k, v, gf, gb, n_ctx):
    B, T, _ = q.shape
    C = RET_CHUNK
    nc = T // C
    ncc = n_ctx // C
    cidx = functools.partial(_ret_chunk_index, nc=nc, ncc=ncc)
    first_back = ncc - 1
    return pl.pallas_call(
        functools.partial(_ret_scan_kernel, nc=nc, ncc=ncc),
        grid=(B, 2 * nc),
        in_specs=[pl.BlockSpec((2 * RET_H, LANES), lambda b, t: (0, 0)),
                  pl.BlockSpec((None, C, D), lambda b, t: (b, cidx(t), 0)),
                  pl.BlockSpec((None, C, D), lambda b, t: (b, cidx(t), 0)),
                  pl.BlockSpec((None, C, RET_VW), lambda b, t: (b, cidx(t), 0)),
                  pl.BlockSpec((None, C, RET_VW), lambda b, t: (b, jnp.where(t < nc, t, nc - 1), 0)),
                  pl.BlockSpec((None, C, RET_VW), lambda b, t: (b, jnp.where(t < nc, first_back, cidx(t)), 0))],
        out_specs=pl.BlockSpec((None, C, RET_VW), lambda b, t: (b, jnp.where(t < nc, first_back, cidx(t)), 0)),
        out_shape=jax.ShapeDtypeStruct((B, T, RET_VW), bf16),
        scratch_shapes=[pltpu.VMEM((RET_H, RET_DK, RET_DV), f32),
                        pltpu.VMEM((T, RET_VW), bf16),
                        pltpu.VMEM((RET_H, C, C), f32),
                        pltpu.VMEM((RET_H, C, 1), f32),
                        pltpu.VMEM((RET_H, C, 1), f32),
                        pltpu.VMEM((RET_H, 1, LANES), f32)],
        compiler_params=_cparams(("arbitrary", "arbitrary")),
        name="ret_scan",
    )(dtab, q, k, v, gf, gb)


def _route(f, rwt_ref, rb_ref, cnt_ref, e_ref, w_ref, r_ref):
    tm = f.shape[0]
    G = N_EXP // N_GRP
    logits = _dot_nt(rwt_ref[...], f, precision=lax.Precision.HIGHEST)
    s = _sigmoid(logits)
    sel = s + rb_ref[...]
    mi = lax.broadcasted_iota(i32, (G, tm), 0)
    neg = -jnp.inf
    s_g = [s[g * G:(g + 1) * G, :] for g in range(N_GRP)]
    sel_g = [sel[g * G:(g + 1) * G, :] for g in range(N_GRP)]

    def first_max(a, ids, big):
        mx = jnp.max(a, axis=0, keepdims=True)
        ix = jnp.min(jnp.where(a == mx, ids, big), axis=0, keepdims=True)
        return mx, ix

    gscore = jnp.zeros((N_GRP, tm), f32)
    gi = lax.broadcasted_iota(i32, (N_GRP, tm), 0)
    for g in range(N_GRP):
        t1, i1 = first_max(sel_g[g], mi, G)
        t2 = jnp.max(jnp.where(mi == i1, neg, sel_g[g]), axis=0, keepdims=True)
        gscore = jnp.where(gi == g, t1 + t2, gscore)
    gmask = jnp.zeros((N_GRP, tm), i32)
    cur = gscore
    for _ in range(TOPK_GRP):
        _, ix = first_max(cur, gi, N_GRP)
        hit = gi == ix
        gmask = jnp.where(hit, 1, gmask)
        cur = jnp.where(hit, neg, cur)
    cand = [jnp.where(gmask[g:g + 1, :] > 0, sel_g[g], neg) for g in range(N_GRP)]
    ids = [mi + g * G for g in range(N_GRP)]

    e_rows, w_rows = [], []
    for _ in range(TOP_K):
        mx = cand[0].max(axis=0, keepdims=True)
        for g in range(1, N_GRP):
            mx = jnp.maximum(mx, cand[g].max(axis=0, keepdims=True))
        ix = jnp.min(jnp.where(cand[0] == mx, ids[0], N_EXP), axis=0, keepdims=True)
        for g in range(1, N_GRP):
            ix = jnp.minimum(ix, jnp.min(jnp.where(cand[g] == mx, ids[g], N_EXP), axis=0, keepdims=True))
        wv = jnp.zeros((1, tm), f32)
        for g in range(N_GRP):
            hit = ids[g] == ix
            cand[g] = jnp.where(hit, neg, cand[g])
            wv = wv + jnp.sum(jnp.where(hit, s_g[g], 0.0), axis=0, keepdims=True)
        e_rows.append(ix)
        w_rows.append(wv)
    wsum = w_rows[0]
    for r in range(1, TOP_K):
        wsum = wsum + w_rows[r]

    selm = [jnp.zeros((G, tm), f32) for _ in range(N_GRP)]
    for r in range(TOP_K):
        for g in range(N_GRP):
            selm[g] = jnp.where(ids[g] == e_rows[r], 1.0, selm[g])
    m_all = jnp.concatenate(selm, axis=0)
    ri = lax.broadcasted_iota(i32, (tm, tm), 0)
    ci = lax.broadcasted_iota(i32, (tm, tm), 1)
    upper = jnp.where(ri <= ci, 1.0, 0.0).astype(bf16)
    incl = _dot(m_all.astype(bf16), upper)
    carry = cnt_ref[:, 0:1]
    rank_all = carry + incl - m_all
    cnt_ref[...] = cnt_ref[...] + incl[:, tm - 1:tm]
    for r in range(TOP_K):
        rk = jnp.zeros((1, tm), f32)
        for g in range(N_GRP):
            rk = rk + jnp.sum(jnp.where(ids[g] == e_rows[r], rank_all[g * G:(g + 1) * G, :], 0.0), axis=0, keepdims=True)
        e_ref[r:r + 1, :] = e_rows[r]
        w_ref[r:r + 1, :] = w_rows[r] / wsum * ROUTED_SCALE
        r_ref[r:r + 1, :] = rk.astype(i32)


def _post_mix_kernel(o_ref, wo_ref, x_ref, mod_ref, g_ref, rwt_ref, rb_ref,
                     x1_ref, fin_ref, hlin_ref, e_ref, w_ref, r_ref, cnt_ref, *, nt, o_transposed):
    i = pl.program_id(1)

    @pl.when(i == 0)
    def _():
        cnt_ref[...] = jnp.zeros_like(cnt_ref)

    @pl.when(i < nt)
    def _():
        mod = mod_ref[...]
        if o_transposed:
            mixed = lax.dot_general(o_ref[...], wo_ref[...], (((0,), (0,)), ((), ())), preferred_element_type=f32)
        else:
            mixed = _dot(o_ref[...], wo_ref[...])
        x1 = x_ref[...] + mod[:, 2 * D:3 * D] * mixed
        x1_ref[...] = x1
        f = (_rms(x1) * g_ref[...]) * (1.0 + mod[:, 4 * D:5 * D]) + mod[:, 3 * D:4 * D]
        fin_ref[...] = f.astype(bf16)
        tm = f.shape[0]
        for j in range(D // LANES):
            hlin_ref[pl.ds(j, tm, stride=D // LANES), :] = f[:, j * LANES:(j + 1) * LANES]
        _route(f, rwt_ref, rb_ref, cnt_ref, e_ref, w_ref, r_ref)

    @pl.when(i == nt)
    def _():
        hlin_ref[...] = jnp.zeros_like(hlin_ref)


def _post_mix(o, w_o, xs, x_tile_off, modtab, n_ctx_tiles, gain, rwt, rb, o_transposed=False):
    if o_transposed:
        B, KO, N = o.shape
    else:
        B, N, KO = o.shape
    tm = ROW_TILE
    nt = N // tm
    last = lambda i: jnp.minimum(i, nt - 1)
    tok = lambda w: pl.BlockSpec((None, tm, w), lambda b, i: (b, last(i), 0))
    sel = lambda: pl.BlockSpec((None, TOP_K, tm), lambda b, i: (b, 0, last(i)))
    o_spec = (pl.BlockSpec((None, KO, tm), lambda b, i: (b, 0, last(i))) if o_transposed else tok(KO))
    return pl.pallas_call(
        functools.partial(_post_mix_kernel, nt=nt, o_transposed=o_transposed),
        grid=(B, nt + 1),
        in_specs=[o_spec,
                  pl.BlockSpec((KO, D), lambda b, i: (0, 0)),
                  pl.BlockSpec((None, tm, D), lambda b, i: (b, last(i) + x_tile_off, 0)),
                  pl.BlockSpec((None, None, 1, 6 * D), lambda b, i: (b, jnp.where(last(i) < n_ctx_tiles, 0, 1), 0, 0)),
                  pl.BlockSpec((1, D), lambda b, i: (0, 0)),
                  pl.BlockSpec((N_EXP, D), lambda b, i: (0, 0)),
                  pl.BlockSpec((N_EXP, 1), lambda b, i: (0, 0))],
        out_specs=[tok(D), tok(D),
                   pl.BlockSpec((None, tm * (D // LANES), LANES), lambda b, i: (b, i, 0)),
                   sel(), sel(), sel(),
                   pl.BlockSpec((None, N_EXP, LANES), lambda b, i: (b, 0, 0))],
        out_shape=[jax.ShapeDtypeStruct((B, N, D), f32), jax.ShapeDtypeStruct((B, N, D), bf16),
                   jax.ShapeDtypeStruct((B, (N + tm) * (D // LANES), LANES), f32),
                   jax.ShapeDtypeStruct((B, TOP_K, N), i32), jax.ShapeDtypeStruct((B, TOP_K, N), f32),
                   jax.ShapeDtypeStruct((B, TOP_K, N), i32),
                   jax.ShapeDtypeStruct((B, N_EXP, LANES), f32)],
        compiler_params=_cparams(("arbitrary", "arbitrary")),
        name="post_mix",
    )(o, w_o, xs, modtab, gain, rwt, rb)


META_W = 256
PLAN_ALIGN = 1024


def _round_up(n, m):
    return -(-n // m) * m


def _moe_sizes(n_tok):
    tm = MOE_TILE
    nt_max = (n_tok * TOP_K + N_EXP * (tm - 1)) // tm + 1
    ntp = _round_up(nt_max + 1, PLAN_ALIGN // tm)
    assert ntp <= META_W
    return ntp, _round_up(n_tok, PLAN_ALIGN), _round_up(n_tok * TOP_K + SUBLANES, PLAN_ALIGN)


def _plan_kernel(e_ref, r_ref, cnt_ref, pos_ref, meta_ref, *, n_tok):
    tm = MOE_TILE
    ntile = jnp.floor((cnt_ref[...] + (tm - 1.0)) * (1.0 / tm))
    ntb = ntile.astype(bf16)
    ei = lax.broadcasted_iota(i32, (N_EXP, LANES), 0)
    ej = lax.broadcasted_iota(i32, (N_EXP, LANES), 1)
    lower = jnp.where(ej <= ei, 1.0, 0.0)[:, :N_EXP].astype(bf16)
    tend = _dot(lower, ntb)
    tstart = tend - ntile
    upper = jnp.where(ei <= ej, 1.0, 0.0).astype(bf16)
    tend_row = lax.dot_general(ntb, upper, (((0,), (0,)), ((), ())), preferred_element_type=f32)[0:1, :]
    tt = lax.broadcasted_iota(i32, (N_EXP, META_W), 1).astype(f32)
    te = jnp.sum(jnp.where(tt >= tend[:, 0:1], 1.0, 0.0), axis=0, keepdims=True)
    meta_ref[...] = jnp.zeros_like(meta_ref)
    meta_ref[0:1, :] = jnp.minimum(te, N_EXP - 1.0).astype(i32)
    meta_ref[1:2, :] = jnp.broadcast_to(tend[N_EXP - 1:N_EXP, 0:1], (1, META_W)).astype(i32)
    meta_ref[2:3, 0:LANES] = tend_row.astype(i32)
    e = e_ref[...]
    base = jnp.zeros(e.shape, f32)
    for ex in range(N_EXP):
        base = jnp.where(e == ex, tstart[ex:ex + 1, 0:1] * float(tm), base)
    pos_ref[...] = jnp.zeros_like(pos_ref)
    pos_ref[:, 0:n_tok] = base.astype(i32) + r_ref[...]


def _plan(e_t, r_t, cnt):
    B, K, N = e_t.shape
    _, npad, _ = _moe_sizes(N)
    return pl.pallas_call(
        functools.partial(_plan_kernel, n_tok=N),
        grid=(B,),
        in_specs=[pl.BlockSpec((None, K, N), lambda b: (b, 0, 0)),
                  pl.BlockSpec((None, K, N), lambda b: (b, 0, 0)),
                  pl.BlockSpec((None, N_EXP, LANES), lambda b: (b, 0, 0))],
        out_specs=[pl.BlockSpec((None, K, npad), lambda b: (b, 0, 0)),
                   pl.BlockSpec((None, SUBLANES, META_W), lambda b: (b, 0, 0))],
        out_shape=[jax.ShapeDtypeStruct((B, K, npad), i32), jax.ShapeDtypeStruct((B, SUBLANES, META_W), i32)],
        compiler_params=_cparams(("arbitrary",)),
        name="moe_plan",
    )(e_t, r_t, cnt)


def _plan_invert_kernel(tend_ref, nt_ref, pos_ref, sk_ref, *, n_tok, ntp):
    b = pl.program_id(0)
    k = pl.program_id(1)
    tm = MOE_TILE
    U = 8
    pad = n_tok * TOP_K

    def fill_tile(tile):
        def body(c, carry):
            for u in range(U):
                sk_ref[tile * tm + c * U + u] = pad
            return carry
        lax.fori_loop(0, tm // U, body, 0)

    @pl.when(k == 0)
    def _():
        def per_expert(ex, carry):
            fill_tile(jnp.maximum(tend_ref[b * N_EXP + ex] - 1, 0))
            return carry
        lax.fori_loop(0, N_EXP, per_expert, 0)

        def tail(tile, carry):
            fill_tile(tile)
            return carry
        lax.fori_loop(nt_ref[b], ntp, tail, 0)

    def body(c, carry):
        n0 = c * U
        v0 = c * (U * TOP_K) + k
        ps = [pos_ref[n0 + u] for u in range(U)]
        for u in range(U):
            sk_ref[ps[u]] = v0 + u * TOP_K
        return carry
    lax.fori_loop(0, n_tok // U, body, 0)


def _plan_invert(tend, nt, pos, n_tok):
    B, K, npad = pos.shape
    ntp, _, _ = _moe_sizes(n_tok)
    plen = ntp * MOE_TILE
    grid_spec = pltpu.PrefetchScalarGridSpec(
        num_scalar_prefetch=2,
        grid=(B, K),
        in_specs=[pl.BlockSpec((npad,), lambda b, k, *_: (b * K + k,), memory_space=pltpu.SMEM)],
        out_specs=pl.BlockSpec((plen,), lambda b, k, *_: (b,), memory_space=pltpu.SMEM),
    )
    return pl.pallas_call(
        functools.partial(_plan_invert_kernel, n_tok=n_tok, ntp=ntp),
        grid_spec=grid_spec,
        out_shape=jax.ShapeDtypeStruct((B * plen,), i32),
        compiler_params=_cparams(("arbitrary", "arbitrary")),
        name="moe_invert",
    )(tend, nt, pos.reshape(-1))


def _moe_kernel(te_ref, nt_ref, skg_ref, sks_ref, w_ref, hlin_ref, wgu_ref, wd_ref, out_ref,
                acc_ref, xs_ref, ylin_ref, *, ntp, n_tok):
    b = pl.program_id(0)
    t = pl.program_id(1)
    TM = MOE_TILE
    NCH = D // LANES
    U = 8

    @pl.when(t == 0)
    def _():
        acc_ref[...] = jnp.zeros_like(acc_ref)
        xs_ref[...] = jnp.zeros_like(xs_ref)
        ylin_ref[...] = jnp.zeros_like(ylin_ref)

    def stage(p):
        for m in range(TM):
            sk = skg_ref[m]
            off = pl.multiple_of(sk & -NCH, NCH)
            xs_ref[p, pl.ds(m * NCH, NCH), :] = hlin_ref[pl.ds(off, NCH), :]
        x = jnp.concatenate([xs_ref[1 - p, pl.ds(j, TM, stride=NCH), :] for j in range(NCH)], axis=1).astype(bf16)
        gu = _dot(x, wgu_ref[...])
        a = (_silu(gu[:, :EXP_FF]) * gu[:, EXP_FF:]).astype(bf16)
        y = _dot(a, wd_ref[...])
        for g in range(TM // SUBLANES):
            for j in range(NCH):
                ylin_ref[1 - p, pl.ds((g * NCH + j) * SUBLANES, SUBLANES), :] = (
                    y[g * SUBLANES:(g + 1) * SUBLANES, j * LANES:(j + 1) * LANES])
        for c in range(TM // U):
            sks = [sks_ref[c * U + u] for u in range(U)]
            offs = [pl.multiple_of(s & -NCH, NCH) for s in sks]
            news = []
            for u in range(U):
                m = c * U + u
                yrow = ylin_ref[p, pl.ds((m // SUBLANES) * (NCH * SUBLANES) + m % SUBLANES, NCH, stride=SUBLANES), :]
                news.append(acc_ref[pl.ds(offs[u], NCH), :] + w_ref[sks[u]] * yrow)
            for u in range(U):
                acc_ref[pl.ds(offs[u], NCH), :] = news[u]

    live = t < nt_ref[b] + 2
    pl.when(jnp.logical_and(live, t % 2 == 0))(functools.partial(stage, 0))
    pl.when(jnp.logical_and(live, t % 2 == 1))(functools.partial(stage, 1))

    @pl.when(t >= ntp + 2)
    def _():
        row0 = (t - (ntp + 2)) * (ROW_TILE * NCH)
        for j in range(NCH):
            out_ref[:, j * LANES:(j + 1) * LANES] = acc_ref[pl.ds(row0 + j, ROW_TILE, stride=NCH), :]


def _moe(te, nt, sk, w_flat, hlin, w_gu, w_d):
    B = hlin.shape[0]
    NCH = D // LANES
    n_tok = hlin.shape[1] // NCH - ROW_TILE
    assert n_tok % ROW_TILE == 0
    nf = n_tok // ROW_TILE
    TM = MOE_TILE
    ntp, _, wlen = _moe_sizes(n_tok)
    pad_tile = ntp - 1

    def gather_tile(b, t, te_ref, nt_ref):
        return b * ntp + jnp.where(t < nt_ref[b], t, pad_tile)

    def scatter_tile(b, t, te_ref, nt_ref):
        return b * ntp + jnp.where(jnp.logical_and(t >= 2, t < nt_ref[b] + 2), t - 2, pad_tile)

    def expert(b, t, te_ref, nt_ref):
        return te_ref[b * META_W + jnp.clip(t - 1, 0, nt_ref[b] - 1)]

    grid_spec = pltpu.PrefetchScalarGridSpec(
        num_scalar_prefetch=2,
        grid=(B, ntp + 2 + nf),
        in_specs=[pl.BlockSpec((TM,), lambda *a: (gather_tile(*a),), memory_space=pltpu.SMEM),
                  pl.BlockSpec((TM,), lambda *a: (scatter_tile(*a),), memory_space=pltpu.SMEM),
                  pl.BlockSpec((wlen,), lambda b, t, *_: (b,), memory_space=pltpu.SMEM),
                  pl.BlockSpec((None, (n_tok + ROW_TILE) * NCH, LANES), lambda b, t, *_: (b, 0, 0),
                               pipeline_mode=pl.Buffered(1)),
                  pl.BlockSpec((None, D, 2 * EXP_FF), lambda *a: (expert(*a), 0, 0)),
                  pl.BlockSpec((None, EXP_FF, D), lambda *a: (expert(*a), 0, 0))],
        out_specs=pl.BlockSpec((None, ROW_TILE, D), lambda b, t, *_: (b, jnp.maximum(t - (ntp + 2), 0), 0)),
        scratch_shapes=[pltpu.VMEM(((n_tok + SUBLANES) * NCH, LANES), f32),
                        pltpu.VMEM((2, TM * NCH, LANES), f32),
                        pltpu.VMEM((2, TM * NCH, LANES), f32)],
    )
    return pl.pallas_call(
        functools.partial(_moe_kernel, ntp=ntp, n_tok=n_tok),
        grid_spec=grid_spec,
        out_shape=jax.ShapeDtypeStruct((B, n_tok, D), f32),
        compiler_params=_cparams(("arbitrary", "arbitrary")),
        name="moe",
    )(te, nt, sk, sk, w_flat, hlin, w_gu, w_d)


def _routed_experts(e_t, w_t, r_t, cnt, hlin, w_gu, w_d):
    B, K, N = e_t.shape
    _, _, wlen = _moe_sizes(N)
    pos, meta = _plan(e_t, r_t, cnt)
    te = meta[:, 0, :].reshape(-1)
    nt = meta[:, 1, 0]
    tend = meta[:, 2, :N_EXP].reshape(-1)
    sk = _plan_invert(tend, nt, pos, N)
    w_flat = jnp.zeros((B, wlen), f32).at[:, :N * K].set(jnp.swapaxes(w_t, 1, 2).reshape(B, N * K)).reshape(-1)
    return _moe(te, nt, sk, w_flat, hlin, w_gu, w_d)


def _shared_ffn(fin, shgu_ref, shd_ref):
    gu = _dot(fin, shgu_ref[...])
    return _dot((_silu(gu[:, :SH_FF]) * gu[:, SH_FF:]).astype(bf16), shd_ref[...])


def _post_ffn_mla_kernel(x1_ref, routed_ref, fin_ref, shgu_ref, shd_ref, mod0_ref, mod1_ref, g_ref, win_ref,
                         x2_ref, a_ref):
    x2 = x1_ref[...] + mod0_ref[...][:, 5 * D:6 * D] * (routed_ref[...] + _shared_ffn(fin_ref[...], shgu_ref, shd_ref))
    x2_ref[...] = x2
    mod1 = mod1_ref[...]
    h = (_rms(x2) * g_ref[...]) * (1.0 + mod1[:, D:2 * D]) + mod1[:, 0:D]
    a_ref[...] = _dot(h.astype(bf16), win_ref[...])


def _post_ffn_mla(x1, routed, fin, sh_gu, sh_d, modtab0, modtab1, gain, w_in, n_ctx_tiles):
    B, T, _ = x1.shape
    tm = ROW_TILE
    tok = lambda w: pl.BlockSpec((None, tm, w), lambda b, i: (b, i, 0))
    modspec = lambda: pl.BlockSpec((None, None, 1, 6 * D), lambda b, i: (b, jnp.where(i < n_ctx_tiles, 0, 1), 0, 0))
    full = lambda r, c: pl.BlockSpec((r, c), lambda b, i: (0, 0))
    return pl.pallas_call(
        _post_ffn_mla_kernel,
        grid=(B, T // tm),
        in_specs=[tok(D), tok(D), tok(D), full(D, 2 * SH_FF), full(SH_FF, D), modspec(), modspec(),
                  full(1, D), full(D, MLA_IN_PAD)],
        out_specs=[tok(D), tok(MLA_IN_PAD)],
        out_shape=[jax.ShapeDtypeStruct((B, T, D), f32), jax.ShapeDtypeStruct((B, T, MLA_IN_PAD), f32)],
        compiler_params=_cparams(("arbitrary", "arbitrary")),
        name="post_ffn_mla",
    )(x1, routed, fin, sh_gu, sh_d, modtab0, modtab1, gain, w_in)


def _post_ffn_final_kernel(x1_ref, routed_ref, fin_ref, shgu_ref, shd_ref, mod_ref, out_ref):
    out_ref[...] = x1_ref[...] + mod_ref[...][:, 5 * D:6 * D] * (
        routed_ref[...] + _shared_ffn(fin_ref[...], shgu_ref, shd_ref))


def _post_ffn_final(x1, routed, fin, sh_gu, sh_d, modtab):
    B, N, _ = x1.shape
    tm = ROW_TILE
    tok = lambda w: pl.BlockSpec((None, tm, w), lambda b, i: (b, i, 0))
    full = lambda r, c: pl.BlockSpec((r, c), lambda b, i: (0, 0))
    return pl.pallas_call(
        _post_ffn_final_kernel,
        grid=(B, N // tm),
        in_specs=[tok(D), tok(D), tok(D), full(D, 2 * SH_FF), full(SH_FF, D),
                  pl.BlockSpec((None, None, 1, 6 * D), lambda b, i: (b, 1, 0, 0))],
        out_specs=tok(D),
        out_shape=jax.ShapeDtypeStruct((B, N, D), f32),
        compiler_params=_cparams(("arbitrary", "arbitrary")),
        name="post_ffn_final",
    )(x1, routed, fin, sh_gu, sh_d, modtab)


def _mla_prep_kernel(a_ref, qan_ref, wqn_ref, wqr_ref, kvan_ref, wk_ref, wv_ref, qnn_ref, qnr_ref, knn_ref, knr_ref,
                     cos_ref, sin_ref, q_ref, k_ref, v_ref):
    a = a_ref[...]
    tm = a.shape[0]
    scale = MLA_QK ** -0.5 * float(np.log2(np.e))
    cos = cos_ref[...]
    sin = sin_ref[...]
    lane = lax.broadcasted_iota(i32, (tm, LANES), 1)
    first = (lane // (MLA_ROPE // 4)) % 2 == 0

    def rope(xb):
        sw = jnp.where(first, pltpu.roll(xb, LANES - MLA_ROPE // 4, axis=1), pltpu.roll(xb, MLA_ROPE // 4, axis=1))
        return xb * cos + sw * sin

    qa = (_rms(a[:, :MLA_QR]) * qan_ref[...]).astype(bf16)
    qn = _dot(qa, wqn_ref[...])
    qr = _dot(qa, wqr_ref[...])
    ri = lax.broadcasted_iota(i32, (MLA_H * MLA_ROPE, MLA_H * MLA_ROPE), 0) // MLA_ROPE
    ci = lax.broadcasted_iota(i32, (MLA_H * MLA_ROPE, MLA_H * MLA_ROPE), 1) // MLA_ROPE
    seg = jnp.where(ri == ci, 1.0, 0.0)
    ssq = jnp.dot(qr * qr, seg, preferred_element_type=f32, precision=lax.Precision.HIGHEST)
    qr = qr * lax.rsqrt(ssq * (1.0 / MLA_ROPE) + EPS) * qnr_ref[...]
    qr_blocks = [rope(qr[:, p * LANES:(p + 1) * LANES]) * scale for p in range(MLA_H // 2)]

    kv = (_rms(a[:, MLA_QR:MLA_QR + MLA_KVR]) * kvan_ref[...]).astype(bf16)
    kn = _dot(kv, wk_ref[...])
    v_ref[...] = _dot_nt(wv_ref[...], kv).astype(bf16)
    kr = a[:, MLA_QR + MLA_KVR:MLA_IN_PAD]
    kr = rope(_rms(kr, MLA_ROPE) * knr_ref[...])
    kr_odd = pltpu.roll(kr, MLA_ROPE, axis=1)
    for hd in range(MLA_H):
        sl = slice(hd * MLA_NOPE, (hd + 1) * MLA_NOPE)
        q_ref[:, 2 * hd * LANES:(2 * hd + 1) * LANES] = (_rms(qn[:, sl]) * qnn_ref[...] * scale).astype(bf16)
        q_ref[:, (2 * hd + 1) * LANES:(2 * hd + 2) * LANES] = qr_blocks[hd // 2].astype(bf16)
        k_ref[:, 2 * hd * LANES:(2 * hd + 1) * LANES] = (_rms(kn[:, sl]) * knn_ref[...]).astype(bf16)
        k_ref[:, (2 * hd + 1) * LANES:(2 * hd + 2) * LANES] = (kr if hd % 2 == 0 else kr_odd).astype(bf16)


def _mla_prep(a, qan, wqn, wqr, kvan, wk, wv, qnn, qnr, knn, knr, cos_t, sin_t):
    B, T, _ = a.shape
    tm = ROW_TILE
    tok = lambda w: pl.BlockSpec((None, tm, w), lambda b, i: (b, i, 0))
    full = lambda r, c: pl.BlockSpec((r, c), lambda b, i: (0, 0))
    hw = 2 * LANES * MLA_H
    return pl.pallas_call(
        _mla_prep_kernel,
        grid=(B, T // tm),
        in_specs=[tok(MLA_IN_PAD), full(1, MLA_QR), full(MLA_QR, MLA_H * MLA_NOPE), full(MLA_QR, MLA_H * MLA_ROPE),
                  full(1, MLA_KVR), full(MLA_KVR, MLA_H * MLA_NOPE), full(MLA_H * MLA_V, MLA_KVR),
                  full(1, MLA_NOPE), full(1, MLA_H * MLA_ROPE), full(1, MLA_NOPE), full(1, LANES),
                  pl.BlockSpec((tm, LANES), lambda b, i: (i, 0)), pl.BlockSpec((tm, LANES), lambda b, i: (i, 0))],
        out_specs=[tok(hw), tok(hw), pl.BlockSpec((None, MLA_H * MLA_V, tm), lambda b, i: (b, 0, i))],
        out_shape=[jax.ShapeDtypeStruct((B, T, hw), bf16), jax.ShapeDtypeStruct((B, T, hw), bf16),
                   jax.ShapeDtypeStruct((B, MLA_H * MLA_V, T), bf16)],
        compiler_params=_cparams(("arbitrary", "arbitrary")),
        name="mla_prep",
    )(a, qan, wqn, wqr, kvan, wk, wv, qnn, qnr, knn, knr, cos_t, sin_t)


ATT_KV_CHUNK = 256


def _mla_attn_kernel(q_ref, k_ref, vt_ref, o_ref, s_ref, m_ref):
    tq = q_ref.shape[0]
    ck = ATT_KV_CHUNK
    nck = k_ref.shape[0] // ck
    i = pl.program_id(2)

    @pl.when(jnp.logical_and(jnp.logical_and(pl.program_id(0) == 0, pl.program_id(1) == 0), i == 0))
    def _():
        s_ref[...] = jnp.zeros_like(s_ref)
        m_ref[...] = jnp.zeros_like(m_ref)

    def fold(a, op):
        return op(a.reshape(ck // SUBLANES, SUBLANES, tq), axis=0)

    def stage(par):
        q = q_ref[...]
        mpart = jnp.full((SUBLANES, tq), -jnp.inf, f32)
        for j in range(nck):
            s = _dot_nt(k_ref[j * ck:(j + 1) * ck, :], q)
            s_ref[par, j * ck:(j + 1) * ck, :] = s
            mpart = jnp.maximum(mpart, fold(s, jnp.max))
        m_ref[par] = mpart
        m = jnp.max(m_ref[1 - par], axis=0, keepdims=True)
        lpart = jnp.zeros((SUBLANES, tq), f32)
        acc = jnp.zeros((MLA_V, tq), f32)
        for j in range(nck):
            p = jnp.exp2(s_ref[1 - par, j * ck:(j + 1) * ck, :] - m)
            lpart = lpart + fold(p, jnp.sum)
            acc = acc + _dot(vt_ref[:, j * ck:(j + 1) * ck], p.astype(bf16))
        o_ref[...] = (acc / jnp.sum(lpart, axis=0, keepdims=True)).astype(bf16)

    pl.when(i % 2 == 0)(functools.partial(stage, 0))
    pl.when(i % 2 == 1)(functools.partial(stage, 1))


def _mla_attn(q, k, vt, n_ctx_tiles):
    B, T, _ = q.shape
    tq = ROW_TILE
    S = T - n_ctx_tiles * tq
    nq = S // tq
    assert T % ATT_KV_CHUNK == 0
    return pl.pallas_call(
        _mla_attn_kernel,
        grid=(B, MLA_H, nq + 1),
        in_specs=[pl.BlockSpec((None, tq, 2 * LANES), lambda b, h, i: (b, jnp.minimum(i, nq - 1) + n_ctx_tiles, h)),
                  pl.BlockSpec((None, T, 2 * LANES), lambda b, h, i: (b, 0, h)),
                  pl.BlockSpec((None, MLA_V, T), lambda b, h, i: (b, h, 0))],
        out_specs=pl.BlockSpec((None, MLA_V, tq), lambda b, h, i: (b, h, jnp.maximum(i - 1, 0))),
        out_shape=jax.ShapeDtypeStruct((B, MLA_H * MLA_V, S), bf16),
        scratch_shapes=[pltpu.VMEM((2, T, tq), f32), pltpu.VMEM((2, SUBLANES, tq), f32)],
        compiler_params=_cparams(("arbitrary", "arbitrary", "arbitrary")),
        name="mla_attn",
    )(q, k, vt)


def _axial_angles(rows_n, rot_dim):
    axis_dim = rot_dim // 2
    inv = ROPE_BASE ** (-jnp.arange(0, axis_dim, 2, dtype=f32) / axis_dim)
    row = jnp.repeat(jnp.arange(rows_n, dtype=f32), GRID_W)
    col = jnp.tile(jnp.arange(GRID_W, dtype=f32), rows_n)
    return row[:, None] * inv, col[:, None] * inv


def _rope_tables(seq, n_ctx, rot_dim, reps):
    ang_r, ang_c = _axial_angles(seq // GRID_W, rot_dim)
    cos = jnp.concatenate([jnp.cos(ang_r)] * 2 + [jnp.cos(ang_c)] * 2, axis=1)
    sin = jnp.concatenate([-jnp.sin(ang_r), jnp.sin(ang_r), -jnp.sin(ang_c), jnp.sin(ang_c)], axis=1)
    cos = jnp.concatenate([jnp.ones((n_ctx, rot_dim), f32), cos], axis=0)
    sin = jnp.concatenate([jnp.zeros((n_ctx, rot_dim), f32), sin], axis=0)
    return jnp.tile(cos, (1, reps)), jnp.tile(sin, (1, reps))


def kernel(x, c, ctx, c_ctx, ada_w, ada_b, norm_mix, norm_ffn, ret_w_in, ret_decay_f, ret_decay_b, ret_w_o,
           mla_w_in, mla_q_a_norm, mla_w_q_b, mla_kv_a_norm, mla_w_kv_b, mla_q_norm, mla_k_norm, mla_w_o,
           router_w, router_bias, exp_w_gu, exp_w_down, sh_w_gu, sh_w_down):
    B, S, _ = x.shape
    n_ctx = ctx.shape[1]
    assert n_ctx % ROW_TILE == 0 and S % ROW_TILE == 0 and S % GRID_W == 0
    n_ctx_tiles = n_ctx // ROW_TILE

    rows = -(-(B + 1) // SUBLANES) * SUBLANES
    cc = jnp.zeros((rows, D), f32).at[:B].set(c).at[B].set(c_ctx)
    mod = _ada(cc, ada_w, ada_b)

    def modtab(i):
        ctx_row = jnp.broadcast_to(mod[i, B][None, :], (B, 6 * D))
        return jnp.stack([ctx_row, mod[i, :B]], axis=1)[:, :, None, :]

    mod0, mod1 = modtab(0), modtab(1)
    xc = jnp.concatenate([ctx, x], axis=1)

    cos_r, sin_r = _rope_tables(S, n_ctx, RET_DK, 1)
    q, k, v, gf, gb = _ret_inproj(xc, mod0, norm_mix[0][None, :], ret_w_in[0].astype(bf16), cos_r, sin_r, n_ctx_tiles)
    dtab = jnp.broadcast_to(jnp.concatenate([ret_decay_f[0], ret_decay_b[0]])[:, None], (2 * RET_H, LANES))
    o = _ret_scan(dtab, q, k, v, gf, gb, n_ctx)
    x1, fin, hlin, e_t, w_t, r_t, cnt = _post_mix(
        o, ret_w_o[0].astype(bf16), xc, 0, mod0, n_ctx_tiles, norm_ffn[0][None, :],
        router_w[0].T, router_bias[0][:, None])
    routed = _routed_experts(e_t, w_t, r_t, cnt, hlin, exp_w_gu[0].astype(bf16), exp_w_down[0].astype(bf16))
    w_in1 = jnp.zeros((D, MLA_IN_PAD), f32).at[:, :mla_w_in.shape[2]].set(mla_w_in[0]).astype(bf16)
    x2, a = _post_ffn_mla(x1, routed, fin, sh_w_gu[0].astype(bf16), sh_w_down[0].astype(bf16), mod0, mod1,
                          norm_mix[1][None, :], w_in1, n_ctx_tiles)

    wq = mla_w_q_b[0].reshape(MLA_QR, MLA_H, MLA_QK)
    wqn = wq[:, :, :MLA_NOPE].reshape(MLA_QR, MLA_H * MLA_NOPE).astype(bf16)
    wqr = wq[:, :, MLA_NOPE:].reshape(MLA_QR, MLA_H * MLA_ROPE).astype(bf16)
    wkv = mla_w_kv_b[0].reshape(MLA_KVR, MLA_H, MLA_NOPE + MLA_V)
    wk = wkv[:, :, :MLA_NOPE].reshape(MLA_KVR, MLA_H * MLA_NOPE).astype(bf16)
    wv = wkv[:, :, MLA_NOPE:].reshape(MLA_KVR, MLA_H * MLA_V).T.astype(bf16)
    qnn = mla_q_norm[0][None, :MLA_NOPE]
    qnr = jnp.tile(mla_q_norm[0][None, MLA_NOPE:], (1, MLA_H))
    knn = mla_k_norm[0][None, :MLA_NOPE]
    knr = jnp.concatenate([mla_k_norm[0][MLA_NOPE:], jnp.zeros((LANES - MLA_ROPE,), f32)])[None, :]
    cos_m, sin_m = _rope_tables(S, n_ctx, MLA_ROPE, LANES // MLA_ROPE)
    qf, kf, vf = _mla_prep(a, mla_q_a_norm[0][None, :], wqn, wqr, mla_kv_a_norm[0][None, :], wk, wv,
                           qnn, qnr, knn, knr, cos_m, sin_m)
    o1 = _mla_attn(qf, kf, vf, n_ctx_tiles)
    x3, fin1, hlin1, e1, w1, r1, cnt1 = _post_mix(
        o1, mla_w_o[0].astype(bf16), x2, n_ctx_tiles, mod1, 0, norm_ffn[1][None, :],
        router_w[1].T, router_bias[1][:, None], o_transposed=True)
    routed1 = _routed_experts(e1, w1, r1, cnt1, hlin1, exp_w_gu[1].astype(bf16), exp_w_down[1].astype(bf16))
    return _post_ffn_final(x3, routed1, fin1, sh_w_gu[1].astype(bf16), sh_w_down[1].astype(bf16), mod1)
```

```python
import functools

import jax
import jax.numpy as jnp
import numpy as np
from jax import lax
from jax.experimental import pallas as pl
from jax.experimental.pallas import tpu as pltpu

f32 = jnp.float32
bf16 = jnp.bfloat16
i32 = jnp.int32

D = 1024
GRID_W = 64
EPS = 1e-6
ROPE_BASE = 10000.0
RET_H = 4
RET_DK = 256
RET_DV = 512
RET_VW = RET_H * RET_DV
RET_CHUNK = 256
MLA_H = 8
MLA_NOPE = 128
MLA_ROPE = 64
MLA_QK = MLA_NOPE + MLA_ROPE
MLA_V = 128
MLA_QR = 384
MLA_KVR = 256
MLA_IN_PAD = 768
N_EXP = 64
TOP_K = 8
N_GRP = 8
TOPK_GRP = 4
EXP_FF = 256
SH_FF = 256
ROUTED_SCALE = 2.5

LANES = 128
SUBLANES = 8
ROW_TILE = 256
MOE_TILE = 256
MOE_GROUP = 2
VMEM_LIMIT = 56 * 1024 * 1024
MOE_VMEM_LIMIT = 62 * 1024 * 1024


def _cparams(sem, vmem=VMEM_LIMIT):
    return pltpu.CompilerParams(dimension_semantics=sem, vmem_limit_bytes=vmem)


def _sigmoid(x):
    return 1.0 / (1.0 + jnp.exp(-x))


def _silu(x):
    return x * _sigmoid(x)


def _rms(x, n=None):
    n = x.shape[-1] if n is None else n
    return x * lax.rsqrt(jnp.sum(x * x, axis=-1, keepdims=True) * (1.0 / n) + EPS)


def _dot(a, b):
    return jnp.dot(a, b, preferred_element_type=f32)


def _dot_nt(a, b, precision=None):
    return lax.dot_general(a, b, (((1,), (1,)), ((), ())), preferred_element_type=f32, precision=precision)


def _ada_kernel(c_ref, w_ref, b_ref, o_ref):
    s = _silu(c_ref[...]).astype(bf16)
    o_ref[...] = _dot(s, w_ref[...].astype(bf16)) + b_ref[...]


def _ada(cc, ada_w, ada_b):
    depth = ada_w.shape[0]
    rows = cc.shape[0]
    tn = 1536
    return pl.pallas_call(
        _ada_kernel,
        grid=(depth, 6 * D // tn),
        in_specs=[pl.BlockSpec((rows, D), lambda i, j: (0, 0)),
                  pl.BlockSpec((None, D, tn), lambda i, j: (i, 0, j)),
                  pl.BlockSpec((None, 1, tn), lambda i, j: (i, 0, j))],
        out_specs=pl.BlockSpec((None, rows, tn), lambda i, j: (i, 0, j)),
        out_shape=jax.ShapeDtypeStruct((depth, rows, 6 * D), f32),
        compiler_params=_cparams(("arbitrary", "arbitrary")),
        name="ada",
    )(cc, ada_w, ada_b.reshape(depth, 1, 6 * D))


def _ret_inproj_kernel(x_ref, mod_ref, g_ref, w_ref, cos_ref, sin_ref, q_ref, k_ref, v_ref, gf_ref, gb_ref):
    x = x_ref[...]
    mod = mod_ref[...]
    h = (_rms(x) * g_ref[...]) * (1.0 + mod[:, D:2 * D]) + mod[:, 0:D]
    hb = h.astype(bf16)
    cos = cos_ref[...]
    sin = sin_ref[...]

    def rope(a):
        outs = []
        for half in range(2):
            sl = slice(half * LANES, (half + 1) * LANES)
            ah = a[:, sl]
            outs.append(ah * cos[:, sl] + pltpu.roll(ah, LANES // 2, axis=1) * sin[:, sl])
        return jnp.concatenate(outs, axis=1)

    for hd in range(RET_H):
        sl = slice(hd * RET_DK, (hd + 1) * RET_DK)
        q_ref[:, sl] = rope(_dot(hb, w_ref[:, sl])).astype(bf16)
    for hd in range(RET_H):
        sl = slice(hd * RET_DK, (hd + 1) * RET_DK)
        wsl = slice(D + hd * RET_DK, D + (hd + 1) * RET_DK)
        k_ref[:, sl] = (rope(_dot(hb, w_ref[:, wsl])) * (RET_DK ** -0.5)).astype(bf16)
    cw = 512
    for c in range(RET_VW // cw):
        sl = slice(c * cw, (c + 1) * cw)
        v_ref[:, sl] = _dot(hb, w_ref[:, 2 * D + c * cw:2 * D + (c + 1) * cw]).astype(bf16)
        gf_ref[:, sl] = _silu(_dot(hb, w_ref[:, 2 * D + RET_VW + c * cw:2 * D + RET_VW + (c + 1) * cw])).astype(bf16)
        gb_ref[:, sl] = _silu(_dot(hb, w_ref[:, 2 * D + 2 * RET_VW + c * cw:2 * D + 2 * RET_VW + (c + 1) * cw])).astype(bf16)


def _ret_inproj(xc, modtab, gain, w_in, cos_t, sin_t, n_ctx_tiles):
    B, T, _ = xc.shape
    tm = ROW_TILE
    n_in = w_in.shape[1]
    tok = lambda w: pl.BlockSpec((None, tm, w), lambda b, i: (b, i, 0))
    return pl.pallas_call(
        _ret_inproj_kernel,
        grid=(B, T // tm),
        in_specs=[tok(D),
                  pl.BlockSpec((None, None, 1, 6 * D), lambda b, i: (b, jnp.where(i < n_ctx_tiles, 0, 1), 0, 0)),
                  pl.BlockSpec((1, D), lambda b, i: (0, 0)),
                  pl.BlockSpec((D, n_in), lambda b, i: (0, 0), pipeline_mode=pl.Buffered(1)),
                  pl.BlockSpec((tm, RET_DK), lambda b, i: (i, 0)),
                  pl.BlockSpec((tm, RET_DK), lambda b, i: (i, 0))],
        out_specs=[tok(D), tok(D), tok(RET_VW), tok(RET_VW), tok(RET_VW)],
        out_shape=[jax.ShapeDtypeStruct((B, T, D), bf16), jax.ShapeDtypeStruct((B, T, D), bf16),
                   jax.ShapeDtypeStruct((B, T, RET_VW), bf16), jax.ShapeDtypeStruct((B, T, RET_VW), bf16),
                   jax.ShapeDtypeStruct((B, T, RET_VW), bf16)],
        compiler_params=_cparams(("arbitrary", "arbitrary")),
        name="ret_inproj",
    )(xc, modtab, gain, w_in, cos_t, sin_t)


def _ret_chunk_index(t, nc, ncc):
    u = t - nc
    back = jnp.where(u < ncc, ncc - 1 - u, nc - 1 - u + ncc)
    return jnp.where(t < nc, t, back)


def _ret_scan_kernel(dt_ref, q_ref, k_ref, v_ref, gf_ref, gb_ref, o_ref,
                     s_ref, of_ref, mask_ref, dq_ref, dk_ref, dc_ref, *, nc, ncc):
    t = pl.program_id(1)
    C = RET_CHUNK

    def init(direction):
        s_ref[...] = jnp.zeros_like(s_ref)
        ii = lax.broadcasted_iota(i32, (C, C), 0)
        jj = lax.broadcasted_iota(i32, (C, C), 1)
        rel = (ii - jj if direction == 0 else jj - ii).astype(f32)
        pos = lax.broadcasted_iota(i32, (C, 1), 0).astype(f32)
        for hd in range(RET_H):
            r = direction * RET_H + hd
            lg = -jnp.exp(dt_ref[r:r + 1, :])
            lg1 = lg[:, 0:1]
            mask_ref[hd] = jnp.where(rel >= 0, jnp.exp(lg1 * jnp.maximum(rel, 0.0)), 0.0)
            if direction == 0:
                dq_ref[hd] = jnp.exp(lg1 * (pos + 1.0))
                dk_ref[hd] = jnp.exp(lg1 * (C - 1.0 - pos))
            else:
                dq_ref[hd] = jnp.exp(lg1 * (C - pos))
                dk_ref[hd] = jnp.exp(lg1 * pos)
            dc_ref[hd] = jnp.exp(lg * float(C))

    pl.when(t == 0)(functools.partial(init, 0))
    pl.when(t == nc)(functools.partial(init, 1))

    fwd = t < nc
    row0 = pl.multiple_of(_ret_chunk_index(t, nc, ncc) * C, C)

    for hd in range(RET_H):
        ks = slice(hd * RET_DK, (hd + 1) * RET_DK)
        vs = slice(hd * RET_DV, (hd + 1) * RET_DV)
        qh = q_ref[:, ks]
        kh = k_ref[:, ks]
        vh = v_ref[:, vs]
        p = (_dot_nt(qh, kh) * mask_ref[hd]).astype(bf16)
        y = _dot(p, vh) + _dot(qh, s_ref[hd].astype(bf16)) * dq_ref[hd]
        kd = (kh.astype(f32) * dk_ref[hd]).astype(bf16)
        upd = lax.dot_general(kd, vh, (((0,), (0,)), ((), ())), preferred_element_type=f32)
        s_ref[hd] = s_ref[hd] * dc_ref[hd][0:1, 0:1] + upd
        yn = _rms(y)

        @pl.when(fwd)
        def _():
            of_ref[pl.ds(row0, C), vs] = (gf_ref[:, vs].astype(f32) * yn).astype(bf16)

        @pl.when(jnp.logical_not(fwd))
        def _():
            o_ref[:, vs] = (of_ref[pl.ds(row0, C), vs].astype(f32) + gb_ref[:, vs].astype(f32) * yn).astype(bf16)


def _ret_scan(dtab, q, k, v, gf, gb, n_ctx):
    B, T, _ = q.shape
    C = RET_CHUNK
    nc = T // C
    ncc = n_ctx // C
    cidx = functools.partial(_ret_chunk_index, nc=nc, ncc=ncc)
    first_back = ncc - 1
    return pl.pallas_call(
        functools.partial(_ret_scan_kernel, nc=nc, ncc=ncc),
        grid=(B, 2 * nc),
        in_specs=[pl.BlockSpec((2 * RET_H, LANES), lambda b, t: (0, 0)),
                  pl.BlockSpec((None, C, D), lambda b, t: (b, cidx(t), 0)),
                  pl.BlockSpec((None, C, D), lambda b, t: (b, cidx(t), 0)),
                  pl.BlockSpec((None, C, RET_VW), lambda b, t: (b, cidx(t), 0)),
                  pl.BlockSpec((None, C, RET_VW), lambda b, t: (b, jnp.where(t < nc, t, nc - 1), 0)),
                  pl.BlockSpec((None, C, RET_VW), lambda b, t: (b, jnp.where(t < nc, first_back, cidx(t)), 0))],
        out_specs=pl.BlockSpec((None, C, RET_VW), lambda b, t: (b, jnp.where(t < nc, first_back, cidx(t)), 0)),
        out_shape=jax.ShapeDtypeStruct((B, T, RET_VW), bf16),
        scratch_shapes=[pltpu.VMEM((RET_H, RET_DK, RET_DV), f32),
                        pltpu.VMEM((T, RET_VW), bf16),
                        pltpu.VMEM((RET_H, C, C), f32),
                        pltpu.VMEM((RET_H, C, 1), f32),
                        pltpu.VMEM((RET_H, C, 1), f32),
                        pltpu.VMEM((RET_H, 1, LANES), f32)],
        compiler_params=_cparams(("arbitrary", "arbitrary")),
        name="ret_scan",
    )(dtab, q, k, v, gf, gb)


def _route(f, rwt_ref, rb_ref, cnt_ref, e_ref, w_ref, r_ref):
    tm = f.shape[0]
    G = N_EXP // N_GRP
    logits = _dot_nt(rwt_ref[...], f, precision=lax.Precision.HIGHEST)
    s = _sigmoid(logits)
    sel = s + rb_ref[...]
    mi = lax.broadcasted_iota(i32, (G, tm), 0)
    neg = -jnp.inf
    s_g = [s[g * G:(g + 1) * G, :] for g in range(N_GRP)]
    sel_g = [sel[g * G:(g + 1) * G, :] for g in range(N_GRP)]

    def first_max(a, ids, big):
        mx = jnp.max(a, axis=0, keepdims=True)
        ix = jnp.min(jnp.where(a == mx, ids, big), axis=0, keepdims=True)
        return mx, ix

    gscore = jnp.zeros((N_GRP, tm), f32)
    gi = lax.broadcasted_iota(i32, (N_GRP, tm), 0)
    for g in range(N_GRP):
        t1, i1 = first_max(sel_g[g], mi, G)
        t2 = jnp.max(jnp.where(mi == i1, neg, sel_g[g]), axis=0, keepdims=True)
        gscore = jnp.where(gi == g, t1 + t2, gscore)
    gmask = jnp.zeros((N_GRP, tm), i32)
    cur = gscore
    for _ in range(TOPK_GRP):
        _, ix = first_max(cur, gi, N_GRP)
        hit = gi == ix
        gmask = jnp.where(hit, 1, gmask)
        cur = jnp.where(hit, neg, cur)
    cand = [jnp.where(gmask[g:g + 1, :] > 0, sel_g[g], neg) for g in range(N_GRP)]
    ids = [mi + g * G for g in range(N_GRP)]

    e_rows, w_rows = [], []
    for _ in range(TOP_K):
        mx = cand[0].max(axis=0, keepdims=True)
        for g in range(1, N_GRP):
            mx = jnp.maximum(mx, cand[g].max(axis=0, keepdims=True))
        ix = jnp.min(jnp.where(cand[0] == mx, ids[0], N_EXP), axis=0, keepdims=True)
        for g in range(1, N_GRP):
            ix = jnp.minimum(ix, jnp.min(jnp.where(cand[g] == mx, ids[g], N_EXP), axis=0, keepdims=True))
        wv = jnp.zeros((1, tm), f32)
        for g in range(N_GRP):
            hit = ids[g] == ix
            cand[g] = jnp.where(hit, neg, cand[g])
            wv = wv + jnp.sum(jnp.where(hit, s_g[g], 0.0), axis=0, keepdims=True)
        e_rows.append(ix)
        w_rows.append(wv)
    wsum = w_rows[0]
    for r in range(1, TOP_K):
        wsum = wsum + w_rows[r]

    selm = [jnp.zeros((G, tm), f32) for _ in range(N_GRP)]
    for r in range(TOP_K):
        for g in range(N_GRP):
            selm[g] = jnp.where(ids[g] == e_rows[r], 1.0, selm[g])
    m_all = jnp.concatenate(selm, axis=0)
    ri = lax.broadcasted_iota(i32, (tm, tm), 0)
    ci = lax.broadcasted_iota(i32, (tm, tm), 1)
    upper = jnp.where(ri <= ci, 1.0, 0.0).astype(bf16)
    incl = _dot(m_all.astype(bf16), upper)
    carry = cnt_ref[:, 0:1]
    rank_all = carry + incl - m_all
    cnt_ref[...] = cnt_ref[...] + incl[:, tm - 1:tm]
    for r in range(TOP_K):
        rk = jnp.zeros((1, tm), f32)
        for g in range(N_GRP):
            rk = rk + jnp.sum(jnp.where(ids[g] == e_rows[r], rank_all[g * G:(g + 1) * G, :], 0.0), axis=0, keepdims=True)
        e_ref[r:r + 1, :] = e_rows[r]
        w_ref[r:r + 1, :] = w_rows[r] / wsum * ROUTED_SCALE
        r_ref[r:r + 1, :] = rk.astype(i32)


def _post_mix_kernel(o_ref, wo_ref, x_ref, mod_ref, g_ref, rwt_ref, rb_ref,
                     x1_ref, fin_ref, hlin_ref, e_ref, w_ref, r_ref, cnt_ref, *, nt, o_transposed):
    i = pl.program_id(1)

    @pl.when(i == 0)
    def _():
        cnt_ref[...] = jnp.zeros_like(cnt_ref)

    @pl.when(i < nt)
    def _():
        mod = mod_ref[...]
        if o_transposed:
            mixed = lax.dot_general(o_ref[...], wo_ref[...], (((0,), (0,)), ((), ())), preferred_element_type=f32)
        else:
            mixed = _dot(o_ref[...], wo_ref[...])
        x1 = x_ref[...] + mod[:, 2 * D:3 * D] * mixed
        x1_ref[...] = x1
        f = (_rms(x1) * g_ref[...]) * (1.0 + mod[:, 4 * D:5 * D]) + mod[:, 3 * D:4 * D]
        fin_ref[...] = f.astype(bf16)
        tm = f.shape[0]
        for j in range(D // LANES):
            hlin_ref[pl.ds(j, tm, stride=D // LANES), :] = f[:, j * LANES:(j + 1) * LANES]
        _route(f, rwt_ref, rb_ref, cnt_ref, e_ref, w_ref, r_ref)

    @pl.when(i == nt)
    def _():
        hlin_ref[...] = jnp.zeros_like(hlin_ref)


def _post_mix(o, w_o, xs, x_tile_off, modtab, n_ctx_tiles, gain, rwt, rb, o_transposed=False):
    if o_transposed:
        B, KO, N = o.shape
    else:
        B, N, KO = o.shape
    tm = ROW_TILE
    nt = N // tm
    last = lambda i: jnp.minimum(i, nt - 1)
    tok = lambda w: pl.BlockSpec((None, tm, w), lambda b, i: (b, last(i), 0))
    sel = lambda: pl.BlockSpec((None, TOP_K, tm), lambda b, i: (b, 0, last(i)))
    o_spec = (pl.BlockSpec((None, KO, tm), lambda b, i: (b, 0, last(i))) if o_transposed else tok(KO))
    return pl.pallas_call(
        functools.partial(_post_mix_kernel, nt=nt, o_transposed=o_transposed),
        grid=(B, nt + 1),
        in_specs=[o_spec,
                  pl.BlockSpec((KO, D), lambda b, i: (0, 0)),
                  pl.BlockSpec((None, tm, D), lambda b, i: (b, last(i) + x_tile_off, 0)),
                  pl.BlockSpec((None, None, 1, 6 * D), lambda b, i: (b, jnp.where(last(i) < n_ctx_tiles, 0, 1), 0, 0)),
                  pl.BlockSpec((1, D), lambda b, i: (0, 0)),
                  pl.BlockSpec((N_EXP, D), lambda b, i: (0, 0)),
                  pl.BlockSpec((N_EXP, 1), lambda b, i: (0, 0))],
        out_specs=[tok(D), tok(D),
                   pl.BlockSpec((None, tm * (D // LANES), LANES), lambda b, i: (b, i, 0)),
                   sel(), sel(), sel(),
                   pl.BlockSpec((None, N_EXP, LANES), lambda b, i: (b, 0, 0))],
        out_shape=[jax.ShapeDtypeStruct((B, N, D), f32), jax.ShapeDtypeStruct((B, N, D), bf16),
                   jax.ShapeDtypeStruct((B, (N + tm) * (D // LANES), LANES), f32),
                   jax.ShapeDtypeStruct((B, TOP_K, N), i32), jax.ShapeDtypeStruct((B, TOP_K, N), f32),
                   jax.ShapeDtypeStruct((B, TOP_K, N), i32),
                   jax.ShapeDtypeStruct((B, N_EXP, LANES), f32)],
        compiler_params=_cparams(("arbitrary", "arbitrary")),
        name="post_mix",
    )(o, w_o, xs, modtab, gain, rwt, rb)


META_W = 256
PLAN_ALIGN = 1024


def _round_up(n, m):
    return -(-n // m) * m


def _moe_sizes(n_tok):
    tm = MOE_TILE
    nt_max = (n_tok * TOP_K + N_EXP * (tm - 1)) // tm + 1
    ntp = _round_up(nt_max + 1, PLAN_ALIGN // tm)
    assert ntp <= META_W
    return ntp, _round_up(n_tok, PLAN_ALIGN), _round_up(n_tok * TOP_K + SUBLANES, PLAN_ALIGN)


def _plan_kernel(e_ref, r_ref, cnt_ref, pos_ref, meta_ref, *, n_tok):
    tm = MOE_TILE
    ntile = jnp.floor((cnt_ref[...] + (tm - 1.0)) * (1.0 / tm))
    ntb = ntile.astype(bf16)
    ei = lax.broadcasted_iota(i32, (N_EXP, LANES), 0)
    ej = lax.broadcasted_iota(i32, (N_EXP, LANES), 1)
    lower = jnp.where(ej <= ei, 1.0, 0.0)[:, :N_EXP].astype(bf16)
    tend = _dot(lower, ntb)
    tstart = tend - ntile
    upper = jnp.where(ei <= ej, 1.0, 0.0).astype(bf16)
    tend_row = lax.dot_general(ntb, upper, (((0,), (0,)), ((), ())), preferred_element_type=f32)[0:1, :]
    tt = lax.broadcasted_iota(i32, (N_EXP, META_W), 1).astype(f32)
    te = jnp.sum(jnp.where(tt >= tend[:, 0:1], 1.0, 0.0), axis=0, keepdims=True)
    meta_ref[...] = jnp.zeros_like(meta_ref)
    meta_ref[0:1, :] = jnp.minimum(te, N_EXP - 1.0).astype(i32)
    meta_ref[1:2, :] = jnp.broadcast_to(tend[N_EXP - 1:N_EXP, 0:1], (1, META_W)).astype(i32)
    meta_ref[2:3, 0:LANES] = tend_row.astype(i32)
    e = e_ref[...]
    base = jnp.zeros(e.shape, f32)
    for ex in range(N_EXP):
        base = jnp.where(e == ex, tstart[ex:ex + 1, 0:1] * float(tm), base)
    pos_ref[...] = jnp.zeros_like(pos_ref)
    pos_ref[:, 0:n_tok] = base.astype(i32) + r_ref[...]


def _plan(e_t, r_t, cnt):
    B, K, N = e_t.shape
    _, npad, _ = _moe_sizes(N)
    return pl.pallas_call(
        functools.partial(_plan_kernel, n_tok=N),
        grid=(B,),
        in_specs=[pl.BlockSpec((None, K, N), lambda b: (b, 0, 0)),
                  pl.BlockSpec((None, K, N), lambda b: (b, 0, 0)),
                  pl.BlockSpec((None, N_EXP, LANES), lambda b: (b, 0, 0))],
        out_specs=[pl.BlockSpec((None, K, npad), lambda b: (b, 0, 0)),
                   pl.BlockSpec((None, SUBLANES, META_W), lambda b: (b, 0, 0))],
        out_shape=[jax.ShapeDtypeStruct((B, K, npad), i32), jax.ShapeDtypeStruct((B, SUBLANES, META_W), i32)],
        compiler_params=_cparams(("arbitrary",)),
        name="moe_plan",
    )(e_t, r_t, cnt)


def _plan_invert_kernel(tend_ref, nt_ref, pos_ref, sk_ref, *, n_tok, ntp):
    b = pl.program_id(0)
    k = pl.program_id(1)
    tm = MOE_TILE
    U = 8
    pad = n_tok * TOP_K

    def fill_tile(tile):
        def body(c, carry):
            for u in range(U):
                sk_ref[tile * tm + c * U + u] = pad
            return carry
        lax.fori_loop(0, tm // U, body, 0)

    @pl.when(k == 0)
    def _():
        def per_expert(ex, carry):
            fill_tile(jnp.maximum(tend_ref[b * N_EXP + ex] - 1, 0))
            return carry
        lax.fori_loop(0, N_EXP, per_expert, 0)

        def tail(tile, carry):
            fill_tile(tile)
            return carry
        lax.fori_loop(nt_ref[b], ntp, tail, 0)

    def body(c, carry):
        n0 = c * U
        v0 = c * (U * TOP_K) + k
        ps = [pos_ref[n0 + u] for u in range(U)]
        for u in range(U):
            sk_ref[ps[u]] = v0 + u * TOP_K
        return carry
    lax.fori_loop(0, n_tok // U, body, 0)


def _plan_invert(tend, nt, pos, n_tok):
    B, K, npad = pos.shape
    ntp, _, _ = _moe_sizes(n_tok)
    plen = ntp * MOE_TILE
    grid_spec = pltpu.PrefetchScalarGridSpec(
        num_scalar_prefetch=2,
        grid=(B, K),
        in_specs=[pl.BlockSpec((npad,), lambda b, k, *_: (b * K + k,), memory_space=pltpu.SMEM)],
        out_specs=pl.BlockSpec((plen,), lambda b, k, *_: (b,), memory_space=pltpu.SMEM),
    )
    return pl.pallas_call(
        functools.partial(_plan_invert_kernel, n_tok=n_tok, ntp=ntp),
        grid_spec=grid_spec,
        out_shape=jax.ShapeDtypeStruct((B * plen,), i32),
        compiler_params=_cparams(("arbitrary", "arbitrary")),
        name="moe_invert",
    )(tend, nt, pos.reshape(-1))


def _moe_kernel(te_ref, nt_ref, *refs, ntp, n_tok):
    R = MOE_GROUP
    skg_refs, sks_refs = refs[0:R], refs[R:2 * R]
    w_ref, hlin_ref = refs[2 * R], refs[2 * R + 1]
    wgu_refs, wd_refs = refs[2 * R + 2:3 * R + 2], refs[3 * R + 2:4 * R + 2]
    out_ref, acc_ref, xs_ref, ylin_ref = refs[4 * R + 2:]
    b = pl.program_id(0)
    t = pl.program_id(1)
    TM = MOE_TILE
    NCH = D // LANES
    U = 8
    ngrp = ntp // R

    @pl.when(t == 0)
    def _():
        acc_ref[...] = jnp.zeros_like(acc_ref)
        xs_ref[...] = jnp.zeros_like(xs_ref)
        ylin_ref[...] = jnp.zeros_like(ylin_ref)

    def stage(p):
        for r in range(R):
            for m in range(TM):
                sk = skg_refs[r][m]
                off = pl.multiple_of(sk & -NCH, NCH)
                xs_ref[p, r, pl.ds(m * NCH, NCH), :] = hlin_ref[pl.ds(off, NCH), :]
        for r in range(R):
            x = jnp.concatenate([xs_ref[1 - p, r, pl.ds(j, TM, stride=NCH), :] for j in range(NCH)],
                                axis=1).astype(bf16)
            gu = _dot(x, wgu_refs[r][...])
            a = (_silu(gu[:, :EXP_FF]) * gu[:, EXP_FF:]).astype(bf16)
            y = _dot(a, wd_refs[r][...])
            for g in range(TM // SUBLANES):
                for j in range(NCH):
                    ylin_ref[1 - p, r, pl.ds((g * NCH + j) * SUBLANES, SUBLANES), :] = (
                        y[g * SUBLANES:(g + 1) * SUBLANES, j * LANES:(j + 1) * LANES])
        for r in range(R):
            for c in range(TM // U):
                sks = [sks_refs[r][c * U + u] for u in range(U)]
                offs = [pl.multiple_of(s & -NCH, NCH) for s in sks]
                news = []
                for u in range(U):
                    m = c * U + u
                    yrow = ylin_ref[p, r, pl.ds((m // SUBLANES) * (NCH * SUBLANES) + m % SUBLANES, NCH,
                                               stride=SUBLANES), :]
                    news.append(acc_ref[pl.ds(offs[u], NCH), :] + w_ref[sks[u]] * yrow)
                for u in range(U):
                    acc_ref[pl.ds(offs[u], NCH), :] = news[u]

    live = (t - 2) * R < nt_ref[b]
    pl.when(jnp.logical_and(live, t % 2 == 0))(functools.partial(stage, 0))
    pl.when(jnp.logical_and(live, t % 2 == 1))(functools.partial(stage, 1))

    @pl.when(t >= ngrp + 2)
    def _():
        row0 = (t - (ngrp + 2)) * (ROW_TILE * NCH)
        for j in range(NCH):
            out_ref[:, j * LANES:(j + 1) * LANES] = acc_ref[pl.ds(row0 + j, ROW_TILE, stride=NCH), :]


def _moe(te, nt, sk, w_flat, hlin, w_gu, w_d):
    B = hlin.shape[0]
    NCH = D // LANES
    n_tok = hlin.shape[1] // NCH - ROW_TILE
    assert n_tok % ROW_TILE == 0
    nf = n_tok // ROW_TILE
    TM = MOE_TILE
    R = MOE_GROUP
    ntp, _, wlen = _moe_sizes(n_tok)
    assert ntp % R == 0
    ngrp = ntp // R
    pad_tile = ntp - 1

    def sk_spec(r, lag):
        def index(b, t, te_ref, nt_ref):
            tile = (t - lag) * R + r
            ok = jnp.logical_and(t >= lag, tile < nt_ref[b])
            return (b * ntp + jnp.where(ok, tile, pad_tile),)
        return pl.BlockSpec((TM,), index, memory_space=pltpu.SMEM)

    def w_spec(shape, r):
        def index(b, t, te_ref, nt_ref):
            tile = jnp.clip((t - 1) * R + r, 0, nt_ref[b] - 1)
            return (te_ref[b * META_W + tile], 0, 0)
        return pl.BlockSpec((None,) + shape, index)

    grid_spec = pltpu.PrefetchScalarGridSpec(
        num_scalar_prefetch=2,
        grid=(B, ngrp + 2 + nf),
        in_specs=([sk_spec(r, 0) for r in range(R)] + [sk_spec(r, 2) for r in range(R)]
                  + [pl.BlockSpec((wlen,), lambda b, t, *_: (b,), memory_space=pltpu.SMEM),
                     pl.BlockSpec((None, (n_tok + ROW_TILE) * NCH, LANES), lambda b, t, *_: (b, 0, 0),
                                  pipeline_mode=pl.Buffered(1))]
                  + [w_spec((D, 2 * EXP_FF), r) for r in range(R)]
                  + [w_spec((EXP_FF, D), r) for r in range(R)]),
        out_specs=pl.BlockSpec((None, ROW_TILE, D), lambda b, t, *_: (b, jnp.maximum(t - (ngrp + 2), 0), 0)),
        scratch_shapes=[pltpu.VMEM(((n_tok + SUBLANES) * NCH, LANES), f32),
                        pltpu.VMEM((2, R, TM * NCH, LANES), f32),
                        pltpu.VMEM((2, R, TM * NCH, LANES), f32)],
    )
    return pl.pallas_call(
        functools.partial(_moe_kernel, ntp=ntp, n_tok=n_tok),
        grid_spec=grid_spec,
        out_shape=jax.ShapeDtypeStruct((B, n_tok, D), f32),
        compiler_params=_cparams(("arbitrary", "arbitrary"), MOE_VMEM_LIMIT),
        name="moe",
    )(te, nt, *([sk] * (2 * R)), w_flat, hlin, *([w_gu] * R), *([w_d] * R))


def _routed_experts(e_t, w_t, r_t, cnt, hlin, w_gu, w_d):
    B, K, N = e_t.shape
    _, _, wlen = _moe_sizes(N)
    pos, meta = _plan(e_t, r_t, cnt)
    te = meta[:, 0, :].reshape(-1)
    nt = meta[:, 1, 0]
    tend = meta[:, 2, :N_EXP].reshape(-1)
    sk = _plan_invert(tend, nt, pos, N)
    w_flat = jnp.zeros((B, wlen), f32).at[:, :N * K].set(jnp.swapaxes(w_t, 1, 2).reshape(B, N * K)).reshape(-1)
    return _moe(te, nt, sk, w_flat, hlin, w_gu, w_d)


def _shared_ffn(fin, shgu_ref, shd_ref):
    gu = _dot(fin, shgu_ref[...])
    return _dot((_silu(gu[:, :SH_FF]) * gu[:, SH_FF:]).astype(bf16), shd_ref[...])


def _post_ffn_mla_kernel(x1_ref, routed_ref, fin_ref, shgu_ref, shd_ref, mod0_ref, mod1_ref, g_ref, win_ref,
                         x2_ref, a_ref):
    x2 = x1_ref[...] + mod0_ref[...][:, 5 * D:6 * D] * (routed_ref[...] + _shared_ffn(fin_ref[...], shgu_ref, shd_ref))
    x2_ref[...] = x2
    mod1 = mod1_ref[...]
    h = (_rms(x2) * g_ref[...]) * (1.0 + mod1[:, D:2 * D]) + mod1[:, 0:D]
    a_ref[...] = _dot(h.astype(bf16), win_ref[...])


def _post_ffn_mla(x1, routed, fin, sh_gu, sh_d, modtab0, modtab1, gain, w_in, n_ctx_tiles):
    B, T, _ = x1.shape
    tm = ROW_TILE
    tok = lambda w: pl.BlockSpec((None, tm, w), lambda b, i: (b, i, 0))
    modspec = lambda: pl.BlockSpec((None, None, 1, 6 * D), lambda b, i: (b, jnp.where(i < n_ctx_tiles, 0, 1), 0, 0))
    full = lambda r, c: pl.BlockSpec((r, c), lambda b, i: (0, 0))
    return pl.pallas_call(
        _post_ffn_mla_kernel,
        grid=(B, T // tm),
        in_specs=[tok(D), tok(D), tok(D), full(D, 2 * SH_FF), full(SH_FF, D), modspec(), modspec(),
                  full(1, D), full(D, MLA_IN_PAD)],
        out_specs=[tok(D), tok(MLA_IN_PAD)],
        out_shape=[jax.ShapeDtypeStruct((B, T, D), f32), jax.ShapeDtypeStruct((B, T, MLA_IN_PAD), f32)],
        compiler_params=_cparams(("arbitrary", "arbitrary")),
        name="post_ffn_mla",
    )(x1, routed, fin, sh_gu, sh_d, modtab0, modtab1, gain, w_in)


def _post_ffn_final_kernel(x1_ref, routed_ref, fin_ref, shgu_ref, shd_ref, mod_ref, out_ref):
    out_ref[...] = x1_ref[...] + mod_ref[...][:, 5 * D:6 * D] * (
        routed_ref[...] + _shared_ffn(fin_ref[...], shgu_ref, shd_ref))


def _post_ffn_final(x1, routed, fin, sh_gu, sh_d, modtab):
    B, N, _ = x1.shape
    tm = ROW_TILE
    tok = lambda w: pl.BlockSpec((None, tm, w), lambda b, i: (b, i, 0))
    full = lambda r, c: pl.BlockSpec((r, c), lambda b, i: (0, 0))
    return pl.pallas_call(
        _post_ffn_final_kernel,
        grid=(B, N // tm),
        in_specs=[tok(D), tok(D), tok(D), full(D, 2 * SH_FF), full(SH_FF, D),
                  pl.BlockSpec((None, None, 1, 6 * D), lambda b, i: (b, 1, 0, 0))],
        out_specs=tok(D),
        out_shape=jax.ShapeDtypeStruct((B, N, D), f32),
        compiler_params=_cparams(("arbitrary", "arbitrary")),
        name="post_ffn_final",
    )(x1, routed, fin, sh_gu, sh_d, modtab)


def _mla_prep_kernel(a_ref, qan_ref, wqn_ref, wqr_ref, kvan_ref, wk_ref, wv_ref, qnn_ref, qnr_ref, knn_ref, knr_ref,
                     cos_ref, sin_ref, q_ref, k_ref, v_ref):
    a = a_ref[...]
    tm = a.shape[0]
    scale = MLA_QK ** -0.5 * float(np.log2(np.e))
    cos = cos_ref[...]
    sin = sin_ref[...]
    lane = lax.broadcasted_iota(i32, (tm, LANES), 1)
    first = (lane // (MLA_ROPE // 4)) % 2 == 0

    def rope(xb):
        sw = jnp.where(first, pltpu.roll(xb, LANES - MLA_ROPE // 4, axis=1), pltpu.roll(xb, MLA_ROPE // 4, axis=1))
        return xb * cos + sw * sin

    qa = (_rms(a[:, :MLA_QR]) * qan_ref[...]).astype(bf16)
    qn = _dot(qa, wqn_ref[...])
    qr = _dot(qa, wqr_ref[...])
    ri = lax.broadcasted_iota(i32, (MLA_H * MLA_ROPE, MLA_H * MLA_ROPE), 0) // MLA_ROPE
    ci = lax.broadcasted_iota(i32, (MLA_H * MLA_ROPE, MLA_H * MLA_ROPE), 1) // MLA_ROPE
    seg = jnp.where(ri == ci, 1.0, 0.0)
    ssq = jnp.dot(qr * qr, seg, preferred_element_type=f32, precision=lax.Precision.HIGHEST)
    qr = qr * lax.rsqrt(ssq * (1.0 / MLA_ROPE) + EPS) * qnr_ref[...]
    qr_blocks = [rope(qr[:, p * LANES:(p + 1) * LANES]) * scale for p in range(MLA_H // 2)]

    kv = (_rms(a[:, MLA_QR:MLA_QR + MLA_KVR]) * kvan_ref[...]).astype(bf16)
    kn = _dot(kv, wk_ref[...])
    v_ref[...] = _dot_nt(wv_ref[...], kv).astype(bf16)
    kr = a[:, MLA_QR + MLA_KVR:MLA_IN_PAD]
    kr = rope(_rms(kr, MLA_ROPE) * knr_ref[...])
    kr_odd = pltpu.roll(kr, MLA_ROPE, axis=1)
    for hd in range(MLA_H):
        sl = slice(hd * MLA_NOPE, (hd + 1) * MLA_NOPE)
        q_ref[:, 2 * hd * LANES:(2 * hd + 1) * LANES] = (_rms(qn[:, sl]) * qnn_ref[...] * scale).astype(bf16)
        q_ref[:, (2 * hd + 1) * LANES:(2 * hd + 2) * LANES] = qr_blocks[hd // 2].astype(bf16)
        k_ref[:, 2 * hd * LANES:(2 * hd + 1) * LANES] = (_rms(kn[:, sl]) * knn_ref[...]).astype(bf16)
        k_ref[:, (2 * hd + 1) * LANES:(2 * hd + 2) * LANES] = (kr if hd % 2 == 0 else kr_odd).astype(bf16)


def _mla_prep(a, qan, wqn, wqr, kvan, wk, wv, qnn, qnr, knn, knr, cos_t, sin_t, n_ctx_tiles):
    B, T, _ = a.shape
    tm = ROW_TILE
    tok = lambda w: pl.BlockSpec((None, tm, w), lambda b, i: (b, i, 0))
    full = lambda r, c: pl.BlockSpec((r, c), lambda b, i: (0, 0))
    hw = 2 * LANES * MLA_H
    q_spec = pl.BlockSpec((None, tm, hw), lambda b, i: (b, jnp.maximum(i - n_ctx_tiles, 0), 0))
    return pl.pallas_call(
        _mla_prep_kernel,
        grid=(B, T // tm),
        in_specs=[tok(MLA_IN_PAD), full(1, MLA_QR), full(MLA_QR, MLA_H * MLA_NOPE), full(MLA_QR, MLA_H * MLA_ROPE),
                  full(1, MLA_KVR), full(MLA_KVR, MLA_H * MLA_NOPE), full(MLA_H * MLA_V, MLA_KVR),
                  full(1, MLA_NOPE), full(1, MLA_H * MLA_ROPE), full(1, MLA_NOPE), full(1, LANES),
                  pl.BlockSpec((tm, LANES), lambda b, i: (i, 0)), pl.BlockSpec((tm, LANES), lambda b, i: (i, 0))],
        out_specs=[q_spec, tok(hw), pl.BlockSpec((None, MLA_H * MLA_V, tm), lambda b, i: (b, 0, i))],
        out_shape=[jax.ShapeDtypeStruct((B, T - n_ctx_tiles * tm, hw), bf16), jax.ShapeDtypeStruct((B, T, hw), bf16),
                   jax.ShapeDtypeStruct((B, MLA_H * MLA_V, T), bf16)],
        compiler_params=_cparams(("arbitrary", "arbitrary")),
        name="mla_prep",
    )(a, qan, wqn, wqr, kvan, wk, wv, qnn, qnr, knn, knr, cos_t, sin_t)


ATT_KV_CHUNK = 256
ATT_TQ = 512


def _mla_attn_kernel(q_ref, k_ref, vt_ref, o_ref, s_ref, m_ref):
    tq = q_ref.shape[0]
    ck = ATT_KV_CHUNK
    nck = k_ref.shape[0] // ck
    i = pl.program_id(0)

    @pl.when(i == 0)
    def _():
        s_ref[...] = jnp.zeros_like(s_ref)
        m_ref[...] = jnp.zeros_like(m_ref)

    def fold(a, op):
        return op(a.reshape(ck // SUBLANES, SUBLANES, tq), axis=0)

    def stage(par):
        q = q_ref[...]
        mpart = jnp.full((SUBLANES, tq), -jnp.inf, f32)
        for j in range(nck):
            s = _dot_nt(k_ref[j * ck:(j + 1) * ck, :], q)
            s_ref[par, j * ck:(j + 1) * ck, :] = s
            mpart = jnp.maximum(mpart, fold(s, jnp.max))
        m_ref[par] = mpart
        m = jnp.max(m_ref[1 - par], axis=0, keepdims=True)
        lpart = jnp.zeros((SUBLANES, tq), f32)
        acc = jnp.zeros((MLA_V, tq), f32)
        for j in range(nck):
            p = jnp.exp2(s_ref[1 - par, j * ck:(j + 1) * ck, :] - m)
            lpart = lpart + fold(p, jnp.sum)
            acc = acc + _dot(vt_ref[:, j * ck:(j + 1) * ck], p.astype(bf16))
        o_ref[...] = (acc / jnp.sum(lpart, axis=0, keepdims=True)).astype(bf16)

    pl.when(i % 2 == 0)(functools.partial(stage, 0))
    pl.when(i % 2 == 1)(functools.partial(stage, 1))


def _mla_attn(q, k, vt):
    B, S, _ = q.shape
    T = k.shape[1]
    tq = ATT_TQ if S % ATT_TQ == 0 else ROW_TILE
    nq = S // tq
    ntile = B * MLA_H * nq
    assert T % ATT_KV_CHUNK == 0

    def tile(g):
        g = jnp.clip(g, 0, ntile - 1)
        return g // (MLA_H * nq), (g // nq) % MLA_H, g % nq

    def q_index(g):
        b, h, i = tile(g)
        return b, i, h

    def k_index(g):
        b, h, _ = tile(g)
        return b, 0, h

    def vt_index(g):
        b, h, _ = tile(g - 1)
        return b, h, 0

    def o_index(g):
        b, h, i = tile(g - 1)
        return b, h, i

    return pl.pallas_call(
        _mla_attn_kernel,
        grid=(ntile + 1,),
        in_specs=[pl.BlockSpec((None, tq, 2 * LANES), q_index),
                  pl.BlockSpec((None, T, 2 * LANES), k_index),
                  pl.BlockSpec((None, MLA_V, T), vt_index)],
        out_specs=pl.BlockSpec((None, MLA_V, tq), o_index),
        out_shape=jax.ShapeDtypeStruct((B, MLA_H * MLA_V, S), bf16),
        scratch_shapes=[pltpu.VMEM((2, T, tq), f32), pltpu.VMEM((2, SUBLANES, tq), f32)],
        compiler_params=_cparams(("arbitrary",)),
        name="mla_attn",
    )(q, k, vt)


def _axial_angles(rows_n, rot_dim):
    axis_dim = rot_dim // 2
    inv = ROPE_BASE ** (-jnp.arange(0, axis_dim, 2, dtype=f32) / axis_dim)
    row = jnp.repeat(jnp.arange(rows_n, dtype=f32), GRID_W)
    col = jnp.tile(jnp.arange(GRID_W, dtype=f32), rows_n)
    return row[:, None] * inv, col[:, None] * inv


def _rope_tables(seq, n_ctx, rot_dim, reps):
    ang_r, ang_c = _axial_angles(seq // GRID_W, rot_dim)
    cos = jnp.concatenate([jnp.cos(ang_r)] * 2 + [jnp.cos(ang_c)] * 2, axis=1)
    sin = jnp.concatenate([-jnp.sin(ang_r), jnp.sin(ang_r), -jnp.sin(ang_c), jnp.sin(ang_c)], axis=1)
    cos = jnp.concatenate([jnp.ones((n_ctx, rot_dim), f32), cos], axis=0)
    sin = jnp.concatenate([jnp.zeros((n_ctx, rot_dim), f32), sin], axis=0)
    return jnp.tile(cos, (1, reps)), jnp.tile(sin, (1, reps))


def kernel(x, c, ctx, c_ctx, ada_w, ada_b, norm_mix, norm_ffn, ret_w_in, ret_decay_f, ret_decay_b, ret_w_o,
           mla_w_in, mla_q_a_norm, mla_w_q_b, mla_kv_a_norm, mla_w_kv_b, mla_q_norm, mla_k_norm, mla_w_o,
           router_w, router_bias, exp_w_gu, exp_w_down, sh_w_gu, sh_w_down):
    B, S, _ = x.shape
    n_ctx = ctx.shape[1]
    assert n_ctx % ROW_TILE == 0 and S % ROW_TILE == 0 and S % GRID_W == 0
    n_ctx_tiles = n_ctx // ROW_TILE

    rows = -(-(B + 1) // SUBLANES) * SUBLANES
    cc = jnp.zeros((rows, D), f32).at[:B].set(c).at[B].set(c_ctx)
    mod = _ada(cc, ada_w, ada_b)

    def modtab(i):
        ctx_row = jnp.broadcast_to(mod[i, B][None, :], (B, 6 * D))
        return jnp.stack([ctx_row, mod[i, :B]], axis=1)[:, :, None, :]

    mod0, mod1 = modtab(0), modtab(1)
    xc = jnp.concatenate([ctx, x], axis=1)

    cos_r, sin_r = _rope_tables(S, n_ctx, RET_DK, 1)
    q, k, v, gf, gb = _ret_inproj(xc, mod0, norm_mix[0][None, :], ret_w_in[0].astype(bf16), cos_r, sin_r, n_ctx_tiles)
    dtab = jnp.broadcast_to(jnp.concatenate([ret_decay_f[0], ret_decay_b[0]])[:, None], (2 * RET_H, LANES))
    o = _ret_scan(dtab, q, k, v, gf, gb, n_ctx)
    x1, fin, hlin, e_t, w_t, r_t, cnt = _post_mix(
        o, ret_w_o[0].astype(bf16), xc, 0, mod0, n_ctx_tiles, norm_ffn[0][None, :],
        router_w[0].T, router_bias[0][:, None])
    routed = _routed_experts(e_t, w_t, r_t, cnt, hlin, exp_w_gu[0].astype(bf16), exp_w_down[0].astype(bf16))
    w_in1 = jnp.zeros((D, MLA_IN_PAD), f32).at[:, :mla_w_in.shape[2]].set(mla_w_in[0]).astype(bf16)
    x2, a = _post_ffn_mla(x1, routed, fin, sh_w_gu[0].astype(bf16), sh_w_down[0].astype(bf16), mod0, mod1,
                          norm_mix[1][None, :], w_in1, n_ctx_tiles)

    wq = mla_w_q_b[0].reshape(MLA_QR, MLA_H, MLA_QK)
    wqn = wq[:, :, :MLA_NOPE].reshape(MLA_QR, MLA_H * MLA_NOPE).astype(bf16)
    wqr = wq[:, :, MLA_NOPE:].reshape(MLA_QR, MLA_H * MLA_ROPE).astype(bf16)
    wkv = mla_w_kv_b[0].reshape(MLA_KVR, MLA_H, MLA_NOPE + MLA_V)
    wk = wkv[:, :, :MLA_NOPE].reshape(MLA_KVR, MLA_H * MLA_NOPE).astype(bf16)
    wv = wkv[:, :, MLA_NOPE:].reshape(MLA_KVR, MLA_H * MLA_V).T.astype(bf16)
    qnn = mla_q_norm[0][None, :MLA_NOPE]
    qnr = jnp.tile(mla_q_norm[0][None, MLA_NOPE:], (1, MLA_H))
    knn = mla_k_norm[0][None, :MLA_NOPE]
    knr = jnp.concatenate([mla_k_norm[0][MLA_NOPE:], jnp.zeros((LANES - MLA_ROPE,), f32)])[None, :]
    cos_m, sin_m = _rope_tables(S, n_ctx, MLA_ROPE, LANES // MLA_ROPE)
    qf, kf, vf = _mla_prep(a, mla_q_a_norm[0][None, :], wqn, wqr, mla_kv_a_norm[0][None, :], wk, wv,
                           qnn, qnr, knn, knr, cos_m, sin_m, n_ctx_tiles)
    o1 = _mla_attn(qf, kf, vf)
    x3, fin1, hlin1, e1, w1, r1, cnt1 = _post_mix(
        o1, mla_w_o[0].astype(bf16), x2, n_ctx_tiles, mod1, 0, norm_ffn[1][None, :],
        router_w[1].T, router_bias[1][:, None], o_transposed=True)
    routed1 = _routed_experts(e1, w1, r1, cnt1, hlin1, exp_w_gu[1].astype(bf16), exp_w_down[1].astype(bf16))
    return _post_ffn_final(x3, routed1, fin1, sh_w_gu[1].astype(bf16), sh_w_down[1].astype(bf16), mod1)
```

```python
import dataclasses
import functools

import jax
import jax.numpy as jnp
import numpy as np
from jax import lax
from jax.experimental import pallas as pl
from jax.experimental.pallas import tpu as pltpu
from jax.experimental.pallas import tpu_sc as plsc

f32 = jnp.float32
bf16 = jnp.bfloat16
i32 = jnp.int32

D = 1024
GRID_W = 64
EPS = 1e-6
ROPE_BASE = 10000.0
RET_H = 4
RET_DK = 256
RET_DV = 512
RET_VW = RET_H * RET_DV
RET_CHUNK = 256
MLA_H = 8
MLA_NOPE = 128
MLA_ROPE = 64
MLA_QK = MLA_NOPE + MLA_ROPE
MLA_V = 128
MLA_QR = 384
MLA_KVR = 256
MLA_IN_PAD = 768
N_EXP = 64
TOP_K = 8
N_GRP = 8
TOPK_GRP = 4
EXP_FF = 256
SH_FF = 256
ROUTED_SCALE = 2.5

LANES = 128
SUBLANES = 8
SC_LANES = 16
ROW_TILE = 256
MOE_TILE = 256
MOE_GROUP = 2
VMEM_LIMIT = 56 * 1024 * 1024
MOE_VMEM_LIMIT = 62 * 1024 * 1024


def _cparams(sem, vmem=VMEM_LIMIT):
    return pltpu.CompilerParams(dimension_semantics=sem, vmem_limit_bytes=vmem)


def _sigmoid(x):
    return 1.0 / (1.0 + jnp.exp(-x))


def _silu(x):
    return x * _sigmoid(x)


def _rms(x, n=None):
    n = x.shape[-1] if n is None else n
    return x * lax.rsqrt(jnp.sum(x * x, axis=-1, keepdims=True) * (1.0 / n) + EPS)


def _dot(a, b):
    return jnp.dot(a, b, preferred_element_type=f32)


def _dot_nt(a, b, precision=None):
    return lax.dot_general(a, b, (((1,), (1,)), ((), ())), preferred_element_type=f32, precision=precision)


def _ada_kernel(c_ref, w_ref, b_ref, o_ref):
    s = _silu(c_ref[...]).astype(bf16)
    o_ref[...] = _dot(s, w_ref[...].astype(bf16)) + b_ref[...]


def _ada(cc, ada_w, ada_b):
    depth = ada_w.shape[0]
    rows = cc.shape[0]
    tn = 1536
    return pl.pallas_call(
        _ada_kernel,
        grid=(depth, 6 * D // tn),
        in_specs=[pl.BlockSpec((rows, D), lambda i, j: (0, 0)),
                  pl.BlockSpec((None, D, tn), lambda i, j: (i, 0, j)),
                  pl.BlockSpec((None, 1, tn), lambda i, j: (i, 0, j))],
        out_specs=pl.BlockSpec((None, rows, tn), lambda i, j: (i, 0, j)),
        out_shape=jax.ShapeDtypeStruct((depth, rows, 6 * D), f32),
        compiler_params=_cparams(("arbitrary", "arbitrary")),
        name="ada",
    )(cc, ada_w, ada_b.reshape(depth, 1, 6 * D))


def _ret_inproj_kernel(x_ref, mod_ref, g_ref, w_ref, cos_ref, sin_ref, q_ref, k_ref, v_ref, gf_ref, gb_ref):
    x = x_ref[...]
    mod = mod_ref[...]
    h = (_rms(x) * g_ref[...]) * (1.0 + mod[:, D:2 * D]) + mod[:, 0:D]
    hb = h.astype(bf16)
    cos = cos_ref[...]
    sin = sin_ref[...]

    def rope(a):
        outs = []
        for half in range(2):
            sl = slice(half * LANES, (half + 1) * LANES)
            ah = a[:, sl]
            outs.append(ah * cos[:, sl] + pltpu.roll(ah, LANES // 2, axis=1) * sin[:, sl])
        return jnp.concatenate(outs, axis=1)

    for hd in range(RET_H):
        sl = slice(hd * RET_DK, (hd + 1) * RET_DK)
        q_ref[:, sl] = rope(_dot(hb, w_ref[:, sl])).astype(bf16)
    for hd in range(RET_H):
        sl = slice(hd * RET_DK, (hd + 1) * RET_DK)
        wsl = slice(D + hd * RET_DK, D + (hd + 1) * RET_DK)
        k_ref[:, sl] = (rope(_dot(hb, w_ref[:, wsl])) * (RET_DK ** -0.5)).astype(bf16)
    cw = 512
    for c in range(RET_VW // cw):
        sl = slice(c * cw, (c + 1) * cw)
        v_ref[:, sl] = _dot(hb, w_ref[:, 2 * D + c * cw:2 * D + (c + 1) * cw]).astype(bf16)
        gf_ref[:, sl] = _silu(_dot(hb, w_ref[:, 2 * D + RET_VW + c * cw:2 * D + RET_VW + (c + 1) * cw])).astype(bf16)
        gb_ref[:, sl] = _silu(_dot(hb, w_ref[:, 2 * D + 2 * RET_VW + c * cw:2 * D + 2 * RET_VW + (c + 1) * cw])).astype(bf16)


def _ret_inproj(xc, modtab, gain, w_in, cos_t, sin_t, n_ctx_tiles):
    B, T, _ = xc.shape
    tm = ROW_TILE
    n_in = w_in.shape[1]
    tok = lambda w: pl.BlockSpec((None, tm, w), lambda b, i: (b, i, 0))
    return pl.pallas_call(
        _ret_inproj_kernel,
        grid=(B, T // tm),
        in_specs=[tok(D),
                  pl.BlockSpec((None, None, 1, 6 * D), lambda b, i: (b, jnp.where(i < n_ctx_tiles, 0, 1), 0, 0)),
                  pl.BlockSpec((1, D), lambda b, i: (0, 0)),
                  pl.BlockSpec((D, n_in), lambda b, i: (0, 0), pipeline_mode=pl.Buffered(1)),
                  pl.BlockSpec((tm, RET_DK), lambda b, i: (i, 0)),
                  pl.BlockSpec((tm, RET_DK), lambda b, i: (i, 0))],
        out_specs=[tok(D), tok(D), tok(RET_VW), tok(RET_VW), tok(RET_VW)],
        out_shape=[jax.ShapeDtypeStruct((B, T, D), bf16), jax.ShapeDtypeStruct((B, T, D), bf16),
                   jax.ShapeDtypeStruct((B, T, RET_VW), bf16), jax.ShapeDtypeStruct((B, T, RET_VW), bf16),
                   jax.ShapeDtypeStruct((B, T, RET_VW), bf16)],
        compiler_params=_cparams(("arbitrary", "arbitrary")),
        name="ret_inproj",
    )(xc, modtab, gain, w_in, cos_t, sin_t)


def _ret_chunk_index(t, nc, ncc):
    u = t - nc
    back = jnp.where(u < ncc, ncc - 1 - u, nc - 1 - u + ncc)
    return jnp.where(t < nc, t, back)


def _ret_scan_kernel(dt_ref, q_ref, k_ref, v_ref, gf_ref, gb_ref, o_ref,
                     s_ref, of_ref, mask_ref, dq_ref, dk_ref, dc_ref, *, nc, ncc):
    t = pl.program_id(1)
    C = RET_CHUNK

    def init(direction):
        s_ref[...] = jnp.zeros_like(s_ref)
        ii = lax.broadcasted_iota(i32, (C, C), 0)
        jj = lax.broadcasted_iota(i32, (C, C), 1)
        rel = (ii - jj if direction == 0 else jj - ii).astype(f32)
        pos = lax.broadcasted_iota(i32, (C, 1), 0).astype(f32)
        for hd in range(RET_H):
            r = direction * RET_H + hd
            lg = -jnp.exp(dt_ref[r:r + 1, :])
            lg1 = lg[:, 0:1]
            mask_ref[hd] = jnp.where(rel >= 0, jnp.exp(lg1 * jnp.maximum(rel, 0.0)), 0.0)
            if direction == 0:
                dq_ref[hd] = jnp.exp(lg1 * (pos + 1.0))
                dk_ref[hd] = jnp.exp(lg1 * (C - 1.0 - pos))
            else:
                dq_ref[hd] = jnp.exp(lg1 * (C - pos))
                dk_ref[hd] = jnp.exp(lg1 * pos)
            dc_ref[hd] = jnp.exp(lg * float(C))

    pl.when(t == 0)(functools.partial(init, 0))
    pl.when(t == nc)(functools.partial(init, 1))

    fwd = t < nc
    row0 = pl.multiple_of(_ret_chunk_index(t, nc, ncc) * C, C)

    for hd in range(RET_H):
        ks = slice(hd * RET_DK, (hd + 1) * RET_DK)
        vs = slice(hd * RET_DV, (hd + 1) * RET_DV)
        qh = q_ref[:, ks]
        kh = k_ref[:, ks]
        vh = v_ref[:, vs]
        p = (_dot_nt(qh, kh) * mask_ref[hd]).astype(bf16)
        y = _dot(p, vh) + _dot(qh, s_ref[hd].astype(bf16)) * dq_ref[hd]
        kd = (kh.astype(f32) * dk_ref[hd]).astype(bf16)
        upd = lax.dot_general(kd, vh, (((0,), (0,)), ((), ())), preferred_element_type=f32)
        s_ref[hd] = s_ref[hd] * dc_ref[hd][0:1, 0:1] + upd
        yn = _rms(y)

        @pl.when(fwd)
        def _():
            of_ref[pl.ds(row0, C), vs] = (gf_ref[:, vs].astype(f32) * yn).astype(bf16)

        @pl.when(jnp.logical_not(fwd))
        def _():
            o_ref[:, vs] = (of_ref[pl.ds(row0, C), vs].astype(f32) + gb_ref[:, vs].astype(f32) * yn).astype(bf16)


def _ret_scan(dtab, q, k, v, gf, gb, n_ctx):
    B, T, _ = q.shape
    C = RET_CHUNK
    nc = T // C
    ncc = n_ctx // C
    cidx = functools.partial(_ret_chunk_index, nc=nc, ncc=ncc)
    first_back = ncc - 1
    return pl.pallas_call(
        functools.partial(_ret_scan_kernel, nc=nc, ncc=ncc),
        grid=(B, 2 * nc),
        in_specs=[pl.BlockSpec((2 * RET_H, LANES), lambda b, t: (0, 0)),
                  pl.BlockSpec((None, C, D), lambda b, t: (b, cidx(t), 0)),
                  pl.BlockSpec((None, C, D), lambda b, t: (b, cidx(t), 0)),
                  pl.BlockSpec((None, C, RET_VW), lambda b, t: (b, cidx(t), 0)),
                  pl.BlockSpec((None, C, RET_VW), lambda b, t: (b, jnp.where(t < nc, t, nc - 1), 0)),
                  pl.BlockSpec((None, C, RET_VW), lambda b, t: (b, jnp.where(t < nc, first_back, cidx(t)), 0))],
        out_specs=pl.BlockSpec((None, C, RET_VW), lambda b, t: (b, jnp.where(t < nc, first_back, cidx(t)), 0)),
        out_shape=jax.ShapeDtypeStruct((B, T, RET_VW), bf16),
        scratch_shapes=[pltpu.VMEM((RET_H, RET_DK, RET_DV), f32),
                        pltpu.VMEM((T, RET_VW), bf16),
                        pltpu.VMEM((RET_H, C, C), f32),
                        pltpu.VMEM((RET_H, C, 1), f32),
                        pltpu.VMEM((RET_H, C, 1), f32),
                        pltpu.VMEM((RET_H, 1, LANES), f32)],
        compiler_params=_cparams(("arbitrary", "arbitrary")),
        name="ret_scan",
    )(dtab, q, k, v, gf, gb)


def _route(f, rwt_ref, rb_ref, cnt_ref, e_ref, w_ref, r_ref):
    tm = f.shape[0]
    G = N_EXP // N_GRP
    logits = _dot_nt(rwt_ref[...], f, precision=lax.Precision.HIGHEST)
    s = _sigmoid(logits)
    sel = s + rb_ref[...]
    mi = lax.broadcasted_iota(i32, (G, tm), 0)
    neg = -jnp.inf
    s_g = [s[g * G:(g + 1) * G, :] for g in range(N_GRP)]
    sel_g = [sel[g * G:(g + 1) * G, :] for g in range(N_GRP)]

    def first_max(a, ids, big):
        mx = jnp.max(a, axis=0, keepdims=True)
        ix = jnp.min(jnp.where(a == mx, ids, big), axis=0, keepdims=True)
        return mx, ix

    gscore = jnp.zeros((N_GRP, tm), f32)
    gi = lax.broadcasted_iota(i32, (N_GRP, tm), 0)
    for g in range(N_GRP):
        t1, i1 = first_max(sel_g[g], mi, G)
        t2 = jnp.max(jnp.where(mi == i1, neg, sel_g[g]), axis=0, keepdims=True)
        gscore = jnp.where(gi == g, t1 + t2, gscore)
    gmask = jnp.zeros((N_GRP, tm), i32)
    cur = gscore
    for _ in range(TOPK_GRP):
        _, ix = first_max(cur, gi, N_GRP)
        hit = gi == ix
        gmask = jnp.where(hit, 1, gmask)
        cur = jnp.where(hit, neg, cur)
    cand = [jnp.where(gmask[g:g + 1, :] > 0, sel_g[g], neg) for g in range(N_GRP)]
    ids = [mi + g * G for g in range(N_GRP)]

    e_rows, w_rows = [], []
    for _ in range(TOP_K):
        mx = cand[0].max(axis=0, keepdims=True)
        for g in range(1, N_GRP):
            mx = jnp.maximum(mx, cand[g].max(axis=0, keepdims=True))
        ix = jnp.min(jnp.where(cand[0] == mx, ids[0], N_EXP), axis=0, keepdims=True)
        for g in range(1, N_GRP):
            ix = jnp.minimum(ix, jnp.min(jnp.where(cand[g] == mx, ids[g], N_EXP), axis=0, keepdims=True))
        wv = jnp.zeros((1, tm), f32)
        for g in range(N_GRP):
            hit = ids[g] == ix
            cand[g] = jnp.where(hit, neg, cand[g])
            wv = wv + jnp.sum(jnp.where(hit, s_g[g], 0.0), axis=0, keepdims=True)
        e_rows.append(ix)
        w_rows.append(wv)
    wsum = w_rows[0]
    for r in range(1, TOP_K):
        wsum = wsum + w_rows[r]

    selm = [jnp.zeros((G, tm), f32) for _ in range(N_GRP)]
    for r in range(TOP_K):
        for g in range(N_GRP):
            selm[g] = jnp.where(ids[g] == e_rows[r], 1.0, selm[g])
    m_all = jnp.concatenate(selm, axis=0)
    ri = lax.broadcasted_iota(i32, (tm, tm), 0)
    ci = lax.broadcasted_iota(i32, (tm, tm), 1)
    upper = jnp.where(ri <= ci, 1.0, 0.0).astype(bf16)
    incl = _dot(m_all.astype(bf16), upper)
    carry = cnt_ref[:, 0:1]
    rank_all = carry + incl - m_all
    cnt_ref[...] = cnt_ref[...] + incl[:, tm - 1:tm]
    for r in range(TOP_K):
        rk = jnp.zeros((1, tm), f32)
        for g in range(N_GRP):
            rk = rk + jnp.sum(jnp.where(ids[g] == e_rows[r], rank_all[g * G:(g + 1) * G, :], 0.0), axis=0, keepdims=True)
        e_ref[r:r + 1, :] = e_rows[r]
        w_ref[r:r + 1, :] = w_rows[r] / wsum * ROUTED_SCALE
        r_ref[r:r + 1, :] = rk.astype(i32)


def _post_mix_kernel(o_ref, wo_ref, x_ref, mod_ref, g_ref, rwt_ref, rb_ref,
                     x1_ref, fin_ref, hlin_ref, e_ref, w_ref, r_ref, cnt_ref, *, nt, o_transposed):
    i = pl.program_id(1)

    @pl.when(i == 0)
    def _():
        cnt_ref[...] = jnp.zeros_like(cnt_ref)

    @pl.when(i < nt)
    def _():
        mod = mod_ref[...]
        if o_transposed:
            mixed = lax.dot_general(o_ref[...], wo_ref[...], (((0,), (0,)), ((), ())), preferred_element_type=f32)
        else:
            mixed = _dot(o_ref[...], wo_ref[...])
        x1 = x_ref[...] + mod[:, 2 * D:3 * D] * mixed
        x1_ref[...] = x1
        f = (_rms(x1) * g_ref[...]) * (1.0 + mod[:, 4 * D:5 * D]) + mod[:, 3 * D:4 * D]
        fin_ref[...] = f.astype(bf16)
        tm = f.shape[0]
        for j in range(D // LANES):
            hlin_ref[pl.ds(j, tm, stride=D // LANES), :] = f[:, j * LANES:(j + 1) * LANES]
        _route(f, rwt_ref, rb_ref, cnt_ref, e_ref, w_ref, r_ref)

    @pl.when(i == nt)
    def _():
        hlin_ref[...] = jnp.zeros_like(hlin_ref)


def _post_mix(o, w_o, xs, x_tile_off, modtab, n_ctx_tiles, gain, rwt, rb, o_transposed=False):
    if o_transposed:
        B, KO, N = o.shape
    else:
        B, N, KO = o.shape
    tm = ROW_TILE
    nt = N // tm
    last = lambda i: jnp.minimum(i, nt - 1)
    tok = lambda w: pl.BlockSpec((None, tm, w), lambda b, i: (b, last(i), 0))
    sel = lambda: pl.BlockSpec((None, TOP_K, tm), lambda b, i: (b, 0, last(i)))
    o_spec = (pl.BlockSpec((None, KO, tm), lambda b, i: (b, 0, last(i))) if o_transposed else tok(KO))
    return pl.pallas_call(
        functools.partial(_post_mix_kernel, nt=nt, o_transposed=o_transposed),
        grid=(B, nt + 1),
        in_specs=[o_spec,
                  pl.BlockSpec((KO, D), lambda b, i: (0, 0)),
                  pl.BlockSpec((None, tm, D), lambda b, i: (b, last(i) + x_tile_off, 0)),
                  pl.BlockSpec((None, None, 1, 6 * D), lambda b, i: (b, jnp.where(last(i) < n_ctx_tiles, 0, 1), 0, 0)),
                  pl.BlockSpec((1, D), lambda b, i: (0, 0)),
                  pl.BlockSpec((N_EXP, D), lambda b, i: (0, 0)),
                  pl.BlockSpec((N_EXP, 1), lambda b, i: (0, 0))],
        out_specs=[tok(D), tok(D),
                   pl.BlockSpec((None, tm * (D // LANES), LANES), lambda b, i: (b, i, 0)),
                   sel(), sel(), sel(),
                   pl.BlockSpec((None, N_EXP, LANES), lambda b, i: (b, 0, 0))],
        out_shape=[jax.ShapeDtypeStruct((B, N, D), f32), jax.ShapeDtypeStruct((B, N, D), bf16),
                   jax.ShapeDtypeStruct((B, (N + tm) * (D // LANES), LANES), f32),
                   jax.ShapeDtypeStruct((B, TOP_K, N), i32), jax.ShapeDtypeStruct((B, TOP_K, N), f32),
                   jax.ShapeDtypeStruct((B, TOP_K, N), i32),
                   jax.ShapeDtypeStruct((B, N_EXP, LANES), f32)],
        compiler_params=_cparams(("arbitrary", "arbitrary")),
        name="post_mix",
    )(o, w_o, xs, modtab, gain, rwt, rb)


META_W = 256
PLAN_ALIGN = 1024


def _round_up(n, m):
    return -(-n // m) * m


def _moe_sizes(n_tok):
    tm = MOE_TILE
    nt_max = (n_tok * TOP_K + N_EXP * (tm - 1)) // tm + 1
    ntp = _round_up(nt_max + 1, PLAN_ALIGN // tm)
    assert ntp <= META_W
    return ntp, _round_up(n_tok, PLAN_ALIGN), _round_up(n_tok * TOP_K + SUBLANES, PLAN_ALIGN)


def _plan_kernel(e_ref, r_ref, cnt_ref, pos_ref, meta_ref, *, n_tok):
    tm = MOE_TILE
    ntile = jnp.floor((cnt_ref[...] + (tm - 1.0)) * (1.0 / tm))
    ntb = ntile.astype(bf16)
    ei = lax.broadcasted_iota(i32, (N_EXP, LANES), 0)
    ej = lax.broadcasted_iota(i32, (N_EXP, LANES), 1)
    lower = jnp.where(ej <= ei, 1.0, 0.0)[:, :N_EXP].astype(bf16)
    tend = _dot(lower, ntb)
    tstart = tend - ntile
    tt = lax.broadcasted_iota(i32, (N_EXP, META_W), 1).astype(f32)
    te = jnp.sum(jnp.where(tt >= tend[:, 0:1], 1.0, 0.0), axis=0, keepdims=True)
    meta_ref[...] = jnp.zeros_like(meta_ref)
    meta_ref[0:1, :] = jnp.minimum(te, N_EXP - 1.0).astype(i32)
    meta_ref[1:2, :] = jnp.broadcast_to(tend[N_EXP - 1:N_EXP, 0:1], (1, META_W)).astype(i32)
    e = e_ref[...]
    base = jnp.zeros(e.shape, f32)
    for ex in range(N_EXP):
        base = jnp.where(e == ex, tstart[ex:ex + 1, 0:1] * float(tm), base)
    pos_ref[...] = jnp.zeros_like(pos_ref)
    pos_ref[:, 0:n_tok] = base.astype(i32) + r_ref[...]


def _plan(e_t, r_t, cnt):
    B, K, N = e_t.shape
    _, npad, _ = _moe_sizes(N)
    return pl.pallas_call(
        functools.partial(_plan_kernel, n_tok=N),
        grid=(B,),
        in_specs=[pl.BlockSpec((None, K, N), lambda b: (b, 0, 0)),
                  pl.BlockSpec((None, K, N), lambda b: (b, 0, 0)),
                  pl.BlockSpec((None, N_EXP, LANES), lambda b: (b, 0, 0))],
        out_specs=[pl.BlockSpec((None, K, npad), lambda b: (b, 0, 0)),
                   pl.BlockSpec((None, SUBLANES, META_W), lambda b: (b, 0, 0))],
        out_shape=[jax.ShapeDtypeStruct((B, K, npad), i32), jax.ShapeDtypeStruct((B, SUBLANES, META_W), i32)],
        compiler_params=_cparams(("arbitrary",)),
        name="moe_plan",
    )(e_t, r_t, cnt)


def _plan_invert(pos, n_tok):
    B, K, npad = pos.shape
    ntp, _, _ = _moe_sizes(n_tok)
    plen = ntp * MOE_TILE
    plist = K * npad
    mesh = plsc.VectorSubcoreMesh(core_axis_name="c", subcore_axis_name="s")
    n_cores = mesh.num_cores
    assert B <= n_cores * mesh.num_subcores and n_tok % SC_LANES == 0 and plen % SC_LANES == 0

    @functools.partial(
        pl.kernel, out_type=jax.ShapeDtypeStruct((B * plen,), i32), mesh=mesh,
        scratch_types=[pltpu.VMEM((plist,), i32), pltpu.VMEM((plen,), i32)],
        compiler_params=dataclasses.replace(pltpu.CompilerParams(), needs_layout_passes=False))
    def invert(pos_hbm, sk_hbm, pos_v, sk_v):
        wid = lax.axis_index("s") * n_cores + lax.axis_index("c")

        @pl.when(wid < B)
        def _():
            pltpu.sync_copy(pos_hbm.at[pl.ds(pl.multiple_of(wid * plist, SUBLANES), plist)], pos_v)
            pad = jnp.full((SC_LANES,), n_tok * TOP_K, i32)

            @pl.loop(0, plen, step=SC_LANES)
            def _(i):
                sk_v[pl.ds(i, SC_LANES)] = pad

            lane = lax.iota(i32, SC_LANES)
            for k in range(K):
                @pl.loop(0, n_tok, step=SC_LANES)
                def _(n):
                    plsc.store_scatter(sk_v, [pos_v[pl.ds(k * npad + n, SC_LANES)]], (lane + n) * TOP_K + k)

            pltpu.sync_copy(sk_v, sk_hbm.at[pl.ds(pl.multiple_of(wid * plen, SUBLANES), plen)])

    return invert(pos.reshape(-1))


def _moe_kernel(te_ref, nt_ref, *refs, ntp, n_tok):
    R = MOE_GROUP
    skg_refs, sks_refs = refs[0:R], refs[R:2 * R]
    w_ref, hlin_ref = refs[2 * R], refs[2 * R + 1]
    wgu_refs, wd_refs = refs[2 * R + 2:3 * R + 2], refs[3 * R + 2:4 * R + 2]
    out_ref, acc_ref, xs_ref, ylin_ref = refs[4 * R + 2:]
    b = pl.program_id(0)
    t = pl.program_id(1)
    TM = MOE_TILE
    NCH = D // LANES
    U = 8
    ngrp = ntp // R

    @pl.when(t == 0)
    def _():
        acc_ref[...] = jnp.zeros_like(acc_ref)
        xs_ref[...] = jnp.zeros_like(xs_ref)
        ylin_ref[...] = jnp.zeros_like(ylin_ref)

    def stage(p):
        for r in range(R):
            for m in range(TM):
                sk = skg_refs[r][m]
                off = pl.multiple_of(sk & -NCH, NCH)
                xs_ref[p, r, pl.ds(m * NCH, NCH), :] = hlin_ref[pl.ds(off, NCH), :]
        for r in range(R):
            x = jnp.concatenate([xs_ref[1 - p, r, pl.ds(j, TM, stride=NCH), :] for j in range(NCH)],
                                axis=1).astype(bf16)
            gu = _dot(x, wgu_refs[r][...])
            a = (_silu(gu[:, :EXP_FF]) * gu[:, EXP_FF:]).astype(bf16)
            y = _dot(a, wd_refs[r][...])
            for g in range(TM // SUBLANES):
                for j in range(NCH):
                    ylin_ref[1 - p, r, pl.ds((g * NCH + j) * SUBLANES, SUBLANES), :] = (
                        y[g * SUBLANES:(g + 1) * SUBLANES, j * LANES:(j + 1) * LANES])
        for r in range(R):
            for c in range(TM // U):
                sks = [sks_refs[r][c * U + u] for u in range(U)]
                offs = [pl.multiple_of(s & -NCH, NCH) for s in sks]
                news = []
                for u in range(U):
                    m = c * U + u
                    yrow = ylin_ref[p, r, pl.ds((m // SUBLANES) * (NCH * SUBLANES) + m % SUBLANES, NCH,
                                               stride=SUBLANES), :]
                    news.append(acc_ref[pl.ds(offs[u], NCH), :] + w_ref[sks[u]] * yrow)
                for u in range(U):
                    acc_ref[pl.ds(offs[u], NCH), :] = news[u]

    live = (t - 2) * R < nt_ref[b]
    pl.when(jnp.logical_and(live, t % 2 == 0))(functools.partial(stage, 0))
    pl.when(jnp.logical_and(live, t % 2 == 1))(functools.partial(stage, 1))

    @pl.when(t >= ngrp + 2)
    def _():
        row0 = (t - (ngrp + 2)) * (ROW_TILE * NCH)
        for j in range(NCH):
            out_ref[:, j * LANES:(j + 1) * LANES] = acc_ref[pl.ds(row0 + j, ROW_TILE, stride=NCH), :]


def _moe(te, nt, sk, w_flat, hlin, w_gu, w_d):
    B = hlin.shape[0]
    NCH = D // LANES
    n_tok = hlin.shape[1] // NCH - ROW_TILE
    assert n_tok % ROW_TILE == 0
    nf = n_tok // ROW_TILE
    TM = MOE_TILE
    R = MOE_GROUP
    ntp, _, wlen = _moe_sizes(n_tok)
    assert ntp % R == 0
    ngrp = ntp // R
    pad_tile = ntp - 1

    def sk_spec(r, lag):
        def index(b, t, te_ref, nt_ref):
            tile = (t - lag) * R + r
            ok = jnp.logical_and(t >= lag, tile < nt_ref[b])
            return (b * ntp + jnp.where(ok, tile, pad_tile),)
        return pl.BlockSpec((TM,), index, memory_space=pltpu.SMEM)

    def w_spec(shape, r):
        def index(b, t, te_ref, nt_ref):
            tile = jnp.clip((t - 1) * R + r, 0, nt_ref[b] - 1)
            return (te_ref[b * META_W + tile], 0, 0)
        return pl.BlockSpec((None,) + shape, index)

    grid_spec = pltpu.PrefetchScalarGridSpec(
        num_scalar_prefetch=2,
        grid=(B, ngrp + 2 + nf),
        in_specs=([sk_spec(r, 0) for r in range(R)] + [sk_spec(r, 2) for r in range(R)]
                  + [pl.BlockSpec((wlen,), lambda b, t, *_: (b,), memory_space=pltpu.SMEM),
                     pl.BlockSpec((None, (n_tok + ROW_TILE) * NCH, LANES), lambda b, t, *_: (b, 0, 0),
                                  pipeline_mode=pl.Buffered(1))]
                  + [w_spec((D, 2 * EXP_FF), r) for r in range(R)]
                  + [w_spec((EXP_FF, D), r) for r in range(R)]),
        out_specs=pl.BlockSpec((None, ROW_TILE, D), lambda b, t, *_: (b, jnp.maximum(t - (ngrp + 2), 0), 0)),
        scratch_shapes=[pltpu.VMEM(((n_tok + SUBLANES) * NCH, LANES), f32),
                        pltpu.VMEM((2, R, TM * NCH, LANES), f32),
                        pltpu.VMEM((2, R, TM * NCH, LANES), f32)],
    )
    return pl.pallas_call(
        functools.partial(_moe_kernel, ntp=ntp, n_tok=n_tok),
        grid_spec=grid_spec,
        out_shape=jax.ShapeDtypeStruct((B, n_tok, D), f32),
        compiler_params=_cparams(("arbitrary", "arbitrary"), MOE_VMEM_LIMIT),
        name="moe",
    )(te, nt, *([sk] * (2 * R)), w_flat, hlin, *([w_gu] * R), *([w_d] * R))


def _routed_experts(e_t, w_t, r_t, cnt, hlin, w_gu, w_d):
    B, K, N = e_t.shape
    _, _, wlen = _moe_sizes(N)
    pos, meta = _plan(e_t, r_t, cnt)
    te = meta[:, 0, :].reshape(-1)
    nt = meta[:, 1, 0]
    sk = _plan_invert(pos, N)
    w_flat = jnp.zeros((B, wlen), f32).at[:, :N * K].set(jnp.swapaxes(w_t, 1, 2).reshape(B, N * K)).reshape(-1)
    return _moe(te, nt, sk, w_flat, hlin, w_gu, w_d)


def _shared_ffn(fin, shgu_ref, shd_ref):
    gu = _dot(fin, shgu_ref[...])
    return _dot((_silu(gu[:, :SH_FF]) * gu[:, SH_FF:]).astype(bf16), shd_ref[...])


def _post_ffn_mla_kernel(x1_ref, routed_ref, fin_ref, shgu_ref, shd_ref, mod0_ref, mod1_ref, g_ref, win_ref,
                         x2_ref, a_ref):
    x2 = x1_ref[...] + mod0_ref[...][:, 5 * D:6 * D] * (routed_ref[...] + _shared_ffn(fin_ref[...], shgu_ref, shd_ref))
    x2_ref[...] = x2
    mod1 = mod1_ref[...]
    h = (_rms(x2) * g_ref[...]) * (1.0 + mod1[:, D:2 * D]) + mod1[:, 0:D]
    a_ref[...] = _dot(h.astype(bf16), win_ref[...])


def _post_ffn_mla(x1, routed, fin, sh_gu, sh_d, modtab0, modtab1, gain, w_in, n_ctx_tiles):
    B, T, _ = x1.shape
    tm = ROW_TILE
    tok = lambda w: pl.BlockSpec((None, tm, w), lambda b, i: (b, i, 0))
    modspec = lambda: pl.BlockSpec((None, None, 1, 6 * D), lambda b, i: (b, jnp.where(i < n_ctx_tiles, 0, 1), 0, 0))
    full = lambda r, c: pl.BlockSpec((r, c), lambda b, i: (0, 0))
    return pl.pallas_call(
        _post_ffn_mla_kernel,
        grid=(B, T // tm),
        in_specs=[tok(D), tok(D), tok(D), full(D, 2 * SH_FF), full(SH_FF, D), modspec(), modspec(),
                  full(1, D), full(D, MLA_IN_PAD)],
        out_specs=[tok(D), tok(MLA_IN_PAD)],
        out_shape=[jax.ShapeDtypeStruct((B, T, D), f32), jax.ShapeDtypeStruct((B, T, MLA_IN_PAD), f32)],
        compiler_params=_cparams(("arbitrary", "arbitrary")),
        name="post_ffn_mla",
    )(x1, routed, fin, sh_gu, sh_d, modtab0, modtab1, gain, w_in)


def _post_ffn_final_kernel(x1_ref, routed_ref, fin_ref, shgu_ref, shd_ref, mod_ref, out_ref):
    out_ref[...] = x1_ref[...] + mod_ref[...][:, 5 * D:6 * D] * (
        routed_ref[...] + _shared_ffn(fin_ref[...], shgu_ref, shd_ref))


def _post_ffn_final(x1, routed, fin, sh_gu, sh_d, modtab):
    B, N, _ = x1.shape
    tm = ROW_TILE
    tok = lambda w: pl.BlockSpec((None, tm, w), lambda b, i: (b, i, 0))
    full = lambda r, c: pl.BlockSpec((r, c), lambda b, i: (0, 0))
    return pl.pallas_call(
        _post_ffn_final_kernel,
        grid=(B, N // tm),
        in_specs=[tok(D), tok(D), tok(D), full(D, 2 * SH_FF), full(SH_FF, D),
                  pl.BlockSpec((None, None, 1, 6 * D), lambda b, i: (b, 1, 0, 0))],
        out_specs=tok(D),
        out_shape=jax.ShapeDtypeStruct((B, N, D), f32),
        compiler_params=_cparams(("arbitrary", "arbitrary")),
        name="post_ffn_final",
    )(x1, routed, fin, sh_gu, sh_d, modtab)


def _mla_prep_kernel(a_ref, qan_ref, wqn_ref, wqr_ref, kvan_ref, wk_ref, wv_ref, qnn_ref, qnr_ref, knn_ref, knr_ref,
                     cos_ref, sin_ref, q_ref, k_ref, v_ref):
    a = a_ref[...]
    tm = a.shape[0]
    scale = MLA_QK ** -0.5 * float(np.log2(np.e))
    cos = cos_ref[...]
    sin = sin_ref[...]
    lane = lax.broadcasted_iota(i32, (tm, LANES), 1)
    first = (lane // (MLA_ROPE // 4)) % 2 == 0

    def rope(xb):
        sw = jnp.where(first, pltpu.roll(xb, LANES - MLA_ROPE // 4, axis=1), pltpu.roll(xb, MLA_ROPE // 4, axis=1))
        return xb * cos + sw * sin

    qa = (_rms(a[:, :MLA_QR]) * qan_ref[...]).astype(bf16)
    qn = _dot(qa, wqn_ref[...])
    qr = _dot(qa, wqr_ref[...])
    ri = lax.broadcasted_iota(i32, (MLA_H * MLA_ROPE, MLA_H * MLA_ROPE), 0) // MLA_ROPE
    ci = lax.broadcasted_iota(i32, (MLA_H * MLA_ROPE, MLA_H * MLA_ROPE), 1) // MLA_ROPE
    seg = jnp.where(ri == ci, 1.0, 0.0)
    ssq = jnp.dot(qr * qr, seg, preferred_element_type=f32, precision=lax.Precision.HIGHEST)
    qr = qr * lax.rsqrt(ssq * (1.0 / MLA_ROPE) + EPS) * qnr_ref[...]
    qr_blocks = [rope(qr[:, p * LANES:(p + 1) * LANES]) * scale for p in range(MLA_H // 2)]

    kv = (_rms(a[:, MLA_QR:MLA_QR + MLA_KVR]) * kvan_ref[...]).astype(bf16)
    kn = _dot(kv, wk_ref[...])
    v_ref[...] = _dot_nt(wv_ref[...], kv).astype(bf16)
    kr = a[:, MLA_QR + MLA_KVR:MLA_IN_PAD]
    kr = rope(_rms(kr, MLA_ROPE) * knr_ref[...])
    kr_odd = pltpu.roll(kr, MLA_ROPE, axis=1)
    for hd in range(MLA_H):
        sl = slice(hd * MLA_NOPE, (hd + 1) * MLA_NOPE)
        q_ref[:, 2 * hd * LANES:(2 * hd + 1) * LANES] = (_rms(qn[:, sl]) * qnn_ref[...] * scale).astype(bf16)
        q_ref[:, (2 * hd + 1) * LANES:(2 * hd + 2) * LANES] = qr_blocks[hd // 2].astype(bf16)
        k_ref[:, 2 * hd * LANES:(2 * hd + 1) * LANES] = (_rms(kn[:, sl]) * knn_ref[...]).astype(bf16)
        k_ref[:, (2 * hd + 1) * LANES:(2 * hd + 2) * LANES] = (kr if hd % 2 == 0 else kr_odd).astype(bf16)


def _mla_prep(a, qan, wqn, wqr, kvan, wk, wv, qnn, qnr, knn, knr, cos_t, sin_t, n_ctx_tiles):
    B, T, _ = a.shape
    tm = ROW_TILE
    tok = lambda w: pl.BlockSpec((None, tm, w), lambda b, i: (b, i, 0))
    full = lambda r, c: pl.BlockSpec((r, c), lambda b, i: (0, 0))
    hw = 2 * LANES * MLA_H
    q_spec = pl.BlockSpec((None, tm, hw), lambda b, i: (b, jnp.maximum(i - n_ctx_tiles, 0), 0))
    return pl.pallas_call(
        _mla_prep_kernel,
        grid=(B, T // tm),
        in_specs=[tok(MLA_IN_PAD), full(1, MLA_QR), full(MLA_QR, MLA_H * MLA_NOPE), full(MLA_QR, MLA_H * MLA_ROPE),
                  full(1, MLA_KVR), full(MLA_KVR, MLA_H * MLA_NOPE), full(MLA_H * MLA_V, MLA_KVR),
                  full(1, MLA_NOPE), full(1, MLA_H * MLA_ROPE), full(1, MLA_NOPE), full(1, LANES),
                  pl.BlockSpec((tm, LANES), lambda b, i: (i, 0)), pl.BlockSpec((tm, LANES), lambda b, i: (i, 0))],
        out_specs=[q_spec, tok(hw), pl.BlockSpec((None, MLA_H * MLA_V, tm), lambda b, i: (b, 0, i))],
        out_shape=[jax.ShapeDtypeStruct((B, T - n_ctx_tiles * tm, hw), bf16), jax.ShapeDtypeStruct((B, T, hw), bf16),
                   jax.ShapeDtypeStruct((B, MLA_H * MLA_V, T), bf16)],
        compiler_params=_cparams(("arbitrary", "arbitrary")),
        name="mla_prep",
    )(a, qan, wqn, wqr, kvan, wk, wv, qnn, qnr, knn, knr, cos_t, sin_t)


ATT_KV_CHUNK = 256
ATT_TQ = 256


def _mla_attn_kernel(q_ref, k_ref, vt_ref, o_ref, s_ref, m_ref):
    tq = q_ref.shape[0]
    ck = ATT_KV_CHUNK
    nck = k_ref.shape[0] // ck
    i = pl.program_id(0)

    @pl.when(i == 0)
    def _():
        s_ref[...] = jnp.zeros_like(s_ref)
        m_ref[...] = jnp.zeros_like(m_ref)

    def fold(a, op):
        return op(a.reshape(ck // SUBLANES, SUBLANES, tq), axis=0)

    def stage(par):
        q = q_ref[...]
        mpart = jnp.full((SUBLANES, tq), -jnp.inf, f32)
        for j in range(nck):
            s = _dot_nt(k_ref[j * ck:(j + 1) * ck, :], q)
            s_ref[par, j * ck:(j + 1) * ck, :] = s
            mpart = jnp.maximum(mpart, fold(s, jnp.max))
        m_ref[par] = mpart
        m = jnp.max(m_ref[1 - par], axis=0, keepdims=True)
        lpart = jnp.zeros((SUBLANES, tq), f32)
        acc = jnp.zeros((MLA_V, tq), f32)
        for j in range(nck):
            p = jnp.exp2(s_ref[1 - par, j * ck:(j + 1) * ck, :] - m)
            lpart = lpart + fold(p, jnp.sum)
            acc = acc + _dot(vt_ref[:, j * ck:(j + 1) * ck], p.astype(bf16))
        o_ref[...] = (acc / jnp.sum(lpart, axis=0, keepdims=True)).astype(bf16)

    pl.when(i % 2 == 0)(functools.partial(stage, 0))
    pl.when(i % 2 == 1)(functools.partial(stage, 1))


def _mla_attn(q, k, vt):
    B, S, _ = q.shape
    T = k.shape[1]
    tq = ATT_TQ if S % ATT_TQ == 0 else ROW_TILE
    nq = S // tq
    ntile = B * MLA_H * nq
    assert T % ATT_KV_CHUNK == 0

    def tile(g):
        g = jnp.clip(g, 0, ntile - 1)
        return g // (MLA_H * nq), (g // nq) % MLA_H, g % nq

    def q_index(g):
        b, h, i = tile(g)
        return b, i, h

    def k_index(g):
        b, h, _ = tile(g)
        return b, 0, h

    def vt_index(g):
        b, h, _ = tile(g - 1)
        return b, h, 0

    def o_index(g):
        b, h, i = tile(g - 1)
        return b, h, i

    return pl.pallas_call(
        _mla_attn_kernel,
        grid=(ntile + 1,),
        in_specs=[pl.BlockSpec((None, tq, 2 * LANES), q_index),
                  pl.BlockSpec((None, T, 2 * LANES), k_index),
                  pl.BlockSpec((None, MLA_V, T), vt_index)],
        out_specs=pl.BlockSpec((None, MLA_V, tq), o_index),
        out_shape=jax.ShapeDtypeStruct((B, MLA_H * MLA_V, S), bf16),
        scratch_shapes=[pltpu.VMEM((2, T, tq), f32), pltpu.VMEM((2, SUBLANES, tq), f32)],
        compiler_params=_cparams(("arbitrary",)),
        name="mla_attn",
    )(q, k, vt)


def _axial_angles(rows_n, rot_dim):
    axis_dim = rot_dim // 2
    inv = ROPE_BASE ** (-jnp.arange(0, axis_dim, 2, dtype=f32) / axis_dim)
    row = jnp.repeat(jnp.arange(rows_n, dtype=f32), GRID_W)
    col = jnp.tile(jnp.arange(GRID_W, dtype=f32), rows_n)
    return row[:, None] * inv, col[:, None] * inv


def _rope_tables(seq, n_ctx, rot_dim, reps):
    ang_r, ang_c = _axial_angles(seq // GRID_W, rot_dim)
    cos = jnp.concatenate([jnp.cos(ang_r)] * 2 + [jnp.cos(ang_c)] * 2, axis=1)
    sin = jnp.concatenate([-jnp.sin(ang_r), jnp.sin(ang_r), -jnp.sin(ang_c), jnp.sin(ang_c)], axis=1)
    cos = jnp.concatenate([jnp.ones((n_ctx, rot_dim), f32), cos], axis=0)
    sin = jnp.concatenate([jnp.zeros((n_ctx, rot_dim), f32), sin], axis=0)
    return jnp.tile(cos, (1, reps)), jnp.tile(sin, (1, reps))


def kernel(x, c, ctx, c_ctx, ada_w, ada_b, norm_mix, norm_ffn, ret_w_in, ret_decay_f, ret_decay_b, ret_w_o,
           mla_w_in, mla_q_a_norm, mla_w_q_b, mla_kv_a_norm, mla_w_kv_b, mla_q_norm, mla_k_norm, mla_w_o,
           router_w, router_bias, exp_w_gu, exp_w_down, sh_w_gu, sh_w_down):
    B, S, _ = x.shape
    n_ctx = ctx.shape[1]
    assert n_ctx % ROW_TILE == 0 and S % ROW_TILE == 0 and S % GRID_W == 0
    n_ctx_tiles = n_ctx // ROW_TILE

    rows = -(-(B + 1) // SUBLANES) * SUBLANES
    cc = jnp.zeros((rows, D), f32).at[:B].set(c).at[B].set(c_ctx)
    mod = _ada(cc, ada_w, ada_b)

    def modtab(i):
        ctx_row = jnp.broadcast_to(mod[i, B][None, :], (B, 6 * D))
        return jnp.stack([ctx_row, mod[i, :B]], axis=1)[:, :, None, :]

    mod0, mod1 = modtab(0), modtab(1)
    xc = jnp.concatenate([ctx, x], axis=1)

    cos_r, sin_r = _rope_tables(S, n_ctx, RET_DK, 1)
    q, k, v, gf, gb = _ret_inproj(xc, mod0, norm_mix[0][None, :], ret_w_in[0].astype(bf16), cos_r, sin_r, n_ctx_tiles)
    dtab = jnp.broadcast_to(jnp.concatenate([ret_decay_f[0], ret_decay_b[0]])[:, None], (2 * RET_H, LANES))
    o = _ret_scan(dtab, q, k, v, gf, gb, n_ctx)
    x1, fin, hlin, e_t, w_t, r_t, cnt = _post_mix(
        o, ret_w_o[0].astype(bf16), xc, 0, mod0, n_ctx_tiles, norm_ffn[0][None, :],
        router_w[0].T, router_bias[0][:, None])
    routed = _routed_experts(e_t, w_t, r_t, cnt, hlin, exp_w_gu[0].astype(bf16), exp_w_down[0].astype(bf16))
    w_in1 = jnp.zeros((D, MLA_IN_PAD), f32).at[:, :mla_w_in.shape[2]].set(mla_w_in[0]).astype(bf16)
    x2, a = _post_ffn_mla(x1, routed, fin, sh_w_gu[0].astype(bf16), sh_w_down[0].astype(bf16), mod0, mod1,
                          norm_mix[1][None, :], w_in1, n_ctx_tiles)

    wq = mla_w_q_b[0].reshape(MLA_QR, MLA_H, MLA_QK)
    wqn = wq[:, :, :MLA_NOPE].reshape(MLA_QR, MLA_H * MLA_NOPE).astype(bf16)
    wqr = wq[:, :, MLA_NOPE:].reshape(MLA_QR, MLA_H * MLA_ROPE).astype(bf16)
    wkv = mla_w_kv_b[0].reshape(MLA_KVR, MLA_H, MLA_NOPE + MLA_V)
    wk = wkv[:, :, :MLA_NOPE].reshape(MLA_KVR, MLA_H * MLA_NOPE).astype(bf16)
    wv = wkv[:, :, MLA_NOPE:].reshape(MLA_KVR, MLA_H * MLA_V).T.astype(bf16)
    qnn = mla_q_norm[0][None, :MLA_NOPE]
    qnr = jnp.tile(mla_q_norm[0][None, MLA_NOPE:], (1, MLA_H))
    knn = mla_k_norm[0][None, :MLA_NOPE]
    knr = jnp.concatenate([mla_k_norm[0][MLA_NOPE:], jnp.zeros((LANES - MLA_ROPE,), f32)])[None, :]
    cos_m, sin_m = _rope_tables(S, n_ctx, MLA_ROPE, LANES // MLA_ROPE)
    qf, kf, vf = _mla_prep(a, mla_q_a_norm[0][None, :], wqn, wqr, mla_kv_a_norm[0][None, :], wk, wv,
                           qnn, qnr, knn, knr, cos_m, sin_m, n_ctx_tiles)
    o1 = _mla_attn(qf, kf, vf)
    x3, fin1, hlin1, e1, w1, r1, cnt1 = _post_mix(
        o1, mla_w_o[0].astype(bf16), x2, n_ctx_tiles, mod1, 0, norm_ffn[1][None, :],
        router_w[1].T, router_bias[1][:, None], o_transposed=True)
    routed1 = _routed_experts(e1, w1, r1, cnt1, hlin1, exp_w_gu[1].astype(bf16), exp_w_down[1].astype(bf16))
    return _post_ffn_final(x3, routed1, fin1, sh_w_gu[1].astype(bf16), sh_w_down[1].astype(bf16), mod1)
```

```python
import dataclasses
import functools

import jax
import jax.numpy as jnp
import numpy as np
from jax import lax
from jax.experimental import pallas as pl
from jax.experimental.pallas import tpu as pltpu
from jax.experimental.pallas import tpu_sc as plsc

f32 = jnp.float32
bf16 = jnp.bfloat16
i32 = jnp.int32

D = 1024
GRID_W = 64
EPS = 1e-6
ROPE_BASE = 10000.0
RET_H = 4
RET_DK = 256
RET_DV = 512
RET_VW = RET_H * RET_DV
RET_CHUNK = 256
MLA_H = 8
MLA_NOPE = 128
MLA_ROPE = 64
MLA_QK = MLA_NOPE + MLA_ROPE
MLA_V = 128
MLA_QR = 384
MLA_KVR = 256
MLA_IN_PAD = 768
N_EXP = 64
TOP_K = 8
N_GRP = 8
TOPK_GRP = 4
EXP_FF = 256
SH_FF = 256
ROUTED_SCALE = 2.5

LANES = 128
SUBLANES = 8
SC_LANES = 16
ROW_TILE = 256
MOE_TILE = 256
MOE_GROUP = 2
VMEM_LIMIT = 56 * 1024 * 1024
MOE_VMEM_LIMIT = 62 * 1024 * 1024


def _cparams(sem, vmem=VMEM_LIMIT):
    return pltpu.CompilerParams(dimension_semantics=sem, vmem_limit_bytes=vmem)


def _sigmoid(x):
    return 1.0 / (1.0 + jnp.exp(-x))


def _silu(x):
    return x * _sigmoid(x)


def _rms(x, n=None):
    n = x.shape[-1] if n is None else n
    return x * lax.rsqrt(jnp.sum(x * x, axis=-1, keepdims=True) * (1.0 / n) + EPS)


def _dot(a, b):
    return jnp.dot(a, b, preferred_element_type=f32)


def _dot_nt(a, b, precision=None):
    return lax.dot_general(a, b, (((1,), (1,)), ((), ())), preferred_element_type=f32, precision=precision)


def _ada_kernel(c_ref, w_ref, b_ref, o_ref):
    s = _silu(c_ref[...]).astype(bf16)
    o_ref[...] = _dot(s, w_ref[...].astype(bf16)) + b_ref[...]


def _ada(cc, ada_w, ada_b):
    depth = ada_w.shape[0]
    rows = cc.shape[0]
    tn = 1536
    return pl.pallas_call(
        _ada_kernel,
        grid=(depth, 6 * D // tn),
        in_specs=[pl.BlockSpec((rows, D), lambda i, j: (0, 0)),
                  pl.BlockSpec((None, D, tn), lambda i, j: (i, 0, j)),
                  pl.BlockSpec((None, 1, tn), lambda i, j: (i, 0, j))],
        out_specs=pl.BlockSpec((None, rows, tn), lambda i, j: (i, 0, j)),
        out_shape=jax.ShapeDtypeStruct((depth, rows, 6 * D), f32),
        compiler_params=_cparams(("arbitrary", "arbitrary")),
        name="ada",
    )(cc, ada_w, ada_b.reshape(depth, 1, 6 * D))


def _stream_tile(ctx_ref, x_ref, n_ctx_tiles):
    return jnp.where(pl.program_id(1) < n_ctx_tiles, ctx_ref[...], x_ref[...])


def _stream_specs(tm, n_ctx_tiles, last=None):
    clamp = (lambda i: i) if last is None else (lambda i: jnp.minimum(i, last))
    return [pl.BlockSpec((None, tm, D), lambda b, i: (b, jnp.minimum(clamp(i), n_ctx_tiles - 1), 0)),
            pl.BlockSpec((None, tm, D), lambda b, i: (b, jnp.maximum(clamp(i) - n_ctx_tiles, 0), 0))]


def _ret_inproj_kernel(ctx_ref, x_ref, mod_ref, g_ref, w_ref, cos_ref, sin_ref, q_ref, k_ref, v_ref, gf_ref, gb_ref,
                       *, n_ctx_tiles):
    x = _stream_tile(ctx_ref, x_ref, n_ctx_tiles)
    mod = mod_ref[...]
    h = (_rms(x) * g_ref[...]) * (1.0 + mod[:, D:2 * D]) + mod[:, 0:D]
    hb = h.astype(bf16)
    cos = cos_ref[...]
    sin = sin_ref[...]

    def rope(a):
        outs = []
        for half in range(2):
            sl = slice(half * LANES, (half + 1) * LANES)
            ah = a[:, sl]
            outs.append(ah * cos[:, sl] + pltpu.roll(ah, LANES // 2, axis=1) * sin[:, sl])
        return jnp.concatenate(outs, axis=1)

    for hd in range(RET_H):
        sl = slice(hd * RET_DK, (hd + 1) * RET_DK)
        q_ref[:, sl] = rope(_dot(hb, w_ref[:, sl])).astype(bf16)
    for hd in range(RET_H):
        sl = slice(hd * RET_DK, (hd + 1) * RET_DK)
        wsl = slice(D + hd * RET_DK, D + (hd + 1) * RET_DK)
        k_ref[:, sl] = (rope(_dot(hb, w_ref[:, wsl])) * (RET_DK ** -0.5)).astype(bf16)
    cw = 512
    for c in range(RET_VW // cw):
        sl = slice(c * cw, (c + 1) * cw)
        v_ref[:, sl] = _dot(hb, w_ref[:, 2 * D + c * cw:2 * D + (c + 1) * cw]).astype(bf16)
        gf_ref[:, sl] = _silu(_dot(hb, w_ref[:, 2 * D + RET_VW + c * cw:2 * D + RET_VW + (c + 1) * cw])).astype(bf16)
        gb_ref[:, sl] = _silu(_dot(hb, w_ref[:, 2 * D + 2 * RET_VW + c * cw:2 * D + 2 * RET_VW + (c + 1) * cw])).astype(bf16)


def _ret_inproj(ctx, x, modtab, gain, w_in, cos_t, sin_t, n_ctx_tiles):
    B = x.shape[0]
    T = ctx.shape[1] + x.shape[1]
    tm = ROW_TILE
    n_in = w_in.shape[1]
    tok = lambda w: pl.BlockSpec((None, tm, w), lambda b, i: (b, i, 0))
    return pl.pallas_call(
        functools.partial(_ret_inproj_kernel, n_ctx_tiles=n_ctx_tiles),
        grid=(B, T // tm),
        in_specs=_stream_specs(tm, n_ctx_tiles) + [
                  pl.BlockSpec((None, None, 1, 6 * D), lambda b, i: (b, jnp.where(i < n_ctx_tiles, 0, 1), 0, 0)),
                  pl.BlockSpec((1, D), lambda b, i: (0, 0)),
                  pl.BlockSpec((D, n_in), lambda b, i: (0, 0), pipeline_mode=pl.Buffered(1)),
                  pl.BlockSpec((tm, RET_DK), lambda b, i: (i, 0)),
                  pl.BlockSpec((tm, RET_DK), lambda b, i: (i, 0))],
        out_specs=[tok(D), tok(D), tok(RET_VW), tok(RET_VW), tok(RET_VW)],
        out_shape=[jax.ShapeDtypeStruct((B, T, D), bf16), jax.ShapeDtypeStruct((B, T, D), bf16),
                   jax.ShapeDtypeStruct((B, T, RET_VW), bf16), jax.ShapeDtypeStruct((B, T, RET_VW), bf16),
                   jax.ShapeDtypeStruct((B, T, RET_VW), bf16)],
        compiler_params=_cparams(("arbitrary", "arbitrary")),
        name="ret_inproj",
    )(ctx, x, modtab, gain, w_in, cos_t, sin_t)


def _ret_chunk_index(t, nc, ncc):
    u = t - nc
    back = jnp.where(u < ncc, ncc - 1 - u, nc - 1 - u + ncc)
    return jnp.where(t < nc, t, back)


def _ret_scan_kernel(dt_ref, q_ref, k_ref, v_ref, gf_ref, gb_ref, o_ref,
                     s_ref, of_ref, mask_ref, dq_ref, dk_ref, dc_ref, *, nc, ncc):
    t = pl.program_id(1)
    C = RET_CHUNK

    def init(direction):
        s_ref[...] = jnp.zeros_like(s_ref)
        ii = lax.broadcasted_iota(i32, (C, C), 0)
        jj = lax.broadcasted_iota(i32, (C, C), 1)
        rel = (ii - jj if direction == 0 else jj - ii).astype(f32)
        pos = lax.broadcasted_iota(i32, (C, 1), 0).astype(f32)
        for hd in range(RET_H):
            r = direction * RET_H + hd
            lg = -jnp.exp(dt_ref[r:r + 1, :])
            lg1 = lg[:, 0:1]
            mask_ref[hd] = jnp.where(rel >= 0, jnp.exp(lg1 * jnp.maximum(rel, 0.0)), 0.0)
            if direction == 0:
                dq_ref[hd] = jnp.exp(lg1 * (pos + 1.0))
                dk_ref[hd] = jnp.exp(lg1 * (C - 1.0 - pos))
            else:
                dq_ref[hd] = jnp.exp(lg1 * (C - pos))
                dk_ref[hd] = jnp.exp(lg1 * pos)
            dc_ref[hd] = jnp.exp(lg * float(C))

    pl.when(t == 0)(functools.partial(init, 0))
    pl.when(t == nc)(functools.partial(init, 1))

    fwd = t < nc
    row0 = pl.multiple_of(_ret_chunk_index(t, nc, ncc) * C, C)

    for hd in range(RET_H):
        ks = slice(hd * RET_DK, (hd + 1) * RET_DK)
        vs = slice(hd * RET_DV, (hd + 1) * RET_DV)
        qh = q_ref[:, ks]
        kh = k_ref[:, ks]
        vh = v_ref[:, vs]
        p = (_dot_nt(qh, kh) * mask_ref[hd]).astype(bf16)
        y = _dot(p, vh) + _dot(qh, s_ref[hd].astype(bf16)) * dq_ref[hd]
        kd = (kh.astype(f32) * dk_ref[hd]).astype(bf16)
        upd = lax.dot_general(kd, vh, (((0,), (0,)), ((), ())), preferred_element_type=f32)
        s_ref[hd] = s_ref[hd] * dc_ref[hd][0:1, 0:1] + upd
        yn = _rms(y)

        @pl.when(fwd)
        def _():
            of_ref[pl.ds(row0, C), vs] = (gf_ref[:, vs].astype(f32) * yn).astype(bf16)

        @pl.when(jnp.logical_not(fwd))
        def _():
            o_ref[:, vs] = (of_ref[pl.ds(row0, C), vs].astype(f32) + gb_ref[:, vs].astype(f32) * yn).astype(bf16)


def _ret_scan(dtab, q, k, v, gf, gb, n_ctx):
    B, T, _ = q.shape
    C = RET_CHUNK
    nc = T // C
    ncc = n_ctx // C
    cidx = functools.partial(_ret_chunk_index, nc=nc, ncc=ncc)
    first_back = ncc - 1
    return pl.pallas_call(
        functools.partial(_ret_scan_kernel, nc=nc, ncc=ncc),
        grid=(B, 2 * nc),
        in_specs=[pl.BlockSpec((2 * RET_H, LANES), lambda b, t: (0, 0)),
                  pl.BlockSpec((None, C, D), lambda b, t: (b, cidx(t), 0)),
                  pl.BlockSpec((None, C, D), lambda b, t: (b, cidx(t), 0)),
                  pl.BlockSpec((None, C, RET_VW), lambda b, t: (b, cidx(t), 0)),
                  pl.BlockSpec((None, C, RET_VW), lambda b, t: (b, jnp.where(t < nc, t, nc - 1), 0)),
                  pl.BlockSpec((None, C, RET_VW), lambda b, t: (b, jnp.where(t < nc, first_back, cidx(t)), 0))],
        out_specs=pl.BlockSpec((None, C, RET_VW), lambda b, t: (b, jnp.where(t < nc, first_back, cidx(t)), 0)),
        out_shape=jax.ShapeDtypeStruct((B, T, RET_VW), bf16),
        scratch_shapes=[pltpu.VMEM((RET_H, RET_DK, RET_DV), f32),
                        pltpu.VMEM((T, RET_VW), bf16),
                        pltpu.VMEM((RET_H, C, C), f32),
                        pltpu.VMEM((RET_H, C, 1), f32),
                        pltpu.VMEM((RET_H, C, 1), f32),
                        pltpu.VMEM((RET_H, 1, LANES), f32)],
        compiler_params=_cparams(("arbitrary", "arbitrary")),
        name="ret_scan",
    )(dtab, q, k, v, gf, gb)


def _route(f, rwt_ref, rb_ref, cnt_ref, e_ref, w_ref, r_ref):
    tm = f.shape[0]
    G = N_EXP // N_GRP
    logits = _dot_nt(rwt_ref[...].astype(bf16), f.astype(bf16))
    s = _sigmoid(logits)
    sel = s + rb_ref[...]
    mi = lax.broadcasted_iota(i32, (G, tm), 0)
    neg = -jnp.inf
    s_g = [s[g * G:(g + 1) * G, :] for g in range(N_GRP)]
    sel_g = [sel[g * G:(g + 1) * G, :] for g in range(N_GRP)]

    def first_max(a, ids, big):
        mx = jnp.max(a, axis=0, keepdims=True)
        ix = jnp.min(jnp.where(a == mx, ids, big), axis=0, keepdims=True)
        return mx, ix

    gscore = jnp.zeros((N_GRP, tm), f32)
    gi = lax.broadcasted_iota(i32, (N_GRP, tm), 0)
    for g in range(N_GRP):
        t1, i1 = first_max(sel_g[g], mi, G)
        t2 = jnp.max(jnp.where(mi == i1, neg, sel_g[g]), axis=0, keepdims=True)
        gscore = jnp.where(gi == g, t1 + t2, gscore)
    gmask = jnp.zeros((N_GRP, tm), i32)
    cur = gscore
    for _ in range(TOPK_GRP):
        _, ix = first_max(cur, gi, N_GRP)
        hit = gi == ix
        gmask = jnp.where(hit, 1, gmask)
        cur = jnp.where(hit, neg, cur)
    cand = [jnp.where(gmask[g:g + 1, :] > 0, sel_g[g], neg) for g in range(N_GRP)]
    ids = [mi + g * G for g in range(N_GRP)]

    e_rows, w_rows = [], []
    for _ in range(TOP_K):
        mx = cand[0].max(axis=0, keepdims=True)
        for g in range(1, N_GRP):
            mx = jnp.maximum(mx, cand[g].max(axis=0, keepdims=True))
        ix = jnp.min(jnp.where(cand[0] == mx, ids[0], N_EXP), axis=0, keepdims=True)
        for g in range(1, N_GRP):
            ix = jnp.minimum(ix, jnp.min(jnp.where(cand[g] == mx, ids[g], N_EXP), axis=0, keepdims=True))
        wv = jnp.zeros((1, tm), f32)
        for g in range(N_GRP):
            hit = ids[g] == ix
            cand[g] = jnp.where(hit, neg, cand[g])
            wv = wv + jnp.sum(jnp.where(hit, s_g[g], 0.0), axis=0, keepdims=True)
        e_rows.append(ix)
        w_rows.append(wv)
    wsum = w_rows[0]
    for r in range(1, TOP_K):
        wsum = wsum + w_rows[r]

    selm = [jnp.zeros((G, tm), f32) for _ in range(N_GRP)]
    for r in range(TOP_K):
        for g in range(N_GRP):
            selm[g] = jnp.where(ids[g] == e_rows[r], 1.0, selm[g])
    m_all = jnp.concatenate(selm, axis=0)
    ri = lax.broadcasted_iota(i32, (tm, tm), 0)
    ci = lax.broadcasted_iota(i32, (tm, tm), 1)
    upper = jnp.where(ri <= ci, 1.0, 0.0).astype(bf16)
    incl = _dot(m_all.astype(bf16), upper)
    carry = cnt_ref[:, 0:1]
    rank_all = carry + incl - m_all
    cnt_ref[...] = cnt_ref[...] + incl[:, tm - 1:tm]
    for r in range(TOP_K):
        rk = jnp.zeros((1, tm), f32)
        for g in range(N_GRP):
            rk = rk + jnp.sum(jnp.where(ids[g] == e_rows[r], rank_all[g * G:(g + 1) * G, :], 0.0), axis=0, keepdims=True)
        e_ref[r:r + 1, :] = e_rows[r]
        w_ref[r:r + 1, :] = w_rows[r] / wsum * ROUTED_SCALE
        r_ref[r:r + 1, :] = rk.astype(i32)


def _post_mix_kernel(o_ref, wo_ref, *refs, nt, o_transposed, split_ctx_tiles):
    n_resid = 2 if split_ctx_tiles else 1
    resid = refs[:n_resid]
    mod_ref, g_ref, rwt_ref, rb_ref, x1_ref, fin_ref, hlin_ref, e_ref, w_ref, r_ref, cnt_ref = refs[n_resid:]
    i = pl.program_id(1)

    @pl.when(i == 0)
    def _():
        cnt_ref[...] = jnp.zeros_like(cnt_ref)

    @pl.when(i < nt)
    def _():
        mod = mod_ref[...]
        if o_transposed:
            mixed = lax.dot_general(o_ref[...], wo_ref[...], (((0,), (0,)), ((), ())), preferred_element_type=f32)
        else:
            mixed = _dot(o_ref[...], wo_ref[...])
        x = _stream_tile(resid[0], resid[1], split_ctx_tiles) if split_ctx_tiles else resid[0][...]
        x1 = x + mod[:, 2 * D:3 * D] * mixed
        x1_ref[...] = x1
        f = (_rms(x1) * g_ref[...]) * (1.0 + mod[:, 4 * D:5 * D]) + mod[:, 3 * D:4 * D]
        fin_ref[...] = f.astype(bf16)
        tm = f.shape[0]
        for j in range(D // LANES):
            hlin_ref[pl.ds(j, tm, stride=D // LANES), :] = f[:, j * LANES:(j + 1) * LANES]
        _route(f, rwt_ref, rb_ref, cnt_ref, e_ref, w_ref, r_ref)

    @pl.when(i == nt)
    def _():
        hlin_ref[...] = jnp.zeros_like(hlin_ref)


def _post_mix(o, w_o, resid, x_tile_off, modtab, n_ctx_tiles, gain, rwt, rb, o_transposed=False):
    if o_transposed:
        B, KO, N = o.shape
    else:
        B, N, KO = o.shape
    tm = ROW_TILE
    nt = N // tm
    last = lambda i: jnp.minimum(i, nt - 1)
    tok = lambda w: pl.BlockSpec((None, tm, w), lambda b, i: (b, last(i), 0))
    sel = lambda: pl.BlockSpec((None, TOP_K, tm), lambda b, i: (b, 0, last(i)))
    o_spec = (pl.BlockSpec((None, KO, tm), lambda b, i: (b, 0, last(i))) if o_transposed else tok(KO))
    split = len(resid) == 2
    resid_specs = (_stream_specs(tm, n_ctx_tiles, nt - 1) if split else
                   [pl.BlockSpec((None, tm, D), lambda b, i: (b, last(i) + x_tile_off, 0))])
    return pl.pallas_call(
        functools.partial(_post_mix_kernel, nt=nt, o_transposed=o_transposed,
                          split_ctx_tiles=n_ctx_tiles if split else 0),
        grid=(B, nt + 1),
        in_specs=[o_spec,
                  pl.BlockSpec((KO, D), lambda b, i: (0, 0))] + resid_specs + [
                  pl.BlockSpec((None, None, 1, 6 * D), lambda b, i: (b, jnp.where(last(i) < n_ctx_tiles, 0, 1), 0, 0)),
                  pl.BlockSpec((1, D), lambda b, i: (0, 0)),
                  pl.BlockSpec((N_EXP, D), lambda b, i: (0, 0)),
                  pl.BlockSpec((N_EXP, 1), lambda b, i: (0, 0))],
        out_specs=[tok(D), tok(D),
                   pl.BlockSpec((None, tm * (D // LANES), LANES), lambda b, i: (b, i, 0)),
                   sel(), sel(), sel(),
                   pl.BlockSpec((None, N_EXP, LANES), lambda b, i: (b, 0, 0))],
        out_shape=[jax.ShapeDtypeStruct((B, N, D), f32), jax.ShapeDtypeStruct((B, N, D), bf16),
                   jax.ShapeDtypeStruct((B, (N + tm) * (D // LANES), LANES), f32),
                   jax.ShapeDtypeStruct((B, TOP_K, N), i32), jax.ShapeDtypeStruct((B, TOP_K, N), f32),
                   jax.ShapeDtypeStruct((B, TOP_K, N), i32),
                   jax.ShapeDtypeStruct((B, N_EXP, LANES), f32)],
        compiler_params=_cparams(("arbitrary", "arbitrary")),
        name="post_mix",
    )(o, w_o, *resid, modtab, gain, rwt, rb)


META_W = 256
PLAN_ALIGN = 1024


def _round_up(n, m):
    return -(-n // m) * m


def _moe_sizes(n_tok):
    tm = MOE_TILE
    nt_max = (n_tok * TOP_K + N_EXP * (tm - 1)) // tm + 1
    ntp = _round_up(nt_max + 1, PLAN_ALIGN // tm)
    assert ntp <= META_W
    return ntp, _round_up(n_tok, PLAN_ALIGN)


def _plan_kernel(e_ref, r_ref, cnt_ref, pos_ref, meta_ref, *, n_tok):
    tm = MOE_TILE
    ntile = jnp.floor((cnt_ref[...] + (tm - 1.0)) * (1.0 / tm))
    ntb = ntile.astype(bf16)
    ei = lax.broadcasted_iota(i32, (N_EXP, LANES), 0)
    ej = lax.broadcasted_iota(i32, (N_EXP, LANES), 1)
    lower = jnp.where(ej <= ei, 1.0, 0.0)[:, :N_EXP].astype(bf16)
    tend = _dot(lower, ntb)
    tstart = tend - ntile
    tt = lax.broadcasted_iota(i32, (N_EXP, META_W), 1).astype(f32)
    te = jnp.sum(jnp.where(tt >= tend[:, 0:1], 1.0, 0.0), axis=0, keepdims=True)
    meta_ref[...] = jnp.zeros_like(meta_ref)
    meta_ref[0:1, :] = jnp.minimum(te, N_EXP - 1.0).astype(i32)
    meta_ref[1:2, :] = jnp.broadcast_to(tend[N_EXP - 1:N_EXP, 0:1], (1, META_W)).astype(i32)
    e = e_ref[...]
    base = jnp.zeros(e.shape, f32)
    for ex in range(N_EXP):
        base = jnp.where(e == ex, tstart[ex:ex + 1, 0:1] * float(tm), base)
    pos_ref[...] = jnp.zeros_like(pos_ref)
    pos_ref[:, 0:n_tok] = base.astype(i32) + r_ref[...]


def _plan(e_t, r_t, cnt):
    B, K, N = e_t.shape
    _, npad = _moe_sizes(N)
    return pl.pallas_call(
        functools.partial(_plan_kernel, n_tok=N),
        grid=(B,),
        in_specs=[pl.BlockSpec((None, K, N), lambda b: (b, 0, 0)),
                  pl.BlockSpec((None, K, N), lambda b: (b, 0, 0)),
                  pl.BlockSpec((None, N_EXP, LANES), lambda b: (b, 0, 0))],
        out_specs=[pl.BlockSpec((None, K, npad), lambda b: (b, 0, 0)),
                   pl.BlockSpec((None, SUBLANES, META_W), lambda b: (b, 0, 0))],
        out_shape=[jax.ShapeDtypeStruct((B, K, npad), i32), jax.ShapeDtypeStruct((B, SUBLANES, META_W), i32)],
        compiler_params=_cparams(("arbitrary",)),
        name="moe_plan",
    )(e_t, r_t, cnt)


def _plan_invert(pos, w_t, n_tok):
    B, K, npad = pos.shape
    ntp, _ = _moe_sizes(n_tok)
    plen = ntp * MOE_TILE
    nch = D // LANES
    mesh = plsc.VectorSubcoreMesh(core_axis_name="c", subcore_axis_name="s")
    n_cores = mesh.num_cores
    assert 2 * B <= n_cores * mesh.num_subcores and n_tok % SC_LANES == 0 and plen % SC_LANES == 0

    @functools.partial(
        pl.kernel, mesh=mesh,
        out_type=[jax.ShapeDtypeStruct((B * plen,), i32), jax.ShapeDtypeStruct((B * plen,), f32)],
        scratch_types=[pltpu.VMEM((npad,), i32), pltpu.VMEM((n_tok,), f32),
                       pltpu.VMEM((plen,), i32), pltpu.VMEM((plen,), f32)],
        compiler_params=dataclasses.replace(pltpu.CompilerParams(), needs_layout_passes=False))
    def invert(pos_hbm, w_hbm, rows_hbm, ws_hbm, pos_c, w_c, rows_v, ws_v):
        wid = lax.axis_index("s") * n_cores + lax.axis_index("c")
        b = wid % B
        lane = lax.iota(i32, SC_LANES)

        def load_pos(k):
            pltpu.sync_copy(pos_hbm.at[pl.ds(pl.multiple_of((b * K + k) * npad, SUBLANES), npad)], pos_c)

        @pl.when(wid < B)
        def _():
            pad = jnp.full((SC_LANES,), n_tok * nch, i32)

            @pl.loop(0, plen, step=SC_LANES)
            def _(i):
                rows_v[pl.ds(i, SC_LANES)] = pad

            for k in range(K):
                load_pos(k)

                @pl.loop(0, n_tok, step=SC_LANES)
                def _(n):
                    plsc.store_scatter(rows_v, [pos_c[pl.ds(n, SC_LANES)]], (lane + n) * nch)

            pltpu.sync_copy(rows_v, rows_hbm.at[pl.ds(pl.multiple_of(b * plen, SUBLANES), plen)])

        @pl.when(jnp.logical_and(wid >= B, wid < 2 * B))
        def _():
            zero = jnp.zeros((SC_LANES,), f32)

            @pl.loop(0, plen, step=SC_LANES)
            def _(i):
                ws_v[pl.ds(i, SC_LANES)] = zero

            for k in range(K):
                load_pos(k)
                pltpu.sync_copy(w_hbm.at[pl.ds(pl.multiple_of((b * K + k) * n_tok, SUBLANES), n_tok)], w_c)

                @pl.loop(0, n_tok, step=SC_LANES)
                def _(n):
                    plsc.store_scatter(ws_v, [pos_c[pl.ds(n, SC_LANES)]], w_c[pl.ds(n, SC_LANES)])

            pltpu.sync_copy(ws_v, ws_hbm.at[pl.ds(pl.multiple_of(b * plen, SUBLANES), plen)])

    return invert(pos.reshape(-1), w_t.reshape(-1))


def _moe_kernel(te_ref, nt_ref, *refs, ntp, n_tok):
    R = MOE_GROUP
    rg_refs, rs_refs, ws_refs = refs[0:R], refs[R:2 * R], refs[2 * R:3 * R]
    hlin_ref = refs[3 * R]
    wgu_refs, wd_refs = refs[3 * R + 1:4 * R + 1], refs[4 * R + 1:5 * R + 1]
    out_ref, acc_ref = refs[5 * R + 1:5 * R + 3]
    bufs = refs[5 * R + 3:]
    xs_refs = [bufs[0:R], bufs[R:2 * R]]
    ylin_refs = [bufs[2 * R:3 * R], bufs[3 * R:4 * R]]
    b = pl.program_id(0)
    t = pl.program_id(1)
    TM = MOE_TILE
    NCH = D // LANES
    U = 8
    ngrp = ntp // R

    @pl.when(t == 0)
    def _():
        acc_ref[...] = jnp.zeros_like(acc_ref)
        for buf in bufs:
            buf[...] = jnp.zeros_like(buf)

    def gather_rows(p, r, c):
        for m in range(c * U, (c + 1) * U):
            off = pl.multiple_of(rg_refs[r][m], NCH)
            xs_refs[p][r][pl.ds(m * NCH, NCH), :] = hlin_ref[pl.ds(off, NCH), :]

    def scatter_rows(p, r, c):
        offs = [pl.multiple_of(rs_refs[r][c * U + u], NCH) for u in range(U)]
        news = []
        for u in range(U):
            m = c * U + u
            yrow = ylin_refs[p][r][pl.ds((m // SUBLANES) * (NCH * SUBLANES) + m % SUBLANES, NCH,
                                         stride=SUBLANES), :]
            news.append(acc_ref[pl.ds(offs[u], NCH), :] + yrow)
        for u in range(U):
            acc_ref[pl.ds(offs[u], NCH), :] = news[u]

    def stage(p):
        for r in range(R):
            for c in range(TM // U):
                gather_rows(p, r, c)
        for r in range(R):
            x = jnp.concatenate([xs_refs[1 - p][r][pl.ds(j, TM, stride=NCH), :] for j in range(NCH)],
                                axis=1).astype(bf16)
            gu = _dot(x, wgu_refs[r][...])
            wcol = jnp.broadcast_to(ws_refs[r][...], (SUBLANES, TM)).T[:, 0:1]
            a = (_silu(gu[:, :EXP_FF]) * gu[:, EXP_FF:] * wcol).astype(bf16)
            y = _dot(a, wd_refs[r][...])
            for g in range(TM // SUBLANES):
                for j in range(NCH):
                    ylin_refs[1 - p][r][pl.ds((g * NCH + j) * SUBLANES, SUBLANES), :] = (
                        y[g * SUBLANES:(g + 1) * SUBLANES, j * LANES:(j + 1) * LANES])
        for r in range(R):
            for c in range(TM // U):
                scatter_rows(p, r, c)

    live = (t - 2) * R < nt_ref[b]
    pl.when(jnp.logical_and(live, t % 2 == 0))(functools.partial(stage, 0))
    pl.when(jnp.logical_and(live, t % 2 == 1))(functools.partial(stage, 1))

    @pl.when(t >= ngrp + 2)
    def _():
        row0 = (t - (ngrp + 2)) * (ROW_TILE * NCH)
        for j in range(NCH):
            out_ref[:, j * LANES:(j + 1) * LANES] = acc_ref[pl.ds(row0 + j, ROW_TILE, stride=NCH), :]


def _moe(te, nt, rows, wsort, hlin, w_gu, w_d, layer):
    B = hlin.shape[0]
    NCH = D // LANES
    n_tok = hlin.shape[1] // NCH - ROW_TILE
    assert n_tok % ROW_TILE == 0
    nf = n_tok // ROW_TILE
    TM = MOE_TILE
    R = MOE_GROUP
    ntp, _ = _moe_sizes(n_tok)
    assert ntp % R == 0
    ngrp = ntp // R
    pad_tile = ntp - 1

    def tile_of(b, t, nt_ref, r, lag):
        tile = (t - lag) * R + r
        ok = jnp.logical_and(t >= lag, tile < nt_ref[b])
        return b * ntp + jnp.where(ok, tile, pad_tile)

    def rows_spec(r, lag):
        return pl.BlockSpec((TM,), lambda b, t, te_ref, nt_ref: (tile_of(b, t, nt_ref, r, lag),),
                            memory_space=pltpu.SMEM)

    def ws_spec(r):
        return pl.BlockSpec((None, 1, TM), lambda b, t, te_ref, nt_ref: (tile_of(b, t, nt_ref, r, 1), 0, 0))

    def w_spec(shape, r):
        def index(b, t, te_ref, nt_ref):
            tile = jnp.clip((t - 1) * R + r, 0, nt_ref[b] - 1)
            return (layer, te_ref[b * META_W + tile], 0, 0)
        return pl.BlockSpec((None, None) + shape, index)

    grid_spec = pltpu.PrefetchScalarGridSpec(
        num_scalar_prefetch=2,
        grid=(B, ngrp + 2 + nf),
        in_specs=([rows_spec(r, 0) for r in range(R)] + [rows_spec(r, 2) for r in range(R)]
                  + [ws_spec(r) for r in range(R)]
                  + [pl.BlockSpec((None, (n_tok + ROW_TILE) * NCH, LANES), lambda b, t, *_: (b, 0, 0),
                                  pipeline_mode=pl.Buffered(1))]
                  + [w_spec((D, 2 * EXP_FF), r) for r in range(R)]
                  + [w_spec((EXP_FF, D), r) for r in range(R)]),
        out_specs=pl.BlockSpec((None, ROW_TILE, D), lambda b, t, *_: (b, jnp.maximum(t - (ngrp + 2), 0), 0)),
        scratch_shapes=([pltpu.VMEM(((n_tok + SUBLANES) * NCH, LANES), f32)]
                        + [pltpu.VMEM((TM * NCH, LANES), f32) for _ in range(4 * R)]),
    )
    return pl.pallas_call(
        functools.partial(_moe_kernel, ntp=ntp, n_tok=n_tok),
        grid_spec=grid_spec,
        out_shape=jax.ShapeDtypeStruct((B, n_tok, D), f32),
        compiler_params=_cparams(("arbitrary", "arbitrary"), MOE_VMEM_LIMIT),
        name="moe",
    )(te, nt, *([rows] * (2 * R)), *([wsort.reshape(B * ntp, 1, TM)] * R), hlin, *([w_gu] * R), *([w_d] * R))


def _routed_experts(e_t, w_t, r_t, cnt, hlin, w_gu, w_d, layer):
    N = e_t.shape[2]
    pos, meta = _plan(e_t, r_t, cnt)
    te = meta[:, 0, :].reshape(-1)
    nt = meta[:, 1, 0]
    rows, wsort = _plan_invert(pos, w_t, N)
    return _moe(te, nt, rows, wsort, hlin, w_gu, w_d, layer)


def _shared_ffn(fin, shgu_ref, shd_ref):
    gu = _dot(fin, shgu_ref[...])
    return _dot((_silu(gu[:, :SH_FF]) * gu[:, SH_FF:]).astype(bf16), shd_ref[...])


def _post_ffn_mla_kernel(x1_ref, routed_ref, fin_ref, shgu_ref, shd_ref, mod0_ref, mod1_ref, g_ref, win_ref,
                         x2_ref, a_ref):
    x2 = x1_ref[...] + mod0_ref[...][:, 5 * D:6 * D] * (routed_ref[...] + _shared_ffn(fin_ref[...], shgu_ref, shd_ref))
    x2_ref[...] = x2
    mod1 = mod1_ref[...]
    h = (_rms(x2) * g_ref[...]) * (1.0 + mod1[:, D:2 * D]) + mod1[:, 0:D]
    a_ref[...] = _dot(h.astype(bf16), win_ref[...])


def _post_ffn_mla(x1, routed, fin, sh_gu, sh_d, modtab0, modtab1, gain, w_in, n_ctx_tiles):
    B, T, _ = x1.shape
    tm = ROW_TILE
    tok = lambda w: pl.BlockSpec((None, tm, w), lambda b, i: (b, i, 0))
    modspec = lambda: pl.BlockSpec((None, None, 1, 6 * D), lambda b, i: (b, jnp.where(i < n_ctx_tiles, 0, 1), 0, 0))
    full = lambda r, c: pl.BlockSpec((r, c), lambda b, i: (0, 0))
    return pl.pallas_call(
        _post_ffn_mla_kernel,
        grid=(B, T // tm),
        in_specs=[tok(D), tok(D), tok(D), full(D, 2 * SH_FF), full(SH_FF, D), modspec(), modspec(),
                  full(1, D), full(D, MLA_IN_PAD)],
        out_specs=[tok(D), tok(MLA_IN_PAD)],
        out_shape=[jax.ShapeDtypeStruct((B, T, D), f32), jax.ShapeDtypeStruct((B, T, MLA_IN_PAD), f32)],
        compiler_params=_cparams(("arbitrary", "arbitrary")),
        name="post_ffn_mla",
    )(x1, routed, fin, sh_gu, sh_d, modtab0, modtab1, gain, w_in)


def _post_ffn_final_kernel(x1_ref, routed_ref, fin_ref, shgu_ref, shd_ref, mod_ref, out_ref):
    out_ref[...] = x1_ref[...] + mod_ref[...][:, 5 * D:6 * D] * (
        routed_ref[...] + _shared_ffn(fin_ref[...], shgu_ref, shd_ref))


def _post_ffn_final(x1, routed, fin, sh_gu, sh_d, modtab):
    B, N, _ = x1.shape
    tm = ROW_TILE
    tok = lambda w: pl.BlockSpec((None, tm, w), lambda b, i: (b, i, 0))
    full = lambda r, c: pl.BlockSpec((r, c), lambda b, i: (0, 0))
    return pl.pallas_call(
        _post_ffn_final_kernel,
        grid=(B, N // tm),
        in_specs=[tok(D), tok(D), tok(D), full(D, 2 * SH_FF), full(SH_FF, D),
                  pl.BlockSpec((None, None, 1, 6 * D), lambda b, i: (b, 1, 0, 0))],
        out_specs=tok(D),
        out_shape=jax.ShapeDtypeStruct((B, N, D), f32),
        compiler_params=_cparams(("arbitrary", "arbitrary")),
        name="post_ffn_final",
    )(x1, routed, fin, sh_gu, sh_d, modtab)


def _mla_prep_kernel(a_ref, qan_ref, wqn_ref, wqr_ref, kvan_ref, wk_ref, wv_ref, qnn_ref, qnr_ref, knn_ref, knr_ref,
                     cos_ref, sin_ref, q_ref, k_ref, v_ref):
    a = a_ref[...]
    tm = a.shape[0]
    scale = MLA_QK ** -0.5 * float(np.log2(np.e))
    cos = cos_ref[...]
    sin = sin_ref[...]
    lane = lax.broadcasted_iota(i32, (tm, LANES), 1)
    first = (lane // (MLA_ROPE // 4)) % 2 == 0

    def rope(xb):
        sw = jnp.where(first, pltpu.roll(xb, LANES - MLA_ROPE // 4, axis=1), pltpu.roll(xb, MLA_ROPE // 4, axis=1))
        return xb * cos + sw * sin

    qa = (_rms(a[:, :MLA_QR]) * qan_ref[...]).astype(bf16)
    qn = _dot(qa, wqn_ref[...])
    qr = _dot(qa, wqr_ref[...])
    ri = lax.broadcasted_iota(i32, (MLA_H * MLA_ROPE, MLA_H * MLA_ROPE), 0) // MLA_ROPE
    ci = lax.broadcasted_iota(i32, (MLA_H * MLA_ROPE, MLA_H * MLA_ROPE), 1) // MLA_ROPE
    seg = jnp.where(ri == ci, 1.0, 0.0)
    ssq = jnp.dot(qr * qr, seg, preferred_element_type=f32, precision=lax.Precision.HIGHEST)
    qr = qr * lax.rsqrt(ssq * (1.0 / MLA_ROPE) + EPS) * qnr_ref[...]
    qr_blocks = [rope(qr[:, p * LANES:(p + 1) * LANES]) * scale for p in range(MLA_H // 2)]

    kv = (_rms(a[:, MLA_QR:MLA_QR + MLA_KVR]) * kvan_ref[...]).astype(bf16)
    kn = _dot(kv, wk_ref[...])
    v_ref[...] = _dot_nt(wv_ref[...], kv).astype(bf16)
    kr = a[:, MLA_QR + MLA_KVR:MLA_IN_PAD]
    kr = rope(_rms(kr, MLA_ROPE) * knr_ref[...])
    kr_odd = pltpu.roll(kr, MLA_ROPE, axis=1)
    for hd in range(MLA_H):
        sl = slice(hd * MLA_NOPE, (hd + 1) * MLA_NOPE)
        q_ref[:, 2 * hd * LANES:(2 * hd + 1) * LANES] = (_rms(qn[:, sl]) * qnn_ref[...] * scale).astype(bf16)
        q_ref[:, (2 * hd + 1) * LANES:(2 * hd + 2) * LANES] = qr_blocks[hd // 2].astype(bf16)
        k_ref[:, 2 * hd * LANES:(2 * hd + 1) * LANES] = (_rms(kn[:, sl]) * knn_ref[...]).astype(bf16)
        k_ref[:, (2 * hd + 1) * LANES:(2 * hd + 2) * LANES] = (kr if hd % 2 == 0 else kr_odd).astype(bf16)


def _mla_prep(a, qan, wqn, wqr, kvan, wk, wv, qnn, qnr, knn, knr, cos_t, sin_t, n_ctx_tiles):
    B, T, _ = a.shape
    tm = ROW_TILE
    tok = lambda w: pl.BlockSpec((None, tm, w), lambda b, i: (b, i, 0))
    full = lambda r, c: pl.BlockSpec((r, c), lambda b, i: (0, 0))
    hw = 2 * LANES * MLA_H
    q_spec = pl.BlockSpec((None, tm, hw), lambda b, i: (b, jnp.maximum(i - n_ctx_tiles, 0), 0))
    return pl.pallas_call(
        _mla_prep_kernel,
        grid=(B, T // tm),
        in_specs=[tok(MLA_IN_PAD), full(1, MLA_QR), full(MLA_QR, MLA_H * MLA_NOPE), full(MLA_QR, MLA_H * MLA_ROPE),
                  full(1, MLA_KVR), full(MLA_KVR, MLA_H * MLA_NOPE), full(MLA_H * MLA_V, MLA_KVR),
                  full(1, MLA_NOPE), full(1, MLA_H * MLA_ROPE), full(1, MLA_NOPE), full(1, LANES),
                  pl.BlockSpec((tm, LANES), lambda b, i: (i, 0)), pl.BlockSpec((tm, LANES), lambda b, i: (i, 0))],
        out_specs=[q_spec, tok(hw), pl.BlockSpec((None, MLA_H * MLA_V, tm), lambda b, i: (b, 0, i))],
        out_shape=[jax.ShapeDtypeStruct((B, T - n_ctx_tiles * tm, hw), bf16), jax.ShapeDtypeStruct((B, T, hw), bf16),
                   jax.ShapeDtypeStruct((B, MLA_H * MLA_V, T), bf16)],
        compiler_params=_cparams(("arbitrary", "arbitrary")),
        name="mla_prep",
    )(a, qan, wqn, wqr, kvan, wk, wv, qnn, qnr, knn, knr, cos_t, sin_t)


ATT_KV_CHUNK = 256
ATT_TQ = 256


def _mla_attn_kernel(q_ref, k_ref, vt_ref, o_ref, s0_ref, s1_ref, m0_ref, m1_ref):
    tq = q_ref.shape[0]
    ck = ATT_KV_CHUNK
    nck = k_ref.shape[0] // ck
    i = pl.program_id(0)

    s_refs, m_refs = (s0_ref, s1_ref), (m0_ref, m1_ref)

    @pl.when(i == 0)
    def _():
        for ref in s_refs + m_refs:
            ref[...] = jnp.zeros_like(ref)

    def fold(a, op):
        return op(a.reshape(ck // SUBLANES, SUBLANES, tq), axis=0)

    def stage(par):
        q = q_ref[...]
        mpart = jnp.full((SUBLANES, tq), -jnp.inf, f32)
        for j in range(nck):
            s = _dot_nt(k_ref[j * ck:(j + 1) * ck, :], q)
            s_refs[par][j * ck:(j + 1) * ck, :] = s
            mpart = jnp.maximum(mpart, fold(s, jnp.max))
        m_refs[par][...] = mpart
        m = jnp.max(m_refs[1 - par][...], axis=0, keepdims=True)
        lpart = jnp.zeros((SUBLANES, tq), f32)
        acc = jnp.zeros((MLA_V, tq), f32)
        for j in range(nck):
            p = jnp.exp2(s_refs[1 - par][j * ck:(j + 1) * ck, :] - m)
            lpart = lpart + fold(p, jnp.sum)
            acc = acc + _dot(vt_ref[:, j * ck:(j + 1) * ck], p.astype(bf16))
        o_ref[...] = (acc / jnp.sum(lpart, axis=0, keepdims=True)).astype(bf16)

    pl.when(i % 2 == 0)(functools.partial(stage, 0))
    pl.when(i % 2 == 1)(functools.partial(stage, 1))


def _mla_attn(q, k, vt):
    B, S, _ = q.shape
    T = k.shape[1]
    tq = ATT_TQ if S % ATT_TQ == 0 else ROW_TILE
    nq = S // tq
    ntile = B * MLA_H * nq
    assert T % ATT_KV_CHUNK == 0

    def tile(g):
        g = jnp.clip(g, 0, ntile - 1)
        return g // (MLA_H * nq), (g // nq) % MLA_H, g % nq

    def q_index(g):
        b, h, i = tile(g)
        return b, i, h

    def k_index(g):
        b, h, _ = tile(g)
        return b, 0, h

    def vt_index(g):
        b, h, _ = tile(g - 1)
        return b, h, 0

    def o_index(g):
        b, h, i = tile(g - 1)
        return b, h, i

    return pl.pallas_call(
        _mla_attn_kernel,
        grid=(ntile + 1,),
        in_specs=[pl.BlockSpec((None, tq, 2 * LANES), q_index),
                  pl.BlockSpec((None, T, 2 * LANES), k_index),
                  pl.BlockSpec((None, MLA_V, T), vt_index)],
        out_specs=pl.BlockSpec((None, MLA_V, tq), o_index),
        out_shape=jax.ShapeDtypeStruct((B, MLA_H * MLA_V, S), bf16),
        scratch_shapes=[pltpu.VMEM((T, tq), f32), pltpu.VMEM((T, tq), f32),
                        pltpu.VMEM((SUBLANES, tq), f32), pltpu.VMEM((SUBLANES, tq), f32)],
        compiler_params=_cparams(("arbitrary",)),
        name="mla_attn",
    )(q, k, vt)


def _axial_angles(rows_n, rot_dim):
    axis_dim = rot_dim // 2
    inv = ROPE_BASE ** (-jnp.arange(0, axis_dim, 2, dtype=f32) / axis_dim)
    row = jnp.repeat(jnp.arange(rows_n, dtype=f32), GRID_W)
    col = jnp.tile(jnp.arange(GRID_W, dtype=f32), rows_n)
    return row[:, None] * inv, col[:, None] * inv


def _rope_tables(seq, n_ctx, rot_dim, reps):
    ang_r, ang_c = _axial_angles(seq // GRID_W, rot_dim)
    cos = jnp.concatenate([jnp.cos(ang_r)] * 2 + [jnp.cos(ang_c)] * 2, axis=1)
    sin = jnp.concatenate([-jnp.sin(ang_r), jnp.sin(ang_r), -jnp.sin(ang_c), jnp.sin(ang_c)], axis=1)
    cos = jnp.concatenate([jnp.ones((n_ctx, rot_dim), f32), cos], axis=0)
    sin = jnp.concatenate([jnp.zeros((n_ctx, rot_dim), f32), sin], axis=0)
    return jnp.tile(cos, (1, reps)), jnp.tile(sin, (1, reps))


def kernel(x, c, ctx, c_ctx, ada_w, ada_b, norm_mix, norm_ffn, ret_w_in, ret_decay_f, ret_decay_b, ret_w_o,
           mla_w_in, mla_q_a_norm, mla_w_q_b, mla_kv_a_norm, mla_w_kv_b, mla_q_norm, mla_k_norm, mla_w_o,
           router_w, router_bias, exp_w_gu, exp_w_down, sh_w_gu, sh_w_down):
    B, S, _ = x.shape
    n_ctx = ctx.shape[1]
    assert n_ctx % ROW_TILE == 0 and S % ROW_TILE == 0 and S % GRID_W == 0
    n_ctx_tiles = n_ctx // ROW_TILE

    rows = -(-(B + 1) // SUBLANES) * SUBLANES
    cc = jnp.zeros((rows, D), f32).at[:B].set(c).at[B].set(c_ctx)
    mod = _ada(cc, ada_w, ada_b)

    def modtab(i):
        ctx_row = jnp.broadcast_to(mod[i, B][None, :], (B, 6 * D))
        return jnp.stack([ctx_row, mod[i, :B]], axis=1)[:, :, None, :]

    mod0, mod1 = modtab(0), modtab(1)

    cos_r, sin_r = _rope_tables(S, n_ctx, RET_DK, 1)
    q, k, v, gf, gb = _ret_inproj(ctx, x, mod0, norm_mix[0][None, :], ret_w_in[0].astype(bf16), cos_r, sin_r,
                                  n_ctx_tiles)
    dtab = jnp.broadcast_to(jnp.concatenate([ret_decay_f[0], ret_decay_b[0]])[:, None], (2 * RET_H, LANES))
    o = _ret_scan(dtab, q, k, v, gf, gb, n_ctx)
    x1, fin, hlin, e_t, w_t, r_t, cnt = _post_mix(
        o, ret_w_o[0].astype(bf16), (ctx, x), 0, mod0, n_ctx_tiles, norm_ffn[0][None, :],
        router_w[0].T, router_bias[0][:, None])
    exp_gu, exp_d = exp_w_gu.astype(bf16), exp_w_down.astype(bf16)
    routed = _routed_experts(e_t, w_t, r_t, cnt, hlin, exp_gu, exp_d, 0)
    w_in1 = jnp.zeros((D, MLA_IN_PAD), f32).at[:, :mla_w_in.shape[2]].set(mla_w_in[0]).astype(bf16)
    x2, a = _post_ffn_mla(x1, routed, fin, sh_w_gu[0].astype(bf16), sh_w_down[0].astype(bf16), mod0, mod1,
                          norm_mix[1][None, :], w_in1, n_ctx_tiles)

    wq = mla_w_q_b[0].reshape(MLA_QR, MLA_H, MLA_QK)
    wqn = wq[:, :, :MLA_NOPE].reshape(MLA_QR, MLA_H * MLA_NOPE).astype(bf16)
    wqr = wq[:, :, MLA_NOPE:].reshape(MLA_QR, MLA_H * MLA_ROPE).astype(bf16)
    wkv = mla_w_kv_b[0].reshape(MLA_KVR, MLA_H, MLA_NOPE + MLA_V)
    wk = wkv[:, :, :MLA_NOPE].reshape(MLA_KVR, MLA_H * MLA_NOPE).astype(bf16)
    wv = wkv[:, :, MLA_NOPE:].reshape(MLA_KVR, MLA_H * MLA_V).T.astype(bf16)
    qnn = mla_q_norm[0][None, :MLA_NOPE]
    qnr = jnp.tile(mla_q_norm[0][None, MLA_NOPE:], (1, MLA_H))
    knn = mla_k_norm[0][None, :MLA_NOPE]
    knr = jnp.concatenate([mla_k_norm[0][MLA_NOPE:], jnp.zeros((LANES - MLA_ROPE,), f32)])[None, :]
    cos_m, sin_m = _rope_tables(S, n_ctx, MLA_ROPE, LANES // MLA_ROPE)
    qf, kf, vf = _mla_prep(a, mla_q_a_norm[0][None, :], wqn, wqr, mla_kv_a_norm[0][None, :], wk, wv,
                           qnn, qnr, knn, knr, cos_m, sin_m, n_ctx_tiles)
    o1 = _mla_attn(qf, kf, vf)
    x3, fin1, hlin1, e1, w1, r1, cnt1 = _post_mix(
        o1, mla_w_o[0].astype(bf16), (x2,), n_ctx_tiles, mod1, 0, norm_ffn[1][None, :],
        router_w[1].T, router_bias[1][:, None], o_transposed=True)
    routed1 = _routed_experts(e1, w1, r1, cnt1, hlin1, exp_gu, exp_d, 1)
    return _post_ffn_final(x3, routed1, fin1, sh_w_gu[1].astype(bf16), sh_w_down[1].astype(bf16), mod1)
```

```python
import dataclasses
import functools

import jax
import jax.numpy as jnp
import numpy as np
from jax import lax
from jax.experimental import pallas as pl
from jax.experimental.pallas import tpu as pltpu
from jax.experimental.pallas import tpu_sc as plsc

f32 = jnp.float32
bf16 = jnp.bfloat16
i32 = jnp.int32

D = 1024
GRID_W = 64
EPS = 1e-6
ROPE_BASE = 10000.0
RET_H = 4
RET_DK = 256
RET_DV = 512
RET_VW = RET_H * RET_DV
RET_CHUNK = 256
MLA_H = 8
MLA_NOPE = 128
MLA_ROPE = 64
MLA_QK = MLA_NOPE + MLA_ROPE
MLA_V = 128
MLA_QR = 384
MLA_KVR = 256
MLA_IN_PAD = 768
N_EXP = 64
TOP_K = 8
N_GRP = 8
TOPK_GRP = 4
EXP_FF = 256
SH_FF = 256
ROUTED_SCALE = 2.5

LANES = 128
SUBLANES = 8
SC_LANES = 16
MXU_N = 256
ROW_TILE = 256
MOE_TILE = 256
MOE_GROUP = 2
VMEM_LIMIT = 56 * 1024 * 1024
MOE_VMEM_LIMIT = 62 * 1024 * 1024


def _cparams(sem, vmem=VMEM_LIMIT):
    return pltpu.CompilerParams(dimension_semantics=sem, vmem_limit_bytes=vmem)


def _sigmoid(x):
    return 1.0 / (1.0 + jnp.exp(-x))


def _silu(x):
    return x * _sigmoid(x)


def _rms(x, n=None):
    n = x.shape[-1] if n is None else n
    return x * lax.rsqrt(jnp.sum(x * x, axis=-1, keepdims=True) * (1.0 / n) + EPS)


def _dot(a, b):
    return jnp.dot(a, b, preferred_element_type=f32)


def _dot_nt(a, b, precision=None):
    return lax.dot_general(a, b, (((1,), (1,)), ((), ())), preferred_element_type=f32, precision=precision)


def _ada_kernel(c_ref, w_ref, b_ref, o_ref):
    s = _silu(c_ref[...]).astype(bf16)
    o_ref[...] = _dot(s, w_ref[...].astype(bf16)) + b_ref[...]


def _ada(cc, ada_w, ada_b):
    depth = ada_w.shape[0]
    rows = cc.shape[0]
    tn = 1536
    return pl.pallas_call(
        _ada_kernel,
        grid=(depth, 6 * D // tn),
        in_specs=[pl.BlockSpec((rows, D), lambda i, j: (0, 0)),
                  pl.BlockSpec((None, D, tn), lambda i, j: (i, 0, j)),
                  pl.BlockSpec((None, 1, tn), lambda i, j: (i, 0, j))],
        out_specs=pl.BlockSpec((None, rows, tn), lambda i, j: (i, 0, j)),
        out_shape=jax.ShapeDtypeStruct((depth, rows, 6 * D), f32),
        compiler_params=_cparams(("arbitrary", "arbitrary")),
        name="ada",
    )(cc, ada_w, ada_b.reshape(depth, 1, 6 * D))


def _stream_tile(ctx_ref, x_ref, n_ctx_tiles):
    return jnp.where(pl.program_id(1) < n_ctx_tiles, ctx_ref[...], x_ref[...])


def _stream_specs(tm, n_ctx_tiles, last=None):
    clamp = (lambda i: i) if last is None else (lambda i: jnp.minimum(i, last))
    return [pl.BlockSpec((None, tm, D), lambda b, i: (b, jnp.minimum(clamp(i), n_ctx_tiles - 1), 0)),
            pl.BlockSpec((None, tm, D), lambda b, i: (b, jnp.maximum(clamp(i) - n_ctx_tiles, 0), 0))]


def _ret_inproj_kernel(ctx_ref, x_ref, mod_ref, g_ref, w_ref, cos_ref, sin_ref, q_ref, k_ref, v_ref, gf_ref, gb_ref,
                       *, n_ctx_tiles):
    x = _stream_tile(ctx_ref, x_ref, n_ctx_tiles)
    mod = mod_ref[...]
    h = (_rms(x) * g_ref[...]) * (1.0 + mod[:, D:2 * D]) + mod[:, 0:D]
    hb = h.astype(bf16)
    cos = cos_ref[...]
    sin = sin_ref[...]

    def rope(a):
        outs = []
        for half in range(2):
            sl = slice(half * LANES, (half + 1) * LANES)
            ah = a[:, sl]
            outs.append(ah * cos[:, sl] + pltpu.roll(ah, LANES // 2, axis=1) * sin[:, sl])
        return jnp.concatenate(outs, axis=1)

    for hd in range(RET_H):
        sl = slice(hd * RET_DK, (hd + 1) * RET_DK)
        q_ref[:, sl] = rope(_dot(hb, w_ref[:, sl])).astype(bf16)
    for hd in range(RET_H):
        sl = slice(hd * RET_DK, (hd + 1) * RET_DK)
        wsl = slice(D + hd * RET_DK, D + (hd + 1) * RET_DK)
        k_ref[:, sl] = (rope(_dot(hb, w_ref[:, wsl])) * (RET_DK ** -0.5)).astype(bf16)
    cw = 512
    for c in range(RET_VW // cw):
        sl = slice(c * cw, (c + 1) * cw)
        v_ref[:, sl] = _dot(hb, w_ref[:, 2 * D + c * cw:2 * D + (c + 1) * cw]).astype(bf16)
        gf_ref[:, sl] = _silu(_dot(hb, w_ref[:, 2 * D + RET_VW + c * cw:2 * D + RET_VW + (c + 1) * cw])).astype(bf16)
        gb_ref[:, sl] = _silu(_dot(hb, w_ref[:, 2 * D + 2 * RET_VW + c * cw:2 * D + 2 * RET_VW + (c + 1) * cw])).astype(bf16)


def _ret_inproj(ctx, x, modtab, gain, w_in, cos_t, sin_t, n_ctx_tiles):
    B = x.shape[0]
    T = ctx.shape[1] + x.shape[1]
    tm = ROW_TILE
    n_in = w_in.shape[1]
    tok = lambda w: pl.BlockSpec((None, tm, w), lambda b, i: (b, i, 0))
    return pl.pallas_call(
        functools.partial(_ret_inproj_kernel, n_ctx_tiles=n_ctx_tiles),
        grid=(B, T // tm),
        in_specs=_stream_specs(tm, n_ctx_tiles) + [
                  pl.BlockSpec((None, None, 1, 6 * D), lambda b, i: (b, jnp.where(i < n_ctx_tiles, 0, 1), 0, 0)),
                  pl.BlockSpec((1, D), lambda b, i: (0, 0)),
                  pl.BlockSpec((D, n_in), lambda b, i: (0, 0), pipeline_mode=pl.Buffered(1)),
                  pl.BlockSpec((tm, RET_DK), lambda b, i: (i, 0)),
                  pl.BlockSpec((tm, RET_DK), lambda b, i: (i, 0))],
        out_specs=[tok(D), tok(D), tok(RET_VW), tok(RET_VW), tok(RET_VW)],
        out_shape=[jax.ShapeDtypeStruct((B, T, D), bf16), jax.ShapeDtypeStruct((B, T, D), bf16),
                   jax.ShapeDtypeStruct((B, T, RET_VW), bf16), jax.ShapeDtypeStruct((B, T, RET_VW), bf16),
                   jax.ShapeDtypeStruct((B, T, RET_VW), bf16)],
        compiler_params=_cparams(("arbitrary", "arbitrary")),
        name="ret_inproj",
    )(ctx, x, modtab, gain, w_in, cos_t, sin_t)


def _ret_chunk_index(t, nc, ncc):
    u = t - nc
    back = jnp.where(u < ncc, ncc - 1 - u, nc - 1 - u + ncc)
    return jnp.where(t < nc, t, back)


def _ret_scan_kernel(dt_ref, q_ref, k_ref, v_ref, gf_ref, gb_ref, o_ref,
                     s_ref, of_ref, mask_ref, dq_ref, dk_ref, dc_ref, *, nc, ncc):
    t = pl.program_id(1)
    C = RET_CHUNK

    def init(direction):
        s_ref[...] = jnp.zeros_like(s_ref)
        ii = lax.broadcasted_iota(i32, (C, C), 0)
        jj = lax.broadcasted_iota(i32, (C, C), 1)
        rel = (ii - jj if direction == 0 else jj - ii).astype(f32)
        pos = lax.broadcasted_iota(i32, (C, 1), 0).astype(f32)
        for hd in range(RET_H):
            r = direction * RET_H + hd
            lg = -jnp.exp(dt_ref[r:r + 1, :])
            lg1 = lg[:, 0:1]
            mask_ref[hd] = jnp.where(rel >= 0, jnp.exp(lg1 * jnp.maximum(rel, 0.0)), 0.0)
            if direction == 0:
                dq_ref[hd] = jnp.exp(lg1 * (pos + 1.0))
                dk_ref[hd] = jnp.exp(lg1 * (C - 1.0 - pos))
            else:
                dq_ref[hd] = jnp.exp(lg1 * (C - pos))
                dk_ref[hd] = jnp.exp(lg1 * pos)
            dc_ref[hd] = jnp.exp(lg * float(C))

    pl.when(t == 0)(functools.partial(init, 0))
    pl.when(t == nc)(functools.partial(init, 1))

    fwd = t < nc
    row0 = pl.multiple_of(_ret_chunk_index(t, nc, ncc) * C, C)

    for hd in range(RET_H):
        ks = slice(hd * RET_DK, (hd + 1) * RET_DK)
        vs = slice(hd * RET_DV, (hd + 1) * RET_DV)
        qh = q_ref[:, ks]
        kh = k_ref[:, ks]
        vh = v_ref[:, vs]
        p = (_dot_nt(qh, kh) * mask_ref[hd]).astype(bf16)
        y = _dot(p, vh) + _dot(qh, s_ref[hd].astype(bf16)) * dq_ref[hd]
        kd = (kh.astype(f32) * dk_ref[hd]).astype(bf16)
        upd = lax.dot_general(kd, vh, (((0,), (0,)), ((), ())), preferred_element_type=f32)
        s_ref[hd] = s_ref[hd] * dc_ref[hd][0:1, 0:1] + upd
        yn = _rms(y)

        @pl.when(fwd)
        def _():
            of_ref[pl.ds(row0, C), vs] = (gf_ref[:, vs].astype(f32) * yn).astype(bf16)

        @pl.when(jnp.logical_not(fwd))
        def _():
            o_ref[:, vs] = (of_ref[pl.ds(row0, C), vs].astype(f32) + gb_ref[:, vs].astype(f32) * yn).astype(bf16)


def _ret_scan(dtab, q, k, v, gf, gb, n_ctx):
    B, T, _ = q.shape
    C = RET_CHUNK
    nc = T // C
    ncc = n_ctx // C
    cidx = functools.partial(_ret_chunk_index, nc=nc, ncc=ncc)
    first_back = ncc - 1
    return pl.pallas_call(
        functools.partial(_ret_scan_kernel, nc=nc, ncc=ncc),
        grid=(B, 2 * nc),
        in_specs=[pl.BlockSpec((2 * RET_H, LANES), lambda b, t: (0, 0)),
                  pl.BlockSpec((None, C, D), lambda b, t: (b, cidx(t), 0)),
                  pl.BlockSpec((None, C, D), lambda b, t: (b, cidx(t), 0)),
                  pl.BlockSpec((None, C, RET_VW), lambda b, t: (b, cidx(t), 0)),
                  pl.BlockSpec((None, C, RET_VW), lambda b, t: (b, jnp.where(t < nc, t, nc - 1), 0)),
                  pl.BlockSpec((None, C, RET_VW), lambda b, t: (b, jnp.where(t < nc, first_back, cidx(t)), 0))],
        out_specs=pl.BlockSpec((None, C, RET_VW), lambda b, t: (b, jnp.where(t < nc, first_back, cidx(t)), 0)),
        out_shape=jax.ShapeDtypeStruct((B, T, RET_VW), bf16),
        scratch_shapes=[pltpu.VMEM((RET_H, RET_DK, RET_DV), f32),
                        pltpu.VMEM((T, RET_VW), bf16),
                        pltpu.VMEM((RET_H, C, C), f32),
                        pltpu.VMEM((RET_H, C, 1), f32),
                        pltpu.VMEM((RET_H, C, 1), f32),
                        pltpu.VMEM((RET_H, 1, LANES), f32)],
        compiler_params=_cparams(("arbitrary", "arbitrary")),
        name="ret_scan",
    )(dtab, q, k, v, gf, gb)


def _route(f, rwt_ref, rb_ref, cnt_ref, e_ref, w_ref, r_ref):
    tm = f.shape[0]
    G = N_EXP // N_GRP
    logits = _dot_nt(rwt_ref[...].astype(bf16), f.astype(bf16))
    s = _sigmoid(logits)
    sel = s + rb_ref[...]
    mi = lax.broadcasted_iota(i32, (G, tm), 0)
    neg = -jnp.inf
    s_g = [s[g * G:(g + 1) * G, :] for g in range(N_GRP)]
    sel_g = [sel[g * G:(g + 1) * G, :] for g in range(N_GRP)]

    def first_max(a, ids, big):
        mx = jnp.max(a, axis=0, keepdims=True)
        ix = jnp.min(jnp.where(a == mx, ids, big), axis=0, keepdims=True)
        return mx, ix

    gscore = jnp.zeros((N_GRP, tm), f32)
    gi = lax.broadcasted_iota(i32, (N_GRP, tm), 0)
    for g in range(N_GRP):
        t1, i1 = first_max(sel_g[g], mi, G)
        t2 = jnp.max(jnp.where(mi == i1, neg, sel_g[g]), axis=0, keepdims=True)
        gscore = jnp.where(gi == g, t1 + t2, gscore)
    gmask = jnp.zeros((N_GRP, tm), i32)
    cur = gscore
    for _ in range(TOPK_GRP):
        _, ix = first_max(cur, gi, N_GRP)
        hit = gi == ix
        gmask = jnp.where(hit, 1, gmask)
        cur = jnp.where(hit, neg, cur)
    cand = [jnp.where(gmask[g:g + 1, :] > 0, sel_g[g], neg) for g in range(N_GRP)]
    ids = [mi + g * G for g in range(N_GRP)]

    e_rows, w_rows = [], []
    for _ in range(TOP_K):
        mx = cand[0].max(axis=0, keepdims=True)
        for g in range(1, N_GRP):
            mx = jnp.maximum(mx, cand[g].max(axis=0, keepdims=True))
        ix = jnp.min(jnp.where(cand[0] == mx, ids[0], N_EXP), axis=0, keepdims=True)
        for g in range(1, N_GRP):
            ix = jnp.minimum(ix, jnp.min(jnp.where(cand[g] == mx, ids[g], N_EXP), axis=0, keepdims=True))
        wv = jnp.zeros((1, tm), f32)
        for g in range(N_GRP):
            hit = ids[g] == ix
            cand[g] = jnp.where(hit, neg, cand[g])
            wv = wv + jnp.sum(jnp.where(hit, s_g[g], 0.0), axis=0, keepdims=True)
        e_rows.append(ix)
        w_rows.append(wv)
    wsum = w_rows[0]
    for r in range(1, TOP_K):
        wsum = wsum + w_rows[r]

    selm = [jnp.zeros((G, tm), f32) for _ in range(N_GRP)]
    for r in range(TOP_K):
        for g in range(N_GRP):
            selm[g] = jnp.where(ids[g] == e_rows[r], 1.0, selm[g])
    m_all = jnp.concatenate(selm, axis=0)
    ri = lax.broadcasted_iota(i32, (tm, tm), 0)
    ci = lax.broadcasted_iota(i32, (tm, tm), 1)
    upper = jnp.where(ri <= ci, 1.0, 0.0).astype(bf16)
    incl = _dot(m_all.astype(bf16), upper)
    carry = cnt_ref[:, 0:1]
    rank_all = carry + incl - m_all
    cnt_ref[...] = cnt_ref[...] + incl[:, tm - 1:tm]
    for r in range(TOP_K):
        rk = jnp.zeros((1, tm), f32)
        for g in range(N_GRP):
            rk = rk + jnp.sum(jnp.where(ids[g] == e_rows[r], rank_all[g * G:(g + 1) * G, :], 0.0), axis=0, keepdims=True)
        e_ref[r:r + 1, :] = e_rows[r]
        w_ref[r:r + 1, :] = w_rows[r] / wsum * ROUTED_SCALE
        r_ref[r:r + 1, :] = rk.astype(i32)


def _post_mix_kernel(o_ref, wo_ref, *refs, nt, o_transposed, split_ctx_tiles):
    n_resid = 2 if split_ctx_tiles else 1
    resid = refs[:n_resid]
    mod_ref, g_ref, rwt_ref, rb_ref, x1_ref, fin_ref, hlin_ref, e_ref, w_ref, r_ref, cnt_ref = refs[n_resid:]
    i = pl.program_id(1)

    @pl.when(i == 0)
    def _():
        cnt_ref[...] = jnp.zeros_like(cnt_ref)

    @pl.when(i < nt)
    def _():
        mod = mod_ref[...]
        if o_transposed:
            mixed = lax.dot_general(o_ref[...], wo_ref[...], (((0,), (0,)), ((), ())), preferred_element_type=f32)
        else:
            mixed = _dot(o_ref[...], wo_ref[...])
        x = _stream_tile(resid[0], resid[1], split_ctx_tiles) if split_ctx_tiles else resid[0][...]
        x1 = x + mod[:, 2 * D:3 * D] * mixed
        x1_ref[...] = x1
        f = (_rms(x1) * g_ref[...]) * (1.0 + mod[:, 4 * D:5 * D]) + mod[:, 3 * D:4 * D]
        fin_ref[...] = f.astype(bf16)
        tm = f.shape[0]
        for j in range(D // LANES):
            hlin_ref[pl.ds(j, tm, stride=D // LANES), :] = f[:, j * LANES:(j + 1) * LANES]
        _route(f, rwt_ref, rb_ref, cnt_ref, e_ref, w_ref, r_ref)

    @pl.when(i == nt)
    def _():
        hlin_ref[...] = jnp.zeros_like(hlin_ref)


def _post_mix(o, w_o, resid, x_tile_off, modtab, n_ctx_tiles, gain, rwt, rb, o_transposed=False):
    if o_transposed:
        B, KO, N = o.shape
    else:
        B, N, KO = o.shape
    tm = ROW_TILE
    nt = N // tm
    last = lambda i: jnp.minimum(i, nt - 1)
    tok = lambda w: pl.BlockSpec((None, tm, w), lambda b, i: (b, last(i), 0))
    sel = lambda: pl.BlockSpec((None, TOP_K, tm), lambda b, i: (b, 0, last(i)))
    o_spec = (pl.BlockSpec((None, KO, tm), lambda b, i: (b, 0, last(i))) if o_transposed else tok(KO))
    split = len(resid) == 2
    resid_specs = (_stream_specs(tm, n_ctx_tiles, nt - 1) if split else
                   [pl.BlockSpec((None, tm, D), lambda b, i: (b, last(i) + x_tile_off, 0))])
    return pl.pallas_call(
        functools.partial(_post_mix_kernel, nt=nt, o_transposed=o_transposed,
                          split_ctx_tiles=n_ctx_tiles if split else 0),
        grid=(B, nt + 1),
        in_specs=[o_spec,
                  pl.BlockSpec((KO, D), lambda b, i: (0, 0))] + resid_specs + [
                  pl.BlockSpec((None, None, 1, 6 * D), lambda b, i: (b, jnp.where(last(i) < n_ctx_tiles, 0, 1), 0, 0)),
                  pl.BlockSpec((1, D), lambda b, i: (0, 0)),
                  pl.BlockSpec((N_EXP, D), lambda b, i: (0, 0)),
                  pl.BlockSpec((N_EXP, 1), lambda b, i: (0, 0))],
        out_specs=[tok(D), tok(D),
                   pl.BlockSpec((None, tm * (D // LANES), LANES), lambda b, i: (b, i, 0)),
                   sel(), sel(), sel(),
                   pl.BlockSpec((None, N_EXP, LANES), lambda b, i: (b, 0, 0))],
        out_shape=[jax.ShapeDtypeStruct((B, N, D), f32), jax.ShapeDtypeStruct((B, N, D), bf16),
                   jax.ShapeDtypeStruct((B, (N + tm) * (D // LANES), LANES), f32),
                   jax.ShapeDtypeStruct((B, TOP_K, N), i32), jax.ShapeDtypeStruct((B, TOP_K, N), f32),
                   jax.ShapeDtypeStruct((B, TOP_K, N), i32),
                   jax.ShapeDtypeStruct((B, N_EXP, LANES), f32)],
        compiler_params=_cparams(("arbitrary", "arbitrary")),
        name="post_mix",
    )(o, w_o, *resid, modtab, gain, rwt, rb)


META_W = 256
PLAN_ALIGN = 1024


def _round_up(n, m):
    return -(-n // m) * m


def _moe_sizes(n_tok):
    tm = MOE_TILE
    nt_max = (n_tok * TOP_K + N_EXP * (tm - 1)) // tm + 1
    ntp = _round_up(nt_max + 1, PLAN_ALIGN // tm)
    assert ntp <= META_W
    return ntp, _round_up(n_tok, PLAN_ALIGN)


def _plan_kernel(e_ref, r_ref, cnt_ref, pos_ref, meta_ref, *, n_tok):
    tm = MOE_TILE
    ntile = jnp.floor((cnt_ref[...] + (tm - 1.0)) * (1.0 / tm))
    ntb = ntile.astype(bf16)
    ei = lax.broadcasted_iota(i32, (N_EXP, LANES), 0)
    ej = lax.broadcasted_iota(i32, (N_EXP, LANES), 1)
    lower = jnp.where(ej <= ei, 1.0, 0.0)[:, :N_EXP].astype(bf16)
    tend = _dot(lower, ntb)
    tstart = tend - ntile
    tt = lax.broadcasted_iota(i32, (N_EXP, META_W), 1).astype(f32)
    te = jnp.sum(jnp.where(tt >= tend[:, 0:1], 1.0, 0.0), axis=0, keepdims=True)
    meta_ref[...] = jnp.zeros_like(meta_ref)
    meta_ref[0:1, :] = jnp.minimum(te, N_EXP - 1.0).astype(i32)
    meta_ref[1:2, :] = jnp.broadcast_to(tend[N_EXP - 1:N_EXP, 0:1], (1, META_W)).astype(i32)
    e = e_ref[...]
    base = jnp.zeros(e.shape, f32)
    for ex in range(N_EXP):
        base = jnp.where(e == ex, tstart[ex:ex + 1, 0:1] * float(tm), base)
    pos_ref[...] = jnp.zeros_like(pos_ref)
    pos_ref[:, 0:n_tok] = base.astype(i32) + r_ref[...]


def _plan(e_t, r_t, cnt):
    B, K, N = e_t.shape
    _, npad = _moe_sizes(N)
    return pl.pallas_call(
        functools.partial(_plan_kernel, n_tok=N),
        grid=(B,),
        in_specs=[pl.BlockSpec((None, K, N), lambda b: (b, 0, 0)),
                  pl.BlockSpec((None, K, N), lambda b: (b, 0, 0)),
                  pl.BlockSpec((None, N_EXP, LANES), lambda b: (b, 0, 0))],
        out_specs=[pl.BlockSpec((None, K, npad), lambda b: (b, 0, 0)),
                   pl.BlockSpec((None, SUBLANES, META_W), lambda b: (b, 0, 0))],
        out_shape=[jax.ShapeDtypeStruct((B, K, npad), i32), jax.ShapeDtypeStruct((B, SUBLANES, META_W), i32)],
        compiler_params=_cparams(("arbitrary",)),
        name="moe_plan",
    )(e_t, r_t, cnt)


def _plan_invert(pos, w_t, n_tok):
    B, K, npad = pos.shape
    ntp, _ = _moe_sizes(n_tok)
    plen = ntp * MOE_TILE
    nch = D // LANES
    mesh = plsc.VectorSubcoreMesh(core_axis_name="c", subcore_axis_name="s")
    n_cores = mesh.num_cores
    assert 2 * B <= n_cores * mesh.num_subcores and n_tok % SC_LANES == 0 and plen % SC_LANES == 0

    @functools.partial(
        pl.kernel, mesh=mesh,
        out_type=[jax.ShapeDtypeStruct((B * plen,), i32), jax.ShapeDtypeStruct((B * plen,), f32)],
        scratch_types=[pltpu.VMEM((npad,), i32), pltpu.VMEM((n_tok,), f32),
                       pltpu.VMEM((plen,), i32), pltpu.VMEM((plen,), f32)],
        compiler_params=dataclasses.replace(pltpu.CompilerParams(), needs_layout_passes=False))
    def invert(pos_hbm, w_hbm, rows_hbm, ws_hbm, pos_c, w_c, rows_v, ws_v):
        wid = lax.axis_index("s") * n_cores + lax.axis_index("c")
        b = wid % B
        lane = lax.iota(i32, SC_LANES)

        def load_pos(k):
            pltpu.sync_copy(pos_hbm.at[pl.ds(pl.multiple_of((b * K + k) * npad, SUBLANES), npad)], pos_c)

        @pl.when(wid < B)
        def _():
            pad = jnp.full((SC_LANES,), n_tok * nch, i32)

            @pl.loop(0, plen, step=SC_LANES)
            def _(i):
                rows_v[pl.ds(i, SC_LANES)] = pad

            for k in range(K):
                load_pos(k)

                @pl.loop(0, n_tok, step=SC_LANES)
                def _(n):
                    plsc.store_scatter(rows_v, [pos_c[pl.ds(n, SC_LANES)]], (lane + n) * nch)

            pltpu.sync_copy(rows_v, rows_hbm.at[pl.ds(pl.multiple_of(b * plen, SUBLANES), plen)])

        @pl.when(jnp.logical_and(wid >= B, wid < 2 * B))
        def _():
            zero = jnp.zeros((SC_LANES,), f32)

            @pl.loop(0, plen, step=SC_LANES)
            def _(i):
                ws_v[pl.ds(i, SC_LANES)] = zero

            for k in range(K):
                load_pos(k)
                pltpu.sync_copy(w_hbm.at[pl.ds(pl.multiple_of((b * K + k) * n_tok, SUBLANES), n_tok)], w_c)

                @pl.loop(0, n_tok, step=SC_LANES)
                def _(n):
                    plsc.store_scatter(ws_v, [pos_c[pl.ds(n, SC_LANES)]], w_c[pl.ds(n, SC_LANES)])

            pltpu.sync_copy(ws_v, ws_hbm.at[pl.ds(pl.multiple_of(b * plen, SUBLANES), plen)])

    return invert(pos.reshape(-1), w_t.reshape(-1))


def _moe_kernel(te_ref, nt_ref, *refs, ntp, n_tok):
    R = MOE_GROUP
    rg_refs, rs_refs, ws_refs = refs[0:R], refs[R:2 * R], refs[2 * R:3 * R]
    hlin_ref = refs[3 * R]
    wgu_refs, wd_refs = refs[3 * R + 1:4 * R + 1], refs[4 * R + 1:5 * R + 1]
    out_ref, acc_ref = refs[5 * R + 1:5 * R + 3]
    bufs = refs[5 * R + 3:]
    xs_refs = [bufs[0:R], bufs[R:2 * R]]
    ylin_refs = [bufs[2 * R:3 * R], bufs[3 * R:4 * R]]
    b = pl.program_id(0)
    t = pl.program_id(1)
    TM = MOE_TILE
    NCH = D // LANES
    U = 8
    ngrp = ntp // R

    @pl.when(t == 0)
    def _():
        acc_ref[...] = jnp.zeros_like(acc_ref)
        for buf in bufs:
            buf[...] = jnp.zeros_like(buf)

    def gather_rows(p, r, c):
        for m in range(c * U, (c + 1) * U):
            off = pl.multiple_of(rg_refs[r][m], NCH)
            xs_refs[p][r][pl.ds(m * NCH, NCH), :] = hlin_ref[pl.ds(off, NCH), :]

    def scatter_rows(p, r, c):
        offs = [pl.multiple_of(rs_refs[r][c * U + u], NCH) for u in range(U)]
        news = []
        for u in range(U):
            m = c * U + u
            yrow = ylin_refs[p][r][pl.ds((m // SUBLANES) * (NCH * SUBLANES) + m % SUBLANES, NCH,
                                         stride=SUBLANES), :]
            news.append(acc_ref[pl.ds(offs[u], NCH), :] + yrow)
        for u in range(U):
            acc_ref[pl.ds(offs[u], NCH), :] = news[u]

    def stage(p):
        row_work = [functools.partial(fn, p, r, c) for c in range(TM // U) for r in range(R)
                    for fn in (gather_rows, scatter_rows)]
        n_pieces = R * (EXP_FF // LANES + D // MXU_N)
        per_piece = -(-len(row_work) // n_pieces)

        def deal():
            for fn in row_work[:per_piece]:
                fn()
            del row_work[:per_piece]

        for r in range(R):
            x = jnp.concatenate([xs_refs[1 - p][r][pl.ds(j, TM, stride=NCH), :] for j in range(NCH)],
                                axis=1).astype(bf16)
            wcol = jnp.broadcast_to(ws_refs[r][...], (SUBLANES, TM)).T[:, 0:1]
            acts = []
            for c in range(EXP_FF // LANES):
                gu = _dot(x, wgu_refs[r][:, c * MXU_N:(c + 1) * MXU_N])
                acts.append((_silu(gu[:, :LANES]) * gu[:, LANES:] * wcol).astype(bf16))
                deal()
            a = jnp.concatenate(acts, axis=1)
            for c in range(D // MXU_N):
                y = _dot(a, wd_refs[r][:, c * MXU_N:(c + 1) * MXU_N])
                for g in range(TM // SUBLANES):
                    for jj in range(MXU_N // LANES):
                        j = c * (MXU_N // LANES) + jj
                        ylin_refs[1 - p][r][pl.ds((g * NCH + j) * SUBLANES, SUBLANES), :] = (
                            y[g * SUBLANES:(g + 1) * SUBLANES, jj * LANES:(jj + 1) * LANES])
                deal()
        while row_work:
            deal()

    live = (t - 2) * R < nt_ref[b]
    pl.when(jnp.logical_and(live, t % 2 == 0))(functools.partial(stage, 0))
    pl.when(jnp.logical_and(live, t % 2 == 1))(functools.partial(stage, 1))

    @pl.when(t >= ngrp + 2)
    def _():
        row0 = (t - (ngrp + 2)) * (ROW_TILE * NCH)
        for j in range(NCH):
            out_ref[:, j * LANES:(j + 1) * LANES] = acc_ref[pl.ds(row0 + j, ROW_TILE, stride=NCH), :]


def _moe(te, nt, rows, wsort, hlin, w_gu, w_d, layer):
    B = hlin.shape[0]
    NCH = D // LANES
    n_tok = hlin.shape[1] // NCH - ROW_TILE
    assert n_tok % ROW_TILE == 0
    nf = n_tok // ROW_TILE
    TM = MOE_TILE
    R = MOE_GROUP
    ntp, _ = _moe_sizes(n_tok)
    assert ntp % R == 0
    ngrp = ntp // R
    pad_tile = ntp - 1

    def tile_of(b, t, nt_ref, r, lag):
        tile = (t - lag) * R + r
        ok = jnp.logical_and(t >= lag, tile < nt_ref[b])
        return b * ntp + jnp.where(ok, tile, pad_tile)

    def rows_spec(r, lag):
        return pl.BlockSpec((TM,), lambda b, t, te_ref, nt_ref: (tile_of(b, t, nt_ref, r, lag),),
                            memory_space=pltpu.SMEM)

    def ws_spec(r):
        return pl.BlockSpec((None, 1, TM), lambda b, t, te_ref, nt_ref: (tile_of(b, t, nt_ref, r, 1), 0, 0))

    def w_spec(shape, r):
        def index(b, t, te_ref, nt_ref):
            tile = jnp.clip((t - 1) * R + r, 0, nt_ref[b] - 1)
            return (layer, te_ref[b * META_W + tile], 0, 0)
        return pl.BlockSpec((None, None) + shape, index)

    grid_spec = pltpu.PrefetchScalarGridSpec(
        num_scalar_prefetch=2,
        grid=(B, ngrp + 2 + nf),
        in_specs=([rows_spec(r, 0) for r in range(R)] + [rows_spec(r, 2) for r in range(R)]
                  + [ws_spec(r) for r in range(R)]
                  + [pl.BlockSpec((None, (n_tok + ROW_TILE) * NCH, LANES), lambda b, t, *_: (b, 0, 0),
                                  pipeline_mode=pl.Buffered(1))]
                  + [w_spec((D, 2 * EXP_FF), r) for r in range(R)]
                  + [w_spec((EXP_FF, D), r) for r in range(R)]),
        out_specs=pl.BlockSpec((None, ROW_TILE, D), lambda b, t, *_: (b, jnp.maximum(t - (ngrp + 2), 0), 0)),
        scratch_shapes=([pltpu.VMEM(((n_tok + SUBLANES) * NCH, LANES), f32)]
                        + [pltpu.VMEM((TM * NCH, LANES), f32) for _ in range(4 * R)]),
    )
    return pl.pallas_call(
        functools.partial(_moe_kernel, ntp=ntp, n_tok=n_tok),
        grid_spec=grid_spec,
        out_shape=jax.ShapeDtypeStruct((B, n_tok, D), f32),
        compiler_params=_cparams(("arbitrary", "arbitrary"), MOE_VMEM_LIMIT),
        name="moe",
    )(te, nt, *([rows] * (2 * R)), *([wsort.reshape(B * ntp, 1, TM)] * R), hlin, *([w_gu] * R), *([w_d] * R))


def _routed_experts(e_t, w_t, r_t, cnt, hlin, w_gu, w_d, layer):
    N = e_t.shape[2]
    pos, meta = _plan(e_t, r_t, cnt)
    te = meta[:, 0, :].reshape(-1)
    nt = meta[:, 1, 0]
    rows, wsort = _plan_invert(pos, w_t, N)
    return _moe(te, nt, rows, wsort, hlin, w_gu, w_d, layer)


def _shared_ffn(fin, shgu_ref, shd_ref):
    gu = _dot(fin, shgu_ref[...])
    return _dot((_silu(gu[:, :SH_FF]) * gu[:, SH_FF:]).astype(bf16), shd_ref[...])


def _post_ffn_mla_kernel(x1_ref, routed_ref, fin_ref, shgu_ref, shd_ref, mod0_ref, mod1_ref, g_ref, win_ref,
                         x2_ref, a_ref):
    x2 = x1_ref[...] + mod0_ref[...][:, 5 * D:6 * D] * (routed_ref[...] + _shared_ffn(fin_ref[...], shgu_ref, shd_ref))
    x2_ref[...] = x2
    mod1 = mod1_ref[...]
    h = (_rms(x2) * g_ref[...]) * (1.0 + mod1[:, D:2 * D]) + mod1[:, 0:D]
    a_ref[...] = _dot(h.astype(bf16), win_ref[...])


def _post_ffn_mla(x1, routed, fin, sh_gu, sh_d, modtab0, modtab1, gain, w_in, n_ctx_tiles):
    B, T, _ = x1.shape
    tm = ROW_TILE
    tok = lambda w: pl.BlockSpec((None, tm, w), lambda b, i: (b, i, 0))
    modspec = lambda: pl.BlockSpec((None, None, 1, 6 * D), lambda b, i: (b, jnp.where(i < n_ctx_tiles, 0, 1), 0, 0))
    full = lambda r, c: pl.BlockSpec((r, c), lambda b, i: (0, 0))
    return pl.pallas_call(
        _post_ffn_mla_kernel,
        grid=(B, T // tm),
        in_specs=[tok(D), tok(D), tok(D), full(D, 2 * SH_FF), full(SH_FF, D), modspec(), modspec(),
                  full(1, D), full(D, MLA_IN_PAD)],
        out_specs=[tok(D), tok(MLA_IN_PAD)],
        out_shape=[jax.ShapeDtypeStruct((B, T, D), f32), jax.ShapeDtypeStruct((B, T, MLA_IN_PAD), f32)],
        compiler_params=_cparams(("arbitrary", "arbitrary")),
        name="post_ffn_mla",
    )(x1, routed, fin, sh_gu, sh_d, modtab0, modtab1, gain, w_in)


def _post_ffn_final_kernel(x1_ref, routed_ref, fin_ref, shgu_ref, shd_ref, mod_ref, out_ref):
    out_ref[...] = x1_ref[...] + mod_ref[...][:, 5 * D:6 * D] * (
        routed_ref[...] + _shared_ffn(fin_ref[...], shgu_ref, shd_ref))


def _post_ffn_final(x1, routed, fin, sh_gu, sh_d, modtab):
    B, N, _ = x1.shape
    tm = ROW_TILE
    tok = lambda w: pl.BlockSpec((None, tm, w), lambda b, i: (b, i, 0))
    full = lambda r, c: pl.BlockSpec((r, c), lambda b, i: (0, 0))
    return pl.pallas_call(
        _post_ffn_final_kernel,
        grid=(B, N // tm),
        in_specs=[tok(D), tok(D), tok(D), full(D, 2 * SH_FF), full(SH_FF, D),
                  pl.BlockSpec((None, None, 1, 6 * D), lambda b, i: (b, 1, 0, 0))],
        out_specs=tok(D),
        out_shape=jax.ShapeDtypeStruct((B, N, D), f32),
        compiler_params=_cparams(("arbitrary", "arbitrary")),
        name="post_ffn_final",
    )(x1, routed, fin, sh_gu, sh_d, modtab)


def _mla_prep_kernel(a_ref, qan_ref, wqn_ref, wqr_ref, kvan_ref, wk_ref, wv_ref, qnn_ref, qnr_ref, knn_ref, knr_ref,
                     cos_ref, sin_ref, q_ref, k_ref, v_ref):
    a = a_ref[...]
    tm = a.shape[0]
    scale = MLA_QK ** -0.5 * float(np.log2(np.e))
    cos = cos_ref[...]
    sin = sin_ref[...]
    lane = lax.broadcasted_iota(i32, (tm, LANES), 1)
    first = (lane // (MLA_ROPE // 4)) % 2 == 0

    def rope(xb):
        sw = jnp.where(first, pltpu.roll(xb, LANES - MLA_ROPE // 4, axis=1), pltpu.roll(xb, MLA_ROPE // 4, axis=1))
        return xb * cos + sw * sin

    qa = (_rms(a[:, :MLA_QR]) * qan_ref[...]).astype(bf16)
    qn = _dot(qa, wqn_ref[...])
    qr = _dot(qa, wqr_ref[...])
    ri = lax.broadcasted_iota(i32, (MLA_H * MLA_ROPE, MLA_H * MLA_ROPE), 0) // MLA_ROPE
    ci = lax.broadcasted_iota(i32, (MLA_H * MLA_ROPE, MLA_H * MLA_ROPE), 1) // MLA_ROPE
    seg = jnp.where(ri == ci, 1.0, 0.0)
    ssq = jnp.dot(qr * qr, seg, preferred_element_type=f32, precision=lax.Precision.HIGHEST)
    qr = qr * lax.rsqrt(ssq * (1.0 / MLA_ROPE) + EPS) * qnr_ref[...]
    qr_blocks = [rope(qr[:, p * LANES:(p + 1) * LANES]) * scale for p in range(MLA_H // 2)]

    kv = (_rms(a[:, MLA_QR:MLA_QR + MLA_KVR]) * kvan_ref[...]).astype(bf16)
    kn = _dot(kv, wk_ref[...])
    v_ref[...] = _dot_nt(wv_ref[...], kv).astype(bf16)
    kr = a[:, MLA_QR + MLA_KVR:MLA_IN_PAD]
    kr = rope(_rms(kr, MLA_ROPE) * knr_ref[...])
    kr_odd = pltpu.roll(kr, MLA_ROPE, axis=1)
    for hd in range(MLA_H):
        sl = slice(hd * MLA_NOPE, (hd + 1) * MLA_NOPE)
        q_ref[:, 2 * hd * LANES:(2 * hd + 1) * LANES] = (_rms(qn[:, sl]) * qnn_ref[...] * scale).astype(bf16)
        q_ref[:, (2 * hd + 1) * LANES:(2 * hd + 2) * LANES] = qr_blocks[hd // 2].astype(bf16)
        k_ref[:, 2 * hd * LANES:(2 * hd + 1) * LANES] = (_rms(kn[:, sl]) * knn_ref[...]).astype(bf16)
        k_ref[:, (2 * hd + 1) * LANES:(2 * hd + 2) * LANES] = (kr if hd % 2 == 0 else kr_odd).astype(bf16)


def _mla_prep(a, qan, wqn, wqr, kvan, wk, wv, qnn, qnr, knn, knr, cos_t, sin_t, n_ctx_tiles):
    B, T, _ = a.shape
    tm = ROW_TILE
    tok = lambda w: pl.BlockSpec((None, tm, w), lambda b, i: (b, i, 0))
    full = lambda r, c: pl.BlockSpec((r, c), lambda b, i: (0, 0))
    hw = 2 * LANES * MLA_H
    q_spec = pl.BlockSpec((None, tm, hw), lambda b, i: (b, jnp.maximum(i - n_ctx_tiles, 0), 0))
    return pl.pallas_call(
        _mla_prep_kernel,
        grid=(B, T // tm),
        in_specs=[tok(MLA_IN_PAD), full(1, MLA_QR), full(MLA_QR, MLA_H * MLA_NOPE), full(MLA_QR, MLA_H * MLA_ROPE),
                  full(1, MLA_KVR), full(MLA_KVR, MLA_H * MLA_NOPE), full(MLA_H * MLA_V, MLA_KVR),
                  full(1, MLA_NOPE), full(1, MLA_H * MLA_ROPE), full(1, MLA_NOPE), full(1, LANES),
                  pl.BlockSpec((tm, LANES), lambda b, i: (i, 0)), pl.BlockSpec((tm, LANES), lambda b, i: (i, 0))],
        out_specs=[q_spec, tok(hw), pl.BlockSpec((None, MLA_H * MLA_V, tm), lambda b, i: (b, 0, i))],
        out_shape=[jax.ShapeDtypeStruct((B, T - n_ctx_tiles * tm, hw), bf16), jax.ShapeDtypeStruct((B, T, hw), bf16),
                   jax.ShapeDtypeStruct((B, MLA_H * MLA_V, T), bf16)],
        compiler_params=_cparams(("arbitrary", "arbitrary")),
        name="mla_prep",
    )(a, qan, wqn, wqr, kvan, wk, wv, qnn, qnr, knn, knr, cos_t, sin_t)


ATT_KV_CHUNK = 256
ATT_TQ = 256


def _mla_attn_kernel(q_ref, k_ref, vt_ref, o_ref, s0_ref, s1_ref, m0_ref, m1_ref):
    tq = q_ref.shape[0]
    ck = ATT_KV_CHUNK
    nck = k_ref.shape[0] // ck
    i = pl.program_id(0)

    s_refs, m_refs = (s0_ref, s1_ref), (m0_ref, m1_ref)

    @pl.when(i == 0)
    def _():
        for ref in s_refs + m_refs:
            ref[...] = jnp.zeros_like(ref)

    def fold(a, op):
        return op(a.reshape(ck // SUBLANES, SUBLANES, tq), axis=0)

    def stage(par):
        q = q_ref[...]
        mpart = jnp.full((SUBLANES, tq), -jnp.inf, f32)
        for j in range(nck):
            s = _dot_nt(k_ref[j * ck:(j + 1) * ck, :], q)
            s_refs[par][j * ck:(j + 1) * ck, :] = s
            mpart = jnp.maximum(mpart, fold(s, jnp.max))
        m_refs[par][...] = mpart
        m = jnp.max(m_refs[1 - par][...], axis=0, keepdims=True)
        lpart = jnp.zeros((SUBLANES, tq), f32)
        acc = jnp.zeros((MLA_V, tq), f32)
        for j in range(nck):
            p = jnp.exp2(s_refs[1 - par][j * ck:(j + 1) * ck, :] - m)
            lpart = lpart + fold(p, jnp.sum)
            acc = acc + _dot(vt_ref[:, j * ck:(j + 1) * ck], p.astype(bf16))
        o_ref[...] = (acc / jnp.sum(lpart, axis=0, keepdims=True)).astype(bf16)

    pl.when(i % 2 == 0)(functools.partial(stage, 0))
    pl.when(i % 2 == 1)(functools.partial(stage, 1))


def _mla_attn(q, k, vt):
    B, S, _ = q.shape
    T = k.shape[1]
    tq = ATT_TQ if S % ATT_TQ == 0 else ROW_TILE
    nq = S // tq
    ntile = B * MLA_H * nq
    assert T % ATT_KV_CHUNK == 0

    def tile(g):
        g = jnp.clip(g, 0, ntile - 1)
        return g // (MLA_H * nq), (g // nq) % MLA_H, g % nq

    def q_index(g):
        b, h, i = tile(g)
        return b, i, h

    def k_index(g):
        b, h, _ = tile(g)
        return b, 0, h

    def vt_index(g):
        b, h, _ = tile(g - 1)
        return b, h, 0

    def o_index(g):
        b, h, i = tile(g - 1)
        return b, h, i

    return pl.pallas_call(
        _mla_attn_kernel,
        grid=(ntile + 1,),
        in_specs=[pl.BlockSpec((None, tq, 2 * LANES), q_index),
                  pl.BlockSpec((None, T, 2 * LANES), k_index),
                  pl.BlockSpec((None, MLA_V, T), vt_index)],
        out_specs=pl.BlockSpec((None, MLA_V, tq), o_index),
        out_shape=jax.ShapeDtypeStruct((B, MLA_H * MLA_V, S), bf16),
        scratch_shapes=[pltpu.VMEM((T, tq), f32), pltpu.VMEM((T, tq), f32),
                        pltpu.VMEM((SUBLANES, tq), f32), pltpu.VMEM((SUBLANES, tq), f32)],
        compiler_params=_cparams(("arbitrary",)),
        name="mla_attn",
    )(q, k, vt)


def _axial_angles(rows_n, rot_dim):
    axis_dim = rot_dim // 2
    inv = ROPE_BASE ** (-jnp.arange(0, axis_dim, 2, dtype=f32) / axis_dim)
    row = jnp.repeat(jnp.arange(rows_n, dtype=f32), GRID_W)
    col = jnp.tile(jnp.arange(GRID_W, dtype=f32), rows_n)
    return row[:, None] * inv, col[:, None] * inv


def _rope_tables(seq, n_ctx, rot_dim, reps):
    ang_r, ang_c = _axial_angles(seq // GRID_W, rot_dim)
    cos = jnp.concatenate([jnp.cos(ang_r)] * 2 + [jnp.cos(ang_c)] * 2, axis=1)
    sin = jnp.concatenate([-jnp.sin(ang_r), jnp.sin(ang_r), -jnp.sin(ang_c), jnp.sin(ang_c)], axis=1)
    cos = jnp.concatenate([jnp.ones((n_ctx, rot_dim), f32), cos], axis=0)
    sin = jnp.concatenate([jnp.zeros((n_ctx, rot_dim), f32), sin], axis=0)
    return jnp.tile(cos, (1, reps)), jnp.tile(sin, (1, reps))


def kernel(x, c, ctx, c_ctx, ada_w, ada_b, norm_mix, norm_ffn, ret_w_in, ret_decay_f, ret_decay_b, ret_w_o,
           mla_w_in, mla_q_a_norm, mla_w_q_b, mla_kv_a_norm, mla_w_kv_b, mla_q_norm, mla_k_norm, mla_w_o,
           router_w, router_bias, exp_w_gu, exp_w_down, sh_w_gu, sh_w_down):
    B, S, _ = x.shape
    n_ctx = ctx.shape[1]
    assert n_ctx % ROW_TILE == 0 and S % ROW_TILE == 0 and S % GRID_W == 0
    n_ctx_tiles = n_ctx // ROW_TILE

    rows = -(-(B + 1) // SUBLANES) * SUBLANES
    cc = jnp.zeros((rows, D), f32).at[:B].set(c).at[B].set(c_ctx)
    mod = _ada(cc, ada_w, ada_b)

    def modtab(i):
        ctx_row = jnp.broadcast_to(mod[i, B][None, :], (B, 6 * D))
        return jnp.stack([ctx_row, mod[i, :B]], axis=1)[:, :, None, :]

    mod0, mod1 = modtab(0), modtab(1)

    cos_r, sin_r = _rope_tables(S, n_ctx, RET_DK, 1)
    q, k, v, gf, gb = _ret_inproj(ctx, x, mod0, norm_mix[0][None, :], ret_w_in[0].astype(bf16), cos_r, sin_r,
                                  n_ctx_tiles)
    dtab = jnp.broadcast_to(jnp.concatenate([ret_decay_f[0], ret_decay_b[0]])[:, None], (2 * RET_H, LANES))
    o = _ret_scan(dtab, q, k, v, gf, gb, n_ctx)
    x1, fin, hlin, e_t, w_t, r_t, cnt = _post_mix(
        o, ret_w_o[0].astype(bf16), (ctx, x), 0, mod0, n_ctx_tiles, norm_ffn[0][None, :],
        router_w[0].T, router_bias[0][:, None])
    n_l, n_e = exp_w_gu.shape[:2]
    exp_gu = exp_w_gu.reshape(n_l, n_e, D, 2, EXP_FF // LANES, LANES).swapaxes(3, 4).reshape(n_l, n_e, D, 2 * EXP_FF)
    exp_gu, exp_d = exp_gu.astype(bf16), exp_w_down.astype(bf16)
    routed = _routed_experts(e_t, w_t, r_t, cnt, hlin, exp_gu, exp_d, 0)
    w_in1 = jnp.zeros((D, MLA_IN_PAD), f32).at[:, :mla_w_in.shape[2]].set(mla_w_in[0]).astype(bf16)
    x2, a = _post_ffn_mla(x1, routed, fin, sh_w_gu[0].astype(bf16), sh_w_down[0].astype(bf16), mod0, mod1,
                          norm_mix[1][None, :], w_in1, n_ctx_tiles)

    wq = mla_w_q_b[0].reshape(MLA_QR, MLA_H, MLA_QK)
    wqn = wq[:, :, :MLA_NOPE].reshape(MLA_QR, MLA_H * MLA_NOPE).astype(bf16)
    wqr = wq[:, :, MLA_NOPE:].reshape(MLA_QR, MLA_H * MLA_ROPE).astype(bf16)
    wkv = mla_w_kv_b[0].reshape(MLA_KVR, MLA_H, MLA_NOPE + MLA_V)
    wk = wkv[:, :, :MLA_NOPE].reshape(MLA_KVR, MLA_H * MLA_NOPE).astype(bf16)
    wv = wkv[:, :, MLA_NOPE:].reshape(MLA_KVR, MLA_H * MLA_V).T.astype(bf16)
    qnn = mla_q_norm[0][None, :MLA_NOPE]
    qnr = jnp.tile(mla_q_norm[0][None, MLA_NOPE:], (1, MLA_H))
    knn = mla_k_norm[0][None, :MLA_NOPE]
    knr = jnp.concatenate([mla_k_norm[0][MLA_NOPE:], jnp.zeros((LANES - MLA_ROPE,), f32)])[None, :]
    cos_m, sin_m = _rope_tables(S, n_ctx, MLA_ROPE, LANES // MLA_ROPE)
    qf, kf, vf = _mla_prep(a, mla_q_a_norm[0][None, :], wqn, wqr, mla_kv_a_norm[0][None, :], wk, wv,
                           qnn, qnr, knn, knr, cos_m, sin_m, n_ctx_tiles)
    o1 = _mla_attn(qf, kf, vf)
    x3, fin1, hlin1, e1, w1, r1, cnt1 = _post_mix(
        o1, mla_w_o[0].astype(bf16), (x2,), n_ctx_tiles, mod1, 0, norm_ffn[1][None, :],
        router_w[1].T, router_bias[1][:, None], o_transposed=True)
    routed1 = _routed_experts(e1, w1, r1, cnt1, hlin1, exp_gu, exp_d, 1)
    return _post_ffn_final(x3, routed1, fin1, sh_w_gu[1].astype(bf16), sh_w_down[1].astype(bf16), mod1)
```

```python
import dataclasses
import functools

import jax
import jax.numpy as jnp
import numpy as np
from jax import lax
from jax.experimental import pallas as pl
from jax.experimental.pallas import tpu as pltpu
from jax.experimental.pallas import tpu_sc as plsc

f32 = jnp.float32
bf16 = jnp.bfloat16
i32 = jnp.int32

D = 1024
GRID_W = 64
EPS = 1e-6
ROPE_BASE = 10000.0
RET_H = 4
RET_DK = 256
RET_DV = 512
RET_VW = RET_H * RET_DV
RET_CHUNK = 256
MLA_H = 8
MLA_NOPE = 128
MLA_ROPE = 64
MLA_QK = MLA_NOPE + MLA_ROPE
MLA_V = 128
MLA_QR = 384
MLA_KVR = 256
MLA_IN_PAD = 768
N_EXP = 64
TOP_K = 8
N_GRP = 8
TOPK_GRP = 4
EXP_FF = 256
SH_FF = 256
ROUTED_SCALE = 2.5

LANES = 128
SUBLANES = 8
SC_LANES = 16
MXU_N = 256
ROW_TILE = 256
MOE_TILE = 256
MOE_GROUP = 2
VMEM_LIMIT = 56 * 1024 * 1024
MOE_VMEM_LIMIT = 62 * 1024 * 1024


def _cparams(sem, vmem=VMEM_LIMIT):
    return pltpu.CompilerParams(dimension_semantics=sem, vmem_limit_bytes=vmem)


def _sigmoid(x):
    return 1.0 / (1.0 + jnp.exp(-x))


def _silu(x):
    return x * _sigmoid(x)


def _rms(x, n=None):
    n = x.shape[-1] if n is None else n
    return x * lax.rsqrt(jnp.sum(x * x, axis=-1, keepdims=True) * (1.0 / n) + EPS)


def _dot(a, b):
    return jnp.dot(a, b, preferred_element_type=f32)


def _dot_nt(a, b, precision=None):
    return lax.dot_general(a, b, (((1,), (1,)), ((), ())), preferred_element_type=f32, precision=precision)


def _ada_kernel(c_ref, w_ref, b_ref, o_ref):
    s = _silu(c_ref[...]).astype(bf16)
    o_ref[...] = _dot(s, w_ref[...].astype(bf16)) + b_ref[...]


def _ada(cc, ada_w, ada_b):
    depth = ada_w.shape[0]
    rows = cc.shape[0]
    tn = 1536
    return pl.pallas_call(
        _ada_kernel,
        grid=(depth, 6 * D // tn),
        in_specs=[pl.BlockSpec((rows, D), lambda i, j: (0, 0)),
                  pl.BlockSpec((None, D, tn), lambda i, j: (i, 0, j)),
                  pl.BlockSpec((None, 1, tn), lambda i, j: (i, 0, j))],
        out_specs=pl.BlockSpec((None, rows, tn), lambda i, j: (i, 0, j)),
        out_shape=jax.ShapeDtypeStruct((depth, rows, 6 * D), f32),
        compiler_params=_cparams(("arbitrary", "arbitrary")),
        name="ada",
    )(cc, ada_w, ada_b.reshape(depth, 1, 6 * D))


def _stream_tile(ctx_ref, x_ref, n_ctx_tiles):
    return jnp.where(pl.program_id(1) < n_ctx_tiles, ctx_ref[...], x_ref[...])


def _stream_specs(tm, n_ctx_tiles, last=None):
    clamp = (lambda i: i) if last is None else (lambda i: jnp.minimum(i, last))
    return [pl.BlockSpec((None, tm, D), lambda b, i: (b, jnp.minimum(clamp(i), n_ctx_tiles - 1), 0)),
            pl.BlockSpec((None, tm, D), lambda b, i: (b, jnp.maximum(clamp(i) - n_ctx_tiles, 0), 0))]


def _ret_inproj_kernel(ctx_ref, x_ref, mod_ref, g_ref, w_ref, cos_ref, sin_ref, q_ref, k_ref, v_ref, gf_ref, gb_ref,
                       *, n_ctx_tiles):
    x = _stream_tile(ctx_ref, x_ref, n_ctx_tiles)
    mod = mod_ref[...]
    h = (_rms(x) * g_ref[...]) * (1.0 + mod[:, D:2 * D]) + mod[:, 0:D]
    hb = h.astype(bf16)
    cos = cos_ref[...]
    sin = sin_ref[...]

    def rope(a):
        outs = []
        for half in range(2):
            sl = slice(half * LANES, (half + 1) * LANES)
            ah = a[:, sl]
            outs.append(ah * cos[:, sl] + pltpu.roll(ah, LANES // 2, axis=1) * sin[:, sl])
        return jnp.concatenate(outs, axis=1)

    for hd in range(RET_H):
        sl = slice(hd * RET_DK, (hd + 1) * RET_DK)
        q_ref[:, sl] = rope(_dot(hb, w_ref[:, sl])).astype(bf16)
    for hd in range(RET_H):
        sl = slice(hd * RET_DK, (hd + 1) * RET_DK)
        wsl = slice(D + hd * RET_DK, D + (hd + 1) * RET_DK)
        k_ref[:, sl] = (rope(_dot(hb, w_ref[:, wsl])) * (RET_DK ** -0.5)).astype(bf16)
    cw = 512
    for c in range(RET_VW // cw):
        sl = slice(c * cw, (c + 1) * cw)
        v_ref[:, sl] = _dot(hb, w_ref[:, 2 * D + c * cw:2 * D + (c + 1) * cw]).astype(bf16)
        gf_ref[:, sl] = _silu(_dot(hb, w_ref[:, 2 * D + RET_VW + c * cw:2 * D + RET_VW + (c + 1) * cw])).astype(bf16)
        gb_ref[:, sl] = _silu(_dot(hb, w_ref[:, 2 * D + 2 * RET_VW + c * cw:2 * D + 2 * RET_VW + (c + 1) * cw])).astype(bf16)


def _ret_inproj(ctx, x, modtab, gain, w_in, cos_t, sin_t, n_ctx_tiles):
    B = x.shape[0]
    T = ctx.shape[1] + x.shape[1]
    tm = ROW_TILE
    n_in = w_in.shape[1]
    tok = lambda w: pl.BlockSpec((None, tm, w), lambda b, i: (b, i, 0))
    return pl.pallas_call(
        functools.partial(_ret_inproj_kernel, n_ctx_tiles=n_ctx_tiles),
        grid=(B, T // tm),
        in_specs=_stream_specs(tm, n_ctx_tiles) + [
                  pl.BlockSpec((None, None, 1, 6 * D), lambda b, i: (b, jnp.where(i < n_ctx_tiles, 0, 1), 0, 0)),
                  pl.BlockSpec((1, D), lambda b, i: (0, 0)),
                  pl.BlockSpec((D, n_in), lambda b, i: (0, 0), pipeline_mode=pl.Buffered(1)),
                  pl.BlockSpec((tm, RET_DK), lambda b, i: (i, 0)),
                  pl.BlockSpec((tm, RET_DK), lambda b, i: (i, 0))],
        out_specs=[tok(D), tok(D), tok(RET_VW), tok(RET_VW), tok(RET_VW)],
        out_shape=[jax.ShapeDtypeStruct((B, T, D), bf16), jax.ShapeDtypeStruct((B, T, D), bf16),
                   jax.ShapeDtypeStruct((B, T, RET_VW), bf16), jax.ShapeDtypeStruct((B, T, RET_VW), bf16),
                   jax.ShapeDtypeStruct((B, T, RET_VW), bf16)],
        compiler_params=_cparams(("arbitrary", "arbitrary")),
        name="ret_inproj",
    )(ctx, x, modtab, gain, w_in, cos_t, sin_t)


def _ret_chunk_index(t, nc, ncc):
    u = t - nc
    back = jnp.where(u < ncc, ncc - 1 - u, nc - 1 - u + ncc)
    return jnp.where(t < nc, t, back)


def _ret_scan_kernel(dt_ref, q_ref, k_ref, v_ref, gf_ref, gb_ref, o_ref,
                     s_ref, of_ref, mask_ref, dq_ref, dk_ref, dc_ref, *, nc, ncc):
    t = pl.program_id(1)
    C = RET_CHUNK

    def init(direction):
        s_ref[...] = jnp.zeros_like(s_ref)
        ii = lax.broadcasted_iota(i32, (C, C), 0)
        jj = lax.broadcasted_iota(i32, (C, C), 1)
        rel = (ii - jj if direction == 0 else jj - ii).astype(f32)
        pos = lax.broadcasted_iota(i32, (C, 1), 0).astype(f32)
        for hd in range(RET_H):
            r = direction * RET_H + hd
            lg = -jnp.exp(dt_ref[r:r + 1, :])
            lg1 = lg[:, 0:1]
            mask_ref[hd] = jnp.where(rel >= 0, jnp.exp(lg1 * jnp.maximum(rel, 0.0)), 0.0)
            if direction == 0:
                dq_ref[hd] = jnp.exp(lg1 * (pos + 1.0))
                dk_ref[hd] = jnp.exp(lg1 * (C - 1.0 - pos))
            else:
                dq_ref[hd] = jnp.exp(lg1 * (C - pos))
                dk_ref[hd] = jnp.exp(lg1 * pos)
            dc_ref[hd] = jnp.exp(lg * float(C))

    pl.when(t == 0)(functools.partial(init, 0))
    pl.when(t == nc)(functools.partial(init, 1))

    fwd = t < nc
    row0 = pl.multiple_of(_ret_chunk_index(t, nc, ncc) * C, C)

    for hd in range(RET_H):
        ks = slice(hd * RET_DK, (hd + 1) * RET_DK)
        vs = slice(hd * RET_DV, (hd + 1) * RET_DV)
        qh = q_ref[:, ks]
        kh = k_ref[:, ks]
        vh = v_ref[:, vs]
        p = (_dot_nt(qh, kh) * mask_ref[hd]).astype(bf16)
        y = _dot(p, vh) + _dot(qh, s_ref[hd].astype(bf16)) * dq_ref[hd]
        kd = (kh.astype(f32) * dk_ref[hd]).astype(bf16)
        upd = lax.dot_general(kd, vh, (((0,), (0,)), ((), ())), preferred_element_type=f32)
        s_ref[hd] = s_ref[hd] * dc_ref[hd][0:1, 0:1] + upd
        yn = _rms(y)

        @pl.when(fwd)
        def _():
            of_ref[pl.ds(row0, C), vs] = (gf_ref[:, vs].astype(f32) * yn).astype(bf16)

        @pl.when(jnp.logical_not(fwd))
        def _():
            o_ref[:, vs] = (of_ref[pl.ds(row0, C), vs].astype(f32) + gb_ref[:, vs].astype(f32) * yn).astype(bf16)


def _ret_scan(dtab, q, k, v, gf, gb, n_ctx):
    B, T, _ = q.shape
    C = RET_CHUNK
    nc = T // C
    ncc = n_ctx // C
    cidx = functools.partial(_ret_chunk_index, nc=nc, ncc=ncc)
    first_back = ncc - 1
    return pl.pallas_call(
        functools.partial(_ret_scan_kernel, nc=nc, ncc=ncc),
        grid=(B, 2 * nc),
        in_specs=[pl.BlockSpec((2 * RET_H, LANES), lambda b, t: (0, 0)),
                  pl.BlockSpec((None, C, D), lambda b, t: (b, cidx(t), 0)),
                  pl.BlockSpec((None, C, D), lambda b, t: (b, cidx(t), 0)),
                  pl.BlockSpec((None, C, RET_VW), lambda b, t: (b, cidx(t), 0)),
                  pl.BlockSpec((None, C, RET_VW), lambda b, t: (b, jnp.where(t < nc, t, nc - 1), 0)),
                  pl.BlockSpec((None, C, RET_VW), lambda b, t: (b, jnp.where(t < nc, first_back, cidx(t)), 0))],
        out_specs=pl.BlockSpec((None, C, RET_VW), lambda b, t: (b, jnp.where(t < nc, first_back, cidx(t)), 0)),
        out_shape=jax.ShapeDtypeStruct((B, T, RET_VW), bf16),
        scratch_shapes=[pltpu.VMEM((RET_H, RET_DK, RET_DV), f32),
                        pltpu.VMEM((T, RET_VW), bf16),
                        pltpu.VMEM((RET_H, C, C), f32),
                        pltpu.VMEM((RET_H, C, 1), f32),
                        pltpu.VMEM((RET_H, C, 1), f32),
                        pltpu.VMEM((RET_H, 1, LANES), f32)],
        compiler_params=_cparams(("arbitrary", "arbitrary")),
        name="ret_scan",
    )(dtab, q, k, v, gf, gb)


def _route(f, rwt_ref, rb_ref, cnt_ref, e_ref, w_ref, r_ref):
    tm = f.shape[0]
    G = N_EXP // N_GRP
    logits = _dot_nt(rwt_ref[...].astype(bf16), f.astype(bf16))
    s = _sigmoid(logits)
    sel = s + rb_ref[...]
    mi = lax.broadcasted_iota(i32, (G, tm), 0)
    neg = -jnp.inf
    s_g = [s[g * G:(g + 1) * G, :] for g in range(N_GRP)]
    sel_g = [sel[g * G:(g + 1) * G, :] for g in range(N_GRP)]

    def first_max(a, ids, big):
        mx = jnp.max(a, axis=0, keepdims=True)
        ix = jnp.min(jnp.where(a == mx, ids, big), axis=0, keepdims=True)
        return mx, ix

    gscore = jnp.zeros((N_GRP, tm), f32)
    gi = lax.broadcasted_iota(i32, (N_GRP, tm), 0)
    for g in range(N_GRP):
        t1, i1 = first_max(sel_g[g], mi, G)
        t2 = jnp.max(jnp.where(mi == i1, neg, sel_g[g]), axis=0, keepdims=True)
        gscore = jnp.where(gi == g, t1 + t2, gscore)
    gmask = jnp.zeros((N_GRP, tm), i32)
    cur = gscore
    for _ in range(TOPK_GRP):
        _, ix = first_max(cur, gi, N_GRP)
        hit = gi == ix
        gmask = jnp.where(hit, 1, gmask)
        cur = jnp.where(hit, neg, cur)
    cand = [jnp.where(gmask[g:g + 1, :] > 0, sel_g[g], neg) for g in range(N_GRP)]
    ids = [mi + g * G for g in range(N_GRP)]

    e_rows, w_rows = [], []
    for _ in range(TOP_K):
        mx = cand[0].max(axis=0, keepdims=True)
        for g in range(1, N_GRP):
            mx = jnp.maximum(mx, cand[g].max(axis=0, keepdims=True))
        ix = jnp.min(jnp.where(cand[0] == mx, ids[0], N_EXP), axis=0, keepdims=True)
        for g in range(1, N_GRP):
            ix = jnp.minimum(ix, jnp.min(jnp.where(cand[g] == mx, ids[g], N_EXP), axis=0, keepdims=True))
        wv = jnp.zeros((1, tm), f32)
        for g in range(N_GRP):
            hit = ids[g] == ix
            cand[g] = jnp.where(hit, neg, cand[g])
            wv = wv + jnp.sum(jnp.where(hit, s_g[g], 0.0), axis=0, keepdims=True)
        e_rows.append(ix)
        w_rows.append(wv)
    wsum = w_rows[0]
    for r in range(1, TOP_K):
        wsum = wsum + w_rows[r]

    selm = [jnp.zeros((G, tm), f32) for _ in range(N_GRP)]
    for r in range(TOP_K):
        for g in range(N_GRP):
            selm[g] = jnp.where(ids[g] == e_rows[r], 1.0, selm[g])
    m_all = jnp.concatenate(selm, axis=0)
    ri = lax.broadcasted_iota(i32, (tm, tm), 0)
    ci = lax.broadcasted_iota(i32, (tm, tm), 1)
    upper = jnp.where(ri <= ci, 1.0, 0.0).astype(bf16)
    incl = _dot(m_all.astype(bf16), upper)
    carry = cnt_ref[:, 0:1]
    rank_all = carry + incl - m_all
    cnt_ref[...] = cnt_ref[...] + incl[:, tm - 1:tm]
    for r in range(TOP_K):
        rk = jnp.zeros((1, tm), f32)
        for g in range(N_GRP):
            rk = rk + jnp.sum(jnp.where(ids[g] == e_rows[r], rank_all[g * G:(g + 1) * G, :], 0.0), axis=0, keepdims=True)
        e_ref[r:r + 1, :] = e_rows[r]
        w_ref[r:r + 1, :] = w_rows[r] / wsum * ROUTED_SCALE
        r_ref[r:r + 1, :] = rk.astype(i32)


def _post_mix_kernel(o_ref, wo_ref, *refs, nt, o_transposed, split_ctx_tiles):
    n_resid = 2 if split_ctx_tiles else 1
    resid = refs[:n_resid]
    mod_ref, g_ref, rwt_ref, rb_ref, x1_ref, fin_ref, hlin_ref, e_ref, w_ref, r_ref, cnt_ref = refs[n_resid:]
    i = pl.program_id(1)

    @pl.when(i == 0)
    def _():
        cnt_ref[...] = jnp.zeros_like(cnt_ref)

    @pl.when(i < nt)
    def _():
        mod = mod_ref[...]
        if o_transposed:
            mixed = lax.dot_general(o_ref[...], wo_ref[...], (((0,), (0,)), ((), ())), preferred_element_type=f32)
        else:
            mixed = _dot(o_ref[...], wo_ref[...])
        x = _stream_tile(resid[0], resid[1], split_ctx_tiles) if split_ctx_tiles else resid[0][...]
        x1 = x + mod[:, 2 * D:3 * D] * mixed
        x1_ref[...] = x1
        f = (_rms(x1) * g_ref[...]) * (1.0 + mod[:, 4 * D:5 * D]) + mod[:, 3 * D:4 * D]
        fin_ref[...] = f.astype(bf16)
        tm = f.shape[0]
        for j in range(D // LANES):
            hlin_ref[pl.ds(j, tm, stride=D // LANES), :] = f[:, j * LANES:(j + 1) * LANES]
        _route(f, rwt_ref, rb_ref, cnt_ref, e_ref, w_ref, r_ref)

    @pl.when(i == nt)
    def _():
        hlin_ref[...] = jnp.zeros_like(hlin_ref)


def _post_mix(o, w_o, resid, x_tile_off, modtab, n_ctx_tiles, gain, rwt, rb, o_transposed=False):
    if o_transposed:
        B, KO, N = o.shape
    else:
        B, N, KO = o.shape
    tm = ROW_TILE
    nt = N // tm
    last = lambda i: jnp.minimum(i, nt - 1)
    tok = lambda w: pl.BlockSpec((None, tm, w), lambda b, i: (b, last(i), 0))
    sel = lambda: pl.BlockSpec((None, TOP_K, tm), lambda b, i: (b, 0, last(i)))
    o_spec = (pl.BlockSpec((None, KO, tm), lambda b, i: (b, 0, last(i))) if o_transposed else tok(KO))
    split = len(resid) == 2
    resid_specs = (_stream_specs(tm, n_ctx_tiles, nt - 1) if split else
                   [pl.BlockSpec((None, tm, D), lambda b, i: (b, last(i) + x_tile_off, 0))])
    return pl.pallas_call(
        functools.partial(_post_mix_kernel, nt=nt, o_transposed=o_transposed,
                          split_ctx_tiles=n_ctx_tiles if split else 0),
        grid=(B, nt + 1),
        in_specs=[o_spec,
                  pl.BlockSpec((KO, D), lambda b, i: (0, 0))] + resid_specs + [
                  pl.BlockSpec((None, None, 1, 6 * D), lambda b, i: (b, jnp.where(last(i) < n_ctx_tiles, 0, 1), 0, 0)),
                  pl.BlockSpec((1, D), lambda b, i: (0, 0)),
                  pl.BlockSpec((N_EXP, D), lambda b, i: (0, 0)),
                  pl.BlockSpec((N_EXP, 1), lambda b, i: (0, 0))],
        out_specs=[tok(D), tok(D),
                   pl.BlockSpec((None, tm * (D // LANES), LANES), lambda b, i: (b, i, 0)),
                   sel(), sel(), sel(),
                   pl.BlockSpec((None, N_EXP, LANES), lambda b, i: (b, 0, 0))],
        out_shape=[jax.ShapeDtypeStruct((B, N, D), f32), jax.ShapeDtypeStruct((B, N, D), bf16),
                   jax.ShapeDtypeStruct((B, (N + tm) * (D // LANES), LANES), f32),
                   jax.ShapeDtypeStruct((B, TOP_K, N), i32), jax.ShapeDtypeStruct((B, TOP_K, N), f32),
                   jax.ShapeDtypeStruct((B, TOP_K, N), i32),
                   jax.ShapeDtypeStruct((B, N_EXP, LANES), f32)],
        compiler_params=_cparams(("arbitrary", "arbitrary")),
        name="post_mix",
    )(o, w_o, *resid, modtab, gain, rwt, rb)


META_W = 256
PLAN_ALIGN = 1024


def _round_up(n, m):
    return -(-n // m) * m


def _moe_sizes(n_tok):
    tm = MOE_TILE
    nt_max = (n_tok * TOP_K + N_EXP * (tm - 1)) // tm + 1
    ntp = _round_up(nt_max + 1, PLAN_ALIGN // tm)
    assert ntp <= META_W
    return ntp, _round_up(n_tok, PLAN_ALIGN)


def _plan_kernel(e_ref, r_ref, cnt_ref, pos_ref, meta_ref, *, n_tok):
    tm = MOE_TILE
    ntile = jnp.floor((cnt_ref[...] + (tm - 1.0)) * (1.0 / tm))
    ntb = ntile.astype(bf16)
    ei = lax.broadcasted_iota(i32, (N_EXP, LANES), 0)
    ej = lax.broadcasted_iota(i32, (N_EXP, LANES), 1)
    lower = jnp.where(ej <= ei, 1.0, 0.0)[:, :N_EXP].astype(bf16)
    tend = _dot(lower, ntb)
    tstart = tend - ntile
    tt = lax.broadcasted_iota(i32, (N_EXP, META_W), 1).astype(f32)
    te = jnp.sum(jnp.where(tt >= tend[:, 0:1], 1.0, 0.0), axis=0, keepdims=True)
    meta_ref[...] = jnp.zeros_like(meta_ref)
    meta_ref[0:1, :] = jnp.minimum(te, N_EXP - 1.0).astype(i32)
    meta_ref[1:2, :] = jnp.broadcast_to(tend[N_EXP - 1:N_EXP, 0:1], (1, META_W)).astype(i32)
    e = e_ref[...]
    base = jnp.zeros(e.shape, f32)
    for ex in range(N_EXP):
        base = jnp.where(e == ex, tstart[ex:ex + 1, 0:1] * float(tm), base)
    pos_ref[...] = jnp.zeros_like(pos_ref)
    pos_ref[:, 0:n_tok] = base.astype(i32) + r_ref[...]


def _plan(e_t, r_t, cnt):
    B, K, N = e_t.shape
    _, npad = _moe_sizes(N)
    return pl.pallas_call(
        functools.partial(_plan_kernel, n_tok=N),
        grid=(B,),
        in_specs=[pl.BlockSpec((None, K, N), lambda b: (b, 0, 0)),
                  pl.BlockSpec((None, K, N), lambda b: (b, 0, 0)),
                  pl.BlockSpec((None, N_EXP, LANES), lambda b: (b, 0, 0))],
        out_specs=[pl.BlockSpec((None, K, npad), lambda b: (b, 0, 0)),
                   pl.BlockSpec((None, SUBLANES, META_W), lambda b: (b, 0, 0))],
        out_shape=[jax.ShapeDtypeStruct((B, K, npad), i32), jax.ShapeDtypeStruct((B, SUBLANES, META_W), i32)],
        compiler_params=_cparams(("arbitrary",)),
        name="moe_plan",
    )(e_t, r_t, cnt)


def _plan_invert(pos, w_t, n_tok):
    B, K, npad = pos.shape
    ntp, _ = _moe_sizes(n_tok)
    plen = ntp * MOE_TILE
    nch = D // LANES
    mesh = plsc.VectorSubcoreMesh(core_axis_name="c", subcore_axis_name="s")
    n_cores = mesh.num_cores
    assert 2 * B <= n_cores * mesh.num_subcores and n_tok % SC_LANES == 0 and plen % SC_LANES == 0

    @functools.partial(
        pl.kernel, mesh=mesh,
        out_type=[jax.ShapeDtypeStruct((B * plen,), i32), jax.ShapeDtypeStruct((B * plen,), f32)],
        scratch_types=[pltpu.VMEM((npad,), i32), pltpu.VMEM((n_tok,), f32),
                       pltpu.VMEM((plen,), i32), pltpu.VMEM((plen,), f32)],
        compiler_params=dataclasses.replace(pltpu.CompilerParams(), needs_layout_passes=False))
    def invert(pos_hbm, w_hbm, rows_hbm, ws_hbm, pos_c, w_c, rows_v, ws_v):
        wid = lax.axis_index("s") * n_cores + lax.axis_index("c")
        b = wid % B
        lane = lax.iota(i32, SC_LANES)

        def load_pos(k):
            pltpu.sync_copy(pos_hbm.at[pl.ds(pl.multiple_of((b * K + k) * npad, SUBLANES), npad)], pos_c)

        @pl.when(wid < B)
        def _():
            pad = jnp.full((SC_LANES,), n_tok * nch, i32)

            @pl.loop(0, plen, step=SC_LANES)
            def _(i):
                rows_v[pl.ds(i, SC_LANES)] = pad

            for k in range(K):
                load_pos(k)

                @pl.loop(0, n_tok, step=SC_LANES)
                def _(n):
                    plsc.store_scatter(rows_v, [pos_c[pl.ds(n, SC_LANES)]], (lane + n) * nch)

            pltpu.sync_copy(rows_v, rows_hbm.at[pl.ds(pl.multiple_of(b * plen, SUBLANES), plen)])

        @pl.when(jnp.logical_and(wid >= B, wid < 2 * B))
        def _():
            zero = jnp.zeros((SC_LANES,), f32)

            @pl.loop(0, plen, step=SC_LANES)
            def _(i):
                ws_v[pl.ds(i, SC_LANES)] = zero

            for k in range(K):
                load_pos(k)
                pltpu.sync_copy(w_hbm.at[pl.ds(pl.multiple_of((b * K + k) * n_tok, SUBLANES), n_tok)], w_c)

                @pl.loop(0, n_tok, step=SC_LANES)
                def _(n):
                    plsc.store_scatter(ws_v, [pos_c[pl.ds(n, SC_LANES)]], w_c[pl.ds(n, SC_LANES)])

            pltpu.sync_copy(ws_v, ws_hbm.at[pl.ds(pl.multiple_of(b * plen, SUBLANES), plen)])

    return invert(pos.reshape(-1), w_t.reshape(-1))


def _moe_kernel(te_ref, nt_ref, *refs, ntp, n_tok):
    R = MOE_GROUP
    rg_refs, rs_refs, ws_refs = refs[0:R], refs[R:2 * R], refs[2 * R:3 * R]
    hlin_ref = refs[3 * R]
    wgu_refs, wd_refs = refs[3 * R + 1:4 * R + 1], refs[4 * R + 1:5 * R + 1]
    out_ref, acc_ref = refs[5 * R + 1:5 * R + 3]
    bufs = refs[5 * R + 3:]
    xs_refs = [bufs[0:R], bufs[R:2 * R]]
    ylin_refs = [bufs[2 * R:3 * R], bufs[3 * R:4 * R]]
    b = pl.program_id(0)
    t = pl.program_id(1)
    TM = MOE_TILE
    NCH = D // LANES
    U = 8
    TMP = TM + SUBLANES
    ngrp = ntp // R

    @pl.when(t == 0)
    def _():
        acc_ref[...] = jnp.zeros_like(acc_ref)
        for buf in bufs:
            buf[...] = jnp.zeros_like(buf)

    def gather_rows(p, r, c):
        for m in range(c * U, (c + 1) * U):
            off = pl.multiple_of(rg_refs[r][m], NCH)
            xs_refs[p][r][pl.ds(m, NCH, stride=TMP), :] = hlin_ref[pl.ds(off, NCH), :]

    def scatter_rows(p, r, c):
        offs = [pl.multiple_of(rs_refs[r][c * U + u], NCH) for u in range(U)]
        news = [acc_ref[pl.ds(offs[u], NCH), :] + ylin_refs[p][r][pl.ds((c * U + u) * NCH, NCH), :]
                for u in range(U)]
        for u in range(U):
            acc_ref[pl.ds(offs[u], NCH), :] = news[u]

    def stage(p):
        row_work = [functools.partial(fn, p, r, c) for c in range(TM // U) for r in range(R)
                    for fn in (gather_rows, scatter_rows)]
        n_pieces = R * (EXP_FF // LANES + D // MXU_N)
        per_piece = -(-len(row_work) // n_pieces)

        def deal():
            for fn in row_work[:per_piece]:
                fn()
            del row_work[:per_piece]

        for r in range(R):
            x = jnp.concatenate([xs_refs[1 - p][r][pl.ds(j * TMP, TM), :] for j in range(NCH)], axis=1).astype(bf16)
            wcol = jnp.broadcast_to(ws_refs[r][...], (SUBLANES, TM)).T[:, 0:1]
            gate = _dot(x, wgu_refs[r][:, :EXP_FF])
            deal()
            up = _dot(x, wgu_refs[r][:, EXP_FF:])
            deal()
            a = (_silu(gate) * up * wcol).astype(bf16)
            for c in range(D // MXU_N):
                y = _dot(a, wd_refs[r][:, c * MXU_N:(c + 1) * MXU_N])
                for jj in range(MXU_N // LANES):
                    j = c * (MXU_N // LANES) + jj
                    ylin_refs[1 - p][r][pl.ds(j, TM, stride=NCH), :] = y[:, jj * LANES:(jj + 1) * LANES]
                deal()
        while row_work:
            deal()

    live = (t - 2) * R < nt_ref[b]
    pl.when(jnp.logical_and(live, t % 2 == 0))(functools.partial(stage, 0))
    pl.when(jnp.logical_and(live, t % 2 == 1))(functools.partial(stage, 1))

    @pl.when(t >= ngrp + 2)
    def _():
        row0 = (t - (ngrp + 2)) * (ROW_TILE * NCH)
        for j in range(NCH):
            out_ref[:, j * LANES:(j + 1) * LANES] = acc_ref[pl.ds(row0 + j, ROW_TILE, stride=NCH), :]


def _moe(te, nt, rows, wsort, hlin, w_gu, w_d, layer):
    B = hlin.shape[0]
    NCH = D // LANES
    n_tok = hlin.shape[1] // NCH - ROW_TILE
    assert n_tok % ROW_TILE == 0
    nf = n_tok // ROW_TILE
    TM = MOE_TILE
    R = MOE_GROUP
    ntp, _ = _moe_sizes(n_tok)
    assert ntp % R == 0
    ngrp = ntp // R
    pad_tile = ntp - 1

    def tile_of(b, t, nt_ref, r, lag):
        tile = (t - lag) * R + r
        ok = jnp.logical_and(t >= lag, tile < nt_ref[b])
        return b * ntp + jnp.where(ok, tile, pad_tile)

    def rows_spec(r, lag):
        return pl.BlockSpec((TM,), lambda b, t, te_ref, nt_ref: (tile_of(b, t, nt_ref, r, lag),),
                            memory_space=pltpu.SMEM)

    def ws_spec(r):
        return pl.BlockSpec((None, 1, TM), lambda b, t, te_ref, nt_ref: (tile_of(b, t, nt_ref, r, 1), 0, 0))

    def w_spec(shape, r):
        def index(b, t, te_ref, nt_ref):
            tile = jnp.clip((t - 1) * R + r, 0, nt_ref[b] - 1)
            return (layer, te_ref[b * META_W + tile], 0, 0)
        return pl.BlockSpec((None, None) + shape, index)

    grid_spec = pltpu.PrefetchScalarGridSpec(
        num_scalar_prefetch=2,
        grid=(B, ngrp + 2 + nf),
        in_specs=([rows_spec(r, 0) for r in range(R)] + [rows_spec(r, 2) for r in range(R)]
                  + [ws_spec(r) for r in range(R)]
                  + [pl.BlockSpec((None, (n_tok + ROW_TILE) * NCH, LANES), lambda b, t, *_: (b, 0, 0),
                                  pipeline_mode=pl.Buffered(1))]
                  + [w_spec((D, 2 * EXP_FF), r) for r in range(R)]
                  + [w_spec((EXP_FF, D), r) for r in range(R)]),
        out_specs=pl.BlockSpec((None, ROW_TILE, D), lambda b, t, *_: (b, jnp.maximum(t - (ngrp + 2), 0), 0)),
        scratch_shapes=([pltpu.VMEM(((n_tok + SUBLANES) * NCH, LANES), f32)]
                        + [pltpu.VMEM(((TM + SUBLANES) * NCH, LANES), f32) for _ in range(2 * R)]
                        + [pltpu.VMEM((TM * NCH, LANES), f32) for _ in range(2 * R)]),
    )
    return pl.pallas_call(
        functools.partial(_moe_kernel, ntp=ntp, n_tok=n_tok),
        grid_spec=grid_spec,
        out_shape=jax.ShapeDtypeStruct((B, n_tok, D), f32),
        compiler_params=_cparams(("arbitrary", "arbitrary"), MOE_VMEM_LIMIT),
        name="moe",
    )(te, nt, *([rows] * (2 * R)), *([wsort.reshape(B * ntp, 1, TM)] * R), hlin, *([w_gu] * R), *([w_d] * R))


def _routed_experts(e_t, w_t, r_t, cnt, hlin, w_gu, w_d, layer):
    N = e_t.shape[2]
    pos, meta = _plan(e_t, r_t, cnt)
    te = meta[:, 0, :].reshape(-1)
    nt = meta[:, 1, 0]
    rows, wsort = _plan_invert(pos, w_t, N)
    return _moe(te, nt, rows, wsort, hlin, w_gu, w_d, layer)


def _shared_ffn(fin, shgu_ref, shd_ref):
    gu = _dot(fin, shgu_ref[...])
    return _dot((_silu(gu[:, :SH_FF]) * gu[:, SH_FF:]).astype(bf16), shd_ref[...])


def _post_ffn_mla_kernel(x1_ref, routed_ref, fin_ref, shgu_ref, shd_ref, mod0_ref, mod1_ref, g_ref, win_ref,
                         x2_ref, a_ref):
    x2 = x1_ref[...] + mod0_ref[...][:, 5 * D:6 * D] * (routed_ref[...] + _shared_ffn(fin_ref[...], shgu_ref, shd_ref))
    x2_ref[...] = x2
    mod1 = mod1_ref[...]
    h = (_rms(x2) * g_ref[...]) * (1.0 + mod1[:, D:2 * D]) + mod1[:, 0:D]
    a_ref[...] = _dot(h.astype(bf16), win_ref[...])


def _post_ffn_mla(x1, routed, fin, sh_gu, sh_d, modtab0, modtab1, gain, w_in, n_ctx_tiles):
    B, T, _ = x1.shape
    tm = ROW_TILE
    tok = lambda w: pl.BlockSpec((None, tm, w), lambda b, i: (b, i, 0))
    modspec = lambda: pl.BlockSpec((None, None, 1, 6 * D), lambda b, i: (b, jnp.where(i < n_ctx_tiles, 0, 1), 0, 0))
    full = lambda r, c: pl.BlockSpec((r, c), lambda b, i: (0, 0))
    return pl.pallas_call(
        _post_ffn_mla_kernel,
        grid=(B, T // tm),
        in_specs=[tok(D), tok(D), tok(D), full(D, 2 * SH_FF), full(SH_FF, D), modspec(), modspec(),
                  full(1, D), full(D, MLA_IN_PAD)],
        out_specs=[tok(D), tok(MLA_IN_PAD)],
        out_shape=[jax.ShapeDtypeStruct((B, T, D), f32), jax.ShapeDtypeStruct((B, T, MLA_IN_PAD), f32)],
        compiler_params=_cparams(("arbitrary", "arbitrary")),
        name="post_ffn_mla",
    )(x1, routed, fin, sh_gu, sh_d, modtab0, modtab1, gain, w_in)


def _post_ffn_final_kernel(x1_ref, routed_ref, fin_ref, shgu_ref, shd_ref, mod_ref, out_ref):
    out_ref[...] = x1_ref[...] + mod_ref[...][:, 5 * D:6 * D] * (
        routed_ref[...] + _shared_ffn(fin_ref[...], shgu_ref, shd_ref))


def _post_ffn_final(x1, routed, fin, sh_gu, sh_d, modtab):
    B, N, _ = x1.shape
    tm = ROW_TILE
    tok = lambda w: pl.BlockSpec((None, tm, w), lambda b, i: (b, i, 0))
    full = lambda r, c: pl.BlockSpec((r, c), lambda b, i: (0, 0))
    return pl.pallas_call(
        _post_ffn_final_kernel,
        grid=(B, N // tm),
        in_specs=[tok(D), tok(D), tok(D), full(D, 2 * SH_FF), full(SH_FF, D),
                  pl.BlockSpec((None, None, 1, 6 * D), lambda b, i: (b, 1, 0, 0))],
        out_specs=tok(D),
        out_shape=jax.ShapeDtypeStruct((B, N, D), f32),
        compiler_params=_cparams(("arbitrary", "arbitrary")),
        name="post_ffn_final",
    )(x1, routed, fin, sh_gu, sh_d, modtab)


def _mla_prep_kernel(a_ref, qan_ref, wqn_ref, wqr_ref, kvan_ref, wk_ref, wv_ref, qnn_ref, qnr_ref, knn_ref, knr_ref,
                     cos_ref, sin_ref, q_ref, k_ref, v_ref):
    a = a_ref[...]
    tm = a.shape[0]
    scale = MLA_QK ** -0.5 * float(np.log2(np.e))
    cos = cos_ref[...]
    sin = sin_ref[...]
    lane = lax.broadcasted_iota(i32, (tm, LANES), 1)
    first = (lane // (MLA_ROPE // 4)) % 2 == 0

    def rope(xb):
        sw = jnp.where(first, pltpu.roll(xb, LANES - MLA_ROPE // 4, axis=1), pltpu.roll(xb, MLA_ROPE // 4, axis=1))
        return xb * cos + sw * sin

    qa = (_rms(a[:, :MLA_QR]) * qan_ref[...]).astype(bf16)
    qn = _dot(qa, wqn_ref[...])
    qr = _dot(qa, wqr_ref[...])
    ri = lax.broadcasted_iota(i32, (LANES, LANES), 0) // MLA_ROPE
    ci = lax.broadcasted_iota(i32, (LANES, LANES), 1) // MLA_ROPE
    seg = jnp.where(ri == ci, 1.0, 0.0).astype(bf16)

    def seg_sum(sq):
        hi = sq.astype(bf16)
        r1 = sq - hi.astype(f32)
        mid = r1.astype(bf16)
        lo = (r1 - mid.astype(f32)).astype(bf16)
        return _dot(hi, seg) + _dot(mid, seg) + _dot(lo, seg)

    qr_blocks = []
    for p in range(MLA_H // 2):
        blk = qr[:, p * LANES:(p + 1) * LANES]
        blk = blk * lax.rsqrt(seg_sum(blk * blk) * (1.0 / MLA_ROPE) + EPS) * qnr_ref[:, p * LANES:(p + 1) * LANES]
        qr_blocks.append(rope(blk) * scale)

    kv = (_rms(a[:, MLA_QR:MLA_QR + MLA_KVR]) * kvan_ref[...]).astype(bf16)
    kn = _dot(kv, wk_ref[...])
    v_ref[...] = _dot_nt(wv_ref[...], kv).astype(bf16)
    kr = a[:, MLA_QR + MLA_KVR:MLA_IN_PAD]
    kr = rope(_rms(kr, MLA_ROPE) * knr_ref[...])
    kr_odd = pltpu.roll(kr, MLA_ROPE, axis=1)
    for hd in range(MLA_H):
        sl = slice(hd * MLA_NOPE, (hd + 1) * MLA_NOPE)
        q_ref[:, 2 * hd * LANES:(2 * hd + 1) * LANES] = (_rms(qn[:, sl]) * qnn_ref[...] * scale).astype(bf16)
        q_ref[:, (2 * hd + 1) * LANES:(2 * hd + 2) * LANES] = qr_blocks[hd // 2].astype(bf16)
        k_ref[:, 2 * hd * LANES:(2 * hd + 1) * LANES] = (_rms(kn[:, sl]) * knn_ref[...]).astype(bf16)
        k_ref[:, (2 * hd + 1) * LANES:(2 * hd + 2) * LANES] = (kr if hd % 2 == 0 else kr_odd).astype(bf16)


def _mla_prep(a, qan, wqn, wqr, kvan, wk, wv, qnn, qnr, knn, knr, cos_t, sin_t, n_ctx_tiles):
    B, T, _ = a.shape
    tm = ROW_TILE
    tok = lambda w: pl.BlockSpec((None, tm, w), lambda b, i: (b, i, 0))
    full = lambda r, c: pl.BlockSpec((r, c), lambda b, i: (0, 0))
    hw = 2 * LANES * MLA_H
    q_spec = pl.BlockSpec((None, tm, hw), lambda b, i: (b, jnp.maximum(i - n_ctx_tiles, 0), 0))
    return pl.pallas_call(
        _mla_prep_kernel,
        grid=(B, T // tm),
        in_specs=[tok(MLA_IN_PAD), full(1, MLA_QR), full(MLA_QR, MLA_H * MLA_NOPE), full(MLA_QR, MLA_H * MLA_ROPE),
                  full(1, MLA_KVR), full(MLA_KVR, MLA_H * MLA_NOPE), full(MLA_H * MLA_V, MLA_KVR),
                  full(1, MLA_NOPE), full(1, MLA_H * MLA_ROPE), full(1, MLA_NOPE), full(1, LANES),
                  pl.BlockSpec((tm, LANES), lambda b, i: (i, 0)), pl.BlockSpec((tm, LANES), lambda b, i: (i, 0))],
        out_specs=[q_spec, tok(hw), pl.BlockSpec((None, MLA_H * MLA_V, tm), lambda b, i: (b, 0, i))],
        out_shape=[jax.ShapeDtypeStruct((B, T - n_ctx_tiles * tm, hw), bf16), jax.ShapeDtypeStruct((B, T, hw), bf16),
                   jax.ShapeDtypeStruct((B, MLA_H * MLA_V, T), bf16)],
        compiler_params=_cparams(("arbitrary", "arbitrary")),
        name="mla_prep",
    )(a, qan, wqn, wqr, kvan, wk, wv, qnn, qnr, knn, knr, cos_t, sin_t)


ATT_KV_CHUNK = 256
ATT_TQ = 256


def _mla_attn_kernel(q_ref, k_ref, vt_ref, o_ref, s0_ref, s1_ref, m0_ref, m1_ref):
    tq = q_ref.shape[0]
    ck = ATT_KV_CHUNK
    nck = k_ref.shape[0] // ck
    i = pl.program_id(0)

    s_refs, m_refs = (s0_ref, s1_ref), (m0_ref, m1_ref)

    @pl.when(i == 0)
    def _():
        for ref in s_refs + m_refs:
            ref[...] = jnp.zeros_like(ref)

    def fold(a, op):
        return op(a.reshape(ck // SUBLANES, SUBLANES, tq), axis=0)

    def stage(par):
        q = q_ref[...]
        mpart = jnp.full((SUBLANES, tq), -jnp.inf, f32)
        for j in range(nck):
            s = _dot_nt(k_ref[j * ck:(j + 1) * ck, :], q)
            s_refs[par][j * ck:(j + 1) * ck, :] = s
            mpart = jnp.maximum(mpart, fold(s, jnp.max))
        m_refs[par][...] = mpart
        m = jnp.max(m_refs[1 - par][...], axis=0, keepdims=True)
        lpart = jnp.zeros((SUBLANES, tq), f32)
        acc = jnp.zeros((MLA_V, tq), f32)
        for j in range(nck):
            p = jnp.exp2(s_refs[1 - par][j * ck:(j + 1) * ck, :] - m)
            lpart = lpart + fold(p, jnp.sum)
            acc = acc + _dot(vt_ref[:, j * ck:(j + 1) * ck], p.astype(bf16))
        o_ref[...] = (acc / jnp.sum(lpart, axis=0, keepdims=True)).astype(bf16)

    pl.when(i % 2 == 0)(functools.partial(stage, 0))
    pl.when(i % 2 == 1)(functools.partial(stage, 1))


def _mla_attn(q, k, vt):
    B, S, _ = q.shape
    T = k.shape[1]
    tq = ATT_TQ if S % ATT_TQ == 0 else ROW_TILE
    nq = S // tq
    ntile = B * MLA_H * nq
    assert T % ATT_KV_CHUNK == 0

    def tile(g):
        g = jnp.clip(g, 0, ntile - 1)
        return g // (MLA_H * nq), (g // nq) % MLA_H, g % nq

    def q_index(g):
        b, h, i = tile(g)
        return b, i, h

    def k_index(g):
        b, h, _ = tile(g)
        return b, 0, h

    def vt_index(g):
        b, h, _ = tile(g - 1)
        return b, h, 0

    def o_index(g):
        b, h, i = tile(g - 1)
        return b, h, i

    return pl.pallas_call(
        _mla_attn_kernel,
        grid=(ntile + 1,),
        in_specs=[pl.BlockSpec((None, tq, 2 * LANES), q_index),
                  pl.BlockSpec((None, T, 2 * LANES), k_index),
                  pl.BlockSpec((None, MLA_V, T), vt_index)],
        out_specs=pl.BlockSpec((None, MLA_V, tq), o_index),
        out_shape=jax.ShapeDtypeStruct((B, MLA_H * MLA_V, S), bf16),
        scratch_shapes=[pltpu.VMEM((T, tq), f32), pltpu.VMEM((T, tq), f32),
                        pltpu.VMEM((SUBLANES, tq), f32), pltpu.VMEM((SUBLANES, tq), f32)],
        compiler_params=_cparams(("arbitrary",)),
        name="mla_attn",
    )(q, k, vt)


def _axial_angles(rows_n, rot_dim):
    axis_dim = rot_dim // 2
    inv = ROPE_BASE ** (-jnp.arange(0, axis_dim, 2, dtype=f32) / axis_dim)
    row = jnp.repeat(jnp.arange(rows_n, dtype=f32), GRID_W)
    col = jnp.tile(jnp.arange(GRID_W, dtype=f32), rows_n)
    return row[:, None] * inv, col[:, None] * inv


def _rope_tables(seq, n_ctx, rot_dim, reps):
    ang_r, ang_c = _axial_angles(seq // GRID_W, rot_dim)
    cos = jnp.concatenate([jnp.cos(ang_r)] * 2 + [jnp.cos(ang_c)] * 2, axis=1)
    sin = jnp.concatenate([-jnp.sin(ang_r), jnp.sin(ang_r), -jnp.sin(ang_c), jnp.sin(ang_c)], axis=1)
    cos = jnp.concatenate([jnp.ones((n_ctx, rot_dim), f32), cos], axis=0)
    sin = jnp.concatenate([jnp.zeros((n_ctx, rot_dim), f32), sin], axis=0)
    return jnp.tile(cos, (1, reps)), jnp.tile(sin, (1, reps))


def kernel(x, c, ctx, c_ctx, ada_w, ada_b, norm_mix, norm_ffn, ret_w_in, ret_decay_f, ret_decay_b, ret_w_o,
           mla_w_in, mla_q_a_norm, mla_w_q_b, mla_kv_a_norm, mla_w_kv_b, mla_q_norm, mla_k_norm, mla_w_o,
           router_w, router_bias, exp_w_gu, exp_w_down, sh_w_gu, sh_w_down):
    B, S, _ = x.shape
    n_ctx = ctx.shape[1]
    assert n_ctx % ROW_TILE == 0 and S % ROW_TILE == 0 and S % GRID_W == 0
    n_ctx_tiles = n_ctx // ROW_TILE

    rows = -(-(B + 1) // SUBLANES) * SUBLANES
    cc = jnp.zeros((rows, D), f32).at[:B].set(c).at[B].set(c_ctx)
    mod = _ada(cc, ada_w, ada_b)

    def modtab(i):
        ctx_row = jnp.broadcast_to(mod[i, B][None, :], (B, 6 * D))
        return jnp.stack([ctx_row, mod[i, :B]], axis=1)[:, :, None, :]

    mod0, mod1 = modtab(0), modtab(1)

    cos_r, sin_r = _rope_tables(S, n_ctx, RET_DK, 1)
    q, k, v, gf, gb = _ret_inproj(ctx, x, mod0, norm_mix[0][None, :], ret_w_in[0].astype(bf16), cos_r, sin_r,
                                  n_ctx_tiles)
    dtab = jnp.broadcast_to(jnp.concatenate([ret_decay_f[0], ret_decay_b[0]])[:, None], (2 * RET_H, LANES))
    o = _ret_scan(dtab, q, k, v, gf, gb, n_ctx)
    x1, fin, hlin, e_t, w_t, r_t, cnt = _post_mix(
        o, ret_w_o[0].astype(bf16), (ctx, x), 0, mod0, n_ctx_tiles, norm_ffn[0][None, :],
        router_w[0].T, router_bias[0][:, None])
    exp_gu, exp_d = exp_w_gu.astype(bf16), exp_w_down.astype(bf16)
    routed = _routed_experts(e_t, w_t, r_t, cnt, hlin, exp_gu, exp_d, 0)
    w_in1 = jnp.zeros((D, MLA_IN_PAD), f32).at[:, :mla_w_in.shape[2]].set(mla_w_in[0]).astype(bf16)
    x2, a = _post_ffn_mla(x1, routed, fin, sh_w_gu[0].astype(bf16), sh_w_down[0].astype(bf16), mod0, mod1,
                          norm_mix[1][None, :], w_in1, n_ctx_tiles)

    wq = mla_w_q_b[0].reshape(MLA_QR, MLA_H, MLA_QK)
    wqn = wq[:, :, :MLA_NOPE].reshape(MLA_QR, MLA_H * MLA_NOPE).astype(bf16)
    wqr = wq[:, :, MLA_NOPE:].reshape(MLA_QR, MLA_H * MLA_ROPE).astype(bf16)
    wkv = mla_w_kv_b[0].reshape(MLA_KVR, MLA_H, MLA_NOPE + MLA_V)
    wk = wkv[:, :, :MLA_NOPE].reshape(MLA_KVR, MLA_H * MLA_NOPE).astype(bf16)
    wv = wkv[:, :, MLA_NOPE:].reshape(MLA_KVR, MLA_H * MLA_V).T.astype(bf16)
    qnn = mla_q_norm[0][None, :MLA_NOPE]
    qnr = jnp.tile(mla_q_norm[0][None, MLA_NOPE:], (1, MLA_H))
    knn = mla_k_norm[0][None, :MLA_NOPE]
    knr = jnp.concatenate([mla_k_norm[0][MLA_NOPE:], jnp.zeros((LANES - MLA_ROPE,), f32)])[None, :]
    cos_m, sin_m = _rope_tables(S, n_ctx, MLA_ROPE, LANES // MLA_ROPE)
    qf, kf, vf = _mla_prep(a, mla_q_a_norm[0][None, :], wqn, wqr, mla_kv_a_norm[0][None, :], wk, wv,
                           qnn, qnr, knn, knr, cos_m, sin_m, n_ctx_tiles)
    o1 = _mla_attn(qf, kf, vf)
    x3, fin1, hlin1, e1, w1, r1, cnt1 = _post_mix(
        o1, mla_w_o[0].astype(bf16), (x2,), n_ctx_tiles, mod1, 0, norm_ffn[1][None, :],
        router_w[1].T, router_bias[1][:, None], o_transposed=True)
    routed1 = _routed_experts(e1, w1, r1, cnt1, hlin1, exp_gu, exp_d, 1)
    return _post_ffn_final(x3, routed1, fin1, sh_w_gu[1].astype(bf16), sh_w_down[1].astype(bf16), mod1)
```

```python
import dataclasses
import functools

import jax
import jax.numpy as jnp
import numpy as np
from jax import lax
from jax.experimental import pallas as pl
from jax.experimental.pallas import tpu as pltpu
from jax.experimental.pallas import tpu_sc as plsc

f32 = jnp.float32
bf16 = jnp.bfloat16
i32 = jnp.int32

D = 1024
GRID_W = 64
EPS = 1e-6
ROPE_BASE = 10000.0
RET_H = 4
RET_DK = 256
RET_DV = 512
RET_VW = RET_H * RET_DV
RET_CHUNK = 256
MLA_H = 8
MLA_NOPE = 128
MLA_ROPE = 64
MLA_QK = MLA_NOPE + MLA_ROPE
MLA_V = 128
MLA_QR = 384
MLA_KVR = 256
MLA_IN_PAD = 768
N_EXP = 64
TOP_K = 8
N_GRP = 8
TOPK_GRP = 4
EXP_FF = 256
SH_FF = 256
ROUTED_SCALE = 2.5

LANES = 128
SUBLANES = 8
SC_LANES = 16
MXU_N = 256
ROW_TILE = 256
MOE_TILE = 256
MOE_GROUP = 2
VMEM_LIMIT = 56 * 1024 * 1024
MOE_VMEM_LIMIT = 62 * 1024 * 1024


def _cparams(sem, vmem=VMEM_LIMIT):
    return pltpu.CompilerParams(dimension_semantics=sem, vmem_limit_bytes=vmem)


def _sigmoid(x):
    return 1.0 / (1.0 + jnp.exp(-x))


def _silu(x):
    return x * _sigmoid(x)


def _rms(x, n=None):
    n = x.shape[-1] if n is None else n
    return x * lax.rsqrt(jnp.sum(x * x, axis=-1, keepdims=True) * (1.0 / n) + EPS)


def _dot(a, b):
    return jnp.dot(a, b, preferred_element_type=f32)


def _dot_nt(a, b, precision=None):
    return lax.dot_general(a, b, (((1,), (1,)), ((), ())), preferred_element_type=f32, precision=precision)


def _ada_kernel(c_ref, w_ref, b_ref, o_ref):
    s = _silu(c_ref[...]).astype(bf16)
    o_ref[...] = _dot(s, w_ref[...].astype(bf16)) + b_ref[...]


def _ada(cc, ada_w, ada_b):
    depth = ada_w.shape[0]
    rows = cc.shape[0]
    tn = 1536
    return pl.pallas_call(
        _ada_kernel,
        grid=(depth, 6 * D // tn),
        in_specs=[pl.BlockSpec((rows, D), lambda i, j: (0, 0)),
                  pl.BlockSpec((None, D, tn), lambda i, j: (i, 0, j)),
                  pl.BlockSpec((None, 1, tn), lambda i, j: (i, 0, j))],
        out_specs=pl.BlockSpec((None, rows, tn), lambda i, j: (i, 0, j)),
        out_shape=jax.ShapeDtypeStruct((depth, rows, 6 * D), f32),
        compiler_params=_cparams(("arbitrary", "arbitrary")),
        name="ada",
    )(cc, ada_w, ada_b.reshape(depth, 1, 6 * D))


def _stream_tile(ctx_ref, x_ref, n_ctx_tiles):
    return jnp.where(pl.program_id(1) < n_ctx_tiles, ctx_ref[...], x_ref[...])


def _stream_specs(tm, n_ctx_tiles, last=None):
    clamp = (lambda i: i) if last is None else (lambda i: jnp.minimum(i, last))
    return [pl.BlockSpec((None, tm, D), lambda b, i: (b, jnp.minimum(clamp(i), n_ctx_tiles - 1), 0)),
            pl.BlockSpec((None, tm, D), lambda b, i: (b, jnp.maximum(clamp(i) - n_ctx_tiles, 0), 0))]


def _ret_inproj_kernel(ctx_ref, x_ref, mod_ref, g_ref, w_ref, cos_ref, sin_ref, q_ref, k_ref, v_ref, gf_ref, gb_ref,
                       *, n_ctx_tiles):
    x = _stream_tile(ctx_ref, x_ref, n_ctx_tiles)
    mod = mod_ref[...]
    h = (_rms(x) * g_ref[...]) * (1.0 + mod[:, D:2 * D]) + mod[:, 0:D]
    hb = h.astype(bf16)
    cos = cos_ref[...]
    sin = sin_ref[...]

    def rope(a):
        outs = []
        for half in range(2):
            sl = slice(half * LANES, (half + 1) * LANES)
            ah = a[:, sl]
            outs.append(ah * cos[:, sl] + pltpu.roll(ah, LANES // 2, axis=1) * sin[:, sl])
        return jnp.concatenate(outs, axis=1)

    for hd in range(RET_H):
        sl = slice(hd * RET_DK, (hd + 1) * RET_DK)
        q_ref[:, sl] = rope(_dot(hb, w_ref[:, sl])).astype(bf16)
    for hd in range(RET_H):
        sl = slice(hd * RET_DK, (hd + 1) * RET_DK)
        wsl = slice(D + hd * RET_DK, D + (hd + 1) * RET_DK)
        k_ref[:, sl] = (rope(_dot(hb, w_ref[:, wsl])) * (RET_DK ** -0.5)).astype(bf16)
    cw = 512
    for c in range(RET_VW // cw):
        sl = slice(c * cw, (c + 1) * cw)
        v_ref[:, sl] = _dot(hb, w_ref[:, 2 * D + c * cw:2 * D + (c + 1) * cw]).astype(bf16)
        gf_ref[:, sl] = _silu(_dot(hb, w_ref[:, 2 * D + RET_VW + c * cw:2 * D + RET_VW + (c + 1) * cw])).astype(bf16)
        gb_ref[:, sl] = _silu(_dot(hb, w_ref[:, 2 * D + 2 * RET_VW + c * cw:2 * D + 2 * RET_VW + (c + 1) * cw])).astype(bf16)


def _ret_inproj(ctx, x, modtab, gain, w_in, cos_t, sin_t, n_ctx_tiles):
    B = x.shape[0]
    T = ctx.shape[1] + x.shape[1]
    tm = ROW_TILE
    n_in = w_in.shape[1]
    tok = lambda w: pl.BlockSpec((None, tm, w), lambda b, i: (b, i, 0))
    return pl.pallas_call(
        functools.partial(_ret_inproj_kernel, n_ctx_tiles=n_ctx_tiles),
        grid=(B, T // tm),
        in_specs=_stream_specs(tm, n_ctx_tiles) + [
                  pl.BlockSpec((None, None, 1, 6 * D), lambda b, i: (b, jnp.where(i < n_ctx_tiles, 0, 1), 0, 0)),
                  pl.BlockSpec((1, D), lambda b, i: (0, 0)),
                  pl.BlockSpec((D, n_in), lambda b, i: (0, 0), pipeline_mode=pl.Buffered(1)),
                  pl.BlockSpec((tm, RET_DK), lambda b, i: (i, 0)),
                  pl.BlockSpec((tm, RET_DK), lambda b, i: (i, 0))],
        out_specs=[tok(D), tok(D), tok(RET_VW), tok(RET_VW), tok(RET_VW)],
        out_shape=[jax.ShapeDtypeStruct((B, T, D), bf16), jax.ShapeDtypeStruct((B, T, D), bf16),
                   jax.ShapeDtypeStruct((B, T, RET_VW), bf16), jax.ShapeDtypeStruct((B, T, RET_VW), bf16),
                   jax.ShapeDtypeStruct((B, T, RET_VW), bf16)],
        compiler_params=_cparams(("arbitrary", "arbitrary")),
        name="ret_inproj",
    )(ctx, x, modtab, gain, w_in, cos_t, sin_t)


def _ret_chunk_index(t, nc, ncc):
    u = t - nc
    back = jnp.where(u < ncc, ncc - 1 - u, nc - 1 - u + ncc)
    return jnp.where(t < nc, t, back)


def _ret_scan_kernel(dt_ref, q_ref, k_ref, v_ref, gf_ref, gb_ref, o_ref,
                     s_ref, of_ref, mask_ref, dq_ref, dk_ref, dc_ref, *, nc, ncc):
    t = pl.program_id(1)
    C = RET_CHUNK

    def init(direction):
        s_ref[...] = jnp.zeros_like(s_ref)
        ii = lax.broadcasted_iota(i32, (C, C), 0)
        jj = lax.broadcasted_iota(i32, (C, C), 1)
        rel = (ii - jj if direction == 0 else jj - ii).astype(f32)
        pos = lax.broadcasted_iota(i32, (C, 1), 0).astype(f32)
        for hd in range(RET_H):
            r = direction * RET_H + hd
            lg = -jnp.exp(dt_ref[r:r + 1, :])
            lg1 = lg[:, 0:1]
            mask_ref[hd] = jnp.where(rel >= 0, jnp.exp(lg1 * jnp.maximum(rel, 0.0)), 0.0)
            if direction == 0:
                dq_ref[hd] = jnp.exp(lg1 * (pos + 1.0))
                dk_ref[hd] = jnp.exp(lg1 * (C - 1.0 - pos))
            else:
                dq_ref[hd] = jnp.exp(lg1 * (C - pos))
                dk_ref[hd] = jnp.exp(lg1 * pos)
            dc_ref[hd] = jnp.exp(lg * float(C))

    pl.when(t == 0)(functools.partial(init, 0))
    pl.when(t == nc)(functools.partial(init, 1))

    fwd = t < nc
    row0 = pl.multiple_of(_ret_chunk_index(t, nc, ncc) * C, C)

    for hd in range(RET_H):
        ks = slice(hd * RET_DK, (hd + 1) * RET_DK)
        vs = slice(hd * RET_DV, (hd + 1) * RET_DV)
        qh = q_ref[:, ks]
        kh = k_ref[:, ks]
        vh = v_ref[:, vs]
        p = (_dot_nt(qh, kh) * mask_ref[hd]).astype(bf16)
        y = _dot(p, vh) + _dot(qh, s_ref[hd].astype(bf16)) * dq_ref[hd]
        kd = (kh.astype(f32) * dk_ref[hd]).astype(bf16)
        upd = lax.dot_general(kd, vh, (((0,), (0,)), ((), ())), preferred_element_type=f32)
        s_ref[hd] = s_ref[hd] * dc_ref[hd][0:1, 0:1] + upd
        yn = _rms(y)

        @pl.when(fwd)
        def _():
            of_ref[pl.ds(row0, C), vs] = (gf_ref[:, vs].astype(f32) * yn).astype(bf16)

        @pl.when(jnp.logical_not(fwd))
        def _():
            o_ref[:, vs] = (of_ref[pl.ds(row0, C), vs].astype(f32) + gb_ref[:, vs].astype(f32) * yn).astype(bf16)


def _ret_scan(dtab, q, k, v, gf, gb, n_ctx):
    B, T, _ = q.shape
    C = RET_CHUNK
    nc = T // C
    ncc = n_ctx // C
    cidx = functools.partial(_ret_chunk_index, nc=nc, ncc=ncc)
    first_back = ncc - 1
    return pl.pallas_call(
        functools.partial(_ret_scan_kernel, nc=nc, ncc=ncc),
        grid=(B, 2 * nc),
        in_specs=[pl.BlockSpec((2 * RET_H, LANES), lambda b, t: (0, 0)),
                  pl.BlockSpec((None, C, D), lambda b, t: (b, cidx(t), 0)),
                  pl.BlockSpec((None, C, D), lambda b, t: (b, cidx(t), 0)),
                  pl.BlockSpec((None, C, RET_VW), lambda b, t: (b, cidx(t), 0)),
                  pl.BlockSpec((None, C, RET_VW), lambda b, t: (b, jnp.where(t < nc, t, nc - 1), 0)),
                  pl.BlockSpec((None, C, RET_VW), lambda b, t: (b, jnp.where(t < nc, first_back, cidx(t)), 0))],
        out_specs=pl.BlockSpec((None, C, RET_VW), lambda b, t: (b, jnp.where(t < nc, first_back, cidx(t)), 0)),
        out_shape=jax.ShapeDtypeStruct((B, T, RET_VW), bf16),
        scratch_shapes=[pltpu.VMEM((RET_H, RET_DK, RET_DV), f32),
                        pltpu.VMEM((T, RET_VW), bf16),
                        pltpu.VMEM((RET_H, C, C), f32),
                        pltpu.VMEM((RET_H, C, 1), f32),
                        pltpu.VMEM((RET_H, C, 1), f32),
                        pltpu.VMEM((RET_H, 1, LANES), f32)],
        compiler_params=_cparams(("arbitrary", "arbitrary")),
        name="ret_scan",
    )(dtab, q, k, v, gf, gb)


def _route(f, rwt_ref, rb_ref, cnt_ref, e_ref, w_ref, r_ref):
    tm = f.shape[0]
    G = N_EXP // N_GRP
    logits = _dot_nt(rwt_ref[...].astype(bf16), f.astype(bf16))
    s = _sigmoid(logits)
    sel = s + rb_ref[...]
    mi = lax.broadcasted_iota(i32, (G, tm), 0)
    neg = -jnp.inf
    s_g = [s[g * G:(g + 1) * G, :] for g in range(N_GRP)]
    sel_g = [sel[g * G:(g + 1) * G, :] for g in range(N_GRP)]

    def first_max(a, ids, big):
        mx = jnp.max(a, axis=0, keepdims=True)
        ix = jnp.min(jnp.where(a == mx, ids, big), axis=0, keepdims=True)
        return mx, ix

    gscore = jnp.zeros((N_GRP, tm), f32)
    gi = lax.broadcasted_iota(i32, (N_GRP, tm), 0)
    for g in range(N_GRP):
        t1, i1 = first_max(sel_g[g], mi, G)
        t2 = jnp.max(jnp.where(mi == i1, neg, sel_g[g]), axis=0, keepdims=True)
        gscore = jnp.where(gi == g, t1 + t2, gscore)
    gmask = jnp.zeros((N_GRP, tm), i32)
    cur = gscore
    for _ in range(TOPK_GRP):
        _, ix = first_max(cur, gi, N_GRP)
        hit = gi == ix
        gmask = jnp.where(hit, 1, gmask)
        cur = jnp.where(hit, neg, cur)
    cand = [jnp.where(gmask[g:g + 1, :] > 0, sel_g[g], neg) for g in range(N_GRP)]
    ids = [mi + g * G for g in range(N_GRP)]

    def across(parts, op):
        acc = parts[0]
        for part in parts[1:]:
            acc = op(acc, part)
        return acc

    e_rows, w_rows = [], []
    for _ in range(TOP_K):
        mx = jnp.max(across(cand, jnp.maximum), axis=0, keepdims=True)
        ix = jnp.min(across([jnp.where(cand[g] == mx, ids[g], N_EXP) for g in range(N_GRP)], jnp.minimum),
                     axis=0, keepdims=True)
        hits = [ids[g] == ix for g in range(N_GRP)]
        cand = [jnp.where(hits[g], neg, cand[g]) for g in range(N_GRP)]
        wv = jnp.sum(across([jnp.where(hits[g], s_g[g], 0.0) for g in range(N_GRP)], jnp.add), axis=0, keepdims=True)
        e_rows.append(ix)
        w_rows.append(wv)
    wsum = w_rows[0]
    for r in range(1, TOP_K):
        wsum = wsum + w_rows[r]

    selm = [jnp.zeros((G, tm), f32) for _ in range(N_GRP)]
    for r in range(TOP_K):
        for g in range(N_GRP):
            selm[g] = jnp.where(ids[g] == e_rows[r], 1.0, selm[g])
    m_all = jnp.concatenate(selm, axis=0)
    ri = lax.broadcasted_iota(i32, (tm, tm), 0)
    ci = lax.broadcasted_iota(i32, (tm, tm), 1)
    upper = jnp.where(ri <= ci, 1.0, 0.0).astype(bf16)
    incl = _dot(m_all.astype(bf16), upper)
    carry = cnt_ref[:, 0:1]
    rank_all = carry + incl - m_all
    cnt_ref[...] = cnt_ref[...] + incl[:, tm - 1:tm]
    for r in range(TOP_K):
        rk = jnp.sum(across([jnp.where(ids[g] == e_rows[r], rank_all[g * G:(g + 1) * G, :], 0.0)
                             for g in range(N_GRP)], jnp.add), axis=0, keepdims=True)
        e_ref[r:r + 1, :] = e_rows[r]
        w_ref[r:r + 1, :] = w_rows[r] / wsum * ROUTED_SCALE
        r_ref[r:r + 1, :] = rk.astype(i32)


def _post_mix_kernel(o_ref, wo_ref, *refs, nt, o_transposed, split_ctx_tiles):
    n_resid = 2 if split_ctx_tiles else 1
    resid = refs[:n_resid]
    mod_ref, g_ref, rwt_ref, rb_ref, x1_ref, fin_ref, hlin_ref, e_ref, w_ref, r_ref, cnt_ref = refs[n_resid:]
    i = pl.program_id(1)

    @pl.when(i == 0)
    def _():
        cnt_ref[...] = jnp.zeros_like(cnt_ref)

    @pl.when(i < nt)
    def _():
        mod = mod_ref[...]
        if o_transposed:
            mixed = lax.dot_general(o_ref[...], wo_ref[...], (((0,), (0,)), ((), ())), preferred_element_type=f32)
        else:
            mixed = _dot(o_ref[...], wo_ref[...])
        x = _stream_tile(resid[0], resid[1], split_ctx_tiles) if split_ctx_tiles else resid[0][...]
        x1 = x + mod[:, 2 * D:3 * D] * mixed
        x1_ref[...] = x1
        f = (_rms(x1) * g_ref[...]) * (1.0 + mod[:, 4 * D:5 * D]) + mod[:, 3 * D:4 * D]
        fin_ref[...] = f.astype(bf16)
        tm = f.shape[0]
        for j in range(D // LANES):
            hlin_ref[pl.ds(j, tm, stride=D // LANES), :] = f[:, j * LANES:(j + 1) * LANES]
        _route(f, rwt_ref, rb_ref, cnt_ref, e_ref, w_ref, r_ref)

    @pl.when(i == nt)
    def _():
        hlin_ref[...] = jnp.zeros_like(hlin_ref)


def _post_mix(o, w_o, resid, x_tile_off, modtab, n_ctx_tiles, gain, rwt, rb, o_transposed=False):
    if o_transposed:
        B, KO, N = o.shape
    else:
        B, N, KO = o.shape
    tm = ROW_TILE
    nt = N // tm
    last = lambda i: jnp.minimum(i, nt - 1)
    tok = lambda w: pl.BlockSpec((None, tm, w), lambda b, i: (b, last(i), 0))
    sel = lambda: pl.BlockSpec((None, TOP_K, tm), lambda b, i: (b, 0, last(i)))
    o_spec = (pl.BlockSpec((None, KO, tm), lambda b, i: (b, 0, last(i))) if o_transposed else tok(KO))
    split = len(resid) == 2
    resid_specs = (_stream_specs(tm, n_ctx_tiles, nt - 1) if split else
                   [pl.BlockSpec((None, tm, D), lambda b, i: (b, last(i) + x_tile_off, 0))])
    return pl.pallas_call(
        functools.partial(_post_mix_kernel, nt=nt, o_transposed=o_transposed,
                          split_ctx_tiles=n_ctx_tiles if split else 0),
        grid=(B, nt + 1),
        in_specs=[o_spec,
                  pl.BlockSpec((KO, D), lambda b, i: (0, 0))] + resid_specs + [
                  pl.BlockSpec((None, None, 1, 6 * D), lambda b, i: (b, jnp.where(last(i) < n_ctx_tiles, 0, 1), 0, 0)),
                  pl.BlockSpec((1, D), lambda b, i: (0, 0)),
                  pl.BlockSpec((N_EXP, D), lambda b, i: (0, 0)),
                  pl.BlockSpec((N_EXP, 1), lambda b, i: (0, 0))],
        out_specs=[tok(D), tok(D),
                   pl.BlockSpec((None, tm * (D // LANES), LANES), lambda b, i: (b, i, 0)),
                   sel(), sel(), sel(),
                   pl.BlockSpec((None, N_EXP, LANES), lambda b, i: (b, 0, 0))],
        out_shape=[jax.ShapeDtypeStruct((B, N, D), f32), jax.ShapeDtypeStruct((B, N, D), bf16),
                   jax.ShapeDtypeStruct((B, (N + tm) * (D // LANES), LANES), f32),
                   jax.ShapeDtypeStruct((B, TOP_K, N), i32), jax.ShapeDtypeStruct((B, TOP_K, N), f32),
                   jax.ShapeDtypeStruct((B, TOP_K, N), i32),
                   jax.ShapeDtypeStruct((B, N_EXP, LANES), f32)],
        compiler_params=_cparams(("arbitrary", "arbitrary")),
        name="post_mix",
    )(o, w_o, *resid, modtab, gain, rwt, rb)


META_W = 256
PLAN_ALIGN = 1024


def _round_up(n, m):
    return -(-n // m) * m


def _moe_sizes(n_tok):
    tm = MOE_TILE
    nt_max = (n_tok * TOP_K + N_EXP * (tm - 1)) // tm + 1
    ntp = _round_up(nt_max + MOE_GROUP, PLAN_ALIGN // tm)
    assert ntp <= META_W
    return ntp, _round_up(n_tok, PLAN_ALIGN)


def _plan_kernel(e_ref, r_ref, cnt_ref, pos_ref, meta_ref, *, n_tok):
    tm = MOE_TILE
    ntile = jnp.floor((cnt_ref[...] + (tm - 1.0)) * (1.0 / tm))
    ntb = ntile.astype(bf16)
    ei = lax.broadcasted_iota(i32, (N_EXP, LANES), 0)
    ej = lax.broadcasted_iota(i32, (N_EXP, LANES), 1)
    lower = jnp.where(ej <= ei, 1.0, 0.0)[:, :N_EXP].astype(bf16)
    tend = _dot(lower, ntb)
    tstart = tend - ntile
    tt = lax.broadcasted_iota(i32, (N_EXP, META_W), 1).astype(f32)
    te = jnp.sum(jnp.where(tt >= tend[:, 0:1], 1.0, 0.0), axis=0, keepdims=True)
    meta_ref[...] = jnp.zeros_like(meta_ref)
    meta_ref[0:1, :] = jnp.minimum(te, N_EXP - 1.0).astype(i32)
    meta_ref[1:2, :] = jnp.broadcast_to(tend[N_EXP - 1:N_EXP, 0:1], (1, META_W)).astype(i32)
    e = e_ref[...]
    base = jnp.zeros(e.shape, f32)
    for ex in range(N_EXP):
        base = jnp.where(e == ex, tstart[ex:ex + 1, 0:1] * float(tm), base)
    pos_ref[...] = jnp.zeros_like(pos_ref)
    pos_ref[:, 0:n_tok] = base.astype(i32) + r_ref[...]


def _plan(e_t, r_t, cnt):
    B, K, N = e_t.shape
    _, npad = _moe_sizes(N)
    return pl.pallas_call(
        functools.partial(_plan_kernel, n_tok=N),
        grid=(B,),
        in_specs=[pl.BlockSpec((None, K, N), lambda b: (b, 0, 0)),
                  pl.BlockSpec((None, K, N), lambda b: (b, 0, 0)),
                  pl.BlockSpec((None, N_EXP, LANES), lambda b: (b, 0, 0))],
        out_specs=[pl.BlockSpec((None, K, npad), lambda b: (b, 0, 0)),
                   pl.BlockSpec((None, SUBLANES, META_W), lambda b: (b, 0, 0))],
        out_shape=[jax.ShapeDtypeStruct((B, K, npad), i32), jax.ShapeDtypeStruct((B, SUBLANES, META_W), i32)],
        compiler_params=_cparams(("arbitrary",)),
        name="moe_plan",
    )(e_t, r_t, cnt)


def _plan_invert(pos, w_t, n_tok):
    B, K, npad = pos.shape
    ntp, _ = _moe_sizes(n_tok)
    plen = ntp * MOE_TILE
    nch = D // LANES
    mesh = plsc.VectorSubcoreMesh(core_axis_name="c", subcore_axis_name="s")
    n_cores = mesh.num_cores
    assert 2 * B <= n_cores * mesh.num_subcores and n_tok % SC_LANES == 0 and plen % SC_LANES == 0

    @functools.partial(
        pl.kernel, mesh=mesh,
        out_type=[jax.ShapeDtypeStruct((B * plen,), i32), jax.ShapeDtypeStruct((B * plen,), f32)],
        scratch_types=[pltpu.VMEM((npad,), i32), pltpu.VMEM((n_tok,), f32),
                       pltpu.VMEM((plen,), i32), pltpu.VMEM((plen,), f32)],
        compiler_params=dataclasses.replace(pltpu.CompilerParams(), needs_layout_passes=False))
    def invert(pos_hbm, w_hbm, rows_hbm, ws_hbm, pos_c, w_c, rows_v, ws_v):
        wid = lax.axis_index("s") * n_cores + lax.axis_index("c")
        b = wid % B
        lane = lax.iota(i32, SC_LANES)

        def load_pos(k):
            pltpu.sync_copy(pos_hbm.at[pl.ds(pl.multiple_of((b * K + k) * npad, SUBLANES), npad)], pos_c)

        @pl.when(wid < B)
        def _():
            pad = jnp.full((SC_LANES,), n_tok * nch, i32)

            @pl.loop(0, plen, step=SC_LANES)
            def _(i):
                rows_v[pl.ds(i, SC_LANES)] = pad

            for k in range(K):
                load_pos(k)

                @pl.loop(0, n_tok, step=SC_LANES)
                def _(n):
                    plsc.store_scatter(rows_v, [pos_c[pl.ds(n, SC_LANES)]], (lane + n) * nch)

            pltpu.sync_copy(rows_v, rows_hbm.at[pl.ds(pl.multiple_of(b * plen, SUBLANES), plen)])

        @pl.when(jnp.logical_and(wid >= B, wid < 2 * B))
        def _():
            zero = jnp.zeros((SC_LANES,), f32)

            @pl.loop(0, plen, step=SC_LANES)
            def _(i):
                ws_v[pl.ds(i, SC_LANES)] = zero

            for k in range(K):
                load_pos(k)
                pltpu.sync_copy(w_hbm.at[pl.ds(pl.multiple_of((b * K + k) * n_tok, SUBLANES), n_tok)], w_c)

                @pl.loop(0, n_tok, step=SC_LANES)
                def _(n):
                    plsc.store_scatter(ws_v, [pos_c[pl.ds(n, SC_LANES)]], w_c[pl.ds(n, SC_LANES)])

            pltpu.sync_copy(ws_v, ws_hbm.at[pl.ds(pl.multiple_of(b * plen, SUBLANES), plen)])

    return invert(pos.reshape(-1), w_t.reshape(-1))


def _moe_kernel(te_ref, nt_ref, *refs, ntp, n_tok):
    R = MOE_GROUP
    rg_ref, rs_ref, ws_ref, hlin_ref = refs[0:4]
    wgu_refs, wd_refs = refs[4:4 + R], refs[4 + R:4 + 2 * R]
    out_ref, acc_ref = refs[4 + 2 * R:6 + 2 * R]
    bufs = refs[6 + 2 * R:]
    xs_refs = [bufs[0:R], bufs[R:2 * R]]
    ylin_refs = [bufs[2 * R:3 * R], bufs[3 * R:4 * R]]
    b = pl.program_id(0)
    t = pl.program_id(1)
    TM = MOE_TILE
    NCH = D // LANES
    U = 8
    TMP = TM + SUBLANES
    ngrp = ntp // R

    @pl.when(t == 0)
    def _():
        acc_ref[...] = jnp.zeros_like(acc_ref)
        for buf in bufs:
            buf[...] = jnp.zeros_like(buf)

    def gather_rows(p, r, c):
        for m in range(c * U, (c + 1) * U):
            off = pl.multiple_of(rg_ref[r * TM + m], NCH)
            xs_refs[p][r][pl.ds(m, NCH, stride=TMP), :] = hlin_ref[pl.ds(off, NCH), :]

    def scatter_rows(p, r, c):
        offs = [pl.multiple_of(rs_ref[r * TM + c * U + u], NCH) for u in range(U)]
        news = [acc_ref[pl.ds(offs[u], NCH), :] + ylin_refs[p][r][pl.ds((c * U + u) * NCH, NCH), :]
                for u in range(U)]
        for u in range(U):
            acc_ref[pl.ds(offs[u], NCH), :] = news[u]

    def stage(p):
        row_work = [functools.partial(fn, p, r, c) for c in range(TM // U) for r in range(R)
                    for fn in (gather_rows, scatter_rows)]
        n_pieces = R * (EXP_FF // LANES + D // MXU_N)
        per_piece = -(-len(row_work) // n_pieces)

        def deal():
            for fn in row_work[:per_piece]:
                fn()
            del row_work[:per_piece]

        for r in range(R):
            x = jnp.concatenate([xs_refs[1 - p][r][pl.ds(j * TMP, TM), :] for j in range(NCH)], axis=1).astype(bf16)
            wcol = jnp.broadcast_to(ws_ref[r:r + 1, :], (SUBLANES, TM)).T[:, 0:1]
            gate = _dot(x, wgu_refs[r][:, :EXP_FF])
            deal()
            up = _dot(x, wgu_refs[r][:, EXP_FF:])
            deal()
            a = (_silu(gate) * up * wcol).astype(bf16)
            for c in range(D // MXU_N):
                y = _dot(a, wd_refs[r][:, c * MXU_N:(c + 1) * MXU_N])
                for jj in range(MXU_N // LANES):
                    j = c * (MXU_N // LANES) + jj
                    ylin_refs[1 - p][r][pl.ds(j, TM, stride=NCH), :] = y[:, jj * LANES:(jj + 1) * LANES]
                deal()
        while row_work:
            deal()

    live = (t - 2) * R < nt_ref[b]
    pl.when(jnp.logical_and(live, t % 2 == 0))(functools.partial(stage, 0))
    pl.when(jnp.logical_and(live, t % 2 == 1))(functools.partial(stage, 1))

    @pl.when(t >= ngrp + 2)
    def _():
        row0 = (t - (ngrp + 2)) * (ROW_TILE * NCH)
        for j in range(NCH):
            out_ref[:, j * LANES:(j + 1) * LANES] = acc_ref[pl.ds(row0 + j, ROW_TILE, stride=NCH), :]


def _moe(te, nt, rows, wsort, hlin, w_gu, w_d, layer):
    B = hlin.shape[0]
    NCH = D // LANES
    n_tok = hlin.shape[1] // NCH - ROW_TILE
    assert n_tok % ROW_TILE == 0
    nf = n_tok // ROW_TILE
    TM = MOE_TILE
    R = MOE_GROUP
    ntp, _ = _moe_sizes(n_tok)
    assert ntp % R == 0
    ngrp = ntp // R

    def group_of(b, t, nt_ref, lag):
        grp = t - lag
        ok = jnp.logical_and(t >= lag, grp * R < nt_ref[b])
        return b * ngrp + jnp.where(ok, grp, ngrp - 1)

    def rows_spec(lag):
        return pl.BlockSpec((R * TM,), lambda b, t, te_ref, nt_ref: (group_of(b, t, nt_ref, lag),),
                            memory_space=pltpu.SMEM)

    ws_spec = pl.BlockSpec((None, R, TM), lambda b, t, te_ref, nt_ref: (group_of(b, t, nt_ref, 1), 0, 0))

    def w_spec(shape, r):
        def index(b, t, te_ref, nt_ref):
            tile = jnp.clip((t - 1) * R + r, 0, nt_ref[b] - 1)
            return (layer, te_ref[b * META_W + tile], 0, 0)
        return pl.BlockSpec((None, None) + shape, index)

    grid_spec = pltpu.PrefetchScalarGridSpec(
        num_scalar_prefetch=2,
        grid=(B, ngrp + 2 + nf),
        in_specs=([rows_spec(0), rows_spec(2), ws_spec,
                   pl.BlockSpec((None, (n_tok + ROW_TILE) * NCH, LANES), lambda b, t, *_: (b, 0, 0),
                                pipeline_mode=pl.Buffered(1))]
                  + [w_spec((D, 2 * EXP_FF), r) for r in range(R)]
                  + [w_spec((EXP_FF, D), r) for r in range(R)]),
        out_specs=pl.BlockSpec((None, ROW_TILE, D), lambda b, t, *_: (b, jnp.maximum(t - (ngrp + 2), 0), 0)),
        scratch_shapes=([pltpu.VMEM(((n_tok + SUBLANES) * NCH, LANES), f32)]
                        + [pltpu.VMEM(((TM + SUBLANES) * NCH, LANES), f32) for _ in range(2 * R)]
                        + [pltpu.VMEM((TM * NCH, LANES), f32) for _ in range(2 * R)]),
    )
    return pl.pallas_call(
        functools.partial(_moe_kernel, ntp=ntp, n_tok=n_tok),
        grid_spec=grid_spec,
        out_shape=jax.ShapeDtypeStruct((B, n_tok, D), f32),
        compiler_params=_cparams(("arbitrary", "arbitrary"), MOE_VMEM_LIMIT),
        name="moe",
    )(te, nt, rows, rows, wsort.reshape(B * ngrp, R, TM), hlin, *([w_gu] * R), *([w_d] * R))


def _routed_experts(e_t, w_t, r_t, cnt, hlin, w_gu, w_d, layer):
    N = e_t.shape[2]
    pos, meta = _plan(e_t, r_t, cnt)
    te = meta[:, 0, :].reshape(-1)
    nt = meta[:, 1, 0]
    rows, wsort = _plan_invert(pos, w_t, N)
    return _moe(te, nt, rows, wsort, hlin, w_gu, w_d, layer)


def _shared_ffn(fin, shgu_ref, shd_ref):
    gu = _dot(fin, shgu_ref[...])
    return _dot((_silu(gu[:, :SH_FF]) * gu[:, SH_FF:]).astype(bf16), shd_ref[...])


def _post_ffn_mla_kernel(x1_ref, routed_ref, fin_ref, shgu_ref, shd_ref, mod0_ref, mod1_ref, g_ref, win_ref,
                         x2_ref, a_ref):
    x2 = x1_ref[...] + mod0_ref[...][:, 5 * D:6 * D] * (routed_ref[...] + _shared_ffn(fin_ref[...], shgu_ref, shd_ref))
    x2_ref[...] = x2
    mod1 = mod1_ref[...]
    h = (_rms(x2) * g_ref[...]) * (1.0 + mod1[:, D:2 * D]) + mod1[:, 0:D]
    a_ref[...] = _dot(h.astype(bf16), win_ref[...])


def _post_ffn_mla(x1, routed, fin, sh_gu, sh_d, modtab0, modtab1, gain, w_in, n_ctx_tiles):
    B, T, _ = x1.shape
    tm = ROW_TILE
    tok = lambda w: pl.BlockSpec((None, tm, w), lambda b, i: (b, i, 0))
    modspec = lambda: pl.BlockSpec((None, None, 1, 6 * D), lambda b, i: (b, jnp.where(i < n_ctx_tiles, 0, 1), 0, 0))
    full = lambda r, c: pl.BlockSpec((r, c), lambda b, i: (0, 0))
    return pl.pallas_call(
        _post_ffn_mla_kernel,
        grid=(B, T // tm),
        in_specs=[tok(D), tok(D), tok(D), full(D, 2 * SH_FF), full(SH_FF, D), modspec(), modspec(),
                  full(1, D), full(D, MLA_IN_PAD)],
        out_specs=[tok(D), tok(MLA_IN_PAD)],
        out_shape=[jax.ShapeDtypeStruct((B, T, D), f32), jax.ShapeDtypeStruct((B, T, MLA_IN_PAD), f32)],
        compiler_params=_cparams(("arbitrary", "arbitrary")),
        name="post_ffn_mla",
    )(x1, routed, fin, sh_gu, sh_d, modtab0, modtab1, gain, w_in)


def _post_ffn_final_kernel(x1_ref, routed_ref, fin_ref, shgu_ref, shd_ref, mod_ref, out_ref):
    out_ref[...] = x1_ref[...] + mod_ref[...][:, 5 * D:6 * D] * (
        routed_ref[...] + _shared_ffn(fin_ref[...], shgu_ref, shd_ref))


def _post_ffn_final(x1, routed, fin, sh_gu, sh_d, modtab):
    B, N, _ = x1.shape
    tm = ROW_TILE
    tok = lambda w: pl.BlockSpec((None, tm, w), lambda b, i: (b, i, 0))
    full = lambda r, c: pl.BlockSpec((r, c), lambda b, i: (0, 0))
    return pl.pallas_call(
        _post_ffn_final_kernel,
        grid=(B, N // tm),
        in_specs=[tok(D), tok(D), tok(D), full(D, 2 * SH_FF), full(SH_FF, D),
                  pl.BlockSpec((None, None, 1, 6 * D), lambda b, i: (b, 1, 0, 0))],
        out_specs=tok(D),
        out_shape=jax.ShapeDtypeStruct((B, N, D), f32),
        compiler_params=_cparams(("arbitrary", "arbitrary")),
        name="post_ffn_final",
    )(x1, routed, fin, sh_gu, sh_d, modtab)


def _mla_prep_kernel(a_ref, qan_ref, wqn_ref, wqr_ref, kvan_ref, wk_ref, wv_ref, qnn_ref, qnr_ref, knn_ref, knr_ref,
                     cos_ref, sin_ref, q_ref, k_ref, v_ref):
    a = a_ref[...]
    tm = a.shape[0]
    scale = MLA_QK ** -0.5 * float(np.log2(np.e))
    cos = cos_ref[...]
    sin = sin_ref[...]
    lane = lax.broadcasted_iota(i32, (tm, LANES), 1)
    first = (lane // (MLA_ROPE // 4)) % 2 == 0

    def rope(xb):
        sw = jnp.where(first, pltpu.roll(xb, LANES - MLA_ROPE // 4, axis=1), pltpu.roll(xb, MLA_ROPE // 4, axis=1))
        return xb * cos + sw * sin

    qa = (_rms(a[:, :MLA_QR]) * qan_ref[...]).astype(bf16)
    qn = _dot(qa, wqn_ref[...])
    qr = _dot(qa, wqr_ref[...])
    ri = lax.broadcasted_iota(i32, (LANES, LANES), 0) // MLA_ROPE
    ci = lax.broadcasted_iota(i32, (LANES, LANES), 1) // MLA_ROPE
    seg = jnp.where(ri == ci, 1.0, 0.0).astype(bf16)

    def seg_sum(sq):
        hi = sq.astype(bf16)
        r1 = sq - hi.astype(f32)
        mid = r1.astype(bf16)
        lo = (r1 - mid.astype(f32)).astype(bf16)
        return _dot(hi, seg) + _dot(mid, seg) + _dot(lo, seg)

    qr_blocks = []
    for p in range(MLA_H // 2):
        blk = qr[:, p * LANES:(p + 1) * LANES]
        blk = blk * lax.rsqrt(seg_sum(blk * blk) * (1.0 / MLA_ROPE) + EPS) * qnr_ref[:, p * LANES:(p + 1) * LANES]
        qr_blocks.append(rope(blk) * scale)

    kv = (_rms(a[:, MLA_QR:MLA_QR + MLA_KVR]) * kvan_ref[...]).astype(bf16)
    kn = _dot(kv, wk_ref[...])
    v_ref[...] = _dot_nt(wv_ref[...], kv).astype(bf16)
    kr = a[:, MLA_QR + MLA_KVR:MLA_IN_PAD]
    kr = rope(_rms(kr, MLA_ROPE) * knr_ref[...])
    kr_odd = pltpu.roll(kr, MLA_ROPE, axis=1)
    for hd in range(MLA_H):
        sl = slice(hd * MLA_NOPE, (hd + 1) * MLA_NOPE)
        q_ref[:, 2 * hd * LANES:(2 * hd + 1) * LANES] = (_rms(qn[:, sl]) * qnn_ref[...] * scale).astype(bf16)
        q_ref[:, (2 * hd + 1) * LANES:(2 * hd + 2) * LANES] = qr_blocks[hd // 2].astype(bf16)
        k_ref[:, 2 * hd * LANES:(2 * hd + 1) * LANES] = (_rms(kn[:, sl]) * knn_ref[...]).astype(bf16)
        k_ref[:, (2 * hd + 1) * LANES:(2 * hd + 2) * LANES] = (kr if hd % 2 == 0 else kr_odd).astype(bf16)


def _mla_prep(a, qan, wqn, wqr, kvan, wk, wv, qnn, qnr, knn, knr, cos_t, sin_t, n_ctx_tiles):
    B, T, _ = a.shape
    tm = ROW_TILE
    tok = lambda w: pl.BlockSpec((None, tm, w), lambda b, i: (b, i, 0))
    full = lambda r, c: pl.BlockSpec((r, c), lambda b, i: (0, 0))
    hw = 2 * LANES * MLA_H
    q_spec = pl.BlockSpec((None, tm, hw), lambda b, i: (b, jnp.maximum(i - n_ctx_tiles, 0), 0))
    return pl.pallas_call(
        _mla_prep_kernel,
        grid=(B, T // tm),
        in_specs=[tok(MLA_IN_PAD), full(1, MLA_QR), full(MLA_QR, MLA_H * MLA_NOPE), full(MLA_QR, MLA_H * MLA_ROPE),
                  full(1, MLA_KVR), full(MLA_KVR, MLA_H * MLA_NOPE), full(MLA_H * MLA_V, MLA_KVR),
                  full(1, MLA_NOPE), full(1, MLA_H * MLA_ROPE), full(1, MLA_NOPE), full(1, LANES),
                  pl.BlockSpec((tm, LANES), lambda b, i: (i, 0)), pl.BlockSpec((tm, LANES), lambda b, i: (i, 0))],
        out_specs=[q_spec, tok(hw), pl.BlockSpec((None, MLA_H * MLA_V, tm), lambda b, i: (b, 0, i))],
        out_shape=[jax.ShapeDtypeStruct((B, T - n_ctx_tiles * tm, hw), bf16), jax.ShapeDtypeStruct((B, T, hw), bf16),
                   jax.ShapeDtypeStruct((B, MLA_H * MLA_V, T), bf16)],
        compiler_params=_cparams(("arbitrary", "arbitrary")),
        name="mla_prep",
    )(a, qan, wqn, wqr, kvan, wk, wv, qnn, qnr, knn, knr, cos_t, sin_t)


ATT_KV_CHUNK = 256
ATT_TQ = 256


def _mla_attn_kernel(q_ref, k_ref, vt_ref, o_ref, s0_ref, s1_ref, m0_ref, m1_ref):
    tq = q_ref.shape[0]
    ck = ATT_KV_CHUNK
    nck = k_ref.shape[0] // ck
    i = pl.program_id(0)

    s_refs, m_refs = (s0_ref, s1_ref), (m0_ref, m1_ref)

    @pl.when(i == 0)
    def _():
        for ref in s_refs + m_refs:
            ref[...] = jnp.zeros_like(ref)

    def fold(a, op):
        return op(a.reshape(ck // SUBLANES, SUBLANES, tq), axis=0)

    def stage(par):
        q = q_ref[...]
        mpart = jnp.full((SUBLANES, tq), -jnp.inf, f32)
        for j in range(nck):
            s = _dot_nt(k_ref[j * ck:(j + 1) * ck, :], q)
            s_refs[par][j * ck:(j + 1) * ck, :] = s
            mpart = jnp.maximum(mpart, fold(s, jnp.max))
        m_refs[par][...] = mpart
        m = jnp.max(m_refs[1 - par][...], axis=0, keepdims=True)
        lpart = jnp.zeros((SUBLANES, tq), f32)
        acc = jnp.zeros((MLA_V, tq), f32)
        for j in range(nck):
            p = jnp.exp2(s_refs[1 - par][j * ck:(j + 1) * ck, :] - m)
            lpart = lpart + fold(p, jnp.sum)
            acc = acc + _dot(vt_ref[:, j * ck:(j + 1) * ck], p.astype(bf16))
        o_ref[...] = (acc / jnp.sum(lpart, axis=0, keepdims=True)).astype(bf16)

    pl.when(i % 2 == 0)(functools.partial(stage, 0))
    pl.when(i % 2 == 1)(functools.partial(stage, 1))


def _mla_attn(q, k, vt):
    B, S, _ = q.shape
    T = k.shape[1]
    tq = ATT_TQ if S % ATT_TQ == 0 else ROW_TILE
    nq = S // tq
    ntile = B * MLA_H * nq
    assert T % ATT_KV_CHUNK == 0

    def tile(g):
        g = jnp.clip(g, 0, ntile - 1)
        return g // (MLA_H * nq), (g // nq) % MLA_H, g % nq

    def q_index(g):
        b, h, i = tile(g)
        return b, i, h

    def k_index(g):
        b, h, _ = tile(g)
        return b, 0, h

    def vt_index(g):
        b, h, _ = tile(g - 1)
        return b, h, 0

    def o_index(g):
        b, h, i = tile(g - 1)
        return b, h, i

    return pl.pallas_call(
        _mla_attn_kernel,
        grid=(ntile + 1,),
        in_specs=[pl.BlockSpec((None, tq, 2 * LANES), q_index),
                  pl.BlockSpec((None, T, 2 * LANES), k_index),
                  pl.BlockSpec((None, MLA_V, T), vt_index)],
        out_specs=pl.BlockSpec((None, MLA_V, tq), o_index),
        out_shape=jax.ShapeDtypeStruct((B, MLA_H * MLA_V, S), bf16),
        scratch_shapes=[pltpu.VMEM((T, tq), f32), pltpu.VMEM((T, tq), f32),
                        pltpu.VMEM((SUBLANES, tq), f32), pltpu.VMEM((SUBLANES, tq), f32)],
        compiler_params=_cparams(("arbitrary",)),
        name="mla_attn",
    )(q, k, vt)


def _axial_angles(rows_n, rot_dim):
    axis_dim = rot_dim // 2
    inv = ROPE_BASE ** (-jnp.arange(0, axis_dim, 2, dtype=f32) / axis_dim)
    row = jnp.repeat(jnp.arange(rows_n, dtype=f32), GRID_W)
    col = jnp.tile(jnp.arange(GRID_W, dtype=f32), rows_n)
    return row[:, None] * inv, col[:, None] * inv


def _rope_tables(seq, n_ctx, rot_dim, reps):
    ang_r, ang_c = _axial_angles(seq // GRID_W, rot_dim)
    cos = jnp.concatenate([jnp.cos(ang_r)] * 2 + [jnp.cos(ang_c)] * 2, axis=1)
    sin = jnp.concatenate([-jnp.sin(ang_r), jnp.sin(ang_r), -jnp.sin(ang_c), jnp.sin(ang_c)], axis=1)
    cos = jnp.concatenate([jnp.ones((n_ctx, rot_dim), f32), cos], axis=0)
    sin = jnp.concatenate([jnp.zeros((n_ctx, rot_dim), f32), sin], axis=0)
    return jnp.tile(cos, (1, reps)), jnp.tile(sin, (1, reps))


def kernel(x, c, ctx, c_ctx, ada_w, ada_b, norm_mix, norm_ffn, ret_w_in, ret_decay_f, ret_decay_b, ret_w_o,
           mla_w_in, mla_q_a_norm, mla_w_q_b, mla_kv_a_norm, mla_w_kv_b, mla_q_norm, mla_k_norm, mla_w_o,
           router_w, router_bias, exp_w_gu, exp_w_down, sh_w_gu, sh_w_down):
    B, S, _ = x.shape
    n_ctx = ctx.shape[1]
    assert n_ctx % ROW_TILE == 0 and S % ROW_TILE == 0 and S % GRID_W == 0
    n_ctx_tiles = n_ctx // ROW_TILE

    rows = -(-(B + 1) // SUBLANES) * SUBLANES
    cc = jnp.zeros((rows, D), f32).at[:B].set(c).at[B].set(c_ctx)
    mod = _ada(cc, ada_w, ada_b)

    def modtab(i):
        ctx_row = jnp.broadcast_to(mod[i, B][None, :], (B, 6 * D))
        return jnp.stack([ctx_row, mod[i, :B]], axis=1)[:, :, None, :]

    mod0, mod1 = modtab(0), modtab(1)

    cos_r, sin_r = _rope_tables(S, n_ctx, RET_DK, 1)
    q, k, v, gf, gb = _ret_inproj(ctx, x, mod0, norm_mix[0][None, :], ret_w_in[0].astype(bf16), cos_r, sin_r,
                                  n_ctx_tiles)
    dtab = jnp.broadcast_to(jnp.concatenate([ret_decay_f[0], ret_decay_b[0]])[:, None], (2 * RET_H, LANES))
    o = _ret_scan(dtab, q, k, v, gf, gb, n_ctx)
    x1, fin, hlin, e_t, w_t, r_t, cnt = _post_mix(
        o, ret_w_o[0].astype(bf16), (ctx, x), 0, mod0, n_ctx_tiles, norm_ffn[0][None, :],
        router_w[0].T, router_bias[0][:, None])
    exp_gu, exp_d = exp_w_gu.astype(bf16), exp_w_down.astype(bf16)
    routed = _routed_experts(e_t, w_t, r_t, cnt, hlin, exp_gu, exp_d, 0)
    w_in1 = jnp.zeros((D, MLA_IN_PAD), f32).at[:, :mla_w_in.shape[2]].set(mla_w_in[0]).astype(bf16)
    x2, a = _post_ffn_mla(x1, routed, fin, sh_w_gu[0].astype(bf16), sh_w_down[0].astype(bf16), mod0, mod1,
                          norm_mix[1][None, :], w_in1, n_ctx_tiles)

    wq = mla_w_q_b[0].reshape(MLA_QR, MLA_H, MLA_QK)
    wqn = wq[:, :, :MLA_NOPE].reshape(MLA_QR, MLA_H * MLA_NOPE).astype(bf16)
    wqr = wq[:, :, MLA_NOPE:].reshape(MLA_QR, MLA_H * MLA_ROPE).astype(bf16)
    wkv = mla_w_kv_b[0].reshape(MLA_KVR, MLA_H, MLA_NOPE + MLA_V)
    wk = wkv[:, :, :MLA_NOPE].reshape(MLA_KVR, MLA_H * MLA_NOPE).astype(bf16)
    wv = wkv[:, :, MLA_NOPE:].reshape(MLA_KVR, MLA_H * MLA_V).T.astype(bf16)
    qnn = mla_q_norm[0][None, :MLA_NOPE]
    qnr = jnp.tile(mla_q_norm[0][None, MLA_NOPE:], (1, MLA_H))
    knn = mla_k_norm[0][None, :MLA_NOPE]
    knr = jnp.concatenate([mla_k_norm[0][MLA_NOPE:], jnp.zeros((LANES - MLA_ROPE,), f32)])[None, :]
    cos_m, sin_m = _rope_tables(S, n_ctx, MLA_ROPE, LANES // MLA_ROPE)
    qf, kf, vf = _mla_prep(a, mla_q_a_norm[0][None, :], wqn, wqr, mla_kv_a_norm[0][None, :], wk, wv,
                           qnn, qnr, knn, knr, cos_m, sin_m, n_ctx_tiles)
    o1 = _mla_attn(qf, kf, vf)
    x3, fin1, hlin1, e1, w1, r1, cnt1 = _post_mix(
        o1, mla_w_o[0].astype(bf16), (x2,), n_ctx_tiles, mod1, 0, norm_ffn[1][None, :],
        router_w[1].T, router_bias[1][:, None], o_transposed=True)
    routed1 = _routed_experts(e1, w1, r1, cnt1, hlin1, exp_gu, exp_d, 1)
    return _post_ffn_final(x3, routed1, fin1, sh_w_gu[1].astype(bf16), sh_w_down[1].astype(bf16), mod1)
```

```python
import dataclasses
import functools

import jax
import jax.numpy as jnp
import numpy as np
from jax import lax
from jax.experimental import pallas as pl
from jax.experimental.pallas import tpu as pltpu
from jax.experimental.pallas import tpu_sc as plsc

f32 = jnp.float32
bf16 = jnp.bfloat16
i32 = jnp.int32

D = 1024
GRID_W = 64
EPS = 1e-6
ROPE_BASE = 10000.0
RET_H = 4
RET_DK = 256
RET_DV = 512
RET_VW = RET_H * RET_DV
RET_CHUNK = 256
MLA_H = 8
MLA_NOPE = 128
MLA_ROPE = 64
MLA_QK = MLA_NOPE + MLA_ROPE
MLA_V = 128
MLA_QR = 384
MLA_KVR = 256
MLA_IN_PAD = 768
N_EXP = 64
TOP_K = 8
N_GRP = 8
TOPK_GRP = 4
EXP_FF = 256
SH_FF = 256
ROUTED_SCALE = 2.5

LANES = 128
SUBLANES = 8
SC_LANES = 16
MXU_N = 256
ROW_TILE = 256
MOE_TILE = 256
MOE_GROUP = 2
VMEM_LIMIT = 56 * 1024 * 1024
MOE_VMEM_LIMIT = 62 * 1024 * 1024


def _cparams(sem, vmem=VMEM_LIMIT):
    return pltpu.CompilerParams(dimension_semantics=sem, vmem_limit_bytes=vmem)


def _sigmoid(x):
    return 1.0 / (1.0 + jnp.exp(-x))


def _silu(x):
    return x * _sigmoid(x)


def _rms(x, n=None):
    n = x.shape[-1] if n is None else n
    return x * lax.rsqrt(jnp.sum(x * x, axis=-1, keepdims=True) * (1.0 / n) + EPS)


def _dot(a, b):
    return jnp.dot(a, b, preferred_element_type=f32)


def _dot_nt(a, b, precision=None):
    return lax.dot_general(a, b, (((1,), (1,)), ((), ())), preferred_element_type=f32, precision=precision)


def _ada_kernel(c_ref, w_ref, b_ref, o_ref):
    s = _silu(c_ref[...]).astype(bf16)
    o_ref[...] = _dot(s, w_ref[...].astype(bf16)) + b_ref[...]


def _ada(cc, ada_w, ada_b):
    depth = ada_w.shape[0]
    rows = cc.shape[0]
    tn = 1536
    return pl.pallas_call(
        _ada_kernel,
        grid=(depth, 6 * D // tn),
        in_specs=[pl.BlockSpec((rows, D), lambda i, j: (0, 0)),
                  pl.BlockSpec((None, D, tn), lambda i, j: (i, 0, j)),
                  pl.BlockSpec((None, 1, tn), lambda i, j: (i, 0, j))],
        out_specs=pl.BlockSpec((None, rows, tn), lambda i, j: (i, 0, j)),
        out_shape=jax.ShapeDtypeStruct((depth, rows, 6 * D), f32),
        compiler_params=_cparams(("arbitrary", "arbitrary")),
        name="ada",
    )(cc, ada_w, ada_b.reshape(depth, 1, 6 * D))


def _stream_tile(ctx_ref, x_ref, n_ctx_tiles):
    return jnp.where(pl.program_id(1) < n_ctx_tiles, ctx_ref[...], x_ref[...])


def _stream_specs(tm, n_ctx_tiles, last=None):
    clamp = (lambda i: i) if last is None else (lambda i: jnp.minimum(i, last))
    return [pl.BlockSpec((None, tm, D), lambda b, i: (b, jnp.minimum(clamp(i), n_ctx_tiles - 1), 0)),
            pl.BlockSpec((None, tm, D), lambda b, i: (b, jnp.maximum(clamp(i) - n_ctx_tiles, 0), 0))]


def _ret_inproj_kernel(ctx_ref, x_ref, mod_ref, g_ref, w_ref, cos_ref, sin_ref, q_ref, k_ref, v_ref, gf_ref, gb_ref,
                       *, n_ctx_tiles):
    x = _stream_tile(ctx_ref, x_ref, n_ctx_tiles)
    mod = mod_ref[...]
    h = (_rms(x) * g_ref[...]) * (1.0 + mod[:, D:2 * D]) + mod[:, 0:D]
    hb = h.astype(bf16)
    cos = cos_ref[...]
    sin = sin_ref[...]

    def rope(a):
        outs = []
        for half in range(2):
            sl = slice(half * LANES, (half + 1) * LANES)
            ah = a[:, sl]
            outs.append(ah * cos[:, sl] + pltpu.roll(ah, LANES // 2, axis=1) * sin[:, sl])
        return jnp.concatenate(outs, axis=1)

    for hd in range(RET_H):
        sl = slice(hd * RET_DK, (hd + 1) * RET_DK)
        q_ref[:, sl] = rope(_dot(hb, w_ref[:, sl])).astype(bf16)
    for hd in range(RET_H):
        sl = slice(hd * RET_DK, (hd + 1) * RET_DK)
        wsl = slice(D + hd * RET_DK, D + (hd + 1) * RET_DK)
        k_ref[:, sl] = (rope(_dot(hb, w_ref[:, wsl])) * (RET_DK ** -0.5)).astype(bf16)
    cw = 512
    for c in range(RET_VW // cw):
        sl = slice(c * cw, (c + 1) * cw)
        v_ref[:, sl] = _dot(hb, w_ref[:, 2 * D + c * cw:2 * D + (c + 1) * cw]).astype(bf16)
        gf_ref[:, sl] = _silu(_dot(hb, w_ref[:, 2 * D + RET_VW + c * cw:2 * D + RET_VW + (c + 1) * cw])).astype(bf16)
        gb_ref[:, sl] = _silu(_dot(hb, w_ref[:, 2 * D + 2 * RET_VW + c * cw:2 * D + 2 * RET_VW + (c + 1) * cw])).astype(bf16)


def _ret_inproj(ctx, x, modtab, gain, w_in, cos_t, sin_t, n_ctx_tiles):
    B = x.shape[0]
    T = ctx.shape[1] + x.shape[1]
    tm = ROW_TILE
    n_in = w_in.shape[1]
    tok = lambda w: pl.BlockSpec((None, tm, w), lambda b, i: (b, i, 0))
    return pl.pallas_call(
        functools.partial(_ret_inproj_kernel, n_ctx_tiles=n_ctx_tiles),
        grid=(B, T // tm),
        in_specs=_stream_specs(tm, n_ctx_tiles) + [
                  pl.BlockSpec((None, None, 1, 6 * D), lambda b, i: (b, jnp.where(i < n_ctx_tiles, 0, 1), 0, 0)),
                  pl.BlockSpec((1, D), lambda b, i: (0, 0)),
                  pl.BlockSpec((D, n_in), lambda b, i: (0, 0), pipeline_mode=pl.Buffered(1)),
                  pl.BlockSpec((tm, RET_DK), lambda b, i: (i, 0)),
                  pl.BlockSpec((tm, RET_DK), lambda b, i: (i, 0))],
        out_specs=[tok(D), tok(D), tok(RET_VW), tok(RET_VW), tok(RET_VW)],
        out_shape=[jax.ShapeDtypeStruct((B, T, D), bf16), jax.ShapeDtypeStruct((B, T, D), bf16),
                   jax.ShapeDtypeStruct((B, T, RET_VW), bf16), jax.ShapeDtypeStruct((B, T, RET_VW), bf16),
                   jax.ShapeDtypeStruct((B, T, RET_VW), bf16)],
        compiler_params=_cparams(("arbitrary", "arbitrary")),
        name="ret_inproj",
    )(ctx, x, modtab, gain, w_in, cos_t, sin_t)


def _ret_chunk_index(t, nc, ncc):
    u = t - nc
    back = jnp.where(u < ncc, ncc - 1 - u, nc - 1 - u + ncc)
    return jnp.where(t < nc, t, back)


def _ret_scan_kernel(dt_ref, q_ref, k_ref, v_ref, gf_ref, gb_ref, o_ref,
                     s_ref, of_ref, mask_ref, dq_ref, dk_ref, dc_ref, *, nc, ncc):
    t = pl.program_id(1)
    C = RET_CHUNK

    def init(direction):
        s_ref[...] = jnp.zeros_like(s_ref)
        ii = lax.broadcasted_iota(i32, (C, C), 0)
        jj = lax.broadcasted_iota(i32, (C, C), 1)
        rel = (ii - jj if direction == 0 else jj - ii).astype(f32)
        pos = lax.broadcasted_iota(i32, (C, 1), 0).astype(f32)
        for hd in range(RET_H):
            r = direction * RET_H + hd
            lg = -jnp.exp(dt_ref[r:r + 1, :])
            lg1 = lg[:, 0:1]
            mask_ref[hd] = jnp.where(rel >= 0, jnp.exp(lg1 * jnp.maximum(rel, 0.0)), 0.0)
            if direction == 0:
                dq_ref[hd] = jnp.exp(lg1 * (pos + 1.0))
                dk_ref[hd] = jnp.exp(lg1 * (C - 1.0 - pos))
            else:
                dq_ref[hd] = jnp.exp(lg1 * (C - pos))
                dk_ref[hd] = jnp.exp(lg1 * pos)
            dc_ref[hd] = jnp.exp(lg * float(C))

    pl.when(t == 0)(functools.partial(init, 0))
    pl.when(t == nc)(functools.partial(init, 1))

    row0 = pl.multiple_of(_ret_chunk_index(t, nc, ncc) * C, C)

    def step(forward):
        for hd in range(RET_H):
            ks = slice(hd * RET_DK, (hd + 1) * RET_DK)
            vs = slice(hd * RET_DV, (hd + 1) * RET_DV)
            qh = q_ref[:, ks]
            kh = k_ref[:, ks]
            vh = v_ref[:, vs]
            p = (_dot_nt(qh, kh) * mask_ref[hd]).astype(bf16)
            y = _dot(p, vh) + _dot(qh, s_ref[hd].astype(bf16)) * dq_ref[hd]
            kd = (kh.astype(f32) * dk_ref[hd]).astype(bf16)
            upd = lax.dot_general(kd, vh, (((0,), (0,)), ((), ())), preferred_element_type=f32)
            s_ref[hd] = s_ref[hd] * dc_ref[hd][0:1, 0:1] + upd
            yn = _rms(y)
            if forward:
                of_ref[pl.ds(row0, C), vs] = (gf_ref[:, vs].astype(f32) * yn).astype(bf16)
            else:
                o_ref[:, vs] = (of_ref[pl.ds(row0, C), vs].astype(f32) + gb_ref[:, vs].astype(f32) * yn).astype(bf16)

    pl.when(t < nc)(functools.partial(step, True))
    pl.when(t >= nc)(functools.partial(step, False))


def _ret_scan(dtab, q, k, v, gf, gb, n_ctx):
    B, T, _ = q.shape
    C = RET_CHUNK
    nc = T // C
    ncc = n_ctx // C
    cidx = functools.partial(_ret_chunk_index, nc=nc, ncc=ncc)
    first_back = ncc - 1
    return pl.pallas_call(
        functools.partial(_ret_scan_kernel, nc=nc, ncc=ncc),
        grid=(B, 2 * nc),
        in_specs=[pl.BlockSpec((2 * RET_H, LANES), lambda b, t: (0, 0)),
                  pl.BlockSpec((None, C, D), lambda b, t: (b, cidx(t), 0)),
                  pl.BlockSpec((None, C, D), lambda b, t: (b, cidx(t), 0)),
                  pl.BlockSpec((None, C, RET_VW), lambda b, t: (b, cidx(t), 0)),
                  pl.BlockSpec((None, C, RET_VW), lambda b, t: (b, jnp.where(t < nc, t, nc - 1), 0)),
                  pl.BlockSpec((None, C, RET_VW), lambda b, t: (b, jnp.where(t < nc, first_back, cidx(t)), 0))],
        out_specs=pl.BlockSpec((None, C, RET_VW), lambda b, t: (b, jnp.where(t < nc, first_back, cidx(t)), 0)),
        out_shape=jax.ShapeDtypeStruct((B, T, RET_VW), bf16),
        scratch_shapes=[pltpu.VMEM((RET_H, RET_DK, RET_DV), f32),
                        pltpu.VMEM((T, RET_VW), bf16),
                        pltpu.VMEM((RET_H, C, C), f32),
                        pltpu.VMEM((RET_H, C, 1), f32),
                        pltpu.VMEM((RET_H, C, 1), f32),
                        pltpu.VMEM((RET_H, 1, LANES), f32)],
        compiler_params=_cparams(("arbitrary", "arbitrary")),
        name="ret_scan",
    )(dtab, q, k, v, gf, gb)


def _route(f, rwt_ref, rb_ref, cnt_ref, e_ref, w_ref, r_ref):
    tm = f.shape[0]
    G = N_EXP // N_GRP
    logits = _dot_nt(rwt_ref[...].astype(bf16), f.astype(bf16))
    s = _sigmoid(logits)
    sel = s + rb_ref[...]
    mi = lax.broadcasted_iota(i32, (G, tm), 0)
    neg = -jnp.inf
    s_g = [s[g * G:(g + 1) * G, :] for g in range(N_GRP)]
    sel_g = [sel[g * G:(g + 1) * G, :] for g in range(N_GRP)]

    def first_max(a, ids, big):
        mx = jnp.max(a, axis=0, keepdims=True)
        ix = jnp.min(jnp.where(a == mx, ids, big), axis=0, keepdims=True)
        return mx, ix

    gscore = jnp.zeros((N_GRP, tm), f32)
    gi = lax.broadcasted_iota(i32, (N_GRP, tm), 0)
    for g in range(N_GRP):
        t1, i1 = first_max(sel_g[g], mi, G)
        t2 = jnp.max(jnp.where(mi == i1, neg, sel_g[g]), axis=0, keepdims=True)
        gscore = jnp.where(gi == g, t1 + t2, gscore)
    gmask = jnp.zeros((N_GRP, tm), i32)
    cur = gscore
    for _ in range(TOPK_GRP):
        _, ix = first_max(cur, gi, N_GRP)
        hit = gi == ix
        gmask = jnp.where(hit, 1, gmask)
        cur = jnp.where(hit, neg, cur)
    cand = [jnp.where(gmask[g:g + 1, :] > 0, sel_g[g], neg) for g in range(N_GRP)]
    ids = [mi + g * G for g in range(N_GRP)]

    def across(parts, op):
        acc = parts[0]
        for part in parts[1:]:
            acc = op(acc, part)
        return acc

    e_rows, w_rows = [], []
    for _ in range(TOP_K):
        mx = jnp.max(across(cand, jnp.maximum), axis=0, keepdims=True)
        ix = jnp.min(across([jnp.where(cand[g] == mx, ids[g], N_EXP) for g in range(N_GRP)], jnp.minimum),
                     axis=0, keepdims=True)
        hits = [ids[g] == ix for g in range(N_GRP)]
        cand = [jnp.where(hits[g], neg, cand[g]) for g in range(N_GRP)]
        wv = jnp.sum(across([jnp.where(hits[g], s_g[g], 0.0) for g in range(N_GRP)], jnp.add), axis=0, keepdims=True)
        e_rows.append(ix)
        w_rows.append(wv)
    wsum = w_rows[0]
    for r in range(1, TOP_K):
        wsum = wsum + w_rows[r]

    selm = [jnp.zeros((G, tm), f32) for _ in range(N_GRP)]
    for r in range(TOP_K):
        for g in range(N_GRP):
            selm[g] = jnp.where(ids[g] == e_rows[r], 1.0, selm[g])
    m_all = jnp.concatenate(selm, axis=0)
    ri = lax.broadcasted_iota(i32, (tm, tm), 0)
    ci = lax.broadcasted_iota(i32, (tm, tm), 1)
    upper = jnp.where(ri <= ci, 1.0, 0.0).astype(bf16)
    incl = _dot(m_all.astype(bf16), upper)
    carry = cnt_ref[:, 0:1]
    rank_all = carry + incl - m_all
    cnt_ref[...] = cnt_ref[...] + incl[:, tm - 1:tm]
    for r in range(TOP_K):
        rk = jnp.sum(across([jnp.where(ids[g] == e_rows[r], rank_all[g * G:(g + 1) * G, :], 0.0)
                             for g in range(N_GRP)], jnp.add), axis=0, keepdims=True)
        e_ref[r:r + 1, :] = e_rows[r]
        w_ref[r:r + 1, :] = w_rows[r] / wsum * ROUTED_SCALE
        r_ref[r:r + 1, :] = rk.astype(i32)


def _post_mix_kernel(o_ref, wo_ref, *refs, nt, o_transposed, split_ctx_tiles):
    n_resid = 2 if split_ctx_tiles else 1
    resid = refs[:n_resid]
    mod_ref, g_ref, rwt_ref, rb_ref, x1_ref, fin_ref, hlin_ref, e_ref, w_ref, r_ref, cnt_ref = refs[n_resid:]
    i = pl.program_id(1)

    @pl.when(i == 0)
    def _():
        cnt_ref[...] = jnp.zeros_like(cnt_ref)

    @pl.when(i < nt)
    def _():
        mod = mod_ref[...]
        if o_transposed:
            mixed = lax.dot_general(o_ref[...], wo_ref[...], (((0,), (0,)), ((), ())), preferred_element_type=f32)
        else:
            mixed = _dot(o_ref[...], wo_ref[...])
        x = _stream_tile(resid[0], resid[1], split_ctx_tiles) if split_ctx_tiles else resid[0][...]
        x1 = x + mod[:, 2 * D:3 * D] * mixed
        x1_ref[...] = x1
        f = (_rms(x1) * g_ref[...]) * (1.0 + mod[:, 4 * D:5 * D]) + mod[:, 3 * D:4 * D]
        fin_ref[...] = f.astype(bf16)
        tm = f.shape[0]
        for j in range(D // LANES):
            hlin_ref[pl.ds(j, tm, stride=D // LANES), :] = f[:, j * LANES:(j + 1) * LANES]
        _route(f, rwt_ref, rb_ref, cnt_ref, e_ref, w_ref, r_ref)

    @pl.when(i == nt)
    def _():
        hlin_ref[...] = jnp.zeros_like(hlin_ref)


def _post_mix(o, w_o, resid, x_tile_off, modtab, n_ctx_tiles, gain, rwt, rb, o_transposed=False):
    if o_transposed:
        B, KO, N = o.shape
    else:
        B, N, KO = o.shape
    tm = ROW_TILE
    nt = N // tm
    last = lambda i: jnp.minimum(i, nt - 1)
    tok = lambda w: pl.BlockSpec((None, tm, w), lambda b, i: (b, last(i), 0))
    sel = lambda: pl.BlockSpec((None, TOP_K, tm), lambda b, i: (b, 0, last(i)))
    o_spec = (pl.BlockSpec((None, KO, tm), lambda b, i: (b, 0, last(i))) if o_transposed else tok(KO))
    split = len(resid) == 2
    resid_specs = (_stream_specs(tm, n_ctx_tiles, nt - 1) if split else
                   [pl.BlockSpec((None, tm, D), lambda b, i: (b, last(i) + x_tile_off, 0))])
    return pl.pallas_call(
        functools.partial(_post_mix_kernel, nt=nt, o_transposed=o_transposed,
                          split_ctx_tiles=n_ctx_tiles if split else 0),
        grid=(B, nt + 1),
        in_specs=[o_spec,
                  pl.BlockSpec((KO, D), lambda b, i: (0, 0))] + resid_specs + [
                  pl.BlockSpec((None, None, 1, 6 * D), lambda b, i: (b, jnp.where(last(i) < n_ctx_tiles, 0, 1), 0, 0)),
                  pl.BlockSpec((1, D), lambda b, i: (0, 0)),
                  pl.BlockSpec((N_EXP, D), lambda b, i: (0, 0)),
                  pl.BlockSpec((N_EXP, 1), lambda b, i: (0, 0))],
        out_specs=[tok(D), tok(D),
                   pl.BlockSpec((None, tm * (D // LANES), LANES), lambda b, i: (b, i, 0)),
                   sel(), sel(), sel(),
                   pl.BlockSpec((None, N_EXP, LANES), lambda b, i: (b, 0, 0))],
        out_shape=[jax.ShapeDtypeStruct((B, N, D), f32), jax.ShapeDtypeStruct((B, N, D), bf16),
                   jax.ShapeDtypeStruct((B, (N + tm) * (D // LANES), LANES), f32),
                   jax.ShapeDtypeStruct((B, TOP_K, N), i32), jax.ShapeDtypeStruct((B, TOP_K, N), f32),
                   jax.ShapeDtypeStruct((B, TOP_K, N), i32),
                   jax.ShapeDtypeStruct((B, N_EXP, LANES), f32)],
        compiler_params=_cparams(("arbitrary", "arbitrary")),
        name="post_mix",
    )(o, w_o, *resid, modtab, gain, rwt, rb)


META_W = 256
PLAN_ALIGN = 1024


def _round_up(n, m):
    return -(-n // m) * m


def _moe_sizes(n_tok):
    tm = MOE_TILE
    nt_max = (n_tok * TOP_K + N_EXP * (tm - 1)) // tm + 1
    ntp = _round_up(nt_max + MOE_GROUP, PLAN_ALIGN // tm)
    assert ntp <= META_W
    return ntp, _round_up(n_tok, PLAN_ALIGN)


def _plan_kernel(e_ref, r_ref, cnt_ref, pos_ref, meta_ref, *, n_tok):
    tm = MOE_TILE
    ntile = jnp.floor((cnt_ref[...] + (tm - 1.0)) * (1.0 / tm))
    ntb = ntile.astype(bf16)
    ei = lax.broadcasted_iota(i32, (N_EXP, LANES), 0)
    ej = lax.broadcasted_iota(i32, (N_EXP, LANES), 1)
    lower = jnp.where(ej <= ei, 1.0, 0.0)[:, :N_EXP].astype(bf16)
    tend = _dot(lower, ntb)
    tstart = tend - ntile
    tt = lax.broadcasted_iota(i32, (N_EXP, META_W), 1).astype(f32)
    te = jnp.sum(jnp.where(tt >= tend[:, 0:1], 1.0, 0.0), axis=0, keepdims=True)
    meta_ref[...] = jnp.zeros_like(meta_ref)
    meta_ref[0:1, :] = jnp.minimum(te, N_EXP - 1.0).astype(i32)
    meta_ref[1:2, :] = jnp.broadcast_to(tend[N_EXP - 1:N_EXP, 0:1], (1, META_W)).astype(i32)
    e = e_ref[...]
    base = jnp.zeros(e.shape, f32)
    for ex in range(N_EXP):
        base = jnp.where(e == ex, tstart[ex:ex + 1, 0:1] * float(tm), base)
    pos_ref[...] = jnp.zeros_like(pos_ref)
    pos_ref[:, 0:n_tok] = base.astype(i32) + r_ref[...]


def _plan(e_t, r_t, cnt):
    B, K, N = e_t.shape
    _, npad = _moe_sizes(N)
    return pl.pallas_call(
        functools.partial(_plan_kernel, n_tok=N),
        grid=(B,),
        in_specs=[pl.BlockSpec((None, K, N), lambda b: (b, 0, 0)),
                  pl.BlockSpec((None, K, N), lambda b: (b, 0, 0)),
                  pl.BlockSpec((None, N_EXP, LANES), lambda b: (b, 0, 0))],
        out_specs=[pl.BlockSpec((None, K, npad), lambda b: (b, 0, 0)),
                   pl.BlockSpec((None, SUBLANES, META_W), lambda b: (b, 0, 0))],
        out_shape=[jax.ShapeDtypeStruct((B, K, npad), i32), jax.ShapeDtypeStruct((B, SUBLANES, META_W), i32)],
        compiler_params=_cparams(("arbitrary",)),
        name="moe_plan",
    )(e_t, r_t, cnt)


def _plan_invert(pos, w_t, n_tok):
    B, K, npad = pos.shape
    ntp, _ = _moe_sizes(n_tok)
    plen = ntp * MOE_TILE
    nch = D // LANES
    mesh = plsc.VectorSubcoreMesh(core_axis_name="c", subcore_axis_name="s")
    n_cores = mesh.num_cores
    assert 2 * B <= n_cores * mesh.num_subcores and n_tok % SC_LANES == 0 and plen % SC_LANES == 0

    @functools.partial(
        pl.kernel, mesh=mesh,
        out_type=[jax.ShapeDtypeStruct((B * plen,), i32), jax.ShapeDtypeStruct((B * plen,), f32)],
        scratch_types=[pltpu.VMEM((npad,), i32), pltpu.VMEM((n_tok,), f32),
                       pltpu.VMEM((plen,), i32), pltpu.VMEM((plen,), f32)],
        compiler_params=dataclasses.replace(pltpu.CompilerParams(), needs_layout_passes=False))
    def invert(pos_hbm, w_hbm, rows_hbm, ws_hbm, pos_c, w_c, rows_v, ws_v):
        wid = lax.axis_index("s") * n_cores + lax.axis_index("c")
        b = wid % B
        lane = lax.iota(i32, SC_LANES)

        def load_pos(k):
            pltpu.sync_copy(pos_hbm.at[pl.ds(pl.multiple_of((b * K + k) * npad, SUBLANES), npad)], pos_c)

        @pl.when(wid < B)
        def _():
            pad = jnp.full((SC_LANES,), n_tok * nch, i32)

            @pl.loop(0, plen, step=SC_LANES)
            def _(i):
                rows_v[pl.ds(i, SC_LANES)] = pad

            for k in range(K):
                load_pos(k)

                @pl.loop(0, n_tok, step=SC_LANES)
                def _(n):
                    plsc.store_scatter(rows_v, [pos_c[pl.ds(n, SC_LANES)]], (lane + n) * nch)

            pltpu.sync_copy(rows_v, rows_hbm.at[pl.ds(pl.multiple_of(b * plen, SUBLANES), plen)])

        @pl.when(jnp.logical_and(wid >= B, wid < 2 * B))
        def _():
            zero = jnp.zeros((SC_LANES,), f32)

            @pl.loop(0, plen, step=SC_LANES)
            def _(i):
                ws_v[pl.ds(i, SC_LANES)] = zero

            for k in range(K):
                load_pos(k)
                pltpu.sync_copy(w_hbm.at[pl.ds(pl.multiple_of((b * K + k) * n_tok, SUBLANES), n_tok)], w_c)

                @pl.loop(0, n_tok, step=SC_LANES)
                def _(n):
                    plsc.store_scatter(ws_v, [pos_c[pl.ds(n, SC_LANES)]], w_c[pl.ds(n, SC_LANES)])

            pltpu.sync_copy(ws_v, ws_hbm.at[pl.ds(pl.multiple_of(b * plen, SUBLANES), plen)])

    return invert(pos.reshape(-1), w_t.reshape(-1))


def _moe_kernel(te_ref, nt_ref, *refs, ntp, n_tok):
    R = MOE_GROUP
    rg_ref, rs_ref, ws_ref, hlin_ref = refs[0:4]
    wgu_refs, wd_refs = refs[4:4 + R], refs[4 + R:4 + 2 * R]
    out_ref, acc_ref = refs[4 + 2 * R:6 + 2 * R]
    bufs = refs[6 + 2 * R:]
    xs_refs = [bufs[0:R], bufs[R:2 * R]]
    ylin_refs = [bufs[2 * R:3 * R], bufs[3 * R:4 * R]]
    b = pl.program_id(0)
    t = pl.program_id(1)
    TM = MOE_TILE
    NCH = D // LANES
    U = 8
    TMP = TM + SUBLANES
    ngrp = ntp // R

    @pl.when(t == 0)
    def _():
        acc_ref[...] = jnp.zeros_like(acc_ref)

    @pl.when(jnp.logical_and(b == 0, t == 0))
    def _():
        for buf in bufs:
            buf[...] = jnp.zeros_like(buf)

    def gather_rows(p, r, c):
        for m in range(c * U, (c + 1) * U):
            off = pl.multiple_of(rg_ref[r * TM + m], NCH)
            xs_refs[p][r][pl.ds(m, NCH, stride=TMP), :] = hlin_ref[pl.ds(off, NCH), :]

    def scatter_rows(p, r, c):
        offs = [pl.multiple_of(rs_ref[r * TM + c * U + u], NCH) for u in range(U)]
        news = [acc_ref[pl.ds(offs[u], NCH), :] + ylin_refs[p][r][pl.ds((c * U + u) * NCH, NCH), :]
                for u in range(U)]
        for u in range(U):
            acc_ref[pl.ds(offs[u], NCH), :] = news[u]

    def stage(p):
        row_work = [functools.partial(fn, p, r, c) for c in range(TM // U) for r in range(R)
                    for fn in (gather_rows, scatter_rows)]
        n_pieces = R * (EXP_FF // LANES + D // MXU_N)
        per_piece = -(-len(row_work) // n_pieces)

        def deal():
            for fn in row_work[:per_piece]:
                fn()
            del row_work[:per_piece]

        for r in range(R):
            x = jnp.concatenate([xs_refs[1 - p][r][pl.ds(j * TMP, TM), :] for j in range(NCH)], axis=1).astype(bf16)
            wcol = jnp.broadcast_to(ws_ref[r:r + 1, :], (SUBLANES, TM)).T[:, 0:1]
            gate = _dot(x, wgu_refs[r][:, :EXP_FF])
            deal()
            up = _dot(x, wgu_refs[r][:, EXP_FF:])
            deal()
            a = (_silu(gate) * up * wcol).astype(bf16)
            for c in range(D // MXU_N):
                y = _dot(a, wd_refs[r][:, c * MXU_N:(c + 1) * MXU_N])
                for jj in range(MXU_N // LANES):
                    j = c * (MXU_N // LANES) + jj
                    ylin_refs[1 - p][r][pl.ds(j, TM, stride=NCH), :] = y[:, jj * LANES:(jj + 1) * LANES]
                deal()
        while row_work:
            deal()

    live = (t - 2) * R < nt_ref[b]
    pl.when(jnp.logical_and(live, t % 2 == 0))(functools.partial(stage, 0))
    pl.when(jnp.logical_and(live, t % 2 == 1))(functools.partial(stage, 1))

    @pl.when(t >= ngrp + 2)
    def _():
        row0 = (t - (ngrp + 2)) * (ROW_TILE * NCH)
        for j in range(NCH):
            out_ref[:, j * LANES:(j + 1) * LANES] = acc_ref[pl.ds(row0 + j, ROW_TILE, stride=NCH), :].astype(bf16)


def _moe(te, nt, rows, wsort, hlin, w_gu, w_d, layer):
    B = hlin.shape[0]
    NCH = D // LANES
    n_tok = hlin.shape[1] // NCH - ROW_TILE
    assert n_tok % ROW_TILE == 0
    nf = n_tok // ROW_TILE
    TM = MOE_TILE
    R = MOE_GROUP
    ntp, _ = _moe_sizes(n_tok)
    assert ntp % R == 0
    ngrp = ntp // R

    def group_of(b, t, nt_ref, lag):
        grp = t - lag
        ok = jnp.logical_and(t >= lag, grp * R < nt_ref[b])
        return b * ngrp + jnp.where(ok, grp, ngrp - 1)

    def rows_spec(lag):
        return pl.BlockSpec((R * TM,), lambda b, t, te_ref, nt_ref: (group_of(b, t, nt_ref, lag),),
                            memory_space=pltpu.SMEM)

    ws_spec = pl.BlockSpec((None, R, TM), lambda b, t, te_ref, nt_ref: (group_of(b, t, nt_ref, 1), 0, 0))

    def w_spec(shape, r):
        def index(b, t, te_ref, nt_ref):
            tile = jnp.clip((t - 1) * R + r, 0, nt_ref[b] - 1)
            return (layer, te_ref[b * META_W + tile], 0, 0)
        return pl.BlockSpec((None, None) + shape, index)

    grid_spec = pltpu.PrefetchScalarGridSpec(
        num_scalar_prefetch=2,
        grid=(B, ngrp + 2 + nf),
        in_specs=([rows_spec(0), rows_spec(2), ws_spec,
                   pl.BlockSpec((None, (n_tok + ROW_TILE) * NCH, LANES), lambda b, t, *_: (b, 0, 0),
                                pipeline_mode=pl.Buffered(1))]
                  + [w_spec((D, 2 * EXP_FF), r) for r in range(R)]
                  + [w_spec((EXP_FF, D), r) for r in range(R)]),
        out_specs=pl.BlockSpec((None, ROW_TILE, D), lambda b, t, *_: (b, jnp.maximum(t - (ngrp + 2), 0), 0)),
        scratch_shapes=([pltpu.VMEM(((n_tok + SUBLANES) * NCH, LANES), f32)]
                        + [pltpu.VMEM(((TM + SUBLANES) * NCH, LANES), f32) for _ in range(2 * R)]
                        + [pltpu.VMEM((TM * NCH, LANES), f32) for _ in range(2 * R)]),
    )
    return pl.pallas_call(
        functools.partial(_moe_kernel, ntp=ntp, n_tok=n_tok),
        grid_spec=grid_spec,
        out_shape=jax.ShapeDtypeStruct((B, n_tok, D), bf16),
        compiler_params=_cparams(("arbitrary", "arbitrary"), MOE_VMEM_LIMIT),
        name="moe",
    )(te, nt, rows, rows, wsort.reshape(B * ngrp, R, TM), hlin, *([w_gu] * R), *([w_d] * R))


def _routed_experts(e_t, w_t, r_t, cnt, hlin, w_gu, w_d, layer):
    N = e_t.shape[2]
    pos, meta = _plan(e_t, r_t, cnt)
    te = meta[:, 0, :].reshape(-1)
    nt = meta[:, 1, 0]
    rows, wsort = _plan_invert(pos, w_t, N)
    return _moe(te, nt, rows, wsort, hlin, w_gu, w_d, layer)


def _shared_ffn(fin, shgu_ref, shd_ref):
    gu = _dot(fin, shgu_ref[...])
    return _dot((_silu(gu[:, :SH_FF]) * gu[:, SH_FF:]).astype(bf16), shd_ref[...])


def _post_ffn_mla_kernel(x1_ref, routed_ref, fin_ref, shgu_ref, shd_ref, mod0_ref, mod1_ref, g_ref, win_ref,
                         x2_ref, a_ref):
    x2 = x1_ref[...] + mod0_ref[...][:, 5 * D:6 * D] * (routed_ref[...] + _shared_ffn(fin_ref[...], shgu_ref, shd_ref))
    x2_ref[...] = x2
    mod1 = mod1_ref[...]
    h = (_rms(x2) * g_ref[...]) * (1.0 + mod1[:, D:2 * D]) + mod1[:, 0:D]
    a_ref[...] = _dot(h.astype(bf16), win_ref[...])


def _post_ffn_mla(x1, routed, fin, sh_gu, sh_d, modtab0, modtab1, gain, w_in, n_ctx_tiles):
    B, T, _ = x1.shape
    tm = ROW_TILE
    tok = lambda w: pl.BlockSpec((None, tm, w), lambda b, i: (b, i, 0))
    modspec = lambda: pl.BlockSpec((None, None, 1, 6 * D), lambda b, i: (b, jnp.where(i < n_ctx_tiles, 0, 1), 0, 0))
    full = lambda r, c: pl.BlockSpec((r, c), lambda b, i: (0, 0))
    return pl.pallas_call(
        _post_ffn_mla_kernel,
        grid=(B, T // tm),
        in_specs=[tok(D), tok(D), tok(D), full(D, 2 * SH_FF), full(SH_FF, D), modspec(), modspec(),
                  full(1, D), full(D, MLA_IN_PAD)],
        out_specs=[tok(D), tok(MLA_IN_PAD)],
        out_shape=[jax.ShapeDtypeStruct((B, T, D), f32), jax.ShapeDtypeStruct((B, T, MLA_IN_PAD), f32)],
        compiler_params=_cparams(("arbitrary", "arbitrary")),
        name="post_ffn_mla",
    )(x1, routed, fin, sh_gu, sh_d, modtab0, modtab1, gain, w_in)


def _post_ffn_final_kernel(x1_ref, routed_ref, fin_ref, shgu_ref, shd_ref, mod_ref, out_ref):
    out_ref[...] = x1_ref[...] + mod_ref[...][:, 5 * D:6 * D] * (
        routed_ref[...] + _shared_ffn(fin_ref[...], shgu_ref, shd_ref))


def _post_ffn_final(x1, routed, fin, sh_gu, sh_d, modtab):
    B, N, _ = x1.shape
    tm = ROW_TILE
    tok = lambda w: pl.BlockSpec((None, tm, w), lambda b, i: (b, i, 0))
    full = lambda r, c: pl.BlockSpec((r, c), lambda b, i: (0, 0))
    return pl.pallas_call(
        _post_ffn_final_kernel,
        grid=(B, N // tm),
        in_specs=[tok(D), tok(D), tok(D), full(D, 2 * SH_FF), full(SH_FF, D),
                  pl.BlockSpec((None, None, 1, 6 * D), lambda b, i: (b, 1, 0, 0))],
        out_specs=tok(D),
        out_shape=jax.ShapeDtypeStruct((B, N, D), f32),
        compiler_params=_cparams(("arbitrary", "arbitrary")),
        name="post_ffn_final",
    )(x1, routed, fin, sh_gu, sh_d, modtab)


def _mla_prep_kernel(a_ref, qan_ref, wqn_ref, wqr_ref, kvan_ref, wk_ref, wv_ref, qnn_ref, qnr_ref, knn_ref, knr_ref,
                     cos_ref, sin_ref, q_ref, k_ref, v_ref):
    a = a_ref[...]
    tm = a.shape[0]
    scale = MLA_QK ** -0.5 * float(np.log2(np.e))
    cos = cos_ref[...]
    sin = sin_ref[...]
    lane = lax.broadcasted_iota(i32, (tm, LANES), 1)
    first = (lane // (MLA_ROPE // 4)) % 2 == 0

    def rope(xb):
        sw = jnp.where(first, pltpu.roll(xb, LANES - MLA_ROPE // 4, axis=1), pltpu.roll(xb, MLA_ROPE // 4, axis=1))
        return xb * cos + sw * sin

    qa = (_rms(a[:, :MLA_QR]) * qan_ref[...]).astype(bf16)
    qn = _dot(qa, wqn_ref[...])
    qr = _dot(qa, wqr_ref[...])
    ri = lax.broadcasted_iota(i32, (LANES, LANES), 0) // MLA_ROPE
    ci = lax.broadcasted_iota(i32, (LANES, LANES), 1) // MLA_ROPE
    seg = jnp.where(ri == ci, 1.0, 0.0).astype(bf16)

    def seg_sum(sq):
        hi = sq.astype(bf16)
        r1 = sq - hi.astype(f32)
        mid = r1.astype(bf16)
        lo = (r1 - mid.astype(f32)).astype(bf16)
        return _dot(hi, seg) + _dot(mid, seg) + _dot(lo, seg)

    qr_blocks = []
    for p in range(MLA_H // 2):
        blk = qr[:, p * LANES:(p + 1) * LANES]
        blk = blk * lax.rsqrt(seg_sum(blk * blk) * (1.0 / MLA_ROPE) + EPS) * qnr_ref[:, p * LANES:(p + 1) * LANES]
        qr_blocks.append(rope(blk) * scale)

    kv = (_rms(a[:, MLA_QR:MLA_QR + MLA_KVR]) * kvan_ref[...]).astype(bf16)
    kn = _dot(kv, wk_ref[...])
    v_ref[...] = _dot_nt(wv_ref[...], kv).astype(bf16)
    kr = a[:, MLA_QR + MLA_KVR:MLA_IN_PAD]
    kr = rope(_rms(kr, MLA_ROPE) * knr_ref[...])
    kr_odd = pltpu.roll(kr, MLA_ROPE, axis=1)
    for hd in range(MLA_H):
        sl = slice(hd * MLA_NOPE, (hd + 1) * MLA_NOPE)
        q_ref[:, 2 * hd * LANES:(2 * hd + 1) * LANES] = (_rms(qn[:, sl]) * qnn_ref[...] * scale).astype(bf16)
        q_ref[:, (2 * hd + 1) * LANES:(2 * hd + 2) * LANES] = qr_blocks[hd // 2].astype(bf16)
        k_ref[:, 2 * hd * LANES:(2 * hd + 1) * LANES] = (_rms(kn[:, sl]) * knn_ref[...]).astype(bf16)
        k_ref[:, (2 * hd + 1) * LANES:(2 * hd + 2) * LANES] = (kr if hd % 2 == 0 else kr_odd).astype(bf16)


def _mla_prep(a, qan, wqn, wqr, kvan, wk, wv, qnn, qnr, knn, knr, cos_t, sin_t, n_ctx_tiles):
    B, T, _ = a.shape
    tm = ROW_TILE
    tok = lambda w: pl.BlockSpec((None, tm, w), lambda b, i: (b, i, 0))
    full = lambda r, c: pl.BlockSpec((r, c), lambda b, i: (0, 0))
    hw = 2 * LANES * MLA_H
    q_spec = pl.BlockSpec((None, tm, hw), lambda b, i: (b, jnp.maximum(i - n_ctx_tiles, 0), 0))
    return pl.pallas_call(
        _mla_prep_kernel,
        grid=(B, T // tm),
        in_specs=[tok(MLA_IN_PAD), full(1, MLA_QR), full(MLA_QR, MLA_H * MLA_NOPE), full(MLA_QR, MLA_H * MLA_ROPE),
                  full(1, MLA_KVR), full(MLA_KVR, MLA_H * MLA_NOPE), full(MLA_H * MLA_V, MLA_KVR),
                  full(1, MLA_NOPE), full(1, MLA_H * MLA_ROPE), full(1, MLA_NOPE), full(1, LANES),
                  pl.BlockSpec((tm, LANES), lambda b, i: (i, 0)), pl.BlockSpec((tm, LANES), lambda b, i: (i, 0))],
        out_specs=[q_spec, tok(hw), pl.BlockSpec((None, MLA_H * MLA_V, tm), lambda b, i: (b, 0, i))],
        out_shape=[jax.ShapeDtypeStruct((B, T - n_ctx_tiles * tm, hw), bf16), jax.ShapeDtypeStruct((B, T, hw), bf16),
                   jax.ShapeDtypeStruct((B, MLA_H * MLA_V, T), bf16)],
        compiler_params=_cparams(("arbitrary", "arbitrary")),
        name="mla_prep",
    )(a, qan, wqn, wqr, kvan, wk, wv, qnn, qnr, knn, knr, cos_t, sin_t)


ATT_KV_CHUNK = 256
ATT_TQ = 256


def _mla_attn_kernel(q_ref, k_ref, vt_ref, o_ref, s0_ref, s1_ref, m0_ref, m1_ref):
    tq = q_ref.shape[0]
    ck = ATT_KV_CHUNK
    nck = k_ref.shape[0] // ck
    i = pl.program_id(0)

    s_refs, m_refs = (s0_ref, s1_ref), (m0_ref, m1_ref)

    @pl.when(i == 0)
    def _():
        for ref in s_refs + m_refs:
            ref[...] = jnp.zeros_like(ref)

    def fold(a, op):
        return op(a.reshape(ck // SUBLANES, SUBLANES, tq), axis=0)

    def stage(par):
        q = q_ref[...]
        mpart = jnp.full((SUBLANES, tq), -jnp.inf, f32)
        for j in range(nck):
            s = _dot_nt(k_ref[j * ck:(j + 1) * ck, :], q)
            s_refs[par][j * ck:(j + 1) * ck, :] = s
            mpart = jnp.maximum(mpart, fold(s, jnp.max))
        m_refs[par][...] = mpart
        m = jnp.max(m_refs[1 - par][...], axis=0, keepdims=True)
        lpart = jnp.zeros((SUBLANES, tq), f32)
        acc = jnp.zeros((MLA_V, tq), f32)
        for j in range(nck):
            p = jnp.exp2(s_refs[1 - par][j * ck:(j + 1) * ck, :] - m)
            lpart = lpart + fold(p, jnp.sum)
            acc = acc + _dot(vt_ref[:, j * ck:(j + 1) * ck], p.astype(bf16))
        o_ref[...] = (acc / jnp.sum(lpart, axis=0, keepdims=True)).astype(bf16)

    pl.when(i % 2 == 0)(functools.partial(stage, 0))
    pl.when(i % 2 == 1)(functools.partial(stage, 1))


def _mla_attn(q, k, vt):
    B, S, _ = q.shape
    T = k.shape[1]
    tq = ATT_TQ if S % ATT_TQ == 0 else ROW_TILE
    nq = S // tq
    ntile = B * MLA_H * nq
    assert T % ATT_KV_CHUNK == 0

    def tile(g):
        g = jnp.clip(g, 0, ntile - 1)
        return g // (MLA_H * nq), (g // nq) % MLA_H, g % nq

    def q_index(g):
        b, h, i = tile(g)
        return b, i, h

    def k_index(g):
        b, h, _ = tile(g)
        return b, 0, h

    def vt_index(g):
        b, h, _ = tile(g - 1)
        return b, h, 0

    def o_index(g):
        b, h, i = tile(g - 1)
        return b, h, i

    return pl.pallas_call(
        _mla_attn_kernel,
        grid=(ntile + 1,),
        in_specs=[pl.BlockSpec((None, tq, 2 * LANES), q_index),
                  pl.BlockSpec((None, T, 2 * LANES), k_index),
                  pl.BlockSpec((None, MLA_V, T), vt_index)],
        out_specs=pl.BlockSpec((None, MLA_V, tq), o_index),
        out_shape=jax.ShapeDtypeStruct((B, MLA_H * MLA_V, S), bf16),
        scratch_shapes=[pltpu.VMEM((T, tq), f32), pltpu.VMEM((T, tq), f32),
                        pltpu.VMEM((SUBLANES, tq), f32), pltpu.VMEM((SUBLANES, tq), f32)],
        compiler_params=_cparams(("arbitrary",)),
        name="mla_attn",
    )(q, k, vt)


def _axial_angles(rows_n, rot_dim):
    axis_dim = rot_dim // 2
    inv = ROPE_BASE ** (-jnp.arange(0, axis_dim, 2, dtype=f32) / axis_dim)
    row = jnp.repeat(jnp.arange(rows_n, dtype=f32), GRID_W)
    col = jnp.tile(jnp.arange(GRID_W, dtype=f32), rows_n)
    return row[:, None] * inv, col[:, None] * inv


def _rope_tables(seq, n_ctx, rot_dim, reps):
    ang_r, ang_c = _axial_angles(seq // GRID_W, rot_dim)
    cos = jnp.concatenate([jnp.cos(ang_r)] * 2 + [jnp.cos(ang_c)] * 2, axis=1)
    sin = jnp.concatenate([-jnp.sin(ang_r), jnp.sin(ang_r), -jnp.sin(ang_c), jnp.sin(ang_c)], axis=1)
    cos = jnp.concatenate([jnp.ones((n_ctx, rot_dim), f32), cos], axis=0)
    sin = jnp.concatenate([jnp.zeros((n_ctx, rot_dim), f32), sin], axis=0)
    return jnp.tile(cos, (1, reps)), jnp.tile(sin, (1, reps))


def kernel(x, c, ctx, c_ctx, ada_w, ada_b, norm_mix, norm_ffn, ret_w_in, ret_decay_f, ret_decay_b, ret_w_o,
           mla_w_in, mla_q_a_norm, mla_w_q_b, mla_kv_a_norm, mla_w_kv_b, mla_q_norm, mla_k_norm, mla_w_o,
           router_w, router_bias, exp_w_gu, exp_w_down, sh_w_gu, sh_w_down):
    B, S, _ = x.shape
    n_ctx = ctx.shape[1]
    assert n_ctx % ROW_TILE == 0 and S % ROW_TILE == 0 and S % GRID_W == 0
    n_ctx_tiles = n_ctx // ROW_TILE

    rows = -(-(B + 1) // SUBLANES) * SUBLANES
    cc = jnp.zeros((rows, D), f32).at[:B].set(c).at[B].set(c_ctx)
    mod = _ada(cc, ada_w, ada_b)

    def modtab(i):
        ctx_row = jnp.broadcast_to(mod[i, B][None, :], (B, 6 * D))
        return jnp.stack([ctx_row, mod[i, :B]], axis=1)[:, :, None, :]

    mod0, mod1 = modtab(0), modtab(1)

    cos_r, sin_r = _rope_tables(S, n_ctx, RET_DK, 1)
    q, k, v, gf, gb = _ret_inproj(ctx, x, mod0, norm_mix[0][None, :], ret_w_in[0].astype(bf16), cos_r, sin_r,
                                  n_ctx_tiles)
    dtab = jnp.broadcast_to(jnp.concatenate([ret_decay_f[0], ret_decay_b[0]])[:, None], (2 * RET_H, LANES))
    o = _ret_scan(dtab, q, k, v, gf, gb, n_ctx)
    x1, fin, hlin, e_t, w_t, r_t, cnt = _post_mix(
        o, ret_w_o[0].astype(bf16), (ctx, x), 0, mod0, n_ctx_tiles, norm_ffn[0][None, :],
        router_w[0].T, router_bias[0][:, None])
    exp_gu, exp_d = exp_w_gu.astype(bf16), exp_w_down.astype(bf16)
    routed = _routed_experts(e_t, w_t, r_t, cnt, hlin, exp_gu, exp_d, 0)
    w_in1 = jnp.zeros((D, MLA_IN_PAD), f32).at[:, :mla_w_in.shape[2]].set(mla_w_in[0]).astype(bf16)
    x2, a = _post_ffn_mla(x1, routed, fin, sh_w_gu[0].astype(bf16), sh_w_down[0].astype(bf16), mod0, mod1,
                          norm_mix[1][None, :], w_in1, n_ctx_tiles)

    wq = mla_w_q_b[0].reshape(MLA_QR, MLA_H, MLA_QK)
    wqn = wq[:, :, :MLA_NOPE].reshape(MLA_QR, MLA_H * MLA_NOPE).astype(bf16)
    wqr = wq[:, :, MLA_NOPE:].reshape(MLA_QR, MLA_H * MLA_ROPE).astype(bf16)
    wkv = mla_w_kv_b[0].reshape(MLA_KVR, MLA_H, MLA_NOPE + MLA_V)
    wk = wkv[:, :, :MLA_NOPE].reshape(MLA_KVR, MLA_H * MLA_NOPE).astype(bf16)
    wv = wkv[:, :, MLA_NOPE:].reshape(MLA_KVR, MLA_H * MLA_V).T.astype(bf16)
    qnn = mla_q_norm[0][None, :MLA_NOPE]
    qnr = jnp.tile(mla_q_norm[0][None, MLA_NOPE:], (1, MLA_H))
    knn = mla_k_norm[0][None, :MLA_NOPE]
    knr = jnp.concatenate([mla_k_norm[0][MLA_NOPE:], jnp.zeros((LANES - MLA_ROPE,), f32)])[None, :]
    cos_m, sin_m = _rope_tables(S, n_ctx, MLA_ROPE, LANES // MLA_ROPE)
    qf, kf, vf = _mla_prep(a, mla_q_a_norm[0][None, :], wqn, wqr, mla_kv_a_norm[0][None, :], wk, wv,
                           qnn, qnr, knn, knr, cos_m, sin_m, n_ctx_tiles)
    o1 = _mla_attn(qf, kf, vf)
    x3, fin1, hlin1, e1, w1, r1, cnt1 = _post_mix(
        o1, mla_w_o[0].astype(bf16), (x2,), n_ctx_tiles, mod1, 0, norm_ffn[1][None, :],
        router_w[1].T, router_bias[1][:, None], o_transposed=True)
    routed1 = _routed_experts(e1, w1, r1, cnt1, hlin1, exp_gu, exp_d, 1)
    return _post_ffn_final(x3, routed1, fin1, sh_w_gu[1].astype(bf16), sh_w_down[1].astype(bf16), mod1)
```

```python
import dataclasses
import functools

import jax
import jax.numpy as jnp
import numpy as np
from jax import lax
from jax.experimental import pallas as pl
from jax.experimental.pallas import tpu as pltpu
from jax.experimental.pallas import tpu_sc as plsc

f32 = jnp.float32
bf16 = jnp.bfloat16
i32 = jnp.int32

D = 1024
GRID_W = 64
EPS = 1e-6
ROPE_BASE = 10000.0
RET_H = 4
RET_DK = 256
RET_DV = 512
RET_VW = RET_H * RET_DV
RET_CHUNK = 256
MLA_H = 8
MLA_NOPE = 128
MLA_ROPE = 64
MLA_QK = MLA_NOPE + MLA_ROPE
MLA_V = 128
MLA_QR = 384
MLA_KVR = 256
MLA_IN_PAD = 768
N_EXP = 64
TOP_K = 8
N_GRP = 8
TOPK_GRP = 4
EXP_FF = 256
SH_FF = 256
ROUTED_SCALE = 2.5

LANES = 128
SUBLANES = 8
SC_LANES = 16
MXU_N = 256
ROW_TILE = 256
MOE_TILE = 256
MOE_GROUP = 2
VMEM_LIMIT = 56 * 1024 * 1024
MOE_VMEM_LIMIT = 62 * 1024 * 1024


def _cparams(sem, vmem=VMEM_LIMIT):
    return pltpu.CompilerParams(dimension_semantics=sem, vmem_limit_bytes=vmem)


def _sigmoid(x):
    return 1.0 / (1.0 + jnp.exp(-x))


def _silu(x):
    return x * _sigmoid(x)


def _rms(x, n=None):
    n = x.shape[-1] if n is None else n
    return x * lax.rsqrt(jnp.sum(x * x, axis=-1, keepdims=True) * (1.0 / n) + EPS)


def _dot(a, b):
    return jnp.dot(a, b, preferred_element_type=f32)


def _dot_nt(a, b, precision=None):
    return lax.dot_general(a, b, (((1,), (1,)), ((), ())), preferred_element_type=f32, precision=precision)


def _ada_kernel(c_ref, w_ref, b_ref, o_ref):
    s = _silu(c_ref[...]).astype(bf16)
    o_ref[...] = _dot(s, w_ref[...].astype(bf16)) + b_ref[...]


def _ada(cc, ada_w, ada_b):
    depth = ada_w.shape[0]
    rows = cc.shape[0]
    tn = 1536
    return pl.pallas_call(
        _ada_kernel,
        grid=(depth, 6 * D // tn),
        in_specs=[pl.BlockSpec((rows, D), lambda i, j: (0, 0)),
                  pl.BlockSpec((None, D, tn), lambda i, j: (i, 0, j)),
                  pl.BlockSpec((None, 1, tn), lambda i, j: (i, 0, j))],
        out_specs=pl.BlockSpec((None, rows, tn), lambda i, j: (i, 0, j)),
        out_shape=jax.ShapeDtypeStruct((depth, rows, 6 * D), f32),
        compiler_params=_cparams(("arbitrary", "arbitrary")),
        name="ada",
    )(cc, ada_w, ada_b.reshape(depth, 1, 6 * D))


def _stream_tile(ctx_ref, x_ref, n_ctx_tiles):
    return jnp.where(pl.program_id(1) < n_ctx_tiles, ctx_ref[...], x_ref[...])


def _stream_specs(tm, n_ctx_tiles, last=None):
    clamp = (lambda i: i) if last is None else (lambda i: jnp.minimum(i, last))
    return [pl.BlockSpec((None, tm, D), lambda b, i: (b, jnp.minimum(clamp(i), n_ctx_tiles - 1), 0)),
            pl.BlockSpec((None, tm, D), lambda b, i: (b, jnp.maximum(clamp(i) - n_ctx_tiles, 0), 0))]


def _ret_inproj_kernel(ctx_ref, x_ref, mod_ref, g_ref, w_ref, cos_ref, sin_ref, q_ref, k_ref, v_ref, gf_ref, gb_ref,
                       *, n_ctx_tiles):
    x = _stream_tile(ctx_ref, x_ref, n_ctx_tiles)
    mod = mod_ref[...]
    h = (_rms(x) * g_ref[...]) * (1.0 + mod[:, D:2 * D]) + mod[:, 0:D]
    hb = h.astype(bf16)
    cos = cos_ref[...]
    sin = sin_ref[...]

    def rope(a):
        outs = []
        for half in range(2):
            sl = slice(half * LANES, (half + 1) * LANES)
            ah = a[:, sl]
            outs.append(ah * cos[:, sl] + pltpu.roll(ah, LANES // 2, axis=1) * sin[:, sl])
        return jnp.concatenate(outs, axis=1)

    for hd in range(RET_H):
        sl = slice(hd * RET_DK, (hd + 1) * RET_DK)
        q_ref[:, sl] = rope(_dot(hb, w_ref[:, sl])).astype(bf16)
    for hd in range(RET_H):
        sl = slice(hd * RET_DK, (hd + 1) * RET_DK)
        wsl = slice(D + hd * RET_DK, D + (hd + 1) * RET_DK)
        k_ref[:, sl] = (rope(_dot(hb, w_ref[:, wsl])) * (RET_DK ** -0.5)).astype(bf16)
    cw = 512
    for c in range(RET_VW // cw):
        sl = slice(c * cw, (c + 1) * cw)
        v_ref[:, sl] = _dot(hb, w_ref[:, 2 * D + c * cw:2 * D + (c + 1) * cw]).astype(bf16)
        gf_ref[:, sl] = _silu(_dot(hb, w_ref[:, 2 * D + RET_VW + c * cw:2 * D + RET_VW + (c + 1) * cw])).astype(bf16)
        gb_ref[:, sl] = _silu(_dot(hb, w_ref[:, 2 * D + 2 * RET_VW + c * cw:2 * D + 2 * RET_VW + (c + 1) * cw])).astype(bf16)


def _ret_inproj(ctx, x, modtab, gain, w_in, cos_t, sin_t, n_ctx_tiles):
    B = x.shape[0]
    T = ctx.shape[1] + x.shape[1]
    tm = ROW_TILE
    n_in = w_in.shape[1]
    tok = lambda w: pl.BlockSpec((None, tm, w), lambda b, i: (b, i, 0))
    return pl.pallas_call(
        functools.partial(_ret_inproj_kernel, n_ctx_tiles=n_ctx_tiles),
        grid=(B, T // tm),
        in_specs=_stream_specs(tm, n_ctx_tiles) + [
                  pl.BlockSpec((None, None, 1, 6 * D), lambda b, i: (b, jnp.where(i < n_ctx_tiles, 0, 1), 0, 0)),
                  pl.BlockSpec((1, D), lambda b, i: (0, 0)),
                  pl.BlockSpec((D, n_in), lambda b, i: (0, 0), pipeline_mode=pl.Buffered(1)),
                  pl.BlockSpec((tm, RET_DK), lambda b, i: (i, 0)),
                  pl.BlockSpec((tm, RET_DK), lambda b, i: (i, 0))],
        out_specs=[tok(D), tok(D), tok(RET_VW), tok(RET_VW), tok(RET_VW)],
        out_shape=[jax.ShapeDtypeStruct((B, T, D), bf16), jax.ShapeDtypeStruct((B, T, D), bf16),
                   jax.ShapeDtypeStruct((B, T, RET_VW), bf16), jax.ShapeDtypeStruct((B, T, RET_VW), bf16),
                   jax.ShapeDtypeStruct((B, T, RET_VW), bf16)],
        compiler_params=_cparams(("arbitrary", "arbitrary")),
        name="ret_inproj",
    )(ctx, x, modtab, gain, w_in, cos_t, sin_t)


def _ret_chunk_index(t, nc, ncc):
    u = t - nc
    back = jnp.where(u < ncc, ncc - 1 - u, nc - 1 - u + ncc)
    return jnp.where(t < nc, t, back)


def _ret_scan_kernel(dt_ref, q_ref, k_ref, v_ref, gf_ref, gb_ref, o_ref,
                     s_ref, of_ref, mask_ref, dq_ref, dk_ref, dc_ref, *, nc, ncc):
    t = pl.program_id(1)
    C = RET_CHUNK

    def init(direction):
        s_ref[...] = jnp.zeros_like(s_ref)
        ii = lax.broadcasted_iota(i32, (C, C), 0)
        jj = lax.broadcasted_iota(i32, (C, C), 1)
        rel = (ii - jj if direction == 0 else jj - ii).astype(f32)
        pos = lax.broadcasted_iota(i32, (C, 1), 0).astype(f32)
        for hd in range(RET_H):
            r = direction * RET_H + hd
            lg = -jnp.exp(dt_ref[r:r + 1, :])
            lg1 = lg[:, 0:1]
            mask_ref[hd] = jnp.where(rel >= 0, jnp.exp(lg1 * jnp.maximum(rel, 0.0)), 0.0)
            if direction == 0:
                dq_ref[hd] = jnp.exp(lg1 * (pos + 1.0))
                dk_ref[hd] = jnp.exp(lg1 * (C - 1.0 - pos))
            else:
                dq_ref[hd] = jnp.exp(lg1 * (C - pos))
                dk_ref[hd] = jnp.exp(lg1 * pos)
            dc_ref[hd] = jnp.exp(lg * float(C))

    pl.when(t == 0)(functools.partial(init, 0))
    pl.when(t == nc)(functools.partial(init, 1))

    row0 = pl.multiple_of(_ret_chunk_index(t, nc, ncc) * C, C)

    def step(forward):
        for hd in range(RET_H):
            ks = slice(hd * RET_DK, (hd + 1) * RET_DK)
            vs = slice(hd * RET_DV, (hd + 1) * RET_DV)
            qh = q_ref[:, ks]
            kh = k_ref[:, ks]
            vh = v_ref[:, vs]
            p = (_dot_nt(qh, kh) * mask_ref[hd]).astype(bf16)
            y = _dot(p, vh) + _dot(qh, s_ref[hd].astype(bf16)) * dq_ref[hd]
            kd = (kh.astype(f32) * dk_ref[hd]).astype(bf16)
            upd = lax.dot_general(kd, vh, (((0,), (0,)), ((), ())), preferred_element_type=f32)
            s_ref[hd] = s_ref[hd] * dc_ref[hd][0:1, 0:1] + upd
            yn = _rms(y)
            if forward:
                of_ref[pl.ds(row0, C), vs] = (gf_ref[:, vs].astype(f32) * yn).astype(bf16)
            else:
                o_ref[:, vs] = (of_ref[pl.ds(row0, C), vs].astype(f32) + gb_ref[:, vs].astype(f32) * yn).astype(bf16)

    pl.when(t < nc)(functools.partial(step, True))
    pl.when(t >= nc)(functools.partial(step, False))


def _ret_scan(dtab, q, k, v, gf, gb, n_ctx):
    B, T, _ = q.shape
    C = RET_CHUNK
    nc = T // C
    ncc = n_ctx // C
    cidx = functools.partial(_ret_chunk_index, nc=nc, ncc=ncc)
    first_back = ncc - 1
    return pl.pallas_call(
        functools.partial(_ret_scan_kernel, nc=nc, ncc=ncc),
        grid=(B, 2 * nc),
        in_specs=[pl.BlockSpec((2 * RET_H, LANES), lambda b, t: (0, 0)),
                  pl.BlockSpec((None, C, D), lambda b, t: (b, cidx(t), 0)),
                  pl.BlockSpec((None, C, D), lambda b, t: (b, cidx(t), 0)),
                  pl.BlockSpec((None, C, RET_VW), lambda b, t: (b, cidx(t), 0)),
                  pl.BlockSpec((None, C, RET_VW), lambda b, t: (b, jnp.where(t < nc, t, nc - 1), 0)),
                  pl.BlockSpec((None, C, RET_VW), lambda b, t: (b, jnp.where(t < nc, first_back, cidx(t)), 0))],
        out_specs=pl.BlockSpec((None, C, RET_VW), lambda b, t: (b, jnp.where(t < nc, first_back, cidx(t)), 0)),
        out_shape=jax.ShapeDtypeStruct((B, T, RET_VW), bf16),
        scratch_shapes=[pltpu.VMEM((RET_H, RET_DK, RET_DV), f32),
                        pltpu.VMEM((T, RET_VW), bf16),
                        pltpu.VMEM((RET_H, C, C), f32),
                        pltpu.VMEM((RET_H, C, 1), f32),
                        pltpu.VMEM((RET_H, C, 1), f32),
                        pltpu.VMEM((RET_H, 1, LANES), f32)],
        compiler_params=_cparams(("arbitrary", "arbitrary")),
        name="ret_scan",
    )(dtab, q, k, v, gf, gb)


def _route(f, rwt_ref, rb_ref, cnt_ref, e_ref, w_ref, r_ref):
    tm = f.shape[0]
    G = N_EXP // N_GRP
    logits = _dot_nt(rwt_ref[...].astype(bf16), f.astype(bf16))
    s = _sigmoid(logits)
    sel = s + rb_ref[...]
    mi = lax.broadcasted_iota(i32, (G, tm), 0)
    neg = -jnp.inf
    s_g = [s[g * G:(g + 1) * G, :] for g in range(N_GRP)]
    sel_g = [sel[g * G:(g + 1) * G, :] for g in range(N_GRP)]

    def first_max(a, ids, big):
        mx = jnp.max(a, axis=0, keepdims=True)
        ix = jnp.min(jnp.where(a == mx, ids, big), axis=0, keepdims=True)
        return mx, ix

    gscore = jnp.zeros((N_GRP, tm), f32)
    gi = lax.broadcasted_iota(i32, (N_GRP, tm), 0)
    for g in range(N_GRP):
        t1, i1 = first_max(sel_g[g], mi, G)
        t2 = jnp.max(jnp.where(mi == i1, neg, sel_g[g]), axis=0, keepdims=True)
        gscore = jnp.where(gi == g, t1 + t2, gscore)
    gmask = jnp.zeros((N_GRP, tm), i32)
    cur = gscore
    for _ in range(TOPK_GRP):
        _, ix = first_max(cur, gi, N_GRP)
        hit = gi == ix
        gmask = jnp.where(hit, 1, gmask)
        cur = jnp.where(hit, neg, cur)
    cand = [jnp.where(gmask[g:g + 1, :] > 0, sel_g[g], neg) for g in range(N_GRP)]
    ids = [mi + g * G for g in range(N_GRP)]

    def across(parts, op):
        acc = parts[0]
        for part in parts[1:]:
            acc = op(acc, part)
        return acc

    e_rows, w_rows = [], []
    for _ in range(TOP_K):
        mx = jnp.max(across(cand, jnp.maximum), axis=0, keepdims=True)
        ix = jnp.min(across([jnp.where(cand[g] == mx, ids[g], N_EXP) for g in range(N_GRP)], jnp.minimum),
                     axis=0, keepdims=True)
        hits = [ids[g] == ix for g in range(N_GRP)]
        cand = [jnp.where(hits[g], neg, cand[g]) for g in range(N_GRP)]
        wv = jnp.sum(across([jnp.where(hits[g], s_g[g], 0.0) for g in range(N_GRP)], jnp.add), axis=0, keepdims=True)
        e_rows.append(ix)
        w_rows.append(wv)
    wsum = w_rows[0]
    for r in range(1, TOP_K):
        wsum = wsum + w_rows[r]

    selm = [jnp.zeros((G, tm), f32) for _ in range(N_GRP)]
    for r in range(TOP_K):
        for g in range(N_GRP):
            selm[g] = jnp.where(ids[g] == e_rows[r], 1.0, selm[g])
    m_all = jnp.concatenate(selm, axis=0)
    ri = lax.broadcasted_iota(i32, (tm, tm), 0)
    ci = lax.broadcasted_iota(i32, (tm, tm), 1)
    upper = jnp.where(ri <= ci, 1.0, 0.0).astype(bf16)
    incl = _dot(m_all.astype(bf16), upper)
    carry = cnt_ref[:, 0:1]
    rank_all = carry + incl - m_all
    cnt_ref[...] = cnt_ref[...] + incl[:, tm - 1:tm]
    for r in range(TOP_K):
        rk = jnp.sum(across([jnp.where(ids[g] == e_rows[r], rank_all[g * G:(g + 1) * G, :], 0.0)
                             for g in range(N_GRP)], jnp.add), axis=0, keepdims=True)
        e_ref[r:r + 1, :] = e_rows[r]
        w_ref[r:r + 1, :] = w_rows[r] / wsum * ROUTED_SCALE
        r_ref[r:r + 1, :] = rk.astype(i32)


def _post_mix_kernel(o_ref, wo_ref, *refs, nt, o_transposed, split_ctx_tiles):
    n_resid = 2 if split_ctx_tiles else 1
    resid = refs[:n_resid]
    mod_ref, g_ref, rwt_ref, rb_ref, x1_ref, fin_ref, hlin_ref, e_ref, w_ref, r_ref, cnt_ref = refs[n_resid:]
    i = pl.program_id(1)

    @pl.when(i == 0)
    def _():
        cnt_ref[...] = jnp.zeros_like(cnt_ref)

    @pl.when(i < nt)
    def _():
        mod = mod_ref[...]
        if o_transposed:
            mixed = lax.dot_general(o_ref[...], wo_ref[...], (((0,), (0,)), ((), ())), preferred_element_type=f32)
        else:
            mixed = _dot(o_ref[...], wo_ref[...])
        x = _stream_tile(resid[0], resid[1], split_ctx_tiles) if split_ctx_tiles else resid[0][...]
        x1 = x + mod[:, 2 * D:3 * D] * mixed
        x1_ref[...] = x1
        f = (_rms(x1) * g_ref[...]) * (1.0 + mod[:, 4 * D:5 * D]) + mod[:, 3 * D:4 * D]
        fin_ref[...] = f.astype(bf16)
        tm = f.shape[0]
        for j in range(D // LANES):
            hlin_ref[pl.ds(j, tm, stride=D // LANES), :] = f[:, j * LANES:(j + 1) * LANES]
        _route(f, rwt_ref, rb_ref, cnt_ref, e_ref, w_ref, r_ref)

    @pl.when(i == nt)
    def _():
        hlin_ref[...] = jnp.zeros_like(hlin_ref)


def _post_mix(o, w_o, resid, x_tile_off, modtab, n_ctx_tiles, gain, rwt, rb, o_transposed=False):
    if o_transposed:
        B, KO, N = o.shape
    else:
        B, N, KO = o.shape
    tm = ROW_TILE
    nt = N // tm
    last = lambda i: jnp.minimum(i, nt - 1)
    tok = lambda w: pl.BlockSpec((None, tm, w), lambda b, i: (b, last(i), 0))
    sel = lambda: pl.BlockSpec((None, TOP_K, tm), lambda b, i: (b, 0, last(i)))
    o_spec = (pl.BlockSpec((None, KO, tm), lambda b, i: (b, 0, last(i))) if o_transposed else tok(KO))
    split = len(resid) == 2
    resid_specs = (_stream_specs(tm, n_ctx_tiles, nt - 1) if split else
                   [pl.BlockSpec((None, tm, D), lambda b, i: (b, last(i) + x_tile_off, 0))])
    return pl.pallas_call(
        functools.partial(_post_mix_kernel, nt=nt, o_transposed=o_transposed,
                          split_ctx_tiles=n_ctx_tiles if split else 0),
        grid=(B, nt + 1),
        in_specs=[o_spec,
                  pl.BlockSpec((KO, D), lambda b, i: (0, 0))] + resid_specs + [
                  pl.BlockSpec((None, None, 1, 6 * D), lambda b, i: (b, jnp.where(last(i) < n_ctx_tiles, 0, 1), 0, 0)),
                  pl.BlockSpec((1, D), lambda b, i: (0, 0)),
                  pl.BlockSpec((N_EXP, D), lambda b, i: (0, 0)),
                  pl.BlockSpec((N_EXP, 1), lambda b, i: (0, 0))],
        out_specs=[tok(D), tok(D),
                   pl.BlockSpec((None, tm * (D // LANES), LANES), lambda b, i: (b, i, 0)),
                   sel(), sel(), sel(),
                   pl.BlockSpec((None, N_EXP, LANES), lambda b, i: (b, 0, 0))],
        out_shape=[jax.ShapeDtypeStruct((B, N, D), f32), jax.ShapeDtypeStruct((B, N, D), bf16),
                   jax.ShapeDtypeStruct((B, (N + tm) * (D // LANES), LANES), f32),
                   jax.ShapeDtypeStruct((B, TOP_K, N), i32), jax.ShapeDtypeStruct((B, TOP_K, N), f32),
                   jax.ShapeDtypeStruct((B, TOP_K, N), i32),
                   jax.ShapeDtypeStruct((B, N_EXP, LANES), f32)],
        compiler_params=_cparams(("arbitrary", "arbitrary")),
        name="post_mix",
    )(o, w_o, *resid, modtab, gain, rwt, rb)


META_W = 256
PLAN_ALIGN = 1024


def _round_up(n, m):
    return -(-n // m) * m


def _moe_sizes(n_tok):
    tm = MOE_TILE
    nt_max = (n_tok * TOP_K + N_EXP * (tm - 1)) // tm + 1
    ntp = _round_up(nt_max + MOE_GROUP, int(np.lcm(PLAN_ALIGN // tm, MOE_GROUP)))
    assert ntp <= META_W
    return ntp, _round_up(n_tok, PLAN_ALIGN)


def _plan_kernel(e_ref, r_ref, cnt_ref, pos_ref, meta_ref, *, n_tok):
    tm = MOE_TILE
    ntile = jnp.floor((cnt_ref[...] + (tm - 1.0)) * (1.0 / tm))
    ntb = ntile.astype(bf16)
    ei = lax.broadcasted_iota(i32, (N_EXP, LANES), 0)
    ej = lax.broadcasted_iota(i32, (N_EXP, LANES), 1)
    lower = jnp.where(ej <= ei, 1.0, 0.0)[:, :N_EXP].astype(bf16)
    tend = _dot(lower, ntb)
    tstart = tend - ntile
    tt = lax.broadcasted_iota(i32, (N_EXP, META_W), 1).astype(f32)
    te = jnp.sum(jnp.where(tt >= tend[:, 0:1], 1.0, 0.0), axis=0, keepdims=True)
    meta_ref[...] = jnp.zeros_like(meta_ref)
    meta_ref[0:1, :] = jnp.minimum(te, N_EXP - 1.0).astype(i32)
    meta_ref[1:2, :] = jnp.broadcast_to(tend[N_EXP - 1:N_EXP, 0:1], (1, META_W)).astype(i32)
    e = e_ref[...]
    base = jnp.zeros(e.shape, f32)
    for ex in range(N_EXP):
        base = jnp.where(e == ex, tstart[ex:ex + 1, 0:1] * float(tm), base)
    pos_ref[...] = jnp.zeros_like(pos_ref)
    pos_ref[:, 0:n_tok] = base.astype(i32) + r_ref[...]


def _plan(e_t, r_t, cnt):
    B, K, N = e_t.shape
    _, npad = _moe_sizes(N)
    return pl.pallas_call(
        functools.partial(_plan_kernel, n_tok=N),
        grid=(B,),
        in_specs=[pl.BlockSpec((None, K, N), lambda b: (b, 0, 0)),
                  pl.BlockSpec((None, K, N), lambda b: (b, 0, 0)),
                  pl.BlockSpec((None, N_EXP, LANES), lambda b: (b, 0, 0))],
        out_specs=[pl.BlockSpec((None, K, npad), lambda b: (b, 0, 0)),
                   pl.BlockSpec((None, SUBLANES, META_W), lambda b: (b, 0, 0))],
        out_shape=[jax.ShapeDtypeStruct((B, K, npad), i32), jax.ShapeDtypeStruct((B, SUBLANES, META_W), i32)],
        compiler_params=_cparams(("arbitrary",)),
        name="moe_plan",
    )(e_t, r_t, cnt)


def _plan_invert(pos, w_t, n_tok):
    B, K, npad = pos.shape
    ntp, _ = _moe_sizes(n_tok)
    plen = ntp * MOE_TILE
    nch = D // LANES
    mesh = plsc.VectorSubcoreMesh(core_axis_name="c", subcore_axis_name="s")
    n_cores = mesh.num_cores
    assert 2 * B <= n_cores * mesh.num_subcores and n_tok % SC_LANES == 0 and plen % SC_LANES == 0

    @functools.partial(
        pl.kernel, mesh=mesh,
        out_type=[jax.ShapeDtypeStruct((B * plen,), i32), jax.ShapeDtypeStruct((B * plen,), f32)],
        scratch_types=[pltpu.VMEM((npad,), i32), pltpu.VMEM((n_tok,), f32),
                       pltpu.VMEM((plen,), i32), pltpu.VMEM((plen,), f32)],
        compiler_params=dataclasses.replace(pltpu.CompilerParams(), needs_layout_passes=False))
    def invert(pos_hbm, w_hbm, rows_hbm, ws_hbm, pos_c, w_c, rows_v, ws_v):
        wid = lax.axis_index("s") * n_cores + lax.axis_index("c")
        b = wid % B
        lane = lax.iota(i32, SC_LANES)

        def load_pos(k):
            pltpu.sync_copy(pos_hbm.at[pl.ds(pl.multiple_of((b * K + k) * npad, SUBLANES), npad)], pos_c)

        @pl.when(wid < B)
        def _():
            pad = jnp.full((SC_LANES,), n_tok * nch, i32)

            @pl.loop(0, plen, step=SC_LANES)
            def _(i):
                rows_v[pl.ds(i, SC_LANES)] = pad

            for k in range(K):
                load_pos(k)

                @pl.loop(0, n_tok, step=SC_LANES)
                def _(n):
                    plsc.store_scatter(rows_v, [pos_c[pl.ds(n, SC_LANES)]], (lane + n) * nch)

            pltpu.sync_copy(rows_v, rows_hbm.at[pl.ds(pl.multiple_of(b * plen, SUBLANES), plen)])

        @pl.when(jnp.logical_and(wid >= B, wid < 2 * B))
        def _():
            zero = jnp.zeros((SC_LANES,), f32)

            @pl.loop(0, plen, step=SC_LANES)
            def _(i):
                ws_v[pl.ds(i, SC_LANES)] = zero

            for k in range(K):
                load_pos(k)
                pltpu.sync_copy(w_hbm.at[pl.ds(pl.multiple_of((b * K + k) * n_tok, SUBLANES), n_tok)], w_c)

                @pl.loop(0, n_tok, step=SC_LANES)
                def _(n):
                    plsc.store_scatter(ws_v, [pos_c[pl.ds(n, SC_LANES)]], w_c[pl.ds(n, SC_LANES)])

            pltpu.sync_copy(ws_v, ws_hbm.at[pl.ds(pl.multiple_of(b * plen, SUBLANES), plen)])

    return invert(pos.reshape(-1), w_t.reshape(-1))


def _moe_kernel(te_ref, nt_ref, *refs, ntp, n_tok):
    R = MOE_GROUP
    rg_ref, rs_ref, ws_ref, hlin_ref = refs[0:4]
    wgu_refs, wd_refs = refs[4:4 + R], refs[4 + R:4 + 2 * R]
    out_ref, acc_ref = refs[4 + 2 * R:6 + 2 * R]
    bufs = refs[6 + 2 * R:]
    xs_refs = [bufs[0:R], bufs[R:2 * R]]
    ylin_refs = [bufs[2 * R:3 * R], bufs[3 * R:4 * R]]
    b = pl.program_id(0)
    t = pl.program_id(1)
    TM = MOE_TILE
    NCH = D // LANES
    U = 8
    TMP = TM + SUBLANES
    ngrp = ntp // R

    @pl.when(t == 0)
    def _():
        acc_ref[...] = jnp.zeros_like(acc_ref)

    @pl.when(jnp.logical_and(b == 0, t == 0))
    def _():
        for buf in bufs:
            buf[...] = jnp.zeros_like(buf)

    def gather_rows(p, r, c):
        for m in range(c * U, (c + 1) * U):
            off = pl.multiple_of(rg_ref[r * TM + m], NCH)
            xs_refs[p][r][pl.ds(m, NCH, stride=TMP), :] = hlin_ref[pl.ds(off, NCH), :]

    def scatter_rows(p, r, c):
        offs = [pl.multiple_of(rs_ref[r * TM + c * U + u], NCH) for u in range(U)]
        news = [acc_ref[pl.ds(offs[u], NCH), :] + ylin_refs[p][r][pl.ds((c * U + u) * NCH, NCH), :]
                for u in range(U)]
        for u in range(U):
            acc_ref[pl.ds(offs[u], NCH), :] = news[u]

    def stage(p):
        row_work = [functools.partial(fn, p, r, c) for c in range(TM // U) for r in range(R)
                    for fn in (gather_rows, scatter_rows)]
        n_pieces = R * (EXP_FF // LANES + D // MXU_N)
        per_piece = -(-len(row_work) // n_pieces)

        def deal():
            for fn in row_work[:per_piece]:
                fn()
            del row_work[:per_piece]

        for r in range(R):
            x = jnp.concatenate([xs_refs[1 - p][r][pl.ds(j * TMP, TM), :] for j in range(NCH)], axis=1).astype(bf16)
            wcol = jnp.broadcast_to(ws_ref[r:r + 1, :], (SUBLANES, TM)).T[:, 0:1]
            gate = _dot(x, wgu_refs[r][:, :EXP_FF])
            deal()
            up = _dot(x, wgu_refs[r][:, EXP_FF:])
            deal()
            a = (_silu(gate) * up * wcol).astype(bf16)
            for c in range(D // MXU_N):
                y = _dot(a, wd_refs[r][:, c * MXU_N:(c + 1) * MXU_N])
                for jj in range(MXU_N // LANES):
                    j = c * (MXU_N // LANES) + jj
                    ylin_refs[1 - p][r][pl.ds(j, TM, stride=NCH), :] = y[:, jj * LANES:(jj + 1) * LANES]
                deal()
        while row_work:
            deal()

    live = (t - 2) * R < nt_ref[b]
    pl.when(jnp.logical_and(live, t % 2 == 0))(functools.partial(stage, 0))
    pl.when(jnp.logical_and(live, t % 2 == 1))(functools.partial(stage, 1))

    @pl.when(t >= ngrp + 2)
    def _():
        row0 = (t - (ngrp + 2)) * (ROW_TILE * NCH)
        for j in range(NCH):
            out_ref[:, j * LANES:(j + 1) * LANES] = acc_ref[pl.ds(row0 + j, ROW_TILE, stride=NCH), :].astype(bf16)


def _moe(te, nt, rows, wsort, hlin, w_gu, w_d, layer):
    B = hlin.shape[0]
    NCH = D // LANES
    n_tok = hlin.shape[1] // NCH - ROW_TILE
    assert n_tok % ROW_TILE == 0
    nf = n_tok // ROW_TILE
    TM = MOE_TILE
    R = MOE_GROUP
    ntp, _ = _moe_sizes(n_tok)
    assert ntp % R == 0
    ngrp = ntp // R

    def group_of(b, t, nt_ref, lag):
        grp = t - lag
        ok = jnp.logical_and(t >= lag, grp * R < nt_ref[b])
        return b * ngrp + jnp.where(ok, grp, ngrp - 1)

    def rows_spec(lag):
        return pl.BlockSpec((R * TM,), lambda b, t, te_ref, nt_ref: (group_of(b, t, nt_ref, lag),),
                            memory_space=pltpu.SMEM)

    ws_spec = pl.BlockSpec((None, R, TM), lambda b, t, te_ref, nt_ref: (group_of(b, t, nt_ref, 1), 0, 0))

    def w_spec(shape, r):
        def index(b, t, te_ref, nt_ref):
            tile = jnp.clip((t - 1) * R + r, 0, nt_ref[b] - 1)
            return (layer, te_ref[b * META_W + tile], 0, 0)
        return pl.BlockSpec((None, None) + shape, index)

    grid_spec = pltpu.PrefetchScalarGridSpec(
        num_scalar_prefetch=2,
        grid=(B, ngrp + 2 + nf),
        in_specs=([rows_spec(0), rows_spec(2), ws_spec,
                   pl.BlockSpec((None, (n_tok + ROW_TILE) * NCH, LANES), lambda b, t, *_: (b, 0, 0),
                                pipeline_mode=pl.Buffered(1))]
                  + [w_spec((D, 2 * EXP_FF), r) for r in range(R)]
                  + [w_spec((EXP_FF, D), r) for r in range(R)]),
        out_specs=pl.BlockSpec((None, ROW_TILE, D), lambda b, t, *_: (b, jnp.maximum(t - (ngrp + 2), 0), 0)),
        scratch_shapes=([pltpu.VMEM(((n_tok + SUBLANES) * NCH, LANES), f32)]
                        + [pltpu.VMEM(((TM + SUBLANES) * NCH, LANES), f32) for _ in range(2 * R)]
                        + [pltpu.VMEM((TM * NCH, LANES), f32) for _ in range(2 * R)]),
    )
    return pl.pallas_call(
        functools.partial(_moe_kernel, ntp=ntp, n_tok=n_tok),
        grid_spec=grid_spec,
        out_shape=jax.ShapeDtypeStruct((B, n_tok, D), bf16),
        compiler_params=_cparams(("arbitrary", "arbitrary"), MOE_VMEM_LIMIT),
        name="moe",
    )(te, nt, rows, rows, wsort.reshape(B * ngrp, R, TM), hlin, *([w_gu] * R), *([w_d] * R))


def _routed_experts(e_t, w_t, r_t, cnt, hlin, w_gu, w_d, layer):
    N = e_t.shape[2]
    pos, meta = _plan(e_t, r_t, cnt)
    te = meta[:, 0, :].reshape(-1)
    nt = meta[:, 1, 0]
    rows, wsort = _plan_invert(pos, w_t, N)
    return _moe(te, nt, rows, wsort, hlin, w_gu, w_d, layer)


def _shared_ffn(fin, shgu_ref, shd_ref):
    gu = _dot(fin, shgu_ref[...])
    return _dot((_silu(gu[:, :SH_FF]) * gu[:, SH_FF:]).astype(bf16), shd_ref[...])


def _post_ffn_mla_kernel(x1_ref, routed_ref, fin_ref, shgu_ref, shd_ref, mod0_ref, mod1_ref, g_ref, win_ref,
                         x2_ref, a_ref):
    x2 = x1_ref[...] + mod0_ref[...][:, 5 * D:6 * D] * (routed_ref[...] + _shared_ffn(fin_ref[...], shgu_ref, shd_ref))
    x2_ref[...] = x2
    mod1 = mod1_ref[...]
    h = (_rms(x2) * g_ref[...]) * (1.0 + mod1[:, D:2 * D]) + mod1[:, 0:D]
    a_ref[...] = _dot(h.astype(bf16), win_ref[...])


def _post_ffn_mla(x1, routed, fin, sh_gu, sh_d, modtab0, modtab1, gain, w_in, n_ctx_tiles):
    B, T, _ = x1.shape
    tm = ROW_TILE
    tok = lambda w: pl.BlockSpec((None, tm, w), lambda b, i: (b, i, 0))
    modspec = lambda: pl.BlockSpec((None, None, 1, 6 * D), lambda b, i: (b, jnp.where(i < n_ctx_tiles, 0, 1), 0, 0))
    full = lambda r, c: pl.BlockSpec((r, c), lambda b, i: (0, 0))
    return pl.pallas_call(
        _post_ffn_mla_kernel,
        grid=(B, T // tm),
        in_specs=[tok(D), tok(D), tok(D), full(D, 2 * SH_FF), full(SH_FF, D), modspec(), modspec(),
                  full(1, D), full(D, MLA_IN_PAD)],
        out_specs=[tok(D), tok(MLA_IN_PAD)],
        out_shape=[jax.ShapeDtypeStruct((B, T, D), f32), jax.ShapeDtypeStruct((B, T, MLA_IN_PAD), f32)],
        compiler_params=_cparams(("arbitrary", "arbitrary")),
        name="post_ffn_mla",
    )(x1, routed, fin, sh_gu, sh_d, modtab0, modtab1, gain, w_in)


def _post_ffn_final_kernel(x1_ref, routed_ref, fin_ref, shgu_ref, shd_ref, mod_ref, out_ref):
    out_ref[...] = x1_ref[...] + mod_ref[...][:, 5 * D:6 * D] * (
        routed_ref[...] + _shared_ffn(fin_ref[...], shgu_ref, shd_ref))


def _post_ffn_final(x1, routed, fin, sh_gu, sh_d, modtab):
    B, N, _ = x1.shape
    tm = ROW_TILE
    tok = lambda w: pl.BlockSpec((None, tm, w), lambda b, i: (b, i, 0))
    full = lambda r, c: pl.BlockSpec((r, c), lambda b, i: (0, 0))
    return pl.pallas_call(
        _post_ffn_final_kernel,
        grid=(B, N // tm),
        in_specs=[tok(D), tok(D), tok(D), full(D, 2 * SH_FF), full(SH_FF, D),
                  pl.BlockSpec((None, None, 1, 6 * D), lambda b, i: (b, 1, 0, 0))],
        out_specs=tok(D),
        out_shape=jax.ShapeDtypeStruct((B, N, D), f32),
        compiler_params=_cparams(("arbitrary", "arbitrary")),
        name="post_ffn_final",
    )(x1, routed, fin, sh_gu, sh_d, modtab)


def _mla_prep_kernel(a_ref, qan_ref, wqn_ref, wqr_ref, kvan_ref, wk_ref, wv_ref, qnn_ref, qnr_ref, knn_ref, knr_ref,
                     cos_ref, sin_ref, q_ref, k_ref, v_ref):
    a = a_ref[...]
    tm = a.shape[0]
    scale = MLA_QK ** -0.5 * float(np.log2(np.e))
    cos = cos_ref[...]
    sin = sin_ref[...]
    lane = lax.broadcasted_iota(i32, (tm, LANES), 1)
    first = (lane // (MLA_ROPE // 4)) % 2 == 0

    def rope(xb):
        sw = jnp.where(first, pltpu.roll(xb, LANES - MLA_ROPE // 4, axis=1), pltpu.roll(xb, MLA_ROPE // 4, axis=1))
        return xb * cos + sw * sin

    qa = (_rms(a[:, :MLA_QR]) * qan_ref[...]).astype(bf16)
    qn = _dot(qa, wqn_ref[...])
    qr = _dot(qa, wqr_ref[...])
    ri = lax.broadcasted_iota(i32, (LANES, LANES), 0) // MLA_ROPE
    ci = lax.broadcasted_iota(i32, (LANES, LANES), 1) // MLA_ROPE
    seg = jnp.where(ri == ci, 1.0, 0.0).astype(bf16)

    def seg_sum(sq):
        hi = sq.astype(bf16)
        r1 = sq - hi.astype(f32)
        mid = r1.astype(bf16)
        lo = (r1 - mid.astype(f32)).astype(bf16)
        return _dot(hi, seg) + _dot(mid, seg) + _dot(lo, seg)

    qr_blocks = []
    for p in range(MLA_H // 2):
        blk = qr[:, p * LANES:(p + 1) * LANES]
        blk = blk * lax.rsqrt(seg_sum(blk * blk) * (1.0 / MLA_ROPE) + EPS) * qnr_ref[:, p * LANES:(p + 1) * LANES]
        qr_blocks.append(rope(blk) * scale)

    kv = (_rms(a[:, MLA_QR:MLA_QR + MLA_KVR]) * kvan_ref[...]).astype(bf16)
    kn = _dot(kv, wk_ref[...])
    v_ref[...] = _dot_nt(wv_ref[...], kv).astype(bf16)
    kr = a[:, MLA_QR + MLA_KVR:MLA_IN_PAD]
    kr = rope(_rms(kr, MLA_ROPE) * knr_ref[...])
    kr_odd = pltpu.roll(kr, MLA_ROPE, axis=1)
    for hd in range(MLA_H):
        sl = slice(hd * MLA_NOPE, (hd + 1) * MLA_NOPE)
        q_ref[:, 2 * hd * LANES:(2 * hd + 1) * LANES] = (_rms(qn[:, sl]) * qnn_ref[...] * scale).astype(bf16)
        q_ref[:, (2 * hd + 1) * LANES:(2 * hd + 2) * LANES] = qr_blocks[hd // 2].astype(bf16)
        k_ref[:, 2 * hd * LANES:(2 * hd + 1) * LANES] = (_rms(kn[:, sl]) * knn_ref[...]).astype(bf16)
        k_ref[:, (2 * hd + 1) * LANES:(2 * hd + 2) * LANES] = (kr if hd % 2 == 0 else kr_odd).astype(bf16)


def _mla_prep(a, qan, wqn, wqr, kvan, wk, wv, qnn, qnr, knn, knr, cos_t, sin_t, n_ctx_tiles):
    B, T, _ = a.shape
    tm = ROW_TILE
    tok = lambda w: pl.BlockSpec((None, tm, w), lambda b, i: (b, i, 0))
    full = lambda r, c: pl.BlockSpec((r, c), lambda b, i: (0, 0))
    hw = 2 * LANES * MLA_H
    q_spec = pl.BlockSpec((None, tm, hw), lambda b, i: (b, jnp.maximum(i - n_ctx_tiles, 0), 0))
    return pl.pallas_call(
        _mla_prep_kernel,
        grid=(B, T // tm),
        in_specs=[tok(MLA_IN_PAD), full(1, MLA_QR), full(MLA_QR, MLA_H * MLA_NOPE), full(MLA_QR, MLA_H * MLA_ROPE),
                  full(1, MLA_KVR), full(MLA_KVR, MLA_H * MLA_NOPE), full(MLA_H * MLA_V, MLA_KVR),
                  full(1, MLA_NOPE), full(1, MLA_H * MLA_ROPE), full(1, MLA_NOPE), full(1, LANES),
                  pl.BlockSpec((tm, LANES), lambda b, i: (i, 0)), pl.BlockSpec((tm, LANES), lambda b, i: (i, 0))],
        out_specs=[q_spec, tok(hw), pl.BlockSpec((None, MLA_H * MLA_V, tm), lambda b, i: (b, 0, i))],
        out_shape=[jax.ShapeDtypeStruct((B, T - n_ctx_tiles * tm, hw), bf16), jax.ShapeDtypeStruct((B, T, hw), bf16),
                   jax.ShapeDtypeStruct((B, MLA_H * MLA_V, T), bf16)],
        compiler_params=_cparams(("arbitrary", "arbitrary")),
        name="mla_prep",
    )(a, qan, wqn, wqr, kvan, wk, wv, qnn, qnr, knn, knr, cos_t, sin_t)


ATT_KV_CHUNK = 256
ATT_TQ = 256
ATT_SUB = 1


def _mla_attn_kernel(q_ref, k_ref, vt_ref, o_ref, *scratch, n_sub):
    tq = q_ref.shape[0] // n_sub
    ck = ATT_KV_CHUNK
    nck = k_ref.shape[0] // ck
    i = pl.program_id(0)

    s_refs = [scratch[0:n_sub], scratch[n_sub:2 * n_sub]]
    m_refs = [scratch[2 * n_sub:3 * n_sub], scratch[3 * n_sub:4 * n_sub]]

    @pl.when(i == 0)
    def _():
        for ref in scratch:
            ref[...] = jnp.zeros_like(ref)

    def fold(a, op):
        return op(a.reshape(ck // SUBLANES, SUBLANES, tq), axis=0)

    def stage(par):
        for u in range(n_sub):
            q = q_ref[u * tq:(u + 1) * tq, :]
            s = _dot_nt(k_ref[...], q)
            s_refs[par][u][...] = s
            m_refs[par][u][...] = jnp.max(s.reshape(nck * ck // SUBLANES, SUBLANES, tq), axis=0)
        for u in range(n_sub):
            m = jnp.max(m_refs[1 - par][u][...], axis=0, keepdims=True)
            lpart = jnp.zeros((SUBLANES, tq), f32)
            acc = jnp.zeros((MLA_V, tq), f32)
            for j in range(nck):
                p = jnp.exp2(s_refs[1 - par][u][j * ck:(j + 1) * ck, :] - m)
                lpart = lpart + fold(p, jnp.sum)
                acc = acc + _dot(vt_ref[:, j * ck:(j + 1) * ck], p.astype(bf16))
            o_ref[:, u * tq:(u + 1) * tq] = (acc / jnp.sum(lpart, axis=0, keepdims=True)).astype(bf16)

    pl.when(i % 2 == 0)(functools.partial(stage, 0))
    pl.when(i % 2 == 1)(functools.partial(stage, 1))


def _mla_attn(q, k, vt):
    B, S, _ = q.shape
    T = k.shape[1]
    n_sub = ATT_SUB if S % (ATT_SUB * ATT_TQ) == 0 else 1
    tq = n_sub * ATT_TQ
    nq = S // tq
    ntile = B * MLA_H * nq
    assert T % ATT_KV_CHUNK == 0

    def tile(g):
        g = jnp.clip(g, 0, ntile - 1)
        return g // (MLA_H * nq), (g // nq) % MLA_H, g % nq

    def q_index(g):
        b, h, i = tile(g)
        return b, i, h

    def k_index(g):
        b, h, _ = tile(g)
        return b, 0, h

    def vt_index(g):
        b, h, _ = tile(g - 1)
        return b, h, 0

    def o_index(g):
        b, h, i = tile(g - 1)
        return b, h, i

    return pl.pallas_call(
        functools.partial(_mla_attn_kernel, n_sub=n_sub),
        grid=(ntile + 1,),
        in_specs=[pl.BlockSpec((None, tq, 2 * LANES), q_index),
                  pl.BlockSpec((None, T, 2 * LANES), k_index),
                  pl.BlockSpec((None, MLA_V, T), vt_index)],
        out_specs=pl.BlockSpec((None, MLA_V, tq), o_index),
        out_shape=jax.ShapeDtypeStruct((B, MLA_H * MLA_V, S), bf16),
        scratch_shapes=([pltpu.VMEM((T, ATT_TQ), f32) for _ in range(2 * n_sub)]
                        + [pltpu.VMEM((SUBLANES, ATT_TQ), f32) for _ in range(2 * n_sub)]),
        compiler_params=_cparams(("arbitrary",)),
        name="mla_attn",
    )(q, k, vt)


def _axial_angles(rows_n, rot_dim):
    axis_dim = rot_dim // 2
    inv = ROPE_BASE ** (-jnp.arange(0, axis_dim, 2, dtype=f32) / axis_dim)
    row = jnp.repeat(jnp.arange(rows_n, dtype=f32), GRID_W)
    col = jnp.tile(jnp.arange(GRID_W, dtype=f32), rows_n)
    return row[:, None] * inv, col[:, None] * inv


def _rope_tables(seq, n_ctx, rot_dim, reps):
    ang_r, ang_c = _axial_angles(seq // GRID_W, rot_dim)
    cos = jnp.concatenate([jnp.cos(ang_r)] * 2 + [jnp.cos(ang_c)] * 2, axis=1)
    sin = jnp.concatenate([-jnp.sin(ang_r), jnp.sin(ang_r), -jnp.sin(ang_c), jnp.sin(ang_c)], axis=1)
    cos = jnp.concatenate([jnp.ones((n_ctx, rot_dim), f32), cos], axis=0)
    sin = jnp.concatenate([jnp.zeros((n_ctx, rot_dim), f32), sin], axis=0)
    return jnp.tile(cos, (1, reps)), jnp.tile(sin, (1, reps))


def kernel(x, c, ctx, c_ctx, ada_w, ada_b, norm_mix, norm_ffn, ret_w_in, ret_decay_f, ret_decay_b, ret_w_o,
           mla_w_in, mla_q_a_norm, mla_w_q_b, mla_kv_a_norm, mla_w_kv_b, mla_q_norm, mla_k_norm, mla_w_o,
           router_w, router_bias, exp_w_gu, exp_w_down, sh_w_gu, sh_w_down):
    B, S, _ = x.shape
    n_ctx = ctx.shape[1]
    assert n_ctx % ROW_TILE == 0 and S % ROW_TILE == 0 and S % GRID_W == 0
    n_ctx_tiles = n_ctx // ROW_TILE

    rows = -(-(B + 1) // SUBLANES) * SUBLANES
    cc = jnp.zeros((rows, D), f32).at[:B].set(c).at[B].set(c_ctx)
    mod = _ada(cc, ada_w, ada_b)

    def modtab(i):
        ctx_row = jnp.broadcast_to(mod[i, B][None, :], (B, 6 * D))
        return jnp.stack([ctx_row, mod[i, :B]], axis=1)[:, :, None, :]

    mod0, mod1 = modtab(0), modtab(1)

    cos_r, sin_r = _rope_tables(S, n_ctx, RET_DK, 1)
    q, k, v, gf, gb = _ret_inproj(ctx, x, mod0, norm_mix[0][None, :], ret_w_in[0].astype(bf16), cos_r, sin_r,
                                  n_ctx_tiles)
    dtab = jnp.broadcast_to(jnp.concatenate([ret_decay_f[0], ret_decay_b[0]])[:, None], (2 * RET_H, LANES))
    o = _ret_scan(dtab, q, k, v, gf, gb, n_ctx)
    x1, fin, hlin, e_t, w_t, r_t, cnt = _post_mix(
        o, ret_w_o[0].astype(bf16), (ctx, x), 0, mod0, n_ctx_tiles, norm_ffn[0][None, :],
        router_w[0].T, router_bias[0][:, None])
    exp_gu, exp_d = exp_w_gu.astype(bf16), exp_w_down.astype(bf16)
    routed = _routed_experts(e_t, w_t, r_t, cnt, hlin, exp_gu, exp_d, 0)
    w_in1 = jnp.zeros((D, MLA_IN_PAD), f32).at[:, :mla_w_in.shape[2]].set(mla_w_in[0]).astype(bf16)
    x2, a = _post_ffn_mla(x1, routed, fin, sh_w_gu[0].astype(bf16), sh_w_down[0].astype(bf16), mod0, mod1,
                          norm_mix[1][None, :], w_in1, n_ctx_tiles)

    wq = mla_w_q_b[0].reshape(MLA_QR, MLA_H, MLA_QK)
    wqn = wq[:, :, :MLA_NOPE].reshape(MLA_QR, MLA_H * MLA_NOPE).astype(bf16)
    wqr = wq[:, :, MLA_NOPE:].reshape(MLA_QR, MLA_H * MLA_ROPE).astype(bf16)
    wkv = mla_w_kv_b[0].reshape(MLA_KVR, MLA_H, MLA_NOPE + MLA_V)
    wk = wkv[:, :, :MLA_NOPE].reshape(MLA_KVR, MLA_H * MLA_NOPE).astype(bf16)
    wv = wkv[:, :, MLA_NOPE:].reshape(MLA_KVR, MLA_H * MLA_V).T.astype(bf16)
    qnn = mla_q_norm[0][None, :MLA_NOPE]
    qnr = jnp.tile(mla_q_norm[0][None, MLA_NOPE:], (1, MLA_H))
    knn = mla_k_norm[0][None, :MLA_NOPE]
    knr = jnp.concatenate([mla_k_norm[0][MLA_NOPE:], jnp.zeros((LANES - MLA_ROPE,), f32)])[None, :]
    cos_m, sin_m = _rope_tables(S, n_ctx, MLA_ROPE, LANES // MLA_ROPE)
    qf, kf, vf = _mla_prep(a, mla_q_a_norm[0][None, :], wqn, wqr, mla_kv_a_norm[0][None, :], wk, wv,
                           qnn, qnr, knn, knr, cos_m, sin_m, n_ctx_tiles)
    o1 = _mla_attn(qf, kf, vf)
    x3, fin1, hlin1, e1, w1, r1, cnt1 = _post_mix(
        o1, mla_w_o[0].astype(bf16), (x2,), n_ctx_tiles, mod1, 0, norm_ffn[1][None, :],
        router_w[1].T, router_bias[1][:, None], o_transposed=True)
    routed1 = _routed_experts(e1, w1, r1, cnt1, hlin1, exp_gu, exp_d, 1)
    return _post_ffn_final(x3, routed1, fin1, sh_w_gu[1].astype(bf16), sh_w_down[1].astype(bf16), mod1)
```

```python
import dataclasses
import functools

import jax
import jax.numpy as jnp
import numpy as np
from jax import lax
from jax.experimental import pallas as pl
from jax.experimental.pallas import tpu as pltpu
from jax.experimental.pallas import tpu_sc as plsc

f32 = jnp.float32
bf16 = jnp.bfloat16
i32 = jnp.int32

D = 1024
GRID_W = 64
EPS = 1e-6
ROPE_BASE = 10000.0
RET_H = 4
RET_DK = 256
RET_DV = 512
RET_VW = RET_H * RET_DV
RET_CHUNK = 256
MLA_H = 8
MLA_NOPE = 128
MLA_ROPE = 64
MLA_QK = MLA_NOPE + MLA_ROPE
MLA_V = 128
MLA_QR = 384
MLA_KVR = 256
MLA_IN_PAD = 768
N_EXP = 64
TOP_K = 8
N_GRP = 8
TOPK_GRP = 4
EXP_FF = 256
SH_FF = 256
ROUTED_SCALE = 2.5

LANES = 128
SUBLANES = 8
SC_LANES = 16
MXU_N = 256
ROW_TILE = 256
MOE_TILE = 256
MOE_GROUP = 2
VMEM_LIMIT = 56 * 1024 * 1024
MOE_VMEM_LIMIT = 62 * 1024 * 1024


def _cparams(sem, vmem=VMEM_LIMIT):
    return pltpu.CompilerParams(dimension_semantics=sem, vmem_limit_bytes=vmem)


def _sigmoid(x):
    return 1.0 / (1.0 + jnp.exp(-x))


def _silu(x):
    return x * _sigmoid(x)


def _rms(x, n=None):
    n = x.shape[-1] if n is None else n
    return x * lax.rsqrt(jnp.sum(x * x, axis=-1, keepdims=True) * (1.0 / n) + EPS)


def _dot(a, b):
    return jnp.dot(a, b, preferred_element_type=f32)


def _dot_nt(a, b, precision=None):
    return lax.dot_general(a, b, (((1,), (1,)), ((), ())), preferred_element_type=f32, precision=precision)


def _ada_kernel(c_ref, w_ref, b_ref, o_ref):
    s = _silu(c_ref[...]).astype(bf16)
    o_ref[...] = _dot(s, w_ref[...].astype(bf16)) + b_ref[...]


def _ada(cc, ada_w, ada_b):
    depth = ada_w.shape[0]
    rows = cc.shape[0]
    tn = 1536
    return pl.pallas_call(
        _ada_kernel,
        grid=(depth, 6 * D // tn),
        in_specs=[pl.BlockSpec((rows, D), lambda i, j: (0, 0)),
                  pl.BlockSpec((None, D, tn), lambda i, j: (i, 0, j)),
                  pl.BlockSpec((None, 1, tn), lambda i, j: (i, 0, j))],
        out_specs=pl.BlockSpec((None, rows, tn), lambda i, j: (i, 0, j)),
        out_shape=jax.ShapeDtypeStruct((depth, rows, 6 * D), f32),
        compiler_params=_cparams(("arbitrary", "arbitrary")),
        name="ada",
    )(cc, ada_w, ada_b.reshape(depth, 1, 6 * D))


def _stream_tile(ctx_ref, x_ref, n_ctx_tiles):
    return jnp.where(pl.program_id(1) < n_ctx_tiles, ctx_ref[...], x_ref[...])


def _stream_specs(tm, n_ctx_tiles, last=None):
    clamp = (lambda i: i) if last is None else (lambda i: jnp.minimum(i, last))
    return [pl.BlockSpec((None, tm, D), lambda b, i: (b, jnp.minimum(clamp(i), n_ctx_tiles - 1), 0)),
            pl.BlockSpec((None, tm, D), lambda b, i: (b, jnp.maximum(clamp(i) - n_ctx_tiles, 0), 0))]


def _ret_inproj_kernel(ctx_ref, x_ref, mod_ref, g_ref, w_ref, cos_ref, sin_ref, q_ref, k_ref, v_ref, gf_ref, gb_ref,
                       *, n_ctx_tiles):
    x = _stream_tile(ctx_ref, x_ref, n_ctx_tiles)
    mod = mod_ref[...]
    h = (_rms(x) * g_ref[...]) * (1.0 + mod[:, D:2 * D]) + mod[:, 0:D]
    hb = h.astype(bf16)
    cos = cos_ref[...]
    sin = sin_ref[...]

    def rope(a):
        outs = []
        for half in range(2):
            sl = slice(half * LANES, (half + 1) * LANES)
            ah = a[:, sl]
            outs.append(ah * cos[:, sl] + pltpu.roll(ah, LANES // 2, axis=1) * sin[:, sl])
        return jnp.concatenate(outs, axis=1)

    for hd in range(RET_H):
        sl = slice(hd * RET_DK, (hd + 1) * RET_DK)
        q_ref[:, sl] = rope(_dot(hb, w_ref[:, sl])).astype(bf16)
    for hd in range(RET_H):
        sl = slice(hd * RET_DK, (hd + 1) * RET_DK)
        wsl = slice(D + hd * RET_DK, D + (hd + 1) * RET_DK)
        k_ref[:, sl] = (rope(_dot(hb, w_ref[:, wsl])) * (RET_DK ** -0.5)).astype(bf16)
    cw = 512
    for c in range(RET_VW // cw):
        sl = slice(c * cw, (c + 1) * cw)
        v_ref[:, sl] = _dot(hb, w_ref[:, 2 * D + c * cw:2 * D + (c + 1) * cw]).astype(bf16)
        gf_ref[:, sl] = _silu(_dot(hb, w_ref[:, 2 * D + RET_VW + c * cw:2 * D + RET_VW + (c + 1) * cw])).astype(bf16)
        gb_ref[:, sl] = _silu(_dot(hb, w_ref[:, 2 * D + 2 * RET_VW + c * cw:2 * D + 2 * RET_VW + (c + 1) * cw])).astype(bf16)


def _ret_inproj(ctx, x, modtab, gain, w_in, cos_t, sin_t, n_ctx_tiles):
    B = x.shape[0]
    T = ctx.shape[1] + x.shape[1]
    tm = ROW_TILE
    n_in = w_in.shape[1]
    tok = lambda w: pl.BlockSpec((None, tm, w), lambda b, i: (b, i, 0))
    return pl.pallas_call(
        functools.partial(_ret_inproj_kernel, n_ctx_tiles=n_ctx_tiles),
        grid=(B, T // tm),
        in_specs=_stream_specs(tm, n_ctx_tiles) + [
                  pl.BlockSpec((None, None, 1, 6 * D), lambda b, i: (b, jnp.where(i < n_ctx_tiles, 0, 1), 0, 0)),
                  pl.BlockSpec((1, D), lambda b, i: (0, 0)),
                  pl.BlockSpec((D, n_in), lambda b, i: (0, 0), pipeline_mode=pl.Buffered(1)),
                  pl.BlockSpec((tm, RET_DK), lambda b, i: (i, 0)),
                  pl.BlockSpec((tm, RET_DK), lambda b, i: (i, 0))],
        out_specs=[tok(D), tok(D), tok(RET_VW), tok(RET_VW), tok(RET_VW)],
        out_shape=[jax.ShapeDtypeStruct((B, T, D), bf16), jax.ShapeDtypeStruct((B, T, D), bf16),
                   jax.ShapeDtypeStruct((B, T, RET_VW), bf16), jax.ShapeDtypeStruct((B, T, RET_VW), bf16),
                   jax.ShapeDtypeStruct((B, T, RET_VW), bf16)],
        compiler_params=_cparams(("arbitrary", "arbitrary")),
        name="ret_inproj",
    )(ctx, x, modtab, gain, w_in, cos_t, sin_t)


def _ret_chunk_index(t, nc, ncc):
    u = t - nc
    back = jnp.where(u < ncc, ncc - 1 - u, nc - 1 - u + ncc)
    return jnp.where(t < nc, t, back)


def _ret_scan_kernel(dt_ref, q_ref, k_ref, v_ref, gf_ref, gb_ref, o_ref,
                     s_ref, of_ref, mask_ref, dq_ref, dk_ref, dc_ref, *, nc, ncc):
    t = pl.program_id(1)
    C = RET_CHUNK

    def init(direction):
        s_ref[...] = jnp.zeros_like(s_ref)
        ii = lax.broadcasted_iota(i32, (C, C), 0)
        jj = lax.broadcasted_iota(i32, (C, C), 1)
        rel = (ii - jj if direction == 0 else jj - ii).astype(f32)
        pos = lax.broadcasted_iota(i32, (C, 1), 0).astype(f32)
        for hd in range(RET_H):
            r = direction * RET_H + hd
            lg = -jnp.exp(dt_ref[r:r + 1, :])
            lg1 = lg[:, 0:1]
            mask_ref[hd] = jnp.where(rel >= 0, jnp.exp(lg1 * jnp.maximum(rel, 0.0)), 0.0)
            if direction == 0:
                dq_ref[hd] = jnp.exp(lg1 * (pos + 1.0))
                dk_ref[hd] = jnp.exp(lg1 * (C - 1.0 - pos))
            else:
                dq_ref[hd] = jnp.exp(lg1 * (C - pos))
                dk_ref[hd] = jnp.exp(lg1 * pos)
            dc_ref[hd] = jnp.exp(lg * float(C))

    pl.when(t == 0)(functools.partial(init, 0))
    pl.when(t == nc)(functools.partial(init, 1))

    row0 = pl.multiple_of(_ret_chunk_index(t, nc, ncc) * C, C)

    def step(forward):
        for hd in range(RET_H):
            ks = slice(hd * RET_DK, (hd + 1) * RET_DK)
            vs = slice(hd * RET_DV, (hd + 1) * RET_DV)
            qh = q_ref[:, ks]
            kh = k_ref[:, ks]
            vh = v_ref[:, vs]
            p = (_dot_nt(qh, kh) * mask_ref[hd]).astype(bf16)
            y = _dot(p, vh) + _dot(qh, s_ref[hd].astype(bf16)) * dq_ref[hd]
            kd = (kh.astype(f32) * dk_ref[hd]).astype(bf16)
            upd = lax.dot_general(kd, vh, (((0,), (0,)), ((), ())), preferred_element_type=f32)
            s_ref[hd] = s_ref[hd] * dc_ref[hd][0:1, 0:1] + upd
            yn = _rms(y)
            if forward:
                of_ref[pl.ds(row0, C), vs] = (gf_ref[:, vs].astype(f32) * yn).astype(bf16)
            else:
                o_ref[:, vs] = (of_ref[pl.ds(row0, C), vs].astype(f32) + gb_ref[:, vs].astype(f32) * yn).astype(bf16)

    pl.when(t < nc)(functools.partial(step, True))
    pl.when(t >= nc)(functools.partial(step, False))


def _ret_scan(dtab, q, k, v, gf, gb, n_ctx):
    B, T, _ = q.shape
    C = RET_CHUNK
    nc = T // C
    ncc = n_ctx // C
    cidx = functools.partial(_ret_chunk_index, nc=nc, ncc=ncc)
    first_back = ncc - 1
    return pl.pallas_call(
        functools.partial(_ret_scan_kernel, nc=nc, ncc=ncc),
        grid=(B, 2 * nc),
        in_specs=[pl.BlockSpec((2 * RET_H, LANES), lambda b, t: (0, 0)),
                  pl.BlockSpec((None, C, D), lambda b, t: (b, cidx(t), 0)),
                  pl.BlockSpec((None, C, D), lambda b, t: (b, cidx(t), 0)),
                  pl.BlockSpec((None, C, RET_VW), lambda b, t: (b, cidx(t), 0)),
                  pl.BlockSpec((None, C, RET_VW), lambda b, t: (b, jnp.where(t < nc, t, nc - 1), 0)),
                  pl.BlockSpec((None, C, RET_VW), lambda b, t: (b, jnp.where(t < nc, first_back, cidx(t)), 0))],
        out_specs=pl.BlockSpec((None, C, RET_VW), lambda b, t: (b, jnp.where(t < nc, first_back, cidx(t)), 0)),
        out_shape=jax.ShapeDtypeStruct((B, T, RET_VW), bf16),
        scratch_shapes=[pltpu.VMEM((RET_H, RET_DK, RET_DV), f32),
                        pltpu.VMEM((T, RET_VW), bf16),
                        pltpu.VMEM((RET_H, C, C), f32),
                        pltpu.VMEM((RET_H, C, 1), f32),
                        pltpu.VMEM((RET_H, C, 1), f32),
                        pltpu.VMEM((RET_H, 1, LANES), f32)],
        compiler_params=_cparams(("arbitrary", "arbitrary")),
        name="ret_scan",
    )(dtab, q, k, v, gf, gb)


def _route(f, rwt_ref, rb_ref, cnt_ref, e_ref, w_ref, r_ref):
    tm = f.shape[0]
    G = N_EXP // N_GRP
    logits = _dot_nt(rwt_ref[...].astype(bf16), f.astype(bf16))
    s = _sigmoid(logits)
    sel = s + rb_ref[...]
    mi = lax.broadcasted_iota(i32, (G, tm), 0)
    neg = -jnp.inf
    s_g = [s[g * G:(g + 1) * G, :] for g in range(N_GRP)]
    sel_g = [sel[g * G:(g + 1) * G, :] for g in range(N_GRP)]

    def first_max(a, ids, big):
        mx = jnp.max(a, axis=0, keepdims=True)
        ix = jnp.min(jnp.where(a == mx, ids, big), axis=0, keepdims=True)
        return mx, ix

    gscore = jnp.zeros((N_GRP, tm), f32)
    gi = lax.broadcasted_iota(i32, (N_GRP, tm), 0)
    for g in range(N_GRP):
        t1, i1 = first_max(sel_g[g], mi, G)
        t2 = jnp.max(jnp.where(mi == i1, neg, sel_g[g]), axis=0, keepdims=True)
        gscore = jnp.where(gi == g, t1 + t2, gscore)
    yield
    gmask = jnp.zeros((N_GRP, tm), i32)
    cur = gscore
    for _ in range(TOPK_GRP):
        _, ix = first_max(cur, gi, N_GRP)
        hit = gi == ix
        gmask = jnp.where(hit, 1, gmask)
        cur = jnp.where(hit, neg, cur)
    cand = [jnp.where(gmask[g:g + 1, :] > 0, sel_g[g], neg) for g in range(N_GRP)]
    ids = [mi + g * G for g in range(N_GRP)]

    def across(parts, op):
        acc = parts[0]
        for part in parts[1:]:
            acc = op(acc, part)
        return acc

    e_rows, w_rows = [], []
    for _ in range(TOP_K):
        mx = jnp.max(across(cand, jnp.maximum), axis=0, keepdims=True)
        ix = jnp.min(across([jnp.where(cand[g] == mx, ids[g], N_EXP) for g in range(N_GRP)], jnp.minimum),
                     axis=0, keepdims=True)
        hits = [ids[g] == ix for g in range(N_GRP)]
        cand = [jnp.where(hits[g], neg, cand[g]) for g in range(N_GRP)]
        wv = jnp.sum(across([jnp.where(hits[g], s_g[g], 0.0) for g in range(N_GRP)], jnp.add), axis=0, keepdims=True)
        e_rows.append(ix)
        w_rows.append(wv)
        yield
    wsum = w_rows[0]
    for r in range(1, TOP_K):
        wsum = wsum + w_rows[r]

    selm = [jnp.zeros((G, tm), f32) for _ in range(N_GRP)]
    for r in range(TOP_K):
        for g in range(N_GRP):
            selm[g] = jnp.where(ids[g] == e_rows[r], 1.0, selm[g])
    m_all = jnp.concatenate(selm, axis=0)
    ri = lax.broadcasted_iota(i32, (tm, tm), 0)
    ci = lax.broadcasted_iota(i32, (tm, tm), 1)
    upper = jnp.where(ri <= ci, 1.0, 0.0).astype(bf16)
    incl = _dot(m_all.astype(bf16), upper)
    carry = cnt_ref[:, 0:1]
    rank_all = carry + incl - m_all
    cnt_ref[...] = cnt_ref[...] + incl[:, tm - 1:tm]
    yield
    for r in range(TOP_K):
        rk = jnp.sum(across([jnp.where(ids[g] == e_rows[r], rank_all[g * G:(g + 1) * G, :], 0.0)
                             for g in range(N_GRP)], jnp.add), axis=0, keepdims=True)
        e_ref[r:r + 1, :] = e_rows[r]
        w_ref[r:r + 1, :] = w_rows[r] / wsum * ROUTED_SCALE
        r_ref[r:r + 1, :] = rk.astype(i32)


def _post_mix_kernel(o_ref, wo_ref, *refs, nt, o_transposed, split_ctx_tiles):
    n_resid = 2 if split_ctx_tiles else 1
    resid = refs[:n_resid]
    (mod_ref, g_ref, rwt_ref, rb_ref, x1_ref, fin_ref, hlin_ref, e_ref, w_ref, r_ref, cnt_ref,
     f0_ref, f1_ref) = refs[n_resid:]
    f_refs = (f0_ref, f1_ref)
    i = pl.program_id(1)

    @pl.when(i == 0)
    def _():
        cnt_ref[...] = jnp.zeros_like(cnt_ref)

    def route(p):
        return _route(f_refs[p][...], rwt_ref, rb_ref, cnt_ref, e_ref, w_ref, r_ref)

    def mix(p, other=()):
        other = iter(other)

        def advance(n):
            for _ in range(n):
                next(other, None)

        mod = mod_ref[...]
        x = _stream_tile(resid[0], resid[1], split_ctx_tiles) if split_ctx_tiles else resid[0][...]
        o = o_ref[...]
        cols = []
        for c in range(D // MXU_N):
            sl = slice(c * MXU_N, (c + 1) * MXU_N)
            if o_transposed:
                piece = lax.dot_general(o, wo_ref[:, sl], (((0,), (0,)), ((), ())), preferred_element_type=f32)
            else:
                piece = _dot(o, wo_ref[:, sl])
            cols.append(x[:, sl] + mod[:, 2 * D + c * MXU_N:2 * D + (c + 1) * MXU_N] * piece)
            advance(2)
        x1 = jnp.concatenate(cols, axis=1)
        x1_ref[...] = x1
        f = (_rms(x1) * g_ref[...]) * (1.0 + mod[:, 4 * D:5 * D]) + mod[:, 3 * D:4 * D]
        advance(2)
        fin_ref[...] = f.astype(bf16)
        tm = f.shape[0]
        for j in range(D // LANES):
            hlin_ref[pl.ds(j, tm, stride=D // LANES), :] = f[:, j * LANES:(j + 1) * LANES]
        f_refs[p][...] = f
        for _ in other:
            pass

    pl.when(i == 0)(functools.partial(mix, 0))
    for p in range(2):
        @pl.when(jnp.logical_and(jnp.logical_and(i >= 1, i < nt), i % 2 == p))
        def _(p=p):
            mix(p, route(1 - p))

    @pl.when(i == nt)
    def _():
        for _ in route((nt - 1) % 2):
            pass
        hlin_ref[...] = jnp.zeros_like(hlin_ref)


def _post_mix(o, w_o, resid, x_tile_off, modtab, n_ctx_tiles, gain, rwt, rb, o_transposed=False):
    if o_transposed:
        B, KO, N = o.shape
    else:
        B, N, KO = o.shape
    tm = ROW_TILE
    nt = N // tm
    last = lambda i: jnp.minimum(i, nt - 1)
    tok = lambda w: pl.BlockSpec((None, tm, w), lambda b, i: (b, last(i), 0))
    sel = lambda: pl.BlockSpec((None, TOP_K, tm), lambda b, i: (b, 0, jnp.maximum(i - 1, 0)))
    o_spec = (pl.BlockSpec((None, KO, tm), lambda b, i: (b, 0, last(i))) if o_transposed else tok(KO))
    split = len(resid) == 2
    resid_specs = (_stream_specs(tm, n_ctx_tiles, nt - 1) if split else
                   [pl.BlockSpec((None, tm, D), lambda b, i: (b, last(i) + x_tile_off, 0))])
    return pl.pallas_call(
        functools.partial(_post_mix_kernel, nt=nt, o_transposed=o_transposed,
                          split_ctx_tiles=n_ctx_tiles if split else 0),
        grid=(B, nt + 1),
        in_specs=[o_spec,
                  pl.BlockSpec((KO, D), lambda b, i: (0, 0))] + resid_specs + [
                  pl.BlockSpec((None, None, 1, 6 * D), lambda b, i: (b, jnp.where(last(i) < n_ctx_tiles, 0, 1), 0, 0)),
                  pl.BlockSpec((1, D), lambda b, i: (0, 0)),
                  pl.BlockSpec((N_EXP, D), lambda b, i: (0, 0)),
                  pl.BlockSpec((N_EXP, 1), lambda b, i: (0, 0))],
        out_specs=[tok(D), tok(D),
                   pl.BlockSpec((None, tm * (D // LANES), LANES), lambda b, i: (b, i, 0)),
                   sel(), sel(), sel(),
                   pl.BlockSpec((None, N_EXP, LANES), lambda b, i: (b, 0, 0))],
        out_shape=[jax.ShapeDtypeStruct((B, N, D), f32), jax.ShapeDtypeStruct((B, N, D), bf16),
                   jax.ShapeDtypeStruct((B, (N + tm) * (D // LANES), LANES), f32),
                   jax.ShapeDtypeStruct((B, TOP_K, N), i32), jax.ShapeDtypeStruct((B, TOP_K, N), f32),
                   jax.ShapeDtypeStruct((B, TOP_K, N), i32),
                   jax.ShapeDtypeStruct((B, N_EXP, LANES), f32)],
        scratch_shapes=[pltpu.VMEM((tm, D), f32), pltpu.VMEM((tm, D), f32)],
        compiler_params=_cparams(("arbitrary", "arbitrary")),
        name="post_mix",
    )(o, w_o, *resid, modtab, gain, rwt, rb)


META_W = 256
PLAN_ALIGN = 1024


def _round_up(n, m):
    return -(-n // m) * m


def _moe_sizes(n_tok):
    tm = MOE_TILE
    nt_max = (n_tok * TOP_K + N_EXP * (tm - 1)) // tm + 1
    ntp = _round_up(nt_max + MOE_GROUP, int(np.lcm(PLAN_ALIGN // tm, MOE_GROUP)))
    assert ntp <= META_W
    return ntp, _round_up(n_tok, PLAN_ALIGN)


def _plan_kernel(e_ref, r_ref, cnt_ref, pos_ref, meta_ref, *, n_tok):
    tm = MOE_TILE
    ntile = jnp.floor((cnt_ref[...] + (tm - 1.0)) * (1.0 / tm))
    ntb = ntile.astype(bf16)
    ei = lax.broadcasted_iota(i32, (N_EXP, LANES), 0)
    ej = lax.broadcasted_iota(i32, (N_EXP, LANES), 1)
    lower = jnp.where(ej <= ei, 1.0, 0.0)[:, :N_EXP].astype(bf16)
    tend = _dot(lower, ntb)
    tstart = tend - ntile
    tt = lax.broadcasted_iota(i32, (N_EXP, META_W), 1).astype(f32)
    te = jnp.sum(jnp.where(tt >= tend[:, 0:1], 1.0, 0.0), axis=0, keepdims=True)
    meta_ref[...] = jnp.zeros_like(meta_ref)
    meta_ref[0:1, :] = jnp.minimum(te, N_EXP - 1.0).astype(i32)
    meta_ref[1:2, :] = jnp.broadcast_to(tend[N_EXP - 1:N_EXP, 0:1], (1, META_W)).astype(i32)
    e = e_ref[...]
    base = jnp.zeros(e.shape, f32)
    for ex in range(N_EXP):
        base = jnp.where(e == ex, tstart[ex:ex + 1, 0:1] * float(tm), base)
    pos_ref[...] = jnp.zeros_like(pos_ref)
    pos_ref[:, 0:n_tok] = base.astype(i32) + r_ref[...]


def _plan(e_t, r_t, cnt):
    B, K, N = e_t.shape
    _, npad = _moe_sizes(N)
    return pl.pallas_call(
        functools.partial(_plan_kernel, n_tok=N),
        grid=(B,),
        in_specs=[pl.BlockSpec((None, K, N), lambda b: (b, 0, 0)),
                  pl.BlockSpec((None, K, N), lambda b: (b, 0, 0)),
                  pl.BlockSpec((None, N_EXP, LANES), lambda b: (b, 0, 0))],
        out_specs=[pl.BlockSpec((None, K, npad), lambda b: (b, 0, 0)),
                   pl.BlockSpec((None, SUBLANES, META_W), lambda b: (b, 0, 0))],
        out_shape=[jax.ShapeDtypeStruct((B, K, npad), i32), jax.ShapeDtypeStruct((B, SUBLANES, META_W), i32)],
        compiler_params=_cparams(("arbitrary",)),
        name="moe_plan",
    )(e_t, r_t, cnt)


def _plan_invert(pos, w_t, n_tok):
    B, K, npad = pos.shape
    ntp, _ = _moe_sizes(n_tok)
    plen = ntp * MOE_TILE
    nch = D // LANES
    mesh = plsc.VectorSubcoreMesh(core_axis_name="c", subcore_axis_name="s")
    n_cores = mesh.num_cores
    assert 2 * B <= n_cores * mesh.num_subcores and n_tok % SC_LANES == 0 and plen % SC_LANES == 0

    @functools.partial(
        pl.kernel, mesh=mesh,
        out_type=[jax.ShapeDtypeStruct((B * plen,), i32), jax.ShapeDtypeStruct((B * plen,), f32)],
        scratch_types=[pltpu.VMEM((npad,), i32), pltpu.VMEM((n_tok,), f32),
                       pltpu.VMEM((plen,), i32), pltpu.VMEM((plen,), f32)],
        compiler_params=dataclasses.replace(pltpu.CompilerParams(), needs_layout_passes=False))
    def invert(pos_hbm, w_hbm, rows_hbm, ws_hbm, pos_c, w_c, rows_v, ws_v):
        wid = lax.axis_index("s") * n_cores + lax.axis_index("c")
        b = wid % B
        lane = lax.iota(i32, SC_LANES)

        def load_pos(k):
            pltpu.sync_copy(pos_hbm.at[pl.ds(pl.multiple_of((b * K + k) * npad, SUBLANES), npad)], pos_c)

        @pl.when(wid < B)
        def _():
            pad = jnp.full((SC_LANES,), n_tok * nch, i32)

            @pl.loop(0, plen, step=SC_LANES)
            def _(i):
                rows_v[pl.ds(i, SC_LANES)] = pad

            for k in range(K):
                load_pos(k)

                @pl.loop(0, n_tok, step=SC_LANES)
                def _(n):
                    plsc.store_scatter(rows_v, [pos_c[pl.ds(n, SC_LANES)]], (lane + n) * nch)

            pltpu.sync_copy(rows_v, rows_hbm.at[pl.ds(pl.multiple_of(b * plen, SUBLANES), plen)])

        @pl.when(jnp.logical_and(wid >= B, wid < 2 * B))
        def _():
            zero = jnp.zeros((SC_LANES,), f32)

            @pl.loop(0, plen, step=SC_LANES)
            def _(i):
                ws_v[pl.ds(i, SC_LANES)] = zero

            for k in range(K):
                load_pos(k)
                pltpu.sync_copy(w_hbm.at[pl.ds(pl.multiple_of((b * K + k) * n_tok, SUBLANES), n_tok)], w_c)

                @pl.loop(0, n_tok, step=SC_LANES)
                def _(n):
                    plsc.store_scatter(ws_v, [pos_c[pl.ds(n, SC_LANES)]], w_c[pl.ds(n, SC_LANES)])

            pltpu.sync_copy(ws_v, ws_hbm.at[pl.ds(pl.multiple_of(b * plen, SUBLANES), plen)])

    return invert(pos.reshape(-1), w_t.reshape(-1))


def _moe_kernel(te_ref, nt_ref, *refs, ntp, n_tok):
    R = MOE_GROUP
    rg_ref, rs_ref, ws_ref, hlin_ref = refs[0:4]
    wgu_refs, wd_refs = refs[4:4 + R], refs[4 + R:4 + 2 * R]
    out_ref, acc_ref = refs[4 + 2 * R:6 + 2 * R]
    bufs = refs[6 + 2 * R:]
    xs_refs = [bufs[0:R], bufs[R:2 * R]]
    ylin_refs = [bufs[2 * R:3 * R], bufs[3 * R:4 * R]]
    b = pl.program_id(0)
    t = pl.program_id(1)
    TM = MOE_TILE
    NCH = D // LANES
    U = 8
    TMP = TM + SUBLANES
    ngrp = ntp // R

    @pl.when(t == 0)
    def _():
        acc_ref[...] = jnp.zeros_like(acc_ref)

    @pl.when(jnp.logical_and(b == 0, t == 0))
    def _():
        for buf in bufs:
            buf[...] = jnp.zeros_like(buf)

    def gather_rows(p, r, c):
        for m in range(c * U, (c + 1) * U):
            off = pl.multiple_of(rg_ref[r * TM + m], NCH)
            xs_refs[p][r][pl.ds(m, NCH, stride=TMP), :] = hlin_ref[pl.ds(off, NCH), :]

    def scatter_rows(p, r, c):
        offs = [pl.multiple_of(rs_ref[r * TM + c * U + u], NCH) for u in range(U)]
        news = [acc_ref[pl.ds(offs[u], NCH), :] + ylin_refs[p][r][pl.ds((c * U + u) * NCH, NCH), :]
                for u in range(U)]
        for u in range(U):
            acc_ref[pl.ds(offs[u], NCH), :] = news[u]

    def stage(p):
        row_work = [functools.partial(fn, p, r, c) for c in range(TM // U) for r in range(R)
                    for fn in (gather_rows, scatter_rows)]
        n_pieces = R * (EXP_FF // LANES + D // MXU_N)
        per_piece = -(-len(row_work) // n_pieces)

        def deal():
            for fn in row_work[:per_piece]:
                fn()
            del row_work[:per_piece]

        for r in range(R):
            x = jnp.concatenate([xs_refs[1 - p][r][pl.ds(j * TMP, TM), :] for j in range(NCH)], axis=1).astype(bf16)
            wcol = jnp.broadcast_to(ws_ref[r:r + 1, :], (SUBLANES, TM)).T[:, 0:1]
            gate = _dot(x, wgu_refs[r][:, :EXP_FF])
            deal()
            up = _dot(x, wgu_refs[r][:, EXP_FF:])
            deal()
            a = (_silu(gate) * up * wcol).astype(bf16)
            for c in range(D // MXU_N):
                y = _dot(a, wd_refs[r][:, c * MXU_N:(c + 1) * MXU_N])
                for jj in range(MXU_N // LANES):
                    j = c * (MXU_N // LANES) + jj
                    ylin_refs[1 - p][r][pl.ds(j, TM, stride=NCH), :] = y[:, jj * LANES:(jj + 1) * LANES]
                deal()
        while row_work:
            deal()

    live = (t - 2) * R < nt_ref[b]
    pl.when(jnp.logical_and(live, t % 2 == 0))(functools.partial(stage, 0))
    pl.when(jnp.logical_and(live, t % 2 == 1))(functools.partial(stage, 1))

    @pl.when(t >= ngrp + 2)
    def _():
        row0 = (t - (ngrp + 2)) * (ROW_TILE * NCH)
        for j in range(NCH):
            out_ref[:, j * LANES:(j + 1) * LANES] = acc_ref[pl.ds(row0 + j, ROW_TILE, stride=NCH), :].astype(bf16)


def _moe(te, nt, rows, wsort, hlin, w_gu, w_d, layer):
    B = hlin.shape[0]
    NCH = D // LANES
    n_tok = hlin.shape[1] // NCH - ROW_TILE
    assert n_tok % ROW_TILE == 0
    nf = n_tok // ROW_TILE
    TM = MOE_TILE
    R = MOE_GROUP
    ntp, _ = _moe_sizes(n_tok)
    assert ntp % R == 0
    ngrp = ntp // R

    def group_of(b, t, nt_ref, lag):
        grp = t - lag
        ok = jnp.logical_and(t >= lag, grp * R < nt_ref[b])
        return b * ngrp + jnp.where(ok, grp, ngrp - 1)

    def rows_spec(lag):
        return pl.BlockSpec((R * TM,), lambda b, t, te_ref, nt_ref: (group_of(b, t, nt_ref, lag),),
                            memory_space=pltpu.SMEM)

    ws_spec = pl.BlockSpec((None, R, TM), lambda b, t, te_ref, nt_ref: (group_of(b, t, nt_ref, 1), 0, 0))

    def w_spec(shape, r):
        def index(b, t, te_ref, nt_ref):
            tile = jnp.clip((t - 1) * R + r, 0, nt_ref[b] - 1)
            return (layer, te_ref[b * META_W + tile], 0, 0)
        return pl.BlockSpec((None, None) + shape, index)

    grid_spec = pltpu.PrefetchScalarGridSpec(
        num_scalar_prefetch=2,
        grid=(B, ngrp + 2 + nf),
        in_specs=([rows_spec(0), rows_spec(2), ws_spec,
                   pl.BlockSpec((None, (n_tok + ROW_TILE) * NCH, LANES), lambda b, t, *_: (b, 0, 0),
                                pipeline_mode=pl.Buffered(1))]
                  + [w_spec((D, 2 * EXP_FF), r) for r in range(R)]
                  + [w_spec((EXP_FF, D), r) for r in range(R)]),
        out_specs=pl.BlockSpec((None, ROW_TILE, D), lambda b, t, *_: (b, jnp.maximum(t - (ngrp + 2), 0), 0)),
        scratch_shapes=([pltpu.VMEM(((n_tok + SUBLANES) * NCH, LANES), f32)]
                        + [pltpu.VMEM(((TM + SUBLANES) * NCH, LANES), f32) for _ in range(2 * R)]
                        + [pltpu.VMEM((TM * NCH, LANES), f32) for _ in range(2 * R)]),
    )
    return pl.pallas_call(
        functools.partial(_moe_kernel, ntp=ntp, n_tok=n_tok),
        grid_spec=grid_spec,
        out_shape=jax.ShapeDtypeStruct((B, n_tok, D), bf16),
        compiler_params=_cparams(("arbitrary", "arbitrary"), MOE_VMEM_LIMIT),
        name="moe",
    )(te, nt, rows, rows, wsort.reshape(B * ngrp, R, TM), hlin, *([w_gu] * R), *([w_d] * R))


def _routed_experts(e_t, w_t, r_t, cnt, hlin, w_gu, w_d, layer):
    N = e_t.shape[2]
    pos, meta = _plan(e_t, r_t, cnt)
    te = meta[:, 0, :].reshape(-1)
    nt = meta[:, 1, 0]
    rows, wsort = _plan_invert(pos, w_t, N)
    return _moe(te, nt, rows, wsort, hlin, w_gu, w_d, layer)


def _shared_ffn(fin, shgu_ref, shd_ref):
    gu = _dot(fin, shgu_ref[...])
    return _dot((_silu(gu[:, :SH_FF]) * gu[:, SH_FF:]).astype(bf16), shd_ref[...])


def _post_ffn_mla_kernel(x1_ref, routed_ref, fin_ref, shgu_ref, shd_ref, mod0_ref, mod1_ref, g_ref, win_ref,
                         x2_ref, a_ref):
    x2 = x1_ref[...] + mod0_ref[...][:, 5 * D:6 * D] * (routed_ref[...] + _shared_ffn(fin_ref[...], shgu_ref, shd_ref))
    x2_ref[...] = x2
    mod1 = mod1_ref[...]
    h = (_rms(x2) * g_ref[...]) * (1.0 + mod1[:, D:2 * D]) + mod1[:, 0:D]
    a_ref[...] = _dot(h.astype(bf16), win_ref[...])


def _post_ffn_mla(x1, routed, fin, sh_gu, sh_d, modtab0, modtab1, gain, w_in, n_ctx_tiles):
    B, T, _ = x1.shape
    tm = ROW_TILE
    tok = lambda w: pl.BlockSpec((None, tm, w), lambda b, i: (b, i, 0))
    modspec = lambda: pl.BlockSpec((None, None, 1, 6 * D), lambda b, i: (b, jnp.where(i < n_ctx_tiles, 0, 1), 0, 0))
    full = lambda r, c: pl.BlockSpec((r, c), lambda b, i: (0, 0))
    return pl.pallas_call(
        _post_ffn_mla_kernel,
        grid=(B, T // tm),
        in_specs=[tok(D), tok(D), tok(D), full(D, 2 * SH_FF), full(SH_FF, D), modspec(), modspec(),
                  full(1, D), full(D, MLA_IN_PAD)],
        out_specs=[tok(D), tok(MLA_IN_PAD)],
        out_shape=[jax.ShapeDtypeStruct((B, T, D), f32), jax.ShapeDtypeStruct((B, T, MLA_IN_PAD), f32)],
        compiler_params=_cparams(("arbitrary", "arbitrary")),
        name="post_ffn_mla",
    )(x1, routed, fin, sh_gu, sh_d, modtab0, modtab1, gain, w_in)


def _post_ffn_final_kernel(x1_ref, routed_ref, fin_ref, shgu_ref, shd_ref, mod_ref, out_ref):
    out_ref[...] = x1_ref[...] + mod_ref[...][:, 5 * D:6 * D] * (
        routed_ref[...] + _shared_ffn(fin_ref[...], shgu_ref, shd_ref))


def _post_ffn_final(x1, routed, fin, sh_gu, sh_d, modtab):
    B, N, _ = x1.shape
    tm = ROW_TILE
    tok = lambda w: pl.BlockSpec((None, tm, w), lambda b, i: (b, i, 0))
    full = lambda r, c: pl.BlockSpec((r, c), lambda b, i: (0, 0))
    return pl.pallas_call(
        _post_ffn_final_kernel,
        grid=(B, N // tm),
        in_specs=[tok(D), tok(D), tok(D), full(D, 2 * SH_FF), full(SH_FF, D),
                  pl.BlockSpec((None, None, 1, 6 * D), lambda b, i: (b, 1, 0, 0))],
        out_specs=tok(D),
        out_shape=jax.ShapeDtypeStruct((B, N, D), f32),
        compiler_params=_cparams(("arbitrary", "arbitrary")),
        name="post_ffn_final",
    )(x1, routed, fin, sh_gu, sh_d, modtab)


def _mla_prep_kernel(a_ref, qan_ref, wqn_ref, wqr_ref, kvan_ref, wk_ref, wv_ref, qnn_ref, qnr_ref, knn_ref, knr_ref,
                     cos_ref, sin_ref, q_ref, k_ref, v_ref):
    a = a_ref[...]
    tm = a.shape[0]
    scale = MLA_QK ** -0.5 * float(np.log2(np.e))
    cos = cos_ref[...]
    sin = sin_ref[...]
    lane = lax.broadcasted_iota(i32, (tm, LANES), 1)
    first = (lane // (MLA_ROPE // 4)) % 2 == 0

    def rope(xb):
        sw = jnp.where(first, pltpu.roll(xb, LANES - MLA_ROPE // 4, axis=1), pltpu.roll(xb, MLA_ROPE // 4, axis=1))
        return xb * cos + sw * sin

    qa = (_rms(a[:, :MLA_QR]) * qan_ref[...]).astype(bf16)
    qn = _dot(qa, wqn_ref[...])
    qr = _dot(qa, wqr_ref[...])
    ri = lax.broadcasted_iota(i32, (LANES, LANES), 0) // MLA_ROPE
    ci = lax.broadcasted_iota(i32, (LANES, LANES), 1) // MLA_ROPE
    seg = jnp.where(ri == ci, 1.0, 0.0).astype(bf16)

    def seg_sum(sq):
        hi = sq.astype(bf16)
        r1 = sq - hi.astype(f32)
        mid = r1.astype(bf16)
        lo = (r1 - mid.astype(f32)).astype(bf16)
        return _dot(hi, seg) + _dot(mid, seg) + _dot(lo, seg)

    qr_blocks = []
    for p in range(MLA_H // 2):
        blk = qr[:, p * LANES:(p + 1) * LANES]
        blk = blk * lax.rsqrt(seg_sum(blk * blk) * (1.0 / MLA_ROPE) + EPS) * qnr_ref[:, p * LANES:(p + 1) * LANES]
        qr_blocks.append(rope(blk) * scale)

    kv = (_rms(a[:, MLA_QR:MLA_QR + MLA_KVR]) * kvan_ref[...]).astype(bf16)
    kn = _dot(kv, wk_ref[...])
    v_ref[...] = _dot_nt(wv_ref[...], kv).astype(bf16)
    kr = a[:, MLA_QR + MLA_KVR:MLA_IN_PAD]
    kr = rope(_rms(kr, MLA_ROPE) * knr_ref[...])
    kr_odd = pltpu.roll(kr, MLA_ROPE, axis=1)
    for hd in range(MLA_H):
        sl = slice(hd * MLA_NOPE, (hd + 1) * MLA_NOPE)
        q_ref[:, 2 * hd * LANES:(2 * hd + 1) * LANES] = (_rms(qn[:, sl]) * qnn_ref[...] * scale).astype(bf16)
        q_ref[:, (2 * hd + 1) * LANES:(2 * hd + 2) * LANES] = qr_blocks[hd // 2].astype(bf16)
        k_ref[:, 2 * hd * LANES:(2 * hd + 1) * LANES] = (_rms(kn[:, sl]) * knn_ref[...]).astype(bf16)
        k_ref[:, (2 * hd + 1) * LANES:(2 * hd + 2) * LANES] = (kr if hd % 2 == 0 else kr_odd).astype(bf16)


def _mla_prep(a, qan, wqn, wqr, kvan, wk, wv, qnn, qnr, knn, knr, cos_t, sin_t, n_ctx_tiles):
    B, T, _ = a.shape
    tm = ROW_TILE
    tok = lambda w: pl.BlockSpec((None, tm, w), lambda b, i: (b, i, 0))
    full = lambda r, c: pl.BlockSpec((r, c), lambda b, i: (0, 0))
    hw = 2 * LANES * MLA_H
    q_spec = pl.BlockSpec((None, tm, hw), lambda b, i: (b, jnp.maximum(i - n_ctx_tiles, 0), 0))
    return pl.pallas_call(
        _mla_prep_kernel,
        grid=(B, T // tm),
        in_specs=[tok(MLA_IN_PAD), full(1, MLA_QR), full(MLA_QR, MLA_H * MLA_NOPE), full(MLA_QR, MLA_H * MLA_ROPE),
                  full(1, MLA_KVR), full(MLA_KVR, MLA_H * MLA_NOPE), full(MLA_H * MLA_V, MLA_KVR),
                  full(1, MLA_NOPE), full(1, MLA_H * MLA_ROPE), full(1, MLA_NOPE), full(1, LANES),
                  pl.BlockSpec((tm, LANES), lambda b, i: (i, 0)), pl.BlockSpec((tm, LANES), lambda b, i: (i, 0))],
        out_specs=[q_spec, tok(hw), pl.BlockSpec((None, MLA_H * MLA_V, tm), lambda b, i: (b, 0, i))],
        out_shape=[jax.ShapeDtypeStruct((B, T - n_ctx_tiles * tm, hw), bf16), jax.ShapeDtypeStruct((B, T, hw), bf16),
                   jax.ShapeDtypeStruct((B, MLA_H * MLA_V, T), bf16)],
        compiler_params=_cparams(("arbitrary", "arbitrary")),
        name="mla_prep",
    )(a, qan, wqn, wqr, kvan, wk, wv, qnn, qnr, knn, knr, cos_t, sin_t)


ATT_KV_CHUNK = 256
ATT_TQ = 256
ATT_SUB = 1


def _mla_attn_kernel(q_ref, k_ref, vt_ref, o_ref, *scratch, n_sub):
    tq = q_ref.shape[0] // n_sub
    ck = ATT_KV_CHUNK
    nck = k_ref.shape[0] // ck
    i = pl.program_id(0)

    s_refs = [scratch[0:n_sub], scratch[n_sub:2 * n_sub]]
    m_refs = [scratch[2 * n_sub:3 * n_sub], scratch[3 * n_sub:4 * n_sub]]

    @pl.when(i == 0)
    def _():
        for ref in scratch:
            ref[...] = jnp.zeros_like(ref)

    def fold(a, op):
        return op(a.reshape(ck // SUBLANES, SUBLANES, tq), axis=0)

    def stage(par):
        for u in range(n_sub):
            q = q_ref[u * tq:(u + 1) * tq, :]
            s = _dot_nt(k_ref[...], q)
            s_refs[par][u][...] = s
            m_refs[par][u][...] = jnp.max(s.reshape(nck * ck // SUBLANES, SUBLANES, tq), axis=0)
        for u in range(n_sub):
            m = jnp.max(m_refs[1 - par][u][...], axis=0, keepdims=True)
            p = jnp.exp2(s_refs[1 - par][u][...] - m)
            lsum = jnp.sum(jnp.sum(p.reshape(nck * ck // SUBLANES, SUBLANES, tq), axis=0), axis=0, keepdims=True)
            acc = _dot(vt_ref[...], p.astype(bf16))
            o_ref[:, u * tq:(u + 1) * tq] = (acc / lsum).astype(bf16)

    pl.when(i % 2 == 0)(functools.partial(stage, 0))
    pl.when(i % 2 == 1)(functools.partial(stage, 1))


def _mla_attn(q, k, vt):
    B, S, _ = q.shape
    T = k.shape[1]
    n_sub = ATT_SUB if S % (ATT_SUB * ATT_TQ) == 0 else 1
    tq = n_sub * ATT_TQ
    nq = S // tq
    ntile = B * MLA_H * nq
    assert T % ATT_KV_CHUNK == 0

    def tile(g):
        g = jnp.clip(g, 0, ntile - 1)
        return g // (MLA_H * nq), (g // nq) % MLA_H, g % nq

    def q_index(g):
        b, h, i = tile(g)
        return b, i, h

    def k_index(g):
        b, h, _ = tile(g)
        return b, 0, h

    def vt_index(g):
        b, h, _ = tile(g - 1)
        return b, h, 0

    def o_index(g):
        b, h, i = tile(g - 1)
        return b, h, i

    return pl.pallas_call(
        functools.partial(_mla_attn_kernel, n_sub=n_sub),
        grid=(ntile + 1,),
        in_specs=[pl.BlockSpec((None, tq, 2 * LANES), q_index),
                  pl.BlockSpec((None, T, 2 * LANES), k_index),
                  pl.BlockSpec((None, MLA_V, T), vt_index)],
        out_specs=pl.BlockSpec((None, MLA_V, tq), o_index),
        out_shape=jax.ShapeDtypeStruct((B, MLA_H * MLA_V, S), bf16),
        scratch_shapes=([pltpu.VMEM((T, ATT_TQ), f32) for _ in range(2 * n_sub)]
                        + [pltpu.VMEM((SUBLANES, ATT_TQ), f32) for _ in range(2 * n_sub)]),
        compiler_params=_cparams(("arbitrary",)),
        name="mla_attn",
    )(q, k, vt)


def _axial_angles(rows_n, rot_dim):
    axis_dim = rot_dim // 2
    inv = ROPE_BASE ** (-jnp.arange(0, axis_dim, 2, dtype=f32) / axis_dim)
    row = jnp.repeat(jnp.arange(rows_n, dtype=f32), GRID_W)
    col = jnp.tile(jnp.arange(GRID_W, dtype=f32), rows_n)
    return row[:, None] * inv, col[:, None] * inv


def _rope_tables(seq, n_ctx, rot_dim, reps):
    ang_r, ang_c = _axial_angles(seq // GRID_W, rot_dim)
    cos = jnp.concatenate([jnp.cos(ang_r)] * 2 + [jnp.cos(ang_c)] * 2, axis=1)
    sin = jnp.concatenate([-jnp.sin(ang_r), jnp.sin(ang_r), -jnp.sin(ang_c), jnp.sin(ang_c)], axis=1)
    cos = jnp.concatenate([jnp.ones((n_ctx, rot_dim), f32), cos], axis=0)
    sin = jnp.concatenate([jnp.zeros((n_ctx, rot_dim), f32), sin], axis=0)
    return jnp.tile(cos, (1, reps)), jnp.tile(sin, (1, reps))


def kernel(x, c, ctx, c_ctx, ada_w, ada_b, norm_mix, norm_ffn, ret_w_in, ret_decay_f, ret_decay_b, ret_w_o,
           mla_w_in, mla_q_a_norm, mla_w_q_b, mla_kv_a_norm, mla_w_kv_b, mla_q_norm, mla_k_norm, mla_w_o,
           router_w, router_bias, exp_w_gu, exp_w_down, sh_w_gu, sh_w_down):
    B, S, _ = x.shape
    n_ctx = ctx.shape[1]
    assert n_ctx % ROW_TILE == 0 and S % ROW_TILE == 0 and S % GRID_W == 0
    n_ctx_tiles = n_ctx // ROW_TILE

    rows = -(-(B + 1) // SUBLANES) * SUBLANES
    cc = jnp.zeros((rows, D), f32).at[:B].set(c).at[B].set(c_ctx)
    mod = _ada(cc, ada_w, ada_b)

    def modtab(i):
        ctx_row = jnp.broadcast_to(mod[i, B][None, :], (B, 6 * D))
        return jnp.stack([ctx_row, mod[i, :B]], axis=1)[:, :, None, :]

    mod0, mod1 = modtab(0), modtab(1)

    cos_r, sin_r = _rope_tables(S, n_ctx, RET_DK, 1)
    q, k, v, gf, gb = _ret_inproj(ctx, x, mod0, norm_mix[0][None, :], ret_w_in[0].astype(bf16), cos_r, sin_r,
                                  n_ctx_tiles)
    dtab = jnp.broadcast_to(jnp.concatenate([ret_decay_f[0], ret_decay_b[0]])[:, None], (2 * RET_H, LANES))
    o = _ret_scan(dtab, q, k, v, gf, gb, n_ctx)
    x1, fin, hlin, e_t, w_t, r_t, cnt = _post_mix(
        o, ret_w_o[0].astype(bf16), (ctx, x), 0, mod0, n_ctx_tiles, norm_ffn[0][None, :],
        router_w[0].T, router_bias[0][:, None])
    exp_gu, exp_d = exp_w_gu.astype(bf16), exp_w_down.astype(bf16)
    routed = _routed_experts(e_t, w_t, r_t, cnt, hlin, exp_gu, exp_d, 0)
    w_in1 = jnp.zeros((D, MLA_IN_PAD), f32).at[:, :mla_w_in.shape[2]].set(mla_w_in[0]).astype(bf16)
    x2, a = _post_ffn_mla(x1, routed, fin, sh_w_gu[0].astype(bf16), sh_w_down[0].astype(bf16), mod0, mod1,
                          norm_mix[1][None, :], w_in1, n_ctx_tiles)

    wq = mla_w_q_b[0].reshape(MLA_QR, MLA_H, MLA_QK)
    wqn = wq[:, :, :MLA_NOPE].reshape(MLA_QR, MLA_H * MLA_NOPE).astype(bf16)
    wqr = wq[:, :, MLA_NOPE:].reshape(MLA_QR, MLA_H * MLA_ROPE).astype(bf16)
    wkv = mla_w_kv_b[0].reshape(MLA_KVR, MLA_H, MLA_NOPE + MLA_V)
    wk = wkv[:, :, :MLA_NOPE].reshape(MLA_KVR, MLA_H * MLA_NOPE).astype(bf16)
    wv = wkv[:, :, MLA_NOPE:].reshape(MLA_KVR, MLA_H * MLA_V).T.astype(bf16)
    qnn = mla_q_norm[0][None, :MLA_NOPE]
    qnr = jnp.tile(mla_q_norm[0][None, MLA_NOPE:], (1, MLA_H))
    knn = mla_k_norm[0][None, :MLA_NOPE]
    knr = jnp.concatenate([mla_k_norm[0][MLA_NOPE:], jnp.zeros((LANES - MLA_ROPE,), f32)])[None, :]
    cos_m, sin_m = _rope_tables(S, n_ctx, MLA_ROPE, LANES // MLA_ROPE)
    qf, kf, vf = _mla_prep(a, mla_q_a_norm[0][None, :], wqn, wqr, mla_kv_a_norm[0][None, :], wk, wv,
                           qnn, qnr, knn, knr, cos_m, sin_m, n_ctx_tiles)
    o1 = _mla_attn(qf, kf, vf)
    x3, fin1, hlin1, e1, w1, r1, cnt1 = _post_mix(
        o1, mla_w_o[0].astype(bf16), (x2,), n_ctx_tiles, mod1, 0, norm_ffn[1][None, :],
        router_w[1].T, router_bias[1][:, None], o_transposed=True)
    routed1 = _routed_experts(e1, w1, r1, cnt1, hlin1, exp_gu, exp_d, 1)
    return _post_ffn_final(x3, routed1, fin1, sh_w_gu[1].astype(bf16), sh_w_down[1].astype(bf16), mod1)
```

```python
import dataclasses
import functools

import jax
import jax.numpy as jnp
import numpy as np
from jax import lax
from jax.experimental import pallas as pl
from jax.experimental.pallas import tpu as pltpu
from jax.experimental.pallas import tpu_sc as plsc

f32 = jnp.float32
bf16 = jnp.bfloat16
i32 = jnp.int32

D = 1024
GRID_W = 64
EPS = 1e-6
ROPE_BASE = 10000.0
RET_H = 4
RET_DK = 256
RET_DV = 512
RET_VW = RET_H * RET_DV
RET_CHUNK = 256
MLA_H = 8
MLA_NOPE = 128
MLA_ROPE = 64
MLA_QK = MLA_NOPE + MLA_ROPE
MLA_V = 128
MLA_QR = 384
MLA_KVR = 256
MLA_IN_PAD = 768
N_EXP = 64
TOP_K = 8
N_GRP = 8
TOPK_GRP = 4
EXP_FF = 256
SH_FF = 256
ROUTED_SCALE = 2.5

LANES = 128
SUBLANES = 8
SC_LANES = 16
MXU_N = 256
ROW_TILE = 256
MOE_TILE = 256
MOE_GROUP = 2
ADA_COLS = 1536
PROJ_COLS = 512
VMEM_LIMIT = 56 * 1024 * 1024
MOE_VMEM_LIMIT = 62 * 1024 * 1024


def _cparams(sem, vmem=VMEM_LIMIT):
    return pltpu.CompilerParams(dimension_semantics=sem, vmem_limit_bytes=vmem)


def _sigmoid(x):
    return 1.0 / (1.0 + jnp.exp(-x))


def _silu(x):
    return x * _sigmoid(x)


def _rms(x, n=None):
    n = x.shape[-1] if n is None else n
    return x * lax.rsqrt(jnp.sum(x * x, axis=-1, keepdims=True) * (1.0 / n) + EPS)


def _dot(a, b):
    return jnp.dot(a, b, preferred_element_type=f32)


def _dot_nt(a, b):
    return lax.dot_general(a, b, (((1,), (1,)), ((), ())), preferred_element_type=f32)


def _ada_kernel(c_ref, w_ref, b_ref, o_ref):
    s = _silu(c_ref[...]).astype(bf16)
    o_ref[...] = _dot(s, w_ref[...].astype(bf16)) + b_ref[...]


def _ada(cc, ada_w, ada_b):
    depth = ada_w.shape[0]
    rows = cc.shape[0]
    tn = ADA_COLS
    return pl.pallas_call(
        _ada_kernel,
        grid=(depth, 6 * D // tn),
        in_specs=[pl.BlockSpec((rows, D), lambda i, j: (0, 0)),
                  pl.BlockSpec((None, D, tn), lambda i, j: (i, 0, j)),
                  pl.BlockSpec((None, 1, tn), lambda i, j: (i, 0, j))],
        out_specs=pl.BlockSpec((None, rows, tn), lambda i, j: (i, 0, j)),
        out_shape=jax.ShapeDtypeStruct((depth, rows, 6 * D), f32),
        compiler_params=_cparams(("arbitrary", "arbitrary")),
        name="ada",
    )(cc, ada_w, ada_b.reshape(depth, 1, 6 * D))


def _stream_tile(ctx_ref, x_ref, n_ctx_tiles):
    return jnp.where(pl.program_id(1) < n_ctx_tiles, ctx_ref[...], x_ref[...])


def _stream_specs(tm, n_ctx_tiles, last=None):
    clamp = (lambda i: i) if last is None else (lambda i: jnp.minimum(i, last))
    return [pl.BlockSpec((None, tm, D), lambda b, i: (b, jnp.minimum(clamp(i), n_ctx_tiles - 1), 0)),
            pl.BlockSpec((None, tm, D), lambda b, i: (b, jnp.maximum(clamp(i) - n_ctx_tiles, 0), 0))]


def _ret_inproj_kernel(ctx_ref, x_ref, mod_ref, g_ref, w_ref, cos_ref, sin_ref, q_ref, k_ref, v_ref, gf_ref, gb_ref,
                       *, n_ctx_tiles):
    x = _stream_tile(ctx_ref, x_ref, n_ctx_tiles)
    mod = mod_ref[...]
    h = (_rms(x) * g_ref[...]) * (1.0 + mod[:, D:2 * D]) + mod[:, 0:D]
    hb = h.astype(bf16)
    cos = cos_ref[...]
    sin = sin_ref[...]

    def rope(a):
        outs = []
        for half in range(2):
            sl = slice(half * LANES, (half + 1) * LANES)
            ah = a[:, sl]
            outs.append(ah * cos[:, sl] + pltpu.roll(ah, LANES // 2, axis=1) * sin[:, sl])
        return jnp.concatenate(outs, axis=1)

    for hd in range(RET_H):
        sl = slice(hd * RET_DK, (hd + 1) * RET_DK)
        q_ref[:, sl] = rope(_dot(hb, w_ref[:, sl])).astype(bf16)
    for hd in range(RET_H):
        sl = slice(hd * RET_DK, (hd + 1) * RET_DK)
        wsl = slice(D + hd * RET_DK, D + (hd + 1) * RET_DK)
        k_ref[:, sl] = (rope(_dot(hb, w_ref[:, wsl])) * (RET_DK ** -0.5)).astype(bf16)
    cw = PROJ_COLS
    for c in range(RET_VW // cw):
        sl = slice(c * cw, (c + 1) * cw)
        v_ref[:, sl] = _dot(hb, w_ref[:, 2 * D + c * cw:2 * D + (c + 1) * cw]).astype(bf16)
        gf_ref[:, sl] = _silu(_dot(hb, w_ref[:, 2 * D + RET_VW + c * cw:2 * D + RET_VW + (c + 1) * cw])).astype(bf16)
        gb_ref[:, sl] = _silu(_dot(hb, w_ref[:, 2 * D + 2 * RET_VW + c * cw:2 * D + 2 * RET_VW + (c + 1) * cw])).astype(bf16)


def _ret_inproj(ctx, x, modtab, gain, w_in, cos_t, sin_t, n_ctx_tiles):
    B = x.shape[0]
    T = ctx.shape[1] + x.shape[1]
    tm = ROW_TILE
    n_in = w_in.shape[1]
    tok = lambda w: pl.BlockSpec((None, tm, w), lambda b, i: (b, i, 0))
    return pl.pallas_call(
        functools.partial(_ret_inproj_kernel, n_ctx_tiles=n_ctx_tiles),
        grid=(B, T // tm),
        in_specs=_stream_specs(tm, n_ctx_tiles) + [
                  pl.BlockSpec((None, None, 1, 6 * D), lambda b, i: (b, jnp.where(i < n_ctx_tiles, 0, 1), 0, 0)),
                  pl.BlockSpec((1, D), lambda b, i: (0, 0)),
                  pl.BlockSpec((D, n_in), lambda b, i: (0, 0), pipeline_mode=pl.Buffered(1)),
                  pl.BlockSpec((tm, RET_DK), lambda b, i: (i, 0)),
                  pl.BlockSpec((tm, RET_DK), lambda b, i: (i, 0))],
        out_specs=[tok(D), tok(D), tok(RET_VW), tok(RET_VW), tok(RET_VW)],
        out_shape=[jax.ShapeDtypeStruct((B, T, D), bf16), jax.ShapeDtypeStruct((B, T, D), bf16),
                   jax.ShapeDtypeStruct((B, T, RET_VW), bf16), jax.ShapeDtypeStruct((B, T, RET_VW), bf16),
                   jax.ShapeDtypeStruct((B, T, RET_VW), bf16)],
        compiler_params=_cparams(("arbitrary", "arbitrary")),
        name="ret_inproj",
    )(ctx, x, modtab, gain, w_in, cos_t, sin_t)


def _ret_chunk_index(t, nc, ncc):
    u = t - nc
    back = jnp.where(u < ncc, ncc - 1 - u, nc - 1 - u + ncc)
    return jnp.where(t < nc, t, back)


def _ret_scan_kernel(dt_ref, q_ref, k_ref, v_ref, gf_ref, gb_ref, o_ref,
                     s_ref, of_ref, mask_ref, dq_ref, dk_ref, dc_ref, *, nc, ncc):
    t = pl.program_id(1)
    C = RET_CHUNK

    def init(direction):
        s_ref[...] = jnp.zeros_like(s_ref)
        ii = lax.broadcasted_iota(i32, (C, C), 0)
        jj = lax.broadcasted_iota(i32, (C, C), 1)
        rel = (ii - jj if direction == 0 else jj - ii).astype(f32)
        pos = lax.broadcasted_iota(i32, (C, 1), 0).astype(f32)
        for hd in range(RET_H):
            r = direction * RET_H + hd
            lg = -jnp.exp(dt_ref[r:r + 1, :])
            lg1 = lg[:, 0:1]
            mask_ref[hd] = jnp.where(rel >= 0, jnp.exp(lg1 * jnp.maximum(rel, 0.0)), 0.0)
            if direction == 0:
                dq_ref[hd] = jnp.exp(lg1 * (pos + 1.0))
                dk_ref[hd] = jnp.exp(lg1 * (C - 1.0 - pos))
            else:
                dq_ref[hd] = jnp.exp(lg1 * (C - pos))
                dk_ref[hd] = jnp.exp(lg1 * pos)
            dc_ref[hd] = jnp.exp(lg * float(C))

    pl.when(t == 0)(functools.partial(init, 0))
    pl.when(t == nc)(functools.partial(init, 1))

    row0 = pl.multiple_of(_ret_chunk_index(t, nc, ncc) * C, C)

    def step(forward):
        for hd in range(RET_H):
            ks = slice(hd * RET_DK, (hd + 1) * RET_DK)
            vs = slice(hd * RET_DV, (hd + 1) * RET_DV)
            qh = q_ref[:, ks]
            kh = k_ref[:, ks]
            vh = v_ref[:, vs]
            p = (_dot_nt(qh, kh) * mask_ref[hd]).astype(bf16)
            y = _dot(p, vh) + _dot(qh, s_ref[hd].astype(bf16)) * dq_ref[hd]
            kd = (kh.astype(f32) * dk_ref[hd]).astype(bf16)
            upd = lax.dot_general(kd, vh, (((0,), (0,)), ((), ())), preferred_element_type=f32)
            s_ref[hd] = s_ref[hd] * dc_ref[hd][0:1, 0:1] + upd
            yn = _rms(y)
            if forward:
                of_ref[pl.ds(row0, C), vs] = (gf_ref[:, vs].astype(f32) * yn).astype(bf16)
            else:
                o_ref[:, vs] = (of_ref[pl.ds(row0, C), vs].astype(f32) + gb_ref[:, vs].astype(f32) * yn).astype(bf16)

    pl.when(t < nc)(functools.partial(step, True))
    pl.when(t >= nc)(functools.partial(step, False))


def _ret_scan(dtab, q, k, v, gf, gb, n_ctx):
    B, T, _ = q.shape
    C = RET_CHUNK
    nc = T // C
    ncc = n_ctx // C
    cidx = functools.partial(_ret_chunk_index, nc=nc, ncc=ncc)
    first_back = ncc - 1
    return pl.pallas_call(
        functools.partial(_ret_scan_kernel, nc=nc, ncc=ncc),
        grid=(B, 2 * nc),
        in_specs=[pl.BlockSpec((2 * RET_H, LANES), lambda b, t: (0, 0)),
                  pl.BlockSpec((None, C, D), lambda b, t: (b, cidx(t), 0)),
                  pl.BlockSpec((None, C, D), lambda b, t: (b, cidx(t), 0)),
                  pl.BlockSpec((None, C, RET_VW), lambda b, t: (b, cidx(t), 0)),
                  pl.BlockSpec((None, C, RET_VW), lambda b, t: (b, jnp.where(t < nc, t, nc - 1), 0)),
                  pl.BlockSpec((None, C, RET_VW), lambda b, t: (b, jnp.where(t < nc, first_back, cidx(t)), 0))],
        out_specs=pl.BlockSpec((None, C, RET_VW), lambda b, t: (b, jnp.where(t < nc, first_back, cidx(t)), 0)),
        out_shape=jax.ShapeDtypeStruct((B, T, RET_VW), bf16),
        scratch_shapes=[pltpu.VMEM((RET_H, RET_DK, RET_DV), f32),
                        pltpu.VMEM((T, RET_VW), bf16),
                        pltpu.VMEM((RET_H, C, C), f32),
                        pltpu.VMEM((RET_H, C, 1), f32),
                        pltpu.VMEM((RET_H, C, 1), f32),
                        pltpu.VMEM((RET_H, 1, LANES), f32)],
        compiler_params=_cparams(("arbitrary", "arbitrary")),
        name="ret_scan",
    )(dtab, q, k, v, gf, gb)


def _route(f, rwt_ref, rb_ref, cnt_ref, e_ref, w_ref, r_ref):
    tm = f.shape[0]
    G = N_EXP // N_GRP
    logits = _dot_nt(rwt_ref[...].astype(bf16), f.astype(bf16))
    s = _sigmoid(logits)
    sel = s + rb_ref[...]
    mi = lax.broadcasted_iota(i32, (G, tm), 0)
    neg = -jnp.inf
    s_g = [s[g * G:(g + 1) * G, :] for g in range(N_GRP)]
    sel_g = [sel[g * G:(g + 1) * G, :] for g in range(N_GRP)]

    def first_max(a, ids, big):
        mx = jnp.max(a, axis=0, keepdims=True)
        ix = jnp.min(jnp.where(a == mx, ids, big), axis=0, keepdims=True)
        return mx, ix

    gscore = jnp.zeros((N_GRP, tm), f32)
    gi = lax.broadcasted_iota(i32, (N_GRP, tm), 0)
    for g in range(N_GRP):
        t1, i1 = first_max(sel_g[g], mi, G)
        t2 = jnp.max(jnp.where(mi == i1, neg, sel_g[g]), axis=0, keepdims=True)
        gscore = jnp.where(gi == g, t1 + t2, gscore)
    yield
    gmask = jnp.zeros((N_GRP, tm), i32)
    cur = gscore
    for _ in range(TOPK_GRP):
        _, ix = first_max(cur, gi, N_GRP)
        hit = gi == ix
        gmask = jnp.where(hit, 1, gmask)
        cur = jnp.where(hit, neg, cur)
    cand = [jnp.where(gmask[g:g + 1, :] > 0, sel_g[g], neg) for g in range(N_GRP)]
    ids = [mi + g * G for g in range(N_GRP)]

    def across(parts, op):
        acc = parts[0]
        for part in parts[1:]:
            acc = op(acc, part)
        return acc

    e_rows, w_rows = [], []
    for _ in range(TOP_K):
        mx = jnp.max(across(cand, jnp.maximum), axis=0, keepdims=True)
        ix = jnp.min(across([jnp.where(cand[g] == mx, ids[g], N_EXP) for g in range(N_GRP)], jnp.minimum),
                     axis=0, keepdims=True)
        hits = [ids[g] == ix for g in range(N_GRP)]
        cand = [jnp.where(hits[g], neg, cand[g]) for g in range(N_GRP)]
        wv = jnp.sum(across([jnp.where(hits[g], s_g[g], 0.0) for g in range(N_GRP)], jnp.add), axis=0, keepdims=True)
        e_rows.append(ix)
        w_rows.append(wv)
        yield
    wsum = w_rows[0]
    for r in range(1, TOP_K):
        wsum = wsum + w_rows[r]

    selm = [jnp.zeros((G, tm), f32) for _ in range(N_GRP)]
    for r in range(TOP_K):
        for g in range(N_GRP):
            selm[g] = jnp.where(ids[g] == e_rows[r], 1.0, selm[g])
    m_all = jnp.concatenate(selm, axis=0)
    ri = lax.broadcasted_iota(i32, (tm, tm), 0)
    ci = lax.broadcasted_iota(i32, (tm, tm), 1)
    upper = jnp.where(ri <= ci, 1.0, 0.0).astype(bf16)
    incl = _dot(m_all.astype(bf16), upper)
    carry = cnt_ref[:, 0:1]
    rank_all = carry + incl - m_all
    cnt_ref[...] = cnt_ref[...] + incl[:, tm - 1:tm]
    yield
    for r in range(TOP_K):
        rk = jnp.sum(across([jnp.where(ids[g] == e_rows[r], rank_all[g * G:(g + 1) * G, :], 0.0)
                             for g in range(N_GRP)], jnp.add), axis=0, keepdims=True)
        e_ref[r:r + 1, :] = e_rows[r]
        w_ref[r:r + 1, :] = w_rows[r] / wsum * ROUTED_SCALE
        r_ref[r:r + 1, :] = rk.astype(i32)


def _post_mix_kernel(o_ref, wo_ref, *refs, nt, o_transposed, split_ctx_tiles):
    n_resid = 2 if split_ctx_tiles else 1
    resid = refs[:n_resid]
    (mod_ref, g_ref, rwt_ref, rb_ref, x1_ref, fin_ref, hlin_ref, e_ref, w_ref, r_ref, cnt_ref,
     f0_ref, f1_ref) = refs[n_resid:]
    f_refs = (f0_ref, f1_ref)
    i = pl.program_id(1)

    @pl.when(i == 0)
    def _():
        cnt_ref[...] = jnp.zeros_like(cnt_ref)

    def route(p):
        return _route(f_refs[p][...], rwt_ref, rb_ref, cnt_ref, e_ref, w_ref, r_ref)

    def mix(p, other=()):
        other = iter(other)

        def advance(n):
            for _ in range(n):
                next(other, None)

        mod = mod_ref[...]
        x = _stream_tile(resid[0], resid[1], split_ctx_tiles) if split_ctx_tiles else resid[0][...]
        o = o_ref[...]
        cols = []
        for c in range(D // MXU_N):
            sl = slice(c * MXU_N, (c + 1) * MXU_N)
            if o_transposed:
                piece = lax.dot_general(o, wo_ref[:, sl], (((0,), (0,)), ((), ())), preferred_element_type=f32)
            else:
                piece = _dot(o, wo_ref[:, sl])
            cols.append(x[:, sl] + mod[:, 2 * D + c * MXU_N:2 * D + (c + 1) * MXU_N] * piece)
            advance(2)
        x1 = jnp.concatenate(cols, axis=1)
        x1_ref[...] = x1
        f = (_rms(x1) * g_ref[...]) * (1.0 + mod[:, 4 * D:5 * D]) + mod[:, 3 * D:4 * D]
        advance(2)
        fin_ref[...] = f.astype(bf16)
        tm = f.shape[0]
        for j in range(D // LANES):
            hlin_ref[pl.ds(j, tm, stride=D // LANES), :] = f[:, j * LANES:(j + 1) * LANES]
        f_refs[p][...] = f
        for _ in other:
            pass

    pl.when(i == 0)(functools.partial(mix, 0))
    for p in range(2):
        @pl.when(jnp.logical_and(jnp.logical_and(i >= 1, i < nt), i % 2 == p))
        def _(p=p):
            mix(p, route(1 - p))

    @pl.when(i == nt)
    def _():
        for _ in route((nt - 1) % 2):
            pass
        hlin_ref[...] = jnp.zeros_like(hlin_ref)


def _post_mix(o, w_o, resid, x_tile_off, modtab, n_ctx_tiles, gain, rwt, rb, o_transposed=False):
    if o_transposed:
        B, KO, N = o.shape
    else:
        B, N, KO = o.shape
    tm = ROW_TILE
    nt = N // tm
    last = lambda i: jnp.minimum(i, nt - 1)
    tok = lambda w: pl.BlockSpec((None, tm, w), lambda b, i: (b, last(i), 0))
    sel = lambda: pl.BlockSpec((None, TOP_K, tm), lambda b, i: (b, 0, jnp.maximum(i - 1, 0)))
    o_spec = (pl.BlockSpec((None, KO, tm), lambda b, i: (b, 0, last(i))) if o_transposed else tok(KO))
    split = len(resid) == 2
    resid_specs = (_stream_specs(tm, n_ctx_tiles, nt - 1) if split else
                   [pl.BlockSpec((None, tm, D), lambda b, i: (b, last(i) + x_tile_off, 0))])
    return pl.pallas_call(
        functools.partial(_post_mix_kernel, nt=nt, o_transposed=o_transposed,
                          split_ctx_tiles=n_ctx_tiles if split else 0),
        grid=(B, nt + 1),
        in_specs=[o_spec,
                  pl.BlockSpec((KO, D), lambda b, i: (0, 0))] + resid_specs + [
                  pl.BlockSpec((None, None, 1, 6 * D), lambda b, i: (b, jnp.where(last(i) < n_ctx_tiles, 0, 1), 0, 0)),
                  pl.BlockSpec((1, D), lambda b, i: (0, 0)),
                  pl.BlockSpec((N_EXP, D), lambda b, i: (0, 0)),
                  pl.BlockSpec((N_EXP, 1), lambda b, i: (0, 0))],
        out_specs=[tok(D), tok(D),
                   pl.BlockSpec((None, tm * (D // LANES), LANES), lambda b, i: (b, i, 0)),
                   sel(), sel(), sel(),
                   pl.BlockSpec((None, N_EXP, LANES), lambda b, i: (b, 0, 0))],
        out_shape=[jax.ShapeDtypeStruct((B, N, D), f32), jax.ShapeDtypeStruct((B, N, D), bf16),
                   jax.ShapeDtypeStruct((B, (N + tm) * (D // LANES), LANES), f32),
                   jax.ShapeDtypeStruct((B, TOP_K, N), i32), jax.ShapeDtypeStruct((B, TOP_K, N), f32),
                   jax.ShapeDtypeStruct((B, TOP_K, N), i32),
                   jax.ShapeDtypeStruct((B, N_EXP, LANES), f32)],
        scratch_shapes=[pltpu.VMEM((tm, D), f32), pltpu.VMEM((tm, D), f32)],
        compiler_params=_cparams(("arbitrary", "arbitrary")),
        name="post_mix",
    )(o, w_o, *resid, modtab, gain, rwt, rb)


META_W = 256
PLAN_ALIGN = 1024


def _round_up(n, m):
    return -(-n // m) * m


def _moe_sizes(n_tok):
    tm = MOE_TILE
    nt_max = (n_tok * TOP_K + N_EXP * (tm - 1)) // tm + 1
    ntp = _round_up(nt_max + MOE_GROUP, int(np.lcm(PLAN_ALIGN // tm, MOE_GROUP)))
    assert ntp <= META_W
    return ntp, _round_up(n_tok, PLAN_ALIGN)


def _plan_kernel(e_ref, r_ref, cnt_ref, pos_ref, meta_ref, *, n_tok):
    tm = MOE_TILE
    ntile = jnp.floor((cnt_ref[...] + (tm - 1.0)) * (1.0 / tm))
    ntb = ntile.astype(bf16)
    ei = lax.broadcasted_iota(i32, (N_EXP, LANES), 0)
    ej = lax.broadcasted_iota(i32, (N_EXP, LANES), 1)
    lower = jnp.where(ej <= ei, 1.0, 0.0)[:, :N_EXP].astype(bf16)
    tend = _dot(lower, ntb)
    tstart = tend - ntile
    tt = lax.broadcasted_iota(i32, (N_EXP, META_W), 1).astype(f32)
    te = jnp.sum(jnp.where(tt >= tend[:, 0:1], 1.0, 0.0), axis=0, keepdims=True)
    meta_ref[...] = jnp.zeros_like(meta_ref)
    meta_ref[0:1, :] = jnp.minimum(te, N_EXP - 1.0).astype(i32)
    meta_ref[1:2, :] = jnp.broadcast_to(tend[N_EXP - 1:N_EXP, 0:1], (1, META_W)).astype(i32)
    e = e_ref[...]
    base = jnp.zeros(e.shape, f32)
    for ex in range(N_EXP):
        base = jnp.where(e == ex, tstart[ex:ex + 1, 0:1] * float(tm), base)
    pos_ref[...] = jnp.zeros_like(pos_ref)
    pos_ref[:, 0:n_tok] = base.astype(i32) + r_ref[...]


def _plan(e_t, r_t, cnt):
    B, K, N = e_t.shape
    _, npad = _moe_sizes(N)
    return pl.pallas_call(
        functools.partial(_plan_kernel, n_tok=N),
        grid=(B,),
        in_specs=[pl.BlockSpec((None, K, N), lambda b: (b, 0, 0)),
                  pl.BlockSpec((None, K, N), lambda b: (b, 0, 0)),
                  pl.BlockSpec((None, N_EXP, LANES), lambda b: (b, 0, 0))],
        out_specs=[pl.BlockSpec((None, K, npad), lambda b: (b, 0, 0)),
                   pl.BlockSpec((None, SUBLANES, META_W), lambda b: (b, 0, 0))],
        out_shape=[jax.ShapeDtypeStruct((B, K, npad), i32), jax.ShapeDtypeStruct((B, SUBLANES, META_W), i32)],
        compiler_params=_cparams(("arbitrary",)),
        name="moe_plan",
    )(e_t, r_t, cnt)


def _plan_invert(pos, w_t, n_tok):
    B, K, npad = pos.shape
    ntp, _ = _moe_sizes(n_tok)
    plen = ntp * MOE_TILE
    nch = D // LANES
    mesh = plsc.VectorSubcoreMesh(core_axis_name="c", subcore_axis_name="s")
    n_cores = mesh.num_cores
    assert 2 * B <= n_cores * mesh.num_subcores and n_tok % SC_LANES == 0 and plen % SC_LANES == 0

    @functools.partial(
        pl.kernel, mesh=mesh,
        out_type=[jax.ShapeDtypeStruct((B * plen,), i32), jax.ShapeDtypeStruct((B * plen,), f32)],
        scratch_types=[pltpu.VMEM((npad,), i32), pltpu.VMEM((n_tok,), f32),
                       pltpu.VMEM((plen,), i32), pltpu.VMEM((plen,), f32)],
        compiler_params=dataclasses.replace(pltpu.CompilerParams(), needs_layout_passes=False))
    def invert(pos_hbm, w_hbm, rows_hbm, ws_hbm, pos_c, w_c, rows_v, ws_v):
        wid = lax.axis_index("s") * n_cores + lax.axis_index("c")
        b = wid % B
        lane = lax.iota(i32, SC_LANES)

        def load_pos(k):
            pltpu.sync_copy(pos_hbm.at[pl.ds(pl.multiple_of((b * K + k) * npad, SUBLANES), npad)], pos_c)

        @pl.when(wid < B)
        def _():
            pad = jnp.full((SC_LANES,), n_tok * nch, i32)

            @pl.loop(0, plen, step=SC_LANES)
            def _(i):
                rows_v[pl.ds(i, SC_LANES)] = pad

            for k in range(K):
                load_pos(k)

                @pl.loop(0, n_tok, step=SC_LANES)
                def _(n):
                    plsc.store_scatter(rows_v, [pos_c[pl.ds(n, SC_LANES)]], (lane + n) * nch)

            pltpu.sync_copy(rows_v, rows_hbm.at[pl.ds(pl.multiple_of(b * plen, SUBLANES), plen)])

        @pl.when(jnp.logical_and(wid >= B, wid < 2 * B))
        def _():
            zero = jnp.zeros((SC_LANES,), f32)

            @pl.loop(0, plen, step=SC_LANES)
            def _(i):
                ws_v[pl.ds(i, SC_LANES)] = zero

            for k in range(K):
                load_pos(k)
                pltpu.sync_copy(w_hbm.at[pl.ds(pl.multiple_of((b * K + k) * n_tok, SUBLANES), n_tok)], w_c)

                @pl.loop(0, n_tok, step=SC_LANES)
                def _(n):
                    plsc.store_scatter(ws_v, [pos_c[pl.ds(n, SC_LANES)]], w_c[pl.ds(n, SC_LANES)])

            pltpu.sync_copy(ws_v, ws_hbm.at[pl.ds(pl.multiple_of(b * plen, SUBLANES), plen)])

    return invert(pos.reshape(-1), w_t.reshape(-1))


def _moe_kernel(te_ref, nt_ref, *refs, ntp, n_tok):
    R = MOE_GROUP
    rg_ref, rs_ref, ws_ref, hlin_ref = refs[0:4]
    wgu_refs, wd_refs = refs[4:4 + R], refs[4 + R:4 + 2 * R]
    out_ref, acc_ref = refs[4 + 2 * R:6 + 2 * R]
    bufs = refs[6 + 2 * R:]
    xs_refs = [bufs[0:R], bufs[R:2 * R]]
    ylin_refs = [bufs[2 * R:3 * R], bufs[3 * R:4 * R]]
    b = pl.program_id(0)
    t = pl.program_id(1)
    TM = MOE_TILE
    NCH = D // LANES
    U = 8
    TMP = TM + SUBLANES
    ngrp = ntp // R

    @pl.when(t == 0)
    def _():
        acc_ref[...] = jnp.zeros_like(acc_ref)

    @pl.when(jnp.logical_and(b == 0, t == 0))
    def _():
        for buf in bufs:
            buf[...] = jnp.zeros_like(buf)

    def gather_rows(p, r, c):
        for m in range(c * U, (c + 1) * U):
            off = pl.multiple_of(rg_ref[r * TM + m], NCH)
            xs_refs[p][r][pl.ds(m, NCH, stride=TMP), :] = hlin_ref[pl.ds(off, NCH), :]

    def scatter_rows(p, r, c):
        offs = [pl.multiple_of(rs_ref[r * TM + c * U + u], NCH) for u in range(U)]
        news = [acc_ref[pl.ds(offs[u], NCH), :] + ylin_refs[p][r][pl.ds((c * U + u) * NCH, NCH), :]
                for u in range(U)]
        for u in range(U):
            acc_ref[pl.ds(offs[u], NCH), :] = news[u]

    def stage(p):
        row_work = [functools.partial(fn, p, r, c) for c in range(TM // U) for r in range(R)
                    for fn in (gather_rows, scatter_rows)]
        n_pieces = R * (EXP_FF // LANES + D // MXU_N)
        per_piece = -(-len(row_work) // n_pieces)

        def deal():
            for fn in row_work[:per_piece]:
                fn()
            del row_work[:per_piece]

        for r in range(R):
            x = jnp.concatenate([xs_refs[1 - p][r][pl.ds(j * TMP, TM), :] for j in range(NCH)], axis=1).astype(bf16)
            wcol = jnp.broadcast_to(ws_ref[r:r + 1, :], (SUBLANES, TM)).T[:, 0:1]
            gate = _dot(x, wgu_refs[r][:, :EXP_FF])
            deal()
            up = _dot(x, wgu_refs[r][:, EXP_FF:])
            deal()
            a = (_silu(gate) * up * wcol).astype(bf16)
            for c in range(D // MXU_N):
                y = _dot(a, wd_refs[r][:, c * MXU_N:(c + 1) * MXU_N])
                for jj in range(MXU_N // LANES):
                    j = c * (MXU_N // LANES) + jj
                    ylin_refs[1 - p][r][pl.ds(j, TM, stride=NCH), :] = y[:, jj * LANES:(jj + 1) * LANES]
                deal()
        while row_work:
            deal()

    live = (t - 2) * R < nt_ref[b]
    pl.when(jnp.logical_and(live, t % 2 == 0))(functools.partial(stage, 0))
    pl.when(jnp.logical_and(live, t % 2 == 1))(functools.partial(stage, 1))

    @pl.when(t >= ngrp + 2)
    def _():
        row0 = (t - (ngrp + 2)) * (ROW_TILE * NCH)
        for j in range(NCH):
            out_ref[:, j * LANES:(j + 1) * LANES] = acc_ref[pl.ds(row0 + j, ROW_TILE, stride=NCH), :].astype(bf16)


def _moe(te, nt, rows, wsort, hlin, w_gu, w_d, layer):
    B = hlin.shape[0]
    NCH = D // LANES
    n_tok = hlin.shape[1] // NCH - ROW_TILE
    assert n_tok % ROW_TILE == 0
    nf = n_tok // ROW_TILE
    TM = MOE_TILE
    R = MOE_GROUP
    ntp, _ = _moe_sizes(n_tok)
    assert ntp % R == 0
    ngrp = ntp // R

    def group_of(b, t, nt_ref, lag):
        grp = t - lag
        ok = jnp.logical_and(t >= lag, grp * R < nt_ref[b])
        return b * ngrp + jnp.where(ok, grp, ngrp - 1)

    def rows_spec(lag):
        return pl.BlockSpec((R * TM,), lambda b, t, te_ref, nt_ref: (group_of(b, t, nt_ref, lag),),
                            memory_space=pltpu.SMEM)

    ws_spec = pl.BlockSpec((None, R, TM), lambda b, t, te_ref, nt_ref: (group_of(b, t, nt_ref, 1), 0, 0))

    def w_spec(shape, r):
        def index(b, t, te_ref, nt_ref):
            tile = jnp.clip((t - 1) * R + r, 0, nt_ref[b] - 1)
            return (layer, te_ref[b * META_W + tile], 0, 0)
        return pl.BlockSpec((None, None) + shape, index)

    grid_spec = pltpu.PrefetchScalarGridSpec(
        num_scalar_prefetch=2,
        grid=(B, ngrp + 2 + nf),
        in_specs=([rows_spec(0), rows_spec(2), ws_spec,
                   pl.BlockSpec((None, (n_tok + ROW_TILE) * NCH, LANES), lambda b, t, *_: (b, 0, 0),
                                pipeline_mode=pl.Buffered(1))]
                  + [w_spec((D, 2 * EXP_FF), r) for r in range(R)]
                  + [w_spec((EXP_FF, D), r) for r in range(R)]),
        out_specs=pl.BlockSpec((None, ROW_TILE, D), lambda b, t, *_: (b, jnp.maximum(t - (ngrp + 2), 0), 0)),
        scratch_shapes=([pltpu.VMEM(((n_tok + SUBLANES) * NCH, LANES), f32)]
                        + [pltpu.VMEM(((TM + SUBLANES) * NCH, LANES), f32) for _ in range(2 * R)]
                        + [pltpu.VMEM((TM * NCH, LANES), f32) for _ in range(2 * R)]),
    )
    return pl.pallas_call(
        functools.partial(_moe_kernel, ntp=ntp, n_tok=n_tok),
        grid_spec=grid_spec,
        out_shape=jax.ShapeDtypeStruct((B, n_tok, D), bf16),
        compiler_params=_cparams(("arbitrary", "arbitrary"), MOE_VMEM_LIMIT),
        name="moe",
    )(te, nt, rows, rows, wsort.reshape(B * ngrp, R, TM), hlin, *([w_gu] * R), *([w_d] * R))


def _routed_experts(e_t, w_t, r_t, cnt, hlin, w_gu, w_d, layer):
    N = e_t.shape[2]
    pos, meta = _plan(e_t, r_t, cnt)
    te = meta[:, 0, :].reshape(-1)
    nt = meta[:, 1, 0]
    rows, wsort = _plan_invert(pos, w_t, N)
    return _moe(te, nt, rows, wsort, hlin, w_gu, w_d, layer)


def _shared_ffn(fin, shgu_ref, shd_ref):
    gu = _dot(fin, shgu_ref[...])
    return _dot((_silu(gu[:, :SH_FF]) * gu[:, SH_FF:]).astype(bf16), shd_ref[...])


def _post_ffn_final_kernel(x1_ref, routed_ref, fin_ref, shgu_ref, shd_ref, mod_ref, out_ref):
    out_ref[...] = x1_ref[...] + mod_ref[...][:, 5 * D:6 * D] * (
        routed_ref[...] + _shared_ffn(fin_ref[...], shgu_ref, shd_ref))


def _post_ffn_final(x1, routed, fin, sh_gu, sh_d, modtab):
    B, N, _ = x1.shape
    tm = ROW_TILE
    tok = lambda w: pl.BlockSpec((None, tm, w), lambda b, i: (b, i, 0))
    full = lambda r, c: pl.BlockSpec((r, c), lambda b, i: (0, 0))
    return pl.pallas_call(
        _post_ffn_final_kernel,
        grid=(B, N // tm),
        in_specs=[tok(D), tok(D), tok(D), full(D, 2 * SH_FF), full(SH_FF, D),
                  pl.BlockSpec((None, None, 1, 6 * D), lambda b, i: (b, 1, 0, 0))],
        out_specs=tok(D),
        out_shape=jax.ShapeDtypeStruct((B, N, D), f32),
        compiler_params=_cparams(("arbitrary", "arbitrary")),
        name="post_ffn_final",
    )(x1, routed, fin, sh_gu, sh_d, modtab)


def _mla_qkv(a, qan_ref, wqn_ref, wqr_ref, kvan_ref, wk_ref, wv_ref, qnn_ref, qnr_ref, knn_ref, knr_ref,
             cos_ref, sin_ref, q_ref, k_ref, v_ref):
    tm = a.shape[0]
    scale = MLA_QK ** -0.5 * float(np.log2(np.e))
    cos = cos_ref[...]
    sin = sin_ref[...]
    lane = lax.broadcasted_iota(i32, (tm, LANES), 1)
    first = (lane // (MLA_ROPE // 4)) % 2 == 0

    def rope(xb):
        sw = jnp.where(first, pltpu.roll(xb, LANES - MLA_ROPE // 4, axis=1), pltpu.roll(xb, MLA_ROPE // 4, axis=1))
        return xb * cos + sw * sin

    qa = (_rms(a[:, :MLA_QR]) * qan_ref[...]).astype(bf16)
    qn = _dot(qa, wqn_ref[...])
    yield
    qr = _dot(qa, wqr_ref[...])
    yield
    ri = lax.broadcasted_iota(i32, (LANES, LANES), 0) // MLA_ROPE
    ci = lax.broadcasted_iota(i32, (LANES, LANES), 1) // MLA_ROPE
    seg = jnp.where(ri == ci, 1.0, 0.0).astype(bf16)

    def seg_sum(sq):
        hi = sq.astype(bf16)
        r1 = sq - hi.astype(f32)
        mid = r1.astype(bf16)
        lo = (r1 - mid.astype(f32)).astype(bf16)
        return _dot(hi, seg) + _dot(mid, seg) + _dot(lo, seg)

    qr_blocks = []
    for p in range(MLA_H // 2):
        blk = qr[:, p * LANES:(p + 1) * LANES]
        blk = blk * lax.rsqrt(seg_sum(blk * blk) * (1.0 / MLA_ROPE) + EPS) * qnr_ref[:, p * LANES:(p + 1) * LANES]
        qr_blocks.append(rope(blk) * scale)
        yield

    kv = (_rms(a[:, MLA_QR:MLA_QR + MLA_KVR]) * kvan_ref[...]).astype(bf16)
    kn = _dot(kv, wk_ref[...])
    yield
    v_ref[...] = _dot_nt(wv_ref[...], kv).astype(bf16)
    yield
    kr = a[:, MLA_QR + MLA_KVR:MLA_IN_PAD]
    kr = rope(_rms(kr, MLA_ROPE) * knr_ref[...])
    kr_odd = pltpu.roll(kr, MLA_ROPE, axis=1)
    for hd in range(MLA_H):
        sl = slice(hd * MLA_NOPE, (hd + 1) * MLA_NOPE)
        q_ref[:, 2 * hd * LANES:(2 * hd + 1) * LANES] = (_rms(qn[:, sl]) * qnn_ref[...] * scale).astype(bf16)
        q_ref[:, (2 * hd + 1) * LANES:(2 * hd + 2) * LANES] = qr_blocks[hd // 2].astype(bf16)
        k_ref[:, 2 * hd * LANES:(2 * hd + 1) * LANES] = (_rms(kn[:, sl]) * knn_ref[...]).astype(bf16)
        k_ref[:, (2 * hd + 1) * LANES:(2 * hd + 2) * LANES] = (kr if hd % 2 == 0 else kr_odd).astype(bf16)
        if hd % 2 == 1:
            yield


def _ffn_mla_kernel(x1_ref, routed_ref, fin_ref, shgu_ref, shd_ref, mod0_ref, mod1_ref, g_ref, win_ref, *refs, nt):
    qkv_refs, (x2_ref, q_ref, k_ref, v_ref, a0_ref, a1_ref) = refs[:12], refs[12:]
    a_refs = (a0_ref, a1_ref)
    i = pl.program_id(1)

    def qkv(p):
        return _mla_qkv(a_refs[p][...], *qkv_refs, q_ref, k_ref, v_ref)

    def combine(p, other=()):
        other = iter(other)

        def advance(n):
            for _ in range(n):
                next(other, None)

        gu = _dot(fin_ref[...], shgu_ref[...])
        advance(3)
        shared = _dot((_silu(gu[:, :SH_FF]) * gu[:, SH_FF:]).astype(bf16), shd_ref[...])
        advance(3)
        x2 = x1_ref[...] + mod0_ref[...][:, 5 * D:6 * D] * (routed_ref[...] + shared)
        x2_ref[...] = x2
        advance(2)
        mod1 = mod1_ref[...]
        h = (_rms(x2) * g_ref[...]) * (1.0 + mod1[:, D:2 * D]) + mod1[:, 0:D]
        advance(2)
        a_refs[p][...] = _dot(h.astype(bf16), win_ref[...])
        for _ in other:
            pass

    pl.when(i == 0)(functools.partial(combine, 0))
    for p in range(2):
        @pl.when(jnp.logical_and(jnp.logical_and(i >= 1, i < nt), i % 2 == p))
        def _(p=p):
            combine(p, qkv(1 - p))

    @pl.when(i == nt)
    def _():
        for _ in qkv((nt - 1) % 2):
            pass


def _ffn_mla(x1, routed, fin, sh_gu, sh_d, modtab0, modtab1, gain, w_in, qkv_params, cos_t, sin_t, n_ctx_tiles):
    B, T, _ = x1.shape
    tm = ROW_TILE
    nt = T // tm
    last = lambda i: jnp.minimum(i, nt - 1)
    prev = lambda i: jnp.maximum(i - 1, 0)
    tok = lambda w: pl.BlockSpec((None, tm, w), lambda b, i: (b, last(i), 0))
    modspec = lambda: pl.BlockSpec((None, None, 1, 6 * D),
                                   lambda b, i: (b, jnp.where(last(i) < n_ctx_tiles, 0, 1), 0, 0))
    full = lambda r, c: pl.BlockSpec((r, c), lambda b, i: (0, 0))
    rope_spec = lambda: pl.BlockSpec((tm, LANES), lambda b, i: (prev(i), 0))
    hw = 2 * LANES * MLA_H
    return pl.pallas_call(
        functools.partial(_ffn_mla_kernel, nt=nt),
        grid=(B, nt + 1),
        in_specs=[tok(D), tok(D), tok(D), full(D, 2 * SH_FF), full(SH_FF, D), modspec(), modspec(),
                  full(1, D), full(D, MLA_IN_PAD),
                  full(1, MLA_QR), full(MLA_QR, MLA_H * MLA_NOPE), full(MLA_QR, MLA_H * MLA_ROPE),
                  full(1, MLA_KVR), full(MLA_KVR, MLA_H * MLA_NOPE), full(MLA_H * MLA_V, MLA_KVR),
                  full(1, MLA_NOPE), full(1, MLA_H * MLA_ROPE), full(1, MLA_NOPE), full(1, LANES),
                  rope_spec(), rope_spec()],
        out_specs=[tok(D),
                   pl.BlockSpec((None, tm, hw), lambda b, i: (b, jnp.maximum(prev(i) - n_ctx_tiles, 0), 0)),
                   pl.BlockSpec((None, tm, hw), lambda b, i: (b, prev(i), 0)),
                   pl.BlockSpec((None, MLA_H * MLA_V, tm), lambda b, i: (b, 0, prev(i)))],
        out_shape=[jax.ShapeDtypeStruct((B, T, D), f32),
                   jax.ShapeDtypeStruct((B, T - n_ctx_tiles * tm, hw), bf16), jax.ShapeDtypeStruct((B, T, hw), bf16),
                   jax.ShapeDtypeStruct((B, MLA_H * MLA_V, T), bf16)],
        scratch_shapes=[pltpu.VMEM((tm, MLA_IN_PAD), f32), pltpu.VMEM((tm, MLA_IN_PAD), f32)],
        compiler_params=_cparams(("arbitrary", "arbitrary")),
        name="ffn_mla",
    )(x1, routed, fin, sh_gu, sh_d, modtab0, modtab1, gain, w_in, *qkv_params, cos_t, sin_t)


ATT_KV_CHUNK = 256
ATT_TQ = 256
ATT_SUB = 1


def _mla_attn_kernel(q_ref, k_ref, vt_ref, o_ref, *scratch, n_sub):
    tq = q_ref.shape[0] // n_sub
    ck = ATT_KV_CHUNK
    nck = k_ref.shape[0] // ck
    i = pl.program_id(0)

    s_refs = [scratch[0:n_sub], scratch[n_sub:2 * n_sub]]
    m_refs = [scratch[2 * n_sub:3 * n_sub], scratch[3 * n_sub:4 * n_sub]]

    @pl.when(i == 0)
    def _():
        for ref in scratch:
            ref[...] = jnp.zeros_like(ref)

    def fold(a, op):
        return op(a.reshape(ck // SUBLANES, SUBLANES, tq), axis=0)

    def stage(par):
        for u in range(n_sub):
            q = q_ref[u * tq:(u + 1) * tq, :]
            s = _dot_nt(k_ref[...], q)
            s_refs[par][u][...] = s
            m_refs[par][u][...] = jnp.max(s.reshape(nck * ck // SUBLANES, SUBLANES, tq), axis=0)
        for u in range(n_sub):
            m = jnp.max(m_refs[1 - par][u][...], axis=0, keepdims=True)
            p = jnp.exp2(s_refs[1 - par][u][...] - m)
            lsum = jnp.sum(jnp.sum(p.reshape(nck * ck // SUBLANES, SUBLANES, tq), axis=0), axis=0, keepdims=True)
            acc = _dot(vt_ref[...], p.astype(bf16))
            o_ref[:, u * tq:(u + 1) * tq] = (acc / lsum).astype(bf16)

    pl.when(i % 2 == 0)(functools.partial(stage, 0))
    pl.when(i % 2 == 1)(functools.partial(stage, 1))


def _mla_attn(q, k, vt):
    B, S, _ = q.shape
    T = k.shape[1]
    n_sub = ATT_SUB if S % (ATT_SUB * ATT_TQ) == 0 else 1
    tq = n_sub * ATT_TQ
    nq = S // tq
    ntile = B * MLA_H * nq
    assert T % ATT_KV_CHUNK == 0

    def tile(g):
        g = jnp.clip(g, 0, ntile - 1)
        return g // (MLA_H * nq), (g // nq) % MLA_H, g % nq

    def q_index(g):
        b, h, i = tile(g)
        return b, i, h

    def k_index(g):
        b, h, _ = tile(g)
        return b, 0, h

    def vt_index(g):
        b, h, _ = tile(g - 1)
        return b, h, 0

    def o_index(g):
        b, h, i = tile(g - 1)
        return b, h, i

    return pl.pallas_call(
        functools.partial(_mla_attn_kernel, n_sub=n_sub),
        grid=(ntile + 1,),
        in_specs=[pl.BlockSpec((None, tq, 2 * LANES), q_index),
                  pl.BlockSpec((None, T, 2 * LANES), k_index),
                  pl.BlockSpec((None, MLA_V, T), vt_index)],
        out_specs=pl.BlockSpec((None, MLA_V, tq), o_index),
        out_shape=jax.ShapeDtypeStruct((B, MLA_H * MLA_V, S), bf16),
        scratch_shapes=([pltpu.VMEM((T, ATT_TQ), f32) for _ in range(2 * n_sub)]
                        + [pltpu.VMEM((SUBLANES, ATT_TQ), f32) for _ in range(2 * n_sub)]),
        compiler_params=_cparams(("arbitrary",)),
        name="mla_attn",
    )(q, k, vt)


def _axial_angles(rows_n, rot_dim):
    axis_dim = rot_dim // 2
    inv = ROPE_BASE ** (-jnp.arange(0, axis_dim, 2, dtype=f32) / axis_dim)
    row = jnp.repeat(jnp.arange(rows_n, dtype=f32), GRID_W)
    col = jnp.tile(jnp.arange(GRID_W, dtype=f32), rows_n)
    return row[:, None] * inv, col[:, None] * inv


def _rope_tables(seq, n_ctx, rot_dim, reps):
    ang_r, ang_c = _axial_angles(seq // GRID_W, rot_dim)
    cos = jnp.concatenate([jnp.cos(ang_r)] * 2 + [jnp.cos(ang_c)] * 2, axis=1)
    sin = jnp.concatenate([-jnp.sin(ang_r), jnp.sin(ang_r), -jnp.sin(ang_c), jnp.sin(ang_c)], axis=1)
    cos = jnp.concatenate([jnp.ones((n_ctx, rot_dim), f32), cos], axis=0)
    sin = jnp.concatenate([jnp.zeros((n_ctx, rot_dim), f32), sin], axis=0)
    return jnp.tile(cos, (1, reps)), jnp.tile(sin, (1, reps))


def kernel(x, c, ctx, c_ctx, ada_w, ada_b, norm_mix, norm_ffn, ret_w_in, ret_decay_f, ret_decay_b, ret_w_o,
           mla_w_in, mla_q_a_norm, mla_w_q_b, mla_kv_a_norm, mla_w_kv_b, mla_q_norm, mla_k_norm, mla_w_o,
           router_w, router_bias, exp_w_gu, exp_w_down, sh_w_gu, sh_w_down):
    B, S, _ = x.shape
    n_ctx = ctx.shape[1]
    assert n_ctx % ROW_TILE == 0 and S % ROW_TILE == 0 and S % GRID_W == 0
    n_ctx_tiles = n_ctx // ROW_TILE

    rows = -(-(B + 1) // SUBLANES) * SUBLANES
    cc = jnp.zeros((rows, D), f32).at[:B].set(c).at[B].set(c_ctx)
    mod = _ada(cc, ada_w, ada_b)

    def modtab(i):
        ctx_row = jnp.broadcast_to(mod[i, B][None, :], (B, 6 * D))
        return jnp.stack([ctx_row, mod[i, :B]], axis=1)[:, :, None, :]

    mod0, mod1 = modtab(0), modtab(1)

    cos_r, sin_r = _rope_tables(S, n_ctx, RET_DK, 1)
    q, k, v, gf, gb = _ret_inproj(ctx, x, mod0, norm_mix[0][None, :], ret_w_in[0].astype(bf16), cos_r, sin_r,
                                  n_ctx_tiles)
    dtab = jnp.broadcast_to(jnp.concatenate([ret_decay_f[0], ret_decay_b[0]])[:, None], (2 * RET_H, LANES))
    o = _ret_scan(dtab, q, k, v, gf, gb, n_ctx)
    x1, fin, hlin, e_t, w_t, r_t, cnt = _post_mix(
        o, ret_w_o[0].astype(bf16), (ctx, x), 0, mod0, n_ctx_tiles, norm_ffn[0][None, :],
        router_w[0].T, router_bias[0][:, None])
    exp_gu, exp_d = exp_w_gu.astype(bf16), exp_w_down.astype(bf16)
    routed = _routed_experts(e_t, w_t, r_t, cnt, hlin, exp_gu, exp_d, 0)
    w_in1 = jnp.zeros((D, MLA_IN_PAD), f32).at[:, :mla_w_in.shape[2]].set(mla_w_in[0]).astype(bf16)

    wq = mla_w_q_b[0].reshape(MLA_QR, MLA_H, MLA_QK)
    wqn = wq[:, :, :MLA_NOPE].reshape(MLA_QR, MLA_H * MLA_NOPE).astype(bf16)
    wqr = wq[:, :, MLA_NOPE:].reshape(MLA_QR, MLA_H * MLA_ROPE).astype(bf16)
    wkv = mla_w_kv_b[0].reshape(MLA_KVR, MLA_H, MLA_NOPE + MLA_V)
    wk = wkv[:, :, :MLA_NOPE].reshape(MLA_KVR, MLA_H * MLA_NOPE).astype(bf16)
    wv = wkv[:, :, MLA_NOPE:].reshape(MLA_KVR, MLA_H * MLA_V).T.astype(bf16)
    qnn = mla_q_norm[0][None, :MLA_NOPE]
    qnr = jnp.tile(mla_q_norm[0][None, MLA_NOPE:], (1, MLA_H))
    knn = mla_k_norm[0][None, :MLA_NOPE]
    knr = jnp.concatenate([mla_k_norm[0][MLA_NOPE:], jnp.zeros((LANES - MLA_ROPE,), f32)])[None, :]
    cos_m, sin_m = _rope_tables(S, n_ctx, MLA_ROPE, LANES // MLA_ROPE)
    qkv_params = (mla_q_a_norm[0][None, :], wqn, wqr, mla_kv_a_norm[0][None, :], wk, wv, qnn, qnr, knn, knr)
    x2, qf, kf, vf = _ffn_mla(x1, routed, fin, sh_w_gu[0].astype(bf16), sh_w_down[0].astype(bf16), mod0, mod1,
                              norm_mix[1][None, :], w_in1, qkv_params, cos_m, sin_m, n_ctx_tiles)
    o1 = _mla_attn(qf, kf, vf)
    x3, fin1, hlin1, e1, w1, r1, cnt1 = _post_mix(
        o1, mla_w_o[0].astype(bf16), (x2,), n_ctx_tiles, mod1, 0, norm_ffn[1][None, :],
        router_w[1].T, router_bias[1][:, None], o_transposed=True)
    routed1 = _routed_experts(e1, w1, r1, cnt1, hlin1, exp_gu, exp_d, 1)
    return _post_ffn_final(x3, routed1, fin1, sh_w_gu[1].astype(bf16), sh_w_down[1].astype(bf16), mod1)
```

```python
import dataclasses
import functools

import jax
import jax.numpy as jnp
import numpy as np
from jax import lax
from jax.experimental import pallas as pl
from jax.experimental.pallas import tpu as pltpu
from jax.experimental.pallas import tpu_sc as plsc

f32 = jnp.float32
bf16 = jnp.bfloat16
i32 = jnp.int32

D = 1024
GRID_W = 64
EPS = 1e-6
ROPE_BASE = 10000.0
RET_H = 4
RET_DK = 256
RET_DV = 512
RET_VW = RET_H * RET_DV
RET_CHUNK = 256
MLA_H = 8
MLA_NOPE = 128
MLA_ROPE = 64
MLA_QK = MLA_NOPE + MLA_ROPE
MLA_V = 128
MLA_QR = 384
MLA_KVR = 256
MLA_IN_PAD = 768
N_EXP = 64
TOP_K = 8
N_GRP = 8
TOPK_GRP = 4
EXP_FF = 256
SH_FF = 256
ROUTED_SCALE = 2.5

LANES = 128
SUBLANES = 8
SC_LANES = 16
MXU_N = 256
ROW_TILE = 256
MOE_TILE = 256
MOE_GROUP = 2
ADA_COLS = 1536
PROJ_COLS = 512
VMEM_LIMIT = 56 * 1024 * 1024
MOE_VMEM_LIMIT = 62 * 1024 * 1024


def _cparams(sem, vmem=VMEM_LIMIT):
    return pltpu.CompilerParams(dimension_semantics=sem, vmem_limit_bytes=vmem)


def _sigmoid(x):
    return 1.0 / (1.0 + jnp.exp(-x))


def _silu(x):
    return x * _sigmoid(x)


def _rms(x, n=None):
    n = x.shape[-1] if n is None else n
    return x * lax.rsqrt(jnp.sum(x * x, axis=-1, keepdims=True) * (1.0 / n) + EPS)


def _dot(a, b):
    return jnp.dot(a, b, preferred_element_type=f32)


def _dot_nt(a, b):
    return lax.dot_general(a, b, (((1,), (1,)), ((), ())), preferred_element_type=f32)


def _ada_kernel(c_ref, w_ref, b_ref, o_ref):
    s = _silu(c_ref[...]).astype(bf16)
    o_ref[...] = _dot(s, w_ref[...].astype(bf16)) + b_ref[...]


def _ada(cc, ada_w, ada_b):
    depth = ada_w.shape[0]
    rows = cc.shape[0]
    tn = ADA_COLS
    return pl.pallas_call(
        _ada_kernel,
        grid=(depth, 6 * D // tn),
        in_specs=[pl.BlockSpec((rows, D), lambda i, j: (0, 0)),
                  pl.BlockSpec((None, D, tn), lambda i, j: (i, 0, j)),
                  pl.BlockSpec((None, 1, tn), lambda i, j: (i, 0, j))],
        out_specs=pl.BlockSpec((None, rows, tn), lambda i, j: (i, 0, j)),
        out_shape=jax.ShapeDtypeStruct((depth, rows, 6 * D), f32),
        compiler_params=_cparams(("arbitrary", "arbitrary")),
        name="ada",
    )(cc, ada_w, ada_b.reshape(depth, 1, 6 * D))


def _stream_tile(ctx_ref, x_ref, n_ctx_tiles):
    return jnp.where(pl.program_id(1) < n_ctx_tiles, ctx_ref[...], x_ref[...])


def _stream_specs(tm, n_ctx_tiles, last=None):
    clamp = (lambda i: i) if last is None else (lambda i: jnp.minimum(i, last))
    return [pl.BlockSpec((None, tm, D), lambda b, i: (b, jnp.minimum(clamp(i), n_ctx_tiles - 1), 0)),
            pl.BlockSpec((None, tm, D), lambda b, i: (b, jnp.maximum(clamp(i) - n_ctx_tiles, 0), 0))]


def _ret_inproj_kernel(ctx_ref, x_ref, mod_ref, g_ref, w_ref, cos_ref, sin_ref, q_ref, k_ref, v_ref, gf_ref, gb_ref,
                       *, n_ctx_tiles):
    x = _stream_tile(ctx_ref, x_ref, n_ctx_tiles)
    mod = mod_ref[...]
    h = (_rms(x) * g_ref[...]) * (1.0 + mod[:, D:2 * D]) + mod[:, 0:D]
    hb = h.astype(bf16)
    cos = cos_ref[...]
    sin = sin_ref[...]

    def rope(a):
        outs = []
        for half in range(2):
            sl = slice(half * LANES, (half + 1) * LANES)
            ah = a[:, sl]
            outs.append(ah * cos[:, sl] + pltpu.roll(ah, LANES // 2, axis=1) * sin[:, sl])
        return jnp.concatenate(outs, axis=1)

    for hd in range(RET_H):
        sl = slice(hd * RET_DK, (hd + 1) * RET_DK)
        q_ref[:, sl] = rope(_dot(hb, w_ref[:, sl])).astype(bf16)
    for hd in range(RET_H):
        sl = slice(hd * RET_DK, (hd + 1) * RET_DK)
        wsl = slice(D + hd * RET_DK, D + (hd + 1) * RET_DK)
        k_ref[:, sl] = (rope(_dot(hb, w_ref[:, wsl])) * (RET_DK ** -0.5)).astype(bf16)
    cw = PROJ_COLS
    for c in range(RET_VW // cw):
        sl = slice(c * cw, (c + 1) * cw)
        v_ref[:, sl] = _dot(hb, w_ref[:, 2 * D + c * cw:2 * D + (c + 1) * cw]).astype(bf16)
        gf_ref[:, sl] = _silu(_dot(hb, w_ref[:, 2 * D + RET_VW + c * cw:2 * D + RET_VW + (c + 1) * cw])).astype(bf16)
        gb_ref[:, sl] = _silu(_dot(hb, w_ref[:, 2 * D + 2 * RET_VW + c * cw:2 * D + 2 * RET_VW + (c + 1) * cw])).astype(bf16)


def _ret_inproj(ctx, x, modtab, gain, w_in, cos_t, sin_t, n_ctx_tiles):
    B = x.shape[0]
    T = ctx.shape[1] + x.shape[1]
    tm = ROW_TILE
    n_in = w_in.shape[1]
    tok = lambda w: pl.BlockSpec((None, tm, w), lambda b, i: (b, i, 0))
    return pl.pallas_call(
        functools.partial(_ret_inproj_kernel, n_ctx_tiles=n_ctx_tiles),
        grid=(B, T // tm),
        in_specs=_stream_specs(tm, n_ctx_tiles) + [
                  pl.BlockSpec((None, None, 1, 6 * D), lambda b, i: (b, jnp.where(i < n_ctx_tiles, 0, 1), 0, 0)),
                  pl.BlockSpec((1, D), lambda b, i: (0, 0)),
                  pl.BlockSpec((D, n_in), lambda b, i: (0, 0), pipeline_mode=pl.Buffered(1)),
                  pl.BlockSpec((tm, RET_DK), lambda b, i: (i, 0)),
                  pl.BlockSpec((tm, RET_DK), lambda b, i: (i, 0))],
        out_specs=[tok(D), tok(D), tok(RET_VW), tok(RET_VW), tok(RET_VW)],
        out_shape=[jax.ShapeDtypeStruct((B, T, D), bf16), jax.ShapeDtypeStruct((B, T, D), bf16),
                   jax.ShapeDtypeStruct((B, T, RET_VW), bf16), jax.ShapeDtypeStruct((B, T, RET_VW), bf16),
                   jax.ShapeDtypeStruct((B, T, RET_VW), bf16)],
        compiler_params=_cparams(("arbitrary", "arbitrary")),
        name="ret_inproj",
    )(ctx, x, modtab, gain, w_in, cos_t, sin_t)


def _ret_chunk_index(t, nc, ncc):
    u = t - nc
    back = jnp.where(u < ncc, ncc - 1 - u, nc - 1 - u + ncc)
    return jnp.where(t < nc, t, back)


def _ret_scan_kernel(dt_ref, q_ref, k_ref, v_ref, gf_ref, gb_ref, o_ref,
                     s_ref, of_ref, mask_ref, dq_ref, dk_ref, dc_ref, *, nc, ncc):
    t = pl.program_id(1)
    C = RET_CHUNK

    def init(direction):
        s_ref[...] = jnp.zeros_like(s_ref)
        ii = lax.broadcasted_iota(i32, (C, C), 0)
        jj = lax.broadcasted_iota(i32, (C, C), 1)
        rel = (ii - jj if direction == 0 else jj - ii).astype(f32)
        pos = lax.broadcasted_iota(i32, (C, 1), 0).astype(f32)
        for hd in range(RET_H):
            r = direction * RET_H + hd
            lg = -jnp.exp(dt_ref[r:r + 1, :])
            lg1 = lg[:, 0:1]
            mask_ref[hd] = jnp.where(rel >= 0, jnp.exp(lg1 * jnp.maximum(rel, 0.0)), 0.0)
            if direction == 0:
                dq_ref[hd] = jnp.exp(lg1 * (pos + 1.0))
                dk_ref[hd] = jnp.exp(lg1 * (C - 1.0 - pos))
            else:
                dq_ref[hd] = jnp.exp(lg1 * (C - pos))
                dk_ref[hd] = jnp.exp(lg1 * pos)
            dc_ref[hd] = jnp.exp(lg * float(C))

    pl.when(t == 0)(functools.partial(init, 0))
    pl.when(t == nc)(functools.partial(init, 1))

    row0 = pl.multiple_of(_ret_chunk_index(t, nc, ncc) * C, C)

    def step(forward):
        for hd in range(RET_H):
            ks = slice(hd * RET_DK, (hd + 1) * RET_DK)
            vs = slice(hd * RET_DV, (hd + 1) * RET_DV)
            qh = q_ref[:, ks]
            kh = k_ref[:, ks]
            vh = v_ref[:, vs]
            p = (_dot_nt(qh, kh) * mask_ref[hd]).astype(bf16)
            y = _dot(p, vh) + _dot(qh, s_ref[hd].astype(bf16)) * dq_ref[hd]
            kd = (kh.astype(f32) * dk_ref[hd]).astype(bf16)
            upd = lax.dot_general(kd, vh, (((0,), (0,)), ((), ())), preferred_element_type=f32)
            s_ref[hd] = s_ref[hd] * dc_ref[hd][0:1, 0:1] + upd
            yn = _rms(y)
            if forward:
                of_ref[pl.ds(row0, C), vs] = (gf_ref[:, vs].astype(f32) * yn).astype(bf16)
            else:
                o_ref[:, vs] = (of_ref[pl.ds(row0, C), vs].astype(f32) + gb_ref[:, vs].astype(f32) * yn).astype(bf16)

    pl.when(t < nc)(functools.partial(step, True))
    pl.when(t >= nc)(functools.partial(step, False))


def _ret_scan(dtab, q, k, v, gf, gb, n_ctx):
    B, T, _ = q.shape
    C = RET_CHUNK
    nc = T // C
    ncc = n_ctx // C
    cidx = functools.partial(_ret_chunk_index, nc=nc, ncc=ncc)
    first_back = ncc - 1
    return pl.pallas_call(
        functools.partial(_ret_scan_kernel, nc=nc, ncc=ncc),
        grid=(B, 2 * nc),
        in_specs=[pl.BlockSpec((2 * RET_H, LANES), lambda b, t: (0, 0)),
                  pl.BlockSpec((None, C, D), lambda b, t: (b, cidx(t), 0)),
                  pl.BlockSpec((None, C, D), lambda b, t: (b, cidx(t), 0)),
                  pl.BlockSpec((None, C, RET_VW), lambda b, t: (b, cidx(t), 0)),
                  pl.BlockSpec((None, C, RET_VW), lambda b, t: (b, jnp.where(t < nc, t, nc - 1), 0)),
                  pl.BlockSpec((None, C, RET_VW), lambda b, t: (b, jnp.where(t < nc, first_back, cidx(t)), 0))],
        out_specs=pl.BlockSpec((None, C, RET_VW), lambda b, t: (b, jnp.where(t < nc, first_back, cidx(t)), 0)),
        out_shape=jax.ShapeDtypeStruct((B, T, RET_VW), bf16),
        scratch_shapes=[pltpu.VMEM((RET_H, RET_DK, RET_DV), f32),
                        pltpu.VMEM((T, RET_VW), bf16),
                        pltpu.VMEM((RET_H, C, C), f32),
                        pltpu.VMEM((RET_H, C, 1), f32),
                        pltpu.VMEM((RET_H, C, 1), f32),
                        pltpu.VMEM((RET_H, 1, LANES), f32)],
        compiler_params=_cparams(("arbitrary", "arbitrary")),
        name="ret_scan",
    )(dtab, q, k, v, gf, gb)


def _route(f, rwt_ref, rb_ref, cnt_ref, e_ref, w_ref, r_ref):
    tm = f.shape[0]
    G = N_EXP // N_GRP
    logits = _dot_nt(rwt_ref[...].astype(bf16), f.astype(bf16))
    s = _sigmoid(logits)
    sel = s + rb_ref[...]
    mi = lax.broadcasted_iota(i32, (G, tm), 0)
    neg = -jnp.inf
    s_g = [s[g * G:(g + 1) * G, :] for g in range(N_GRP)]
    sel_g = [sel[g * G:(g + 1) * G, :] for g in range(N_GRP)]

    def first_max(a, ids, big):
        mx = jnp.max(a, axis=0, keepdims=True)
        ix = jnp.min(jnp.where(a == mx, ids, big), axis=0, keepdims=True)
        return mx, ix

    gscore = jnp.zeros((N_GRP, tm), f32)
    gi = lax.broadcasted_iota(i32, (N_GRP, tm), 0)
    for g in range(N_GRP):
        t1, i1 = first_max(sel_g[g], mi, G)
        t2 = jnp.max(jnp.where(mi == i1, neg, sel_g[g]), axis=0, keepdims=True)
        gscore = jnp.where(gi == g, t1 + t2, gscore)
    yield
    gmask = jnp.zeros((N_GRP, tm), i32)
    cur = gscore
    for _ in range(TOPK_GRP):
        _, ix = first_max(cur, gi, N_GRP)
        hit = gi == ix
        gmask = jnp.where(hit, 1, gmask)
        cur = jnp.where(hit, neg, cur)
    cand = [jnp.where(gmask[g:g + 1, :] > 0, sel_g[g], neg) for g in range(N_GRP)]
    ids = [mi + g * G for g in range(N_GRP)]

    def across(parts, op):
        acc = parts[0]
        for part in parts[1:]:
            acc = op(acc, part)
        return acc

    e_rows, w_rows = [], []
    for _ in range(TOP_K):
        mx = jnp.max(across(cand, jnp.maximum), axis=0, keepdims=True)
        ix = jnp.min(across([jnp.where(cand[g] == mx, ids[g], N_EXP) for g in range(N_GRP)], jnp.minimum),
                     axis=0, keepdims=True)
        hits = [ids[g] == ix for g in range(N_GRP)]
        cand = [jnp.where(hits[g], neg, cand[g]) for g in range(N_GRP)]
        wv = jnp.sum(across([jnp.where(hits[g], s_g[g], 0.0) for g in range(N_GRP)], jnp.add), axis=0, keepdims=True)
        e_rows.append(ix)
        w_rows.append(wv)
        yield
    wsum = w_rows[0]
    for r in range(1, TOP_K):
        wsum = wsum + w_rows[r]

    selm = [jnp.zeros((G, tm), f32) for _ in range(N_GRP)]
    for r in range(TOP_K):
        for g in range(N_GRP):
            selm[g] = jnp.where(ids[g] == e_rows[r], 1.0, selm[g])
    m_all = jnp.concatenate(selm, axis=0)
    ri = lax.broadcasted_iota(i32, (tm, tm), 0)
    ci = lax.broadcasted_iota(i32, (tm, tm), 1)
    upper = jnp.where(ri <= ci, 1.0, 0.0).astype(bf16)
    incl = _dot(m_all.astype(bf16), upper)
    carry = cnt_ref[:, 0:1]
    rank_all = carry + incl - m_all
    cnt_ref[...] = cnt_ref[...] + incl[:, tm - 1:tm]
    yield
    for r in range(TOP_K):
        rk = jnp.sum(across([jnp.where(ids[g] == e_rows[r], rank_all[g * G:(g + 1) * G, :], 0.0)
                             for g in range(N_GRP)], jnp.add), axis=0, keepdims=True)
        e_ref[r:r + 1, :] = e_rows[r]
        w_ref[r:r + 1, :] = w_rows[r] / wsum * ROUTED_SCALE
        r_ref[r:r + 1, :] = rk.astype(i32)


def _post_mix_kernel(o_ref, wo_ref, *refs, nt, o_transposed, split_ctx_tiles):
    n_resid = 2 if split_ctx_tiles else 1
    resid = refs[:n_resid]
    (mod_ref, g_ref, rwt_ref, rb_ref, x1_ref, fin_ref, hlin_ref, e_ref, w_ref, r_ref, cnt_ref,
     f0_ref, f1_ref) = refs[n_resid:]
    f_refs = (f0_ref, f1_ref)
    i = pl.program_id(1)

    @pl.when(i == 0)
    def _():
        cnt_ref[...] = jnp.zeros_like(cnt_ref)

    def route(p):
        return _route(f_refs[p][...], rwt_ref, rb_ref, cnt_ref, e_ref, w_ref, r_ref)

    def mix(p, other=()):
        other = iter(other)

        def advance(n):
            for _ in range(n):
                next(other, None)

        mod = mod_ref[...]
        x = _stream_tile(resid[0], resid[1], split_ctx_tiles) if split_ctx_tiles else resid[0][...]
        o = o_ref[...]
        cols = []
        for c in range(D // MXU_N):
            sl = slice(c * MXU_N, (c + 1) * MXU_N)
            if o_transposed:
                piece = lax.dot_general(o, wo_ref[:, sl], (((0,), (0,)), ((), ())), preferred_element_type=f32)
            else:
                piece = _dot(o, wo_ref[:, sl])
            cols.append(x[:, sl] + mod[:, 2 * D + c * MXU_N:2 * D + (c + 1) * MXU_N] * piece)
            advance(2)
        x1 = jnp.concatenate(cols, axis=1)
        x1_ref[...] = x1
        f = (_rms(x1) * g_ref[...]) * (1.0 + mod[:, 4 * D:5 * D]) + mod[:, 3 * D:4 * D]
        advance(2)
        fin_ref[...] = f.astype(bf16)
        tm = f.shape[0]
        for j in range(D // LANES):
            hlin_ref[pl.ds(j, tm, stride=D // LANES), :] = f[:, j * LANES:(j + 1) * LANES]
        f_refs[p][...] = f
        for _ in other:
            pass

    pl.when(i == 0)(functools.partial(mix, 0))
    for p in range(2):
        @pl.when(jnp.logical_and(jnp.logical_and(i >= 1, i < nt), i % 2 == p))
        def _(p=p):
            mix(p, route(1 - p))

    @pl.when(i == nt)
    def _():
        for _ in route((nt - 1) % 2):
            pass
        hlin_ref[...] = jnp.zeros_like(hlin_ref)


def _post_mix(o, w_o, resid, x_tile_off, modtab, n_ctx_tiles, gain, rwt, rb, o_transposed=False):
    if o_transposed:
        B, KO, N = o.shape
    else:
        B, N, KO = o.shape
    tm = ROW_TILE
    nt = N // tm
    last = lambda i: jnp.minimum(i, nt - 1)
    tok = lambda w: pl.BlockSpec((None, tm, w), lambda b, i: (b, last(i), 0))
    sel = lambda: pl.BlockSpec((None, TOP_K, tm), lambda b, i: (b, 0, jnp.maximum(i - 1, 0)))
    o_spec = (pl.BlockSpec((None, KO, tm), lambda b, i: (b, 0, last(i))) if o_transposed else tok(KO))
    split = len(resid) == 2
    resid_specs = (_stream_specs(tm, n_ctx_tiles, nt - 1) if split else
                   [pl.BlockSpec((None, tm, D), lambda b, i: (b, last(i) + x_tile_off, 0))])
    return pl.pallas_call(
        functools.partial(_post_mix_kernel, nt=nt, o_transposed=o_transposed,
                          split_ctx_tiles=n_ctx_tiles if split else 0),
        grid=(B, nt + 1),
        in_specs=[o_spec,
                  pl.BlockSpec((KO, D), lambda b, i: (0, 0))] + resid_specs + [
                  pl.BlockSpec((None, None, 1, 6 * D), lambda b, i: (b, jnp.where(last(i) < n_ctx_tiles, 0, 1), 0, 0)),
                  pl.BlockSpec((1, D), lambda b, i: (0, 0)),
                  pl.BlockSpec((N_EXP, D), lambda b, i: (0, 0)),
                  pl.BlockSpec((N_EXP, 1), lambda b, i: (0, 0))],
        out_specs=[tok(D), tok(D),
                   pl.BlockSpec((None, tm * (D // LANES), LANES), lambda b, i: (b, i, 0)),
                   sel(), sel(), sel(),
                   pl.BlockSpec((None, N_EXP, LANES), lambda b, i: (b, 0, 0))],
        out_shape=[jax.ShapeDtypeStruct((B, N, D), f32), jax.ShapeDtypeStruct((B, N, D), bf16),
                   jax.ShapeDtypeStruct((B, (N + tm) * (D // LANES), LANES), f32),
                   jax.ShapeDtypeStruct((B, TOP_K, N), i32), jax.ShapeDtypeStruct((B, TOP_K, N), f32),
                   jax.ShapeDtypeStruct((B, TOP_K, N), i32),
                   jax.ShapeDtypeStruct((B, N_EXP, LANES), f32)],
        scratch_shapes=[pltpu.VMEM((tm, D), f32), pltpu.VMEM((tm, D), f32)],
        compiler_params=_cparams(("arbitrary", "arbitrary")),
        name="post_mix",
    )(o, w_o, *resid, modtab, gain, rwt, rb)


META_W = 256
PLAN_ALIGN = 1024


def _round_up(n, m):
    return -(-n // m) * m


def _moe_sizes(n_tok):
    tm = MOE_TILE
    nt_max = (n_tok * TOP_K + N_EXP * (tm - 1)) // tm + 1
    ntp = _round_up(nt_max + MOE_GROUP, int(np.lcm(PLAN_ALIGN // tm, MOE_GROUP)))
    assert ntp <= META_W
    return ntp, _round_up(n_tok, PLAN_ALIGN)


def _plan_kernel(e_ref, r_ref, cnt_ref, pos_ref, meta_ref, *, n_tok):
    tm = MOE_TILE
    ntile = jnp.floor((cnt_ref[...] + (tm - 1.0)) * (1.0 / tm))
    ntb = ntile.astype(bf16)
    ei = lax.broadcasted_iota(i32, (N_EXP, LANES), 0)
    ej = lax.broadcasted_iota(i32, (N_EXP, LANES), 1)
    lower = jnp.where(ej <= ei, 1.0, 0.0)[:, :N_EXP].astype(bf16)
    tend = _dot(lower, ntb)
    tstart = tend - ntile
    tt = lax.broadcasted_iota(i32, (N_EXP, META_W), 1).astype(f32)
    te = jnp.sum(jnp.where(tt >= tend[:, 0:1], 1.0, 0.0), axis=0, keepdims=True)
    meta_ref[...] = jnp.zeros_like(meta_ref)
    meta_ref[0:1, :] = jnp.minimum(te, N_EXP - 1.0).astype(i32)
    meta_ref[1:2, :] = jnp.broadcast_to(tend[N_EXP - 1:N_EXP, 0:1], (1, META_W)).astype(i32)
    e = e_ref[...]
    base = jnp.zeros(e.shape, f32)
    for ex in range(N_EXP):
        base = jnp.where(e == ex, tstart[ex:ex + 1, 0:1] * float(tm), base)
    pos_ref[...] = jnp.zeros_like(pos_ref)
    pos_ref[:, 0:n_tok] = base.astype(i32) + r_ref[...]


def _plan(e_t, r_t, cnt):
    B, K, N = e_t.shape
    _, npad = _moe_sizes(N)
    return pl.pallas_call(
        functools.partial(_plan_kernel, n_tok=N),
        grid=(B,),
        in_specs=[pl.BlockSpec((None, K, N), lambda b: (b, 0, 0)),
                  pl.BlockSpec((None, K, N), lambda b: (b, 0, 0)),
                  pl.BlockSpec((None, N_EXP, LANES), lambda b: (b, 0, 0))],
        out_specs=[pl.BlockSpec((None, K, npad), lambda b: (b, 0, 0)),
                   pl.BlockSpec((None, SUBLANES, META_W), lambda b: (b, 0, 0))],
        out_shape=[jax.ShapeDtypeStruct((B, K, npad), i32), jax.ShapeDtypeStruct((B, SUBLANES, META_W), i32)],
        compiler_params=_cparams(("arbitrary",)),
        name="moe_plan",
    )(e_t, r_t, cnt)


def _plan_invert(pos, w_t, n_tok):
    B, K, npad = pos.shape
    ntp, _ = _moe_sizes(n_tok)
    plen = ntp * MOE_TILE
    nch = D // LANES
    mesh = plsc.VectorSubcoreMesh(core_axis_name="c", subcore_axis_name="s")
    n_cores = mesh.num_cores
    assert 2 * B <= n_cores * mesh.num_subcores and n_tok % SC_LANES == 0 and plen % SC_LANES == 0

    @functools.partial(
        pl.kernel, mesh=mesh,
        out_type=[jax.ShapeDtypeStruct((B * plen,), i32), jax.ShapeDtypeStruct((B * plen,), f32)],
        scratch_types=[pltpu.VMEM((npad,), i32), pltpu.VMEM((n_tok,), f32),
                       pltpu.VMEM((plen,), i32), pltpu.VMEM((plen,), f32)],
        compiler_params=dataclasses.replace(pltpu.CompilerParams(), needs_layout_passes=False))
    def invert(pos_hbm, w_hbm, rows_hbm, ws_hbm, pos_c, w_c, rows_v, ws_v):
        wid = lax.axis_index("s") * n_cores + lax.axis_index("c")
        b = wid % B
        lane = lax.iota(i32, SC_LANES)

        def load_pos(k):
            pltpu.sync_copy(pos_hbm.at[pl.ds(pl.multiple_of((b * K + k) * npad, SUBLANES), npad)], pos_c)

        @pl.when(wid < B)
        def _():
            pad = jnp.full((SC_LANES,), n_tok * nch, i32)

            @pl.loop(0, plen, step=SC_LANES)
            def _(i):
                rows_v[pl.ds(i, SC_LANES)] = pad

            for k in range(K):
                load_pos(k)

                @pl.loop(0, n_tok, step=SC_LANES)
                def _(n):
                    plsc.store_scatter(rows_v, [pos_c[pl.ds(n, SC_LANES)]], (lane + n) * nch)

            pltpu.sync_copy(rows_v, rows_hbm.at[pl.ds(pl.multiple_of(b * plen, SUBLANES), plen)])

        @pl.when(jnp.logical_and(wid >= B, wid < 2 * B))
        def _():
            zero = jnp.zeros((SC_LANES,), f32)

            @pl.loop(0, plen, step=SC_LANES)
            def _(i):
                ws_v[pl.ds(i, SC_LANES)] = zero

            for k in range(K):
                load_pos(k)
                pltpu.sync_copy(w_hbm.at[pl.ds(pl.multiple_of((b * K + k) * n_tok, SUBLANES), n_tok)], w_c)

                @pl.loop(0, n_tok, step=SC_LANES)
                def _(n):
                    plsc.store_scatter(ws_v, [pos_c[pl.ds(n, SC_LANES)]], w_c[pl.ds(n, SC_LANES)])

            pltpu.sync_copy(ws_v, ws_hbm.at[pl.ds(pl.multiple_of(b * plen, SUBLANES), plen)])

    return invert(pos.reshape(-1), w_t.reshape(-1))


def _moe_kernel(te_ref, nt_ref, *refs, ntp, n_tok):
    R = MOE_GROUP
    rg_ref, rs_ref, ws_ref, hlin_ref = refs[0:4]
    wgu_refs, wd_refs = refs[4:4 + R], refs[4 + R:4 + 2 * R]
    out_ref, acc_ref = refs[4 + 2 * R:6 + 2 * R]
    bufs = refs[6 + 2 * R:]
    xs_refs = [bufs[0:R], bufs[R:2 * R]]
    ylin_refs = [bufs[2 * R:3 * R], bufs[3 * R:4 * R]]
    b = pl.program_id(0)
    t = pl.program_id(1)
    TM = MOE_TILE
    NCH = D // LANES
    U = 8
    TMP = TM + SUBLANES
    ngrp = ntp // R

    @pl.when(t == 0)
    def _():
        acc_ref[...] = jnp.zeros_like(acc_ref)

    @pl.when(jnp.logical_and(b == 0, t == 0))
    def _():
        for buf in bufs:
            buf[...] = jnp.zeros_like(buf)

    def gather_rows(p, r, c):
        for m in range(c * U, (c + 1) * U):
            off = pl.multiple_of(rg_ref[r * TM + m], NCH)
            xs_refs[p][r][pl.ds(m, NCH, stride=TMP), :] = hlin_ref[pl.ds(off, NCH), :]

    def scatter_rows(p, r, c):
        offs = [pl.multiple_of(rs_ref[r * TM + c * U + u], NCH) for u in range(U)]
        news = [acc_ref[pl.ds(offs[u], NCH), :] + ylin_refs[p][r][pl.ds((c * U + u) * NCH, NCH), :]
                for u in range(U)]
        for u in range(U):
            acc_ref[pl.ds(offs[u], NCH), :] = news[u]

    def stage(p):
        row_work = [functools.partial(fn, p, r, c) for c in range(TM // U) for r in range(R)
                    for fn in (gather_rows, scatter_rows)]
        n_pieces = R * (EXP_FF // LANES + D // MXU_N)
        per_piece = -(-len(row_work) // n_pieces)

        def deal():
            for fn in row_work[:per_piece]:
                fn()
            del row_work[:per_piece]

        for r in range(R):
            x = jnp.concatenate([xs_refs[1 - p][r][pl.ds(j * TMP, TM), :] for j in range(NCH)], axis=1).astype(bf16)
            wcol = jnp.broadcast_to(ws_ref[r:r + 1, :], (SUBLANES, TM)).T[:, 0:1]
            gate = _dot(x, wgu_refs[r][:, :EXP_FF])
            deal()
            up = _dot(x, wgu_refs[r][:, EXP_FF:])
            deal()
            a = (_silu(gate) * up * wcol).astype(bf16)
            for c in range(D // MXU_N):
                y = _dot(a, wd_refs[r][:, c * MXU_N:(c + 1) * MXU_N])
                for jj in range(MXU_N // LANES):
                    j = c * (MXU_N // LANES) + jj
                    ylin_refs[1 - p][r][pl.ds(j, TM, stride=NCH), :] = y[:, jj * LANES:(jj + 1) * LANES]
                deal()
        while row_work:
            deal()

    live = (t - 2) * R < nt_ref[b]
    pl.when(jnp.logical_and(live, t % 2 == 0))(functools.partial(stage, 0))
    pl.when(jnp.logical_and(live, t % 2 == 1))(functools.partial(stage, 1))

    @pl.when(t >= ngrp + 2)
    def _():
        row0 = (t - (ngrp + 2)) * (ROW_TILE * NCH)
        for j in range(NCH):
            out_ref[:, j * LANES:(j + 1) * LANES] = acc_ref[pl.ds(row0 + j, ROW_TILE, stride=NCH), :].astype(bf16)


def _moe(te, nt, rows, wsort, hlin, w_gu, w_d, layer):
    B = hlin.shape[0]
    NCH = D // LANES
    n_tok = hlin.shape[1] // NCH - ROW_TILE
    assert n_tok % ROW_TILE == 0
    nf = n_tok // ROW_TILE
    TM = MOE_TILE
    R = MOE_GROUP
    ntp, _ = _moe_sizes(n_tok)
    assert ntp % R == 0
    ngrp = ntp // R

    def group_of(b, t, nt_ref, lag):
        grp = t - lag
        ok = jnp.logical_and(t >= lag, grp * R < nt_ref[b])
        return b * ngrp + jnp.where(ok, grp, ngrp - 1)

    def rows_spec(lag):
        return pl.BlockSpec((R * TM,), lambda b, t, te_ref, nt_ref: (group_of(b, t, nt_ref, lag),),
                            memory_space=pltpu.SMEM)

    ws_spec = pl.BlockSpec((None, R, TM), lambda b, t, te_ref, nt_ref: (group_of(b, t, nt_ref, 1), 0, 0))

    def w_spec(shape, r):
        def index(b, t, te_ref, nt_ref):
            tile = jnp.clip((t - 1) * R + r, 0, nt_ref[b] - 1)
            return (layer, te_ref[b * META_W + tile], 0, 0)
        return pl.BlockSpec((None, None) + shape, index)

    grid_spec = pltpu.PrefetchScalarGridSpec(
        num_scalar_prefetch=2,
        grid=(B, ngrp + 2 + nf),
        in_specs=([rows_spec(0), rows_spec(2), ws_spec,
                   pl.BlockSpec((None, (n_tok + ROW_TILE) * NCH, LANES), lambda b, t, *_: (b, 0, 0),
                                pipeline_mode=pl.Buffered(1))]
                  + [w_spec((D, 2 * EXP_FF), r) for r in range(R)]
                  + [w_spec((EXP_FF, D), r) for r in range(R)]),
        out_specs=pl.BlockSpec((None, ROW_TILE, D), lambda b, t, *_: (b, jnp.maximum(t - (ngrp + 2), 0), 0)),
        scratch_shapes=([pltpu.VMEM(((n_tok + SUBLANES) * NCH, LANES), f32)]
                        + [pltpu.VMEM(((TM + SUBLANES) * NCH, LANES), f32) for _ in range(2 * R)]
                        + [pltpu.VMEM((TM * NCH, LANES), f32) for _ in range(2 * R)]),
    )
    return pl.pallas_call(
        functools.partial(_moe_kernel, ntp=ntp, n_tok=n_tok),
        grid_spec=grid_spec,
        out_shape=jax.ShapeDtypeStruct((B, n_tok, D), bf16),
        compiler_params=_cparams(("arbitrary", "arbitrary"), MOE_VMEM_LIMIT),
        name="moe",
    )(te, nt, rows, rows, wsort.reshape(B * ngrp, R, TM), hlin, *([w_gu] * R), *([w_d] * R))


def _routed_experts(e_t, w_t, r_t, cnt, hlin, w_gu, w_d, layer):
    N = e_t.shape[2]
    pos, meta = _plan(e_t, r_t, cnt)
    te = meta[:, 0, :].reshape(-1)
    nt = meta[:, 1, 0]
    rows, wsort = _plan_invert(pos, w_t, N)
    return _moe(te, nt, rows, wsort, hlin, w_gu, w_d, layer)


def _shared_ffn(fin, shgu_ref, shd_ref):
    gu = _dot(fin, shgu_ref[...])
    return _dot((_silu(gu[:, :SH_FF]) * gu[:, SH_FF:]).astype(bf16), shd_ref[...])


def _post_ffn_final_kernel(x1_ref, routed_ref, fin_ref, shgu_ref, shd_ref, mod_ref, out_ref):
    out_ref[...] = x1_ref[...] + mod_ref[...][:, 5 * D:6 * D] * (
        routed_ref[...] + _shared_ffn(fin_ref[...], shgu_ref, shd_ref))


def _post_ffn_final(x1, routed, fin, sh_gu, sh_d, modtab):
    B, N, _ = x1.shape
    tm = ROW_TILE
    tok = lambda w: pl.BlockSpec((None, tm, w), lambda b, i: (b, i, 0))
    full = lambda r, c: pl.BlockSpec((r, c), lambda b, i: (0, 0))
    return pl.pallas_call(
        _post_ffn_final_kernel,
        grid=(B, N // tm),
        in_specs=[tok(D), tok(D), tok(D), full(D, 2 * SH_FF), full(SH_FF, D),
                  pl.BlockSpec((None, None, 1, 6 * D), lambda b, i: (b, 1, 0, 0))],
        out_specs=tok(D),
        out_shape=jax.ShapeDtypeStruct((B, N, D), f32),
        compiler_params=_cparams(("arbitrary", "arbitrary")),
        name="post_ffn_final",
    )(x1, routed, fin, sh_gu, sh_d, modtab)


def _mla_qkv(a, qan_ref, wqn_ref, wqr_ref, kvan_ref, wk_ref, wv_ref, qnn_ref, qnr_ref, knn_ref, knr_ref,
             cos_ref, sin_ref, q_ref, k_ref, v_ref):
    tm = a.shape[0]
    scale = MLA_QK ** -0.5 * float(np.log2(np.e))
    cos = cos_ref[...]
    sin = sin_ref[...]
    lane = lax.broadcasted_iota(i32, (tm, LANES), 1)
    first = (lane // (MLA_ROPE // 4)) % 2 == 0

    def rope(xb):
        sw = jnp.where(first, pltpu.roll(xb, LANES - MLA_ROPE // 4, axis=1), pltpu.roll(xb, MLA_ROPE // 4, axis=1))
        return xb * cos + sw * sin

    qa = (_rms(a[:, :MLA_QR]) * qan_ref[...]).astype(bf16)
    qn = _dot(qa, wqn_ref[...])
    yield
    qr = _dot(qa, wqr_ref[...])
    yield
    ri = lax.broadcasted_iota(i32, (LANES, LANES), 0) // MLA_ROPE
    ci = lax.broadcasted_iota(i32, (LANES, LANES), 1) // MLA_ROPE
    seg = jnp.where(ri == ci, 1.0, 0.0).astype(bf16)

    def seg_sum(sq):
        hi = sq.astype(bf16)
        r1 = sq - hi.astype(f32)
        mid = r1.astype(bf16)
        lo = (r1 - mid.astype(f32)).astype(bf16)
        return _dot(hi, seg) + _dot(mid, seg) + _dot(lo, seg)

    qr_blocks = []
    for p in range(MLA_H // 2):
        blk = qr[:, p * LANES:(p + 1) * LANES]
        blk = blk * lax.rsqrt(seg_sum(blk * blk) * (1.0 / MLA_ROPE) + EPS) * qnr_ref[:, p * LANES:(p + 1) * LANES]
        qr_blocks.append(rope(blk) * scale)
        yield

    kv = (_rms(a[:, MLA_QR:MLA_QR + MLA_KVR]) * kvan_ref[...]).astype(bf16)
    kn = _dot(kv, wk_ref[...])
    yield
    v_ref[...] = _dot_nt(wv_ref[...], kv).astype(bf16)
    yield
    kr = a[:, MLA_QR + MLA_KVR:MLA_IN_PAD]
    kr = rope(_rms(kr, MLA_ROPE) * knr_ref[...])
    kr_odd = pltpu.roll(kr, MLA_ROPE, axis=1)
    for hd in range(MLA_H):
        sl = slice(hd * MLA_NOPE, (hd + 1) * MLA_NOPE)
        q_ref[:, 2 * hd * LANES:(2 * hd + 1) * LANES] = (_rms(qn[:, sl]) * qnn_ref[...] * scale).astype(bf16)
        q_ref[:, (2 * hd + 1) * LANES:(2 * hd + 2) * LANES] = qr_blocks[hd // 2].astype(bf16)
        k_ref[:, 2 * hd * LANES:(2 * hd + 1) * LANES] = (_rms(kn[:, sl]) * knn_ref[...]).astype(bf16)
        k_ref[:, (2 * hd + 1) * LANES:(2 * hd + 2) * LANES] = (kr if hd % 2 == 0 else kr_odd).astype(bf16)
        if hd % 2 == 1:
            yield


def _ffn_mla_kernel(x1_ref, routed_ref, fin_ref, shgu_ref, shd_ref, mod0_ref, mod1_ref, g_ref, win_ref, *refs, nt):
    qkv_refs, (x2_ref, q_ref, k_ref, v_ref, a0_ref, a1_ref) = refs[:12], refs[12:]
    a_refs = (a0_ref, a1_ref)
    i = pl.program_id(1)

    def qkv(p):
        return _mla_qkv(a_refs[p][...], *qkv_refs, q_ref, k_ref, v_ref)

    def combine(p, other=()):
        other = iter(other)

        def advance(n):
            for _ in range(n):
                next(other, None)

        gu = _dot(fin_ref[...], shgu_ref[...])
        advance(3)
        shared = _dot((_silu(gu[:, :SH_FF]) * gu[:, SH_FF:]).astype(bf16), shd_ref[...])
        advance(3)
        x2 = x1_ref[...] + mod0_ref[...][:, 5 * D:6 * D] * (routed_ref[...] + shared)
        x2_ref[...] = x2
        advance(2)
        mod1 = mod1_ref[...]
        h = (_rms(x2) * g_ref[...]) * (1.0 + mod1[:, D:2 * D]) + mod1[:, 0:D]
        advance(2)
        a_refs[p][...] = _dot(h.astype(bf16), win_ref[...])
        for _ in other:
            pass

    pl.when(i == 0)(functools.partial(combine, 0))
    for p in range(2):
        @pl.when(jnp.logical_and(jnp.logical_and(i >= 1, i < nt), i % 2 == p))
        def _(p=p):
            combine(p, qkv(1 - p))

    @pl.when(i == nt)
    def _():
        for _ in qkv((nt - 1) % 2):
            pass


def _ffn_mla(x1, routed, fin, sh_gu, sh_d, modtab0, modtab1, gain, w_in, qkv_params, cos_t, sin_t, n_ctx_tiles):
    B, T, _ = x1.shape
    tm = ROW_TILE
    nt = T // tm
    last = lambda i: jnp.minimum(i, nt - 1)
    prev = lambda i: jnp.maximum(i - 1, 0)
    tok = lambda w: pl.BlockSpec((None, tm, w), lambda b, i: (b, last(i), 0))
    modspec = lambda: pl.BlockSpec((None, None, 1, 6 * D),
                                   lambda b, i: (b, jnp.where(last(i) < n_ctx_tiles, 0, 1), 0, 0))
    full = lambda r, c: pl.BlockSpec((r, c), lambda b, i: (0, 0))
    rope_spec = lambda: pl.BlockSpec((tm, LANES), lambda b, i: (prev(i), 0))
    hw = 2 * LANES * MLA_H
    return pl.pallas_call(
        functools.partial(_ffn_mla_kernel, nt=nt),
        grid=(B, nt + 1),
        in_specs=[tok(D), tok(D), tok(D), full(D, 2 * SH_FF), full(SH_FF, D), modspec(), modspec(),
                  full(1, D), full(D, MLA_IN_PAD),
                  full(1, MLA_QR), full(MLA_QR, MLA_H * MLA_NOPE), full(MLA_QR, MLA_H * MLA_ROPE),
                  full(1, MLA_KVR), full(MLA_KVR, MLA_H * MLA_NOPE), full(MLA_H * MLA_V, MLA_KVR),
                  full(1, MLA_NOPE), full(1, MLA_H * MLA_ROPE), full(1, MLA_NOPE), full(1, LANES),
                  rope_spec(), rope_spec()],
        out_specs=[tok(D),
                   pl.BlockSpec((None, tm, hw), lambda b, i: (b, jnp.maximum(prev(i) - n_ctx_tiles, 0), 0)),
                   pl.BlockSpec((None, tm, hw), lambda b, i: (b, prev(i), 0)),
                   pl.BlockSpec((None, MLA_H * MLA_V, tm), lambda b, i: (b, 0, prev(i)))],
        out_shape=[jax.ShapeDtypeStruct((B, T, D), f32),
                   jax.ShapeDtypeStruct((B, T - n_ctx_tiles * tm, hw), bf16), jax.ShapeDtypeStruct((B, T, hw), bf16),
                   jax.ShapeDtypeStruct((B, MLA_H * MLA_V, T), bf16)],
        scratch_shapes=[pltpu.VMEM((tm, MLA_IN_PAD), f32), pltpu.VMEM((tm, MLA_IN_PAD), f32)],
        compiler_params=_cparams(("arbitrary", "arbitrary")),
        name="ffn_mla",
    )(x1, routed, fin, sh_gu, sh_d, modtab0, modtab1, gain, w_in, *qkv_params, cos_t, sin_t)


ATT_TQ = 256


def _mla_attn_kernel(q_ref, k_ref, vt_ref, o_ref, s0_ref, s1_ref, m0_ref, m1_ref):
    tq = q_ref.shape[0]
    groups = k_ref.shape[0] // SUBLANES
    i = pl.program_id(0)
    s_refs, m_refs = (s0_ref, s1_ref), (m0_ref, m1_ref)

    @pl.when(i == 0)
    def _():
        for ref in s_refs + m_refs:
            ref[...] = jnp.zeros_like(ref)

    def stage(par):
        s = _dot_nt(k_ref[...], q_ref[...])
        s_refs[par][...] = s
        m_refs[par][...] = jnp.max(s.reshape(groups, SUBLANES, tq), axis=0)
        m = jnp.max(m_refs[1 - par][...], axis=0, keepdims=True)
        p = jnp.exp2(s_refs[1 - par][...] - m)
        lsum = jnp.sum(jnp.sum(p.reshape(groups, SUBLANES, tq), axis=0), axis=0, keepdims=True)
        acc = _dot(vt_ref[...], p.astype(bf16))
        o_ref[...] = (acc / lsum).astype(bf16)

    pl.when(i % 2 == 0)(functools.partial(stage, 0))
    pl.when(i % 2 == 1)(functools.partial(stage, 1))


def _mla_attn(q, k, vt):
    B, S, _ = q.shape
    T = k.shape[1]
    tq = ATT_TQ
    nq = S // tq
    ntile = B * MLA_H * nq
    assert S % tq == 0 and T % SUBLANES == 0

    def tile(g):
        g = jnp.clip(g, 0, ntile - 1)
        return g // (MLA_H * nq), (g // nq) % MLA_H, g % nq

    def q_index(g):
        b, h, i = tile(g)
        return b, i, h

    def k_index(g):
        b, h, _ = tile(g)
        return b, 0, h

    def vt_index(g):
        b, h, _ = tile(g - 1)
        return b, h, 0

    def o_index(g):
        b, h, i = tile(g - 1)
        return b, h, i

    return pl.pallas_call(
        _mla_attn_kernel,
        grid=(ntile + 1,),
        in_specs=[pl.BlockSpec((None, tq, 2 * LANES), q_index),
                  pl.BlockSpec((None, T, 2 * LANES), k_index),
                  pl.BlockSpec((None, MLA_V, T), vt_index)],
        out_specs=pl.BlockSpec((None, MLA_V, tq), o_index),
        out_shape=jax.ShapeDtypeStruct((B, MLA_H * MLA_V, S), bf16),
        scratch_shapes=[pltpu.VMEM((T, tq), f32), pltpu.VMEM((T, tq), f32),
                        pltpu.VMEM((SUBLANES, tq), f32), pltpu.VMEM((SUBLANES, tq), f32)],
        compiler_params=_cparams(("arbitrary",)),
        name="mla_attn",
    )(q, k, vt)


def _axial_angles(rows_n, rot_dim):
    axis_dim = rot_dim // 2
    inv = ROPE_BASE ** (-jnp.arange(0, axis_dim, 2, dtype=f32) / axis_dim)
    row = jnp.repeat(jnp.arange(rows_n, dtype=f32), GRID_W)
    col = jnp.tile(jnp.arange(GRID_W, dtype=f32), rows_n)
    return row[:, None] * inv, col[:, None] * inv


def _rope_tables(seq, n_ctx, rot_dim, reps):
    ang_r, ang_c = _axial_angles(seq // GRID_W, rot_dim)
    cos = jnp.concatenate([jnp.cos(ang_r)] * 2 + [jnp.cos(ang_c)] * 2, axis=1)
    sin = jnp.concatenate([-jnp.sin(ang_r), jnp.sin(ang_r), -jnp.sin(ang_c), jnp.sin(ang_c)], axis=1)
    cos = jnp.concatenate([jnp.ones((n_ctx, rot_dim), f32), cos], axis=0)
    sin = jnp.concatenate([jnp.zeros((n_ctx, rot_dim), f32), sin], axis=0)
    return jnp.tile(cos, (1, reps)), jnp.tile(sin, (1, reps))


def kernel(x, c, ctx, c_ctx, ada_w, ada_b, norm_mix, norm_ffn, ret_w_in, ret_decay_f, ret_decay_b, ret_w_o,
           mla_w_in, mla_q_a_norm, mla_w_q_b, mla_kv_a_norm, mla_w_kv_b, mla_q_norm, mla_k_norm, mla_w_o,
           router_w, router_bias, exp_w_gu, exp_w_down, sh_w_gu, sh_w_down):
    B, S, _ = x.shape
    n_ctx = ctx.shape[1]
    assert n_ctx % ROW_TILE == 0 and S % ROW_TILE == 0 and S % GRID_W == 0
    n_ctx_tiles = n_ctx // ROW_TILE

    rows = -(-(B + 1) // SUBLANES) * SUBLANES
    cc = jnp.zeros((rows, D), f32).at[:B].set(c).at[B].set(c_ctx)
    mod = _ada(cc, ada_w, ada_b)

    def modtab(i):
        ctx_row = jnp.broadcast_to(mod[i, B][None, :], (B, 6 * D))
        return jnp.stack([ctx_row, mod[i, :B]], axis=1)[:, :, None, :]

    mod0, mod1 = modtab(0), modtab(1)

    cos_r, sin_r = _rope_tables(S, n_ctx, RET_DK, 1)
    q, k, v, gf, gb = _ret_inproj(ctx, x, mod0, norm_mix[0][None, :], ret_w_in[0].astype(bf16), cos_r, sin_r,
                                  n_ctx_tiles)
    dtab = jnp.broadcast_to(jnp.concatenate([ret_decay_f[0], ret_decay_b[0]])[:, None], (2 * RET_H, LANES))
    o = _ret_scan(dtab, q, k, v, gf, gb, n_ctx)
    x1, fin, hlin, e_t, w_t, r_t, cnt = _post_mix(
        o, ret_w_o[0].astype(bf16), (ctx, x), 0, mod0, n_ctx_tiles, norm_ffn[0][None, :],
        router_w[0].T, router_bias[0][:, None])
    exp_gu, exp_d = exp_w_gu.astype(bf16), exp_w_down.astype(bf16)
    routed = _routed_experts(e_t, w_t, r_t, cnt, hlin, exp_gu, exp_d, 0)
    w_in1 = jnp.zeros((D, MLA_IN_PAD), f32).at[:, :mla_w_in.shape[2]].set(mla_w_in[0]).astype(bf16)

    wq = mla_w_q_b[0].reshape(MLA_QR, MLA_H, MLA_QK)
    wqn = wq[:, :, :MLA_NOPE].reshape(MLA_QR, MLA_H * MLA_NOPE).astype(bf16)
    wqr = wq[:, :, MLA_NOPE:].reshape(MLA_QR, MLA_H * MLA_ROPE).astype(bf16)
    wkv = mla_w_kv_b[0].reshape(MLA_KVR, MLA_H, MLA_NOPE + MLA_V)
    wk = wkv[:, :, :MLA_NOPE].reshape(MLA_KVR, MLA_H * MLA_NOPE).astype(bf16)
    wv = wkv[:, :, MLA_NOPE:].reshape(MLA_KVR, MLA_H * MLA_V).T.astype(bf16)
    qnn = mla_q_norm[0][None, :MLA_NOPE]
    qnr = jnp.tile(mla_q_norm[0][None, MLA_NOPE:], (1, MLA_H))
    knn = mla_k_norm[0][None, :MLA_NOPE]
    knr = jnp.concatenate([mla_k_norm[0][MLA_NOPE:], jnp.zeros((LANES - MLA_ROPE,), f32)])[None, :]
    cos_m, sin_m = _rope_tables(S, n_ctx, MLA_ROPE, LANES // MLA_ROPE)
    qkv_params = (mla_q_a_norm[0][None, :], wqn, wqr, mla_kv_a_norm[0][None, :], wk, wv, qnn, qnr, knn, knr)
    x2, qf, kf, vf = _ffn_mla(x1, routed, fin, sh_w_gu[0].astype(bf16), sh_w_down[0].astype(bf16), mod0, mod1,
                              norm_mix[1][None, :], w_in1, qkv_params, cos_m, sin_m, n_ctx_tiles)
    o1 = _mla_attn(qf, kf, vf)
    x3, fin1, hlin1, e1, w1, r1, cnt1 = _post_mix(
        o1, mla_w_o[0].astype(bf16), (x2,), n_ctx_tiles, mod1, 0, norm_ffn[1][None, :],
        router_w[1].T, router_bias[1][:, None], o_transposed=True)
    routed1 = _routed_experts(e1, w1, r1, cnt1, hlin1, exp_gu, exp_d, 1)
    return _post_ffn_final(x3, routed1, fin1, sh_w_gu[1].astype(bf16), sh_w_down[1].astype(bf16), mod1)
```

```python
import dataclasses
import functools

import jax
import jax.numpy as jnp
import numpy as np
from jax import lax
from jax.experimental import pallas as pl
from jax.experimental.pallas import tpu as pltpu
from jax.experimental.pallas import tpu_sc as plsc

f32 = jnp.float32
bf16 = jnp.bfloat16
i32 = jnp.int32

D = 1024
GRID_W = 64
EPS = 1e-6
ROPE_BASE = 10000.0
RET_H = 4
RET_DK = 256
RET_DV = 512
RET_VW = RET_H * RET_DV
RET_CHUNK = 256
MLA_H = 8
MLA_NOPE = 128
MLA_ROPE = 64
MLA_QK = MLA_NOPE + MLA_ROPE
MLA_V = 128
MLA_QR = 384
MLA_KVR = 256
MLA_IN_PAD = 768
N_EXP = 64
TOP_K = 8
N_GRP = 8
TOPK_GRP = 4
EXP_FF = 256
SH_FF = 256
ROUTED_SCALE = 2.5

LANES = 128
SUBLANES = 8
SC_LANES = 16
MXU_N = 256
ROW_TILE = 256
MOE_TILE = 256
MOE_GROUP = 2
ADA_COLS = 1536
PROJ_COLS = 512
RIDER_SLICE_BYTES = 4 * 1024 * 1024
VMEM_LIMIT = 56 * 1024 * 1024
MOE_VMEM_LIMIT = 62 * 1024 * 1024


def _cparams(sem, vmem=VMEM_LIMIT):
    return pltpu.CompilerParams(dimension_semantics=sem, vmem_limit_bytes=vmem)


def _sigmoid(x):
    return 1.0 / (1.0 + jnp.exp(-x))


def _silu(x):
    return x * _sigmoid(x)


def _rms(x, n=None):
    n = x.shape[-1] if n is None else n
    return x * lax.rsqrt(jnp.sum(x * x, axis=-1, keepdims=True) * (1.0 / n) + EPS)


def _dot(a, b):
    return jnp.dot(a, b, preferred_element_type=f32)


def _dot_nt(a, b):
    return lax.dot_general(a, b, (((1,), (1,)), ((), ())), preferred_element_type=f32)


def _ada_kernel(c_ref, w_ref, b_ref, o_ref):
    s = _silu(c_ref[...]).astype(bf16)
    o_ref[...] = _dot(s, w_ref[...].astype(bf16)) + b_ref[...]


def _ada(cc, ada_w, ada_b):
    depth = ada_w.shape[0]
    rows = cc.shape[0]
    tn = ADA_COLS
    return pl.pallas_call(
        _ada_kernel,
        grid=(depth, 6 * D // tn),
        in_specs=[pl.BlockSpec((rows, D), lambda i, j: (0, 0)),
                  pl.BlockSpec((None, D, tn), lambda i, j: (i, 0, j)),
                  pl.BlockSpec((None, 1, tn), lambda i, j: (i, 0, j))],
        out_specs=pl.BlockSpec((None, rows, tn), lambda i, j: (i, 0, j)),
        out_shape=jax.ShapeDtypeStruct((depth, rows, 6 * D), f32),
        compiler_params=_cparams(("arbitrary", "arbitrary")),
        name="ada",
    )(cc, ada_w, ada_b.reshape(depth, 1, 6 * D))


def _stream_tile(ctx_ref, x_ref, n_ctx_tiles):
    return jnp.where(pl.program_id(1) < n_ctx_tiles, ctx_ref[...], x_ref[...])


def _stream_specs(tm, n_ctx_tiles, last=None):
    clamp = (lambda i: i) if last is None else (lambda i: jnp.minimum(i, last))
    return [pl.BlockSpec((None, tm, D), lambda b, i: (b, jnp.minimum(clamp(i), n_ctx_tiles - 1), 0)),
            pl.BlockSpec((None, tm, D), lambda b, i: (b, jnp.maximum(clamp(i) - n_ctx_tiles, 0), 0))]


def _ret_inproj_kernel(ctx_ref, x_ref, mod_ref, g_ref, w_ref, cos_ref, sin_ref, *refs, n_ctx_tiles, n_cast):
    cast_in, cast_out = refs[:n_cast], refs[n_cast + 5:]
    q_ref, k_ref, v_ref, gf_ref, gb_ref = refs[n_cast:n_cast + 5]
    x = _stream_tile(ctx_ref, x_ref, n_ctx_tiles)
    mod = mod_ref[...]
    h = (_rms(x) * g_ref[...]) * (1.0 + mod[:, D:2 * D]) + mod[:, 0:D]
    hb = h.astype(bf16)
    cos = cos_ref[...]
    sin = sin_ref[...]

    def rope(a):
        outs = []
        for half in range(2):
            sl = slice(half * LANES, (half + 1) * LANES)
            ah = a[:, sl]
            outs.append(ah * cos[:, sl] + pltpu.roll(ah, LANES // 2, axis=1) * sin[:, sl])
        return jnp.concatenate(outs, axis=1)

    for hd in range(RET_H):
        sl = slice(hd * RET_DK, (hd + 1) * RET_DK)
        q_ref[:, sl] = rope(_dot(hb, w_ref[:, sl])).astype(bf16)
    for src, dst in zip(cast_in, cast_out):
        dst[...] = src[...].astype(bf16)
    for hd in range(RET_H):
        sl = slice(hd * RET_DK, (hd + 1) * RET_DK)
        wsl = slice(D + hd * RET_DK, D + (hd + 1) * RET_DK)
        k_ref[:, sl] = (rope(_dot(hb, w_ref[:, wsl])) * (RET_DK ** -0.5)).astype(bf16)
    cw = PROJ_COLS
    for c in range(RET_VW // cw):
        sl = slice(c * cw, (c + 1) * cw)
        v_ref[:, sl] = _dot(hb, w_ref[:, 2 * D + c * cw:2 * D + (c + 1) * cw]).astype(bf16)
        gf_ref[:, sl] = _silu(_dot(hb, w_ref[:, 2 * D + RET_VW + c * cw:2 * D + RET_VW + (c + 1) * cw])).astype(bf16)
        gb_ref[:, sl] = _silu(_dot(hb, w_ref[:, 2 * D + 2 * RET_VW + c * cw:2 * D + 2 * RET_VW + (c + 1) * cw])).astype(bf16)


def _cast_steps(batch, n_tiles):
    return batch * max(n_tiles - 1, 1)


def _ret_inproj(ctx, x, modtab, gain, w_in, cos_t, sin_t, n_ctx_tiles, riders=()):
    B = x.shape[0]
    T = ctx.shape[1] + x.shape[1]
    tm = ROW_TILE
    nt = T // tm
    n_in = w_in.shape[1]
    tok = lambda w: pl.BlockSpec((None, tm, w), lambda b, i: (b, i, 0))
    steps = _cast_steps(B, nt)
    per_b = steps // B

    def rider_spec(arr):
        rows, cols = arr.shape
        assert rows % steps == 0
        return pl.BlockSpec((rows // steps, cols), lambda b, i: (b * per_b + jnp.minimum(i, per_b - 1), 0))

    return pl.pallas_call(
        functools.partial(_ret_inproj_kernel, n_ctx_tiles=n_ctx_tiles, n_cast=len(riders)),
        grid=(B, nt),
        in_specs=_stream_specs(tm, n_ctx_tiles) + [
                  pl.BlockSpec((None, None, 1, 6 * D), lambda b, i: (b, jnp.where(i < n_ctx_tiles, 0, 1), 0, 0)),
                  pl.BlockSpec((1, D), lambda b, i: (0, 0)),
                  pl.BlockSpec((D, n_in), lambda b, i: (0, 0), pipeline_mode=pl.Buffered(1)),
                  pl.BlockSpec((tm, RET_DK), lambda b, i: (i, 0)),
                  pl.BlockSpec((tm, RET_DK), lambda b, i: (i, 0))] + [rider_spec(r) for r in riders],
        out_specs=[tok(D), tok(D), tok(RET_VW), tok(RET_VW), tok(RET_VW)] + [rider_spec(r) for r in riders],
        out_shape=[jax.ShapeDtypeStruct((B, T, D), bf16), jax.ShapeDtypeStruct((B, T, D), bf16),
                   jax.ShapeDtypeStruct((B, T, RET_VW), bf16), jax.ShapeDtypeStruct((B, T, RET_VW), bf16),
                   jax.ShapeDtypeStruct((B, T, RET_VW), bf16)]
                  + [jax.ShapeDtypeStruct(r.shape, bf16) for r in riders],
        compiler_params=_cparams(("arbitrary", "arbitrary")),
        name="ret_inproj",
    )(ctx, x, modtab, gain, w_in, cos_t, sin_t, *riders)


def _ret_chunk_index(t, nc, ncc):
    u = t - nc
    back = jnp.where(u < ncc, ncc - 1 - u, nc - 1 - u + ncc)
    return jnp.where(t < nc, t, back)


def _ret_scan_kernel(dt_ref, q_ref, k_ref, v_ref, gf_ref, gb_ref, o_ref,
                     s_ref, of_ref, mask_ref, dq_ref, dk_ref, dc_ref, *, nc, ncc):
    t = pl.program_id(1)
    C = RET_CHUNK

    def init(direction):
        s_ref[...] = jnp.zeros_like(s_ref)
        ii = lax.broadcasted_iota(i32, (C, C), 0)
        jj = lax.broadcasted_iota(i32, (C, C), 1)
        rel = (ii - jj if direction == 0 else jj - ii).astype(f32)
        pos = lax.broadcasted_iota(i32, (C, 1), 0).astype(f32)
        for hd in range(RET_H):
            r = direction * RET_H + hd
            lg = -jnp.exp(dt_ref[r:r + 1, :])
            lg1 = lg[:, 0:1]
            mask_ref[hd] = jnp.where(rel >= 0, jnp.exp(lg1 * jnp.maximum(rel, 0.0)), 0.0)
            if direction == 0:
                dq_ref[hd] = jnp.exp(lg1 * (pos + 1.0))
                dk_ref[hd] = jnp.exp(lg1 * (C - 1.0 - pos))
            else:
                dq_ref[hd] = jnp.exp(lg1 * (C - pos))
                dk_ref[hd] = jnp.exp(lg1 * pos)
            dc_ref[hd] = jnp.exp(lg * float(C))

    pl.when(t == 0)(functools.partial(init, 0))
    pl.when(t == nc)(functools.partial(init, 1))

    row0 = pl.multiple_of(_ret_chunk_index(t, nc, ncc) * C, C)

    def step(forward):
        for hd in range(RET_H):
            ks = slice(hd * RET_DK, (hd + 1) * RET_DK)
            vs = slice(hd * RET_DV, (hd + 1) * RET_DV)
            qh = q_ref[:, ks]
            kh = k_ref[:, ks]
            vh = v_ref[:, vs]
            p = (_dot_nt(qh, kh) * mask_ref[hd]).astype(bf16)
            y = _dot(p, vh) + _dot(qh, s_ref[hd].astype(bf16)) * dq_ref[hd]
            kd = (kh.astype(f32) * dk_ref[hd]).astype(bf16)
            upd = lax.dot_general(kd, vh, (((0,), (0,)), ((), ())), preferred_element_type=f32)
            s_ref[hd] = s_ref[hd] * dc_ref[hd][0:1, 0:1] + upd
            yn = _rms(y)
            if forward:
                of_ref[pl.ds(row0, C), vs] = (gf_ref[:, vs].astype(f32) * yn).astype(bf16)
            else:
                o_ref[:, vs] = (of_ref[pl.ds(row0, C), vs].astype(f32) + gb_ref[:, vs].astype(f32) * yn).astype(bf16)

    pl.when(t < nc)(functools.partial(step, True))
    pl.when(t >= nc)(functools.partial(step, False))


def _ret_scan(dtab, q, k, v, gf, gb, n_ctx):
    B, T, _ = q.shape
    C = RET_CHUNK
    nc = T // C
    ncc = n_ctx // C
    cidx = functools.partial(_ret_chunk_index, nc=nc, ncc=ncc)
    first_back = ncc - 1
    return pl.pallas_call(
        functools.partial(_ret_scan_kernel, nc=nc, ncc=ncc),
        grid=(B, 2 * nc),
        in_specs=[pl.BlockSpec((2 * RET_H, LANES), lambda b, t: (0, 0)),
                  pl.BlockSpec((None, C, D), lambda b, t: (b, cidx(t), 0)),
                  pl.BlockSpec((None, C, D), lambda b, t: (b, cidx(t), 0)),
                  pl.BlockSpec((None, C, RET_VW), lambda b, t: (b, cidx(t), 0)),
                  pl.BlockSpec((None, C, RET_VW), lambda b, t: (b, jnp.where(t < nc, t, nc - 1), 0)),
                  pl.BlockSpec((None, C, RET_VW), lambda b, t: (b, jnp.where(t < nc, first_back, cidx(t)), 0))],
        out_specs=pl.BlockSpec((None, C, RET_VW), lambda b, t: (b, jnp.where(t < nc, first_back, cidx(t)), 0)),
        out_shape=jax.ShapeDtypeStruct((B, T, RET_VW), bf16),
        scratch_shapes=[pltpu.VMEM((RET_H, RET_DK, RET_DV), f32),
                        pltpu.VMEM((T, RET_VW), bf16),
                        pltpu.VMEM((RET_H, C, C), f32),
                        pltpu.VMEM((RET_H, C, 1), f32),
                        pltpu.VMEM((RET_H, C, 1), f32),
                        pltpu.VMEM((RET_H, 1, LANES), f32)],
        compiler_params=_cparams(("arbitrary", "arbitrary")),
        name="ret_scan",
    )(dtab, q, k, v, gf, gb)


def _route(f, rwt_ref, rb_ref, cnt_ref, e_ref, w_ref, r_ref):
    tm = f.shape[0]
    G = N_EXP // N_GRP
    logits = _dot_nt(rwt_ref[...].astype(bf16), f.astype(bf16))
    s = _sigmoid(logits)
    sel = s + rb_ref[...]
    mi = lax.broadcasted_iota(i32, (G, tm), 0)
    neg = -jnp.inf
    s_g = [s[g * G:(g + 1) * G, :] for g in range(N_GRP)]
    sel_g = [sel[g * G:(g + 1) * G, :] for g in range(N_GRP)]

    def first_max(a, ids, big):
        mx = jnp.max(a, axis=0, keepdims=True)
        ix = jnp.min(jnp.where(a == mx, ids, big), axis=0, keepdims=True)
        return mx, ix

    gscore = jnp.zeros((N_GRP, tm), f32)
    gi = lax.broadcasted_iota(i32, (N_GRP, tm), 0)
    for g in range(N_GRP):
        t1, i1 = first_max(sel_g[g], mi, G)
        t2 = jnp.max(jnp.where(mi == i1, neg, sel_g[g]), axis=0, keepdims=True)
        gscore = jnp.where(gi == g, t1 + t2, gscore)
    yield
    gmask = jnp.zeros((N_GRP, tm), i32)
    cur = gscore
    for _ in range(TOPK_GRP):
        _, ix = first_max(cur, gi, N_GRP)
        hit = gi == ix
        gmask = jnp.where(hit, 1, gmask)
        cur = jnp.where(hit, neg, cur)
    cand = [jnp.where(gmask[g:g + 1, :] > 0, sel_g[g], neg) for g in range(N_GRP)]
    ids = [mi + g * G for g in range(N_GRP)]

    def across(parts, op):
        acc = parts[0]
        for part in parts[1:]:
            acc = op(acc, part)
        return acc

    e_rows, w_rows = [], []
    for _ in range(TOP_K):
        mx = jnp.max(across(cand, jnp.maximum), axis=0, keepdims=True)
        ix = jnp.min(across([jnp.where(cand[g] == mx, ids[g], N_EXP) for g in range(N_GRP)], jnp.minimum),
                     axis=0, keepdims=True)
        hits = [ids[g] == ix for g in range(N_GRP)]
        cand = [jnp.where(hits[g], neg, cand[g]) for g in range(N_GRP)]
        wv = jnp.sum(across([jnp.where(hits[g], s_g[g], 0.0) for g in range(N_GRP)], jnp.add), axis=0, keepdims=True)
        e_rows.append(ix)
        w_rows.append(wv)
        yield
    wsum = w_rows[0]
    for r in range(1, TOP_K):
        wsum = wsum + w_rows[r]

    selm = [jnp.zeros((G, tm), f32) for _ in range(N_GRP)]
    for r in range(TOP_K):
        for g in range(N_GRP):
            selm[g] = jnp.where(ids[g] == e_rows[r], 1.0, selm[g])
    m_all = jnp.concatenate(selm, axis=0)
    ri = lax.broadcasted_iota(i32, (tm, tm), 0)
    ci = lax.broadcasted_iota(i32, (tm, tm), 1)
    upper = jnp.where(ri <= ci, 1.0, 0.0).astype(bf16)
    incl = _dot(m_all.astype(bf16), upper)
    carry = cnt_ref[:, 0:1]
    rank_all = carry + incl - m_all
    cnt_ref[...] = cnt_ref[...] + incl[:, tm - 1:tm]
    yield
    for r in range(TOP_K):
        rk = jnp.sum(across([jnp.where(ids[g] == e_rows[r], rank_all[g * G:(g + 1) * G, :], 0.0)
                             for g in range(N_GRP)], jnp.add), axis=0, keepdims=True)
        e_ref[r:r + 1, :] = e_rows[r]
        w_ref[r:r + 1, :] = w_rows[r] / wsum * ROUTED_SCALE
        r_ref[r:r + 1, :] = rk.astype(i32)


def _post_mix_kernel(o_ref, wo_ref, *refs, nt, o_transposed, split_ctx_tiles):
    n_resid = 2 if split_ctx_tiles else 1
    resid = refs[:n_resid]
    (mod_ref, g_ref, rwt_ref, rb_ref, x1_ref, fin_ref, hlin_ref, e_ref, w_ref, r_ref, cnt_ref,
     f0_ref, f1_ref) = refs[n_resid:]
    f_refs = (f0_ref, f1_ref)
    i = pl.program_id(1)

    @pl.when(i == 0)
    def _():
        cnt_ref[...] = jnp.zeros_like(cnt_ref)

    def route(p):
        return _route(f_refs[p][...], rwt_ref, rb_ref, cnt_ref, e_ref, w_ref, r_ref)

    def mix(p, other=()):
        other = iter(other)

        def advance(n):
            for _ in range(n):
                next(other, None)

        mod = mod_ref[...]
        x = _stream_tile(resid[0], resid[1], split_ctx_tiles) if split_ctx_tiles else resid[0][...]
        o = o_ref[...]
        cols = []
        for c in range(D // MXU_N):
            sl = slice(c * MXU_N, (c + 1) * MXU_N)
            if o_transposed:
                piece = lax.dot_general(o, wo_ref[:, sl], (((0,), (0,)), ((), ())), preferred_element_type=f32)
            else:
                piece = _dot(o, wo_ref[:, sl])
            cols.append(x[:, sl] + mod[:, 2 * D + c * MXU_N:2 * D + (c + 1) * MXU_N] * piece)
            advance(2)
        x1 = jnp.concatenate(cols, axis=1)
        x1_ref[...] = x1
        f = (_rms(x1) * g_ref[...]) * (1.0 + mod[:, 4 * D:5 * D]) + mod[:, 3 * D:4 * D]
        advance(2)
        fin_ref[...] = f.astype(bf16)
        tm = f.shape[0]
        for j in range(D // LANES):
            hlin_ref[pl.ds(j, tm, stride=D // LANES), :] = f[:, j * LANES:(j + 1) * LANES]
        f_refs[p][...] = f
        for _ in other:
            pass

    pl.when(i == 0)(functools.partial(mix, 0))
    for p in range(2):
        @pl.when(jnp.logical_and(jnp.logical_and(i >= 1, i < nt), i % 2 == p))
        def _(p=p):
            mix(p, route(1 - p))

    @pl.when(i == nt)
    def _():
        for _ in route((nt - 1) % 2):
            pass
        hlin_ref[...] = jnp.zeros_like(hlin_ref)


def _post_mix(o, w_o, resid, x_tile_off, modtab, n_ctx_tiles, gain, rwt, rb, o_transposed=False):
    if o_transposed:
        B, KO, N = o.shape
    else:
        B, N, KO = o.shape
    tm = ROW_TILE
    nt = N // tm
    last = lambda i: jnp.minimum(i, nt - 1)
    tok = lambda w: pl.BlockSpec((None, tm, w), lambda b, i: (b, last(i), 0))
    sel = lambda: pl.BlockSpec((None, TOP_K, tm), lambda b, i: (b, 0, jnp.maximum(i - 1, 0)))
    o_spec = (pl.BlockSpec((None, KO, tm), lambda b, i: (b, 0, last(i))) if o_transposed else tok(KO))
    split = len(resid) == 2
    resid_specs = (_stream_specs(tm, n_ctx_tiles, nt - 1) if split else
                   [pl.BlockSpec((None, tm, D), lambda b, i: (b, last(i) + x_tile_off, 0))])
    return pl.pallas_call(
        functools.partial(_post_mix_kernel, nt=nt, o_transposed=o_transposed,
                          split_ctx_tiles=n_ctx_tiles if split else 0),
        grid=(B, nt + 1),
        in_specs=[o_spec,
                  pl.BlockSpec((KO, D), lambda b, i: (0, 0))] + resid_specs + [
                  pl.BlockSpec((None, None, 1, 6 * D), lambda b, i: (b, jnp.where(last(i) < n_ctx_tiles, 0, 1), 0, 0)),
                  pl.BlockSpec((1, D), lambda b, i: (0, 0)),
                  pl.BlockSpec((N_EXP, D), lambda b, i: (0, 0)),
                  pl.BlockSpec((N_EXP, 1), lambda b, i: (0, 0))],
        out_specs=[tok(D), tok(D),
                   pl.BlockSpec((None, tm * (D // LANES), LANES), lambda b, i: (b, i, 0)),
                   sel(), sel(), sel(),
                   pl.BlockSpec((None, N_EXP, LANES), lambda b, i: (b, 0, 0))],
        out_shape=[jax.ShapeDtypeStruct((B, N, D), f32), jax.ShapeDtypeStruct((B, N, D), bf16),
                   jax.ShapeDtypeStruct((B, (N + tm) * (D // LANES), LANES), f32),
                   jax.ShapeDtypeStruct((B, TOP_K, N), i32), jax.ShapeDtypeStruct((B, TOP_K, N), f32),
                   jax.ShapeDtypeStruct((B, TOP_K, N), i32),
                   jax.ShapeDtypeStruct((B, N_EXP, LANES), f32)],
        scratch_shapes=[pltpu.VMEM((tm, D), f32), pltpu.VMEM((tm, D), f32)],
        compiler_params=_cparams(("arbitrary", "arbitrary")),
        name="post_mix",
    )(o, w_o, *resid, modtab, gain, rwt, rb)


META_W = 256
PLAN_ALIGN = 1024


def _round_up(n, m):
    return -(-n // m) * m


def _moe_sizes(n_tok):
    tm = MOE_TILE
    nt_max = (n_tok * TOP_K + N_EXP * (tm - 1)) // tm + 1
    ntp = _round_up(nt_max + MOE_GROUP, int(np.lcm(PLAN_ALIGN // tm, MOE_GROUP)))
    assert ntp <= META_W
    return ntp, _round_up(n_tok, PLAN_ALIGN)


def _plan_kernel(e_ref, r_ref, cnt_ref, pos_ref, meta_ref, *, n_tok):
    tm = MOE_TILE
    ntile = jnp.floor((cnt_ref[...] + (tm - 1.0)) * (1.0 / tm))
    ntb = ntile.astype(bf16)
    ei = lax.broadcasted_iota(i32, (N_EXP, LANES), 0)
    ej = lax.broadcasted_iota(i32, (N_EXP, LANES), 1)
    lower = jnp.where(ej <= ei, 1.0, 0.0)[:, :N_EXP].astype(bf16)
    tend = _dot(lower, ntb)
    tstart = tend - ntile
    tt = lax.broadcasted_iota(i32, (N_EXP, META_W), 1).astype(f32)
    te = jnp.sum(jnp.where(tt >= tend[:, 0:1], 1.0, 0.0), axis=0, keepdims=True)
    meta_ref[...] = jnp.zeros_like(meta_ref)
    meta_ref[0:1, :] = jnp.minimum(te, N_EXP - 1.0).astype(i32)
    meta_ref[1:2, :] = jnp.broadcast_to(tend[N_EXP - 1:N_EXP, 0:1], (1, META_W)).astype(i32)
    e = e_ref[...]
    base = jnp.zeros(e.shape, f32)
    for ex in range(N_EXP):
        base = jnp.where(e == ex, tstart[ex:ex + 1, 0:1] * float(tm), base)
    pos_ref[...] = jnp.zeros_like(pos_ref)
    pos_ref[:, 0:n_tok] = base.astype(i32) + r_ref[...]


def _plan(e_t, r_t, cnt):
    B, K, N = e_t.shape
    _, npad = _moe_sizes(N)
    return pl.pallas_call(
        functools.partial(_plan_kernel, n_tok=N),
        grid=(B,),
        in_specs=[pl.BlockSpec((None, K, N), lambda b: (b, 0, 0)),
                  pl.BlockSpec((None, K, N), lambda b: (b, 0, 0)),
                  pl.BlockSpec((None, N_EXP, LANES), lambda b: (b, 0, 0))],
        out_specs=[pl.BlockSpec((None, K, npad), lambda b: (b, 0, 0)),
                   pl.BlockSpec((None, SUBLANES, META_W), lambda b: (b, 0, 0))],
        out_shape=[jax.ShapeDtypeStruct((B, K, npad), i32), jax.ShapeDtypeStruct((B, SUBLANES, META_W), i32)],
        compiler_params=_cparams(("arbitrary",)),
        name="moe_plan",
    )(e_t, r_t, cnt)


def _plan_invert(pos, w_t, n_tok):
    B, K, npad = pos.shape
    ntp, _ = _moe_sizes(n_tok)
    plen = ntp * MOE_TILE
    nch = D // LANES
    mesh = plsc.VectorSubcoreMesh(core_axis_name="c", subcore_axis_name="s")
    n_cores = mesh.num_cores
    assert 2 * B <= n_cores * mesh.num_subcores and n_tok % SC_LANES == 0 and plen % SC_LANES == 0

    @functools.partial(
        pl.kernel, mesh=mesh,
        out_type=[jax.ShapeDtypeStruct((B * plen,), i32), jax.ShapeDtypeStruct((B * plen,), f32)],
        scratch_types=[pltpu.VMEM((npad,), i32), pltpu.VMEM((n_tok,), f32),
                       pltpu.VMEM((plen,), i32), pltpu.VMEM((plen,), f32)],
        compiler_params=dataclasses.replace(pltpu.CompilerParams(), needs_layout_passes=False))
    def invert(pos_hbm, w_hbm, rows_hbm, ws_hbm, pos_c, w_c, rows_v, ws_v):
        wid = lax.axis_index("s") * n_cores + lax.axis_index("c")
        b = wid % B
        lane = lax.iota(i32, SC_LANES)

        def load_pos(k):
            pltpu.sync_copy(pos_hbm.at[pl.ds(pl.multiple_of((b * K + k) * npad, SUBLANES), npad)], pos_c)

        @pl.when(wid < B)
        def _():
            pad = jnp.full((SC_LANES,), n_tok * nch, i32)

            @pl.loop(0, plen, step=SC_LANES)
            def _(i):
                rows_v[pl.ds(i, SC_LANES)] = pad

            for k in range(K):
                load_pos(k)

                @pl.loop(0, n_tok, step=SC_LANES)
                def _(n):
                    plsc.store_scatter(rows_v, [pos_c[pl.ds(n, SC_LANES)]], (lane + n) * nch)

            pltpu.sync_copy(rows_v, rows_hbm.at[pl.ds(pl.multiple_of(b * plen, SUBLANES), plen)])

        @pl.when(jnp.logical_and(wid >= B, wid < 2 * B))
        def _():
            zero = jnp.zeros((SC_LANES,), f32)

            @pl.loop(0, plen, step=SC_LANES)
            def _(i):
                ws_v[pl.ds(i, SC_LANES)] = zero

            for k in range(K):
                load_pos(k)
                pltpu.sync_copy(w_hbm.at[pl.ds(pl.multiple_of((b * K + k) * n_tok, SUBLANES), n_tok)], w_c)

                @pl.loop(0, n_tok, step=SC_LANES)
                def _(n):
                    plsc.store_scatter(ws_v, [pos_c[pl.ds(n, SC_LANES)]], w_c[pl.ds(n, SC_LANES)])

            pltpu.sync_copy(ws_v, ws_hbm.at[pl.ds(pl.multiple_of(b * plen, SUBLANES), plen)])

    return invert(pos.reshape(-1), w_t.reshape(-1))


def _moe_kernel(te_ref, nt_ref, *refs, ntp, n_tok):
    R = MOE_GROUP
    rg_ref, rs_ref, ws_ref, hlin_ref = refs[0:4]
    wgu_refs, wd_refs = refs[4:4 + R], refs[4 + R:4 + 2 * R]
    out_ref, acc_ref = refs[4 + 2 * R:6 + 2 * R]
    bufs = refs[6 + 2 * R:]
    xs_refs = [bufs[0:R], bufs[R:2 * R]]
    ylin_refs = [bufs[2 * R:3 * R], bufs[3 * R:4 * R]]
    b = pl.program_id(0)
    t = pl.program_id(1)
    TM = MOE_TILE
    NCH = D // LANES
    U = 8
    TMP = TM + SUBLANES
    ngrp = ntp // R

    @pl.when(t == 0)
    def _():
        acc_ref[...] = jnp.zeros_like(acc_ref)

    @pl.when(jnp.logical_and(b == 0, t == 0))
    def _():
        for buf in bufs:
            buf[...] = jnp.zeros_like(buf)

    def gather_rows(p, r, c):
        for m in range(c * U, (c + 1) * U):
            off = pl.multiple_of(rg_ref[r * TM + m], NCH)
            xs_refs[p][r][pl.ds(m, NCH, stride=TMP), :] = hlin_ref[pl.ds(off, NCH), :]

    def scatter_rows(p, r, c):
        offs = [pl.multiple_of(rs_ref[r * TM + c * U + u], NCH) for u in range(U)]
        news = [acc_ref[pl.ds(offs[u], NCH), :] + ylin_refs[p][r][pl.ds((c * U + u) * NCH, NCH), :]
                for u in range(U)]
        for u in range(U):
            acc_ref[pl.ds(offs[u], NCH), :] = news[u]

    def stage(p):
        row_work = [functools.partial(fn, p, r, c) for c in range(TM // U) for r in range(R)
                    for fn in (gather_rows, scatter_rows)]
        n_pieces = R * (EXP_FF // LANES + D // MXU_N)
        per_piece = -(-len(row_work) // n_pieces)

        def deal():
            for fn in row_work[:per_piece]:
                fn()
            del row_work[:per_piece]

        for r in range(R):
            x = jnp.concatenate([xs_refs[1 - p][r][pl.ds(j * TMP, TM), :] for j in range(NCH)], axis=1).astype(bf16)
            wcol = jnp.broadcast_to(ws_ref[r:r + 1, :], (SUBLANES, TM)).T[:, 0:1]
            gate = _dot(x, wgu_refs[r][:, :EXP_FF])
            deal()
            up = _dot(x, wgu_refs[r][:, EXP_FF:])
            deal()
            a = (_silu(gate) * up * wcol).astype(bf16)
            for c in range(D // MXU_N):
                y = _dot(a, wd_refs[r][:, c * MXU_N:(c + 1) * MXU_N])
                for jj in range(MXU_N // LANES):
                    j = c * (MXU_N // LANES) + jj
                    ylin_refs[1 - p][r][pl.ds(j, TM, stride=NCH), :] = y[:, jj * LANES:(jj + 1) * LANES]
                deal()
        while row_work:
            deal()

    live = (t - 2) * R < nt_ref[b]
    pl.when(jnp.logical_and(live, t % 2 == 0))(functools.partial(stage, 0))
    pl.when(jnp.logical_and(live, t % 2 == 1))(functools.partial(stage, 1))

    @pl.when(t >= ngrp + 2)
    def _():
        row0 = (t - (ngrp + 2)) * (ROW_TILE * NCH)
        for j in range(NCH):
            out_ref[:, j * LANES:(j + 1) * LANES] = acc_ref[pl.ds(row0 + j, ROW_TILE, stride=NCH), :].astype(bf16)


def _moe(te, nt, rows, wsort, hlin, w_gu, w_d, layer):
    B = hlin.shape[0]
    NCH = D // LANES
    n_tok = hlin.shape[1] // NCH - ROW_TILE
    assert n_tok % ROW_TILE == 0
    nf = n_tok // ROW_TILE
    TM = MOE_TILE
    R = MOE_GROUP
    ntp, _ = _moe_sizes(n_tok)
    assert ntp % R == 0
    ngrp = ntp // R

    def group_of(b, t, nt_ref, lag):
        grp = t - lag
        ok = jnp.logical_and(t >= lag, grp * R < nt_ref[b])
        return b * ngrp + jnp.where(ok, grp, ngrp - 1)

    def rows_spec(lag):
        return pl.BlockSpec((R * TM,), lambda b, t, te_ref, nt_ref: (group_of(b, t, nt_ref, lag),),
                            memory_space=pltpu.SMEM)

    ws_spec = pl.BlockSpec((None, R, TM), lambda b, t, te_ref, nt_ref: (group_of(b, t, nt_ref, 1), 0, 0))

    def w_spec(shape, r):
        def index(b, t, te_ref, nt_ref):
            tile = jnp.clip((t - 1) * R + r, 0, nt_ref[b] - 1)
            return (layer, te_ref[b * META_W + tile], 0, 0)
        return pl.BlockSpec((None, None) + shape, index)

    grid_spec = pltpu.PrefetchScalarGridSpec(
        num_scalar_prefetch=2,
        grid=(B, ngrp + 2 + nf),
        in_specs=([rows_spec(0), rows_spec(2), ws_spec,
                   pl.BlockSpec((None, (n_tok + ROW_TILE) * NCH, LANES), lambda b, t, *_: (b, 0, 0),
                                pipeline_mode=pl.Buffered(1))]
                  + [w_spec((D, 2 * EXP_FF), r) for r in range(R)]
                  + [w_spec((EXP_FF, D), r) for r in range(R)]),
        out_specs=pl.BlockSpec((None, ROW_TILE, D), lambda b, t, *_: (b, jnp.maximum(t - (ngrp + 2), 0), 0)),
        scratch_shapes=([pltpu.VMEM(((n_tok + SUBLANES) * NCH, LANES), f32)]
                        + [pltpu.VMEM(((TM + SUBLANES) * NCH, LANES), f32) for _ in range(2 * R)]
                        + [pltpu.VMEM((TM * NCH, LANES), f32) for _ in range(2 * R)]),
    )
    return pl.pallas_call(
        functools.partial(_moe_kernel, ntp=ntp, n_tok=n_tok),
        grid_spec=grid_spec,
        out_shape=jax.ShapeDtypeStruct((B, n_tok, D), bf16),
        compiler_params=_cparams(("arbitrary", "arbitrary"), MOE_VMEM_LIMIT),
        name="moe",
    )(te, nt, rows, rows, wsort.reshape(B * ngrp, R, TM), hlin, *([w_gu] * R), *([w_d] * R))


def _routed_experts(e_t, w_t, r_t, cnt, hlin, w_gu, w_d, layer):
    N = e_t.shape[2]
    pos, meta = _plan(e_t, r_t, cnt)
    te = meta[:, 0, :].reshape(-1)
    nt = meta[:, 1, 0]
    rows, wsort = _plan_invert(pos, w_t, N)
    return _moe(te, nt, rows, wsort, hlin, w_gu, w_d, layer)


def _shared_ffn(fin, shgu_ref, shd_ref):
    gu = _dot(fin, shgu_ref[...])
    return _dot((_silu(gu[:, :SH_FF]) * gu[:, SH_FF:]).astype(bf16), shd_ref[...])


def _post_ffn_final_kernel(x1_ref, routed_ref, fin_ref, shgu_ref, shd_ref, mod_ref, out_ref):
    out_ref[...] = x1_ref[...] + mod_ref[...][:, 5 * D:6 * D] * (
        routed_ref[...] + _shared_ffn(fin_ref[...], shgu_ref, shd_ref))


def _post_ffn_final(x1, routed, fin, sh_gu, sh_d, modtab):
    B, N, _ = x1.shape
    tm = ROW_TILE
    tok = lambda w: pl.BlockSpec((None, tm, w), lambda b, i: (b, i, 0))
    full = lambda r, c: pl.BlockSpec((r, c), lambda b, i: (0, 0))
    return pl.pallas_call(
        _post_ffn_final_kernel,
        grid=(B, N // tm),
        in_specs=[tok(D), tok(D), tok(D), full(D, 2 * SH_FF), full(SH_FF, D),
                  pl.BlockSpec((None, None, 1, 6 * D), lambda b, i: (b, 1, 0, 0))],
        out_specs=tok(D),
        out_shape=jax.ShapeDtypeStruct((B, N, D), f32),
        compiler_params=_cparams(("arbitrary", "arbitrary")),
        name="post_ffn_final",
    )(x1, routed, fin, sh_gu, sh_d, modtab)


def _mla_qkv(a, qan_ref, wqn_ref, wqr_ref, kvan_ref, wk_ref, wv_ref, qnn_ref, qnr_ref, knn_ref, knr_ref,
             cos_ref, sin_ref, q_ref, k_ref, v_ref):
    tm = a.shape[0]
    scale = MLA_QK ** -0.5 * float(np.log2(np.e))
    cos = cos_ref[...]
    sin = sin_ref[...]
    lane = lax.broadcasted_iota(i32, (tm, LANES), 1)
    first = (lane // (MLA_ROPE // 4)) % 2 == 0

    def rope(xb):
        sw = jnp.where(first, pltpu.roll(xb, LANES - MLA_ROPE // 4, axis=1), pltpu.roll(xb, MLA_ROPE // 4, axis=1))
        return xb * cos + sw * sin

    qa = (_rms(a[:, :MLA_QR]) * qan_ref[...]).astype(bf16)
    qn = _dot(qa, wqn_ref[...])
    yield
    qr = _dot(qa, wqr_ref[...])
    yield
    ri = lax.broadcasted_iota(i32, (LANES, LANES), 0) // MLA_ROPE
    ci = lax.broadcasted_iota(i32, (LANES, LANES), 1) // MLA_ROPE
    seg = jnp.where(ri == ci, 1.0, 0.0).astype(bf16)

    def seg_sum(sq):
        hi = sq.astype(bf16)
        r1 = sq - hi.astype(f32)
        mid = r1.astype(bf16)
        lo = (r1 - mid.astype(f32)).astype(bf16)
        return _dot(hi, seg) + _dot(mid, seg) + _dot(lo, seg)

    qr_blocks = []
    for p in range(MLA_H // 2):
        blk = qr[:, p * LANES:(p + 1) * LANES]
        blk = blk * lax.rsqrt(seg_sum(blk * blk) * (1.0 / MLA_ROPE) + EPS) * qnr_ref[:, p * LANES:(p + 1) * LANES]
        qr_blocks.append(rope(blk) * scale)
        yield

    kv = (_rms(a[:, MLA_QR:MLA_QR + MLA_KVR]) * kvan_ref[...]).astype(bf16)
    kn = _dot(kv, wk_ref[...])
    yield
    v_ref[...] = _dot_nt(wv_ref[...], kv).astype(bf16)
    yield
    kr = a[:, MLA_QR + MLA_KVR:MLA_IN_PAD]
    kr = rope(_rms(kr, MLA_ROPE) * knr_ref[...])
    kr_odd = pltpu.roll(kr, MLA_ROPE, axis=1)
    for hd in range(MLA_H):
        sl = slice(hd * MLA_NOPE, (hd + 1) * MLA_NOPE)
        q_ref[:, 2 * hd * LANES:(2 * hd + 1) * LANES] = (_rms(qn[:, sl]) * qnn_ref[...] * scale).astype(bf16)
        q_ref[:, (2 * hd + 1) * LANES:(2 * hd + 2) * LANES] = qr_blocks[hd // 2].astype(bf16)
        k_ref[:, 2 * hd * LANES:(2 * hd + 1) * LANES] = (_rms(kn[:, sl]) * knn_ref[...]).astype(bf16)
        k_ref[:, (2 * hd + 1) * LANES:(2 * hd + 2) * LANES] = (kr if hd % 2 == 0 else kr_odd).astype(bf16)
        if hd % 2 == 1:
            yield


def _ffn_mla_kernel(x1_ref, routed_ref, fin_ref, shgu_ref, shd_ref, mod0_ref, mod1_ref, g_ref, win_ref, *refs, nt):
    qkv_refs, (x2_ref, q_ref, k_ref, v_ref, a0_ref, a1_ref) = refs[:12], refs[12:]
    a_refs = (a0_ref, a1_ref)
    i = pl.program_id(1)

    def qkv(p):
        return _mla_qkv(a_refs[p][...], *qkv_refs, q_ref, k_ref, v_ref)

    def combine(p, other=()):
        other = iter(other)

        def advance(n):
            for _ in range(n):
                next(other, None)

        gu = _dot(fin_ref[...], shgu_ref[...])
        advance(3)
        shared = _dot((_silu(gu[:, :SH_FF]) * gu[:, SH_FF:]).astype(bf16), shd_ref[...])
        advance(3)
        x2 = x1_ref[...] + mod0_ref[...][:, 5 * D:6 * D] * (routed_ref[...] + shared)
        x2_ref[...] = x2
        advance(2)
        mod1 = mod1_ref[...]
        h = (_rms(x2) * g_ref[...]) * (1.0 + mod1[:, D:2 * D]) + mod1[:, 0:D]
        advance(2)
        a_refs[p][...] = _dot(h.astype(bf16), win_ref[...])
        for _ in other:
            pass

    pl.when(i == 0)(functools.partial(combine, 0))
    for p in range(2):
        @pl.when(jnp.logical_and(jnp.logical_and(i >= 1, i < nt), i % 2 == p))
        def _(p=p):
            combine(p, qkv(1 - p))

    @pl.when(i == nt)
    def _():
        for _ in qkv((nt - 1) % 2):
            pass


def _ffn_mla(x1, routed, fin, sh_gu, sh_d, modtab0, modtab1, gain, w_in, qkv_params, cos_t, sin_t, n_ctx_tiles):
    B, T, _ = x1.shape
    tm = ROW_TILE
    nt = T // tm
    last = lambda i: jnp.minimum(i, nt - 1)
    prev = lambda i: jnp.maximum(i - 1, 0)
    tok = lambda w: pl.BlockSpec((None, tm, w), lambda b, i: (b, last(i), 0))
    modspec = lambda: pl.BlockSpec((None, None, 1, 6 * D),
                                   lambda b, i: (b, jnp.where(last(i) < n_ctx_tiles, 0, 1), 0, 0))
    full = lambda r, c: pl.BlockSpec((r, c), lambda b, i: (0, 0))
    rope_spec = lambda: pl.BlockSpec((tm, LANES), lambda b, i: (prev(i), 0))
    hw = 2 * LANES * MLA_H
    return pl.pallas_call(
        functools.partial(_ffn_mla_kernel, nt=nt),
        grid=(B, nt + 1),
        in_specs=[tok(D), tok(D), tok(D), full(D, 2 * SH_FF), full(SH_FF, D), modspec(), modspec(),
                  full(1, D), full(D, MLA_IN_PAD),
                  full(1, MLA_QR), full(MLA_QR, MLA_H * MLA_NOPE), full(MLA_QR, MLA_H * MLA_ROPE),
                  full(1, MLA_KVR), full(MLA_KVR, MLA_H * MLA_NOPE), full(MLA_H * MLA_V, MLA_KVR),
                  full(1, MLA_NOPE), full(1, MLA_H * MLA_ROPE), full(1, MLA_NOPE), full(1, LANES),
                  rope_spec(), rope_spec()],
        out_specs=[tok(D),
                   pl.BlockSpec((None, tm, hw), lambda b, i: (b, jnp.maximum(prev(i) - n_ctx_tiles, 0), 0)),
                   pl.BlockSpec((None, tm, hw), lambda b, i: (b, prev(i), 0)),
                   pl.BlockSpec((None, MLA_H * MLA_V, tm), lambda b, i: (b, 0, prev(i)))],
        out_shape=[jax.ShapeDtypeStruct((B, T, D), f32),
                   jax.ShapeDtypeStruct((B, T - n_ctx_tiles * tm, hw), bf16), jax.ShapeDtypeStruct((B, T, hw), bf16),
                   jax.ShapeDtypeStruct((B, MLA_H * MLA_V, T), bf16)],
        scratch_shapes=[pltpu.VMEM((tm, MLA_IN_PAD), f32), pltpu.VMEM((tm, MLA_IN_PAD), f32)],
        compiler_params=_cparams(("arbitrary", "arbitrary")),
        name="ffn_mla",
    )(x1, routed, fin, sh_gu, sh_d, modtab0, modtab1, gain, w_in, *qkv_params, cos_t, sin_t)


ATT_TQ = 256


def _mla_attn_kernel(q_ref, k_ref, vt_ref, o_ref, s0_ref, s1_ref, m0_ref, m1_ref):
    tq = q_ref.shape[0]
    groups = k_ref.shape[0] // SUBLANES
    i = pl.program_id(0)
    s_refs, m_refs = (s0_ref, s1_ref), (m0_ref, m1_ref)

    @pl.when(i == 0)
    def _():
        for ref in s_refs + m_refs:
            ref[...] = jnp.zeros_like(ref)

    def stage(par):
        s = _dot_nt(k_ref[...], q_ref[...])
        s_refs[par][...] = s
        m_refs[par][...] = jnp.max(s.reshape(groups, SUBLANES, tq), axis=0)
        m = jnp.max(m_refs[1 - par][...], axis=0, keepdims=True)
        p = jnp.exp2(s_refs[1 - par][...] - m)
        lsum = jnp.sum(jnp.sum(p.reshape(groups, SUBLANES, tq), axis=0), axis=0, keepdims=True)
        acc = _dot(vt_ref[...], p.astype(bf16))
        o_ref[...] = (acc / lsum).astype(bf16)

    pl.when(i % 2 == 0)(functools.partial(stage, 0))
    pl.when(i % 2 == 1)(functools.partial(stage, 1))


def _mla_attn(q, k, vt):
    B, S, _ = q.shape
    T = k.shape[1]
    tq = ATT_TQ
    nq = S // tq
    ntile = B * MLA_H * nq
    assert S % tq == 0 and T % SUBLANES == 0

    def tile(g):
        g = jnp.clip(g, 0, ntile - 1)
        return g // (MLA_H * nq), (g // nq) % MLA_H, g % nq

    def q_index(g):
        b, h, i = tile(g)
        return b, i, h

    def k_index(g):
        b, h, _ = tile(g)
        return b, 0, h

    def vt_index(g):
        b, h, _ = tile(g - 1)
        return b, h, 0

    def o_index(g):
        b, h, i = tile(g - 1)
        return b, h, i

    return pl.pallas_call(
        _mla_attn_kernel,
        grid=(ntile + 1,),
        in_specs=[pl.BlockSpec((None, tq, 2 * LANES), q_index),
                  pl.BlockSpec((None, T, 2 * LANES), k_index),
                  pl.BlockSpec((None, MLA_V, T), vt_index)],
        out_specs=pl.BlockSpec((None, MLA_V, tq), o_index),
        out_shape=jax.ShapeDtypeStruct((B, MLA_H * MLA_V, S), bf16),
        scratch_shapes=[pltpu.VMEM((T, tq), f32), pltpu.VMEM((T, tq), f32),
                        pltpu.VMEM((SUBLANES, tq), f32), pltpu.VMEM((SUBLANES, tq), f32)],
        compiler_params=_cparams(("arbitrary",)),
        name="mla_attn",
    )(q, k, vt)


def _axial_angles(rows_n, rot_dim):
    axis_dim = rot_dim // 2
    inv = ROPE_BASE ** (-jnp.arange(0, axis_dim, 2, dtype=f32) / axis_dim)
    row = jnp.repeat(jnp.arange(rows_n, dtype=f32), GRID_W)
    col = jnp.tile(jnp.arange(GRID_W, dtype=f32), rows_n)
    return row[:, None] * inv, col[:, None] * inv


def _rope_tables(seq, n_ctx, rot_dim, reps):
    ang_r, ang_c = _axial_angles(seq // GRID_W, rot_dim)
    cos = jnp.concatenate([jnp.cos(ang_r)] * 2 + [jnp.cos(ang_c)] * 2, axis=1)
    sin = jnp.concatenate([-jnp.sin(ang_r), jnp.sin(ang_r), -jnp.sin(ang_c), jnp.sin(ang_c)], axis=1)
    cos = jnp.concatenate([jnp.ones((n_ctx, rot_dim), f32), cos], axis=0)
    sin = jnp.concatenate([jnp.zeros((n_ctx, rot_dim), f32), sin], axis=0)
    return jnp.tile(cos, (1, reps)), jnp.tile(sin, (1, reps))


def kernel(x, c, ctx, c_ctx, ada_w, ada_b, norm_mix, norm_ffn, ret_w_in, ret_decay_f, ret_decay_b, ret_w_o,
           mla_w_in, mla_q_a_norm, mla_w_q_b, mla_kv_a_norm, mla_w_kv_b, mla_q_norm, mla_k_norm, mla_w_o,
           router_w, router_bias, exp_w_gu, exp_w_down, sh_w_gu, sh_w_down):
    B, S, _ = x.shape
    n_ctx = ctx.shape[1]
    assert n_ctx % ROW_TILE == 0 and S % ROW_TILE == 0 and S % GRID_W == 0
    n_ctx_tiles = n_ctx // ROW_TILE

    rows = -(-(B + 1) // SUBLANES) * SUBLANES
    cc = jnp.zeros((rows, D), f32).at[:B].set(c).at[B].set(c_ctx)
    mod = _ada(cc, ada_w, ada_b)

    def modtab(i):
        ctx_row = jnp.broadcast_to(mod[i, B][None, :], (B, 6 * D))
        return jnp.stack([ctx_row, mod[i, :B]], axis=1)[:, :, None, :]

    mod0, mod1 = modtab(0), modtab(1)

    cos_r, sin_r = _rope_tables(S, n_ctx, RET_DK, 1)
    riders = (exp_w_gu.reshape(-1, exp_w_gu.shape[-1]), exp_w_down.reshape(-1, exp_w_down.shape[-1]))
    steps = _cast_steps(B, (n_ctx + S) // ROW_TILE)
    ride = all(r.shape[0] % (steps * 2 * SUBLANES) == 0 and r.size // steps * 4 <= RIDER_SLICE_BYTES for r in riders)
    proj = _ret_inproj(ctx, x, mod0, norm_mix[0][None, :], ret_w_in[0].astype(bf16), cos_r, sin_r, n_ctx_tiles,
                       riders if ride else ())
    q, k, v, gf, gb = proj[:5]
    if ride:
        exp_gu, exp_d = proj[5].reshape(exp_w_gu.shape), proj[6].reshape(exp_w_down.shape)
    else:
        exp_gu, exp_d = exp_w_gu.astype(bf16), exp_w_down.astype(bf16)
    dtab = jnp.broadcast_to(jnp.concatenate([ret_decay_f[0], ret_decay_b[0]])[:, None], (2 * RET_H, LANES))
    o = _ret_scan(dtab, q, k, v, gf, gb, n_ctx)
    x1, fin, hlin, e_t, w_t, r_t, cnt = _post_mix(
        o, ret_w_o[0].astype(bf16), (ctx, x), 0, mod0, n_ctx_tiles, norm_ffn[0][None, :],
        router_w[0].T, router_bias[0][:, None])
    routed = _routed_experts(e_t, w_t, r_t, cnt, hlin, exp_gu, exp_d, 0)
    w_in1 = jnp.zeros((D, MLA_IN_PAD), f32).at[:, :mla_w_in.shape[2]].set(mla_w_in[0]).astype(bf16)

    wq = mla_w_q_b[0].reshape(MLA_QR, MLA_H, MLA_QK)
    wqn = wq[:, :, :MLA_NOPE].reshape(MLA_QR, MLA_H * MLA_NOPE).astype(bf16)
    wqr = wq[:, :, MLA_NOPE:].reshape(MLA_QR, MLA_H * MLA_ROPE).astype(bf16)
    wkv = mla_w_kv_b[0].reshape(MLA_KVR, MLA_H, MLA_NOPE + MLA_V)
    wk = wkv[:, :, :MLA_NOPE].reshape(MLA_KVR, MLA_H * MLA_NOPE).astype(bf16)
    wv = wkv[:, :, MLA_NOPE:].reshape(MLA_KVR, MLA_H * MLA_V).T.astype(bf16)
    qnn = mla_q_norm[0][None, :MLA_NOPE]
    qnr = jnp.tile(mla_q_norm[0][None, MLA_NOPE:], (1, MLA_H))
    knn = mla_k_norm[0][None, :MLA_NOPE]
    knr = jnp.concatenate([mla_k_norm[0][MLA_NOPE:], jnp.zeros((LANES - MLA_ROPE,), f32)])[None, :]
    cos_m, sin_m = _rope_tables(S, n_ctx, MLA_ROPE, LANES // MLA_ROPE)
    qkv_params = (mla_q_a_norm[0][None, :], wqn, wqr, mla_kv_a_norm[0][None, :], wk, wv, qnn, qnr, knn, knr)
    x2, qf, kf, vf = _ffn_mla(x1, routed, fin, sh_w_gu[0].astype(bf16), sh_w_down[0].astype(bf16), mod0, mod1,
                              norm_mix[1][None, :], w_in1, qkv_params, cos_m, sin_m, n_ctx_tiles)
    o1 = _mla_attn(qf, kf, vf)
    x3, fin1, hlin1, e1, w1, r1, cnt1 = _post_mix(
        o1, mla_w_o[0].astype(bf16), (x2,), n_ctx_tiles, mod1, 0, norm_ffn[1][None, :],
        router_w[1].T, router_bias[1][:, None], o_transposed=True)
    routed1 = _routed_experts(e1, w1, r1, cnt1, hlin1, exp_gu, exp_d, 1)
    return _post_ffn_final(x3, routed1, fin1, sh_w_gu[1].astype(bf16), sh_w_down[1].astype(bf16), mod1)
```

```python
import dataclasses
import functools

import jax
import jax.numpy as jnp
import numpy as np
from jax import lax
from jax.experimental import pallas as pl
from jax.experimental.pallas import tpu as pltpu
from jax.experimental.pallas import tpu_sc as plsc

f32 = jnp.float32
bf16 = jnp.bfloat16
i32 = jnp.int32

D = 1024
GRID_W = 64
EPS = 1e-6
ROPE_BASE = 10000.0
RET_H = 4
RET_DK = 256
RET_DV = 512
RET_VW = RET_H * RET_DV
RET_CHUNK = 256
MLA_H = 8
MLA_NOPE = 128
MLA_ROPE = 64
MLA_QK = MLA_NOPE + MLA_ROPE
MLA_V = 128
MLA_QR = 384
MLA_KVR = 256
MLA_IN_PAD = 768
N_EXP = 64
TOP_K = 8
N_GRP = 8
TOPK_GRP = 4
EXP_FF = 256
SH_FF = 256
ROUTED_SCALE = 2.5

LANES = 128
SUBLANES = 8
SC_LANES = 16
MXU_N = 256
ROW_TILE = 256
MOE_TILE = 256
MOE_GROUP = 2
ADA_COLS = 1536
PROJ_COLS = 512
VMEM_LIMIT = 56 * 1024 * 1024
MOE_VMEM_LIMIT = 62 * 1024 * 1024


def _cparams(sem, vmem=VMEM_LIMIT):
    return pltpu.CompilerParams(dimension_semantics=sem, vmem_limit_bytes=vmem)


def _sigmoid(x):
    return 1.0 / (1.0 + jnp.exp(-x))


def _silu(x):
    return x * _sigmoid(x)


def _rms(x, n=None):
    n = x.shape[-1] if n is None else n
    return x * lax.rsqrt(jnp.sum(x * x, axis=-1, keepdims=True) * (1.0 / n) + EPS)


def _dot(a, b):
    return jnp.dot(a, b, preferred_element_type=f32)


def _dot_nt(a, b):
    return lax.dot_general(a, b, (((1,), (1,)), ((), ())), preferred_element_type=f32)


def _ada_kernel(c_ref, w_ref, b_ref, o_ref):
    s = _silu(c_ref[...]).astype(bf16)
    o_ref[...] = _dot(s, w_ref[...].astype(bf16)) + b_ref[...]


def _ada(cc, ada_w, ada_b):
    depth = ada_w.shape[0]
    rows = cc.shape[0]
    tn = ADA_COLS
    return pl.pallas_call(
        _ada_kernel,
        grid=(depth, 6 * D // tn),
        in_specs=[pl.BlockSpec((rows, D), lambda i, j: (0, 0)),
                  pl.BlockSpec((None, D, tn), lambda i, j: (i, 0, j)),
                  pl.BlockSpec((None, 1, tn), lambda i, j: (i, 0, j))],
        out_specs=pl.BlockSpec((None, rows, tn), lambda i, j: (i, 0, j)),
        out_shape=jax.ShapeDtypeStruct((depth, rows, 6 * D), f32),
        compiler_params=_cparams(("arbitrary", "arbitrary")),
        name="ada",
    )(cc, ada_w, ada_b.reshape(depth, 1, 6 * D))


def _stream_tile(ctx_ref, x_ref, n_ctx_tiles):
    return jnp.where(pl.program_id(1) < n_ctx_tiles, ctx_ref[...], x_ref[...])


def _stream_specs(tm, n_ctx_tiles, last=None):
    clamp = (lambda i: i) if last is None else (lambda i: jnp.minimum(i, last))
    return [pl.BlockSpec((None, tm, D), lambda b, i: (b, jnp.minimum(clamp(i), n_ctx_tiles - 1), 0)),
            pl.BlockSpec((None, tm, D), lambda b, i: (b, jnp.maximum(clamp(i) - n_ctx_tiles, 0), 0))]


def _ret_inproj_kernel(ctx_ref, x_ref, mod_ref, g_ref, w_ref, cos_ref, sin_ref, *refs, n_ctx_tiles, n_cast):
    cast_in, cast_out = refs[:n_cast], refs[n_cast + 5:]
    q_ref, k_ref, v_ref, gf_ref, gb_ref = refs[n_cast:n_cast + 5]
    x = _stream_tile(ctx_ref, x_ref, n_ctx_tiles)
    mod = mod_ref[...]
    h = (_rms(x) * g_ref[...]) * (1.0 + mod[:, D:2 * D]) + mod[:, 0:D]
    hb = h.astype(bf16)
    cos = cos_ref[...]
    sin = sin_ref[...]

    def rope(a):
        outs = []
        for half in range(2):
            sl = slice(half * LANES, (half + 1) * LANES)
            ah = a[:, sl]
            outs.append(ah * cos[:, sl] + pltpu.roll(ah, LANES // 2, axis=1) * sin[:, sl])
        return jnp.concatenate(outs, axis=1)

    for hd in range(RET_H):
        sl = slice(hd * RET_DK, (hd + 1) * RET_DK)
        q_ref[:, sl] = rope(_dot(hb, w_ref[:, sl])).astype(bf16)
    if n_cast:
        (gu_ref, dn_ref), (wcat_ref,) = cast_in, cast_out
        wcat_ref[0:D, :] = gu_ref[...].astype(bf16)
        dn = dn_ref[...]
        wcat_ref[D:D + EXP_FF, :] = dn[:, :2 * EXP_FF].astype(bf16)
        wcat_ref[D + EXP_FF:D + 2 * EXP_FF, :] = dn[:, 2 * EXP_FF:].astype(bf16)
    for hd in range(RET_H):
        sl = slice(hd * RET_DK, (hd + 1) * RET_DK)
        wsl = slice(D + hd * RET_DK, D + (hd + 1) * RET_DK)
        k_ref[:, sl] = (rope(_dot(hb, w_ref[:, wsl])) * (RET_DK ** -0.5)).astype(bf16)
    cw = PROJ_COLS
    for c in range(RET_VW // cw):
        sl = slice(c * cw, (c + 1) * cw)
        v_ref[:, sl] = _dot(hb, w_ref[:, 2 * D + c * cw:2 * D + (c + 1) * cw]).astype(bf16)
        gf_ref[:, sl] = _silu(_dot(hb, w_ref[:, 2 * D + RET_VW + c * cw:2 * D + RET_VW + (c + 1) * cw])).astype(bf16)
        gb_ref[:, sl] = _silu(_dot(hb, w_ref[:, 2 * D + 2 * RET_VW + c * cw:2 * D + 2 * RET_VW + (c + 1) * cw])).astype(bf16)


def _cast_steps(batch, n_tiles):
    return batch * max(n_tiles - 1, 1)


def _packed_expert_shape(w_gu):
    n_l, n_e = w_gu.shape[:2]
    return n_l, n_e, D + 2 * EXP_FF, 2 * EXP_FF


def _ret_inproj(ctx, x, modtab, gain, w_in, cos_t, sin_t, n_ctx_tiles, riders=()):
    B = x.shape[0]
    T = ctx.shape[1] + x.shape[1]
    tm = ROW_TILE
    nt = T // tm
    n_in = w_in.shape[1]
    tok = lambda w: pl.BlockSpec((None, tm, w), lambda b, i: (b, i, 0))
    per_b = _cast_steps(B, nt) // B
    rider_out = []
    if riders:
        n_l, n_e = riders[0].shape[:2]
        assert n_l * n_e == per_b * B and D == 4 * EXP_FF
        rider_out = [jax.ShapeDtypeStruct(_packed_expert_shape(riders[0]), bf16)]

    def rider_spec(shape):
        def index(b, i):
            s = b * per_b + jnp.minimum(i, per_b - 1)
            return s // n_e, s % n_e, 0, 0
        return pl.BlockSpec((None, None) + tuple(shape[2:]), index)

    return pl.pallas_call(
        functools.partial(_ret_inproj_kernel, n_ctx_tiles=n_ctx_tiles, n_cast=len(riders)),
        grid=(B, nt),
        in_specs=_stream_specs(tm, n_ctx_tiles) + [
                  pl.BlockSpec((None, None, 1, 6 * D), lambda b, i: (b, jnp.where(i < n_ctx_tiles, 0, 1), 0, 0)),
                  pl.BlockSpec((1, D), lambda b, i: (0, 0)),
                  pl.BlockSpec((D, n_in), lambda b, i: (0, 0), pipeline_mode=pl.Buffered(1)),
                  pl.BlockSpec((tm, RET_DK), lambda b, i: (i, 0)),
                  pl.BlockSpec((tm, RET_DK), lambda b, i: (i, 0))] + [rider_spec(r.shape) for r in riders],
        out_specs=[tok(D), tok(D), tok(RET_VW), tok(RET_VW), tok(RET_VW)] + [rider_spec(r.shape) for r in rider_out],
        out_shape=[jax.ShapeDtypeStruct((B, T, D), bf16), jax.ShapeDtypeStruct((B, T, D), bf16),
                   jax.ShapeDtypeStruct((B, T, RET_VW), bf16), jax.ShapeDtypeStruct((B, T, RET_VW), bf16),
                   jax.ShapeDtypeStruct((B, T, RET_VW), bf16)] + rider_out,
        compiler_params=_cparams(("arbitrary", "arbitrary")),
        name="ret_inproj",
    )(ctx, x, modtab, gain, w_in, cos_t, sin_t, *riders)


def _ret_chunk_index(t, nc, ncc):
    u = t - nc
    back = jnp.where(u < ncc, ncc - 1 - u, nc - 1 - u + ncc)
    return jnp.where(t < nc, t, back)


def _ret_scan_kernel(dt_ref, q_ref, k_ref, v_ref, gf_ref, gb_ref, o_ref,
                     s_ref, of_ref, mask_ref, dq_ref, dk_ref, dc_ref, *, nc, ncc):
    t = pl.program_id(1)
    C = RET_CHUNK

    def init(direction):
        s_ref[...] = jnp.zeros_like(s_ref)
        ii = lax.broadcasted_iota(i32, (C, C), 0)
        jj = lax.broadcasted_iota(i32, (C, C), 1)
        rel = (ii - jj if direction == 0 else jj - ii).astype(f32)
        pos = lax.broadcasted_iota(i32, (C, 1), 0).astype(f32)
        for hd in range(RET_H):
            r = direction * RET_H + hd
            lg = -jnp.exp(dt_ref[r:r + 1, :])
            lg1 = lg[:, 0:1]
            mask_ref[hd] = jnp.where(rel >= 0, jnp.exp(lg1 * jnp.maximum(rel, 0.0)), 0.0)
            if direction == 0:
                dq_ref[hd] = jnp.exp(lg1 * (pos + 1.0))
                dk_ref[hd] = jnp.exp(lg1 * (C - 1.0 - pos))
            else:
                dq_ref[hd] = jnp.exp(lg1 * (C - pos))
                dk_ref[hd] = jnp.exp(lg1 * pos)
            dc_ref[hd] = jnp.exp(lg * float(C))

    pl.when(t == 0)(functools.partial(init, 0))
    pl.when(t == nc)(functools.partial(init, 1))

    row0 = pl.multiple_of(_ret_chunk_index(t, nc, ncc) * C, C)

    def step(forward):
        for hd in range(RET_H):
            ks = slice(hd * RET_DK, (hd + 1) * RET_DK)
            vs = slice(hd * RET_DV, (hd + 1) * RET_DV)
            qh = q_ref[:, ks]
            kh = k_ref[:, ks]
            vh = v_ref[:, vs]
            p = (_dot_nt(qh, kh) * mask_ref[hd]).astype(bf16)
            y = _dot(p, vh) + _dot(qh, s_ref[hd].astype(bf16)) * dq_ref[hd]
            kd = (kh.astype(f32) * dk_ref[hd]).astype(bf16)
            upd = lax.dot_general(kd, vh, (((0,), (0,)), ((), ())), preferred_element_type=f32)
            s_ref[hd] = s_ref[hd] * dc_ref[hd][0:1, 0:1] + upd
            yn = _rms(y)
            if forward:
                of_ref[pl.ds(row0, C), vs] = (gf_ref[:, vs].astype(f32) * yn).astype(bf16)
            else:
                o_ref[:, vs] = (of_ref[pl.ds(row0, C), vs].astype(f32) + gb_ref[:, vs].astype(f32) * yn).astype(bf16)

    pl.when(t < nc)(functools.partial(step, True))
    pl.when(t >= nc)(functools.partial(step, False))


def _ret_scan(dtab, q, k, v, gf, gb, n_ctx):
    B, T, _ = q.shape
    C = RET_CHUNK
    nc = T // C
    ncc = n_ctx // C
    cidx = functools.partial(_ret_chunk_index, nc=nc, ncc=ncc)
    first_back = ncc - 1
    return pl.pallas_call(
        functools.partial(_ret_scan_kernel, nc=nc, ncc=ncc),
        grid=(B, 2 * nc),
        in_specs=[pl.BlockSpec((2 * RET_H, LANES), lambda b, t: (0, 0)),
                  pl.BlockSpec((None, C, D), lambda b, t: (b, cidx(t), 0)),
                  pl.BlockSpec((None, C, D), lambda b, t: (b, cidx(t), 0)),
                  pl.BlockSpec((None, C, RET_VW), lambda b, t: (b, cidx(t), 0)),
                  pl.BlockSpec((None, C, RET_VW), lambda b, t: (b, jnp.where(t < nc, t, nc - 1), 0)),
                  pl.BlockSpec((None, C, RET_VW), lambda b, t: (b, jnp.where(t < nc, first_back, cidx(t)), 0))],
        out_specs=pl.BlockSpec((None, C, RET_VW), lambda b, t: (b, jnp.where(t < nc, first_back, cidx(t)), 0)),
        out_shape=jax.ShapeDtypeStruct((B, T, RET_VW), bf16),
        scratch_shapes=[pltpu.VMEM((RET_H, RET_DK, RET_DV), f32),
                        pltpu.VMEM((T, RET_VW), bf16),
                        pltpu.VMEM((RET_H, C, C), f32),
                        pltpu.VMEM((RET_H, C, 1), f32),
                        pltpu.VMEM((RET_H, C, 1), f32),
                        pltpu.VMEM((RET_H, 1, LANES), f32)],
        compiler_params=_cparams(("arbitrary", "arbitrary")),
        name="ret_scan",
    )(dtab, q, k, v, gf, gb)


def _route(f, rwt_ref, rb_ref, cnt_ref, e_ref, w_ref, r_ref):
    tm = f.shape[0]
    G = N_EXP // N_GRP
    logits = _dot_nt(rwt_ref[...].astype(bf16), f.astype(bf16))
    s = _sigmoid(logits)
    sel = s + rb_ref[...]
    mi = lax.broadcasted_iota(i32, (G, tm), 0)
    neg = -jnp.inf
    s_g = [s[g * G:(g + 1) * G, :] for g in range(N_GRP)]
    sel_g = [sel[g * G:(g + 1) * G, :] for g in range(N_GRP)]

    def first_max(a, ids, big):
        mx = jnp.max(a, axis=0, keepdims=True)
        ix = jnp.min(jnp.where(a == mx, ids, big), axis=0, keepdims=True)
        return mx, ix

    gscore = jnp.zeros((N_GRP, tm), f32)
    gi = lax.broadcasted_iota(i32, (N_GRP, tm), 0)
    for g in range(N_GRP):
        t1, i1 = first_max(sel_g[g], mi, G)
        t2 = jnp.max(jnp.where(mi == i1, neg, sel_g[g]), axis=0, keepdims=True)
        gscore = jnp.where(gi == g, t1 + t2, gscore)
    yield
    gmask = jnp.zeros((N_GRP, tm), i32)
    cur = gscore
    for _ in range(TOPK_GRP):
        _, ix = first_max(cur, gi, N_GRP)
        hit = gi == ix
        gmask = jnp.where(hit, 1, gmask)
        cur = jnp.where(hit, neg, cur)
    cand = [jnp.where(gmask[g:g + 1, :] > 0, sel_g[g], neg) for g in range(N_GRP)]
    ids = [mi + g * G for g in range(N_GRP)]

    def across(parts, op):
        acc = parts[0]
        for part in parts[1:]:
            acc = op(acc, part)
        return acc

    e_rows, w_rows = [], []
    for _ in range(TOP_K):
        mx = jnp.max(across(cand, jnp.maximum), axis=0, keepdims=True)
        ix = jnp.min(across([jnp.where(cand[g] == mx, ids[g], N_EXP) for g in range(N_GRP)], jnp.minimum),
                     axis=0, keepdims=True)
        hits = [ids[g] == ix for g in range(N_GRP)]
        cand = [jnp.where(hits[g], neg, cand[g]) for g in range(N_GRP)]
        wv = jnp.sum(across([jnp.where(hits[g], s_g[g], 0.0) for g in range(N_GRP)], jnp.add), axis=0, keepdims=True)
        e_rows.append(ix)
        w_rows.append(wv)
        yield
    wsum = w_rows[0]
    for r in range(1, TOP_K):
        wsum = wsum + w_rows[r]

    selm = [jnp.zeros((G, tm), f32) for _ in range(N_GRP)]
    for r in range(TOP_K):
        for g in range(N_GRP):
            selm[g] = jnp.where(ids[g] == e_rows[r], 1.0, selm[g])
    m_all = jnp.concatenate(selm, axis=0)
    ri = lax.broadcasted_iota(i32, (tm, tm), 0)
    ci = lax.broadcasted_iota(i32, (tm, tm), 1)
    upper = jnp.where(ri <= ci, 1.0, 0.0).astype(bf16)
    incl = _dot(m_all.astype(bf16), upper)
    carry = cnt_ref[:, 0:1]
    rank_all = carry + incl - m_all
    cnt_ref[...] = cnt_ref[...] + incl[:, tm - 1:tm]
    yield
    for r in range(TOP_K):
        rk = jnp.sum(across([jnp.where(ids[g] == e_rows[r], rank_all[g * G:(g + 1) * G, :], 0.0)
                             for g in range(N_GRP)], jnp.add), axis=0, keepdims=True)
        e_ref[r:r + 1, :] = e_rows[r]
        w_ref[r:r + 1, :] = w_rows[r] / wsum * ROUTED_SCALE
        r_ref[r:r + 1, :] = rk.astype(i32)


def _post_mix_kernel(o_ref, wo_ref, *refs, nt, o_transposed, split_ctx_tiles):
    n_resid = 2 if split_ctx_tiles else 1
    resid = refs[:n_resid]
    (mod_ref, g_ref, rwt_ref, rb_ref, x1_ref, fin_ref, hlin_ref, e_ref, w_ref, r_ref, cnt_ref,
     f0_ref, f1_ref) = refs[n_resid:]
    f_refs = (f0_ref, f1_ref)
    i = pl.program_id(1)

    @pl.when(i == 0)
    def _():
        cnt_ref[...] = jnp.zeros_like(cnt_ref)

    def route(p):
        return _route(f_refs[p][...], rwt_ref, rb_ref, cnt_ref, e_ref, w_ref, r_ref)

    def mix(p, other=()):
        other = iter(other)

        def advance(n):
            for _ in range(n):
                next(other, None)

        mod = mod_ref[...]
        x = _stream_tile(resid[0], resid[1], split_ctx_tiles) if split_ctx_tiles else resid[0][...]
        o = o_ref[...]
        cols = []
        for c in range(D // MXU_N):
            sl = slice(c * MXU_N, (c + 1) * MXU_N)
            if o_transposed:
                piece = lax.dot_general(o, wo_ref[:, sl], (((0,), (0,)), ((), ())), preferred_element_type=f32)
            else:
                piece = _dot(o, wo_ref[:, sl])
            cols.append(x[:, sl] + mod[:, 2 * D + c * MXU_N:2 * D + (c + 1) * MXU_N] * piece)
            advance(2)
        x1 = jnp.concatenate(cols, axis=1)
        x1_ref[...] = x1
        f = (_rms(x1) * g_ref[...]) * (1.0 + mod[:, 4 * D:5 * D]) + mod[:, 3 * D:4 * D]
        advance(2)
        fin_ref[...] = f.astype(bf16)
        tm = f.shape[0]
        for j in range(D // LANES):
            hlin_ref[pl.ds(j, tm, stride=D // LANES), :] = f[:, j * LANES:(j + 1) * LANES]
        f_refs[p][...] = f
        for _ in other:
            pass

    pl.when(i == 0)(functools.partial(mix, 0))
    for p in range(2):
        @pl.when(jnp.logical_and(jnp.logical_and(i >= 1, i < nt), i % 2 == p))
        def _(p=p):
            mix(p, route(1 - p))

    @pl.when(i == nt)
    def _():
        for _ in route((nt - 1) % 2):
            pass
        hlin_ref[...] = jnp.zeros_like(hlin_ref)


def _post_mix(o, w_o, resid, x_tile_off, modtab, n_ctx_tiles, gain, rwt, rb, o_transposed=False):
    if o_transposed:
        B, KO, N = o.shape
    else:
        B, N, KO = o.shape
    tm = ROW_TILE
    nt = N // tm
    last = lambda i: jnp.minimum(i, nt - 1)
    tok = lambda w: pl.BlockSpec((None, tm, w), lambda b, i: (b, last(i), 0))
    sel = lambda: pl.BlockSpec((None, TOP_K, tm), lambda b, i: (b, 0, jnp.maximum(i - 1, 0)))
    o_spec = (pl.BlockSpec((None, KO, tm), lambda b, i: (b, 0, last(i))) if o_transposed else tok(KO))
    split = len(resid) == 2
    resid_specs = (_stream_specs(tm, n_ctx_tiles, nt - 1) if split else
                   [pl.BlockSpec((None, tm, D), lambda b, i: (b, last(i) + x_tile_off, 0))])
    return pl.pallas_call(
        functools.partial(_post_mix_kernel, nt=nt, o_transposed=o_transposed,
                          split_ctx_tiles=n_ctx_tiles if split else 0),
        grid=(B, nt + 1),
        in_specs=[o_spec,
                  pl.BlockSpec((KO, D), lambda b, i: (0, 0))] + resid_specs + [
                  pl.BlockSpec((None, None, 1, 6 * D), lambda b, i: (b, jnp.where(last(i) < n_ctx_tiles, 0, 1), 0, 0)),
                  pl.BlockSpec((1, D), lambda b, i: (0, 0)),
                  pl.BlockSpec((N_EXP, D), lambda b, i: (0, 0)),
                  pl.BlockSpec((N_EXP, 1), lambda b, i: (0, 0))],
        out_specs=[tok(D), tok(D),
                   pl.BlockSpec((None, tm * (D // LANES), LANES), lambda b, i: (b, i, 0)),
                   sel(), sel(), sel(),
                   pl.BlockSpec((None, N_EXP, LANES), lambda b, i: (b, 0, 0))],
        out_shape=[jax.ShapeDtypeStruct((B, N, D), f32), jax.ShapeDtypeStruct((B, N, D), bf16),
                   jax.ShapeDtypeStruct((B, (N + tm) * (D // LANES), LANES), f32),
                   jax.ShapeDtypeStruct((B, TOP_K, N), i32), jax.ShapeDtypeStruct((B, TOP_K, N), f32),
                   jax.ShapeDtypeStruct((B, TOP_K, N), i32),
                   jax.ShapeDtypeStruct((B, N_EXP, LANES), f32)],
        scratch_shapes=[pltpu.VMEM((tm, D), f32), pltpu.VMEM((tm, D), f32)],
        compiler_params=_cparams(("arbitrary", "arbitrary")),
        name="post_mix",
    )(o, w_o, *resid, modtab, gain, rwt, rb)


META_W = 256
PLAN_ALIGN = 1024


def _round_up(n, m):
    return -(-n // m) * m


def _moe_sizes(n_tok):
    tm = MOE_TILE
    nt_max = (n_tok * TOP_K + N_EXP * (tm - 1)) // tm + 1
    ntp = _round_up(nt_max + MOE_GROUP, int(np.lcm(PLAN_ALIGN // tm, MOE_GROUP)))
    assert ntp <= META_W
    return ntp, _round_up(n_tok, PLAN_ALIGN)


def _plan_kernel(e_ref, r_ref, cnt_ref, pos_ref, meta_ref, *, n_tok):
    tm = MOE_TILE
    ntile = jnp.floor((cnt_ref[...] + (tm - 1.0)) * (1.0 / tm))
    ntb = ntile.astype(bf16)
    ei = lax.broadcasted_iota(i32, (N_EXP, LANES), 0)
    ej = lax.broadcasted_iota(i32, (N_EXP, LANES), 1)
    lower = jnp.where(ej <= ei, 1.0, 0.0)[:, :N_EXP].astype(bf16)
    tend = _dot(lower, ntb)
    tstart = tend - ntile
    tt = lax.broadcasted_iota(i32, (N_EXP, META_W), 1).astype(f32)
    te = jnp.sum(jnp.where(tt >= tend[:, 0:1], 1.0, 0.0), axis=0, keepdims=True)
    meta_ref[...] = jnp.zeros_like(meta_ref)
    meta_ref[0:1, :] = jnp.minimum(te, N_EXP - 1.0).astype(i32)
    meta_ref[1:2, :] = jnp.broadcast_to(tend[N_EXP - 1:N_EXP, 0:1], (1, META_W)).astype(i32)
    e = e_ref[...]
    base = jnp.zeros(e.shape, f32)
    for ex in range(N_EXP):
        base = jnp.where(e == ex, tstart[ex:ex + 1, 0:1] * float(tm), base)
    pos_ref[...] = jnp.zeros_like(pos_ref)
    pos_ref[:, 0:n_tok] = base.astype(i32) + r_ref[...]


def _plan(e_t, r_t, cnt):
    B, K, N = e_t.shape
    _, npad = _moe_sizes(N)
    return pl.pallas_call(
        functools.partial(_plan_kernel, n_tok=N),
        grid=(B,),
        in_specs=[pl.BlockSpec((None, K, N), lambda b: (b, 0, 0)),
                  pl.BlockSpec((None, K, N), lambda b: (b, 0, 0)),
                  pl.BlockSpec((None, N_EXP, LANES), lambda b: (b, 0, 0))],
        out_specs=[pl.BlockSpec((None, K, npad), lambda b: (b, 0, 0)),
                   pl.BlockSpec((None, SUBLANES, META_W), lambda b: (b, 0, 0))],
        out_shape=[jax.ShapeDtypeStruct((B, K, npad), i32), jax.ShapeDtypeStruct((B, SUBLANES, META_W), i32)],
        compiler_params=_cparams(("arbitrary",)),
        name="moe_plan",
    )(e_t, r_t, cnt)


def _plan_invert(pos, w_t, n_tok):
    B, K, npad = pos.shape
    ntp, _ = _moe_sizes(n_tok)
    plen = ntp * MOE_TILE
    nch = D // LANES
    mesh = plsc.VectorSubcoreMesh(core_axis_name="c", subcore_axis_name="s")
    n_cores = mesh.num_cores
    assert 2 * B <= n_cores * mesh.num_subcores and n_tok % SC_LANES == 0 and plen % SC_LANES == 0

    @functools.partial(
        pl.kernel, mesh=mesh,
        out_type=[jax.ShapeDtypeStruct((B * plen,), i32), jax.ShapeDtypeStruct((B * plen,), f32)],
        scratch_types=[pltpu.VMEM((npad,), i32), pltpu.VMEM((n_tok,), f32),
                       pltpu.VMEM((plen,), i32), pltpu.VMEM((plen,), f32)],
        compiler_params=dataclasses.replace(pltpu.CompilerParams(), needs_layout_passes=False))
    def invert(pos_hbm, w_hbm, rows_hbm, ws_hbm, pos_c, w_c, rows_v, ws_v):
        wid = lax.axis_index("s") * n_cores + lax.axis_index("c")
        b = wid % B
        lane = lax.iota(i32, SC_LANES)

        def load_pos(k):
            pltpu.sync_copy(pos_hbm.at[pl.ds(pl.multiple_of((b * K + k) * npad, SUBLANES), npad)], pos_c)

        @pl.when(wid < B)
        def _():
            pad = jnp.full((SC_LANES,), n_tok * nch, i32)

            @pl.loop(0, plen, step=SC_LANES)
            def _(i):
                rows_v[pl.ds(i, SC_LANES)] = pad

            for k in range(K):
                load_pos(k)

                @pl.loop(0, n_tok, step=SC_LANES)
                def _(n):
                    plsc.store_scatter(rows_v, [pos_c[pl.ds(n, SC_LANES)]], (lane + n) * nch)

            pltpu.sync_copy(rows_v, rows_hbm.at[pl.ds(pl.multiple_of(b * plen, SUBLANES), plen)])

        @pl.when(jnp.logical_and(wid >= B, wid < 2 * B))
        def _():
            zero = jnp.zeros((SC_LANES,), f32)

            @pl.loop(0, plen, step=SC_LANES)
            def _(i):
                ws_v[pl.ds(i, SC_LANES)] = zero

            for k in range(K):
                load_pos(k)
                pltpu.sync_copy(w_hbm.at[pl.ds(pl.multiple_of((b * K + k) * n_tok, SUBLANES), n_tok)], w_c)

                @pl.loop(0, n_tok, step=SC_LANES)
                def _(n):
                    plsc.store_scatter(ws_v, [pos_c[pl.ds(n, SC_LANES)]], w_c[pl.ds(n, SC_LANES)])

            pltpu.sync_copy(ws_v, ws_hbm.at[pl.ds(pl.multiple_of(b * plen, SUBLANES), plen)])

    return invert(pos.reshape(-1), w_t.reshape(-1))


def _moe_kernel(te_ref, nt_ref, *refs, ntp, n_tok):
    R = MOE_GROUP
    rg_ref, rs_ref, ws_ref, hlin_ref = refs[0:4]
    w_refs = refs[4:4 + R]
    out_ref, acc_ref = refs[4 + R:6 + R]
    bufs = refs[6 + R:]
    xs_refs = [bufs[0:R], bufs[R:2 * R]]
    ylin_refs = [bufs[2 * R:3 * R], bufs[3 * R:4 * R]]
    b = pl.program_id(0)
    t = pl.program_id(1)
    TM = MOE_TILE
    NCH = D // LANES
    U = 8
    TMP = TM + SUBLANES
    ngrp = ntp // R

    @pl.when(t == 0)
    def _():
        acc_ref[...] = jnp.zeros_like(acc_ref)

    @pl.when(jnp.logical_and(b == 0, t == 0))
    def _():
        for buf in bufs:
            buf[...] = jnp.zeros_like(buf)

    def gather_rows(p, r, c):
        for m in range(c * U, (c + 1) * U):
            off = pl.multiple_of(rg_ref[r * TM + m], NCH)
            xs_refs[p][r][pl.ds(m, NCH, stride=TMP), :] = hlin_ref[pl.ds(off, NCH), :]

    def scatter_rows(p, r, c):
        offs = [pl.multiple_of(rs_ref[r * TM + c * U + u], NCH) for u in range(U)]
        news = [acc_ref[pl.ds(offs[u], NCH), :] + ylin_refs[p][r][pl.ds((c * U + u) * NCH, NCH), :]
                for u in range(U)]
        for u in range(U):
            acc_ref[pl.ds(offs[u], NCH), :] = news[u]

    def stage(p):
        row_work = [functools.partial(fn, p, r, c) for c in range(TM // U) for r in range(R)
                    for fn in (gather_rows, scatter_rows)]
        n_pieces = R * (EXP_FF // LANES + D // MXU_N)
        per_piece = -(-len(row_work) // n_pieces)

        def deal():
            for fn in row_work[:per_piece]:
                fn()
            del row_work[:per_piece]

        for r in range(R):
            x = jnp.concatenate([xs_refs[1 - p][r][pl.ds(j * TMP, TM), :] for j in range(NCH)], axis=1).astype(bf16)
            wcol = jnp.broadcast_to(ws_ref[r:r + 1, :], (SUBLANES, TM)).T[:, 0:1]
            gate = _dot(x, w_refs[r][0:D, :EXP_FF])
            deal()
            up = _dot(x, w_refs[r][0:D, EXP_FF:])
            deal()
            a = (_silu(gate) * up * wcol).astype(bf16)
            per_half = 2 * EXP_FF // MXU_N
            for c in range(D // MXU_N):
                row0 = D + (c // per_half) * EXP_FF
                col0 = (c % per_half) * MXU_N
                y = _dot(a, w_refs[r][row0:row0 + EXP_FF, col0:col0 + MXU_N])
                for jj in range(MXU_N // LANES):
                    j = c * (MXU_N // LANES) + jj
                    ylin_refs[1 - p][r][pl.ds(j, TM, stride=NCH), :] = y[:, jj * LANES:(jj + 1) * LANES]
                deal()
        while row_work:
            deal()

    live = (t - 2) * R < nt_ref[b]
    pl.when(jnp.logical_and(live, t % 2 == 0))(functools.partial(stage, 0))
    pl.when(jnp.logical_and(live, t % 2 == 1))(functools.partial(stage, 1))

    @pl.when(t >= ngrp + 2)
    def _():
        row0 = (t - (ngrp + 2)) * (ROW_TILE * NCH)
        for j in range(NCH):
            out_ref[:, j * LANES:(j + 1) * LANES] = acc_ref[pl.ds(row0 + j, ROW_TILE, stride=NCH), :].astype(bf16)


def _moe(te, nt, rows, wsort, hlin, w, layer):
    B = hlin.shape[0]
    NCH = D // LANES
    n_tok = hlin.shape[1] // NCH - ROW_TILE
    assert n_tok % ROW_TILE == 0
    nf = n_tok // ROW_TILE
    TM = MOE_TILE
    R = MOE_GROUP
    ntp, _ = _moe_sizes(n_tok)
    assert ntp % R == 0
    ngrp = ntp // R

    def group_of(b, t, nt_ref, lag):
        grp = t - lag
        ok = jnp.logical_and(t >= lag, grp * R < nt_ref[b])
        return b * ngrp + jnp.where(ok, grp, ngrp - 1)

    def rows_spec(lag):
        return pl.BlockSpec((R * TM,), lambda b, t, te_ref, nt_ref: (group_of(b, t, nt_ref, lag),),
                            memory_space=pltpu.SMEM)

    ws_spec = pl.BlockSpec((None, R, TM), lambda b, t, te_ref, nt_ref: (group_of(b, t, nt_ref, 1), 0, 0))

    def w_spec(r):
        def index(b, t, te_ref, nt_ref):
            tile = jnp.clip((t - 1) * R + r, 0, nt_ref[b] - 1)
            return (layer, te_ref[b * META_W + tile], 0, 0)
        return pl.BlockSpec((None, None) + tuple(w.shape[2:]), index)

    grid_spec = pltpu.PrefetchScalarGridSpec(
        num_scalar_prefetch=2,
        grid=(B, ngrp + 2 + nf),
        in_specs=([rows_spec(0), rows_spec(2), ws_spec,
                   pl.BlockSpec((None, (n_tok + ROW_TILE) * NCH, LANES), lambda b, t, *_: (b, 0, 0),
                                pipeline_mode=pl.Buffered(1))]
                  + [w_spec(r) for r in range(R)]),
        out_specs=pl.BlockSpec((None, ROW_TILE, D), lambda b, t, *_: (b, jnp.maximum(t - (ngrp + 2), 0), 0)),
        scratch_shapes=([pltpu.VMEM(((n_tok + SUBLANES) * NCH, LANES), f32)]
                        + [pltpu.VMEM(((TM + SUBLANES) * NCH, LANES), f32) for _ in range(2 * R)]
                        + [pltpu.VMEM((TM * NCH, LANES), f32) for _ in range(2 * R)]),
    )
    return pl.pallas_call(
        functools.partial(_moe_kernel, ntp=ntp, n_tok=n_tok),
        grid_spec=grid_spec,
        out_shape=jax.ShapeDtypeStruct((B, n_tok, D), bf16),
        compiler_params=_cparams(("arbitrary", "arbitrary"), MOE_VMEM_LIMIT),
        name="moe",
    )(te, nt, rows, rows, wsort.reshape(B * ngrp, R, TM), hlin, *([w] * R))


def _routed_experts(e_t, w_t, r_t, cnt, hlin, w, layer):
    N = e_t.shape[2]
    pos, meta = _plan(e_t, r_t, cnt)
    te = meta[:, 0, :].reshape(-1)
    nt = meta[:, 1, 0]
    rows, wsort = _plan_invert(pos, w_t, N)
    return _moe(te, nt, rows, wsort, hlin, w, layer)


def _shared_ffn(fin, shgu_ref, shd_ref):
    gu = _dot(fin, shgu_ref[...])
    return _dot((_silu(gu[:, :SH_FF]) * gu[:, SH_FF:]).astype(bf16), shd_ref[...])


def _post_ffn_final_kernel(x1_ref, routed_ref, fin_ref, shgu_ref, shd_ref, mod_ref, out_ref):
    out_ref[...] = x1_ref[...] + mod_ref[...][:, 5 * D:6 * D] * (
        routed_ref[...] + _shared_ffn(fin_ref[...], shgu_ref, shd_ref))


def _post_ffn_final(x1, routed, fin, sh_gu, sh_d, modtab):
    B, N, _ = x1.shape
    tm = ROW_TILE
    tok = lambda w: pl.BlockSpec((None, tm, w), lambda b, i: (b, i, 0))
    full = lambda r, c: pl.BlockSpec((r, c), lambda b, i: (0, 0))
    return pl.pallas_call(
        _post_ffn_final_kernel,
        grid=(B, N // tm),
        in_specs=[tok(D), tok(D), tok(D), full(D, 2 * SH_FF), full(SH_FF, D),
                  pl.BlockSpec((None, None, 1, 6 * D), lambda b, i: (b, 1, 0, 0))],
        out_specs=tok(D),
        out_shape=jax.ShapeDtypeStruct((B, N, D), f32),
        compiler_params=_cparams(("arbitrary", "arbitrary")),
        name="post_ffn_final",
    )(x1, routed, fin, sh_gu, sh_d, modtab)


def _mla_qkv(a, qan_ref, wqn_ref, wqr_ref, kvan_ref, wk_ref, wv_ref, qnn_ref, qnr_ref, knn_ref, knr_ref,
             cos_ref, sin_ref, q_ref, k_ref, v_ref):
    tm = a.shape[0]
    scale = MLA_QK ** -0.5 * float(np.log2(np.e))
    cos = cos_ref[...]
    sin = sin_ref[...]
    lane = lax.broadcasted_iota(i32, (tm, LANES), 1)
    first = (lane // (MLA_ROPE // 4)) % 2 == 0

    def rope(xb):
        sw = jnp.where(first, pltpu.roll(xb, LANES - MLA_ROPE // 4, axis=1), pltpu.roll(xb, MLA_ROPE // 4, axis=1))
        return xb * cos + sw * sin

    qa = (_rms(a[:, :MLA_QR]) * qan_ref[...]).astype(bf16)
    qn = _dot(qa, wqn_ref[...])
    yield
    qr = _dot(qa, wqr_ref[...])
    yield
    ri = lax.broadcasted_iota(i32, (LANES, LANES), 0) // MLA_ROPE
    ci = lax.broadcasted_iota(i32, (LANES, LANES), 1) // MLA_ROPE
    seg = jnp.where(ri == ci, 1.0, 0.0).astype(bf16)

    def seg_sum(sq):
        hi = sq.astype(bf16)
        r1 = sq - hi.astype(f32)
        mid = r1.astype(bf16)
        lo = (r1 - mid.astype(f32)).astype(bf16)
        return _dot(hi, seg) + _dot(mid, seg) + _dot(lo, seg)

    qr_blocks = []
    for p in range(MLA_H // 2):
        blk = qr[:, p * LANES:(p + 1) * LANES]
        blk = blk * lax.rsqrt(seg_sum(blk * blk) * (1.0 / MLA_ROPE) + EPS) * qnr_ref[:, p * LANES:(p + 1) * LANES]
        qr_blocks.append(rope(blk) * scale)
        yield

    kv = (_rms(a[:, MLA_QR:MLA_QR + MLA_KVR]) * kvan_ref[...]).astype(bf16)
    kn = _dot(kv, wk_ref[...])
    yield
    v_ref[...] = _dot_nt(wv_ref[...], kv).astype(bf16)
    yield
    kr = a[:, MLA_QR + MLA_KVR:MLA_IN_PAD]
    kr = rope(_rms(kr, MLA_ROPE) * knr_ref[...])
    kr_odd = pltpu.roll(kr, MLA_ROPE, axis=1)
    for hd in range(MLA_H):
        sl = slice(hd * MLA_NOPE, (hd + 1) * MLA_NOPE)
        q_ref[:, 2 * hd * LANES:(2 * hd + 1) * LANES] = (_rms(qn[:, sl]) * qnn_ref[...] * scale).astype(bf16)
        q_ref[:, (2 * hd + 1) * LANES:(2 * hd + 2) * LANES] = qr_blocks[hd // 2].astype(bf16)
        k_ref[:, 2 * hd * LANES:(2 * hd + 1) * LANES] = (_rms(kn[:, sl]) * knn_ref[...]).astype(bf16)
        k_ref[:, (2 * hd + 1) * LANES:(2 * hd + 2) * LANES] = (kr if hd % 2 == 0 else kr_odd).astype(bf16)
        if hd % 2 == 1:
            yield


def _ffn_mla_kernel(x1_ref, routed_ref, fin_ref, shgu_ref, shd_ref, mod0_ref, mod1_ref, g_ref, win_ref, *refs, nt):
    qkv_refs, (x2_ref, q_ref, k_ref, v_ref, a0_ref, a1_ref) = refs[:12], refs[12:]
    a_refs = (a0_ref, a1_ref)
    i = pl.program_id(1)

    def qkv(p):
        return _mla_qkv(a_refs[p][...], *qkv_refs, q_ref, k_ref, v_ref)

    def combine(p, other=()):
        other = iter(other)

        def advance(n):
            for _ in range(n):
                next(other, None)

        gu = _dot(fin_ref[...], shgu_ref[...])
        advance(3)
        shared = _dot((_silu(gu[:, :SH_FF]) * gu[:, SH_FF:]).astype(bf16), shd_ref[...])
        advance(3)
        x2 = x1_ref[...] + mod0_ref[...][:, 5 * D:6 * D] * (routed_ref[...] + shared)
        x2_ref[...] = x2
        advance(2)
        mod1 = mod1_ref[...]
        h = (_rms(x2) * g_ref[...]) * (1.0 + mod1[:, D:2 * D]) + mod1[:, 0:D]
        advance(2)
        a_refs[p][...] = _dot(h.astype(bf16), win_ref[...])
        for _ in other:
            pass

    pl.when(i == 0)(functools.partial(combine, 0))
    for p in range(2):
        @pl.when(jnp.logical_and(jnp.logical_and(i >= 1, i < nt), i % 2 == p))
        def _(p=p):
            combine(p, qkv(1 - p))

    @pl.when(i == nt)
    def _():
        for _ in qkv((nt - 1) % 2):
            pass


def _ffn_mla(x1, routed, fin, sh_gu, sh_d, modtab0, modtab1, gain, w_in, qkv_params, cos_t, sin_t, n_ctx_tiles):
    B, T, _ = x1.shape
    tm = ROW_TILE
    nt = T // tm
    last = lambda i: jnp.minimum(i, nt - 1)
    prev = lambda i: jnp.maximum(i - 1, 0)
    tok = lambda w: pl.BlockSpec((None, tm, w), lambda b, i: (b, last(i), 0))
    modspec = lambda: pl.BlockSpec((None, None, 1, 6 * D),
                                   lambda b, i: (b, jnp.where(last(i) < n_ctx_tiles, 0, 1), 0, 0))
    full = lambda r, c: pl.BlockSpec((r, c), lambda b, i: (0, 0))
    rope_spec = lambda: pl.BlockSpec((tm, LANES), lambda b, i: (prev(i), 0))
    hw = 2 * LANES * MLA_H
    return pl.pallas_call(
        functools.partial(_ffn_mla_kernel, nt=nt),
        grid=(B, nt + 1),
        in_specs=[tok(D), tok(D), tok(D), full(D, 2 * SH_FF), full(SH_FF, D), modspec(), modspec(),
                  full(1, D), full(D, MLA_IN_PAD),
                  full(1, MLA_QR), full(MLA_QR, MLA_H * MLA_NOPE), full(MLA_QR, MLA_H * MLA_ROPE),
                  full(1, MLA_KVR), full(MLA_KVR, MLA_H * MLA_NOPE), full(MLA_H * MLA_V, MLA_KVR),
                  full(1, MLA_NOPE), full(1, MLA_H * MLA_ROPE), full(1, MLA_NOPE), full(1, LANES),
                  rope_spec(), rope_spec()],
        out_specs=[tok(D),
                   pl.BlockSpec((None, tm, hw), lambda b, i: (b, jnp.maximum(prev(i) - n_ctx_tiles, 0), 0)),
                   pl.BlockSpec((None, tm, hw), lambda b, i: (b, prev(i), 0)),
                   pl.BlockSpec((None, MLA_H * MLA_V, tm), lambda b, i: (b, 0, prev(i)))],
        out_shape=[jax.ShapeDtypeStruct((B, T, D), f32),
                   jax.ShapeDtypeStruct((B, T - n_ctx_tiles * tm, hw), bf16), jax.ShapeDtypeStruct((B, T, hw), bf16),
                   jax.ShapeDtypeStruct((B, MLA_H * MLA_V, T), bf16)],
        scratch_shapes=[pltpu.VMEM((tm, MLA_IN_PAD), f32), pltpu.VMEM((tm, MLA_IN_PAD), f32)],
        compiler_params=_cparams(("arbitrary", "arbitrary")),
        name="ffn_mla",
    )(x1, routed, fin, sh_gu, sh_d, modtab0, modtab1, gain, w_in, *qkv_params, cos_t, sin_t)


ATT_TQ = 256


def _mla_attn_kernel(q_ref, k_ref, vt_ref, o_ref, s0_ref, s1_ref, m0_ref, m1_ref):
    tq = q_ref.shape[0]
    groups = k_ref.shape[0] // SUBLANES
    i = pl.program_id(0)
    s_refs, m_refs = (s0_ref, s1_ref), (m0_ref, m1_ref)

    @pl.when(i == 0)
    def _():
        for ref in s_refs + m_refs:
            ref[...] = jnp.zeros_like(ref)

    def stage(par):
        s = _dot_nt(k_ref[...], q_ref[...])
        s_refs[par][...] = s
        m_refs[par][...] = jnp.max(s.reshape(groups, SUBLANES, tq), axis=0)
        m = jnp.max(m_refs[1 - par][...], axis=0, keepdims=True)
        p = jnp.exp2(s_refs[1 - par][...] - m)
        lsum = jnp.sum(jnp.sum(p.reshape(groups, SUBLANES, tq), axis=0), axis=0, keepdims=True)
        acc = _dot(vt_ref[...], p.astype(bf16))
        o_ref[...] = (acc / lsum).astype(bf16)

    pl.when(i % 2 == 0)(functools.partial(stage, 0))
    pl.when(i % 2 == 1)(functools.partial(stage, 1))


def _mla_attn(q, k, vt):
    B, S, _ = q.shape
    T = k.shape[1]
    tq = ATT_TQ
    nq = S // tq
    ntile = B * MLA_H * nq
    assert S % tq == 0 and T % SUBLANES == 0

    def tile(g):
        g = jnp.clip(g, 0, ntile - 1)
        return g // (MLA_H * nq), (g // nq) % MLA_H, g % nq

    def q_index(g):
        b, h, i = tile(g)
        return b, i, h

    def k_index(g):
        b, h, _ = tile(g)
        return b, 0, h

    def vt_index(g):
        b, h, _ = tile(g - 1)
        return b, h, 0

    def o_index(g):
        b, h, i = tile(g - 1)
        return b, h, i

    return pl.pallas_call(
        _mla_attn_kernel,
        grid=(ntile + 1,),
        in_specs=[pl.BlockSpec((None, tq, 2 * LANES), q_index),
                  pl.BlockSpec((None, T, 2 * LANES), k_index),
                  pl.BlockSpec((None, MLA_V, T), vt_index)],
        out_specs=pl.BlockSpec((None, MLA_V, tq), o_index),
        out_shape=jax.ShapeDtypeStruct((B, MLA_H * MLA_V, S), bf16),
        scratch_shapes=[pltpu.VMEM((T, tq), f32), pltpu.VMEM((T, tq), f32),
                        pltpu.VMEM((SUBLANES, tq), f32), pltpu.VMEM((SUBLANES, tq), f32)],
        compiler_params=_cparams(("arbitrary",)),
        name="mla_attn",
    )(q, k, vt)


def _axial_angles(rows_n, rot_dim):
    axis_dim = rot_dim // 2
    inv = ROPE_BASE ** (-jnp.arange(0, axis_dim, 2, dtype=f32) / axis_dim)
    row = jnp.repeat(jnp.arange(rows_n, dtype=f32), GRID_W)
    col = jnp.tile(jnp.arange(GRID_W, dtype=f32), rows_n)
    return row[:, None] * inv, col[:, None] * inv


def _rope_tables(seq, n_ctx, rot_dim, reps):
    ang_r, ang_c = _axial_angles(seq // GRID_W, rot_dim)
    cos = jnp.concatenate([jnp.cos(ang_r)] * 2 + [jnp.cos(ang_c)] * 2, axis=1)
    sin = jnp.concatenate([-jnp.sin(ang_r), jnp.sin(ang_r), -jnp.sin(ang_c), jnp.sin(ang_c)], axis=1)
    cos = jnp.concatenate([jnp.ones((n_ctx, rot_dim), f32), cos], axis=0)
    sin = jnp.concatenate([jnp.zeros((n_ctx, rot_dim), f32), sin], axis=0)
    return jnp.tile(cos, (1, reps)), jnp.tile(sin, (1, reps))


def kernel(x, c, ctx, c_ctx, ada_w, ada_b, norm_mix, norm_ffn, ret_w_in, ret_decay_f, ret_decay_b, ret_w_o,
           mla_w_in, mla_q_a_norm, mla_w_q_b, mla_kv_a_norm, mla_w_kv_b, mla_q_norm, mla_k_norm, mla_w_o,
           router_w, router_bias, exp_w_gu, exp_w_down, sh_w_gu, sh_w_down):
    B, S, _ = x.shape
    n_ctx = ctx.shape[1]
    assert n_ctx % ROW_TILE == 0 and S % ROW_TILE == 0 and S % GRID_W == 0
    n_ctx_tiles = n_ctx // ROW_TILE

    rows = -(-(B + 1) // SUBLANES) * SUBLANES
    cc = jnp.zeros((rows, D), f32).at[:B].set(c).at[B].set(c_ctx)
    mod = _ada(cc, ada_w, ada_b)

    def modtab(i):
        ctx_row = jnp.broadcast_to(mod[i, B][None, :], (B, 6 * D))
        return jnp.stack([ctx_row, mod[i, :B]], axis=1)[:, :, None, :]

    mod0, mod1 = modtab(0), modtab(1)

    cos_r, sin_r = _rope_tables(S, n_ctx, RET_DK, 1)
    ride = exp_w_gu.shape[0] * exp_w_gu.shape[1] == _cast_steps(B, (n_ctx + S) // ROW_TILE) and D == 4 * EXP_FF
    proj = _ret_inproj(ctx, x, mod0, norm_mix[0][None, :], ret_w_in[0].astype(bf16), cos_r, sin_r, n_ctx_tiles,
                       (exp_w_gu, exp_w_down) if ride else ())
    q, k, v, gf, gb = proj[:5]
    if ride:
        exp_w = proj[5]
    else:
        exp_w = jnp.concatenate([exp_w_gu, exp_w_down[..., :2 * EXP_FF], exp_w_down[..., 2 * EXP_FF:]],
                                axis=2).astype(bf16)
    dtab = jnp.broadcast_to(jnp.concatenate([ret_decay_f[0], ret_decay_b[0]])[:, None], (2 * RET_H, LANES))
    o = _ret_scan(dtab, q, k, v, gf, gb, n_ctx)
    x1, fin, hlin, e_t, w_t, r_t, cnt = _post_mix(
        o, ret_w_o[0].astype(bf16), (ctx, x), 0, mod0, n_ctx_tiles, norm_ffn[0][None, :],
        router_w[0].T, router_bias[0][:, None])
    routed = _routed_experts(e_t, w_t, r_t, cnt, hlin, exp_w, 0)
    w_in1 = jnp.zeros((D, MLA_IN_PAD), f32).at[:, :mla_w_in.shape[2]].set(mla_w_in[0]).astype(bf16)

    wq = mla_w_q_b[0].reshape(MLA_QR, MLA_H, MLA_QK)
    wqn = wq[:, :, :MLA_NOPE].reshape(MLA_QR, MLA_H * MLA_NOPE).astype(bf16)
    wqr = wq[:, :, MLA_NOPE:].reshape(MLA_QR, MLA_H * MLA_ROPE).astype(bf16)
    wkv = mla_w_kv_b[0].reshape(MLA_KVR, MLA_H, MLA_NOPE + MLA_V)
    wk = wkv[:, :, :MLA_NOPE].reshape(MLA_KVR, MLA_H * MLA_NOPE).astype(bf16)
    wv = wkv[:, :, MLA_NOPE:].reshape(MLA_KVR, MLA_H * MLA_V).T.astype(bf16)
    qnn = mla_q_norm[0][None, :MLA_NOPE]
    qnr = jnp.tile(mla_q_norm[0][None, MLA_NOPE:], (1, MLA_H))
    knn = mla_k_norm[0][None, :MLA_NOPE]
    knr = jnp.concatenate([mla_k_norm[0][MLA_NOPE:], jnp.zeros((LANES - MLA_ROPE,), f32)])[None, :]
    cos_m, sin_m = _rope_tables(S, n_ctx, MLA_ROPE, LANES // MLA_ROPE)
    qkv_params = (mla_q_a_norm[0][None, :], wqn, wqr, mla_kv_a_norm[0][None, :], wk, wv, qnn, qnr, knn, knr)
    x2, qf, kf, vf = _ffn_mla(x1, routed, fin, sh_w_gu[0].astype(bf16), sh_w_down[0].astype(bf16), mod0, mod1,
                              norm_mix[1][None, :], w_in1, qkv_params, cos_m, sin_m, n_ctx_tiles)
    o1 = _mla_attn(qf, kf, vf)
    x3, fin1, hlin1, e1, w1, r1, cnt1 = _post_mix(
        o1, mla_w_o[0].astype(bf16), (x2,), n_ctx_tiles, mod1, 0, norm_ffn[1][None, :],
        router_w[1].T, router_bias[1][:, None], o_transposed=True)
    routed1 = _routed_experts(e1, w1, r1, cnt1, hlin1, exp_w, 1)
    return _post_ffn_final(x3, routed1, fin1, sh_w_gu[1].astype(bf16), sh_w_down[1].astype(bf16), mod1)
```

```python
import dataclasses
import functools

import jax
import jax.numpy as jnp
import numpy as np
from jax import lax
from jax.experimental import pallas as pl
from jax.experimental.pallas import tpu as pltpu
from jax.experimental.pallas import tpu_sc as plsc

f32 = jnp.float32
bf16 = jnp.bfloat16
i32 = jnp.int32

D = 1024
GRID_W = 64
EPS = 1e-6
ROPE_BASE = 10000.0
RET_H = 4
RET_DK = 256
RET_DV = 512
RET_VW = RET_H * RET_DV
RET_CHUNK = 256
MLA_H = 8
MLA_NOPE = 128
MLA_ROPE = 64
MLA_QK = MLA_NOPE + MLA_ROPE
MLA_V = 128
MLA_QR = 384
MLA_KVR = 256
MLA_IN_PAD = 768
N_EXP = 64
TOP_K = 8
N_GRP = 8
TOPK_GRP = 4
EXP_FF = 256
SH_FF = 256
ROUTED_SCALE = 2.5

LANES = 128
SUBLANES = 8
SC_LANES = 16
MXU_N = 256
ROW_TILE = 256
MOE_TILE = 256
MOE_GROUP = 2
ADA_COLS = 1536
PROJ_COLS = 512
VMEM_LIMIT = 56 * 1024 * 1024
MOE_VMEM_LIMIT = 62 * 1024 * 1024


def _cparams(sem, vmem=VMEM_LIMIT):
    return pltpu.CompilerParams(dimension_semantics=sem, vmem_limit_bytes=vmem)


def _sigmoid(x):
    return 1.0 / (1.0 + jnp.exp(-x))


def _silu(x):
    return x * _sigmoid(x)


def _rms(x, n=None):
    n = x.shape[-1] if n is None else n
    return x * lax.rsqrt(jnp.sum(x * x, axis=-1, keepdims=True) * (1.0 / n) + EPS)


def _dot(a, b):
    return jnp.dot(a, b, preferred_element_type=f32)


def _dot_nt(a, b):
    return lax.dot_general(a, b, (((1,), (1,)), ((), ())), preferred_element_type=f32)


def _ada_kernel(c_ref, w_ref, b_ref, o_ref):
    s = _silu(c_ref[...]).astype(bf16)
    o_ref[...] = _dot(s, w_ref[...].astype(bf16)) + b_ref[...]


def _ada(cc, ada_w, ada_b):
    depth = ada_w.shape[0]
    rows = cc.shape[0]
    tn = ADA_COLS
    return pl.pallas_call(
        _ada_kernel,
        grid=(depth, 6 * D // tn),
        in_specs=[pl.BlockSpec((rows, D), lambda i, j: (0, 0)),
                  pl.BlockSpec((None, D, tn), lambda i, j: (i, 0, j)),
                  pl.BlockSpec((None, 1, tn), lambda i, j: (i, 0, j))],
        out_specs=pl.BlockSpec((None, rows, tn), lambda i, j: (i, 0, j)),
        out_shape=jax.ShapeDtypeStruct((depth, rows, 6 * D), f32),
        compiler_params=_cparams(("arbitrary", "arbitrary")),
        name="ada",
    )(cc, ada_w, ada_b.reshape(depth, 1, 6 * D))


def _stream_tile(ctx_ref, x_ref, n_ctx_tiles):
    return jnp.where(pl.program_id(1) < n_ctx_tiles, ctx_ref[...], x_ref[...])


def _stream_specs(tm, n_ctx_tiles, last=None):
    clamp = (lambda i: i) if last is None else (lambda i: jnp.minimum(i, last))
    return [pl.BlockSpec((None, tm, D), lambda b, i: (b, jnp.minimum(clamp(i), n_ctx_tiles - 1), 0)),
            pl.BlockSpec((None, tm, D), lambda b, i: (b, jnp.maximum(clamp(i) - n_ctx_tiles, 0), 0))]


def _ret_inproj_kernel(ctx_ref, x_ref, mod_ref, g_ref, w_ref, cos_ref, sin_ref, *refs, n_ctx_tiles, n_cast):
    cast_in, cast_out = refs[:n_cast], refs[n_cast + 5:]
    q_ref, k_ref, v_ref, gf_ref, gb_ref = refs[n_cast:n_cast + 5]
    x = _stream_tile(ctx_ref, x_ref, n_ctx_tiles)
    mod = mod_ref[...]
    h = (_rms(x) * g_ref[...]) * (1.0 + mod[:, D:2 * D]) + mod[:, 0:D]
    hb = h.astype(bf16)
    cos = cos_ref[...]
    sin = sin_ref[...]

    def rope(a):
        outs = []
        for half in range(2):
            sl = slice(half * LANES, (half + 1) * LANES)
            ah = a[:, sl]
            outs.append(ah * cos[:, sl] + pltpu.roll(ah, LANES // 2, axis=1) * sin[:, sl])
        return jnp.concatenate(outs, axis=1)

    for hd in range(RET_H):
        sl = slice(hd * RET_DK, (hd + 1) * RET_DK)
        q_ref[:, sl] = rope(_dot(hb, w_ref[:, sl])).astype(bf16)
    if n_cast:
        (gu_ref, dn_ref), (wcat_ref,) = cast_in, cast_out
        wcat_ref[0:D, :] = gu_ref[...].astype(bf16)
        dn = dn_ref[...]
        wcat_ref[D:D + EXP_FF, :] = dn[:, :2 * EXP_FF].astype(bf16)
        wcat_ref[D + EXP_FF:D + 2 * EXP_FF, :] = dn[:, 2 * EXP_FF:].astype(bf16)
    for hd in range(RET_H):
        sl = slice(hd * RET_DK, (hd + 1) * RET_DK)
        wsl = slice(D + hd * RET_DK, D + (hd + 1) * RET_DK)
        k_ref[:, sl] = (rope(_dot(hb, w_ref[:, wsl])) * (RET_DK ** -0.5)).astype(bf16)
    cw = PROJ_COLS
    for c in range(RET_VW // cw):
        sl = slice(c * cw, (c + 1) * cw)
        v_ref[:, sl] = _dot(hb, w_ref[:, 2 * D + c * cw:2 * D + (c + 1) * cw]).astype(bf16)
        gf_ref[:, sl] = _silu(_dot(hb, w_ref[:, 2 * D + RET_VW + c * cw:2 * D + RET_VW + (c + 1) * cw])).astype(bf16)
        gb_ref[:, sl] = _silu(_dot(hb, w_ref[:, 2 * D + 2 * RET_VW + c * cw:2 * D + 2 * RET_VW + (c + 1) * cw])).astype(bf16)


def _cast_steps(batch, n_tiles):
    return batch * max(n_tiles - 1, 1)


def _packed_expert_shape(w_gu):
    n_l, n_e = w_gu.shape[:2]
    return n_l, n_e, D + 2 * EXP_FF, 2 * EXP_FF


def _ret_inproj(ctx, x, modtab, gain, w_in, cos_t, sin_t, n_ctx_tiles, riders=()):
    B = x.shape[0]
    T = ctx.shape[1] + x.shape[1]
    tm = ROW_TILE
    nt = T // tm
    n_in = w_in.shape[1]
    tok = lambda w: pl.BlockSpec((None, tm, w), lambda b, i: (b, i, 0))
    per_b = _cast_steps(B, nt) // B
    rider_out = []
    if riders:
        n_l, n_e = riders[0].shape[:2]
        assert n_l * n_e == per_b * B and D == 4 * EXP_FF
        rider_out = [jax.ShapeDtypeStruct(_packed_expert_shape(riders[0]), bf16)]

    def rider_spec(shape):
        def index(b, i):
            s = b * per_b + jnp.minimum(i, per_b - 1)
            return s // n_e, s % n_e, 0, 0
        return pl.BlockSpec((None, None) + tuple(shape[2:]), index)

    return pl.pallas_call(
        functools.partial(_ret_inproj_kernel, n_ctx_tiles=n_ctx_tiles, n_cast=len(riders)),
        grid=(B, nt),
        in_specs=_stream_specs(tm, n_ctx_tiles) + [
                  pl.BlockSpec((None, None, 1, 6 * D), lambda b, i: (b, jnp.where(i < n_ctx_tiles, 0, 1), 0, 0)),
                  pl.BlockSpec((1, D), lambda b, i: (0, 0)),
                  pl.BlockSpec((D, n_in), lambda b, i: (0, 0), pipeline_mode=pl.Buffered(1)),
                  pl.BlockSpec((tm, RET_DK), lambda b, i: (i, 0)),
                  pl.BlockSpec((tm, RET_DK), lambda b, i: (i, 0))] + [rider_spec(r.shape) for r in riders],
        out_specs=[tok(D), tok(D), tok(RET_VW), tok(RET_VW), tok(RET_VW)] + [rider_spec(r.shape) for r in rider_out],
        out_shape=[jax.ShapeDtypeStruct((B, T, D), bf16), jax.ShapeDtypeStruct((B, T, D), bf16),
                   jax.ShapeDtypeStruct((B, T, RET_VW), bf16), jax.ShapeDtypeStruct((B, T, RET_VW), bf16),
                   jax.ShapeDtypeStruct((B, T, RET_VW), bf16)] + rider_out,
        compiler_params=_cparams(("arbitrary", "arbitrary")),
        name="ret_inproj",
    )(ctx, x, modtab, gain, w_in, cos_t, sin_t, *riders)


def _ret_chunk_index(t, nc, ncc):
    u = t - nc
    back = jnp.where(u < ncc, ncc - 1 - u, nc - 1 - u + ncc)
    return jnp.where(t < nc, t, back)


def _ret_scan_kernel(dt_ref, q_ref, k_ref, v_ref, gf_ref, gb_ref, o_ref,
                     s_ref, of_ref, mask_ref, dq_ref, dk_ref, dc_ref, *, nc, ncc):
    t = pl.program_id(1)
    C = RET_CHUNK

    def init(direction):
        s_ref[...] = jnp.zeros_like(s_ref)
        ii = lax.broadcasted_iota(i32, (C, C), 0)
        jj = lax.broadcasted_iota(i32, (C, C), 1)
        rel = (ii - jj if direction == 0 else jj - ii).astype(f32)
        pos = lax.broadcasted_iota(i32, (C, 1), 0).astype(f32)
        for hd in range(RET_H):
            r = direction * RET_H + hd
            lg = -jnp.exp(dt_ref[r:r + 1, :])
            lg1 = lg[:, 0:1]
            mask_ref[hd] = jnp.where(rel >= 0, jnp.exp(lg1 * jnp.maximum(rel, 0.0)), 0.0)
            if direction == 0:
                dq_ref[hd] = jnp.exp(lg1 * (pos + 1.0))
                dk_ref[hd] = jnp.exp(lg1 * (C - 1.0 - pos))
            else:
                dq_ref[hd] = jnp.exp(lg1 * (C - pos))
                dk_ref[hd] = jnp.exp(lg1 * pos)
            dc_ref[hd] = jnp.exp(lg * float(C))

    pl.when(t == 0)(functools.partial(init, 0))
    pl.when(t == nc)(functools.partial(init, 1))

    row0 = pl.multiple_of(_ret_chunk_index(t, nc, ncc) * C, C)

    def step(forward):
        for hd in range(RET_H):
            ks = slice(hd * RET_DK, (hd + 1) * RET_DK)
            vs = slice(hd * RET_DV, (hd + 1) * RET_DV)
            qh = q_ref[:, ks]
            kh = k_ref[:, ks]
            vh = v_ref[:, vs]
            p = (_dot_nt(qh, kh) * mask_ref[hd]).astype(bf16)
            y = _dot(p, vh) + _dot(qh, s_ref[hd].astype(bf16)) * dq_ref[hd]
            kd = (kh.astype(f32) * dk_ref[hd]).astype(bf16)
            upd = lax.dot_general(kd, vh, (((0,), (0,)), ((), ())), preferred_element_type=f32)
            s_ref[hd] = s_ref[hd] * dc_ref[hd][0:1, 0:1] + upd
            yn = _rms(y)
            if forward:
                of_ref[pl.ds(row0, C), vs] = (gf_ref[:, vs].astype(f32) * yn).astype(bf16)
            else:
                o_ref[:, vs] = (of_ref[pl.ds(row0, C), vs].astype(f32) + gb_ref[:, vs].astype(f32) * yn).astype(bf16)

    pl.when(t < nc)(functools.partial(step, True))
    pl.when(t >= nc)(functools.partial(step, False))


def _ret_scan(dtab, q, k, v, gf, gb, n_ctx):
    B, T, _ = q.shape
    C = RET_CHUNK
    nc = T // C
    ncc = n_ctx // C
    cidx = functools.partial(_ret_chunk_index, nc=nc, ncc=ncc)
    first_back = ncc - 1
    return pl.pallas_call(
        functools.partial(_ret_scan_kernel, nc=nc, ncc=ncc),
        grid=(B, 2 * nc),
        in_specs=[pl.BlockSpec((2 * RET_H, LANES), lambda b, t: (0, 0)),
                  pl.BlockSpec((None, C, D), lambda b, t: (b, cidx(t), 0)),
                  pl.BlockSpec((None, C, D), lambda b, t: (b, cidx(t), 0)),
                  pl.BlockSpec((None, C, RET_VW), lambda b, t: (b, cidx(t), 0)),
                  pl.BlockSpec((None, C, RET_VW), lambda b, t: (b, jnp.where(t < nc, t, nc - 1), 0)),
                  pl.BlockSpec((None, C, RET_VW), lambda b, t: (b, jnp.where(t < nc, first_back, cidx(t)), 0))],
        out_specs=pl.BlockSpec((None, C, RET_VW), lambda b, t: (b, jnp.where(t < nc, first_back, cidx(t)), 0)),
        out_shape=jax.ShapeDtypeStruct((B, T, RET_VW), bf16),
        scratch_shapes=[pltpu.VMEM((RET_H, RET_DK, RET_DV), f32),
                        pltpu.VMEM((T, RET_VW), bf16),
                        pltpu.VMEM((RET_H, C, C), f32),
                        pltpu.VMEM((RET_H, C, 1), f32),
                        pltpu.VMEM((RET_H, C, 1), f32),
                        pltpu.VMEM((RET_H, 1, LANES), f32)],
        compiler_params=_cparams(("arbitrary", "arbitrary")),
        name="ret_scan",
    )(dtab, q, k, v, gf, gb)


def _route(f, rwt_ref, rb_ref, cnt_ref, e_ref, w_ref, r_ref):
    tm = f.shape[0]
    G = N_EXP // N_GRP
    logits = _dot_nt(rwt_ref[...].astype(bf16), f.astype(bf16))
    s = _sigmoid(logits)
    sel = s + rb_ref[...]
    mi = lax.broadcasted_iota(i32, (G, tm), 0)
    neg = -jnp.inf
    s_g = [s[g * G:(g + 1) * G, :] for g in range(N_GRP)]
    sel_g = [sel[g * G:(g + 1) * G, :] for g in range(N_GRP)]

    def first_max(a, ids, big):
        mx = jnp.max(a, axis=0, keepdims=True)
        ix = jnp.min(jnp.where(a == mx, ids, big), axis=0, keepdims=True)
        return mx, ix

    gscore = jnp.zeros((N_GRP, tm), f32)
    gi = lax.broadcasted_iota(i32, (N_GRP, tm), 0)
    for g in range(N_GRP):
        t1, i1 = first_max(sel_g[g], mi, G)
        t2 = jnp.max(jnp.where(mi == i1, neg, sel_g[g]), axis=0, keepdims=True)
        gscore = jnp.where(gi == g, t1 + t2, gscore)
    yield
    gmask = jnp.zeros((N_GRP, tm), i32)
    cur = gscore
    for _ in range(TOPK_GRP):
        _, ix = first_max(cur, gi, N_GRP)
        hit = gi == ix
        gmask = jnp.where(hit, 1, gmask)
        cur = jnp.where(hit, neg, cur)
    cand = [jnp.where(gmask[g:g + 1, :] > 0, sel_g[g], neg) for g in range(N_GRP)]
    ids = [mi + g * G for g in range(N_GRP)]

    def across(parts, op):
        acc = parts[0]
        for part in parts[1:]:
            acc = op(acc, part)
        return acc

    e_rows, w_rows = [], []
    for _ in range(TOP_K):
        mx = jnp.max(across(cand, jnp.maximum), axis=0, keepdims=True)
        ix = jnp.min(across([jnp.where(cand[g] == mx, ids[g], N_EXP) for g in range(N_GRP)], jnp.minimum),
                     axis=0, keepdims=True)
        hits = [ids[g] == ix for g in range(N_GRP)]
        cand = [jnp.where(hits[g], neg, cand[g]) for g in range(N_GRP)]
        wv = jnp.sum(across([jnp.where(hits[g], s_g[g], 0.0) for g in range(N_GRP)], jnp.add), axis=0, keepdims=True)
        e_rows.append(ix)
        w_rows.append(wv)
        yield
    wsum = w_rows[0]
    for r in range(1, TOP_K):
        wsum = wsum + w_rows[r]

    selm = [jnp.zeros((G, tm), f32) for _ in range(N_GRP)]
    for r in range(TOP_K):
        for g in range(N_GRP):
            selm[g] = jnp.where(ids[g] == e_rows[r], 1.0, selm[g])
    m_all = jnp.concatenate(selm, axis=0)
    ri = lax.broadcasted_iota(i32, (tm, tm), 0)
    ci = lax.broadcasted_iota(i32, (tm, tm), 1)
    upper = jnp.where(ri <= ci, 1.0, 0.0).astype(bf16)
    incl = _dot(m_all.astype(bf16), upper)
    carry = cnt_ref[:, 0:1]
    rank_all = carry + incl - m_all
    cnt_ref[...] = cnt_ref[...] + incl[:, tm - 1:tm]
    yield
    for r in range(TOP_K):
        rk = jnp.sum(across([jnp.where(ids[g] == e_rows[r], rank_all[g * G:(g + 1) * G, :], 0.0)
                             for g in range(N_GRP)], jnp.add), axis=0, keepdims=True)
        e_ref[r:r + 1, :] = e_rows[r]
        w_ref[r:r + 1, :] = w_rows[r] / wsum * ROUTED_SCALE
        r_ref[r:r + 1, :] = rk.astype(i32)


def _post_mix_kernel(o_ref, wo_ref, *refs, nt, o_transposed, split_ctx_tiles):
    n_resid = 2 if split_ctx_tiles else 1
    resid = refs[:n_resid]
    (mod_ref, g_ref, rwt_ref, rb_ref, x1_ref, fin_ref, hlin_ref, e_ref, w_ref, r_ref, cnt_ref,
     f0_ref, f1_ref) = refs[n_resid:]
    f_refs = (f0_ref, f1_ref)
    i = pl.program_id(1)

    @pl.when(i == 0)
    def _():
        cnt_ref[...] = jnp.zeros_like(cnt_ref)

    def route(p):
        return _route(f_refs[p][...], rwt_ref, rb_ref, cnt_ref, e_ref, w_ref, r_ref)

    def mix(p, other=()):
        other = iter(other)

        def advance(n):
            for _ in range(n):
                next(other, None)

        mod = mod_ref[...]
        x = _stream_tile(resid[0], resid[1], split_ctx_tiles) if split_ctx_tiles else resid[0][...]
        o = o_ref[...]
        cols = []
        for c in range(D // MXU_N):
            sl = slice(c * MXU_N, (c + 1) * MXU_N)
            if o_transposed:
                piece = lax.dot_general(o, wo_ref[:, sl], (((0,), (0,)), ((), ())), preferred_element_type=f32)
            else:
                piece = _dot(o, wo_ref[:, sl])
            cols.append(x[:, sl] + mod[:, 2 * D + c * MXU_N:2 * D + (c + 1) * MXU_N] * piece)
            advance(2)
        x1 = jnp.concatenate(cols, axis=1)
        x1_ref[...] = x1
        f = (_rms(x1) * g_ref[...]) * (1.0 + mod[:, 4 * D:5 * D]) + mod[:, 3 * D:4 * D]
        advance(2)
        fin_ref[...] = f.astype(bf16)
        tm = f.shape[0]
        for j in range(D // LANES):
            hlin_ref[pl.ds(j, tm, stride=D // LANES), :] = f[:, j * LANES:(j + 1) * LANES]
        f_refs[p][...] = f
        for _ in other:
            pass

    pl.when(i == 0)(functools.partial(mix, 0))
    for p in range(2):
        @pl.when(jnp.logical_and(jnp.logical_and(i >= 1, i < nt), i % 2 == p))
        def _(p=p):
            mix(p, route(1 - p))

    @pl.when(i == nt)
    def _():
        for _ in route((nt - 1) % 2):
            pass
        hlin_ref[...] = jnp.zeros_like(hlin_ref)


def _post_mix(o, w_o, resid, x_tile_off, modtab, n_ctx_tiles, gain, rwt, rb, o_transposed=False):
    if o_transposed:
        B, KO, N = o.shape
    else:
        B, N, KO = o.shape
    tm = ROW_TILE
    nt = N // tm
    last = lambda i: jnp.minimum(i, nt - 1)
    tok = lambda w: pl.BlockSpec((None, tm, w), lambda b, i: (b, last(i), 0))
    sel = lambda: pl.BlockSpec((None, TOP_K, tm), lambda b, i: (b, 0, jnp.maximum(i - 1, 0)))
    o_spec = (pl.BlockSpec((None, KO, tm), lambda b, i: (b, 0, last(i))) if o_transposed else tok(KO))
    split = len(resid) == 2
    resid_specs = (_stream_specs(tm, n_ctx_tiles, nt - 1) if split else
                   [pl.BlockSpec((None, tm, D), lambda b, i: (b, last(i) + x_tile_off, 0))])
    return pl.pallas_call(
        functools.partial(_post_mix_kernel, nt=nt, o_transposed=o_transposed,
                          split_ctx_tiles=n_ctx_tiles if split else 0),
        grid=(B, nt + 1),
        in_specs=[o_spec,
                  pl.BlockSpec((KO, D), lambda b, i: (0, 0))] + resid_specs + [
                  pl.BlockSpec((None, None, 1, 6 * D), lambda b, i: (b, jnp.where(last(i) < n_ctx_tiles, 0, 1), 0, 0)),
                  pl.BlockSpec((1, D), lambda b, i: (0, 0)),
                  pl.BlockSpec((N_EXP, D), lambda b, i: (0, 0)),
                  pl.BlockSpec((N_EXP, 1), lambda b, i: (0, 0))],
        out_specs=[tok(D), tok(D),
                   pl.BlockSpec((None, tm * (D // LANES), LANES), lambda b, i: (b, i, 0)),
                   sel(), sel(), sel(),
                   pl.BlockSpec((None, N_EXP, LANES), lambda b, i: (b, 0, 0))],
        out_shape=[jax.ShapeDtypeStruct((B, N, D), f32), jax.ShapeDtypeStruct((B, N, D), bf16),
                   jax.ShapeDtypeStruct((B, (N + tm) * (D // LANES), LANES), f32),
                   jax.ShapeDtypeStruct((B, TOP_K, N), i32), jax.ShapeDtypeStruct((B, TOP_K, N), f32),
                   jax.ShapeDtypeStruct((B, TOP_K, N), i32),
                   jax.ShapeDtypeStruct((B, N_EXP, LANES), f32)],
        scratch_shapes=[pltpu.VMEM((tm, D), f32), pltpu.VMEM((tm, D), f32)],
        compiler_params=_cparams(("arbitrary", "arbitrary")),
        name="post_mix",
    )(o, w_o, *resid, modtab, gain, rwt, rb)


META_W = 256
PLAN_ALIGN = 1024


def _round_up(n, m):
    return -(-n // m) * m


def _moe_sizes(n_tok):
    tm = MOE_TILE
    nt_max = (n_tok * TOP_K + N_EXP * (tm - 1)) // tm + 1
    ntp = _round_up(nt_max + MOE_GROUP, int(np.lcm(PLAN_ALIGN // tm, MOE_GROUP)))
    assert ntp <= META_W
    return ntp, _round_up(n_tok, PLAN_ALIGN)


def _plan_kernel(e_ref, r_ref, cnt_ref, pos_ref, meta_ref, *, n_tok):
    tm = MOE_TILE
    ntile = jnp.floor((cnt_ref[...] + (tm - 1.0)) * (1.0 / tm))
    ntb = ntile.astype(bf16)
    ei = lax.broadcasted_iota(i32, (N_EXP, LANES), 0)
    ej = lax.broadcasted_iota(i32, (N_EXP, LANES), 1)
    lower = jnp.where(ej <= ei, 1.0, 0.0)[:, :N_EXP].astype(bf16)
    tend = _dot(lower, ntb)
    tstart = tend - ntile
    tt = lax.broadcasted_iota(i32, (N_EXP, META_W), 1).astype(f32)
    te = jnp.sum(jnp.where(tt >= tend[:, 0:1], 1.0, 0.0), axis=0, keepdims=True)
    meta_ref[...] = jnp.zeros_like(meta_ref)
    meta_ref[0:1, :] = jnp.minimum(te, N_EXP - 1.0).astype(i32)
    meta_ref[1:2, :] = jnp.broadcast_to(tend[N_EXP - 1:N_EXP, 0:1], (1, META_W)).astype(i32)
    e = e_ref[...]
    base = jnp.zeros(e.shape, f32)
    for ex in range(N_EXP):
        base = jnp.where(e == ex, tstart[ex:ex + 1, 0:1] * float(tm), base)
    pos_ref[...] = jnp.zeros_like(pos_ref)
    pos_ref[:, 0:n_tok] = base.astype(i32) + r_ref[...]


def _plan(e_t, r_t, cnt):
    B, K, N = e_t.shape
    _, npad = _moe_sizes(N)
    return pl.pallas_call(
        functools.partial(_plan_kernel, n_tok=N),
        grid=(B,),
        in_specs=[pl.BlockSpec((None, K, N), lambda b: (b, 0, 0)),
                  pl.BlockSpec((None, K, N), lambda b: (b, 0, 0)),
                  pl.BlockSpec((None, N_EXP, LANES), lambda b: (b, 0, 0))],
        out_specs=[pl.BlockSpec((None, K, npad), lambda b: (b, 0, 0)),
                   pl.BlockSpec((None, SUBLANES, META_W), lambda b: (b, 0, 0))],
        out_shape=[jax.ShapeDtypeStruct((B, K, npad), i32), jax.ShapeDtypeStruct((B, SUBLANES, META_W), i32)],
        compiler_params=_cparams(("arbitrary",)),
        name="moe_plan",
    )(e_t, r_t, cnt)


def _plan_invert(pos, w_t, n_tok):
    B, K, npad = pos.shape
    ntp, _ = _moe_sizes(n_tok)
    plen = ntp * MOE_TILE
    nch = D // LANES
    mesh = plsc.VectorSubcoreMesh(core_axis_name="c", subcore_axis_name="s")
    n_cores = mesh.num_cores
    assert 2 * B <= n_cores * mesh.num_subcores and n_tok % SC_LANES == 0 and plen % SC_LANES == 0

    @functools.partial(
        pl.kernel, mesh=mesh,
        out_type=[jax.ShapeDtypeStruct((B * plen,), i32), jax.ShapeDtypeStruct((B * plen,), f32)],
        scratch_types=[pltpu.VMEM((npad,), i32), pltpu.VMEM((n_tok,), f32),
                       pltpu.VMEM((plen,), i32), pltpu.VMEM((plen,), f32)],
        compiler_params=dataclasses.replace(pltpu.CompilerParams(), needs_layout_passes=False))
    def invert(pos_hbm, w_hbm, rows_hbm, ws_hbm, pos_c, w_c, rows_v, ws_v):
        wid = lax.axis_index("s") * n_cores + lax.axis_index("c")
        b = wid % B
        lane = lax.iota(i32, SC_LANES)

        def load_pos(k):
            pltpu.sync_copy(pos_hbm.at[pl.ds(pl.multiple_of((b * K + k) * npad, SUBLANES), npad)], pos_c)

        @pl.when(wid < B)
        def _():
            pad = jnp.full((SC_LANES,), n_tok * nch, i32)

            @pl.loop(0, plen, step=SC_LANES)
            def _(i):
                rows_v[pl.ds(i, SC_LANES)] = pad

            for k in range(K):
                load_pos(k)

                @pl.loop(0, n_tok, step=SC_LANES)
                def _(n):
                    plsc.store_scatter(rows_v, [pos_c[pl.ds(n, SC_LANES)]], (lane + n) * nch)

            pltpu.sync_copy(rows_v, rows_hbm.at[pl.ds(pl.multiple_of(b * plen, SUBLANES), plen)])

        @pl.when(jnp.logical_and(wid >= B, wid < 2 * B))
        def _():
            zero = jnp.zeros((SC_LANES,), f32)

            @pl.loop(0, plen, step=SC_LANES)
            def _(i):
                ws_v[pl.ds(i, SC_LANES)] = zero

            for k in range(K):
                load_pos(k)
                pltpu.sync_copy(w_hbm.at[pl.ds(pl.multiple_of((b * K + k) * n_tok, SUBLANES), n_tok)], w_c)

                @pl.loop(0, n_tok, step=SC_LANES)
                def _(n):
                    plsc.store_scatter(ws_v, [pos_c[pl.ds(n, SC_LANES)]], w_c[pl.ds(n, SC_LANES)])

            pltpu.sync_copy(ws_v, ws_hbm.at[pl.ds(pl.multiple_of(b * plen, SUBLANES), plen)])

    return invert(pos.reshape(-1), w_t.reshape(-1))


def _moe_kernel(te_ref, nt_ref, *refs, ntp, n_tok, n_final):
    R = MOE_GROUP
    rg_ref, rs_ref, ws_ref, hlin_ref = refs[0:4]
    w_refs = refs[4:4 + R]
    final_refs = refs[4 + R:4 + R + n_final]
    out_ref, acc_ref = refs[4 + R + n_final:6 + R + n_final]
    bufs = refs[6 + R + n_final:]
    xs_refs = [bufs[0:R], bufs[R:2 * R]]
    ylin_refs = [bufs[2 * R:3 * R], bufs[3 * R:4 * R]]
    b = pl.program_id(0)
    t = pl.program_id(1)
    TM = MOE_TILE
    NCH = D // LANES
    U = 8
    TMP = TM + SUBLANES
    ngrp = ntp // R

    @pl.when(t == 0)
    def _():
        acc_ref[...] = jnp.zeros_like(acc_ref)

    @pl.when(jnp.logical_and(b == 0, t == 0))
    def _():
        for buf in bufs:
            buf[...] = jnp.zeros_like(buf)

    def gather_rows(p, r, c):
        for m in range(c * U, (c + 1) * U):
            off = pl.multiple_of(rg_ref[r * TM + m], NCH)
            xs_refs[p][r][pl.ds(m, NCH, stride=TMP), :] = hlin_ref[pl.ds(off, NCH), :]

    def scatter_rows(p, r, c):
        offs = [pl.multiple_of(rs_ref[r * TM + c * U + u], NCH) for u in range(U)]
        news = [acc_ref[pl.ds(offs[u], NCH), :] + ylin_refs[p][r][pl.ds((c * U + u) * NCH, NCH), :]
                for u in range(U)]
        for u in range(U):
            acc_ref[pl.ds(offs[u], NCH), :] = news[u]

    def stage(p):
        row_work = [functools.partial(fn, p, r, c) for c in range(TM // U) for r in range(R)
                    for fn in (gather_rows, scatter_rows)]
        n_pieces = R * (EXP_FF // LANES + D // MXU_N)
        per_piece = -(-len(row_work) // n_pieces)

        def deal():
            for fn in row_work[:per_piece]:
                fn()
            del row_work[:per_piece]

        for r in range(R):
            x = jnp.concatenate([xs_refs[1 - p][r][pl.ds(j * TMP, TM), :] for j in range(NCH)], axis=1).astype(bf16)
            wcol = jnp.broadcast_to(ws_ref[r:r + 1, :], (SUBLANES, TM)).T[:, 0:1]
            gate = _dot(x, w_refs[r][0:D, :EXP_FF])
            deal()
            up = _dot(x, w_refs[r][0:D, EXP_FF:])
            deal()
            a = (_silu(gate) * up * wcol).astype(bf16)
            per_half = 2 * EXP_FF // MXU_N
            for c in range(D // MXU_N):
                row0 = D + (c // per_half) * EXP_FF
                col0 = (c % per_half) * MXU_N
                y = _dot(a, w_refs[r][row0:row0 + EXP_FF, col0:col0 + MXU_N])
                for jj in range(MXU_N // LANES):
                    j = c * (MXU_N // LANES) + jj
                    ylin_refs[1 - p][r][pl.ds(j, TM, stride=NCH), :] = y[:, jj * LANES:(jj + 1) * LANES]
                deal()
        while row_work:
            deal()

    live = (t - 2) * R < nt_ref[b]
    pl.when(jnp.logical_and(live, t % 2 == 0))(functools.partial(stage, 0))
    pl.when(jnp.logical_and(live, t % 2 == 1))(functools.partial(stage, 1))

    @pl.when(t >= ngrp + 2)
    def _():
        row0 = (t - (ngrp + 2)) * (ROW_TILE * NCH)
        routed = [acc_ref[pl.ds(row0 + j, ROW_TILE, stride=NCH), :] for j in range(NCH)]
        if n_final:
            x_ref, fin_ref, shgu_ref, shd_ref, mod_ref = final_refs
            gate = mod_ref[...][:, 5 * D:6 * D]
            out_ref[...] = x_ref[...] + gate * (jnp.concatenate(routed, axis=1)
                                                + _shared_ffn(fin_ref[...], shgu_ref, shd_ref))
        else:
            for j in range(NCH):
                out_ref[:, j * LANES:(j + 1) * LANES] = routed[j].astype(bf16)


def _moe(te, nt, rows, wsort, hlin, w, layer, final=()):
    B = hlin.shape[0]
    NCH = D // LANES
    n_tok = hlin.shape[1] // NCH - ROW_TILE
    assert n_tok % ROW_TILE == 0
    nf = n_tok // ROW_TILE
    TM = MOE_TILE
    R = MOE_GROUP
    ntp, _ = _moe_sizes(n_tok)
    assert ntp % R == 0
    ngrp = ntp // R

    def group_of(b, t, nt_ref, lag):
        grp = t - lag
        ok = jnp.logical_and(t >= lag, grp * R < nt_ref[b])
        return b * ngrp + jnp.where(ok, grp, ngrp - 1)

    def rows_spec(lag):
        return pl.BlockSpec((R * TM,), lambda b, t, te_ref, nt_ref: (group_of(b, t, nt_ref, lag),),
                            memory_space=pltpu.SMEM)

    ws_spec = pl.BlockSpec((None, R, TM), lambda b, t, te_ref, nt_ref: (group_of(b, t, nt_ref, 1), 0, 0))

    def w_spec(r):
        def index(b, t, te_ref, nt_ref):
            tile = jnp.clip((t - 1) * R + r, 0, nt_ref[b] - 1)
            return (layer, te_ref[b * META_W + tile], 0, 0)
        return pl.BlockSpec((None, None) + tuple(w.shape[2:]), index)

    flush_tile = lambda t: jnp.maximum(t - (ngrp + 2), 0)
    flush_tok = lambda: pl.BlockSpec((None, ROW_TILE, D), lambda b, t, *_: (b, flush_tile(t), 0))
    final_specs = []
    if final:
        const = lambda a: pl.BlockSpec(a.shape, lambda b, t, *_: (0, 0), pipeline_mode=pl.Buffered(1))
        final_specs = [flush_tok(), flush_tok(), const(final[2]), const(final[3]),
                       pl.BlockSpec((None, None, 1, 6 * D), lambda b, t, *_: (b, 1, 0, 0))]

    grid_spec = pltpu.PrefetchScalarGridSpec(
        num_scalar_prefetch=2,
        grid=(B, ngrp + 2 + nf),
        in_specs=([rows_spec(0), rows_spec(2), ws_spec,
                   pl.BlockSpec((None, (n_tok + ROW_TILE) * NCH, LANES), lambda b, t, *_: (b, 0, 0),
                                pipeline_mode=pl.Buffered(1))]
                  + [w_spec(r) for r in range(R)] + final_specs),
        out_specs=flush_tok(),
        scratch_shapes=([pltpu.VMEM(((n_tok + SUBLANES) * NCH, LANES), f32)]
                        + [pltpu.VMEM(((TM + SUBLANES) * NCH, LANES), f32) for _ in range(2 * R)]
                        + [pltpu.VMEM((TM * NCH, LANES), f32) for _ in range(2 * R)]),
    )
    return pl.pallas_call(
        functools.partial(_moe_kernel, ntp=ntp, n_tok=n_tok, n_final=len(final)),
        grid_spec=grid_spec,
        out_shape=jax.ShapeDtypeStruct((B, n_tok, D), f32 if final else bf16),
        compiler_params=_cparams(("arbitrary", "arbitrary"), MOE_VMEM_LIMIT),
        name="moe",
    )(te, nt, rows, rows, wsort.reshape(B * ngrp, R, TM), hlin, *([w] * R), *final)


def _routed_experts(e_t, w_t, r_t, cnt, hlin, w, layer, final=()):
    N = e_t.shape[2]
    pos, meta = _plan(e_t, r_t, cnt)
    te = meta[:, 0, :].reshape(-1)
    nt = meta[:, 1, 0]
    rows, wsort = _plan_invert(pos, w_t, N)
    return _moe(te, nt, rows, wsort, hlin, w, layer, final)


def _shared_ffn(fin, shgu_ref, shd_ref):
    gu = _dot(fin, shgu_ref[...])
    return _dot((_silu(gu[:, :SH_FF]) * gu[:, SH_FF:]).astype(bf16), shd_ref[...])


def _mla_qkv(a, qan_ref, wqn_ref, wqr_ref, kvan_ref, wk_ref, wv_ref, qnn_ref, qnr_ref, knn_ref, knr_ref,
             cos_ref, sin_ref, q_ref, k_ref, v_ref):
    tm = a.shape[0]
    scale = MLA_QK ** -0.5 * float(np.log2(np.e))
    cos = cos_ref[...]
    sin = sin_ref[...]
    lane = lax.broadcasted_iota(i32, (tm, LANES), 1)
    first = (lane // (MLA_ROPE // 4)) % 2 == 0

    def rope(xb):
        sw = jnp.where(first, pltpu.roll(xb, LANES - MLA_ROPE // 4, axis=1), pltpu.roll(xb, MLA_ROPE // 4, axis=1))
        return xb * cos + sw * sin

    qa = (_rms(a[:, :MLA_QR]) * qan_ref[...]).astype(bf16)
    qn = _dot(qa, wqn_ref[...])
    yield
    qr = _dot(qa, wqr_ref[...])
    yield
    ri = lax.broadcasted_iota(i32, (LANES, LANES), 0) // MLA_ROPE
    ci = lax.broadcasted_iota(i32, (LANES, LANES), 1) // MLA_ROPE
    seg = jnp.where(ri == ci, 1.0, 0.0).astype(bf16)

    def seg_sum(sq):
        hi = sq.astype(bf16)
        r1 = sq - hi.astype(f32)
        mid = r1.astype(bf16)
        lo = (r1 - mid.astype(f32)).astype(bf16)
        return _dot(hi, seg) + _dot(mid, seg) + _dot(lo, seg)

    qr_blocks = []
    for p in range(MLA_H // 2):
        blk = qr[:, p * LANES:(p + 1) * LANES]
        blk = blk * lax.rsqrt(seg_sum(blk * blk) * (1.0 / MLA_ROPE) + EPS) * qnr_ref[:, p * LANES:(p + 1) * LANES]
        qr_blocks.append(rope(blk) * scale)
        yield

    kv = (_rms(a[:, MLA_QR:MLA_QR + MLA_KVR]) * kvan_ref[...]).astype(bf16)
    kn = _dot(kv, wk_ref[...])
    yield
    v_ref[...] = _dot_nt(wv_ref[...], kv).astype(bf16)
    yield
    kr = a[:, MLA_QR + MLA_KVR:MLA_IN_PAD]
    kr = rope(_rms(kr, MLA_ROPE) * knr_ref[...])
    kr_odd = pltpu.roll(kr, MLA_ROPE, axis=1)
    for hd in range(MLA_H):
        sl = slice(hd * MLA_NOPE, (hd + 1) * MLA_NOPE)
        q_ref[:, 2 * hd * LANES:(2 * hd + 1) * LANES] = (_rms(qn[:, sl]) * qnn_ref[...] * scale).astype(bf16)
        q_ref[:, (2 * hd + 1) * LANES:(2 * hd + 2) * LANES] = qr_blocks[hd // 2].astype(bf16)
        k_ref[:, 2 * hd * LANES:(2 * hd + 1) * LANES] = (_rms(kn[:, sl]) * knn_ref[...]).astype(bf16)
        k_ref[:, (2 * hd + 1) * LANES:(2 * hd + 2) * LANES] = (kr if hd % 2 == 0 else kr_odd).astype(bf16)
        if hd % 2 == 1:
            yield


def _ffn_mla_kernel(x1_ref, routed_ref, fin_ref, shgu_ref, shd_ref, mod0_ref, mod1_ref, g_ref, win_ref, *refs, nt):
    qkv_refs, (x2_ref, q_ref, k_ref, v_ref, a0_ref, a1_ref) = refs[:12], refs[12:]
    a_refs = (a0_ref, a1_ref)
    i = pl.program_id(1)

    def qkv(p):
        return _mla_qkv(a_refs[p][...], *qkv_refs, q_ref, k_ref, v_ref)

    def combine(p, other=()):
        other = iter(other)

        def advance(n):
            for _ in range(n):
                next(other, None)

        gu = _dot(fin_ref[...], shgu_ref[...])
        advance(3)
        shared = _dot((_silu(gu[:, :SH_FF]) * gu[:, SH_FF:]).astype(bf16), shd_ref[...])
        advance(3)
        x2 = x1_ref[...] + mod0_ref[...][:, 5 * D:6 * D] * (routed_ref[...] + shared)
        x2_ref[...] = x2
        advance(2)
        mod1 = mod1_ref[...]
        h = (_rms(x2) * g_ref[...]) * (1.0 + mod1[:, D:2 * D]) + mod1[:, 0:D]
        advance(2)
        a_refs[p][...] = _dot(h.astype(bf16), win_ref[...])
        for _ in other:
            pass

    pl.when(i == 0)(functools.partial(combine, 0))
    for p in range(2):
        @pl.when(jnp.logical_and(jnp.logical_and(i >= 1, i < nt), i % 2 == p))
        def _(p=p):
            combine(p, qkv(1 - p))

    @pl.when(i == nt)
    def _():
        for _ in qkv((nt - 1) % 2):
            pass


def _ffn_mla(x1, routed, fin, sh_gu, sh_d, modtab0, modtab1, gain, w_in, qkv_params, cos_t, sin_t, n_ctx_tiles):
    B, T, _ = x1.shape
    tm = ROW_TILE
    nt = T // tm
    last = lambda i: jnp.minimum(i, nt - 1)
    prev = lambda i: jnp.maximum(i - 1, 0)
    tok = lambda w: pl.BlockSpec((None, tm, w), lambda b, i: (b, last(i), 0))
    modspec = lambda: pl.BlockSpec((None, None, 1, 6 * D),
                                   lambda b, i: (b, jnp.where(last(i) < n_ctx_tiles, 0, 1), 0, 0))
    full = lambda r, c: pl.BlockSpec((r, c), lambda b, i: (0, 0))
    rope_spec = lambda: pl.BlockSpec((tm, LANES), lambda b, i: (prev(i), 0))
    hw = 2 * LANES * MLA_H
    return pl.pallas_call(
        functools.partial(_ffn_mla_kernel, nt=nt),
        grid=(B, nt + 1),
        in_specs=[tok(D), tok(D), tok(D), full(D, 2 * SH_FF), full(SH_FF, D), modspec(), modspec(),
                  full(1, D), full(D, MLA_IN_PAD),
                  full(1, MLA_QR), full(MLA_QR, MLA_H * MLA_NOPE), full(MLA_QR, MLA_H * MLA_ROPE),
                  full(1, MLA_KVR), full(MLA_KVR, MLA_H * MLA_NOPE), full(MLA_H * MLA_V, MLA_KVR),
                  full(1, MLA_NOPE), full(1, MLA_H * MLA_ROPE), full(1, MLA_NOPE), full(1, LANES),
                  rope_spec(), rope_spec()],
        out_specs=[tok(D),
                   pl.BlockSpec((None, tm, hw), lambda b, i: (b, jnp.maximum(prev(i) - n_ctx_tiles, 0), 0)),
                   pl.BlockSpec((None, tm, hw), lambda b, i: (b, prev(i), 0)),
                   pl.BlockSpec((None, MLA_H * MLA_V, tm), lambda b, i: (b, 0, prev(i)))],
        out_shape=[jax.ShapeDtypeStruct((B, T, D), f32),
                   jax.ShapeDtypeStruct((B, T - n_ctx_tiles * tm, hw), bf16), jax.ShapeDtypeStruct((B, T, hw), bf16),
                   jax.ShapeDtypeStruct((B, MLA_H * MLA_V, T), bf16)],
        scratch_shapes=[pltpu.VMEM((tm, MLA_IN_PAD), f32), pltpu.VMEM((tm, MLA_IN_PAD), f32)],
        compiler_params=_cparams(("arbitrary", "arbitrary")),
        name="ffn_mla",
    )(x1, routed, fin, sh_gu, sh_d, modtab0, modtab1, gain, w_in, *qkv_params, cos_t, sin_t)


ATT_TQ = 256


def _mla_attn_kernel(q_ref, k_ref, vt_ref, o_ref, s0_ref, s1_ref, m0_ref, m1_ref):
    tq = q_ref.shape[0]
    groups = k_ref.shape[0] // SUBLANES
    i = pl.program_id(0)
    s_refs, m_refs = (s0_ref, s1_ref), (m0_ref, m1_ref)

    @pl.when(i == 0)
    def _():
        for ref in s_refs + m_refs:
            ref[...] = jnp.zeros_like(ref)

    def stage(par):
        s = _dot_nt(k_ref[...], q_ref[...])
        s_refs[par][...] = s
        m_refs[par][...] = jnp.max(s.reshape(groups, SUBLANES, tq), axis=0)
        m = jnp.max(m_refs[1 - par][...], axis=0, keepdims=True)
        p = jnp.exp2(s_refs[1 - par][...] - m)
        lsum = jnp.sum(jnp.sum(p.reshape(groups, SUBLANES, tq), axis=0), axis=0, keepdims=True)
        acc = _dot(vt_ref[...], p.astype(bf16))
        o_ref[...] = (acc / lsum).astype(bf16)

    pl.when(i % 2 == 0)(functools.partial(stage, 0))
    pl.when(i % 2 == 1)(functools.partial(stage, 1))


def _mla_attn(q, k, vt):
    B, S, _ = q.shape
    T = k.shape[1]
    tq = ATT_TQ
    nq = S // tq
    ntile = B * MLA_H * nq
    assert S % tq == 0 and T % SUBLANES == 0

    def tile(g):
        g = jnp.clip(g, 0, ntile - 1)
        return g // (MLA_H * nq), (g // nq) % MLA_H, g % nq

    def q_index(g):
        b, h, i = tile(g)
        return b, i, h

    def k_index(g):
        b, h, _ = tile(g)
        return b, 0, h

    def vt_index(g):
        b, h, _ = tile(g - 1)
        return b, h, 0

    def o_index(g):
        b, h, i = tile(g - 1)
        return b, h, i

    return pl.pallas_call(
        _mla_attn_kernel,
        grid=(ntile + 1,),
        in_specs=[pl.BlockSpec((None, tq, 2 * LANES), q_index),
                  pl.BlockSpec((None, T, 2 * LANES), k_index),
                  pl.BlockSpec((None, MLA_V, T), vt_index)],
        out_specs=pl.BlockSpec((None, MLA_V, tq), o_index),
        out_shape=jax.ShapeDtypeStruct((B, MLA_H * MLA_V, S), bf16),
        scratch_shapes=[pltpu.VMEM((T, tq), f32), pltpu.VMEM((T, tq), f32),
                        pltpu.VMEM((SUBLANES, tq), f32), pltpu.VMEM((SUBLANES, tq), f32)],
        compiler_params=_cparams(("arbitrary",)),
        name="mla_attn",
    )(q, k, vt)


def _axial_angles(rows_n, rot_dim):
    axis_dim = rot_dim // 2
    inv = ROPE_BASE ** (-jnp.arange(0, axis_dim, 2, dtype=f32) / axis_dim)
    row = jnp.repeat(jnp.arange(rows_n, dtype=f32), GRID_W)
    col = jnp.tile(jnp.arange(GRID_W, dtype=f32), rows_n)
    return row[:, None] * inv, col[:, None] * inv


def _rope_tables(seq, n_ctx, rot_dim, reps):
    ang_r, ang_c = _axial_angles(seq // GRID_W, rot_dim)
    cos = jnp.concatenate([jnp.cos(ang_r)] * 2 + [jnp.cos(ang_c)] * 2, axis=1)
    sin = jnp.concatenate([-jnp.sin(ang_r), jnp.sin(ang_r), -jnp.sin(ang_c), jnp.sin(ang_c)], axis=1)
    cos = jnp.concatenate([jnp.ones((n_ctx, rot_dim), f32), cos], axis=0)
    sin = jnp.concatenate([jnp.zeros((n_ctx, rot_dim), f32), sin], axis=0)
    return jnp.tile(cos, (1, reps)), jnp.tile(sin, (1, reps))


def kernel(x, c, ctx, c_ctx, ada_w, ada_b, norm_mix, norm_ffn, ret_w_in, ret_decay_f, ret_decay_b, ret_w_o,
           mla_w_in, mla_q_a_norm, mla_w_q_b, mla_kv_a_norm, mla_w_kv_b, mla_q_norm, mla_k_norm, mla_w_o,
           router_w, router_bias, exp_w_gu, exp_w_down, sh_w_gu, sh_w_down):
    B, S, _ = x.shape
    n_ctx = ctx.shape[1]
    assert n_ctx % ROW_TILE == 0 and S % ROW_TILE == 0 and S % GRID_W == 0
    n_ctx_tiles = n_ctx // ROW_TILE

    rows = -(-(B + 1) // SUBLANES) * SUBLANES
    cc = jnp.zeros((rows, D), f32).at[:B].set(c).at[B].set(c_ctx)
    mod = _ada(cc, ada_w, ada_b)

    def modtab(i):
        ctx_row = jnp.broadcast_to(mod[i, B][None, :], (B, 6 * D))
        return jnp.stack([ctx_row, mod[i, :B]], axis=1)[:, :, None, :]

    mod0, mod1 = modtab(0), modtab(1)

    cos_r, sin_r = _rope_tables(S, n_ctx, RET_DK, 1)
    ride = exp_w_gu.shape[0] * exp_w_gu.shape[1] == _cast_steps(B, (n_ctx + S) // ROW_TILE) and D == 4 * EXP_FF
    proj = _ret_inproj(ctx, x, mod0, norm_mix[0][None, :], ret_w_in[0].astype(bf16), cos_r, sin_r, n_ctx_tiles,
                       (exp_w_gu, exp_w_down) if ride else ())
    q, k, v, gf, gb = proj[:5]
    if ride:
        exp_w = proj[5]
    else:
        exp_w = jnp.concatenate([exp_w_gu, exp_w_down[..., :2 * EXP_FF], exp_w_down[..., 2 * EXP_FF:]],
                                axis=2).astype(bf16)
    dtab = jnp.broadcast_to(jnp.concatenate([ret_decay_f[0], ret_decay_b[0]])[:, None], (2 * RET_H, LANES))
    o = _ret_scan(dtab, q, k, v, gf, gb, n_ctx)
    x1, fin, hlin, e_t, w_t, r_t, cnt = _post_mix(
        o, ret_w_o[0].astype(bf16), (ctx, x), 0, mod0, n_ctx_tiles, norm_ffn[0][None, :],
        router_w[0].T, router_bias[0][:, None])
    routed = _routed_experts(e_t, w_t, r_t, cnt, hlin, exp_w, 0)
    w_in1 = jnp.zeros((D, MLA_IN_PAD), f32).at[:, :mla_w_in.shape[2]].set(mla_w_in[0]).astype(bf16)

    wq = mla_w_q_b[0].reshape(MLA_QR, MLA_H, MLA_QK)
    wqn = wq[:, :, :MLA_NOPE].reshape(MLA_QR, MLA_H * MLA_NOPE).astype(bf16)
    wqr = wq[:, :, MLA_NOPE:].reshape(MLA_QR, MLA_H * MLA_ROPE).astype(bf16)
    wkv = mla_w_kv_b[0].reshape(MLA_KVR, MLA_H, MLA_NOPE + MLA_V)
    wk = wkv[:, :, :MLA_NOPE].reshape(MLA_KVR, MLA_H * MLA_NOPE).astype(bf16)
    wv = wkv[:, :, MLA_NOPE:].reshape(MLA_KVR, MLA_H * MLA_V).T.astype(bf16)
    qnn = mla_q_norm[0][None, :MLA_NOPE]
    qnr = jnp.tile(mla_q_norm[0][None, MLA_NOPE:], (1, MLA_H))
    knn = mla_k_norm[0][None, :MLA_NOPE]
    knr = jnp.concatenate([mla_k_norm[0][MLA_NOPE:], jnp.zeros((LANES - MLA_ROPE,), f32)])[None, :]
    cos_m, sin_m = _rope_tables(S, n_ctx, MLA_ROPE, LANES // MLA_ROPE)
    qkv_params = (mla_q_a_norm[0][None, :], wqn, wqr, mla_kv_a_norm[0][None, :], wk, wv, qnn, qnr, knn, knr)
    x2, qf, kf, vf = _ffn_mla(x1, routed, fin, sh_w_gu[0].astype(bf16), sh_w_down[0].astype(bf16), mod0, mod1,
                              norm_mix[1][None, :], w_in1, qkv_params, cos_m, sin_m, n_ctx_tiles)
    o1 = _mla_attn(qf, kf, vf)
    x3, fin1, hlin1, e1, w1, r1, cnt1 = _post_mix(
        o1, mla_w_o[0].astype(bf16), (x2,), n_ctx_tiles, mod1, 0, norm_ffn[1][None, :],
        router_w[1].T, router_bias[1][:, None], o_transposed=True)
    return _routed_experts(e1, w1, r1, cnt1, hlin1, exp_w, 1,
                           final=(x3, fin1, sh_w_gu[1].astype(bf16), sh_w_down[1].astype(bf16), mod1))
```

```python
import dataclasses
import functools

import jax
import jax.numpy as jnp
import numpy as np
from jax import lax
from jax.experimental import pallas as pl
from jax.experimental.pallas import tpu as pltpu
from jax.experimental.pallas import tpu_sc as plsc

f32 = jnp.float32
bf16 = jnp.bfloat16
i32 = jnp.int32

D = 1024
GRID_W = 64
EPS = 1e-6
ROPE_BASE = 10000.0
RET_H = 4
RET_DK = 256
RET_DV = 512
RET_VW = RET_H * RET_DV
RET_CHUNK = 256
MLA_H = 8
MLA_NOPE = 128
MLA_ROPE = 64
MLA_QK = MLA_NOPE + MLA_ROPE
MLA_V = 128
MLA_QR = 384
MLA_KVR = 256
MLA_IN_PAD = 768
N_EXP = 64
TOP_K = 8
N_GRP = 8
TOPK_GRP = 4
EXP_FF = 256
SH_FF = 256
ROUTED_SCALE = 2.5

LANES = 128
SUBLANES = 8
SC_LANES = 16
MXU_N = 256
ROW_TILE = 256
MOE_TILE = 256
MOE_GROUP = 2
ADA_COLS = 1536
PROJ_COLS = 512
VMEM_LIMIT = 56 * 1024 * 1024
MOE_VMEM_LIMIT = 62 * 1024 * 1024


def _cparams(sem, vmem=VMEM_LIMIT):
    return pltpu.CompilerParams(dimension_semantics=sem, vmem_limit_bytes=vmem)


def _sigmoid(x):
    return 1.0 / (1.0 + jnp.exp(-x))


def _silu(x):
    return x * _sigmoid(x)


def _rms(x, n=None):
    n = x.shape[-1] if n is None else n
    return x * lax.rsqrt(jnp.sum(x * x, axis=-1, keepdims=True) * (1.0 / n) + EPS)


def _dot(a, b):
    return jnp.dot(a, b, preferred_element_type=f32)


def _dot_nt(a, b):
    return lax.dot_general(a, b, (((1,), (1,)), ((), ())), preferred_element_type=f32)


def _ada_kernel(c_ref, w_ref, b_ref, o_ref):
    s = _silu(c_ref[...]).astype(bf16)
    o_ref[...] = _dot(s, w_ref[...].astype(bf16)) + b_ref[...]


def _ada(cc, ada_w, ada_b):
    depth = ada_w.shape[0]
    rows = cc.shape[0]
    tn = ADA_COLS
    return pl.pallas_call(
        _ada_kernel,
        grid=(depth, 6 * D // tn),
        in_specs=[pl.BlockSpec((rows, D), lambda i, j: (0, 0)),
                  pl.BlockSpec((None, D, tn), lambda i, j: (i, 0, j)),
                  pl.BlockSpec((None, 1, tn), lambda i, j: (i, 0, j))],
        out_specs=pl.BlockSpec((None, rows, tn), lambda i, j: (i, 0, j)),
        out_shape=jax.ShapeDtypeStruct((depth, rows, 6 * D), f32),
        compiler_params=_cparams(("arbitrary", "arbitrary")),
        name="ada",
    )(cc, ada_w, ada_b.reshape(depth, 1, 6 * D))


def _stream_tile(ctx_ref, x_ref, n_ctx_tiles):
    return jnp.where(pl.program_id(1) < n_ctx_tiles, ctx_ref[...], x_ref[...])


def _stream_specs(tm, n_ctx_tiles, last=None):
    clamp = (lambda i: i) if last is None else (lambda i: jnp.minimum(i, last))
    return [pl.BlockSpec((None, tm, D), lambda b, i: (b, jnp.minimum(clamp(i), n_ctx_tiles - 1), 0)),
            pl.BlockSpec((None, tm, D), lambda b, i: (b, jnp.maximum(clamp(i) - n_ctx_tiles, 0), 0))]


def _ret_inproj_kernel(ctx_ref, x_ref, mod_ref, g_ref, w_ref, cos_ref, sin_ref, *refs, n_ctx_tiles, n_cast):
    cast_in, cast_out = refs[:n_cast], refs[n_cast + 5:]
    q_ref, k_ref, v_ref, gf_ref, gb_ref = refs[n_cast:n_cast + 5]
    x = _stream_tile(ctx_ref, x_ref, n_ctx_tiles)
    mod = mod_ref[...]
    h = (_rms(x) * g_ref[...]) * (1.0 + mod[:, D:2 * D]) + mod[:, 0:D]
    hb = h.astype(bf16)
    cos = cos_ref[...]
    sin = sin_ref[...]

    def rope(a):
        outs = []
        for half in range(2):
            sl = slice(half * LANES, (half + 1) * LANES)
            ah = a[:, sl]
            outs.append(ah * cos[:, sl] + pltpu.roll(ah, LANES // 2, axis=1) * sin[:, sl])
        return jnp.concatenate(outs, axis=1)

    for hd in range(RET_H):
        sl = slice(hd * RET_DK, (hd + 1) * RET_DK)
        q_ref[:, sl] = rope(_dot(hb, w_ref[:, sl])).astype(bf16)
    if n_cast:
        (gu_ref, dn_ref), (wcat_ref,) = cast_in, cast_out
        wcat_ref[0:D, :] = gu_ref[...].astype(bf16)
        dn = dn_ref[...]
        wcat_ref[D:D + EXP_FF, :] = dn[:, :2 * EXP_FF].astype(bf16)
        wcat_ref[D + EXP_FF:D + 2 * EXP_FF, :] = dn[:, 2 * EXP_FF:].astype(bf16)
    for hd in range(RET_H):
        sl = slice(hd * RET_DK, (hd + 1) * RET_DK)
        wsl = slice(D + hd * RET_DK, D + (hd + 1) * RET_DK)
        k_ref[:, sl] = (rope(_dot(hb, w_ref[:, wsl])) * (RET_DK ** -0.5)).astype(bf16)
    cw = PROJ_COLS
    for c in range(RET_VW // cw):
        sl = slice(c * cw, (c + 1) * cw)
        v_ref[:, sl] = _dot(hb, w_ref[:, 2 * D + c * cw:2 * D + (c + 1) * cw]).astype(bf16)
        gf_ref[:, sl] = _silu(_dot(hb, w_ref[:, 2 * D + RET_VW + c * cw:2 * D + RET_VW + (c + 1) * cw])).astype(bf16)
        gb_ref[:, sl] = _silu(_dot(hb, w_ref[:, 2 * D + 2 * RET_VW + c * cw:2 * D + 2 * RET_VW + (c + 1) * cw])).astype(bf16)


def _cast_steps(batch, n_tiles):
    return batch * max(n_tiles - 1, 1)


def _packed_expert_shape(w_gu):
    n_l, n_e = w_gu.shape[:2]
    return n_l, n_e, D + 2 * EXP_FF, 2 * EXP_FF


def _ret_inproj(ctx, x, modtab, gain, w_in, cos_t, sin_t, n_ctx_tiles, riders=()):
    B = x.shape[0]
    T = ctx.shape[1] + x.shape[1]
    tm = ROW_TILE
    nt = T // tm
    n_in = w_in.shape[1]
    tok = lambda w: pl.BlockSpec((None, tm, w), lambda b, i: (b, i, 0))
    per_b = _cast_steps(B, nt) // B
    rider_out = []
    if riders:
        n_l, n_e = riders[0].shape[:2]
        assert n_l * n_e == per_b * B and D == 4 * EXP_FF
        rider_out = [jax.ShapeDtypeStruct(_packed_expert_shape(riders[0]), bf16)]

    def rider_spec(shape):
        def index(b, i):
            s = b * per_b + jnp.minimum(i, per_b - 1)
            return s // n_e, s % n_e, 0, 0
        return pl.BlockSpec((None, None) + tuple(shape[2:]), index)

    return pl.pallas_call(
        functools.partial(_ret_inproj_kernel, n_ctx_tiles=n_ctx_tiles, n_cast=len(riders)),
        grid=(B, nt),
        in_specs=_stream_specs(tm, n_ctx_tiles) + [
                  pl.BlockSpec((None, None, 1, 6 * D), lambda b, i: (b, jnp.where(i < n_ctx_tiles, 0, 1), 0, 0)),
                  pl.BlockSpec((1, D), lambda b, i: (0, 0)),
                  pl.BlockSpec((D, n_in), lambda b, i: (0, 0), pipeline_mode=pl.Buffered(1)),
                  pl.BlockSpec((tm, RET_DK), lambda b, i: (i, 0)),
                  pl.BlockSpec((tm, RET_DK), lambda b, i: (i, 0))] + [rider_spec(r.shape) for r in riders],
        out_specs=[tok(D), tok(D), tok(RET_VW), tok(RET_VW), tok(RET_VW)] + [rider_spec(r.shape) for r in rider_out],
        out_shape=[jax.ShapeDtypeStruct((B, T, D), bf16), jax.ShapeDtypeStruct((B, T, D), bf16),
                   jax.ShapeDtypeStruct((B, T, RET_VW), bf16), jax.ShapeDtypeStruct((B, T, RET_VW), bf16),
                   jax.ShapeDtypeStruct((B, T, RET_VW), bf16)] + rider_out,
        compiler_params=_cparams(("arbitrary", "arbitrary")),
        name="ret_inproj",
    )(ctx, x, modtab, gain, w_in, cos_t, sin_t, *riders)


def _ret_chunk_index(t, nc, ncc):
    u = t - nc
    back = jnp.where(u < ncc, ncc - 1 - u, nc - 1 - u + ncc)
    return jnp.where(t < nc, t, back)


def _ret_scan_kernel(dt_ref, q_ref, k_ref, v_ref, gf_ref, gb_ref, o_ref,
                     s_ref, of_ref, mask_ref, dq_ref, dk_ref, dc_ref, *, nc, ncc):
    t = pl.program_id(1)
    C = RET_CHUNK

    def init(direction):
        s_ref[...] = jnp.zeros_like(s_ref)
        ii = lax.broadcasted_iota(i32, (C, C), 0)
        jj = lax.broadcasted_iota(i32, (C, C), 1)
        rel = (ii - jj if direction == 0 else jj - ii).astype(f32)
        pos = lax.broadcasted_iota(i32, (C, 1), 0).astype(f32)
        for hd in range(RET_H):
            r = direction * RET_H + hd
            lg = -jnp.exp(dt_ref[r:r + 1, :])
            lg1 = lg[:, 0:1]
            mask_ref[hd] = jnp.where(rel >= 0, jnp.exp(lg1 * jnp.maximum(rel, 0.0)), 0.0)
            if direction == 0:
                dq_ref[hd] = jnp.exp(lg1 * (pos + 1.0))
                dk_ref[hd] = jnp.exp(lg1 * (C - 1.0 - pos))
            else:
                dq_ref[hd] = jnp.exp(lg1 * (C - pos))
                dk_ref[hd] = jnp.exp(lg1 * pos)
            dc_ref[hd] = jnp.exp(lg * float(C))

    pl.when(t == 0)(functools.partial(init, 0))
    pl.when(t == nc)(functools.partial(init, 1))

    row0 = pl.multiple_of(_ret_chunk_index(t, nc, ncc) * C, C)

    def step(forward):
        for hd in range(RET_H):
            ks = slice(hd * RET_DK, (hd + 1) * RET_DK)
            vs = slice(hd * RET_DV, (hd + 1) * RET_DV)
            qh = q_ref[:, ks]
            kh = k_ref[:, ks]
            vh = v_ref[:, vs]
            p = (_dot_nt(qh, kh) * mask_ref[hd]).astype(bf16)
            y = _dot(p, vh) + _dot(qh, s_ref[hd].astype(bf16)) * dq_ref[hd]
            kd = (kh.astype(f32) * dk_ref[hd]).astype(bf16)
            upd = lax.dot_general(kd, vh, (((0,), (0,)), ((), ())), preferred_element_type=f32)
            s_ref[hd] = s_ref[hd] * dc_ref[hd][0:1, 0:1] + upd
            yn = _rms(y)
            if forward:
                of_ref[pl.ds(row0, C), vs] = (gf_ref[:, vs].astype(f32) * yn).astype(bf16)
            else:
                o_ref[:, vs] = (of_ref[pl.ds(row0, C), vs].astype(f32) + gb_ref[:, vs].astype(f32) * yn).astype(bf16)

    pl.when(t < nc)(functools.partial(step, True))
    pl.when(t >= nc)(functools.partial(step, False))


def _ret_scan(dtab, q, k, v, gf, gb, n_ctx):
    B, T, _ = q.shape
    C = RET_CHUNK
    nc = T // C
    ncc = n_ctx // C
    cidx = functools.partial(_ret_chunk_index, nc=nc, ncc=ncc)
    first_back = ncc - 1
    return pl.pallas_call(
        functools.partial(_ret_scan_kernel, nc=nc, ncc=ncc),
        grid=(B, 2 * nc),
        in_specs=[pl.BlockSpec((2 * RET_H, LANES), lambda b, t: (0, 0)),
                  pl.BlockSpec((None, C, D), lambda b, t: (b, cidx(t), 0)),
                  pl.BlockSpec((None, C, D), lambda b, t: (b, cidx(t), 0)),
                  pl.BlockSpec((None, C, RET_VW), lambda b, t: (b, cidx(t), 0)),
                  pl.BlockSpec((None, C, RET_VW), lambda b, t: (b, jnp.where(t < nc, t, nc - 1), 0)),
                  pl.BlockSpec((None, C, RET_VW), lambda b, t: (b, jnp.where(t < nc, first_back, cidx(t)), 0))],
        out_specs=pl.BlockSpec((None, C, RET_VW), lambda b, t: (b, jnp.where(t < nc, first_back, cidx(t)), 0)),
        out_shape=jax.ShapeDtypeStruct((B, T, RET_VW), bf16),
        scratch_shapes=[pltpu.VMEM((RET_H, RET_DK, RET_DV), f32),
                        pltpu.VMEM((T, RET_VW), bf16),
                        pltpu.VMEM((RET_H, C, C), f32),
                        pltpu.VMEM((RET_H, C, 1), f32),
                        pltpu.VMEM((RET_H, C, 1), f32),
                        pltpu.VMEM((RET_H, 1, LANES), f32)],
        compiler_params=_cparams(("arbitrary", "arbitrary")),
        name="ret_scan",
    )(dtab, q, k, v, gf, gb)


def _route(f, rwt_ref, rb_ref, cnt_ref, e_ref, w_ref, r_ref):
    tm = f.shape[0]
    G = N_EXP // N_GRP
    logits = _dot_nt(rwt_ref[...].astype(bf16), f.astype(bf16))
    s = _sigmoid(logits)
    sel = s + rb_ref[...]
    mi = lax.broadcasted_iota(i32, (G, tm), 0)
    neg = -jnp.inf
    s_g = [s[g * G:(g + 1) * G, :] for g in range(N_GRP)]
    sel_g = [sel[g * G:(g + 1) * G, :] for g in range(N_GRP)]

    def first_max(a, ids, big):
        mx = jnp.max(a, axis=0, keepdims=True)
        ix = jnp.min(jnp.where(a == mx, ids, big), axis=0, keepdims=True)
        return mx, ix

    gscore = jnp.zeros((N_GRP, tm), f32)
    gi = lax.broadcasted_iota(i32, (N_GRP, tm), 0)
    for g in range(N_GRP):
        t1, i1 = first_max(sel_g[g], mi, G)
        t2 = jnp.max(jnp.where(mi == i1, neg, sel_g[g]), axis=0, keepdims=True)
        gscore = jnp.where(gi == g, t1 + t2, gscore)
    yield
    gmask = jnp.zeros((N_GRP, tm), i32)
    cur = gscore
    for _ in range(TOPK_GRP):
        _, ix = first_max(cur, gi, N_GRP)
        hit = gi == ix
        gmask = jnp.where(hit, 1, gmask)
        cur = jnp.where(hit, neg, cur)
    cand = [jnp.where(gmask[g:g + 1, :] > 0, sel_g[g], neg) for g in range(N_GRP)]
    ids = [mi + g * G for g in range(N_GRP)]

    def across(parts, op):
        acc = parts[0]
        for part in parts[1:]:
            acc = op(acc, part)
        return acc

    e_rows, w_rows = [], []
    for _ in range(TOP_K):
        mx = jnp.max(across(cand, jnp.maximum), axis=0, keepdims=True)
        ix = jnp.min(across([jnp.where(cand[g] == mx, ids[g], N_EXP) for g in range(N_GRP)], jnp.minimum),
                     axis=0, keepdims=True)
        hits = [ids[g] == ix for g in range(N_GRP)]
        cand = [jnp.where(hits[g], neg, cand[g]) for g in range(N_GRP)]
        wv = jnp.sum(across([jnp.where(hits[g], s_g[g], 0.0) for g in range(N_GRP)], jnp.add), axis=0, keepdims=True)
        e_rows.append(ix)
        w_rows.append(wv)
        yield
    wsum = w_rows[0]
    for r in range(1, TOP_K):
        wsum = wsum + w_rows[r]

    selm = [jnp.zeros((G, tm), f32) for _ in range(N_GRP)]
    for r in range(TOP_K):
        for g in range(N_GRP):
            selm[g] = jnp.where(ids[g] == e_rows[r], 1.0, selm[g])
    m_all = jnp.concatenate(selm, axis=0)
    ri = lax.broadcasted_iota(i32, (tm, tm), 0)
    ci = lax.broadcasted_iota(i32, (tm, tm), 1)
    upper = jnp.where(ri <= ci, 1.0, 0.0).astype(bf16)
    incl = _dot(m_all.astype(bf16), upper)
    carry = cnt_ref[:, 0:1]
    rank_all = carry + incl - m_all
    cnt_ref[...] = cnt_ref[...] + incl[:, tm - 1:tm]
    yield
    for r in range(TOP_K):
        rk = jnp.sum(across([jnp.where(ids[g] == e_rows[r], rank_all[g * G:(g + 1) * G, :], 0.0)
                             for g in range(N_GRP)], jnp.add), axis=0, keepdims=True)
        e_ref[r:r + 1, :] = e_rows[r]
        w_ref[r:r + 1, :] = w_rows[r] / wsum * ROUTED_SCALE
        r_ref[r:r + 1, :] = rk.astype(i32)


def _post_mix_kernel(o_ref, wo_ref, *refs, nt, o_transposed, split_ctx_tiles):
    n_resid = 2 if split_ctx_tiles else 1
    resid = refs[:n_resid]
    (mod_ref, g_ref, rwt_ref, rb_ref, x1_ref, fin_ref, hlin_ref, e_ref, w_ref, r_ref, cnt_ref,
     f0_ref, f1_ref) = refs[n_resid:]
    f_refs = (f0_ref, f1_ref)
    i = pl.program_id(1)

    @pl.when(i == 0)
    def _():
        cnt_ref[...] = jnp.zeros_like(cnt_ref)

    def route(p):
        return _route(f_refs[p][...], rwt_ref, rb_ref, cnt_ref, e_ref, w_ref, r_ref)

    def mix(p, other=()):
        other = iter(other)

        def advance(n):
            for _ in range(n):
                next(other, None)

        mod = mod_ref[...]
        x = _stream_tile(resid[0], resid[1], split_ctx_tiles) if split_ctx_tiles else resid[0][...]
        o = o_ref[...]
        cols = []
        for c in range(D // MXU_N):
            sl = slice(c * MXU_N, (c + 1) * MXU_N)
            if o_transposed:
                piece = lax.dot_general(o, wo_ref[:, sl], (((0,), (0,)), ((), ())), preferred_element_type=f32)
            else:
                piece = _dot(o, wo_ref[:, sl])
            cols.append(x[:, sl] + mod[:, 2 * D + c * MXU_N:2 * D + (c + 1) * MXU_N] * piece)
            advance(2)
        x1 = jnp.concatenate(cols, axis=1)
        x1_ref[...] = x1
        f = (_rms(x1) * g_ref[...]) * (1.0 + mod[:, 4 * D:5 * D]) + mod[:, 3 * D:4 * D]
        advance(2)
        fin_ref[...] = f.astype(bf16)
        tm = f.shape[0]
        for j in range(D // LANES):
            hlin_ref[pl.ds(j, tm, stride=D // LANES), :] = f[:, j * LANES:(j + 1) * LANES]
        f_refs[p][...] = f
        for _ in other:
            pass

    pl.when(i == 0)(functools.partial(mix, 0))
    for p in range(2):
        @pl.when(jnp.logical_and(jnp.logical_and(i >= 1, i < nt), i % 2 == p))
        def _(p=p):
            mix(p, route(1 - p))

    @pl.when(i == nt)
    def _():
        for _ in route((nt - 1) % 2):
            pass
        hlin_ref[...] = jnp.zeros_like(hlin_ref)


def _post_mix(o, w_o, resid, x_tile_off, modtab, n_ctx_tiles, gain, rwt, rb, o_transposed=False):
    if o_transposed:
        B, KO, N = o.shape
    else:
        B, N, KO = o.shape
    tm = ROW_TILE
    nt = N // tm
    last = lambda i: jnp.minimum(i, nt - 1)
    tok = lambda w: pl.BlockSpec((None, tm, w), lambda b, i: (b, last(i), 0))
    sel = lambda: pl.BlockSpec((None, TOP_K, tm), lambda b, i: (b, 0, jnp.maximum(i - 1, 0)))
    o_spec = (pl.BlockSpec((None, KO, tm), lambda b, i: (b, 0, last(i))) if o_transposed else tok(KO))
    split = len(resid) == 2
    resid_specs = (_stream_specs(tm, n_ctx_tiles, nt - 1) if split else
                   [pl.BlockSpec((None, tm, D), lambda b, i: (b, last(i) + x_tile_off, 0))])
    return pl.pallas_call(
        functools.partial(_post_mix_kernel, nt=nt, o_transposed=o_transposed,
                          split_ctx_tiles=n_ctx_tiles if split else 0),
        grid=(B, nt + 1),
        in_specs=[o_spec,
                  pl.BlockSpec((KO, D), lambda b, i: (0, 0))] + resid_specs + [
                  pl.BlockSpec((None, None, 1, 6 * D), lambda b, i: (b, jnp.where(last(i) < n_ctx_tiles, 0, 1), 0, 0)),
                  pl.BlockSpec((1, D), lambda b, i: (0, 0)),
                  pl.BlockSpec((N_EXP, D), lambda b, i: (0, 0)),
                  pl.BlockSpec((N_EXP, 1), lambda b, i: (0, 0))],
        out_specs=[tok(D), tok(D),
                   pl.BlockSpec((None, tm * (D // LANES), LANES), lambda b, i: (b, i, 0)),
                   sel(), sel(), sel(),
                   pl.BlockSpec((None, N_EXP, LANES), lambda b, i: (b, 0, 0))],
        out_shape=[jax.ShapeDtypeStruct((B, N, D), f32), jax.ShapeDtypeStruct((B, N, D), bf16),
                   jax.ShapeDtypeStruct((B, (N + tm) * (D // LANES), LANES), f32),
                   jax.ShapeDtypeStruct((B, TOP_K, N), i32), jax.ShapeDtypeStruct((B, TOP_K, N), f32),
                   jax.ShapeDtypeStruct((B, TOP_K, N), i32),
                   jax.ShapeDtypeStruct((B, N_EXP, LANES), f32)],
        scratch_shapes=[pltpu.VMEM((tm, D), f32), pltpu.VMEM((tm, D), f32)],
        compiler_params=_cparams(("arbitrary", "arbitrary")),
        name="post_mix",
    )(o, w_o, *resid, modtab, gain, rwt, rb)


META_W = 256
PLAN_ALIGN = 1024


def _round_up(n, m):
    return -(-n // m) * m


def _moe_sizes(n_tok):
    tm = MOE_TILE
    nt_max = (n_tok * TOP_K + N_EXP * (tm - 1)) // tm + 1
    ntp = _round_up(nt_max + MOE_GROUP, int(np.lcm(PLAN_ALIGN // tm, MOE_GROUP)))
    assert ntp <= META_W
    return ntp, _round_up(n_tok, PLAN_ALIGN)


def _plan_kernel(e_ref, r_ref, cnt_ref, pos_ref, meta_ref, *, n_tok):
    tm = MOE_TILE
    ntile = jnp.floor((cnt_ref[...] + (tm - 1.0)) * (1.0 / tm))
    ntb = ntile.astype(bf16)
    ei = lax.broadcasted_iota(i32, (N_EXP, LANES), 0)
    ej = lax.broadcasted_iota(i32, (N_EXP, LANES), 1)
    lower = jnp.where(ej <= ei, 1.0, 0.0)[:, :N_EXP].astype(bf16)
    tend = _dot(lower, ntb)
    tstart = tend - ntile
    tt = lax.broadcasted_iota(i32, (N_EXP, META_W), 1).astype(f32)
    te = jnp.sum(jnp.where(tt >= tend[:, 0:1], 1.0, 0.0), axis=0, keepdims=True)
    meta_ref[...] = jnp.zeros_like(meta_ref)
    meta_ref[0:1, :] = jnp.minimum(te, N_EXP - 1.0).astype(i32)
    meta_ref[1:2, :] = jnp.broadcast_to(tend[N_EXP - 1:N_EXP, 0:1], (1, META_W)).astype(i32)
    e = e_ref[...]
    base = jnp.zeros(e.shape, f32)
    for ex in range(N_EXP):
        base = jnp.where(e == ex, tstart[ex:ex + 1, 0:1] * float(tm), base)
    pos_ref[...] = jnp.zeros_like(pos_ref)
    pos_ref[:, 0:n_tok] = base.astype(i32) + r_ref[...]


def _plan(e_t, r_t, cnt):
    B, K, N = e_t.shape
    _, npad = _moe_sizes(N)
    return pl.pallas_call(
        functools.partial(_plan_kernel, n_tok=N),
        grid=(B,),
        in_specs=[pl.BlockSpec((None, K, N), lambda b: (b, 0, 0)),
                  pl.BlockSpec((None, K, N), lambda b: (b, 0, 0)),
                  pl.BlockSpec((None, N_EXP, LANES), lambda b: (b, 0, 0))],
        out_specs=[pl.BlockSpec((None, K, npad), lambda b: (b, 0, 0)),
                   pl.BlockSpec((None, SUBLANES, META_W), lambda b: (b, 0, 0))],
        out_shape=[jax.ShapeDtypeStruct((B, K, npad), i32), jax.ShapeDtypeStruct((B, SUBLANES, META_W), i32)],
        compiler_params=_cparams(("arbitrary",)),
        name="moe_plan",
    )(e_t, r_t, cnt)


def _plan_invert(pos, w_t, n_tok):
    B, K, npad = pos.shape
    ntp, _ = _moe_sizes(n_tok)
    plen = ntp * MOE_TILE
    nch = D // LANES
    mesh = plsc.VectorSubcoreMesh(core_axis_name="c", subcore_axis_name="s")
    n_cores = mesh.num_cores
    assert 2 * B <= n_cores * mesh.num_subcores and n_tok % SC_LANES == 0 and plen % SC_LANES == 0

    @functools.partial(
        pl.kernel, mesh=mesh,
        out_type=[jax.ShapeDtypeStruct((B * plen,), i32), jax.ShapeDtypeStruct((B * plen,), f32)],
        scratch_types=[pltpu.VMEM((npad,), i32), pltpu.VMEM((n_tok,), f32),
                       pltpu.VMEM((plen,), i32), pltpu.VMEM((plen,), f32)],
        compiler_params=dataclasses.replace(pltpu.CompilerParams(), needs_layout_passes=False))
    def invert(pos_hbm, w_hbm, rows_hbm, ws_hbm, pos_c, w_c, rows_v, ws_v):
        wid = lax.axis_index("s") * n_cores + lax.axis_index("c")
        b = wid % B
        lane = lax.iota(i32, SC_LANES)

        def load_pos(k):
            pltpu.sync_copy(pos_hbm.at[pl.ds(pl.multiple_of((b * K + k) * npad, SUBLANES), npad)], pos_c)

        @pl.when(wid < B)
        def _():
            pad = jnp.full((SC_LANES,), n_tok * nch, i32)

            @pl.loop(0, plen, step=SC_LANES)
            def _(i):
                rows_v[pl.ds(i, SC_LANES)] = pad

            for k in range(K):
                load_pos(k)

                @pl.loop(0, n_tok, step=SC_LANES)
                def _(n):
                    plsc.store_scatter(rows_v, [pos_c[pl.ds(n, SC_LANES)]], (lane + n) * nch)

            pltpu.sync_copy(rows_v, rows_hbm.at[pl.ds(pl.multiple_of(b * plen, SUBLANES), plen)])

        @pl.when(jnp.logical_and(wid >= B, wid < 2 * B))
        def _():
            zero = jnp.zeros((SC_LANES,), f32)

            @pl.loop(0, plen, step=SC_LANES)
            def _(i):
                ws_v[pl.ds(i, SC_LANES)] = zero

            for k in range(K):
                load_pos(k)
                pltpu.sync_copy(w_hbm.at[pl.ds(pl.multiple_of((b * K + k) * n_tok, SUBLANES), n_tok)], w_c)

                @pl.loop(0, n_tok, step=SC_LANES)
                def _(n):
                    plsc.store_scatter(ws_v, [pos_c[pl.ds(n, SC_LANES)]], w_c[pl.ds(n, SC_LANES)])

            pltpu.sync_copy(ws_v, ws_hbm.at[pl.ds(pl.multiple_of(b * plen, SUBLANES), plen)])

    return invert(pos.reshape(-1), w_t.reshape(-1))


def _moe_kernel(te_ref, nt_ref, *refs, ntp, n_tok, n_final, layer):
    R = MOE_GROUP
    rg_ref, rs_ref, ws_ref, hlin_ref = refs[0:4]
    w_hbm = refs[4]
    final_refs = refs[5:5 + n_final]
    out_ref, acc_ref, wbuf, sem, st_ref = refs[5 + n_final:10 + n_final]
    bufs = refs[10 + n_final:]
    xs_refs = [bufs[0:R], bufs[R:2 * R]]
    ylin_refs = [bufs[2 * R:3 * R], bufs[3 * R:4 * R]]
    NS = 2 * R
    b = pl.program_id(0)
    t = pl.program_id(1)
    TM = MOE_TILE
    NCH = D // LANES
    U = 8
    TMP = TM + SUBLANES
    ngrp = ntp // R

    @pl.when(t == 0)
    def _():
        acc_ref[...] = jnp.zeros_like(acc_ref)

    @pl.when(jnp.logical_and(b == 0, t == 0))
    def _():
        for buf in bufs:
            buf[...] = jnp.zeros_like(buf)

    def gather_rows(p, r, c):
        for m in range(c * U, (c + 1) * U):
            off = pl.multiple_of(rg_ref[r * TM + m], NCH)
            xs_refs[p][r][pl.ds(m, NCH, stride=TMP), :] = hlin_ref[pl.ds(off, NCH), :]

    def scatter_rows(p, r, c):
        offs = [pl.multiple_of(rs_ref[r * TM + c * U + u], NCH) for u in range(U)]
        news = [acc_ref[pl.ds(offs[u], NCH), :] + ylin_refs[p][r][pl.ds((c * U + u) * NCH, NCH), :]
                for u in range(U)]
        for u in range(U):
            acc_ref[pl.ds(offs[u], NCH), :] = news[u]

    def stage(p):
        row_work = [functools.partial(fn, p, r, c) for c in range(TM // U) for r in range(R)
                    for fn in (gather_rows, scatter_rows)]
        n_pieces = R * (EXP_FF // LANES + D // MXU_N)
        per_piece = -(-len(row_work) // n_pieces)

        def deal():
            for fn in row_work[:per_piece]:
                fn()
            del row_work[:per_piece]

        for r in range(R):
            x = jnp.concatenate([xs_refs[1 - p][r][pl.ds(j * TMP, TM), :] for j in range(NCH)], axis=1).astype(bf16)
            wcol = jnp.broadcast_to(ws_ref[r:r + 1, :], (SUBLANES, TM)).T[:, 0:1]
            w_ref = wbuf.at[st_ref[2 + r]]
            gate = _dot(x, w_ref[0:D, :EXP_FF])
            deal()
            up = _dot(x, w_ref[0:D, EXP_FF:])
            deal()
            a = (_silu(gate) * up * wcol).astype(bf16)
            per_half = 2 * EXP_FF // MXU_N
            for c in range(D // MXU_N):
                row0 = D + (c // per_half) * EXP_FF
                col0 = (c % per_half) * MXU_N
                y = _dot(a, w_ref[row0:row0 + EXP_FF, col0:col0 + MXU_N])
                for jj in range(MXU_N // LANES):
                    j = c * (MXU_N // LANES) + jj
                    ylin_refs[1 - p][r][pl.ds(j, TM, stride=NCH), :] = y[:, jj * LANES:(jj + 1) * LANES]
                deal()
        while row_work:
            deal()

    live = (t - 2) * R < nt_ref[b]

    @pl.when(live)
    def _():
        last = nt_ref[b] - 1
        expert = lambda k: te_ref[b * META_W + jnp.clip(k, 0, last)]

        def copy(e, slot):
            return pltpu.make_async_copy(w_hbm.at[layer, e], wbuf.at[slot], sem.at[slot])

        @pl.when(t == 0)
        def _():
            copy(expert(0), 0).start()
            st_ref[0] = 0
            st_ref[1] = 1
            copy(0, 0).wait()

        k0 = (t - 1) * R
        for r in range(R):
            @pl.when(expert(k0 + r) != expert(k0 + r - 1))
            def _():
                run = st_ref[0] + 1
                st_ref[0] = run
                copy(0, run % NS).wait()
            st_ref[2 + r] = st_ref[0] % NS
        for r in range(R):
            e = expert(k0 + R + r)

            @pl.when(e != expert(k0 + R + r - 1))
            def _():
                n = st_ref[1]
                copy(e, n % NS).start()
                st_ref[1] = n + 1

    pl.when(jnp.logical_and(live, t % 2 == 0))(functools.partial(stage, 0))
    pl.when(jnp.logical_and(live, t % 2 == 1))(functools.partial(stage, 1))

    @pl.when(t >= ngrp + 2)
    def _():
        row0 = (t - (ngrp + 2)) * (ROW_TILE * NCH)
        routed = [acc_ref[pl.ds(row0 + j, ROW_TILE, stride=NCH), :] for j in range(NCH)]
        if n_final:
            x_ref, fin_ref, shgu_ref, shd_ref, mod_ref = final_refs
            gate = mod_ref[...][:, 5 * D:6 * D]
            out_ref[...] = x_ref[...] + gate * (jnp.concatenate(routed, axis=1)
                                                + _shared_ffn(fin_ref[...], shgu_ref, shd_ref))
        else:
            for j in range(NCH):
                out_ref[:, j * LANES:(j + 1) * LANES] = routed[j].astype(bf16)


def _moe(te, nt, rows, wsort, hlin, w, layer, final=()):
    B = hlin.shape[0]
    NCH = D // LANES
    n_tok = hlin.shape[1] // NCH - ROW_TILE
    assert n_tok % ROW_TILE == 0
    nf = n_tok // ROW_TILE
    TM = MOE_TILE
    R = MOE_GROUP
    ntp, _ = _moe_sizes(n_tok)
    assert ntp % R == 0
    ngrp = ntp // R

    def group_of(b, t, nt_ref, lag):
        grp = t - lag
        ok = jnp.logical_and(t >= lag, grp * R < nt_ref[b])
        return b * ngrp + jnp.where(ok, grp, ngrp - 1)

    def rows_spec(lag):
        return pl.BlockSpec((R * TM,), lambda b, t, te_ref, nt_ref: (group_of(b, t, nt_ref, lag),),
                            memory_space=pltpu.SMEM)

    ws_spec = pl.BlockSpec((None, R, TM), lambda b, t, te_ref, nt_ref: (group_of(b, t, nt_ref, 1), 0, 0))

    flush_tile = lambda t: jnp.maximum(t - (ngrp + 2), 0)
    flush_tok = lambda: pl.BlockSpec((None, ROW_TILE, D), lambda b, t, *_: (b, flush_tile(t), 0))
    final_specs = []
    if final:
        const = lambda a: pl.BlockSpec(a.shape, lambda b, t, *_: (0, 0), pipeline_mode=pl.Buffered(1))
        final_specs = [flush_tok(), flush_tok(), const(final[2]), const(final[3]),
                       pl.BlockSpec((None, None, 1, 6 * D), lambda b, t, *_: (b, 1, 0, 0))]

    grid_spec = pltpu.PrefetchScalarGridSpec(
        num_scalar_prefetch=2,
        grid=(B, ngrp + 2 + nf),
        in_specs=([rows_spec(0), rows_spec(2), ws_spec,
                   pl.BlockSpec((None, (n_tok + ROW_TILE) * NCH, LANES), lambda b, t, *_: (b, 0, 0),
                                pipeline_mode=pl.Buffered(1))]
                  + [pl.BlockSpec(memory_space=pl.ANY)] + final_specs),
        out_specs=flush_tok(),
        scratch_shapes=([pltpu.VMEM(((n_tok + SUBLANES) * NCH, LANES), f32),
                         pltpu.VMEM((2 * R,) + tuple(w.shape[2:]), bf16),
                         pltpu.SemaphoreType.DMA((2 * R,)),
                         pltpu.SMEM((2 + R,), i32)]
                        + [pltpu.VMEM(((TM + SUBLANES) * NCH, LANES), f32) for _ in range(2 * R)]
                        + [pltpu.VMEM((TM * NCH, LANES), f32) for _ in range(2 * R)]),
    )
    return pl.pallas_call(
        functools.partial(_moe_kernel, ntp=ntp, n_tok=n_tok, n_final=len(final), layer=layer),
        grid_spec=grid_spec,
        out_shape=jax.ShapeDtypeStruct((B, n_tok, D), f32 if final else bf16),
        compiler_params=_cparams(("arbitrary", "arbitrary"), MOE_VMEM_LIMIT),
        name="moe",
    )(te, nt, rows, rows, wsort.reshape(B * ngrp, R, TM), hlin, w, *final)


def _routed_experts(e_t, w_t, r_t, cnt, hlin, w, layer, final=()):
    N = e_t.shape[2]
    pos, meta = _plan(e_t, r_t, cnt)
    te = meta[:, 0, :].reshape(-1)
    nt = meta[:, 1, 0]
    rows, wsort = _plan_invert(pos, w_t, N)
    return _moe(te, nt, rows, wsort, hlin, w, layer, final)


def _shared_ffn(fin, shgu_ref, shd_ref):
    gu = _dot(fin, shgu_ref[...])
    return _dot((_silu(gu[:, :SH_FF]) * gu[:, SH_FF:]).astype(bf16), shd_ref[...])


def _mla_qkv(a, qan_ref, wqn_ref, wqr_ref, kvan_ref, wk_ref, wv_ref, qnn_ref, qnr_ref, knn_ref, knr_ref,
             cos_ref, sin_ref, q_ref, k_ref, v_ref):
    tm = a.shape[0]
    scale = MLA_QK ** -0.5 * float(np.log2(np.e))
    cos = cos_ref[...]
    sin = sin_ref[...]
    lane = lax.broadcasted_iota(i32, (tm, LANES), 1)
    first = (lane // (MLA_ROPE // 4)) % 2 == 0

    def rope(xb):
        sw = jnp.where(first, pltpu.roll(xb, LANES - MLA_ROPE // 4, axis=1), pltpu.roll(xb, MLA_ROPE // 4, axis=1))
        return xb * cos + sw * sin

    qa = (_rms(a[:, :MLA_QR]) * qan_ref[...]).astype(bf16)
    qn = _dot(qa, wqn_ref[...])
    yield
    qr = _dot(qa, wqr_ref[...])
    yield
    ri = lax.broadcasted_iota(i32, (LANES, LANES), 0) // MLA_ROPE
    ci = lax.broadcasted_iota(i32, (LANES, LANES), 1) // MLA_ROPE
    seg = jnp.where(ri == ci, 1.0, 0.0).astype(bf16)

    def seg_sum(sq):
        hi = sq.astype(bf16)
        r1 = sq - hi.astype(f32)
        mid = r1.astype(bf16)
        lo = (r1 - mid.astype(f32)).astype(bf16)
        return _dot(hi, seg) + _dot(mid, seg) + _dot(lo, seg)

    qr_blocks = []
    for p in range(MLA_H // 2):
        blk = qr[:, p * LANES:(p + 1) * LANES]
        blk = blk * lax.rsqrt(seg_sum(blk * blk) * (1.0 / MLA_ROPE) + EPS) * qnr_ref[:, p * LANES:(p + 1) * LANES]
        qr_blocks.append(rope(blk) * scale)
        yield

    kv = (_rms(a[:, MLA_QR:MLA_QR + MLA_KVR]) * kvan_ref[...]).astype(bf16)
    kn = _dot(kv, wk_ref[...])
    yield
    v_ref[...] = _dot_nt(wv_ref[...], kv).astype(bf16)
    yield
    kr = a[:, MLA_QR + MLA_KVR:MLA_IN_PAD]
    kr = rope(_rms(kr, MLA_ROPE) * knr_ref[...])
    kr_odd = pltpu.roll(kr, MLA_ROPE, axis=1)
    for hd in range(MLA_H):
        sl = slice(hd * MLA_NOPE, (hd + 1) * MLA_NOPE)
        q_ref[:, 2 * hd * LANES:(2 * hd + 1) * LANES] = (_rms(qn[:, sl]) * qnn_ref[...] * scale).astype(bf16)
        q_ref[:, (2 * hd + 1) * LANES:(2 * hd + 2) * LANES] = qr_blocks[hd // 2].astype(bf16)
        k_ref[:, 2 * hd * LANES:(2 * hd + 1) * LANES] = (_rms(kn[:, sl]) * knn_ref[...]).astype(bf16)
        k_ref[:, (2 * hd + 1) * LANES:(2 * hd + 2) * LANES] = (kr if hd % 2 == 0 else kr_odd).astype(bf16)
        if hd % 2 == 1:
            yield


def _ffn_mla_kernel(x1_ref, routed_ref, fin_ref, shgu_ref, shd_ref, mod0_ref, mod1_ref, g_ref, win_ref, *refs, nt):
    qkv_refs, (x2_ref, q_ref, k_ref, v_ref, a0_ref, a1_ref) = refs[:12], refs[12:]
    a_refs = (a0_ref, a1_ref)
    i = pl.program_id(1)

    def qkv(p):
        return _mla_qkv(a_refs[p][...], *qkv_refs, q_ref, k_ref, v_ref)

    def combine(p, other=()):
        other = iter(other)

        def advance(n):
            for _ in range(n):
                next(other, None)

        gu = _dot(fin_ref[...], shgu_ref[...])
        advance(3)
        shared = _dot((_silu(gu[:, :SH_FF]) * gu[:, SH_FF:]).astype(bf16), shd_ref[...])
        advance(3)
        x2 = x1_ref[...] + mod0_ref[...][:, 5 * D:6 * D] * (routed_ref[...] + shared)
        x2_ref[...] = x2
        advance(2)
        mod1 = mod1_ref[...]
        h = (_rms(x2) * g_ref[...]) * (1.0 + mod1[:, D:2 * D]) + mod1[:, 0:D]
        advance(2)
        a_refs[p][...] = _dot(h.astype(bf16), win_ref[...])
        for _ in other:
            pass

    pl.when(i == 0)(functools.partial(combine, 0))
    for p in range(2):
        @pl.when(jnp.logical_and(jnp.logical_and(i >= 1, i < nt), i % 2 == p))
        def _(p=p):
            combine(p, qkv(1 - p))

    @pl.when(i == nt)
    def _():
        for _ in qkv((nt - 1) % 2):
            pass


def _ffn_mla(x1, routed, fin, sh_gu, sh_d, modtab0, modtab1, gain, w_in, qkv_params, cos_t, sin_t, n_ctx_tiles):
    B, T, _ = x1.shape
    tm = ROW_TILE
    nt = T // tm
    last = lambda i: jnp.minimum(i, nt - 1)
    prev = lambda i: jnp.maximum(i - 1, 0)
    tok = lambda w: pl.BlockSpec((None, tm, w), lambda b, i: (b, last(i), 0))
    modspec = lambda: pl.BlockSpec((None, None, 1, 6 * D),
                                   lambda b, i: (b, jnp.where(last(i) < n_ctx_tiles, 0, 1), 0, 0))
    full = lambda r, c: pl.BlockSpec((r, c), lambda b, i: (0, 0))
    rope_spec = lambda: pl.BlockSpec((tm, LANES), lambda b, i: (prev(i), 0))
    hw = 2 * LANES * MLA_H
    return pl.pallas_call(
        functools.partial(_ffn_mla_kernel, nt=nt),
        grid=(B, nt + 1),
        in_specs=[tok(D), tok(D), tok(D), full(D, 2 * SH_FF), full(SH_FF, D), modspec(), modspec(),
                  full(1, D), full(D, MLA_IN_PAD),
                  full(1, MLA_QR), full(MLA_QR, MLA_H * MLA_NOPE), full(MLA_QR, MLA_H * MLA_ROPE),
                  full(1, MLA_KVR), full(MLA_KVR, MLA_H * MLA_NOPE), full(MLA_H * MLA_V, MLA_KVR),
                  full(1, MLA_NOPE), full(1, MLA_H * MLA_ROPE), full(1, MLA_NOPE), full(1, LANES),
                  rope_spec(), rope_spec()],
        out_specs=[tok(D),
                   pl.BlockSpec((None, tm, hw), lambda b, i: (b, jnp.maximum(prev(i) - n_ctx_tiles, 0), 0)),
                   pl.BlockSpec((None, tm, hw), lambda b, i: (b, prev(i), 0)),
                   pl.BlockSpec((None, MLA_H * MLA_V, tm), lambda b, i: (b, 0, prev(i)))],
        out_shape=[jax.ShapeDtypeStruct((B, T, D), f32),
                   jax.ShapeDtypeStruct((B, T - n_ctx_tiles * tm, hw), bf16), jax.ShapeDtypeStruct((B, T, hw), bf16),
                   jax.ShapeDtypeStruct((B, MLA_H * MLA_V, T), bf16)],
        scratch_shapes=[pltpu.VMEM((tm, MLA_IN_PAD), f32), pltpu.VMEM((tm, MLA_IN_PAD), f32)],
        compiler_params=_cparams(("arbitrary", "arbitrary")),
        name="ffn_mla",
    )(x1, routed, fin, sh_gu, sh_d, modtab0, modtab1, gain, w_in, *qkv_params, cos_t, sin_t)


ATT_TQ = 256


def _mla_attn_kernel(q_ref, k_ref, vt_ref, o_ref, s0_ref, s1_ref, m0_ref, m1_ref):
    tq = q_ref.shape[0]
    groups = k_ref.shape[0] // SUBLANES
    i = pl.program_id(0)
    s_refs, m_refs = (s0_ref, s1_ref), (m0_ref, m1_ref)

    @pl.when(i == 0)
    def _():
        for ref in s_refs + m_refs:
            ref[...] = jnp.zeros_like(ref)

    def stage(par):
        s = _dot_nt(k_ref[...], q_ref[...])
        s_refs[par][...] = s
        m_refs[par][...] = jnp.max(s.reshape(groups, SUBLANES, tq), axis=0)
        m = jnp.max(m_refs[1 - par][...], axis=0, keepdims=True)
        p = jnp.exp2(s_refs[1 - par][...] - m)
        lsum = jnp.sum(jnp.sum(p.reshape(groups, SUBLANES, tq), axis=0), axis=0, keepdims=True)
        acc = _dot(vt_ref[...], p.astype(bf16))
        o_ref[...] = (acc / lsum).astype(bf16)

    pl.when(i % 2 == 0)(functools.partial(stage, 0))
    pl.when(i % 2 == 1)(functools.partial(stage, 1))


def _mla_attn(q, k, vt):
    B, S, _ = q.shape
    T = k.shape[1]
    tq = ATT_TQ
    nq = S // tq
    ntile = B * MLA_H * nq
    assert S % tq == 0 and T % SUBLANES == 0

    def tile(g):
        g = jnp.clip(g, 0, ntile - 1)
        return g // (MLA_H * nq), (g // nq) % MLA_H, g % nq

    def q_index(g):
        b, h, i = tile(g)
        return b, i, h

    def k_index(g):
        b, h, _ = tile(g)
        return b, 0, h

    def vt_index(g):
        b, h, _ = tile(g - 1)
        return b, h, 0

    def o_index(g):
        b, h, i = tile(g - 1)
        return b, h, i

    return pl.pallas_call(
        _mla_attn_kernel,
        grid=(ntile + 1,),
        in_specs=[pl.BlockSpec((None, tq, 2 * LANES), q_index),
                  pl.BlockSpec((None, T, 2 * LANES), k_index),
                  pl.BlockSpec((None, MLA_V, T), vt_index)],
        out_specs=pl.BlockSpec((None, MLA_V, tq), o_index),
        out_shape=jax.ShapeDtypeStruct((B, MLA_H * MLA_V, S), bf16),
        scratch_shapes=[pltpu.VMEM((T, tq), f32), pltpu.VMEM((T, tq), f32),
                        pltpu.VMEM((SUBLANES, tq), f32), pltpu.VMEM((SUBLANES, tq), f32)],
        compiler_params=_cparams(("arbitrary",)),
        name="mla_attn",
    )(q, k, vt)


def _axial_angles(rows_n, rot_dim):
    axis_dim = rot_dim // 2
    inv = ROPE_BASE ** (-jnp.arange(0, axis_dim, 2, dtype=f32) / axis_dim)
    row = jnp.repeat(jnp.arange(rows_n, dtype=f32), GRID_W)
    col = jnp.tile(jnp.arange(GRID_W, dtype=f32), rows_n)
    return row[:, None] * inv, col[:, None] * inv


def _rope_tables(seq, n_ctx, rot_dim, reps):
    ang_r, ang_c = _axial_angles(seq // GRID_W, rot_dim)
    cos = jnp.concatenate([jnp.cos(ang_r)] * 2 + [jnp.cos(ang_c)] * 2, axis=1)
    sin = jnp.concatenate([-jnp.sin(ang_r), jnp.sin(ang_r), -jnp.sin(ang_c), jnp.sin(ang_c)], axis=1)
    cos = jnp.concatenate([jnp.ones((n_ctx, rot_dim), f32), cos], axis=0)
    sin = jnp.concatenate([jnp.zeros((n_ctx, rot_dim), f32), sin], axis=0)
    return jnp.tile(cos, (1, reps)), jnp.tile(sin, (1, reps))


def kernel(x, c, ctx, c_ctx, ada_w, ada_b, norm_mix, norm_ffn, ret_w_in, ret_decay_f, ret_decay_b, ret_w_o,
           mla_w_in, mla_q_a_norm, mla_w_q_b, mla_kv_a_norm, mla_w_kv_b, mla_q_norm, mla_k_norm, mla_w_o,
           router_w, router_bias, exp_w_gu, exp_w_down, sh_w_gu, sh_w_down):
    B, S, _ = x.shape
    n_ctx = ctx.shape[1]
    assert n_ctx % ROW_TILE == 0 and S % ROW_TILE == 0 and S % GRID_W == 0
    n_ctx_tiles = n_ctx // ROW_TILE

    rows = -(-(B + 1) // SUBLANES) * SUBLANES
    cc = jnp.zeros((rows, D), f32).at[:B].set(c).at[B].set(c_ctx)
    mod = _ada(cc, ada_w, ada_b)

    def modtab(i):
        ctx_row = jnp.broadcast_to(mod[i, B][None, :], (B, 6 * D))
        return jnp.stack([ctx_row, mod[i, :B]], axis=1)[:, :, None, :]

    mod0, mod1 = modtab(0), modtab(1)

    cos_r, sin_r = _rope_tables(S, n_ctx, RET_DK, 1)
    ride = exp_w_gu.shape[0] * exp_w_gu.shape[1] == _cast_steps(B, (n_ctx + S) // ROW_TILE) and D == 4 * EXP_FF
    proj = _ret_inproj(ctx, x, mod0, norm_mix[0][None, :], ret_w_in[0].astype(bf16), cos_r, sin_r, n_ctx_tiles,
                       (exp_w_gu, exp_w_down) if ride else ())
    q, k, v, gf, gb = proj[:5]
    if ride:
        exp_w = proj[5]
    else:
        exp_w = jnp.concatenate([exp_w_gu, exp_w_down[..., :2 * EXP_FF], exp_w_down[..., 2 * EXP_FF:]],
                                axis=2).astype(bf16)
    dtab = jnp.broadcast_to(jnp.concatenate([ret_decay_f[0], ret_decay_b[0]])[:, None], (2 * RET_H, LANES))
    o = _ret_scan(dtab, q, k, v, gf, gb, n_ctx)
    x1, fin, hlin, e_t, w_t, r_t, cnt = _post_mix(
        o, ret_w_o[0].astype(bf16), (ctx, x), 0, mod0, n_ctx_tiles, norm_ffn[0][None, :],
        router_w[0].T, router_bias[0][:, None])
    routed = _routed_experts(e_t, w_t, r_t, cnt, hlin, exp_w, 0)
    w_in1 = jnp.zeros((D, MLA_IN_PAD), f32).at[:, :mla_w_in.shape[2]].set(mla_w_in[0]).astype(bf16)

    wq = mla_w_q_b[0].reshape(MLA_QR, MLA_H, MLA_QK)
    wqn = wq[:, :, :MLA_NOPE].reshape(MLA_QR, MLA_H * MLA_NOPE).astype(bf16)
    wqr = wq[:, :, MLA_NOPE:].reshape(MLA_QR, MLA_H * MLA_ROPE).astype(bf16)
    wkv = mla_w_kv_b[0].reshape(MLA_KVR, MLA_H, MLA_NOPE + MLA_V)
    wk = wkv[:, :, :MLA_NOPE].reshape(MLA_KVR, MLA_H * MLA_NOPE).astype(bf16)
    wv = wkv[:, :, MLA_NOPE:].reshape(MLA_KVR, MLA_H * MLA_V).T.astype(bf16)
    qnn = mla_q_norm[0][None, :MLA_NOPE]
    qnr = jnp.tile(mla_q_norm[0][None, MLA_NOPE:], (1, MLA_H))
    knn = mla_k_norm[0][None, :MLA_NOPE]
    knr = jnp.concatenate([mla_k_norm[0][MLA_NOPE:], jnp.zeros((LANES - MLA_ROPE,), f32)])[None, :]
    cos_m, sin_m = _rope_tables(S, n_ctx, MLA_ROPE, LANES // MLA_ROPE)
    qkv_params = (mla_q_a_norm[0][None, :], wqn, wqr, mla_kv_a_norm[0][None, :], wk, wv, qnn, qnr, knn, knr)
    x2, qf, kf, vf = _ffn_mla(x1, routed, fin, sh_w_gu[0].astype(bf16), sh_w_down[0].astype(bf16), mod0, mod1,
                              norm_mix[1][None, :], w_in1, qkv_params, cos_m, sin_m, n_ctx_tiles)
    o1 = _mla_attn(qf, kf, vf)
    x3, fin1, hlin1, e1, w1, r1, cnt1 = _post_mix(
        o1, mla_w_o[0].astype(bf16), (x2,), n_ctx_tiles, mod1, 0, norm_ffn[1][None, :],
        router_w[1].T, router_bias[1][:, None], o_transposed=True)
    return _routed_experts(e1, w1, r1, cnt1, hlin1, exp_w, 1,
                           final=(x3, fin1, sh_w_gu[1].astype(bf16), sh_w_down[1].astype(bf16), mod1))
```

```python
import dataclasses
import functools

import jax
import jax.numpy as jnp
import numpy as np
from jax import lax
from jax.experimental import pallas as pl
from jax.experimental.pallas import tpu as pltpu
from jax.experimental.pallas import tpu_sc as plsc

f32 = jnp.float32
bf16 = jnp.bfloat16
i32 = jnp.int32

D = 1024
GRID_W = 64
EPS = 1e-6
ROPE_BASE = 10000.0
RET_H = 4
RET_DK = 256
RET_DV = 512
RET_VW = RET_H * RET_DV
RET_CHUNK = 256
MLA_H = 8
MLA_NOPE = 128
MLA_ROPE = 64
MLA_QK = MLA_NOPE + MLA_ROPE
MLA_V = 128
MLA_QR = 384
MLA_KVR = 256
MLA_IN_PAD = 768
N_EXP = 64
TOP_K = 8
N_GRP = 8
TOPK_GRP = 4
EXP_FF = 256
SH_FF = 256
ROUTED_SCALE = 2.5

LANES = 128
SUBLANES = 8
SC_LANES = 16
MXU_N = 256
ROW_TILE = 256
MOE_TILE = 256
MOE_GROUP = 2
ADA_COLS = 1536
PROJ_COLS = 512
VMEM_LIMIT = 56 * 1024 * 1024
MOE_VMEM_LIMIT = 62 * 1024 * 1024


def _cparams(sem, vmem=VMEM_LIMIT):
    return pltpu.CompilerParams(dimension_semantics=sem, vmem_limit_bytes=vmem)


def _sigmoid(x):
    return 1.0 / (1.0 + jnp.exp(-x))


def _silu(x):
    return x * _sigmoid(x)


def _rms(x, n=None):
    n = x.shape[-1] if n is None else n
    return x * lax.rsqrt(jnp.sum(x * x, axis=-1, keepdims=True) * (1.0 / n) + EPS)


def _dot(a, b):
    return jnp.dot(a, b, preferred_element_type=f32)


def _dot_nt(a, b):
    return lax.dot_general(a, b, (((1,), (1,)), ((), ())), preferred_element_type=f32)


def _ada_kernel(c_ref, w_ref, b_ref, o_ref):
    s = _silu(c_ref[...]).astype(bf16)
    o_ref[...] = _dot(s, w_ref[...].astype(bf16)) + b_ref[...]


def _ada(cc, ada_w, ada_b):
    depth = ada_w.shape[0]
    rows = cc.shape[0]
    tn = ADA_COLS
    return pl.pallas_call(
        _ada_kernel,
        grid=(depth, 6 * D // tn),
        in_specs=[pl.BlockSpec((rows, D), lambda i, j: (0, 0)),
                  pl.BlockSpec((None, D, tn), lambda i, j: (i, 0, j)),
                  pl.BlockSpec((None, 1, tn), lambda i, j: (i, 0, j))],
        out_specs=pl.BlockSpec((None, rows, tn), lambda i, j: (i, 0, j)),
        out_shape=jax.ShapeDtypeStruct((depth, rows, 6 * D), f32),
        compiler_params=_cparams(("arbitrary", "arbitrary")),
        name="ada",
    )(cc, ada_w, ada_b.reshape(depth, 1, 6 * D))


def _stream_tile(ctx_ref, x_ref, n_ctx_tiles):
    return jnp.where(pl.program_id(1) < n_ctx_tiles, ctx_ref[...], x_ref[...])


def _stream_specs(tm, n_ctx_tiles, last=None):
    clamp = (lambda i: i) if last is None else (lambda i: jnp.minimum(i, last))
    return [pl.BlockSpec((None, tm, D), lambda b, i: (b, jnp.minimum(clamp(i), n_ctx_tiles - 1), 0)),
            pl.BlockSpec((None, tm, D), lambda b, i: (b, jnp.maximum(clamp(i) - n_ctx_tiles, 0), 0))]


def _ret_inproj_kernel(ctx_ref, x_ref, mod_ref, g_ref, w_ref, cos_ref, sin_ref, *refs, n_ctx_tiles, n_cast):
    cast_in, cast_out = refs[:n_cast], refs[n_cast + 5:]
    q_ref, k_ref, v_ref, gf_ref, gb_ref = refs[n_cast:n_cast + 5]
    x = _stream_tile(ctx_ref, x_ref, n_ctx_tiles)
    mod = mod_ref[...]
    h = (_rms(x) * g_ref[...]) * (1.0 + mod[:, D:2 * D]) + mod[:, 0:D]
    hb = h.astype(bf16)
    cos = cos_ref[...]
    sin = sin_ref[...]

    def rope(a):
        outs = []
        for half in range(2):
            sl = slice(half * LANES, (half + 1) * LANES)
            ah = a[:, sl]
            outs.append(ah * cos[:, sl] + pltpu.roll(ah, LANES // 2, axis=1) * sin[:, sl])
        return jnp.concatenate(outs, axis=1)

    for hd in range(RET_H):
        sl = slice(hd * RET_DK, (hd + 1) * RET_DK)
        q_ref[:, sl] = rope(_dot(hb, w_ref[:, sl])).astype(bf16)
    if n_cast:
        (gu_ref, dn_ref), (wcat_ref,) = cast_in, cast_out
        wcat_ref[0:D, :] = gu_ref[...].astype(bf16)
        dn = dn_ref[...]
        wcat_ref[D:D + EXP_FF, :] = dn[:, :2 * EXP_FF].astype(bf16)
        wcat_ref[D + EXP_FF:D + 2 * EXP_FF, :] = dn[:, 2 * EXP_FF:].astype(bf16)
    for hd in range(RET_H):
        sl = slice(hd * RET_DK, (hd + 1) * RET_DK)
        wsl = slice(D + hd * RET_DK, D + (hd + 1) * RET_DK)
        k_ref[:, sl] = (rope(_dot(hb, w_ref[:, wsl])) * (RET_DK ** -0.5)).astype(bf16)
    cw = PROJ_COLS
    for c in range(RET_VW // cw):
        sl = slice(c * cw, (c + 1) * cw)
        v_ref[:, sl] = _dot(hb, w_ref[:, 2 * D + c * cw:2 * D + (c + 1) * cw]).astype(bf16)
        gf_ref[:, sl] = _silu(_dot(hb, w_ref[:, 2 * D + RET_VW + c * cw:2 * D + RET_VW + (c + 1) * cw])).astype(bf16)
        gb_ref[:, sl] = _silu(_dot(hb, w_ref[:, 2 * D + 2 * RET_VW + c * cw:2 * D + 2 * RET_VW + (c + 1) * cw])).astype(bf16)


def _cast_steps(batch, n_tiles):
    return batch * max(n_tiles - 1, 1)


def _packed_expert_shape(w_gu):
    n_l, n_e = w_gu.shape[:2]
    return n_l, n_e, D + 2 * EXP_FF, 2 * EXP_FF


def _ret_inproj(ctx, x, modtab, gain, w_in, cos_t, sin_t, n_ctx_tiles, riders=()):
    B = x.shape[0]
    T = ctx.shape[1] + x.shape[1]
    tm = ROW_TILE
    nt = T // tm
    n_in = w_in.shape[1]
    tok = lambda w: pl.BlockSpec((None, tm, w), lambda b, i: (b, i, 0))
    per_b = _cast_steps(B, nt) // B
    rider_out = []
    if riders:
        n_l, n_e = riders[0].shape[:2]
        assert n_l * n_e == per_b * B and D == 4 * EXP_FF
        rider_out = [jax.ShapeDtypeStruct(_packed_expert_shape(riders[0]), bf16)]

    def rider_spec(shape):
        def index(b, i):
            s = b * per_b + jnp.minimum(i, per_b - 1)
            return s // n_e, s % n_e, 0, 0
        return pl.BlockSpec((None, None) + tuple(shape[2:]), index)

    return pl.pallas_call(
        functools.partial(_ret_inproj_kernel, n_ctx_tiles=n_ctx_tiles, n_cast=len(riders)),
        grid=(B, nt),
        in_specs=_stream_specs(tm, n_ctx_tiles) + [
                  pl.BlockSpec((None, None, 1, 6 * D), lambda b, i: (b, jnp.where(i < n_ctx_tiles, 0, 1), 0, 0)),
                  pl.BlockSpec((1, D), lambda b, i: (0, 0)),
                  pl.BlockSpec((D, n_in), lambda b, i: (0, 0), pipeline_mode=pl.Buffered(1)),
                  pl.BlockSpec((tm, RET_DK), lambda b, i: (i, 0)),
                  pl.BlockSpec((tm, RET_DK), lambda b, i: (i, 0))] + [rider_spec(r.shape) for r in riders],
        out_specs=[tok(D), tok(D), tok(RET_VW), tok(RET_VW), tok(RET_VW)] + [rider_spec(r.shape) for r in rider_out],
        out_shape=[jax.ShapeDtypeStruct((B, T, D), bf16), jax.ShapeDtypeStruct((B, T, D), bf16),
                   jax.ShapeDtypeStruct((B, T, RET_VW), bf16), jax.ShapeDtypeStruct((B, T, RET_VW), bf16),
                   jax.ShapeDtypeStruct((B, T, RET_VW), bf16)] + rider_out,
        compiler_params=_cparams(("arbitrary", "arbitrary")),
        name="ret_inproj",
    )(ctx, x, modtab, gain, w_in, cos_t, sin_t, *riders)


def _ret_chunk_index(t, nc, ncc):
    u = t - nc
    back = jnp.where(u < ncc, ncc - 1 - u, nc - 1 - u + ncc)
    return jnp.where(t < nc, t, back)


def _ret_scan_kernel(dt_ref, q_ref, k_ref, v_ref, gf_ref, gb_ref, o_ref,
                     s_ref, of_ref, mask_ref, dq_ref, dk_ref, dc_ref, *, nc, ncc):
    t = pl.program_id(1)
    C = RET_CHUNK

    def init(direction):
        s_ref[...] = jnp.zeros_like(s_ref)
        ii = lax.broadcasted_iota(i32, (C, C), 0)
        jj = lax.broadcasted_iota(i32, (C, C), 1)
        rel = (ii - jj if direction == 0 else jj - ii).astype(f32)
        pos = lax.broadcasted_iota(i32, (C, 1), 0).astype(f32)
        for hd in range(RET_H):
            r = direction * RET_H + hd
            lg = -jnp.exp(dt_ref[r:r + 1, :])
            lg1 = lg[:, 0:1]
            mask_ref[hd] = jnp.where(rel >= 0, jnp.exp(lg1 * jnp.maximum(rel, 0.0)), 0.0)
            if direction == 0:
                dq_ref[hd] = jnp.exp(lg1 * (pos + 1.0))
                dk_ref[hd] = jnp.exp(lg1 * (C - 1.0 - pos))
            else:
                dq_ref[hd] = jnp.exp(lg1 * (C - pos))
                dk_ref[hd] = jnp.exp(lg1 * pos)
            dc_ref[hd] = jnp.exp(lg * float(C))

    pl.when(t == 0)(functools.partial(init, 0))
    pl.when(t == nc)(functools.partial(init, 1))

    row0 = pl.multiple_of(_ret_chunk_index(t, nc, ncc) * C, C)

    def step(forward):
        for hd in range(RET_H):
            ks = slice(hd * RET_DK, (hd + 1) * RET_DK)
            vs = slice(hd * RET_DV, (hd + 1) * RET_DV)
            qh = q_ref[:, ks]
            kh = k_ref[:, ks]
            vh = v_ref[:, vs]
            p = (_dot_nt(qh, kh) * mask_ref[hd]).astype(bf16)
            y = _dot(p, vh) + _dot(qh, s_ref[hd].astype(bf16)) * dq_ref[hd]
            kd = (kh.astype(f32) * dk_ref[hd]).astype(bf16)
            upd = lax.dot_general(kd, vh, (((0,), (0,)), ((), ())), preferred_element_type=f32)
            s_ref[hd] = s_ref[hd] * dc_ref[hd][0:1, 0:1] + upd
            yn = _rms(y)
            if forward:
                of_ref[pl.ds(row0, C), vs] = (gf_ref[:, vs].astype(f32) * yn).astype(bf16)
            else:
                o_ref[:, vs] = (of_ref[pl.ds(row0, C), vs].astype(f32) + gb_ref[:, vs].astype(f32) * yn).astype(bf16)

    pl.when(t < nc)(functools.partial(step, True))
    pl.when(t >= nc)(functools.partial(step, False))


def _ret_scan(dtab, q, k, v, gf, gb, n_ctx):
    B, T, _ = q.shape
    C = RET_CHUNK
    nc = T // C
    ncc = n_ctx // C
    cidx = functools.partial(_ret_chunk_index, nc=nc, ncc=ncc)
    first_back = ncc - 1
    return pl.pallas_call(
        functools.partial(_ret_scan_kernel, nc=nc, ncc=ncc),
        grid=(B, 2 * nc),
        in_specs=[pl.BlockSpec((2 * RET_H, LANES), lambda b, t: (0, 0)),
                  pl.BlockSpec((None, C, D), lambda b, t: (b, cidx(t), 0)),
                  pl.BlockSpec((None, C, D), lambda b, t: (b, cidx(t), 0)),
                  pl.BlockSpec((None, C, RET_VW), lambda b, t: (b, cidx(t), 0)),
                  pl.BlockSpec((None, C, RET_VW), lambda b, t: (b, jnp.where(t < nc, t, nc - 1), 0)),
                  pl.BlockSpec((None, C, RET_VW), lambda b, t: (b, jnp.where(t < nc, first_back, cidx(t)), 0))],
        out_specs=pl.BlockSpec((None, C, RET_VW), lambda b, t: (b, jnp.where(t < nc, first_back, cidx(t)), 0)),
        out_shape=jax.ShapeDtypeStruct((B, T, RET_VW), bf16),
        scratch_shapes=[pltpu.VMEM((RET_H, RET_DK, RET_DV), f32),
                        pltpu.VMEM((T, RET_VW), bf16),
                        pltpu.VMEM((RET_H, C, C), f32),
                        pltpu.VMEM((RET_H, C, 1), f32),
                        pltpu.VMEM((RET_H, C, 1), f32),
                        pltpu.VMEM((RET_H, 1, LANES), f32)],
        compiler_params=_cparams(("arbitrary", "arbitrary")),
        name="ret_scan",
    )(dtab, q, k, v, gf, gb)


def _route(f, rwt_ref, rb_ref, cnt_ref, e_ref, w_ref, r_ref):
    tm = f.shape[0]
    G = N_EXP // N_GRP
    logits = _dot_nt(rwt_ref[...].astype(bf16), f.astype(bf16))
    s = _sigmoid(logits)
    sel = s + rb_ref[...]
    mi = lax.broadcasted_iota(i32, (G, tm), 0)
    neg = -jnp.inf
    s_g = [s[g * G:(g + 1) * G, :] for g in range(N_GRP)]
    sel_g = [sel[g * G:(g + 1) * G, :] for g in range(N_GRP)]

    def first_max(a, ids, big):
        mx = jnp.max(a, axis=0, keepdims=True)
        ix = jnp.min(jnp.where(a == mx, ids, big), axis=0, keepdims=True)
        return mx, ix

    gscore = jnp.zeros((N_GRP, tm), f32)
    gi = lax.broadcasted_iota(i32, (N_GRP, tm), 0)
    for g in range(N_GRP):
        t1, i1 = first_max(sel_g[g], mi, G)
        t2 = jnp.max(jnp.where(mi == i1, neg, sel_g[g]), axis=0, keepdims=True)
        gscore = jnp.where(gi == g, t1 + t2, gscore)
    yield
    gmask = jnp.zeros((N_GRP, tm), i32)
    cur = gscore
    for _ in range(TOPK_GRP):
        _, ix = first_max(cur, gi, N_GRP)
        hit = gi == ix
        gmask = jnp.where(hit, 1, gmask)
        cur = jnp.where(hit, neg, cur)
    cand = [jnp.where(gmask[g:g + 1, :] > 0, sel_g[g], neg) for g in range(N_GRP)]
    ids = [mi + g * G for g in range(N_GRP)]

    def across(parts, op):
        acc = parts[0]
        for part in parts[1:]:
            acc = op(acc, part)
        return acc

    e_rows, w_rows = [], []
    for _ in range(TOP_K):
        mx = jnp.max(across(cand, jnp.maximum), axis=0, keepdims=True)
        ix = jnp.min(across([jnp.where(cand[g] == mx, ids[g], N_EXP) for g in range(N_GRP)], jnp.minimum),
                     axis=0, keepdims=True)
        hits = [ids[g] == ix for g in range(N_GRP)]
        cand = [jnp.where(hits[g], neg, cand[g]) for g in range(N_GRP)]
        wv = jnp.sum(across([jnp.where(hits[g], s_g[g], 0.0) for g in range(N_GRP)], jnp.add), axis=0, keepdims=True)
        e_rows.append(ix)
        w_rows.append(wv)
        yield
    wsum = w_rows[0]
    for r in range(1, TOP_K):
        wsum = wsum + w_rows[r]

    selm = [jnp.zeros((G, tm), f32) for _ in range(N_GRP)]
    for r in range(TOP_K):
        for g in range(N_GRP):
            selm[g] = jnp.where(ids[g] == e_rows[r], 1.0, selm[g])
    m_all = jnp.concatenate(selm, axis=0)
    ri = lax.broadcasted_iota(i32, (tm, tm), 0)
    ci = lax.broadcasted_iota(i32, (tm, tm), 1)
    upper = jnp.where(ri <= ci, 1.0, 0.0).astype(bf16)
    incl = _dot(m_all.astype(bf16), upper)
    carry = cnt_ref[:, 0:1]
    rank_all = carry + incl - m_all
    cnt_ref[...] = cnt_ref[...] + incl[:, tm - 1:tm]
    yield
    for r in range(TOP_K):
        rk = jnp.sum(across([jnp.where(ids[g] == e_rows[r], rank_all[g * G:(g + 1) * G, :], 0.0)
                             for g in range(N_GRP)], jnp.add), axis=0, keepdims=True)
        e_ref[r:r + 1, :] = e_rows[r]
        w_ref[r:r + 1, :] = w_rows[r] / wsum * ROUTED_SCALE
        r_ref[r:r + 1, :] = rk.astype(i32)


def _post_mix_kernel(o_ref, wo_ref, *refs, nt, o_transposed, split_ctx_tiles):
    n_resid = 2 if split_ctx_tiles else 1
    resid = refs[:n_resid]
    (mod_ref, g_ref, rwt_ref, rb_ref, x1_ref, fin_ref, hlin_ref, e_ref, w_ref, r_ref, cnt_ref,
     f0_ref, f1_ref) = refs[n_resid:]
    f_refs = (f0_ref, f1_ref)
    i = pl.program_id(1)

    @pl.when(i == 0)
    def _():
        cnt_ref[...] = jnp.zeros_like(cnt_ref)

    def route(p):
        return _route(f_refs[p][...], rwt_ref, rb_ref, cnt_ref, e_ref, w_ref, r_ref)

    def mix(p, other=()):
        other = iter(other)

        def advance(n):
            for _ in range(n):
                next(other, None)

        mod = mod_ref[...]
        x = _stream_tile(resid[0], resid[1], split_ctx_tiles) if split_ctx_tiles else resid[0][...]
        o = o_ref[...]
        cols = []
        for c in range(D // MXU_N):
            sl = slice(c * MXU_N, (c + 1) * MXU_N)
            if o_transposed:
                piece = lax.dot_general(o, wo_ref[:, sl], (((0,), (0,)), ((), ())), preferred_element_type=f32)
            else:
                piece = _dot(o, wo_ref[:, sl])
            cols.append(x[:, sl] + mod[:, 2 * D + c * MXU_N:2 * D + (c + 1) * MXU_N] * piece)
            advance(2)
        x1 = jnp.concatenate(cols, axis=1)
        x1_ref[...] = x1
        f = (_rms(x1) * g_ref[...]) * (1.0 + mod[:, 4 * D:5 * D]) + mod[:, 3 * D:4 * D]
        advance(2)
        fin_ref[...] = f.astype(bf16)
        tm = f.shape[0]
        for j in range(D // LANES):
            hlin_ref[pl.ds(j, tm, stride=D // LANES), :] = f[:, j * LANES:(j + 1) * LANES]
        f_refs[p][...] = f
        for _ in other:
            pass

    pl.when(i == 0)(functools.partial(mix, 0))
    for p in range(2):
        @pl.when(jnp.logical_and(jnp.logical_and(i >= 1, i < nt), i % 2 == p))
        def _(p=p):
            mix(p, route(1 - p))

    @pl.when(i == nt)
    def _():
        for _ in route((nt - 1) % 2):
            pass
        hlin_ref[...] = jnp.zeros_like(hlin_ref)


def _post_mix(o, w_o, resid, x_tile_off, modtab, n_ctx_tiles, gain, rwt, rb, o_transposed=False):
    if o_transposed:
        B, KO, N = o.shape
    else:
        B, N, KO = o.shape
    tm = ROW_TILE
    nt = N // tm
    last = lambda i: jnp.minimum(i, nt - 1)
    tok = lambda w: pl.BlockSpec((None, tm, w), lambda b, i: (b, last(i), 0))
    sel = lambda: pl.BlockSpec((None, TOP_K, tm), lambda b, i: (b, 0, jnp.maximum(i - 1, 0)))
    o_spec = (pl.BlockSpec((None, KO, tm), lambda b, i: (b, 0, last(i))) if o_transposed else tok(KO))
    split = len(resid) == 2
    resid_specs = (_stream_specs(tm, n_ctx_tiles, nt - 1) if split else
                   [pl.BlockSpec((None, tm, D), lambda b, i: (b, last(i) + x_tile_off, 0))])
    return pl.pallas_call(
        functools.partial(_post_mix_kernel, nt=nt, o_transposed=o_transposed,
                          split_ctx_tiles=n_ctx_tiles if split else 0),
        grid=(B, nt + 1),
        in_specs=[o_spec,
                  pl.BlockSpec((KO, D), lambda b, i: (0, 0))] + resid_specs + [
                  pl.BlockSpec((None, None, 1, 6 * D), lambda b, i: (b, jnp.where(last(i) < n_ctx_tiles, 0, 1), 0, 0)),
                  pl.BlockSpec((1, D), lambda b, i: (0, 0)),
                  pl.BlockSpec((N_EXP, D), lambda b, i: (0, 0)),
                  pl.BlockSpec((N_EXP, 1), lambda b, i: (0, 0))],
        out_specs=[tok(D), tok(D),
                   pl.BlockSpec((None, tm * (D // LANES), LANES), lambda b, i: (b, i, 0)),
                   sel(), sel(), sel(),
                   pl.BlockSpec((None, N_EXP, LANES), lambda b, i: (b, 0, 0))],
        out_shape=[jax.ShapeDtypeStruct((B, N, D), f32), jax.ShapeDtypeStruct((B, N, D), bf16),
                   jax.ShapeDtypeStruct((B, (N + tm) * (D // LANES), LANES), f32),
                   jax.ShapeDtypeStruct((B, TOP_K, N), i32), jax.ShapeDtypeStruct((B, TOP_K, N), f32),
                   jax.ShapeDtypeStruct((B, TOP_K, N), i32),
                   jax.ShapeDtypeStruct((B, N_EXP, LANES), f32)],
        scratch_shapes=[pltpu.VMEM((tm, D), f32), pltpu.VMEM((tm, D), f32)],
        compiler_params=_cparams(("arbitrary", "arbitrary")),
        name="post_mix",
    )(o, w_o, *resid, modtab, gain, rwt, rb)


META_W = 256
PLAN_ALIGN = 1024


def _round_up(n, m):
    return -(-n // m) * m


def _moe_sizes(n_tok):
    tm = MOE_TILE
    nt_max = (n_tok * TOP_K + N_EXP * (tm - 1)) // tm + 1
    ntp = _round_up(nt_max + MOE_GROUP, int(np.lcm(PLAN_ALIGN // tm, MOE_GROUP)))
    assert ntp <= META_W
    return ntp, _round_up(n_tok, PLAN_ALIGN)


def _plan_kernel(e_ref, r_ref, cnt_ref, pos_ref, meta_ref, *, n_tok):
    tm = MOE_TILE
    ntile = jnp.floor((cnt_ref[...] + (tm - 1.0)) * (1.0 / tm))
    ntb = ntile.astype(bf16)
    ei = lax.broadcasted_iota(i32, (N_EXP, LANES), 0)
    ej = lax.broadcasted_iota(i32, (N_EXP, LANES), 1)
    lower = jnp.where(ej <= ei, 1.0, 0.0)[:, :N_EXP].astype(bf16)
    tend = _dot(lower, ntb)
    tstart = tend - ntile
    tt = lax.broadcasted_iota(i32, (N_EXP, META_W), 1).astype(f32)
    te = jnp.sum(jnp.where(tt >= tend[:, 0:1], 1.0, 0.0), axis=0, keepdims=True)
    meta_ref[...] = jnp.zeros_like(meta_ref)
    meta_ref[0:1, :] = jnp.minimum(te, N_EXP - 1.0).astype(i32)
    meta_ref[1:2, :] = jnp.broadcast_to(tend[N_EXP - 1:N_EXP, 0:1], (1, META_W)).astype(i32)
    e = e_ref[...]
    base = jnp.zeros(e.shape, f32)
    for ex in range(N_EXP):
        base = jnp.where(e == ex, tstart[ex:ex + 1, 0:1] * float(tm), base)
    pos_ref[...] = jnp.zeros_like(pos_ref)
    pos_ref[:, 0:n_tok] = base.astype(i32) + r_ref[...]


def _plan(e_t, r_t, cnt):
    B, K, N = e_t.shape
    _, npad = _moe_sizes(N)
    return pl.pallas_call(
        functools.partial(_plan_kernel, n_tok=N),
        grid=(B,),
        in_specs=[pl.BlockSpec((None, K, N), lambda b: (b, 0, 0)),
                  pl.BlockSpec((None, K, N), lambda b: (b, 0, 0)),
                  pl.BlockSpec((None, N_EXP, LANES), lambda b: (b, 0, 0))],
        out_specs=[pl.BlockSpec((None, K, npad), lambda b: (b, 0, 0)),
                   pl.BlockSpec((None, SUBLANES, META_W), lambda b: (b, 0, 0))],
        out_shape=[jax.ShapeDtypeStruct((B, K, npad), i32), jax.ShapeDtypeStruct((B, SUBLANES, META_W), i32)],
        compiler_params=_cparams(("arbitrary",)),
        name="moe_plan",
    )(e_t, r_t, cnt)


def _plan_invert(pos, w_t, n_tok):
    B, K, npad = pos.shape
    ntp, _ = _moe_sizes(n_tok)
    plen = ntp * MOE_TILE
    nch = D // LANES
    mesh = plsc.VectorSubcoreMesh(core_axis_name="c", subcore_axis_name="s")
    n_cores = mesh.num_cores
    assert 2 * B <= n_cores * mesh.num_subcores and n_tok % SC_LANES == 0 and plen % SC_LANES == 0

    @functools.partial(
        pl.kernel, mesh=mesh,
        out_type=[jax.ShapeDtypeStruct((B * plen,), i32), jax.ShapeDtypeStruct((B * plen,), f32)],
        scratch_types=[pltpu.VMEM((npad,), i32), pltpu.VMEM((n_tok,), f32),
                       pltpu.VMEM((plen,), i32), pltpu.VMEM((plen,), f32)],
        compiler_params=dataclasses.replace(pltpu.CompilerParams(), needs_layout_passes=False))
    def invert(pos_hbm, w_hbm, rows_hbm, ws_hbm, pos_c, w_c, rows_v, ws_v):
        wid = lax.axis_index("s") * n_cores + lax.axis_index("c")
        b = wid % B
        lane = lax.iota(i32, SC_LANES)

        def load_pos(k):
            pltpu.sync_copy(pos_hbm.at[pl.ds(pl.multiple_of((b * K + k) * npad, SUBLANES), npad)], pos_c)

        @pl.when(wid < B)
        def _():
            pad = jnp.full((SC_LANES,), n_tok * nch, i32)

            @pl.loop(0, plen, step=SC_LANES)
            def _(i):
                rows_v[pl.ds(i, SC_LANES)] = pad

            for k in range(K):
                load_pos(k)

                @pl.loop(0, n_tok, step=SC_LANES)
                def _(n):
                    plsc.store_scatter(rows_v, [pos_c[pl.ds(n, SC_LANES)]], (lane + n) * nch)

            pltpu.sync_copy(rows_v, rows_hbm.at[pl.ds(pl.multiple_of(b * plen, SUBLANES), plen)])

        @pl.when(jnp.logical_and(wid >= B, wid < 2 * B))
        def _():
            zero = jnp.zeros((SC_LANES,), f32)

            @pl.loop(0, plen, step=SC_LANES)
            def _(i):
                ws_v[pl.ds(i, SC_LANES)] = zero

            for k in range(K):
                load_pos(k)
                pltpu.sync_copy(w_hbm.at[pl.ds(pl.multiple_of((b * K + k) * n_tok, SUBLANES), n_tok)], w_c)

                @pl.loop(0, n_tok, step=SC_LANES)
                def _(n):
                    plsc.store_scatter(ws_v, [pos_c[pl.ds(n, SC_LANES)]], w_c[pl.ds(n, SC_LANES)])

            pltpu.sync_copy(ws_v, ws_hbm.at[pl.ds(pl.multiple_of(b * plen, SUBLANES), plen)])

    return invert(pos.reshape(-1), w_t.reshape(-1))


def _moe_kernel(te_ref, nt_ref, *refs, ntp, n_tok, n_final):
    R = MOE_GROUP
    rg_ref, rs_ref, ws_ref, hlin_hbm = refs[0:4]
    w_refs = refs[4:4 + R]
    final_refs = refs[4 + R:4 + R + n_final]
    out_ref, acc_ref, hlin_ref, hsem = refs[4 + R + n_final:8 + R + n_final]
    bufs = refs[8 + R + n_final:]
    xs_refs = [bufs[0:R], bufs[R:2 * R]]
    ylin_refs = [bufs[2 * R:3 * R], bufs[3 * R:4 * R]]
    b = pl.program_id(0)
    t = pl.program_id(1)
    TM = MOE_TILE
    NCH = D // LANES
    U = 8
    TMP = TM + SUBLANES
    ngrp = ntp // R

    def fetch(bb):
        return pltpu.make_async_copy(hlin_hbm.at[bb], hlin_ref, hsem.at[0])

    @pl.when(jnp.logical_and(t == ngrp + 2, b + 1 < pl.num_programs(0)))
    def _():
        fetch(b + 1).start()

    @pl.when(t == 0)
    def _():
        acc_ref[...] = jnp.zeros_like(acc_ref)

        @pl.when(b == 0)
        def _():
            fetch(0).start()

        fetch(b).wait()

    @pl.when(jnp.logical_and(b == 0, t == 0))
    def _():
        for buf in bufs:
            buf[...] = jnp.zeros_like(buf)

    def gather_rows(p, r, c):
        for m in range(c * U, (c + 1) * U):
            off = pl.multiple_of(rg_ref[r * TM + m], NCH)
            xs_refs[p][r][pl.ds(m, NCH, stride=TMP), :] = hlin_ref[pl.ds(off, NCH), :]

    def scatter_rows(p, r, c):
        offs = [pl.multiple_of(rs_ref[r * TM + c * U + u], NCH) for u in range(U)]
        news = [acc_ref[pl.ds(offs[u], NCH), :] + ylin_refs[p][r][pl.ds((c * U + u) * NCH, NCH), :]
                for u in range(U)]
        for u in range(U):
            acc_ref[pl.ds(offs[u], NCH), :] = news[u]

    def stage(p):
        row_work = [functools.partial(fn, p, r, c) for c in range(TM // U) for r in range(R)
                    for fn in (gather_rows, scatter_rows)]
        n_pieces = R * (EXP_FF // LANES + D // MXU_N)
        per_piece = -(-len(row_work) // n_pieces)

        def deal():
            for fn in row_work[:per_piece]:
                fn()
            del row_work[:per_piece]

        for r in range(R):
            x = jnp.concatenate([xs_refs[1 - p][r][pl.ds(j * TMP, TM), :] for j in range(NCH)], axis=1).astype(bf16)
            wcol = jnp.broadcast_to(ws_ref[r:r + 1, :], (SUBLANES, TM)).T[:, 0:1]
            gate = _dot(x, w_refs[r][0:D, :EXP_FF])
            deal()
            up = _dot(x, w_refs[r][0:D, EXP_FF:])
            deal()
            a = (_silu(gate) * up * wcol).astype(bf16)
            per_half = 2 * EXP_FF // MXU_N
            for c in range(D // MXU_N):
                row0 = D + (c // per_half) * EXP_FF
                col0 = (c % per_half) * MXU_N
                y = _dot(a, w_refs[r][row0:row0 + EXP_FF, col0:col0 + MXU_N])
                for jj in range(MXU_N // LANES):
                    j = c * (MXU_N // LANES) + jj
                    ylin_refs[1 - p][r][pl.ds(j, TM, stride=NCH), :] = y[:, jj * LANES:(jj + 1) * LANES]
                deal()
        while row_work:
            deal()

    live = (t - 2) * R < nt_ref[b]
    pl.when(jnp.logical_and(live, t % 2 == 0))(functools.partial(stage, 0))
    pl.when(jnp.logical_and(live, t % 2 == 1))(functools.partial(stage, 1))

    @pl.when(t >= ngrp + 2)
    def _():
        row0 = (t - (ngrp + 2)) * (ROW_TILE * NCH)
        routed = [acc_ref[pl.ds(row0 + j, ROW_TILE, stride=NCH), :] for j in range(NCH)]
        if n_final:
            x_ref, fin_ref, shgu_ref, shd_ref, mod_ref = final_refs
            gate = mod_ref[...][:, 5 * D:6 * D]
            out_ref[...] = x_ref[...] + gate * (jnp.concatenate(routed, axis=1)
                                                + _shared_ffn(fin_ref[...], shgu_ref, shd_ref))
        else:
            for j in range(NCH):
                out_ref[:, j * LANES:(j + 1) * LANES] = routed[j].astype(bf16)


def _moe(te, nt, rows, wsort, hlin, w, layer, final=()):
    B = hlin.shape[0]
    NCH = D // LANES
    n_tok = hlin.shape[1] // NCH - ROW_TILE
    assert n_tok % ROW_TILE == 0
    nf = n_tok // ROW_TILE
    TM = MOE_TILE
    R = MOE_GROUP
    ntp, _ = _moe_sizes(n_tok)
    assert ntp % R == 0
    ngrp = ntp // R

    def group_of(b, t, nt_ref, lag):
        grp = t - lag
        ok = jnp.logical_and(t >= lag, grp * R < nt_ref[b])
        return b * ngrp + jnp.where(ok, grp, ngrp - 1)

    def rows_spec(lag):
        return pl.BlockSpec((R * TM,), lambda b, t, te_ref, nt_ref: (group_of(b, t, nt_ref, lag),),
                            memory_space=pltpu.SMEM)

    ws_spec = pl.BlockSpec((None, R, TM), lambda b, t, te_ref, nt_ref: (group_of(b, t, nt_ref, 1), 0, 0))

    def w_spec(r):
        def index(b, t, te_ref, nt_ref):
            tile = jnp.clip((t - 1) * R + r, 0, nt_ref[b] - 1)
            return (layer, te_ref[b * META_W + tile], 0, 0)
        return pl.BlockSpec((None, None) + tuple(w.shape[2:]), index)

    flush_tile = lambda t: jnp.maximum(t - (ngrp + 2), 0)
    flush_tok = lambda: pl.BlockSpec((None, ROW_TILE, D), lambda b, t, *_: (b, flush_tile(t), 0))
    final_specs = []
    if final:
        const = lambda a: pl.BlockSpec(a.shape, lambda b, t, *_: (0, 0), pipeline_mode=pl.Buffered(1))
        final_specs = [flush_tok(), flush_tok(), const(final[2]), const(final[3]),
                       pl.BlockSpec((None, None, 1, 6 * D), lambda b, t, *_: (b, 1, 0, 0))]

    grid_spec = pltpu.PrefetchScalarGridSpec(
        num_scalar_prefetch=2,
        grid=(B, ngrp + 2 + nf),
        in_specs=([rows_spec(0), rows_spec(2), ws_spec,
                   pl.BlockSpec(memory_space=pl.ANY)]
                  + [w_spec(r) for r in range(R)] + final_specs),
        out_specs=flush_tok(),
        scratch_shapes=([pltpu.VMEM(((n_tok + SUBLANES) * NCH, LANES), f32),
                         pltpu.VMEM(((n_tok + ROW_TILE) * NCH, LANES), f32),
                         pltpu.SemaphoreType.DMA((1,))]
                        + [pltpu.VMEM(((TM + SUBLANES) * NCH, LANES), f32) for _ in range(2 * R)]
                        + [pltpu.VMEM((TM * NCH, LANES), f32) for _ in range(2 * R)]),
    )
    return pl.pallas_call(
        functools.partial(_moe_kernel, ntp=ntp, n_tok=n_tok, n_final=len(final)),
        grid_spec=grid_spec,
        out_shape=jax.ShapeDtypeStruct((B, n_tok, D), f32 if final else bf16),
        compiler_params=_cparams(("arbitrary", "arbitrary"), MOE_VMEM_LIMIT),
        name="moe",
    )(te, nt, rows, rows, wsort.reshape(B * ngrp, R, TM), hlin, *([w] * R), *final)


def _routed_experts(e_t, w_t, r_t, cnt, hlin, w, layer, final=()):
    N = e_t.shape[2]
    pos, meta = _plan(e_t, r_t, cnt)
    te = meta[:, 0, :].reshape(-1)
    nt = meta[:, 1, 0]
    rows, wsort = _plan_invert(pos, w_t, N)
    return _moe(te, nt, rows, wsort, hlin, w, layer, final)


def _shared_ffn(fin, shgu_ref, shd_ref):
    gu = _dot(fin, shgu_ref[...])
    return _dot((_silu(gu[:, :SH_FF]) * gu[:, SH_FF:]).astype(bf16), shd_ref[...])


def _mla_qkv(a, qan_ref, wqn_ref, wqr_ref, kvan_ref, wk_ref, wv_ref, qnn_ref, qnr_ref, knn_ref, knr_ref,
             cos_ref, sin_ref, q_ref, k_ref, v_ref):
    tm = a.shape[0]
    scale = MLA_QK ** -0.5 * float(np.log2(np.e))
    cos = cos_ref[...]
    sin = sin_ref[...]
    lane = lax.broadcasted_iota(i32, (tm, LANES), 1)
    first = (lane // (MLA_ROPE // 4)) % 2 == 0

    def rope(xb):
        sw = jnp.where(first, pltpu.roll(xb, LANES - MLA_ROPE // 4, axis=1), pltpu.roll(xb, MLA_ROPE // 4, axis=1))
        return xb * cos + sw * sin

    qa = (_rms(a[:, :MLA_QR]) * qan_ref[...]).astype(bf16)
    qn = _dot(qa, wqn_ref[...])
    yield
    qr = _dot(qa, wqr_ref[...])
    yield
    ri = lax.broadcasted_iota(i32, (LANES, LANES), 0) // MLA_ROPE
    ci = lax.broadcasted_iota(i32, (LANES, LANES), 1) // MLA_ROPE
    seg = jnp.where(ri == ci, 1.0, 0.0).astype(bf16)

    def seg_sum(sq):
        hi = sq.astype(bf16)
        r1 = sq - hi.astype(f32)
        mid = r1.astype(bf16)
        lo = (r1 - mid.astype(f32)).astype(bf16)
        return _dot(hi, seg) + _dot(mid, seg) + _dot(lo, seg)

    qr_blocks = []
    for p in range(MLA_H // 2):
        blk = qr[:, p * LANES:(p + 1) * LANES]
        blk = blk * lax.rsqrt(seg_sum(blk * blk) * (1.0 / MLA_ROPE) + EPS) * qnr_ref[:, p * LANES:(p + 1) * LANES]
        qr_blocks.append(rope(blk) * scale)
        yield

    kv = (_rms(a[:, MLA_QR:MLA_QR + MLA_KVR]) * kvan_ref[...]).astype(bf16)
    kn = _dot(kv, wk_ref[...])
    yield
    v_ref[...] = _dot_nt(wv_ref[...], kv).astype(bf16)
    yield
    kr = a[:, MLA_QR + MLA_KVR:MLA_IN_PAD]
    kr = rope(_rms(kr, MLA_ROPE) * knr_ref[...])
    kr_odd = pltpu.roll(kr, MLA_ROPE, axis=1)
    for hd in range(MLA_H):
        sl = slice(hd * MLA_NOPE, (hd + 1) * MLA_NOPE)
        q_ref[:, 2 * hd * LANES:(2 * hd + 1) * LANES] = (_rms(qn[:, sl]) * qnn_ref[...] * scale).astype(bf16)
        q_ref[:, (2 * hd + 1) * LANES:(2 * hd + 2) * LANES] = qr_blocks[hd // 2].astype(bf16)
        k_ref[:, 2 * hd * LANES:(2 * hd + 1) * LANES] = (_rms(kn[:, sl]) * knn_ref[...]).astype(bf16)
        k_ref[:, (2 * hd + 1) * LANES:(2 * hd + 2) * LANES] = (kr if hd % 2 == 0 else kr_odd).astype(bf16)
        if hd % 2 == 1:
            yield


def _ffn_mla_kernel(x1_ref, routed_ref, fin_ref, shgu_ref, shd_ref, mod0_ref, mod1_ref, g_ref, win_ref, *refs, nt):
    qkv_refs, (x2_ref, q_ref, k_ref, v_ref, a0_ref, a1_ref) = refs[:12], refs[12:]
    a_refs = (a0_ref, a1_ref)
    i = pl.program_id(1)

    def qkv(p):
        return _mla_qkv(a_refs[p][...], *qkv_refs, q_ref, k_ref, v_ref)

    def combine(p, other=()):
        other = iter(other)

        def advance(n):
            for _ in range(n):
                next(other, None)

        gu = _dot(fin_ref[...], shgu_ref[...])
        advance(3)
        shared = _dot((_silu(gu[:, :SH_FF]) * gu[:, SH_FF:]).astype(bf16), shd_ref[...])
        advance(3)
        x2 = x1_ref[...] + mod0_ref[...][:, 5 * D:6 * D] * (routed_ref[...] + shared)
        x2_ref[...] = x2
        advance(2)
        mod1 = mod1_ref[...]
        h = (_rms(x2) * g_ref[...]) * (1.0 + mod1[:, D:2 * D]) + mod1[:, 0:D]
        advance(2)
        a_refs[p][...] = _dot(h.astype(bf16), win_ref[...])
        for _ in other:
            pass

    pl.when(i == 0)(functools.partial(combine, 0))
    for p in range(2):
        @pl.when(jnp.logical_and(jnp.logical_and(i >= 1, i < nt), i % 2 == p))
        def _(p=p):
            combine(p, qkv(1 - p))

    @pl.when(i == nt)
    def _():
        for _ in qkv((nt - 1) % 2):
            pass


def _ffn_mla(x1, routed, fin, sh_gu, sh_d, modtab0, modtab1, gain, w_in, qkv_params, cos_t, sin_t, n_ctx_tiles):
    B, T, _ = x1.shape
    tm = ROW_TILE
    nt = T // tm
    last = lambda i: jnp.minimum(i, nt - 1)
    prev = lambda i: jnp.maximum(i - 1, 0)
    tok = lambda w: pl.BlockSpec((None, tm, w), lambda b, i: (b, last(i), 0))
    modspec = lambda: pl.BlockSpec((None, None, 1, 6 * D),
                                   lambda b, i: (b, jnp.where(last(i) < n_ctx_tiles, 0, 1), 0, 0))
    full = lambda r, c: pl.BlockSpec((r, c), lambda b, i: (0, 0))
    rope_spec = lambda: pl.BlockSpec((tm, LANES), lambda b, i: (prev(i), 0))
    hw = 2 * LANES * MLA_H
    return pl.pallas_call(
        functools.partial(_ffn_mla_kernel, nt=nt),
        grid=(B, nt + 1),
        in_specs=[tok(D), tok(D), tok(D), full(D, 2 * SH_FF), full(SH_FF, D), modspec(), modspec(),
                  full(1, D), full(D, MLA_IN_PAD),
                  full(1, MLA_QR), full(MLA_QR, MLA_H * MLA_NOPE), full(MLA_QR, MLA_H * MLA_ROPE),
                  full(1, MLA_KVR), full(MLA_KVR, MLA_H * MLA_NOPE), full(MLA_H * MLA_V, MLA_KVR),
                  full(1, MLA_NOPE), full(1, MLA_H * MLA_ROPE), full(1, MLA_NOPE), full(1, LANES),
                  rope_spec(), rope_spec()],
        out_specs=[tok(D),
                   pl.BlockSpec((None, tm, hw), lambda b, i: (b, jnp.maximum(prev(i) - n_ctx_tiles, 0), 0)),
                   pl.BlockSpec((None, tm, hw), lambda b, i: (b, prev(i), 0)),
                   pl.BlockSpec((None, MLA_H * MLA_V, tm), lambda b, i: (b, 0, prev(i)))],
        out_shape=[jax.ShapeDtypeStruct((B, T, D), f32),
                   jax.ShapeDtypeStruct((B, T - n_ctx_tiles * tm, hw), bf16), jax.ShapeDtypeStruct((B, T, hw), bf16),
                   jax.ShapeDtypeStruct((B, MLA_H * MLA_V, T), bf16)],
        scratch_shapes=[pltpu.VMEM((tm, MLA_IN_PAD), f32), pltpu.VMEM((tm, MLA_IN_PAD), f32)],
        compiler_params=_cparams(("arbitrary", "arbitrary")),
        name="ffn_mla",
    )(x1, routed, fin, sh_gu, sh_d, modtab0, modtab1, gain, w_in, *qkv_params, cos_t, sin_t)


ATT_TQ = 256


def _mla_attn_kernel(q_ref, k_ref, vt_ref, o_ref, s0_ref, s1_ref, m0_ref, m1_ref):
    tq = q_ref.shape[0]
    groups = k_ref.shape[0] // SUBLANES
    i = pl.program_id(0)
    s_refs, m_refs = (s0_ref, s1_ref), (m0_ref, m1_ref)

    @pl.when(i == 0)
    def _():
        for ref in s_refs + m_refs:
            ref[...] = jnp.zeros_like(ref)

    def stage(par):
        s = _dot_nt(k_ref[...], q_ref[...])
        s_refs[par][...] = s
        m_refs[par][...] = jnp.max(s.reshape(groups, SUBLANES, tq), axis=0)
        m = jnp.max(m_refs[1 - par][...], axis=0, keepdims=True)
        p = jnp.exp2(s_refs[1 - par][...] - m)
        lsum = jnp.sum(jnp.sum(p.reshape(groups, SUBLANES, tq), axis=0), axis=0, keepdims=True)
        acc = _dot(vt_ref[...], p.astype(bf16))
        o_ref[...] = (acc / lsum).astype(bf16)

    pl.when(i % 2 == 0)(functools.partial(stage, 0))
    pl.when(i % 2 == 1)(functools.partial(stage, 1))


def _mla_attn(q, k, vt):
    B, S, _ = q.shape
    T = k.shape[1]
    tq = ATT_TQ
    nq = S // tq
    ntile = B * MLA_H * nq
    assert S % tq == 0 and T % SUBLANES == 0

    def tile(g):
        g = jnp.clip(g, 0, ntile - 1)
        return g // (MLA_H * nq), (g // nq) % MLA_H, g % nq

    def q_index(g):
        b, h, i = tile(g)
        return b, i, h

    def k_index(g):
        b, h, _ = tile(g)
        return b, 0, h

    def vt_index(g):
        b, h, _ = tile(g - 1)
        return b, h, 0

    def o_index(g):
        b, h, i = tile(g - 1)
        return b, h, i

    return pl.pallas_call(
        _mla_attn_kernel,
        grid=(ntile + 1,),
        in_specs=[pl.BlockSpec((None, tq, 2 * LANES), q_index),
                  pl.BlockSpec((None, T, 2 * LANES), k_index),
                  pl.BlockSpec((None, MLA_V, T), vt_index)],
        out_specs=pl.BlockSpec((None, MLA_V, tq), o_index),
        out_shape=jax.ShapeDtypeStruct((B, MLA_H * MLA_V, S), bf16),
        scratch_shapes=[pltpu.VMEM((T, tq), f32), pltpu.VMEM((T, tq), f32),
                        pltpu.VMEM((SUBLANES, tq), f32), pltpu.VMEM((SUBLANES, tq), f32)],
        compiler_params=_cparams(("arbitrary",)),
        name="mla_attn",
    )(q, k, vt)


def _axial_angles(rows_n, rot_dim):
    axis_dim = rot_dim // 2
    inv = ROPE_BASE ** (-jnp.arange(0, axis_dim, 2, dtype=f32) / axis_dim)
    row = jnp.repeat(jnp.arange(rows_n, dtype=f32), GRID_W)
    col = jnp.tile(jnp.arange(GRID_W, dtype=f32), rows_n)
    return row[:, None] * inv, col[:, None] * inv


def _rope_tables(seq, n_ctx, rot_dim, reps):
    ang_r, ang_c = _axial_angles(seq // GRID_W, rot_dim)
    cos = jnp.concatenate([jnp.cos(ang_r)] * 2 + [jnp.cos(ang_c)] * 2, axis=1)
    sin = jnp.concatenate([-jnp.sin(ang_r), jnp.sin(ang_r), -jnp.sin(ang_c), jnp.sin(ang_c)], axis=1)
    cos = jnp.concatenate([jnp.ones((n_ctx, rot_dim), f32), cos], axis=0)
    sin = jnp.concatenate([jnp.zeros((n_ctx, rot_dim), f32), sin], axis=0)
    return jnp.tile(cos, (1, reps)), jnp.tile(sin, (1, reps))


def kernel(x, c, ctx, c_ctx, ada_w, ada_b, norm_mix, norm_ffn, ret_w_in, ret_decay_f, ret_decay_b, ret_w_o,
           mla_w_in, mla_q_a_norm, mla_w_q_b, mla_kv_a_norm, mla_w_kv_b, mla_q_norm, mla_k_norm, mla_w_o,
           router_w, router_bias, exp_w_gu, exp_w_down, sh_w_gu, sh_w_down):
    B, S, _ = x.shape
    n_ctx = ctx.shape[1]
    assert n_ctx % ROW_TILE == 0 and S % ROW_TILE == 0 and S % GRID_W == 0
    n_ctx_tiles = n_ctx // ROW_TILE

    rows = -(-(B + 1) // SUBLANES) * SUBLANES
    cc = jnp.zeros((rows, D), f32).at[:B].set(c).at[B].set(c_ctx)
    mod = _ada(cc, ada_w, ada_b)

    def modtab(i):
        ctx_row = jnp.broadcast_to(mod[i, B][None, :], (B, 6 * D))
        return jnp.stack([ctx_row, mod[i, :B]], axis=1)[:, :, None, :]

    mod0, mod1 = modtab(0), modtab(1)

    cos_r, sin_r = _rope_tables(S, n_ctx, RET_DK, 1)
    ride = exp_w_gu.shape[0] * exp_w_gu.shape[1] == _cast_steps(B, (n_ctx + S) // ROW_TILE) and D == 4 * EXP_FF
    proj = _ret_inproj(ctx, x, mod0, norm_mix[0][None, :], ret_w_in[0].astype(bf16), cos_r, sin_r, n_ctx_tiles,
                       (exp_w_gu, exp_w_down) if ride else ())
    q, k, v, gf, gb = proj[:5]
    if ride:
        exp_w = proj[5]
    else:
        exp_w = jnp.concatenate([exp_w_gu, exp_w_down[..., :2 * EXP_FF], exp_w_down[..., 2 * EXP_FF:]],
                                axis=2).astype(bf16)
    dtab = jnp.broadcast_to(jnp.concatenate([ret_decay_f[0], ret_decay_b[0]])[:, None], (2 * RET_H, LANES))
    o = _ret_scan(dtab, q, k, v, gf, gb, n_ctx)
    x1, fin, hlin, e_t, w_t, r_t, cnt = _post_mix(
        o, ret_w_o[0].astype(bf16), (ctx, x), 0, mod0, n_ctx_tiles, norm_ffn[0][None, :],
        router_w[0].T, router_bias[0][:, None])
    routed = _routed_experts(e_t, w_t, r_t, cnt, hlin, exp_w, 0)
    w_in1 = jnp.zeros((D, MLA_IN_PAD), f32).at[:, :mla_w_in.shape[2]].set(mla_w_in[0]).astype(bf16)

    wq = mla_w_q_b[0].reshape(MLA_QR, MLA_H, MLA_QK)
    wqn = wq[:, :, :MLA_NOPE].reshape(MLA_QR, MLA_H * MLA_NOPE).astype(bf16)
    wqr = wq[:, :, MLA_NOPE:].reshape(MLA_QR, MLA_H * MLA_ROPE).astype(bf16)
    wkv = mla_w_kv_b[0].reshape(MLA_KVR, MLA_H, MLA_NOPE + MLA_V)
    wk = wkv[:, :, :MLA_NOPE].reshape(MLA_KVR, MLA_H * MLA_NOPE).astype(bf16)
    wv = wkv[:, :, MLA_NOPE:].reshape(MLA_KVR, MLA_H * MLA_V).T.astype(bf16)
    qnn = mla_q_norm[0][None, :MLA_NOPE]
    qnr = jnp.tile(mla_q_norm[0][None, MLA_NOPE:], (1, MLA_H))
    knn = mla_k_norm[0][None, :MLA_NOPE]
    knr = jnp.concatenate([mla_k_norm[0][MLA_NOPE:], jnp.zeros((LANES - MLA_ROPE,), f32)])[None, :]
    cos_m, sin_m = _rope_tables(S, n_ctx, MLA_ROPE, LANES // MLA_ROPE)
    qkv_params = (mla_q_a_norm[0][None, :], wqn, wqr, mla_kv_a_norm[0][None, :], wk, wv, qnn, qnr, knn, knr)
    x2, qf, kf, vf = _ffn_mla(x1, routed, fin, sh_w_gu[0].astype(bf16), sh_w_down[0].astype(bf16), mod0, mod1,
                              norm_mix[1][None, :], w_in1, qkv_params, cos_m, sin_m, n_ctx_tiles)
    o1 = _mla_attn(qf, kf, vf)
    x3, fin1, hlin1, e1, w1, r1, cnt1 = _post_mix(
        o1, mla_w_o[0].astype(bf16), (x2,), n_ctx_tiles, mod1, 0, norm_ffn[1][None, :],
        router_w[1].T, router_bias[1][:, None], o_transposed=True)
    return _routed_experts(e1, w1, r1, cnt1, hlin1, exp_w, 1,
                           final=(x3, fin1, sh_w_gu[1].astype(bf16), sh_w_down[1].astype(bf16), mod1))
```
